```python
import jax, jax.numpy as jnp
from jax import lax
import numpy as np

D_MODEL = 1024
BATCH = 32
SEQ = 2048
DEPTH = 1

GRID_W = 64
CTX_LEN = 256
D_FF = 2816
D_CONV = 1024
CONV_WIDTH = 31
GLA_HEADS = 4
GLA_DK = 128
GLA_DV = 256
GLA_LOWRANK = 16
GLA_TAU = 16.0
GLA_CHUNK = 64
N_MOD = 9
EPS = 1e-6
QK_W = GLA_HEADS * GLA_DK
V_W = GLA_HEADS * GLA_DV
IN_SPLITS = (2 * D_CONV, QK_W, QK_W, V_W, V_W, GLA_LOWRANK, GLA_LOWRANK, D_MODEL, D_MODEL)
D_IN = 2 * D_CONV + 2 * QK_W + 2 * V_W + 2 * GLA_LOWRANK + 2 * D_MODEL

kernel_name = 'hybrid_conv_gla_macaron_dit_layer'


def rmsnorm(h, g):
    hf = h.astype(jnp.float32)
    y = hf * lax.rsqrt(jnp.mean(hf * hf, axis=-1, keepdims=True) + EPS)
    return (y * g.astype(jnp.float32)).astype(h.dtype)


def layernorm(h, g, b):
    hf = h.astype(jnp.float32)
    mu = jnp.mean(hf, axis=-1, keepdims=True)
    var = jnp.mean(jnp.square(hf - mu), axis=-1, keepdims=True)
    y = (hf - mu) * lax.rsqrt(var + EPS)
    return (y * g.astype(jnp.float32) + b.astype(jnp.float32)).astype(h.dtype)


def modulate(h, g, shift, scale):
    return rmsnorm(h, g) * (1 + scale) + shift


def half_ffn(h, g, shift, scale, gate, w_gu, w_down):
    u = modulate(h, g, shift, scale)
    a, b = jnp.split(u @ w_gu, 2, axis=-1)
    return h + 0.5 * gate * ((jax.nn.silu(a) * b) @ w_down)


def split_in(p):
    offs, o = [], 0
    for w in IN_SPLITS[:-1]:
        o += w
        offs.append(o)
    return jnp.split(p, offs, axis=-1)


def heads(t, d):
    return t.reshape(t.shape[:-1] + (GLA_HEADS, d))


def flip(t):
    return jnp.flip(t, axis=1)


def log_decay(lr, w, b):
    z = (lr @ w + b).astype(jnp.float32)
    return heads(jax.nn.log_sigmoid(z) / GLA_TAU, GLA_DK)


def conformer_conv(p, dw_w, dw_b, ln_g, ln_b, w_o):
    a, b = jnp.split(p, 2, axis=-1)
    z = a * jax.nn.sigmoid(b)
    z = lax.conv_general_dilated(
        z, dw_w[:, None, :].astype(z.dtype), window_strides=(1,),
        padding=((CONV_WIDTH // 2, CONV_WIDTH // 2),),
        dimension_numbers=('NWC', 'WIO', 'NWC'), feature_group_count=D_CONV) + dw_b
    z = jax.nn.silu(layernorm(z, ln_g, ln_b))
    return z @ w_o


def gla_scan(q, k, v, log_a, s0):
    bsz, L = q.shape[0], q.shape[1]
    n = L // GLA_CHUNK

    def chunks(t):
        t = t.astype(jnp.float32).reshape(bsz, n, GLA_CHUNK, GLA_HEADS, t.shape[-1])
        return jnp.transpose(t, (1, 0, 3, 2, 4))

    mask = jnp.tril(jnp.ones((GLA_CHUNK, GLA_CHUNK), dtype=bool))[:, :, None]

    def step(s, inp):
        qc, kc, vc, gc = inp
        b = jnp.cumsum(gc, axis=2)
        inter = jnp.einsum('bhtk,bhkv->bhtv', qc * jnp.exp(b), s)
        rel = jnp.where(mask, b[:, :, :, None, :] - b[:, :, None, :, :], -jnp.inf)
        att = jnp.einsum('bhtk,bhtsk,bhsk->bhts', qc, jnp.exp(rel), kc)
        intra = jnp.einsum('bhts,bhsv->bhtv', att, vc)
        b_last = b[:, :, -1:, :]
        s_new = jnp.exp(b_last[:, :, 0, :, None]) * s + jnp.einsum(
            'bhsk,bhsv->bhkv', kc * jnp.exp(b_last - b), vc)
        return s_new, inter + intra

    _, o = lax.scan(step, s0, (chunks(q), chunks(k), chunks(v), chunks(log_a)))
    return jnp.transpose(o, (1, 0, 3, 2, 4)).reshape(bsz, L, GLA_HEADS, GLA_DV)


def gla_final_state(k, v, log_a):
    b = jnp.cumsum(log_a.astype(jnp.float32), axis=1)
    w = jnp.exp(b[:, -1:] - b)
    return jnp.einsum('blhk,blhv->bhkv', k.astype(jnp.float32) * w, v.astype(jnp.float32))


def bidir_gla(q, k, v, la_f, la_b, s_f, s_b):
    o_f = gla_scan(q, k, v, la_f, s_f)
    o_b = gla_scan(flip(q), flip(k), flip(v), flip(la_b), s_b)
    return o_f + flip(o_b)


def gla_output(o, og, gn_g, w_go):
    o = o * lax.rsqrt(jnp.mean(o * o, axis=-1, keepdims=True) + EPS)
    o = (o.reshape(o.shape[0], o.shape[1], V_W) * gn_g.astype(jnp.float32)).astype(og.dtype)
    return (o * jax.nn.silu(og)) @ w_go


def branch_merge(conv_in, o, og, ga, gb, dw_w, dw_b, ln_g, ln_b, w_co, gn_g, w_go, w_o):
    y_conv = conformer_conv(conv_in, dw_w, dw_b, ln_g, ln_b, w_co)
    y_gla = gla_output(o, og, gn_g, w_go)
    return (jax.nn.sigmoid(ga) * y_conv + jax.nn.sigmoid(gb) * y_gla) @ w_o


def _fwd_setup_inputs(seed: int = 0) -> dict:
    key = jax.random.key(seed)
    ks = jax.random.split(key, 32)
    f32 = jnp.float32
    D, L = D_MODEL, DEPTH

    def nrm(k, shape, scale):
        return jax.random.normal(k, shape, f32) * scale

    def gain(k, shape):
        return 1.0 + 0.05 * jax.random.normal(k, shape, f32)

    return {
        'x': nrm(ks[0], (BATCH, SEQ, D), 1.0),
        'c': nrm(ks[1], (BATCH, D), 1.0),
        'ctx': nrm(ks[2], (BATCH, CTX_LEN, D), 1.0),
        'c_ctx': nrm(ks[3], (D,), 1.0),
        'w_mod': nrm(ks[4], (L, D, N_MOD * D), 0.5 * D ** -0.5),
        'b_mod': nrm(ks[5], (L, N_MOD * D), 0.01),
        'g_ffn1': gain(ks[6], (L, D)),
        'w1_gu': nrm(ks[7], (L, D, 2 * D_FF), D ** -0.5),
        'w1_down': nrm(ks[8], (L, D_FF, D), D_FF ** -0.5),
        'g_mix': gain(ks[9], (L, D)),
        'w_in': nrm(ks[10], (L, D, D_IN), D ** -0.5),
        'dw_weight': nrm(ks[11], (L, CONV_WIDTH, D_CONV), CONV_WIDTH ** -0.5),
        'dw_bias': nrm(ks[12], (L, D_CONV), 0.01),
        'conv_ln_g': gain(ks[13], (L, D_CONV)),
        'conv_ln_b': nrm(ks[14], (L, D_CONV), 0.01),
        'w_conv_out': nrm(ks[15], (L, D_CONV, D), D_CONV ** -0.5),
        'w_alpha_f': nrm(ks[16], (L, GLA_LOWRANK, QK_W), GLA_LOWRANK ** -0.5),
        'b_alpha_f': nrm(ks[17], (L, QK_W), 0.1),
        'w_alpha_b': nrm(ks[18], (L, GLA_LOWRANK, QK_W), GLA_LOWRANK ** -0.5),
        'b_alpha_b': nrm(ks[19], (L, QK_W), 0.1),
        'gla_norm_g': gain(ks[20], (L, V_W)),
        'w_gla_out': nrm(ks[21], (L, V_W, D), V_W ** -0.5),
        'w_out': nrm(ks[22], (L, D, D), D ** -0.5),
        'g_ffn2': gain(ks[23], (L, D)),
        'w2_gu': nrm(ks[24], (L, D, 2 * D_FF), D ** -0.5),
        'w2_down': nrm(ks[25], (L, D_FF, D), D_FF ** -0.5),
        'g_final': gain(ks[26], (D,)),
    }


def _fwd_reference(x, c, ctx, c_ctx, w_mod, b_mod, g_ffn1, w1_gu, w1_down, g_mix, w_in,
              dw_weight, dw_bias, conv_ln_g, conv_ln_b, w_conv_out, w_alpha_f, b_alpha_f,
              w_alpha_b, b_alpha_b, gla_norm_g, w_gla_out, w_out, g_ffn2, w2_gu, w2_down, g_final):
    q_scale = GLA_DK ** -0.5
    h = ctx
    for l in range(DEPTH):
        last = l == DEPTH - 1
        mx = jnp.split((jax.nn.silu(c) @ w_mod[l] + b_mod[l])[:, None, :], N_MOD, axis=-1)
        mc = jnp.split(jax.nn.silu(c_ctx) @ w_mod[l] + b_mod[l], N_MOD, axis=-1)

        x = half_ffn(x, g_ffn1[l], mx[0], mx[1], mx[2], w1_gu[l], w1_down[l])
        h = half_ffn(h, g_ffn1[l], mc[0], mc[1], mc[2], w1_gu[l], w1_down[l])

        conv_x, q_x, k_x, v_x, og_x, af_x, ab_x, ga_x, gb_x = split_in(
            modulate(x, g_mix[l], mx[3], mx[4]) @ w_in[l])
        conv_c, q_c, k_c, v_c, og_c, af_c, ab_c, ga_c, gb_c = split_in(
            modulate(h, g_mix[l], mc[3], mc[4]) @ w_in[l])

        k_c, v_c = heads(k_c, GLA_DK), heads(v_c, GLA_DV)
        laf_c = log_decay(af_c, w_alpha_f[l], b_alpha_f[l])
        lab_c = log_decay(ab_c, w_alpha_b[l], b_alpha_b[l])
        s_f = gla_final_state(k_c, v_c, laf_c)
        s_b = gla_final_state(flip(k_c), flip(v_c), flip(lab_c))

        o_x = bidir_gla(heads(q_x, GLA_DK) * q_scale, heads(k_x, GLA_DK), heads(v_x, GLA_DV),
                        log_decay(af_x, w_alpha_f[l], b_alpha_f[l]),
                        log_decay(ab_x, w_alpha_b[l], b_alpha_b[l]), s_f, s_b)
        mix_x = branch_merge(conv_x, o_x, og_x, ga_x, gb_x, dw_weight[l], dw_bias[l], conv_ln_g[l],
                             conv_ln_b[l], w_conv_out[l], gla_norm_g[l], w_gla_out[l], w_out[l])

        if not last:
            zeros = jnp.zeros_like(s_f)
            o_c = bidir_gla(heads(q_c, GLA_DK) * q_scale, k_c, v_c, laf_c, lab_c, zeros, zeros)
            mix_c = branch_merge(conv_c, o_c, og_c, ga_c, gb_c, dw_weight[l], dw_bias[l], conv_ln_g[l],
                                 conv_ln_b[l], w_conv_out[l], gla_norm_g[l], w_gla_out[l], w_out[l])
            h = h + mc[5] * mix_c
            h = half_ffn(h, g_ffn2[l], mc[6], mc[7], mc[8], w2_gu[l], w2_down[l])

        x = x + mx[5] * mix_x
        x = half_ffn(x, g_ffn2[l], mx[6], mx[7], mx[8], w2_gu[l], w2_down[l])
    return rmsnorm(x, g_final)


import jax as _jax
import jax.numpy as _jnp

TWIN_FORMAT = 'train_step'
FWD_PARAMS = ['x', 'c', 'ctx', 'c_ctx', 'w_mod', 'b_mod', 'g_ffn1', 'w1_gu', 'w1_down', 'g_mix', 'w_in', 'dw_weight', 'dw_bias', 'conv_ln_g', 'conv_ln_b', 'w_conv_out', 'w_alpha_f', 'b_alpha_f', 'w_alpha_b', 'b_alpha_b', 'gla_norm_g', 'w_gla_out', 'w_out', 'g_ffn2', 'w2_gu', 'w2_down', 'g_final']
TWIN_WEIGHTS = ['c_ctx', 'w_mod', 'b_mod', 'g_ffn1', 'w1_gu', 'w1_down', 'g_mix', 'w_in', 'dw_weight', 'dw_bias', 'conv_ln_g', 'conv_ln_b', 'w_conv_out', 'w_alpha_f', 'b_alpha_f', 'w_alpha_b', 'b_alpha_b', 'gla_norm_g', 'w_gla_out', 'w_out', 'g_ffn2', 'w2_gu', 'w2_down', 'g_final']
TWIN_DIFF_INPUT = 'x'
TWIN_INPUTS = ['x', 'c', 'ctx', 'c_ctx', 'w_mod', 'b_mod', 'g_ffn1', 'w1_gu', 'w1_down', 'g_mix', 'w_in', 'dw_weight', 'dw_bias', 'conv_ln_g', 'conv_ln_b', 'w_conv_out', 'w_alpha_f', 'b_alpha_f', 'w_alpha_b', 'b_alpha_b', 'gla_norm_g', 'w_gla_out', 'w_out', 'g_ffn2', 'w2_gu', 'w2_down', 'g_final', 'loss_target', 'm_c_ctx', 'm_w_mod', 'm_b_mod', 'm_g_ffn1', 'm_w1_gu', 'm_w1_down', 'm_g_mix', 'm_w_in', 'm_dw_weight', 'm_dw_bias', 'm_conv_ln_g', 'm_conv_ln_b', 'm_w_conv_out', 'm_w_alpha_f', 'm_b_alpha_f', 'm_w_alpha_b', 'm_b_alpha_b', 'm_gla_norm_g', 'm_w_gla_out', 'm_w_out', 'm_g_ffn2', 'm_w2_gu', 'm_w2_down', 'm_g_final', 'v_c_ctx', 'v_w_mod', 'v_b_mod', 'v_g_ffn1', 'v_w1_gu', 'v_w1_down', 'v_g_mix', 'v_w_in', 'v_dw_weight', 'v_dw_bias', 'v_conv_ln_g', 'v_conv_ln_b', 'v_w_conv_out', 'v_w_alpha_f', 'v_b_alpha_f', 'v_w_alpha_b', 'v_b_alpha_b', 'v_gla_norm_g', 'v_w_gla_out', 'v_w_out', 'v_g_ffn2', 'v_w2_gu', 'v_w2_down', 'v_g_final']
TWIN_OUTPUTS = ['loss', 'grad_x', 'grad_c_ctx', 'grad_w_mod', 'grad_b_mod', 'grad_g_ffn1', 'grad_w1_gu', 'grad_w1_down', 'grad_g_mix', 'grad_w_in', 'grad_dw_weight', 'grad_dw_bias', 'grad_conv_ln_g', 'grad_conv_ln_b', 'grad_w_conv_out', 'grad_w_alpha_f', 'grad_b_alpha_f', 'grad_w_alpha_b', 'grad_b_alpha_b', 'grad_gla_norm_g', 'grad_w_gla_out', 'grad_w_out', 'grad_g_ffn2', 'grad_w2_gu', 'grad_w2_down', 'grad_g_final', 'delta_c_ctx', 'delta_w_mod', 'delta_b_mod', 'delta_g_ffn1', 'delta_w1_gu', 'delta_w1_down', 'delta_g_mix', 'delta_w_in', 'delta_dw_weight', 'delta_dw_bias', 'delta_conv_ln_g', 'delta_conv_ln_b', 'delta_w_conv_out', 'delta_w_alpha_f', 'delta_b_alpha_f', 'delta_w_alpha_b', 'delta_b_alpha_b', 'delta_gla_norm_g', 'delta_w_gla_out', 'delta_w_out', 'delta_g_ffn2', 'delta_w2_gu', 'delta_w2_down', 'delta_g_final', 'new_m_c_ctx', 'new_m_w_mod', 'new_m_b_mod', 'new_m_g_ffn1', 'new_m_w1_gu', 'new_m_w1_down', 'new_m_g_mix', 'new_m_w_in', 'new_m_dw_weight', 'new_m_dw_bias', 'new_m_conv_ln_g', 'new_m_conv_ln_b', 'new_m_w_conv_out', 'new_m_w_alpha_f', 'new_m_b_alpha_f', 'new_m_w_alpha_b', 'new_m_b_alpha_b', 'new_m_gla_norm_g', 'new_m_w_gla_out', 'new_m_w_out', 'new_m_g_ffn2', 'new_m_w2_gu', 'new_m_w2_down', 'new_m_g_final', 'new_v_c_ctx', 'new_v_w_mod', 'new_v_b_mod', 'new_v_g_ffn1', 'new_v_w1_gu', 'new_v_w1_down', 'new_v_g_mix', 'new_v_w_in', 'new_v_dw_weight', 'new_v_dw_bias', 'new_v_conv_ln_g', 'new_v_conv_ln_b', 'new_v_w_conv_out', 'new_v_w_alpha_f', 'new_v_b_alpha_f', 'new_v_w_alpha_b', 'new_v_b_alpha_b', 'new_v_gla_norm_g', 'new_v_w_gla_out', 'new_v_w_out', 'new_v_g_ffn2', 'new_v_w2_gu', 'new_v_w2_down', 'new_v_g_final']
TWIN_LEAF_KINDS = {'loss': 'loss', 'grad_x': 'grad_x', 'grad_c_ctx': 'grad_w', 'grad_w_mod': 'grad_w', 'grad_b_mod': 'grad_w', 'grad_g_ffn1': 'grad_w', 'grad_w1_gu': 'grad_w', 'grad_w1_down': 'grad_w', 'grad_g_mix': 'grad_w', 'grad_w_in': 'grad_w', 'grad_dw_weight': 'grad_w', 'grad_dw_bias': 'grad_w', 'grad_conv_ln_g': 'grad_w', 'grad_conv_ln_b': 'grad_w', 'grad_w_conv_out': 'grad_w', 'grad_w_alpha_f': 'grad_w', 'grad_b_alpha_f': 'grad_w', 'grad_w_alpha_b': 'grad_w', 'grad_b_alpha_b': 'grad_w', 'grad_gla_norm_g': 'grad_w', 'grad_w_gla_out': 'grad_w', 'grad_w_out': 'grad_w', 'grad_g_ffn2': 'grad_w', 'grad_w2_gu': 'grad_w', 'grad_w2_down': 'grad_w', 'grad_g_final': 'grad_w', 'delta_c_ctx': 'delta_w', 'delta_w_mod': 'delta_w', 'delta_b_mod': 'delta_w', 'delta_g_ffn1': 'delta_w', 'delta_w1_gu': 'delta_w', 'delta_w1_down': 'delta_w', 'delta_g_mix': 'delta_w', 'delta_w_in': 'delta_w', 'delta_dw_weight': 'delta_w', 'delta_dw_bias': 'delta_w', 'delta_conv_ln_g': 'delta_w', 'delta_conv_ln_b': 'delta_w', 'delta_w_conv_out': 'delta_w', 'delta_w_alpha_f': 'delta_w', 'delta_b_alpha_f': 'delta_w', 'delta_w_alpha_b': 'delta_w', 'delta_b_alpha_b': 'delta_w', 'delta_gla_norm_g': 'delta_w', 'delta_w_gla_out': 'delta_w', 'delta_w_out': 'delta_w', 'delta_g_ffn2': 'delta_w', 'delta_w2_gu': 'delta_w', 'delta_w2_down': 'delta_w', 'delta_g_final': 'delta_w', 'new_m_c_ctx': 'new_m', 'new_m_w_mod': 'new_m', 'new_m_b_mod': 'new_m', 'new_m_g_ffn1': 'new_m', 'new_m_w1_gu': 'new_m', 'new_m_w1_down': 'new_m', 'new_m_g_mix': 'new_m', 'new_m_w_in': 'new_m', 'new_m_dw_weight': 'new_m', 'new_m_dw_bias': 'new_m', 'new_m_conv_ln_g': 'new_m', 'new_m_conv_ln_b': 'new_m', 'new_m_w_conv_out': 'new_m', 'new_m_w_alpha_f': 'new_m', 'new_m_b_alpha_f': 'new_m', 'new_m_w_alpha_b': 'new_m', 'new_m_b_alpha_b': 'new_m', 'new_m_gla_norm_g': 'new_m', 'new_m_w_gla_out': 'new_m', 'new_m_w_out': 'new_m', 'new_m_g_ffn2': 'new_m', 'new_m_w2_gu': 'new_m', 'new_m_w2_down': 'new_m', 'new_m_g_final': 'new_m', 'new_v_c_ctx': 'new_v', 'new_v_w_mod': 'new_v', 'new_v_b_mod': 'new_v', 'new_v_g_ffn1': 'new_v', 'new_v_w1_gu': 'new_v', 'new_v_w1_down': 'new_v', 'new_v_g_mix': 'new_v', 'new_v_w_in': 'new_v', 'new_v_dw_weight': 'new_v', 'new_v_dw_bias': 'new_v', 'new_v_conv_ln_g': 'new_v', 'new_v_conv_ln_b': 'new_v', 'new_v_w_conv_out': 'new_v', 'new_v_w_alpha_f': 'new_v', 'new_v_b_alpha_f': 'new_v', 'new_v_w_alpha_b': 'new_v', 'new_v_b_alpha_b': 'new_v', 'new_v_gla_norm_g': 'new_v', 'new_v_w_gla_out': 'new_v', 'new_v_w_out': 'new_v', 'new_v_g_ffn2': 'new_v', 'new_v_w2_gu': 'new_v', 'new_v_w2_down': 'new_v', 'new_v_g_final': 'new_v'}


def _forward(args):
    return _fwd_reference(*[args[k] for k in FWD_PARAMS])


def _output_shape():
    out = _jax.eval_shape(lambda: _forward(_fwd_setup_inputs(0)))
    return out.shape, out.dtype

N_MICROBATCH = 1
ADAM_LR = 0.001
ADAM_B1 = 0.9
ADAM_B2 = 0.999
ADAM_EPS = 1e-08
ADAM_WD = 0.01
ADAM_STEP = 10
PER_EXAMPLE_BATCH_AXIS = {'x': 0, 'c': 0, 'ctx': 0, 'loss_target': 0}
SHARED_INPUTS = []
_WEIGHT_DTYPES = {'c_ctx': _jnp.float32, 'w_mod': _jnp.float32, 'b_mod': _jnp.float32, 'g_ffn1': _jnp.float32, 'w1_gu': _jnp.float32, 'w1_down': _jnp.float32, 'g_mix': _jnp.float32, 'w_in': _jnp.float32, 'dw_weight': _jnp.float32, 'dw_bias': _jnp.float32, 'conv_ln_g': _jnp.float32, 'conv_ln_b': _jnp.float32, 'w_conv_out': _jnp.float32, 'w_alpha_f': _jnp.float32, 'b_alpha_f': _jnp.float32, 'w_alpha_b': _jnp.float32, 'b_alpha_b': _jnp.float32, 'gla_norm_g': _jnp.float32, 'w_gla_out': _jnp.float32, 'w_out': _jnp.float32, 'g_ffn2': _jnp.float32, 'w2_gu': _jnp.float32, 'w2_down': _jnp.float32, 'g_final': _jnp.float32}
MOMENT_SCALE = {'c_ctx': 4.171761e-03, 'w_mod': 5.197534e-02, 'b_mod': 9.037981e-02, 'g_ffn1': 3.836079e-02, 'w1_gu': 1.733824e-02, 'w1_down': 2.847224e-02, 'g_mix': 6.340329e-02, 'w_in': 2.504715e-02, 'dw_weight': 2.716647e-02, 'dw_bias': 4.360327e-02, 'conv_ln_g': 3.134846e-02, 'conv_ln_b': 2.554228e-02, 'w_conv_out': 2.516147e-02, 'w_alpha_f': 5.028444e-03, 'b_alpha_f': 1.279996e-02, 'w_alpha_b': 4.787917e-03, 'b_alpha_b': 1.281208e-02, 'gla_norm_g': 2.865964e-02, 'w_gla_out': 2.780516e-02, 'w_out': 3.773495e-02, 'g_ffn2': 4.237372e-02, 'w2_gu': 1.678269e-02, 'w2_down': 2.751674e-02, 'g_final': 6.398586e+01}


def _to_microbatches(a, axis):
    t = _jnp.moveaxis(a, axis, 0)
    t = t.reshape((N_MICROBATCH, t.shape[0] // N_MICROBATCH) + t.shape[1:])
    return _jnp.moveaxis(t, 1, axis + 1)


def setup_inputs(seed: int = 0) -> dict:
    inp = _fwd_setup_inputs(seed)
    key = _jax.random.fold_in(_jax.random.key(seed), 7919)
    shape, _ = _output_shape()
    out = dict(inp)
    out["loss_target"] = _jax.random.normal(_jax.random.fold_in(key, 0), shape, _jnp.float32)
    for i, name in enumerate(TWIN_WEIGHTS):
        w = inp[name].astype(_jnp.float32)
        if MOMENT_SCALE is None:
            s = _jnp.sqrt(_jnp.mean(_jnp.square(w)) + 1e-30)
        else:
            s = MOMENT_SCALE[name]
        km, kv = _jax.random.split(_jax.random.fold_in(key, i + 1))
        out[name] = w
        out["m_" + name] = s * _jax.random.normal(km, w.shape, _jnp.float32)
        out["v_" + name] = (s * s) * _jax.random.uniform(kv, w.shape, _jnp.float32, 0.5, 1.5)
    if N_MICROBATCH > 1:
        for name, axis in PER_EXAMPLE_BATCH_AXIS.items():
            out[name] = _to_microbatches(out[name], axis)
    return {'x': out['x'], 'c': out['c'], 'ctx': out['ctx'], 'c_ctx': out['c_ctx'], 'w_mod': out['w_mod'], 'b_mod': out['b_mod'], 'g_ffn1': out['g_ffn1'], 'w1_gu': out['w1_gu'], 'w1_down': out['w1_down'], 'g_mix': out['g_mix'], 'w_in': out['w_in'], 'dw_weight': out['dw_weight'], 'dw_bias': out['dw_bias'], 'conv_ln_g': out['conv_ln_g'], 'conv_ln_b': out['conv_ln_b'], 'w_conv_out': out['w_conv_out'], 'w_alpha_f': out['w_alpha_f'], 'b_alpha_f': out['b_alpha_f'], 'w_alpha_b': out['w_alpha_b'], 'b_alpha_b': out['b_alpha_b'], 'gla_norm_g': out['gla_norm_g'], 'w_gla_out': out['w_gla_out'], 'w_out': out['w_out'], 'g_ffn2': out['g_ffn2'], 'w2_gu': out['w2_gu'], 'w2_down': out['w2_down'], 'g_final': out['g_final'], 'loss_target': out['loss_target'], 'm_c_ctx': out['m_c_ctx'], 'm_w_mod': out['m_w_mod'], 'm_b_mod': out['m_b_mod'], 'm_g_ffn1': out['m_g_ffn1'], 'm_w1_gu': out['m_w1_gu'], 'm_w1_down': out['m_w1_down'], 'm_g_mix': out['m_g_mix'], 'm_w_in': out['m_w_in'], 'm_dw_weight': out['m_dw_weight'], 'm_dw_bias': out['m_dw_bias'], 'm_conv_ln_g': out['m_conv_ln_g'], 'm_conv_ln_b': out['m_conv_ln_b'], 'm_w_conv_out': out['m_w_conv_out'], 'm_w_alpha_f': out['m_w_alpha_f'], 'm_b_alpha_f': out['m_b_alpha_f'], 'm_w_alpha_b': out['m_w_alpha_b'], 'm_b_alpha_b': out['m_b_alpha_b'], 'm_gla_norm_g': out['m_gla_norm_g'], 'm_w_gla_out': out['m_w_gla_out'], 'm_w_out': out['m_w_out'], 'm_g_ffn2': out['m_g_ffn2'], 'm_w2_gu': out['m_w2_gu'], 'm_w2_down': out['m_w2_down'], 'm_g_final': out['m_g_final'], 'v_c_ctx': out['v_c_ctx'], 'v_w_mod': out['v_w_mod'], 'v_b_mod': out['v_b_mod'], 'v_g_ffn1': out['v_g_ffn1'], 'v_w1_gu': out['v_w1_gu'], 'v_w1_down': out['v_w1_down'], 'v_g_mix': out['v_g_mix'], 'v_w_in': out['v_w_in'], 'v_dw_weight': out['v_dw_weight'], 'v_dw_bias': out['v_dw_bias'], 'v_conv_ln_g': out['v_conv_ln_g'], 'v_conv_ln_b': out['v_conv_ln_b'], 'v_w_conv_out': out['v_w_conv_out'], 'v_w_alpha_f': out['v_w_alpha_f'], 'v_b_alpha_f': out['v_b_alpha_f'], 'v_w_alpha_b': out['v_w_alpha_b'], 'v_b_alpha_b': out['v_b_alpha_b'], 'v_gla_norm_g': out['v_gla_norm_g'], 'v_w_gla_out': out['v_w_gla_out'], 'v_w_out': out['v_w_out'], 'v_g_ffn2': out['v_g_ffn2'], 'v_w2_gu': out['v_w2_gu'], 'v_w2_down': out['v_w2_down'], 'v_g_final': out['v_g_final']}


def _loss(weights, diff, rest, loss_target):
    with _jax.named_scope("forward"):
        args = {**rest, TWIN_DIFF_INPUT: diff, **{k: w.astype(_WEIGHT_DTYPES[k]) for k, w in weights.items()}}
        y = _forward(args)
    with _jax.named_scope("loss_head"):
        err = _jnp.square(y.astype(_jnp.float32) - loss_target)
        return 0.5 * _jnp.sum(_jnp.mean(err, axis=-1)) if err.ndim else 0.5 * err


def _adamw(w, g, m, v):
    m = ADAM_B1 * m + (1.0 - ADAM_B1) * g
    v = ADAM_B2 * v + (1.0 - ADAM_B2) * _jnp.square(g)
    m_hat = m / (1.0 - ADAM_B1 ** ADAM_STEP)
    v_hat = v / (1.0 - ADAM_B2 ** ADAM_STEP)
    delta = -ADAM_LR * (m_hat / (_jnp.sqrt(v_hat) + ADAM_EPS) + ADAM_WD * w)
    return delta, m, v


def reference(x, c, ctx, c_ctx, w_mod, b_mod, g_ffn1, w1_gu, w1_down, g_mix, w_in, dw_weight, dw_bias, conv_ln_g, conv_ln_b, w_conv_out, w_alpha_f, b_alpha_f, w_alpha_b, b_alpha_b, gla_norm_g, w_gla_out, w_out, g_ffn2, w2_gu, w2_down, g_final, loss_target, m_c_ctx, m_w_mod, m_b_mod, m_g_ffn1, m_w1_gu, m_w1_down, m_g_mix, m_w_in, m_dw_weight, m_dw_bias, m_conv_ln_g, m_conv_ln_b, m_w_conv_out, m_w_alpha_f, m_b_alpha_f, m_w_alpha_b, m_b_alpha_b, m_gla_norm_g, m_w_gla_out, m_w_out, m_g_ffn2, m_w2_gu, m_w2_down, m_g_final, v_c_ctx, v_w_mod, v_b_mod, v_g_ffn1, v_w1_gu, v_w1_down, v_g_mix, v_w_in, v_dw_weight, v_dw_bias, v_conv_ln_g, v_conv_ln_b, v_w_conv_out, v_w_alpha_f, v_b_alpha_f, v_w_alpha_b, v_b_alpha_b, v_gla_norm_g, v_w_gla_out, v_w_out, v_g_ffn2, v_w2_gu, v_w2_down, v_g_final):
    given = dict(x=x, c=c, ctx=ctx, c_ctx=c_ctx, w_mod=w_mod, b_mod=b_mod, g_ffn1=g_ffn1, w1_gu=w1_gu, w1_down=w1_down, g_mix=g_mix, w_in=w_in, dw_weight=dw_weight, dw_bias=dw_bias, conv_ln_g=conv_ln_g, conv_ln_b=conv_ln_b, w_conv_out=w_conv_out, w_alpha_f=w_alpha_f, b_alpha_f=b_alpha_f, w_alpha_b=w_alpha_b, b_alpha_b=b_alpha_b, gla_norm_g=gla_norm_g, w_gla_out=w_gla_out, w_out=w_out, g_ffn2=g_ffn2, w2_gu=w2_gu, w2_down=w2_down, g_final=g_final, loss_target=loss_target, m_c_ctx=m_c_ctx, m_w_mod=m_w_mod, m_b_mod=m_b_mod, m_g_ffn1=m_g_ffn1, m_w1_gu=m_w1_gu, m_w1_down=m_w1_down, m_g_mix=m_g_mix, m_w_in=m_w_in, m_dw_weight=m_dw_weight, m_dw_bias=m_dw_bias, m_conv_ln_g=m_conv_ln_g, m_conv_ln_b=m_conv_ln_b, m_w_conv_out=m_w_conv_out, m_w_alpha_f=m_w_alpha_f, m_b_alpha_f=m_b_alpha_f, m_w_alpha_b=m_w_alpha_b, m_b_alpha_b=m_b_alpha_b, m_gla_norm_g=m_gla_norm_g, m_w_gla_out=m_w_gla_out, m_w_out=m_w_out, m_g_ffn2=m_g_ffn2, m_w2_gu=m_w2_gu, m_w2_down=m_w2_down, m_g_final=m_g_final, v_c_ctx=v_c_ctx, v_w_mod=v_w_mod, v_b_mod=v_b_mod, v_g_ffn1=v_g_ffn1, v_w1_gu=v_w1_gu, v_w1_down=v_w1_down, v_g_mix=v_g_mix, v_w_in=v_w_in, v_dw_weight=v_dw_weight, v_dw_bias=v_dw_bias, v_conv_ln_g=v_conv_ln_g, v_conv_ln_b=v_conv_ln_b, v_w_conv_out=v_w_conv_out, v_w_alpha_f=v_w_alpha_f, v_b_alpha_f=v_b_alpha_f, v_w_alpha_b=v_w_alpha_b, v_b_alpha_b=v_b_alpha_b, v_gla_norm_g=v_gla_norm_g, v_w_gla_out=v_w_gla_out, v_w_out=v_w_out, v_g_ffn2=v_g_ffn2, v_w2_gu=v_w2_gu, v_w2_down=v_w2_down, v_g_final=v_g_final)
    weights = {n: given[n] for n in TWIN_WEIGHTS}
    shared = {n: given[n] for n in SHARED_INPUTS}
    per_example = {n: given[n] for n in ['x', 'c', 'ctx']}
    grad_fn = _jax.value_and_grad(_loss, argnums=(0, 1))

    def one_microbatch(ex, loss_target):
        ex = dict(ex)
        diff = ex.pop(TWIN_DIFF_INPUT)
        return grad_fn(weights, diff, {**shared, **ex}, loss_target)

    if N_MICROBATCH == 1:
        loss, (grad_w, grad_x) = one_microbatch(per_example, given["loss_target"])
    else:
        def body(carry, xs):
            loss_sum, grad_sum = carry
            l_k, (gw_k, gx_k) = one_microbatch(xs[0], xs[1])
            with _jax.named_scope("update"):
                return (loss_sum + l_k, _jax.tree.map(_jnp.add, grad_sum, gw_k)), gx_k

        init = (_jnp.zeros((), _jnp.float32), _jax.tree.map(_jnp.zeros_like, weights))
        (loss, grad_w), grad_x = _jax.lax.scan(body, init, (per_example, given["loss_target"]))
    with _jax.named_scope("update"):
        delta_w, new_m, new_v = {}, {}, {}
        for n in TWIN_WEIGHTS:
            delta_w[n], new_m[n], new_v[n] = _adamw(weights[n], grad_w[n], given["m_" + n], given["v_" + n])
    return (loss, grad_x, *[grad_w[n] for n in TWIN_WEIGHTS], *[delta_w[n] for n in TWIN_WEIGHTS],
            *[new_m[n] for n in TWIN_WEIGHTS], *[new_v[n] for n in TWIN_WEIGHTS])
```

```python
import functools
import math

import jax
import jax.numpy as jnp
from jax import lax
from jax.experimental import pallas as pl
from jax.experimental.pallas import tpu as pltpu

F32, BF16 = jnp.float32, jnp.bfloat16
MESH = pl.DeviceIdType.MESH
HIGHEST = lax.Precision.HIGHEST

D = 1024
FF = 2816
HEADS, DK, DV = 4, 128, 256
LOWRANK = 16
CONV_W = 31
CHUNK = 64
TAU = 16.0
EPS = 1e-6
Q_SCALE = DK ** -0.5
TM = 256
D_IN = 7200
D_INP = 7296
LR_COL = 7168
VMEM_LIMIT = 52 * 1024 * 1024

ADAM_LR, ADAM_B1, ADAM_B2, ADAM_EPS, ADAM_WD, ADAM_STEP = 0.001, 0.9, 0.999, 1e-08, 0.01, 10

PIECE_COLS = 1024
PIECE_ROWS = 9216


def _silu(x):
    return x * jax.nn.sigmoid(x)


def _rms(h, g):
    return h * lax.rsqrt(jnp.mean(h * h, axis=-1, keepdims=True) + EPS) * g


def _modnorm(x, g, shift, scale):
    return _rms(x, g) * (1 + scale) + shift


def _cparams(sem=None):
    return pltpu.CompilerParams(dimension_semantics=sem, vmem_limit_bytes=VMEM_LIMIT)


def tok(arr, width=None, cb=0, clamp=None):
    return ("tok", arr, arr.shape[1] if width is None else width, cb, clamp)


def mod(arr):
    return ("mod", arr)


def const(arr):
    return ("const", arr)


def _rowmap(name, body, n_tiles, ins, outs, *, tpb, nb):
    def modrow(i):
        return jnp.minimum(i // tpb, nb)

    in_specs, args = [], []
    for spec in ins:
        if spec[0] == "tok":
            _, arr, width, cb, clamp = spec
            if clamp is None:
                im = lambda i, cb=cb: (i, cb)
            else:
                im = lambda i, cb=cb, clamp=clamp: (jnp.minimum(i, clamp), cb)
            in_specs.append(pl.BlockSpec((TM, width), im))
        elif spec[0] == "mod":
            arr = spec[1]
            in_specs.append(pl.BlockSpec((1, 1, arr.shape[2]), lambda i: (modrow(i), 0, 0)))
        else:
            arr = spec[1]
            in_specs.append(pl.BlockSpec(arr.shape, lambda i, nd=arr.ndim: (0,) * nd))
        args.append(arr)
    out_specs, out_shapes = [], []
    for o in outs:
        if o[0] == "tok":
            _, rows, width, dtype = o
            out_shapes.append(jax.ShapeDtypeStruct((rows, width), dtype))
            out_specs.append(pl.BlockSpec((TM, width), lambda i: (i, 0)))
        elif o[0] == "acc":
            _, rows, width = o
            out_shapes.append(jax.ShapeDtypeStruct((rows, width), F32))
            out_specs.append(pl.BlockSpec((rows, width), lambda i: (0, 0)))
        else:
            width = o[1]
            rows_visited = min((n_tiles - 1) // tpb, nb) + 1
            out_shapes.append(jax.ShapeDtypeStruct((rows_visited, 1, width), F32))
            out_specs.append(pl.BlockSpec((1, 1, width), lambda i: (modrow(i), 0, 0)))
    n_in = len(ins)

    def kern(*refs):
        i = pl.program_id(0)
        vals = []
        for r, spec in zip(refs[:n_in], ins):
            vals.append(r[0] if spec[0] == "mod" else r[...])
        res = body(i, *vals)
        for r, o, val in zip(refs[n_in:], outs, res):
            if o[0] == "tok":
                r[...] = val.astype(r.dtype)
            elif o[0] == "acc":
                @pl.when(i == 0)
                def _():
                    r[...] = jnp.zeros(r.shape, F32)
                r[...] += jnp.broadcast_to(val, r.shape)
            else:
                first = jnp.logical_or(i == 0, modrow(i) != modrow(jnp.maximum(i - 1, 0)))

                @pl.when(first)
                def _():
                    r[...] = jnp.zeros(r.shape, F32)
                r[0] += val

    return pl.pallas_call(
        kern, grid=(n_tiles,), in_specs=in_specs, out_specs=out_specs, out_shape=out_shapes,
        compiler_params=_cparams(("arbitrary",)), name=name)(*args)


def _pick(n, cands):
    for c in cands:
        if n % c == 0:
            return c
    return n


def _mm(name, a, b, *, trans_b=False, out_dtype=F32, a_fn=None, bias=None, rows=None):
    m, k = a.shape
    m = m if rows is None else rows
    n = b.shape[0] if trans_b else b.shape[1]
    tk = k if k <= 2816 else _pick(k, (2816, 2432, 2304, 2048, 1536, 1408, 1024, 512, 256, 128))
    tm = _pick(m, (1024, 512, 256) if tk <= 1024 else (512, 256))
    tn = _pick(n, (512, 384, 256, 128))
    nk = k // tk
    dims = (((1,), (1,)), ((), ())) if trans_b else (((1,), (0,)), ((), ()))

    def kern(*refs):
        a_ref, b_ref = refs[0], refs[1]
        bias_ref = refs[2] if bias is not None else None
        o_ref, acc_ref = refs[-2], refs[-1]
        kk = pl.program_id(2)
        av = a_ref[...]
        if a_fn is not None:
            av = a_fn(av)
        p = lax.dot_general(av.astype(BF16), b_ref[...].astype(BF16), dims, preferred_element_type=F32)

        def finish(total):
            if bias_ref is not None:
                total = total + bias_ref[...]
            o_ref[...] = total.astype(o_ref.dtype)

        if nk == 1:
            finish(p)
        else:
            @pl.when(kk == 0)
            def _():
                acc_ref[...] = p

            @pl.when(kk > 0)
            def _():
                acc_ref[...] += p

            @pl.when(kk == nk - 1)
            def _():
                finish(acc_ref[...])

    in_specs = [pl.BlockSpec((tm, tk), lambda i, j, kk: (i, kk)),
                pl.BlockSpec((tn, tk), lambda i, j, kk: (j, kk)) if trans_b
                else pl.BlockSpec((tk, tn), lambda i, j, kk: (kk, j))]
    args = [a, b]
    if bias is not None:
        in_specs.append(pl.BlockSpec((1, tn), lambda i, j, kk: (0, j)))
        args.append(bias)
    return pl.pallas_call(
        kern, grid=(m // tm, n // tn, nk), in_specs=in_specs,
        out_specs=pl.BlockSpec((tm, tn), lambda i, j, kk: (i, j)),
        out_shape=jax.ShapeDtypeStruct((m, n), out_dtype),
        scratch_shapes=[pltpu.VMEM((tm, tn) if nk > 1 else (8, 128), F32)],
        compiler_params=_cparams(("parallel", "parallel", "arbitrary")), name=name)(*args)


def _mm_tn(name, x, dy, rows=None):
    t = x.shape[0] if rows is None else rows
    k1, n1 = x.shape[1], dy.shape[1]
    tt = _pick(t, (512, 256, 128, 64, 8))
    tk1 = _pick(k1, (1024, 1408, 512, 256, 128))
    tn = _pick(n1, (512, 384, 256, 128))

    def kern(x_ref, dy_ref, o_ref):
        s = pl.program_id(2)

        @pl.when(s == 0)
        def _():
            o_ref[...] = jnp.zeros(o_ref.shape, F32)
        o_ref[...] += lax.dot_general(x_ref[...].astype(BF16), dy_ref[...].astype(BF16),
                                      (((0,), (0,)), ((), ())), preferred_element_type=F32)

    return pl.pallas_call(
        kern, grid=(k1 // tk1, n1 // tn, t // tt),
        in_specs=[pl.BlockSpec((tt, tk1), lambda i, j, s: (s, i)),
                  pl.BlockSpec((tt, tn), lambda i, j, s: (s, j))],
        out_specs=pl.BlockSpec((tk1, tn), lambda i, j, s: (i, j)),
        out_shape=jax.ShapeDtypeStruct((k1, n1), F32),
        compiler_params=_cparams(("parallel", "parallel", "arbitrary")), name=name)(x, dy)


def _gla_maps(bl, t, tc):
    nx, nc = t // CHUNK, tc // CHUNK
    nxb = bl * nx

    def rowblk(d, b, n):
        c_ctx = jnp.where(d == 0, n, nc - 1 - n)
        c_x = jnp.where(d == 0, n - nc, nx - 1 - (n - nc))
        return jnp.where(n < nc, nxb + b * nc + c_ctx, b * nx + c_x)

    def xblk(d, b, n):
        n2 = jnp.maximum(n, nc)
        return b * nx + jnp.where(d == 0, n2 - nc, nx - 1 - (n2 - nc))

    return nx, nc, rowblk, xblk


def _gla_chunk(m, q, k, v, g, h):
    gh = g[:, h * DK:(h + 1) * DK]
    b = jnp.dot(m, gh, precision=HIGHEST, preferred_element_type=F32)
    tot = jnp.sum(gh, axis=0, keepdims=True)
    mid = b[CHUNK // 2:CHUNK // 2 + 1, :]
    qh = q[:, h * DK:(h + 1) * DK] * Q_SCALE
    kh = k[:, h * DK:(h + 1) * DK]
    vh = v[:, h * DV:(h + 1) * DV]
    return b, tot, mid, qh, kh, vh


def _dot(a, b, ca, cb):
    return lax.dot_general(a.astype(BF16), b.astype(BF16), (((ca,), (cb,)), ((), ())),
                           preferred_element_type=F32)


def _gla_fwd(p, g2, mmats, bl, t, tc):
    nx, nc, rowblk, xblk = _gla_maps(bl, t, tc)
    ns = nx + nc

    def kern(q_ref, k_ref, v_ref, g_ref, m_ref, o_ref, ss_ref, s_ref):
        n = pl.program_id(2)

        @pl.when(n == 0)
        def _():
            s_ref[...] = jnp.zeros(s_ref.shape, F32)
        m = m_ref[0]
        q, k, v, g = q_ref[...], k_ref[...], v_ref[...], g_ref[0]
        for h in range(HEADS):
            b, tot, mid, qh, kh, vh = _gla_chunk(m, q, k, v, g, h)
            s0 = s_ref[h * DV:(h + 1) * DV, :]
            ss_ref[0, 0, 0, h * DV:(h + 1) * DV, :] = s0
            kl = kh * jnp.exp(tot - b)
            s_ref[h * DV:(h + 1) * DV, :] = s0 * jnp.exp(tot) + _dot(vh, kl, 0, 0)

            @pl.when(n >= nc)
            def _():
                qe = qh * jnp.exp(b)
                qm = qh * jnp.exp(b - mid)
                km = kh * jnp.exp(mid - b)
                att = m * _dot(qm, km, 1, 1)
                o_ref[0, :, h * DV:(h + 1) * DV] = _dot(qe, s0, 1, 1) + _dot(att, vh, 1, 0)

    nt = p.shape[0]
    return pl.pallas_call(
        kern, grid=(2, bl, ns),
        in_specs=[pl.BlockSpec((CHUNK, 512), lambda d, b, n: (rowblk(d, b, n), 4)),
                  pl.BlockSpec((CHUNK, 512), lambda d, b, n: (rowblk(d, b, n), 5)),
                  pl.BlockSpec((CHUNK, 1024), lambda d, b, n: (rowblk(d, b, n), 3)),
                  pl.BlockSpec((1, CHUNK, 512), lambda d, b, n: (d, rowblk(d, b, n), 0)),
                  pl.BlockSpec((1, CHUNK, CHUNK), lambda d, b, n: (d, 0, 0))],
        out_specs=[pl.BlockSpec((1, CHUNK, 1024), lambda d, b, n: (d, xblk(d, b, n), 0)),
                   pl.BlockSpec((1, 1, 1, HEADS * DV, DK), lambda d, b, n: (d, b, n, 0, 0))],
        out_shape=[jax.ShapeDtypeStruct((2, bl * t, HEADS * DV), F32),
                   jax.ShapeDtypeStruct((2, bl, ns, HEADS * DV, DK), F32)],
        scratch_shapes=[pltpu.VMEM((HEADS * DV, DK), F32)],
        compiler_params=_cparams(("parallel", "parallel", "arbitrary")), name="gla_fwd")(p, p, p, g2, mmats)


def _gla_bwd(p, g2, mmats, ssave, do, bl, t, tc):
    nx, nc, rowblk, xblk = _gla_maps(bl, t, tc)
    ns = nx + nc
    rev = lambda n: ns - 1 - n

    def kern(q_ref, k_ref, v_ref, g_ref, m_ref, ss_ref, do_ref, dq_ref, dk_ref, dv_ref, dg_ref, ds_ref):
        step = pl.program_id(2)
        n = ns - 1 - step

        @pl.when(step == 0)
        def _():
            ds_ref[...] = jnp.zeros(ds_ref.shape, F32)
        m = m_ref[0]
        q, k, v, g = q_ref[...], k_ref[...], v_ref[...], g_ref[0]
        dov = do_ref[...] * (n >= nc).astype(F32)
        for h in range(HEADS):
            b, tot, mid, qh, kh, vh = _gla_chunk(m, q, k, v, g, h)
            doh = dov[:, h * DV:(h + 1) * DV]
            s0 = ss_ref[0, 0, 0, h * DV:(h + 1) * DV, :]
            ds1 = ds_ref[h * DV:(h + 1) * DV, :]
            eb, ebm, emb, etb = jnp.exp(b), jnp.exp(b - mid), jnp.exp(mid - b), jnp.exp(tot - b)
            etot = jnp.exp(tot)
            qe, qm, km, kl = qh * eb, qh * ebm, kh * emb, kh * etb
            att = m * _dot(qm, km, 1, 1)
            ds_ref[h * DV:(h + 1) * DV, :] = ds1 * etot + _dot(doh, qe, 0, 0)
            dqe = _dot(doh, s0, 1, 0)
            datt = m * _dot(doh, vh, 1, 1)
            dqm = _dot(datt, km, 1, 0)
            dkm = _dot(datt, qm, 0, 0)
            dvh = _dot(att, doh, 0, 0) + _dot(kl, ds1, 1, 1)
            dkl = _dot(vh, ds1, 1, 0)
            dtot = etot * jnp.sum(ds1 * s0, axis=0, keepdims=True) + jnp.sum(dkl * kl, axis=0, keepdims=True)
            db = dqe * qe + dqm * qm - dkm * km - dkl * kl
            dgh = lax.dot_general(m, db, (((0,), (0,)), ((), ())), precision=HIGHEST,
                                  preferred_element_type=F32) + dtot
            dq_ref[0, :, h * DK:(h + 1) * DK] = (dqe * eb + dqm * ebm) * Q_SCALE
            dk_ref[0, :, h * DK:(h + 1) * DK] = dkm * emb + dkl * etb
            dv_ref[0, :, h * DV:(h + 1) * DV] = dvh
            dg_ref[0, :, h * DK:(h + 1) * DK] = dgh

    nt = p.shape[0]
    row3 = lambda d, b, s: (d, rowblk(d, b, rev(s)), 0)
    return pl.pallas_call(
        kern, grid=(2, bl, ns),
        in_specs=[pl.BlockSpec((CHUNK, 512), lambda d, b, s: (rowblk(d, b, rev(s)), 4)),
                  pl.BlockSpec((CHUNK, 512), lambda d, b, s: (rowblk(d, b, rev(s)), 5)),
                  pl.BlockSpec((CHUNK, 1024), lambda d, b, s: (rowblk(d, b, rev(s)), 3)),
                  pl.BlockSpec((1, CHUNK, 512), row3),
                  pl.BlockSpec((1, CHUNK, CHUNK), lambda d, b, s: (d, 0, 0)),
                  pl.BlockSpec((1, 1, 1, HEADS * DV, DK), lambda d, b, s: (d, b, rev(s), 0, 0)),
                  pl.BlockSpec((CHUNK, 1024), lambda d, b, s: (xblk(d, b, rev(s)), 0))],
        out_specs=[pl.BlockSpec((1, CHUNK, 512), row3), pl.BlockSpec((1, CHUNK, 512), row3),
                   pl.BlockSpec((1, CHUNK, 1024), row3), pl.BlockSpec((1, CHUNK, 512), row3)],
        out_shape=[jax.ShapeDtypeStruct((2, nt, 512), F32), jax.ShapeDtypeStruct((2, nt, 512), F32),
                   jax.ShapeDtypeStruct((2, nt, 1024), F32), jax.ShapeDtypeStruct((2, nt, 512), F32)],
        scratch_shapes=[pltpu.VMEM((HEADS * DV, DK), F32)],
        compiler_params=_cparams(("parallel", "parallel", "arbitrary")), name="gla_bwd")(
            p, p, p, g2, mmats, ssave, do)


CONV_CT = 256
CONV_PAD = 16
CONV_RC = 128


def _conv_fill(zp, z_ref, t):
    zp[0:CONV_PAD, :] = jnp.zeros((CONV_PAD, CONV_CT), F32)
    zp[CONV_PAD + t:2 * CONV_PAD + t, :] = jnp.zeros((CONV_PAD, CONV_CT), F32)
    zp[CONV_PAD:CONV_PAD + t, :] = z_ref[...]


def _dwconv(name, z, w, bias, bl, t, flip):
    def kern(z_ref, w_ref, b_ref, o_ref, zp):
        _conv_fill(zp, z_ref, t)
        for r in range(0, t, CONV_RC):
            acc = jnp.broadcast_to(b_ref[...], (CONV_RC, CONV_CT))
            for j in range(CONV_W):
                off = (CONV_W - j) if flip else (j + 1)
                acc = acc + w_ref[j:j + 1, :] * zp[r + off:r + off + CONV_RC, :]
            o_ref[r:r + CONV_RC, :] = acc

    return pl.pallas_call(
        kern, grid=(bl, 1024 // CONV_CT),
        in_specs=[pl.BlockSpec((t, CONV_CT), lambda b, c: (b, c)),
                  pl.BlockSpec((32, CONV_CT), lambda b, c: (0, c)),
                  pl.BlockSpec((1, CONV_CT), lambda b, c: (0, c))],
        out_specs=pl.BlockSpec((t, CONV_CT), lambda b, c: (b, c)),
        out_shape=jax.ShapeDtypeStruct(z.shape, F32),
        scratch_shapes=[pltpu.VMEM((t + 2 * CONV_PAD, CONV_CT), F32)],
        compiler_params=_cparams(("parallel", "parallel")), name=name)(z, w, bias)


def _dwconv_wgrad(z, dzc, bl, t):
    def kern(z_ref, d_ref, dw_ref, db_ref, zp):
        b = pl.program_id(1)

        @pl.when(b == 0)
        def _():
            dw_ref[...] = jnp.zeros(dw_ref.shape, F32)
            db_ref[...] = jnp.zeros(db_ref.shape, F32)
        _conv_fill(zp, z_ref, t)
        for j in range(CONV_W):
            acc = jnp.zeros((CONV_RC, CONV_CT), F32)
            for r in range(0, t, CONV_RC):
                acc = acc + d_ref[r:r + CONV_RC, :] * zp[r + j + 1:r + j + 1 + CONV_RC, :]
            dw_ref[j:j + 1, :] += jnp.sum(acc, axis=0, keepdims=True)
        db_ref[...] += jnp.sum(d_ref[...], axis=0, keepdims=True)

    return pl.pallas_call(
        kern, grid=(1024 // CONV_CT, bl),
        in_specs=[pl.BlockSpec((t, CONV_CT), lambda c, b: (b, c)),
                  pl.BlockSpec((t, CONV_CT), lambda c, b: (b, c))],
        out_specs=[pl.BlockSpec((32, CONV_CT), lambda c, b: (0, c)),
                   pl.BlockSpec((1, CONV_CT), lambda c, b: (0, c))],
        out_shape=[jax.ShapeDtypeStruct((32, 1024), F32), jax.ShapeDtypeStruct((1, 1024), F32)],
        scratch_shapes=[pltpu.VMEM((t + 2 * CONV_PAD, CONV_CT), F32)],
        compiler_params=_cparams(("parallel", "arbitrary")), name="dwconv_wgrad")(z, dzc)


def _ffn_fwd(tag, xin, n_tiles, g, sh, sc, gate, w_gu, w_down, tpb, nb):
    rows = n_tiles * TM
    rm = functools.partial(_rowmap, tpb=tpb, nb=nb)
    (u,) = rm(tag + "_norm", lambda i, x, g_, sh_, sc_: (_modnorm(x, g_, sh_, sc_),), n_tiles,
              [tok(xin), const(g), mod(sh), mod(sc)], [("tok", rows, D, BF16)])
    ab = _mm(tag + "_gu", u, w_gu)
    (hm,) = rm(tag + "_act", lambda i, a, b: (_silu(a) * b,), n_tiles,
               [tok(ab, FF, 0), tok(ab, FF, 1)], [("tok", rows, FF, BF16)])
    f = _mm(tag + "_down", hm, w_down)
    (xout,) = rm(tag + "_res", lambda i, x, f_, gt: (x + 0.5 * gt * f_,), n_tiles,
                 [tok(xin), tok(f), mod(gate)], [("tok", rows, D, F32)])
    return xout, (u, ab, hm, f)


def _ffn_bwd(tag, xin, saved, dxout, dx_clamp, n_tiles, g, sh, sc, gate, w_gu, w_down, tpb, nb):
    u, ab, hm, f = saved
    rows = n_tiles * TM
    rm = functools.partial(_rowmap, tpb=tpb, nb=nb)

    def mask(i):
        return 1.0 if dx_clamp is None else (i <= dx_clamp).astype(F32)

    def b1(i, dx, f_, gt):
        dx = dx * mask(i)
        return (0.5 * gt * dx, jnp.sum(0.5 * f_ * dx, axis=0, keepdims=True))
    df, dgate = rm(tag + "_bres", b1, n_tiles, [tok(dxout, clamp=dx_clamp), tok(f), mod(gate)],
                   [("tok", rows, D, BF16), ("modacc", D)])
    dhm = _mm(tag + "_bdown", df, w_down, trans_b=True)
    dw_down = _mm_tn(tag + "_wdown", hm, df)

    def b2(i, dh, a, b):
        s = jax.nn.sigmoid(a)
        return (jnp.concatenate([dh * b * (s * (1 + a * (1 - s))), dh * (a * s)], axis=1),)
    (dab,) = rm(tag + "_bact", b2, n_tiles, [tok(dhm), tok(ab, FF, 0), tok(ab, FF, 1)],
                [("tok", rows, 2 * FF, BF16)])
    du = _mm(tag + "_bgu", dab, w_gu, trans_b=True)
    dw_gu = _mm_tn(tag + "_wgu", u, dab)

    def b3(i, x, g_, sh_, sc_, du_, dx):
        _, vjp = jax.vjp(_modnorm, x, g_, sh_, sc_)
        dxn, dg, dsh, dsc = vjp(du_)
        return (dx * mask(i) + dxn, dg, dsh, dsc)
    dxin, dg, dsh, dsc = rm(tag + "_bnorm", b3, n_tiles,
                            [tok(xin), const(g), mod(sh), mod(sc), tok(du), tok(dxout, clamp=dx_clamp)],
                            [("tok", rows, D, F32), ("acc", 1, D), ("modacc", D), ("modacc", D)])
    return dxin, dict(g=dg, sh=dsh, sc=dsc, gate=dgate, w_gu=dw_gu, w_down=dw_down)


def _perm_in_cols(w):
    pad = jnp.zeros(w.shape[:-1] + (D_INP - D_IN,), w.dtype)
    return jnp.concatenate([w[..., :5120], w[..., 5152:7200], w[..., 5120:5152], pad], axis=-1)


def _unperm_in_cols(w):
    return jnp.concatenate([w[..., :5120], w[..., LR_COL:LR_COL + 32], w[..., 5120:LR_COL]], axis=-1)


def _local_step(x, c, ctx, target, wts):
    bl, t, _ = x.shape
    tc = ctx.shape[1]
    nx_rows, nc_rows = bl * t, bl * tc
    nt_rows = nx_rows + nc_rows
    tpb = t // TM
    nxt, ntt = nx_rows // TM, nt_rows // TM
    nb = bl
    rm = functools.partial(_rowmap, tpb=tpb, nb=nb)
    last_x = nxt - 1

    x0 = jnp.concatenate([x.reshape(nx_rows, D), ctx.reshape(nc_rows, D)], axis=0)
    tgt = target.reshape(nx_rows, D)

    cc = jnp.concatenate([c, wts["c_ctx"].reshape(1, D), jnp.zeros((8 - bl - 1, D), F32)], axis=0)
    modv = _mm("mod_fwd", cc, wts["w_mod"], a_fn=_silu, bias=wts["b_mod"])
    mods = [modv[:nb + 1, k * D:(k + 1) * D].reshape(nb + 1, 1, D) for k in range(9)]

    x1, sv1 = _ffn_fwd("ffn1", x0, ntt, wts["g_ffn1"], mods[0], mods[1], mods[2], wts["w1_gu"], wts["w1_down"],
                       tpb, nb)
    (u2,) = rm("in_norm", lambda i, x_, g_, sh_, sc_: (_modnorm(x_, g_, sh_, sc_),), ntt,
               [tok(x1), const(wts["g_mix"]), mod(mods[3]), mod(mods[4])], [("tok", nt_rows, D, BF16)])
    w_inp = wts["w_in_p"]
    p = _mm("in_proj", u2, w_inp)

    waf, wab, baf, bab = wts["w_alpha_f_pad"], wts["w_alpha_b_pad"], wts["b_alpha_f"], wts["b_alpha_b"]

    def dec_fwd(i, lr, wf, wb, bf_, bb_):
        zf = _dot(lr, wf, 1, 0) + bf_
        zb = _dot(lr, wb, 1, 0) + bb_
        return (jnp.concatenate([jax.nn.log_sigmoid(zf) / TAU, jax.nn.log_sigmoid(zb) / TAU], axis=1),)
    (gfb,) = rm("decay_fwd", dec_fwd, ntt, [tok(p, 128, LR_COL // 128), const(waf), const(wab), const(baf), const(bab)],
                [("tok", nt_rows, 1024, F32)])
    g2 = jnp.stack([gfb[:, :512], gfb[:, 512:]])
    tri = jnp.tril(jnp.ones((CHUNK, CHUNK), F32))
    mmats = jnp.stack([tri, tri.T])
    o2, ssave = _gla_fwd(p, g2, mmats, bl, t, tc)

    gn_g = wts["gla_norm_g"]

    def gla_out(of, ob, og, gn):
        o = of + ob
        parts = []
        for h in range(HEADS):
            oh = o[:, h * DV:(h + 1) * DV]
            parts.append(oh * lax.rsqrt(jnp.mean(oh * oh, axis=-1, keepdims=True) + EPS))
        return jnp.concatenate(parts, axis=1) * gn * _silu(og)
    (yg_in,) = rm("gla_out", lambda i, of, ob, og, gn: (gla_out(of, ob, og, gn),), nxt,
                  [tok(o2[0]), tok(o2[1]), tok(p, 1024, 4), const(gn_g)], [("tok", nx_rows, D, BF16)])
    y_gla = _mm("gla_proj", yg_in, wts["w_gla_out"])

    (z,) = rm("glu", lambda i, a, b: (a * jax.nn.sigmoid(b),), nxt, [tok(p, 1024, 0), tok(p, 1024, 1)],
              [("tok", nx_rows, D, F32)])
    dw_w = jnp.concatenate([wts["dw_weight"], jnp.zeros((1, D), F32)], axis=0)
    zc = _dwconv("dwconv_fwd", z, dw_w, wts["dw_bias"], bl, t, False)

    def ln_silu(zc_, g_, b_):
        mu = jnp.mean(zc_, axis=-1, keepdims=True)
        var = jnp.mean(jnp.square(zc_ - mu), axis=-1, keepdims=True)
        return _silu((zc_ - mu) * lax.rsqrt(var + EPS) * g_ + b_)
    ln_g, ln_b = wts["conv_ln_g"], wts["conv_ln_b"]
    (zl,) = rm("conv_ln", lambda i, zc_, g_, b_: (ln_silu(zc_, g_, b_),), nxt, [tok(zc), const(ln_g), const(ln_b)],
               [("tok", nx_rows, D, BF16)])
    y_conv = _mm("conv_proj", zl, wts["w_conv_out"])

    (mg,) = rm("merge", lambda i, ga, gb, yc, yg: (jax.nn.sigmoid(ga) * yc + jax.nn.sigmoid(gb) * yg,), nxt,
               [tok(p, 1024, 5), tok(p, 1024, 6), tok(y_conv), tok(y_gla)], [("tok", nx_rows, D, BF16)])
    mix = _mm("out_proj", mg, wts["w_out"])
    (x2,) = rm("mix_res", lambda i, x_, mx_, gt: (x_ + gt * mx_,), nxt, [tok(x1), tok(mix), mod(mods[5])],
               [("tok", nx_rows, D, F32)])

    x3, sv2 = _ffn_fwd("ffn2", x2, nxt, wts["g_ffn2"], mods[6], mods[7], mods[8], wts["w2_gu"], wts["w2_down"],
                       tpb, nb)
    g_fin = wts["g_final"].reshape(1, D)

    def head(i, x_, g_, tg):
        y, vjp = jax.vjp(_rms, x_, g_)
        diff = y - tg
        dx, dg = vjp(diff * (1.0 / D))
        loss = 0.5 * jnp.sum(jnp.mean(diff * diff, axis=-1, keepdims=True))
        return dx, dg, loss
    dx3, dg_final, loss_acc = rm("loss_head", head, nxt, [tok(x3), const(g_fin), tok(tgt)],
                                 [("tok", nx_rows, D, F32), ("acc", 1, D), ("acc", 8, 128)])
    loss = loss_acc[0, 0]

    grads = {}
    dx2, gf2 = _ffn_bwd("ffn2", x2, sv2, dx3, None, nxt, wts["g_ffn2"], mods[6], mods[7], mods[8],
                        wts["w2_gu"], wts["w2_down"], tpb, nb)
    grads.update(g_ffn2=gf2["g"], w2_gu=gf2["w_gu"], w2_down=gf2["w_down"])

    dmix, dgate5 = rm("mix_bres", lambda i, dx, mx_, gt: (gt * dx, jnp.sum(mx_ * dx, axis=0, keepdims=True)), nxt,
                      [tok(dx2), tok(mix), mod(mods[5])], [("tok", nx_rows, D, BF16), ("modacc", D)])
    dmg = _mm("out_bproj", dmix, wts["w_out"], trans_b=True)
    grads["w_out"] = _mm_tn("out_wgrad", mg, dmix)

    def merge_bwd(i, dm, ga, gb, yc, yg):
        keep = (i <= last_x).astype(F32)
        dm = dm * keep
        sa, sb = jax.nn.sigmoid(ga), jax.nn.sigmoid(gb)
        return dm * sa, dm * sb, dm * yc * sa * (1 - sa), dm * yg * sb * (1 - sb)
    cl = dict(clamp=last_x)
    dyc, dyg, dga, dgb = rm("merge_bwd", merge_bwd, ntt,
                            [tok(dmg, **cl), tok(p, 1024, 5, last_x), tok(p, 1024, 6, last_x), tok(y_conv, **cl),
                             tok(y_gla, **cl)],
                            [("tok", nt_rows, D, BF16)] * 4)

    dzl = _mm("conv_bproj", dyc, wts["w_conv_out"], trans_b=True, rows=nx_rows)
    grads["w_conv_out"] = _mm_tn("conv_wgrad", zl, dyc, rows=nx_rows)

    def ln_bwd(i, zc_, g_, b_, dz_):
        _, vjp = jax.vjp(ln_silu, zc_, g_, b_)
        return vjp(dz_)
    dzc, dln_g, dln_b = rm("conv_ln_bwd", ln_bwd, nxt, [tok(zc), const(ln_g), const(ln_b), tok(dzl)],
                           [("tok", nx_rows, D, F32), ("acc", 1, D), ("acc", 1, D)])
    dz = _dwconv("dwconv_bwd", dzc, dw_w, jnp.zeros((1, D), F32), bl, t, True)
    ddw, ddb = _dwconv_wgrad(z, dzc, bl, t)
    grads.update(conv_ln_g=dln_g, conv_ln_b=dln_b, dw_weight=ddw[:CONV_W], dw_bias=ddb)

    def glu_bwd(i, dz_, a, b):
        keep = (i <= last_x).astype(F32)
        dz_ = dz_ * keep
        s = jax.nn.sigmoid(b)
        return (jnp.concatenate([dz_ * s, dz_ * a * s * (1 - s)], axis=1),)
    (dconv,) = rm("glu_bwd", glu_bwd, ntt, [tok(dz, **cl), tok(p, 1024, 0, last_x), tok(p, 1024, 1, last_x)],
                  [("tok", nt_rows, 2048, BF16)])

    dyg_in = _mm("gla_bproj", dyg, wts["w_gla_out"], trans_b=True, rows=nx_rows)
    grads["w_gla_out"] = _mm_tn("gla_wgrad", yg_in, dyg, rows=nx_rows)

    def gla_out_bwd(i, of, ob, og, gn, dy):
        _, vjp = jax.vjp(gla_out, of, ob, og, gn)
        do_, _, dog_, dgn_ = vjp(dy)
        return do_, dog_, dgn_
    do, dog_x, dgn = rm("gla_out_bwd", gla_out_bwd, nxt,
                        [tok(o2[0]), tok(o2[1]), tok(p, 1024, 4), const(gn_g), tok(dyg_in)],
                        [("tok", nx_rows, D, F32), ("tok", nx_rows, D, BF16), ("acc", 1, D)])
    grads["gla_norm_g"] = dgn
    dog = jnp.concatenate([dog_x, jnp.zeros((nc_rows, D), BF16)], axis=0)

    dq2, dk2, dv2, dg2 = _gla_bwd(p, g2, mmats, ssave, do, bl, t, tc)

    def dec_bwd(i, lr, wf, wb, bf_, bb_, dgf, dgb_):
        zf = _dot(lr, wf, 1, 0) + bf_
        zb = _dot(lr, wb, 1, 0) + bb_
        dzf = dgf * (1 - jax.nn.sigmoid(zf)) * (1.0 / TAU)
        dzb = dgb_ * (1 - jax.nn.sigmoid(zb)) * (1.0 / TAU)
        dlr = _dot(dzf, wf, 1, 1) + _dot(dzb, wb, 1, 1)
        return (dlr, _dot(lr, dzf, 0, 0), _dot(lr, dzb, 0, 0), jnp.sum(dzf, axis=0, keepdims=True),
                jnp.sum(dzb, axis=0, keepdims=True))
    dlr, dwaf, dwab, dbaf, dbab = rm(
        "decay_bwd", dec_bwd, ntt,
        [tok(p, 128, LR_COL // 128), const(waf), const(wab), const(baf), const(bab), tok(dg2[0]), tok(dg2[1])],
        [("tok", nt_rows, 128, BF16), ("acc", 128, 512), ("acc", 128, 512), ("acc", 1, 512), ("acc", 1, 512)])
    grads.update(w_alpha_f=dwaf[:LOWRANK], w_alpha_b=dwab[LOWRANK:2 * LOWRANK], b_alpha_f=dbaf, b_alpha_b=dbab)

    dq, dk, dv = rm("gla_sum", lambda i, q0, q1, k0, k1, v0, v1: (q0 + q1, k0 + k1, v0 + v1), ntt,
                    [tok(dq2[0]), tok(dq2[1]), tok(dk2[0]), tok(dk2[1]), tok(dv2[0]), tok(dv2[1])],
                    [("tok", nt_rows, 512, BF16), ("tok", nt_rows, 512, BF16), ("tok", nt_rows, 1024, BF16)])

    dp = jnp.concatenate([dconv, dq, dk, dv, dog, dga, dgb, dlr], axis=1)
    du2 = _mm("in_bproj", dp, w_inp, trans_b=True)
    grads["w_in"] = _unperm_in_cols(_mm_tn("in_wgrad", u2, dp))

    def in_norm_bwd(i, x_, g_, sh_, sc_, du_, dx):
        keep = (i <= last_x).astype(F32)
        _, vjp = jax.vjp(_modnorm, x_, g_, sh_, sc_)
        dxn, dg, dsh, dsc = vjp(du_)
        return (dx * keep + dxn, dg, dsh, dsc)
    dx1, dg_mix, dsh3, dsc4 = rm("in_norm_bwd", in_norm_bwd, ntt,
                                 [tok(x1), const(wts["g_mix"]), mod(mods[3]), mod(mods[4]), tok(du2), tok(dx2, **cl)],
                                 [("tok", nt_rows, D, F32), ("acc", 1, D), ("modacc", D), ("modacc", D)])
    grads["g_mix"] = dg_mix

    dx0, gf1 = _ffn_bwd("ffn1", x0, sv1, dx1, None, ntt, wts["g_ffn1"], mods[0], mods[1], mods[2],
                        wts["w1_gu"], wts["w1_down"], tpb, nb)
    grads.update(g_ffn1=gf1["g"], w1_gu=gf1["w_gu"], w1_down=gf1["w_down"])
    grad_x = dx0[:nx_rows].reshape(bl, t, D)

    dmods = [gf1["sh"], gf1["sc"], gf1["gate"], dsh3, dsc4, dgate5, gf2["sh"], gf2["sc"], gf2["gate"]]
    dmod = jnp.concatenate(
        [jnp.concatenate([a.reshape(a.shape[0], D), jnp.zeros((8 - a.shape[0], D), F32)], axis=0) for a in dmods],
        axis=1)
    dsc_ = _mm("mod_bproj", dmod, wts["w_mod"], trans_b=True)

    def silu_bwd(cc_ref, d_ref, dm_ref, dcc_ref, scc_ref, db_ref):
        cc_ = cc_ref[...]
        s = jax.nn.sigmoid(cc_)
        dcc_ref[...] = d_ref[...] * (s * (1 + cc_ * (1 - s)))
        scc_ref[...] = (cc_ * s).astype(BF16)
        db_ref[...] = jnp.sum(dm_ref[...], axis=0, keepdims=True)
    dcc, scc, db_mod = pl.pallas_call(
        silu_bwd, out_shape=[jax.ShapeDtypeStruct((8, D), F32), jax.ShapeDtypeStruct((8, D), BF16),
                             jax.ShapeDtypeStruct((1, 9 * D), F32)], name="mod_silu_bwd")(cc, dsc_, dmod)
    grads["c_ctx"] = dcc[nb]
    grads["b_mod"] = db_mod
    grads["w_mod"] = _mm_tn("mod_wgrad", scc, dmod)
    grads["g_final"] = dg_final.reshape(D)
    return loss, grad_x, grads


ANY = pl.BlockSpec(memory_space=pl.ANY)


def _place():
    x, y, c = lax.axis_index("x"), lax.axis_index("y"), lax.axis_index("c")
    chips = [(1 - x, y), (x, 1 - y), (1 - x, 1 - y)]
    return x, y, c, chips


def _gather_weights(piece):
    rows = piece.shape[0]
    hr = rows // 2

    def body(x_ref, out_ref, send_sems, recv_sems, local_sem):
        x, y, c, chips = _place()
        me = 2 * x + y
        sibling = (x, y, 1 - c)

        def half(chip, hc):
            return out_ref.at[chip, pl.ds(hc * hr, hr), :]

        def copy(k, src, dst, to):
            return pltpu.make_async_remote_copy(src_ref=src, dst_ref=dst, send_sem=send_sems.at[k],
                                                recv_sem=recv_sems.at[k], device_id=to, device_id_type=MESH)

        mine = pltpu.make_async_copy(x_ref, out_ref.at[me], local_sem)
        mine.start()
        first = [copy(j, x_ref.at[pl.ds(c * hr, hr), :], half(me, c), (px, py, c)) for j, (px, py) in enumerate(chips)]
        for cp in first:
            cp.start()
        passed = []
        for j, (px, py) in enumerate(chips):
            src = 2 * px + py
            copy(j, half(src, c), half(src, c), (px, py, c)).wait_recv()
            fw = copy(3 + j, half(src, c), half(src, c), sibling)
            fw.start()
            passed.append(fw)
        for j, (px, py) in enumerate(chips):
            src = 2 * px + py
            copy(3 + j, half(src, 1 - c), half(src, 1 - c), sibling).wait_recv()
        for cp in first + passed:
            cp.wait_send()
        mine.wait()

    return pl.pallas_call(
        body, out_shape=jax.ShapeDtypeStruct((4,) + piece.shape, piece.dtype), in_specs=[ANY], out_specs=ANY,
        scratch_shapes=[pltpu.SemaphoreType.DMA((6,)), pltpu.SemaphoreType.DMA((6,)), pltpu.SemaphoreType.DMA(())],
        name="gather_weights")(piece)


def _swap_halves(g):
    rows = g.shape[1]
    hr = rows // 2

    def body(g_ref, r_ref, send_sem, recv_sem):
        x, y, c, _ = _place()
        cp = pltpu.make_async_remote_copy(src_ref=g_ref.at[:, pl.ds((1 - c) * hr, hr), :], dst_ref=r_ref,
                                          send_sem=send_sem, recv_sem=recv_sem, device_id=(x, y, 1 - c),
                                          device_id_type=MESH)
        cp.start()
        cp.wait()

    return pl.pallas_call(
        body, out_shape=jax.ShapeDtypeStruct((4, hr, g.shape[2]), g.dtype), in_specs=[ANY], out_specs=ANY,
        scratch_shapes=[pltpu.SemaphoreType.DMA(()), pltpu.SemaphoreType.DMA(())], name="grad_swap_halves")(g)


def _add_halves(g, r, cidx):
    hr = r.shape[1]
    tr = 256
    nblk = hr // tr

    def kern(c_ref, g_ref, r_ref, o_ref):
        o_ref[...] = (g_ref[...].astype(F32) + r_ref[...].astype(F32)).astype(o_ref.dtype)

    blk = (1, tr, g.shape[2])
    return pl.pallas_call(
        kern,
        grid_spec=pltpu.PrefetchScalarGridSpec(
            num_scalar_prefetch=1, grid=(4, nblk),
            in_specs=[pl.BlockSpec(blk, lambda j, i, c: (j, c[0] * nblk + i, 0)),
                      pl.BlockSpec(blk, lambda j, i, c: (j, i, 0))],
            out_specs=pl.BlockSpec(blk, lambda j, i, c: (j, i, 0))),
        out_shape=jax.ShapeDtypeStruct(r.shape, BF16),
        compiler_params=_cparams(("parallel", "parallel")), name="grad_add_halves")(cidx, g, r)


def _scatter_chips(cs):
    def body(c_ref, r_ref, send_sems, recv_sems, local_sem):
        x, y, c, chips = _place()
        me = 2 * x + y

        def copy(k, src, dst, to):
            return pltpu.make_async_remote_copy(src_ref=src, dst_ref=dst, send_sem=send_sems.at[k],
                                                recv_sem=recv_sems.at[k], device_id=to, device_id_type=MESH)

        mine = pltpu.make_async_copy(c_ref.at[me], r_ref.at[me], local_sem)
        mine.start()
        sends = [copy(j, c_ref.at[2 * px + py], r_ref.at[me], (px, py, c)) for j, (px, py) in enumerate(chips)]
        for cp in sends:
            cp.start()
        for j, (px, py) in enumerate(chips):
            src = 2 * px + py
            copy(j, c_ref.at[src], r_ref.at[src], (px, py, c)).wait_recv()
        for cp in sends:
            cp.wait_send()
        mine.wait()

    return pl.pallas_call(
        body, out_shape=jax.ShapeDtypeStruct(cs.shape, cs.dtype), in_specs=[ANY], out_specs=ANY,
        scratch_shapes=[pltpu.SemaphoreType.DMA((3,)), pltpu.SemaphoreType.DMA((3,)), pltpu.SemaphoreType.DMA(())],
        name="grad_scatter_chips")(cs)


def _sum_chips(r):
    hr = r.shape[1]
    tr = 256

    def kern(r_ref, o_ref):
        acc = r_ref[0].astype(F32)
        for k in range(1, 4):
            acc = acc + r_ref[k].astype(F32)
        o_ref[...] = acc

    return pl.pallas_call(
        kern, grid=(hr // tr,), in_specs=[pl.BlockSpec((4, tr, r.shape[2]), lambda i: (0, i, 0))],
        out_specs=pl.BlockSpec((tr, r.shape[2]), lambda i: (i, 0)),
        out_shape=jax.ShapeDtypeStruct((hr, r.shape[2]), F32),
        compiler_params=_cparams(("parallel",)), name="grad_sum_chips")(r)


def _join_halves(f):
    hr = f.shape[0]

    def body(f_ref, out_ref, send_sem, recv_sem, local_sem):
        x, y, c, _ = _place()
        mine = pltpu.make_async_copy(f_ref, out_ref.at[pl.ds(c * hr, hr), :], local_sem)
        mine.start()
        cp = pltpu.make_async_remote_copy(src_ref=f_ref, dst_ref=out_ref.at[pl.ds(c * hr, hr), :], send_sem=send_sem,
                                          recv_sem=recv_sem, device_id=(x, y, 1 - c), device_id_type=MESH)
        cp.start()
        pltpu.make_async_remote_copy(src_ref=f_ref, dst_ref=out_ref.at[pl.ds((1 - c) * hr, hr), :], send_sem=send_sem,
                                     recv_sem=recv_sem, device_id=(x, y, 1 - c), device_id_type=MESH).wait_recv()
        cp.wait_send()
        mine.wait()

    return pl.pallas_call(
        body, out_shape=jax.ShapeDtypeStruct((2 * hr, f.shape[1]), f.dtype), in_specs=[ANY], out_specs=ANY,
        scratch_shapes=[pltpu.SemaphoreType.DMA(()), pltpu.SemaphoreType.DMA(()), pltpu.SemaphoreType.DMA(())],
        name="grad_join_halves")(f)


def _allreduce_small(v):
    def body(x_ref, out_ref, gath, send_sems, recv_sems, local_sem):
        x, y, c, chips = _place()
        me, sibling = (x, y, c), (x, y, 1 - c)

        def slot(px, py, pc):
            return gath.at[4 * px + 2 * py + pc]

        def copy(k, block, to, src=None):
            return pltpu.make_async_remote_copy(
                src_ref=slot(*block) if src is None else src, dst_ref=slot(*block), send_sem=send_sems.at[k],
                recv_sem=recv_sems.at[k], device_id=to, device_id_type=MESH)

        mine = pltpu.make_async_copy(x_ref, slot(*me), local_sem)
        mine.start()
        first = [copy(0, me, sibling, src=x_ref)]
        first += [copy(1 + j, me, (*chip, c), src=x_ref) for j, chip in enumerate(chips)]
        for cp in first:
            cp.start()
        passed = [copy(4 + j, (*chip, c), sibling) for j, chip in enumerate(chips)]
        for j, chip in enumerate(chips):
            copy(1 + j, (*chip, c), me).wait_recv()
            passed[j].start()
        copy(0, sibling, me).wait_recv()
        for j, chip in enumerate(chips):
            copy(4 + j, (*chip, 1 - c), me).wait_recv()
        for cp in first + passed:
            cp.wait_send()
        mine.wait()
        acc = gath[0]
        for k in range(1, 8):
            acc = acc + gath[k]
        out_ref[...] = acc

    vm = pl.BlockSpec(memory_space=pltpu.VMEM)
    return pl.pallas_call(
        body, out_shape=jax.ShapeDtypeStruct(v.shape, F32), in_specs=[vm], out_specs=vm,
        scratch_shapes=[pltpu.VMEM((8,) + v.shape, F32), pltpu.SemaphoreType.DMA((7,)),
                        pltpu.SemaphoreType.DMA((7,)), pltpu.SemaphoreType.DMA(())],
        name="allreduce_small")(v)


def _adamw(name, w, g, m, v):
    r, cols = w.shape
    budget = 262144
    tr = r if r * cols <= budget else next(c for c in (256, 128, 64, 32, 16, 8) if r % c == 0 and c * cols <= budget)

    def kern(w_ref, g_ref, m_ref, v_ref, d_ref, nm_ref, nv_ref):
        gv = g_ref[...]
        nm = ADAM_B1 * m_ref[...] + (1.0 - ADAM_B1) * gv
        nv = ADAM_B2 * v_ref[...] + (1.0 - ADAM_B2) * jnp.square(gv)
        m_hat = nm / (1.0 - ADAM_B1 ** ADAM_STEP)
        v_hat = nv / (1.0 - ADAM_B2 ** ADAM_STEP)
        d_ref[...] = -ADAM_LR * (m_hat / (jnp.sqrt(v_hat) + ADAM_EPS) + ADAM_WD * w_ref[...])
        nm_ref[...] = nm
        nv_ref[...] = nv

    spec = pl.BlockSpec((tr, cols), lambda i: (i, 0))
    shp = jax.ShapeDtypeStruct((r, cols), F32)
    return pl.pallas_call(kern, grid=(r // tr,), in_specs=[spec] * 4, out_specs=[spec] * 3, out_shape=[shp] * 3,
                          compiler_params=_cparams(("parallel",)), name=name)(w, g, m, v)


SHARDED = (("w_mod", 1), ("w1_gu", 1), ("w1_down", 0), ("w_in", 1), ("dw_weight", 1), ("w_conv_out", 0),
           ("w_alpha_f", 1), ("w_alpha_b", 1), ("w_gla_out", 0), ("w_out", 0), ("w2_gu", 1), ("w2_down", 0))
REPLICATED = ("c_ctx", "b_mod", "g_ffn1", "g_mix", "dw_bias", "conv_ln_g", "conv_ln_b", "b_alpha_f", "b_alpha_b",
              "gla_norm_g", "g_ffn2", "g_final")
WEIGHTS = ("c_ctx", "w_mod", "b_mod", "g_ffn1", "w1_gu", "w1_down", "g_mix", "w_in", "dw_weight", "dw_bias",
           "conv_ln_g", "conv_ln_b", "w_conv_out", "w_alpha_f", "b_alpha_f", "w_alpha_b", "b_alpha_b", "gla_norm_g",
           "w_gla_out", "w_out", "g_ffn2", "w2_gu", "w2_down", "g_final")
PIECE_ALIGN = 512 * 1024


def _pack_flat(parts, align):
    flat = jnp.concatenate([p.reshape(-1) for p in parts])
    pad = (-flat.shape[0]) % align
    return jnp.concatenate([flat, jnp.zeros((pad,), flat.dtype)]).reshape(-1, 1024)


def _unpack_flat(flat2d, shapes):
    flat = flat2d.reshape(-1)
    out, off = [], 0
    for s in shapes:
        n = math.prod(s)
        out.append(flat[off:off + n].reshape(s))
        off += n
    return out


def kernel(x, c, ctx, c_ctx, w_mod, b_mod, g_ffn1, w1_gu, w1_down, g_mix, w_in, dw_weight, dw_bias, conv_ln_g, conv_ln_b, w_conv_out, w_alpha_f, b_alpha_f, w_alpha_b, b_alpha_b, gla_norm_g, w_gla_out, w_out, g_ffn2, w2_gu, w2_down, g_final, loss_target, m_c_ctx, m_w_mod, m_b_mod, m_g_ffn1, m_w1_gu, m_w1_down, m_g_mix, m_w_in, m_dw_weight, m_dw_bias, m_conv_ln_g, m_conv_ln_b, m_w_conv_out, m_w_alpha_f, m_b_alpha_f, m_w_alpha_b, m_b_alpha_b, m_gla_norm_g, m_w_gla_out, m_w_out, m_g_ffn2, m_w2_gu, m_w2_down, m_g_final, v_c_ctx, v_w_mod, v_b_mod, v_g_ffn1, v_w1_gu, v_w1_down, v_g_mix, v_w_in, v_dw_weight, v_dw_bias, v_conv_ln_g, v_conv_ln_b, v_w_conv_out, v_w_alpha_f, v_b_alpha_f, v_w_alpha_b, v_b_alpha_b, v_gla_norm_g, v_w_gla_out, v_w_out, v_g_ffn2, v_w2_gu, v_w2_down, v_g_final):
    given = dict(locals())
    w = {n: given[n] for n in WEIGHTS}
    m = {n: given["m_" + n] for n in WEIGHTS}
    v = {n: given["v_" + n] for n in WEIGHTS}

    shard_shapes = [w[n].shape[1:] for n, _ in SHARDED]
    sent_shapes = [s + (2,) if n == "dw_weight" else s for s, (n, _) in zip(shard_shapes, SHARDED)]
    piece = _pack_flat([lax.bitcast_convert_type(w[n], BF16) if n == "dw_weight" else w[n].astype(BF16)
                        for n, _ in SHARDED], PIECE_ALIGN)
    gathered = _gather_weights(piece)
    per_chip = [_unpack_flat(gathered[j], sent_shapes) for j in range(4)]
    full = {}
    for i, (n, ax) in enumerate(SHARDED):
        parts = [per_chip[j][i] for j in range(4)]
        if n == "dw_weight":
            parts = [lax.bitcast_convert_type(q, F32) for q in parts]
        full[n] = jnp.concatenate(parts, axis=ax)

    wts = {n: w[n] for n in REPLICATED}
    for n in ("w_mod", "w1_gu", "w1_down", "w_conv_out", "w_gla_out", "w_out", "w2_gu", "w2_down"):
        wts[n] = full[n]
    wts["w_in_p"] = _perm_in_cols(full["w_in"])
    wts["dw_weight"] = full["dw_weight"].astype(F32)
    zpad = jnp.zeros((128, HEADS * DK), BF16)
    wts["w_alpha_f_pad"] = zpad.at[0:LOWRANK].set(full["w_alpha_f"])
    wts["w_alpha_b_pad"] = zpad.at[LOWRANK:2 * LOWRANK].set(full["w_alpha_b"])

    loss, grad_x, grads = _local_step(x, c, ctx, loss_target, wts)
    loss = lax.psum(loss, ("x", "y", "c"))

    packed = jnp.stack([
        _pack_flat([jnp.split(grads[n], 4, axis=ax)[j].astype(BF16) for n, ax in SHARDED], PIECE_ALIGN)
        for j in range(4)])
    cidx = lax.axis_index("c").astype(jnp.int32).reshape(1)
    chip_sums = _add_halves(packed, _swap_halves(packed), cidx)
    reduced = _join_halves(_sum_chips(_scatter_chips(chip_sums)))
    g_shard = dict(zip([n for n, _ in SHARDED], _unpack_flat(reduced, shard_shapes)))

    rep_shapes = [w[n].shape for n in REPLICATED]
    small = _allreduce_small(_pack_flat([grads[n].reshape(w[n].shape) for n in REPLICATED], 8 * 1024))
    g_rep = dict(zip(REPLICATED, _unpack_flat(small, rep_shapes)))

    g_out, d_out, m_out, v_out = {}, {}, {}, {}
    for n, _ in SHARDED:
        s2 = w[n].shape[1:]
        d, nm, nv = _adamw("adamw_" + n, w[n].reshape(s2), g_shard[n], m[n].reshape(s2), v[n].reshape(s2))
        g_out[n] = g_shard[n].reshape(w[n].shape)
        d_out[n], m_out[n], v_out[n] = d.reshape(w[n].shape), nm.reshape(w[n].shape), nv.reshape(w[n].shape)
    pk = lambda t: _pack_flat([t[n] for n in REPLICATED], 8 * 1024)
    d, nm, nv = _adamw("adamw_vectors", pk(w), small, pk(m), pk(v))
    for n, dd, mm, vv in zip(REPLICATED, _unpack_flat(d, rep_shapes), _unpack_flat(nm, rep_shapes),
                             _unpack_flat(nv, rep_shapes)):
        g_out[n], d_out[n], m_out[n], v_out[n] = g_rep[n], dd, mm, vv

    return (loss, grad_x, *[g_out[n] for n in WEIGHTS], *[d_out[n] for n in WEIGHTS],
            *[m_out[n] for n in WEIGHTS], *[v_out[n] for n in WEIGHTS])
```

```python
import functools
import math

import jax
import jax.numpy as jnp
from jax import lax
from jax.experimental import pallas as pl
from jax.experimental.pallas import tpu as pltpu

F32, BF16 = jnp.float32, jnp.bfloat16
MESH = pl.DeviceIdType.MESH
HIGHEST = lax.Precision.HIGHEST

D = 1024
FF = 2816
HEADS, DK, DV = 4, 128, 256
LOWRANK = 16
CONV_W = 31
CHUNK = 64
TAU = 16.0
EPS = 1e-6
Q_SCALE = DK ** -0.5
TM = 256
D_IN = 7200
D_INP = 7296
LR_COL = 7168
VMEM_LIMIT = 52 * 1024 * 1024

ADAM_LR, ADAM_B1, ADAM_B2, ADAM_EPS, ADAM_WD, ADAM_STEP = 0.001, 0.9, 0.999, 1e-08, 0.01, 10


def _silu(x):
    return x * jax.nn.sigmoid(x)


def _rms(h, g):
    return h * lax.rsqrt(jnp.mean(h * h, axis=-1, keepdims=True) + EPS) * g


def _modnorm(x, g, shift, scale):
    return _rms(x, g) * (1 + scale) + shift


def _cparams(sem=None):
    return pltpu.CompilerParams(dimension_semantics=sem, vmem_limit_bytes=VMEM_LIMIT)


def tok(arr, width=None, cb=0, clamp=None):
    return ("tok", arr, arr.shape[1] if width is None else width, cb, clamp)


def mod(arr):
    return ("mod", arr)


def const(arr):
    return ("const", arr)


def _rowmap(name, body, n_tiles, ins, outs, *, tpb, nb):
    def modrow(i):
        return jnp.minimum(i // tpb, nb)

    in_specs, args = [], []
    for spec in ins:
        if spec[0] == "tok":
            _, arr, width, cb, clamp = spec
            if clamp is None:
                im = lambda i, cb=cb: (i, cb)
            else:
                im = lambda i, cb=cb, clamp=clamp: (jnp.minimum(i, clamp), cb)
            in_specs.append(pl.BlockSpec((TM, width), im))
        elif spec[0] == "mod":
            arr = spec[1]
            in_specs.append(pl.BlockSpec((1, 1, arr.shape[2]), lambda i: (modrow(i), 0, 0)))
        else:
            arr = spec[1]
            in_specs.append(pl.BlockSpec(arr.shape, lambda i, nd=arr.ndim: (0,) * nd))
        args.append(arr)
    out_specs, out_shapes = [], []
    for o in outs:
        if o[0] == "tok":
            _, rows, width, dtype = o
            out_shapes.append(jax.ShapeDtypeStruct((rows, width), dtype))
            out_specs.append(pl.BlockSpec((TM, width), lambda i: (i, 0)))
        elif o[0] == "acc":
            _, rows, width = o
            out_shapes.append(jax.ShapeDtypeStruct((rows, width), F32))
            out_specs.append(pl.BlockSpec((rows, width), lambda i: (0, 0)))
        else:
            width = o[1]
            rows_visited = min((n_tiles - 1) // tpb, nb) + 1
            out_shapes.append(jax.ShapeDtypeStruct((rows_visited, 1, width), F32))
            out_specs.append(pl.BlockSpec((1, 1, width), lambda i: (modrow(i), 0, 0)))
    n_in = len(ins)

    def kern(*refs):
        i = pl.program_id(0)
        vals = []
        for r, spec in zip(refs[:n_in], ins):
            vals.append(r[0] if spec[0] == "mod" else r[...])
        res = body(i, *vals)
        for r, o, val in zip(refs[n_in:], outs, res):
            if o[0] == "tok":
                r[...] = val.astype(r.dtype)
            elif o[0] == "acc":
                @pl.when(i == 0)
                def _():
                    r[...] = jnp.zeros(r.shape, F32)
                r[...] += jnp.broadcast_to(val, r.shape)
            else:
                first = jnp.logical_or(i == 0, modrow(i) != modrow(jnp.maximum(i - 1, 0)))

                @pl.when(first)
                def _():
                    r[...] = jnp.zeros(r.shape, F32)
                r[0] += val

    return pl.pallas_call(
        kern, grid=(n_tiles,), in_specs=in_specs, out_specs=out_specs, out_shape=out_shapes,
        compiler_params=_cparams(("arbitrary",)), name=name)(*args)


def _pick(n, cands):
    for c in cands:
        if n % c == 0:
            return c
    return n


def _mm(name, a, b, *, trans_b=False, out_dtype=F32, a_fn=None, bias=None, rows=None):
    m, k = a.shape
    m = m if rows is None else rows
    shard = b.shape[2] if b.ndim == 3 else None
    if trans_b:
        n = b.shape[-2]
        tk = _pick(shard, (2816, 2304, 1408, 1024)) if shard else (
            k if k <= 2816 else _pick(k, (2816, 2432, 2304, 2048, 1536, 1408, 1024, 512, 256, 128)))
        tn = _pick(n, (512, 384, 256, 128))
    else:
        n = 4 * shard if shard else b.shape[1]
        tk = k if k <= 2816 else _pick(k, (2816, 2432, 2304, 2048, 1536, 1408, 1024, 512, 256, 128))
        tn = _pick(shard, (512, 384, 1408, 256, 128)) if shard else _pick(n, (512, 384, 256, 128))
    tm = _pick(m, (1024, 512, 256) if tk <= 1024 else (512, 256))
    nk = k // tk
    per = shard // (tk if trans_b else tn) if shard else None
    dims = (((1,), (1,)), ((), ())) if trans_b else (((1,), (0,)), ((), ()))

    def kern(*refs):
        a_ref, b_ref = refs[0], refs[1]
        bias_ref = refs[2] if bias is not None else None
        o_ref, acc_ref = refs[-2], refs[-1]
        kk = pl.program_id(2)
        av = a_ref[...]
        if a_fn is not None:
            av = a_fn(av)
        p = lax.dot_general(av.astype(BF16), b_ref[...].astype(BF16), dims, preferred_element_type=F32)

        def finish(total):
            if bias_ref is not None:
                total = total + bias_ref[...]
            o_ref[...] = total.astype(o_ref.dtype)

        if nk == 1:
            finish(p)
        else:
            @pl.when(kk == 0)
            def _():
                acc_ref[...] = p

            @pl.when(kk > 0)
            def _():
                acc_ref[...] += p

            @pl.when(kk == nk - 1)
            def _():
                finish(acc_ref[...])

    if shard and trans_b:
        b_spec = pl.BlockSpec((None, tn, tk), lambda i, j, kk: (kk // per, j, kk % per))
    elif shard:
        b_spec = pl.BlockSpec((None, tk, tn), lambda i, j, kk: (j // per, kk, j % per))
    elif trans_b:
        b_spec = pl.BlockSpec((tn, tk), lambda i, j, kk: (j, kk))
    else:
        b_spec = pl.BlockSpec((tk, tn), lambda i, j, kk: (kk, j))
    in_specs = [pl.BlockSpec((tm, tk), lambda i, j, kk: (i, kk)), b_spec]
    args = [a, b]
    if bias is not None:
        in_specs.append(pl.BlockSpec((1, tn), lambda i, j, kk: (0, j)))
        args.append(bias)
    return pl.pallas_call(
        kern, grid=(m // tm, n // tn, nk), in_specs=in_specs,
        out_specs=pl.BlockSpec((tm, tn), lambda i, j, kk: (i, j)),
        out_shape=jax.ShapeDtypeStruct((m, n), out_dtype),
        scratch_shapes=[pltpu.VMEM((tm, tn) if nk > 1 else (8, 128), F32)],
        compiler_params=_cparams(("parallel", "parallel", "arbitrary")), name=name)(*args)


def _mm_tn(name, x, dy, rows=None, col_shards=False):
    t = x.shape[0] if rows is None else rows
    k1, n1 = x.shape[1], dy.shape[1]
    tt = _pick(t, (512, 256, 128, 64, 8))
    tk1 = _pick(k1, (1024, 1408, 512, 256, 128))
    tn = _pick(n1 // 4, (512, 384, 1408, 256, 128)) if col_shards else _pick(n1, (512, 384, 256, 128))
    per = n1 // 4 // tn
    ns = t // tt

    def kern(x_ref, dy_ref, o_ref, acc_ref):
        s = pl.program_id(2)
        p = lax.dot_general(x_ref[...].astype(BF16), dy_ref[...].astype(BF16), (((0,), (0,)), ((), ())),
                            preferred_element_type=F32)

        @pl.when(s == 0)
        def _():
            acc_ref[...] = p

        @pl.when(s > 0)
        def _():
            acc_ref[...] += p

        @pl.when(s == ns - 1)
        def _():
            o_ref[...] = acc_ref[...].astype(o_ref.dtype)

    if col_shards:
        out_spec = pl.BlockSpec((None, tk1, tn), lambda i, j, s: (j // per, i, j % per))
        out_shape = jax.ShapeDtypeStruct((4, k1, n1 // 4), BF16)
    else:
        out_spec = pl.BlockSpec((tk1, tn), lambda i, j, s: (i, j))
        out_shape = jax.ShapeDtypeStruct((k1, n1), BF16)
    return pl.pallas_call(
        kern, grid=(k1 // tk1, n1 // tn, ns),
        in_specs=[pl.BlockSpec((tt, tk1), lambda i, j, s: (s, i)),
                  pl.BlockSpec((tt, tn), lambda i, j, s: (s, j))],
        out_specs=out_spec, out_shape=out_shape, scratch_shapes=[pltpu.VMEM((tk1, tn), F32)],
        compiler_params=_cparams(("parallel", "parallel", "arbitrary")), name=name)(x, dy)


def _gla_maps(bl, t, tc):
    nx, nc = t // CHUNK, tc // CHUNK
    nxb = bl * nx

    def rowblk(d, b, n):
        c_ctx = jnp.where(d == 0, n, nc - 1 - n)
        c_x = jnp.where(d == 0, n - nc, nx - 1 - (n - nc))
        return jnp.where(n < nc, nxb + b * nc + c_ctx, b * nx + c_x)

    def xblk(d, b, n):
        n2 = jnp.maximum(n, nc)
        return b * nx + jnp.where(d == 0, n2 - nc, nx - 1 - (n2 - nc))

    return nx, nc, rowblk, xblk


def _gla_chunk(m, q, k, v, g, h):
    gh = g[:, h * DK:(h + 1) * DK]
    b = jnp.dot(m, gh, precision=HIGHEST, preferred_element_type=F32)
    tot = jnp.sum(gh, axis=0, keepdims=True)
    mid = b[CHUNK // 2:CHUNK // 2 + 1, :]
    qh = q[:, h * DK:(h + 1) * DK] * Q_SCALE
    kh = k[:, h * DK:(h + 1) * DK]
    vh = v[:, h * DV:(h + 1) * DV]
    return b, tot, mid, qh, kh, vh


def _dot(a, b, ca, cb):
    return lax.dot_general(a.astype(BF16), b.astype(BF16), (((ca,), (cb,)), ((), ())),
                           preferred_element_type=F32)


def _gla_fwd(p, g2, mmats, bl, t, tc):
    nx, nc, rowblk, xblk = _gla_maps(bl, t, tc)
    ns = nx + nc

    def kern(q_ref, k_ref, v_ref, g_ref, m_ref, o_ref, ss_ref, s_ref):
        n = pl.program_id(2)

        @pl.when(n == 0)
        def _():
            s_ref[...] = jnp.zeros(s_ref.shape, F32)
        m = m_ref[0]
        q, k, v, g = q_ref[...], k_ref[...], v_ref[...], g_ref[...]
        for h in range(HEADS):
            b, tot, mid, qh, kh, vh = _gla_chunk(m, q, k, v, g, h)
            s0 = s_ref[h * DV:(h + 1) * DV, :]
            ss_ref[0, 0, 0, h * DV:(h + 1) * DV, :] = s0
            kl = kh * jnp.exp(tot - b)
            s_ref[h * DV:(h + 1) * DV, :] = s0 * jnp.exp(tot) + _dot(vh, kl, 0, 0)

            @pl.when(n >= nc)
            def _():
                qe = qh * jnp.exp(b)
                qm = qh * jnp.exp(b - mid)
                km = kh * jnp.exp(mid - b)
                att = m * _dot(qm, km, 1, 1)
                o_ref[0, :, h * DV:(h + 1) * DV] = _dot(qe, s0, 1, 1) + _dot(att, vh, 1, 0)

    nt = p.shape[0]
    return pl.pallas_call(
        kern, grid=(2, bl, ns),
        in_specs=[pl.BlockSpec((CHUNK, 512), lambda d, b, n: (rowblk(d, b, n), 4)),
                  pl.BlockSpec((CHUNK, 512), lambda d, b, n: (rowblk(d, b, n), 5)),
                  pl.BlockSpec((CHUNK, 1024), lambda d, b, n: (rowblk(d, b, n), 3)),
                  pl.BlockSpec((CHUNK, 512), lambda d, b, n: (rowblk(d, b, n), d)),
                  pl.BlockSpec((1, CHUNK, CHUNK), lambda d, b, n: (d, 0, 0))],
        out_specs=[pl.BlockSpec((1, CHUNK, 1024), lambda d, b, n: (d, xblk(d, b, n), 0)),
                   pl.BlockSpec((1, 1, 1, HEADS * DV, DK), lambda d, b, n: (d, b, n, 0, 0))],
        out_shape=[jax.ShapeDtypeStruct((2, bl * t, HEADS * DV), F32),
                   jax.ShapeDtypeStruct((2, bl, ns, HEADS * DV, DK), F32)],
        scratch_shapes=[pltpu.VMEM((HEADS * DV, DK), F32)],
        compiler_params=_cparams(("parallel", "parallel", "arbitrary")), name="gla_fwd")(p, p, p, g2, mmats)


def _gla_bwd(p, g2, mmats, ssave, do, bl, t, tc):
    nx, nc, rowblk, xblk = _gla_maps(bl, t, tc)
    ns = nx + nc
    rev = lambda n: ns - 1 - n

    def kern(q_ref, k_ref, v_ref, g_ref, m_ref, ss_ref, do_ref, dq_ref, dk_ref, dv_ref, dg_ref, ds_ref):
        step = pl.program_id(2)
        n = ns - 1 - step

        @pl.when(step == 0)
        def _():
            ds_ref[...] = jnp.zeros(ds_ref.shape, F32)
        m = m_ref[0]
        q, k, v, g = q_ref[...], k_ref[...], v_ref[...], g_ref[...]
        dov = do_ref[...] * (n >= nc).astype(F32)
        for h in range(HEADS):
            b, tot, mid, qh, kh, vh = _gla_chunk(m, q, k, v, g, h)
            doh = dov[:, h * DV:(h + 1) * DV]
            s0 = ss_ref[0, 0, 0, h * DV:(h + 1) * DV, :]
            ds1 = ds_ref[h * DV:(h + 1) * DV, :]
            eb, ebm, emb, etb = jnp.exp(b), jnp.exp(b - mid), jnp.exp(mid - b), jnp.exp(tot - b)
            etot = jnp.exp(tot)
            qe, qm, km, kl = qh * eb, qh * ebm, kh * emb, kh * etb
            att = m * _dot(qm, km, 1, 1)
            ds_ref[h * DV:(h + 1) * DV, :] = ds1 * etot + _dot(doh, qe, 0, 0)
            dqe = _dot(doh, s0, 1, 0)
            datt = m * _dot(doh, vh, 1, 1)
            dqm = _dot(datt, km, 1, 0)
            dkm = _dot(datt, qm, 0, 0)
            dvh = _dot(att, doh, 0, 0) + _dot(kl, ds1, 1, 1)
            dkl = _dot(vh, ds1, 1, 0)
            dtot = etot * jnp.sum(ds1 * s0, axis=0, keepdims=True) + jnp.sum(dkl * kl, axis=0, keepdims=True)
            db = dqe * qe + dqm * qm - dkm * km - dkl * kl
            dgh = lax.dot_general(m, db, (((0,), (0,)), ((), ())), precision=HIGHEST,
                                  preferred_element_type=F32) + dtot
            dq_ref[0, :, h * DK:(h + 1) * DK] = (dqe * eb + dqm * ebm) * Q_SCALE
            dk_ref[0, :, h * DK:(h + 1) * DK] = dkm * emb + dkl * etb
            dv_ref[0, :, h * DV:(h + 1) * DV] = dvh
            dg_ref[0, :, h * DK:(h + 1) * DK] = dgh

    nt = p.shape[0]
    row3 = lambda d, b, s: (d, rowblk(d, b, rev(s)), 0)
    return pl.pallas_call(
        kern, grid=(2, bl, ns),
        in_specs=[pl.BlockSpec((CHUNK, 512), lambda d, b, s: (rowblk(d, b, rev(s)), 4)),
                  pl.BlockSpec((CHUNK, 512), lambda d, b, s: (rowblk(d, b, rev(s)), 5)),
                  pl.BlockSpec((CHUNK, 1024), lambda d, b, s: (rowblk(d, b, rev(s)), 3)),
                  pl.BlockSpec((CHUNK, 512), lambda d, b, s: (rowblk(d, b, rev(s)), d)),
                  pl.BlockSpec((1, CHUNK, CHUNK), lambda d, b, s: (d, 0, 0)),
                  pl.BlockSpec((1, 1, 1, HEADS * DV, DK), lambda d, b, s: (d, b, rev(s), 0, 0)),
                  pl.BlockSpec((CHUNK, 1024), lambda d, b, s: (xblk(d, b, rev(s)), 0))],
        out_specs=[pl.BlockSpec((1, CHUNK, 512), row3), pl.BlockSpec((1, CHUNK, 512), row3),
                   pl.BlockSpec((1, CHUNK, 1024), row3), pl.BlockSpec((1, CHUNK, 512), row3)],
        out_shape=[jax.ShapeDtypeStruct((2, nt, 512), F32), jax.ShapeDtypeStruct((2, nt, 512), F32),
                   jax.ShapeDtypeStruct((2, nt, 1024), F32), jax.ShapeDtypeStruct((2, nt, 512), F32)],
        scratch_shapes=[pltpu.VMEM((HEADS * DV, DK), F32)],
        compiler_params=_cparams(("parallel", "parallel", "arbitrary")), name="gla_bwd")(
            p, p, p, g2, mmats, ssave, do)


CONV_CT = 256
CONV_PAD = 16
CONV_RC = 128


def _conv_fill(zp, z_ref, t):
    zp[0:CONV_PAD, :] = jnp.zeros((CONV_PAD, CONV_CT), F32)
    zp[CONV_PAD + t:2 * CONV_PAD + t, :] = jnp.zeros((CONV_PAD, CONV_CT), F32)
    zp[CONV_PAD:CONV_PAD + t, :] = z_ref[...]


def _dwconv(name, z, w, bias, bl, t, flip):
    def kern(z_ref, w_ref, b_ref, o_ref, zp):
        _conv_fill(zp, z_ref, t)
        for r in range(0, t, CONV_RC):
            acc = jnp.broadcast_to(b_ref[...], (CONV_RC, CONV_CT))
            for j in range(CONV_W):
                off = (CONV_W - j) if flip else (j + 1)
                acc = acc + w_ref[j:j + 1, :] * zp[r + off:r + off + CONV_RC, :]
            o_ref[r:r + CONV_RC, :] = acc

    return pl.pallas_call(
        kern, grid=(bl, 1024 // CONV_CT),
        in_specs=[pl.BlockSpec((t, CONV_CT), lambda b, c: (b, c)),
                  pl.BlockSpec((32, CONV_CT), lambda b, c: (0, c)),
                  pl.BlockSpec((1, CONV_CT), lambda b, c: (0, c))],
        out_specs=pl.BlockSpec((t, CONV_CT), lambda b, c: (b, c)),
        out_shape=jax.ShapeDtypeStruct(z.shape, F32),
        scratch_shapes=[pltpu.VMEM((t + 2 * CONV_PAD, CONV_CT), F32)],
        compiler_params=_cparams(("parallel", "parallel")), name=name)(z, w, bias)


def _dwconv_wgrad(z, dzc, bl, t):
    def kern(z_ref, d_ref, dw_ref, db_ref, zp):
        b = pl.program_id(1)

        @pl.when(b == 0)
        def _():
            dw_ref[...] = jnp.zeros(dw_ref.shape, F32)
            db_ref[...] = jnp.zeros(db_ref.shape, F32)
        _conv_fill(zp, z_ref, t)
        for j in range(CONV_W):
            acc = jnp.zeros((CONV_RC, CONV_CT), F32)
            for r in range(0, t, CONV_RC):
                acc = acc + d_ref[r:r + CONV_RC, :] * zp[r + j + 1:r + j + 1 + CONV_RC, :]
            dw_ref[j:j + 1, :] += jnp.sum(acc, axis=0, keepdims=True)
        db_ref[...] += jnp.sum(d_ref[...], axis=0, keepdims=True)

    return pl.pallas_call(
        kern, grid=(1024 // CONV_CT, bl),
        in_specs=[pl.BlockSpec((t, CONV_CT), lambda c, b: (b, c)),
                  pl.BlockSpec((t, CONV_CT), lambda c, b: (b, c))],
        out_specs=[pl.BlockSpec((32, CONV_CT), lambda c, b: (0, c)),
                   pl.BlockSpec((1, CONV_CT), lambda c, b: (0, c))],
        out_shape=[jax.ShapeDtypeStruct((32, 1024), F32), jax.ShapeDtypeStruct((1, 1024), F32)],
        scratch_shapes=[pltpu.VMEM((t + 2 * CONV_PAD, CONV_CT), F32)],
        compiler_params=_cparams(("parallel", "arbitrary")), name="dwconv_wgrad")(z, dzc)


def _ffn_fwd(tag, xin, n_tiles, g, sh, sc, gate, w_gu, w_down, tpb, nb):
    rows = n_tiles * TM
    rm = functools.partial(_rowmap, tpb=tpb, nb=nb)
    (u,) = rm(tag + "_norm", lambda i, x, g_, sh_, sc_: (_modnorm(x, g_, sh_, sc_),), n_tiles,
              [tok(xin), const(g), mod(sh), mod(sc)], [("tok", rows, D, BF16)])
    ab = _mm(tag + "_gu", u, w_gu)
    (hm,) = rm(tag + "_act", lambda i, a, b: (_silu(a) * b,), n_tiles,
               [tok(ab, FF, 0), tok(ab, FF, 1)], [("tok", rows, FF, BF16)])
    f = _mm(tag + "_down", hm, w_down)
    (xout,) = rm(tag + "_res", lambda i, x, f_, gt: (x + 0.5 * gt * f_,), n_tiles,
                 [tok(xin), tok(f), mod(gate)], [("tok", rows, D, F32)])
    return xout, (u, ab, hm, f)


def _ffn_bwd(tag, xin, saved, dxout, dx_clamp, n_tiles, g, sh, sc, gate, w_gu, w_down, tpb, nb):
    u, ab, hm, f = saved
    rows = n_tiles * TM
    rm = functools.partial(_rowmap, tpb=tpb, nb=nb)

    def mask(i):
        return 1.0 if dx_clamp is None else (i <= dx_clamp).astype(F32)

    def b1(i, dx, f_, gt):
        dx = dx * mask(i)
        return (0.5 * gt * dx, jnp.sum(0.5 * f_ * dx, axis=0, keepdims=True))
    df, dgate = rm(tag + "_bres", b1, n_tiles, [tok(dxout, clamp=dx_clamp), tok(f), mod(gate)],
                   [("tok", rows, D, BF16), ("modacc", D)])
    dhm = _mm(tag + "_bdown", df, w_down, trans_b=True)
    dw_down = _mm_tn(tag + "_wdown", hm, df)

    def b2(i, dh, a, b):
        s = jax.nn.sigmoid(a)
        return (jnp.concatenate([dh * b * (s * (1 + a * (1 - s))), dh * (a * s)], axis=1),)
    (dab,) = rm(tag + "_bact", b2, n_tiles, [tok(dhm), tok(ab, FF, 0), tok(ab, FF, 1)],
                [("tok", rows, 2 * FF, BF16)])
    du = _mm(tag + "_bgu", dab, w_gu, trans_b=True)
    dw_gu = _mm_tn(tag + "_wgu", u, dab, col_shards=True)

    def b3(i, x, g_, sh_, sc_, du_, dx):
        _, vjp = jax.vjp(_modnorm, x, g_, sh_, sc_)
        dxn, dg, dsh, dsc = vjp(du_)
        return (dx * mask(i) + dxn, dg, dsh, dsc)
    dxin, dg, dsh, dsc = rm(tag + "_bnorm", b3, n_tiles,
                            [tok(xin), const(g), mod(sh), mod(sc), tok(du), tok(dxout, clamp=dx_clamp)],
                            [("tok", rows, D, F32), ("acc", 1, D), ("modacc", D), ("modacc", D)])
    return dxin, dict(g=dg, sh=dsh, sc=dsc, gate=dgate, w_gu=dw_gu, w_down=dw_down)


def _perm_in_cols(w):
    pad = jnp.zeros(w.shape[:-1] + (D_INP - D_IN,), w.dtype)
    return jnp.concatenate([w[..., :5120], w[..., 5152:7200], w[..., 5120:5152], pad], axis=-1)


def _unperm_in_cols(w):
    return jnp.concatenate([w[..., :5120], w[..., LR_COL:LR_COL + 32], w[..., 5120:LR_COL]], axis=-1)


def _local_step(x, c, ctx, target, wts):
    bl, t, _ = x.shape
    tc = ctx.shape[1]
    nx_rows, nc_rows = bl * t, bl * tc
    nt_rows = nx_rows + nc_rows
    tpb = t // TM
    nxt, ntt = nx_rows // TM, nt_rows // TM
    nb = bl
    rm = functools.partial(_rowmap, tpb=tpb, nb=nb)
    last_x = nxt - 1

    x0 = jnp.concatenate([x.reshape(nx_rows, D), ctx.reshape(nc_rows, D)], axis=0)
    tgt = target.reshape(nx_rows, D)

    cc = jnp.concatenate([c, wts["c_ctx"].reshape(1, D), jnp.zeros((8 - bl - 1, D), F32)], axis=0)
    modv = _mm("mod_fwd", cc, wts["w_mod"], a_fn=_silu, bias=wts["b_mod"])
    mods = [modv[:nb + 1, k * D:(k + 1) * D].reshape(nb + 1, 1, D) for k in range(9)]

    x1, sv1 = _ffn_fwd("ffn1", x0, ntt, wts["g_ffn1"], mods[0], mods[1], mods[2], wts["w1_gu"], wts["w1_down"],
                       tpb, nb)
    (u2,) = rm("in_norm", lambda i, x_, g_, sh_, sc_: (_modnorm(x_, g_, sh_, sc_),), ntt,
               [tok(x1), const(wts["g_mix"]), mod(mods[3]), mod(mods[4])], [("tok", nt_rows, D, BF16)])
    w_inp = wts["w_in_p"]
    p = _mm("in_proj", u2, w_inp)

    waf, wab, baf, bab = wts["w_alpha_f_pad"], wts["w_alpha_b_pad"], wts["b_alpha_f"], wts["b_alpha_b"]

    def dec_fwd(i, lr, wf, wb, bf_, bb_):
        zf = _dot(lr, wf, 1, 0) + bf_
        zb = _dot(lr, wb, 1, 0) + bb_
        return (jnp.concatenate([jax.nn.log_sigmoid(zf) / TAU, jax.nn.log_sigmoid(zb) / TAU], axis=1),)
    (gfb,) = rm("decay_fwd", dec_fwd, ntt, [tok(p, 128, LR_COL // 128), const(waf), const(wab), const(baf), const(bab)],
                [("tok", nt_rows, 1024, F32)])
    g2 = gfb
    tri = jnp.tril(jnp.ones((CHUNK, CHUNK), F32))
    mmats = jnp.stack([tri, tri.T])
    o2, ssave = _gla_fwd(p, g2, mmats, bl, t, tc)

    gn_g = wts["gla_norm_g"]

    def gla_out(of, ob, og, gn):
        o = of + ob
        parts = []
        for h in range(HEADS):
            oh = o[:, h * DV:(h + 1) * DV]
            parts.append(oh * lax.rsqrt(jnp.mean(oh * oh, axis=-1, keepdims=True) + EPS))
        return jnp.concatenate(parts, axis=1) * gn * _silu(og)
    (yg_in,) = rm("gla_out", lambda i, of, ob, og, gn: (gla_out(of, ob, og, gn),), nxt,
                  [tok(o2[0]), tok(o2[1]), tok(p, 1024, 4), const(gn_g)], [("tok", nx_rows, D, BF16)])
    y_gla = _mm("gla_proj", yg_in, wts["w_gla_out"])

    (z,) = rm("glu", lambda i, a, b: (a * jax.nn.sigmoid(b),), nxt, [tok(p, 1024, 0), tok(p, 1024, 1)],
              [("tok", nx_rows, D, F32)])
    dw_w = jnp.concatenate([wts["dw_weight"], jnp.zeros((1, D), F32)], axis=0)
    zc = _dwconv("dwconv_fwd", z, dw_w, wts["dw_bias"], bl, t, False)

    def ln_silu(zc_, g_, b_):
        mu = jnp.mean(zc_, axis=-1, keepdims=True)
        var = jnp.mean(jnp.square(zc_ - mu), axis=-1, keepdims=True)
        return _silu((zc_ - mu) * lax.rsqrt(var + EPS) * g_ + b_)
    ln_g, ln_b = wts["conv_ln_g"], wts["conv_ln_b"]
    (zl,) = rm("conv_ln", lambda i, zc_, g_, b_: (ln_silu(zc_, g_, b_),), nxt, [tok(zc), const(ln_g), const(ln_b)],
               [("tok", nx_rows, D, BF16)])
    y_conv = _mm("conv_proj", zl, wts["w_conv_out"])

    (mg,) = rm("merge", lambda i, ga, gb, yc, yg: (jax.nn.sigmoid(ga) * yc + jax.nn.sigmoid(gb) * yg,), nxt,
               [tok(p, 1024, 5), tok(p, 1024, 6), tok(y_conv), tok(y_gla)], [("tok", nx_rows, D, BF16)])
    mix = _mm("out_proj", mg, wts["w_out"])
    (x2,) = rm("mix_res", lambda i, x_, mx_, gt: (x_ + gt * mx_,), nxt, [tok(x1), tok(mix), mod(mods[5])],
               [("tok", nx_rows, D, F32)])

    x3, sv2 = _ffn_fwd("ffn2", x2, nxt, wts["g_ffn2"], mods[6], mods[7], mods[8], wts["w2_gu"], wts["w2_down"],
                       tpb, nb)
    g_fin = wts["g_final"].reshape(1, D)

    def head(i, x_, g_, tg):
        y, vjp = jax.vjp(_rms, x_, g_)
        diff = y - tg
        dx, dg = vjp(diff * (1.0 / D))
        loss = 0.5 * jnp.sum(jnp.mean(diff * diff, axis=-1, keepdims=True))
        return dx, dg, loss
    dx3, dg_final, loss_acc = rm("loss_head", head, nxt, [tok(x3), const(g_fin), tok(tgt)],
                                 [("tok", nx_rows, D, F32), ("acc", 1, D), ("acc", 8, 128)])
    loss = loss_acc[0, 0]

    grads = {}
    dx2, gf2 = _ffn_bwd("ffn2", x2, sv2, dx3, None, nxt, wts["g_ffn2"], mods[6], mods[7], mods[8],
                        wts["w2_gu"], wts["w2_down"], tpb, nb)
    grads.update(g_ffn2=gf2["g"], w2_gu=gf2["w_gu"], w2_down=gf2["w_down"])

    dmix, dgate5 = rm("mix_bres", lambda i, dx, mx_, gt: (gt * dx, jnp.sum(mx_ * dx, axis=0, keepdims=True)), nxt,
                      [tok(dx2), tok(mix), mod(mods[5])], [("tok", nx_rows, D, BF16), ("modacc", D)])
    dmg = _mm("out_bproj", dmix, wts["w_out"], trans_b=True)
    grads["w_out"] = _mm_tn("out_wgrad", mg, dmix)

    def merge_bwd(i, dm, ga, gb, yc, yg):
        keep = (i <= last_x).astype(F32)
        dm = dm * keep
        sa, sb = jax.nn.sigmoid(ga), jax.nn.sigmoid(gb)
        return dm * sa, dm * sb, dm * yc * sa * (1 - sa), dm * yg * sb * (1 - sb)
    cl = dict(clamp=last_x)
    dyc, dyg, dga, dgb = rm("merge_bwd", merge_bwd, ntt,
                            [tok(dmg, **cl), tok(p, 1024, 5, last_x), tok(p, 1024, 6, last_x), tok(y_conv, **cl),
                             tok(y_gla, **cl)],
                            [("tok", nt_rows, D, BF16)] * 4)

    dzl = _mm("conv_bproj", dyc, wts["w_conv_out"], trans_b=True, rows=nx_rows)
    grads["w_conv_out"] = _mm_tn("conv_wgrad", zl, dyc, rows=nx_rows)

    def ln_bwd(i, zc_, g_, b_, dz_):
        _, vjp = jax.vjp(ln_silu, zc_, g_, b_)
        return vjp(dz_)
    dzc, dln_g, dln_b = rm("conv_ln_bwd", ln_bwd, nxt, [tok(zc), const(ln_g), const(ln_b), tok(dzl)],
                           [("tok", nx_rows, D, F32), ("acc", 1, D), ("acc", 1, D)])
    dz = _dwconv("dwconv_bwd", dzc, dw_w, jnp.zeros((1, D), F32), bl, t, True)
    ddw, ddb = _dwconv_wgrad(z, dzc, bl, t)
    grads.update(conv_ln_g=dln_g, conv_ln_b=dln_b, dw_weight=ddw[:CONV_W], dw_bias=ddb)

    def glu_bwd(i, dz_, a, b):
        keep = (i <= last_x).astype(F32)
        dz_ = dz_ * keep
        s = jax.nn.sigmoid(b)
        return (jnp.concatenate([dz_ * s, dz_ * a * s * (1 - s)], axis=1),)
    (dconv,) = rm("glu_bwd", glu_bwd, ntt, [tok(dz, **cl), tok(p, 1024, 0, last_x), tok(p, 1024, 1, last_x)],
                  [("tok", nt_rows, 2048, BF16)])

    dyg_in = _mm("gla_bproj", dyg, wts["w_gla_out"], trans_b=True, rows=nx_rows)
    grads["w_gla_out"] = _mm_tn("gla_wgrad", yg_in, dyg, rows=nx_rows)

    def gla_out_bwd(i, of, ob, og, gn, dy):
        _, vjp = jax.vjp(gla_out, of, ob, og, gn)
        do_, _, dog_, dgn_ = vjp(dy)
        return do_, dog_, dgn_
    do, dog_x, dgn = rm("gla_out_bwd", gla_out_bwd, nxt,
                        [tok(o2[0]), tok(o2[1]), tok(p, 1024, 4), const(gn_g), tok(dyg_in)],
                        [("tok", nx_rows, D, F32), ("tok", nx_rows, D, BF16), ("acc", 1, D)])
    grads["gla_norm_g"] = dgn
    dog = jnp.concatenate([dog_x, jnp.zeros((nc_rows, D), BF16)], axis=0)

    dq2, dk2, dv2, dg2 = _gla_bwd(p, g2, mmats, ssave, do, bl, t, tc)

    def dec_bwd(i, lr, wf, wb, bf_, bb_, dgf, dgb_):
        zf = _dot(lr, wf, 1, 0) + bf_
        zb = _dot(lr, wb, 1, 0) + bb_
        dzf = dgf * (1 - jax.nn.sigmoid(zf)) * (1.0 / TAU)
        dzb = dgb_ * (1 - jax.nn.sigmoid(zb)) * (1.0 / TAU)
        dlr = _dot(dzf, wf, 1, 1) + _dot(dzb, wb, 1, 1)
        return (dlr, _dot(lr, dzf, 0, 0), _dot(lr, dzb, 0, 0), jnp.sum(dzf, axis=0, keepdims=True),
                jnp.sum(dzb, axis=0, keepdims=True))
    dlr, dwaf, dwab, dbaf, dbab = rm(
        "decay_bwd", dec_bwd, ntt,
        [tok(p, 128, LR_COL // 128), const(waf), const(wab), const(baf), const(bab), tok(dg2[0]), tok(dg2[1])],
        [("tok", nt_rows, 128, BF16), ("acc", 128, 512), ("acc", 128, 512), ("acc", 1, 512), ("acc", 1, 512)])
    grads.update(w_alpha_f=dwaf[:LOWRANK], w_alpha_b=dwab[LOWRANK:2 * LOWRANK], b_alpha_f=dbaf, b_alpha_b=dbab)

    dq, dk, dv = rm("gla_sum", lambda i, q0, q1, k0, k1, v0, v1: (q0 + q1, k0 + k1, v0 + v1), ntt,
                    [tok(dq2[0]), tok(dq2[1]), tok(dk2[0]), tok(dk2[1]), tok(dv2[0]), tok(dv2[1])],
                    [("tok", nt_rows, 512, BF16), ("tok", nt_rows, 512, BF16), ("tok", nt_rows, 1024, BF16)])

    dp = jnp.concatenate([dconv, dq, dk, dv, dog, dga, dgb, dlr], axis=1)
    du2 = _mm("in_bproj", dp, w_inp, trans_b=True)
    grads["w_in_p"] = _mm_tn("in_wgrad", u2, dp)

    def in_norm_bwd(i, x_, g_, sh_, sc_, du_, dx):
        keep = (i <= last_x).astype(F32)
        _, vjp = jax.vjp(_modnorm, x_, g_, sh_, sc_)
        dxn, dg, dsh, dsc = vjp(du_)
        return (dx * keep + dxn, dg, dsh, dsc)
    dx1, dg_mix, dsh3, dsc4 = rm("in_norm_bwd", in_norm_bwd, ntt,
                                 [tok(x1), const(wts["g_mix"]), mod(mods[3]), mod(mods[4]), tok(du2), tok(dx2, **cl)],
                                 [("tok", nt_rows, D, F32), ("acc", 1, D), ("modacc", D), ("modacc", D)])
    grads["g_mix"] = dg_mix

    dx0, gf1 = _ffn_bwd("ffn1", x0, sv1, dx1, None, ntt, wts["g_ffn1"], mods[0], mods[1], mods[2],
                        wts["w1_gu"], wts["w1_down"], tpb, nb)
    grads.update(g_ffn1=gf1["g"], w1_gu=gf1["w_gu"], w1_down=gf1["w_down"])
    grad_x = dx0[:nx_rows].reshape(bl, t, D)

    dmods = [gf1["sh"], gf1["sc"], gf1["gate"], dsh3, dsc4, dgate5, gf2["sh"], gf2["sc"], gf2["gate"]]
    dmod = jnp.concatenate(
        [jnp.concatenate([a.reshape(a.shape[0], D), jnp.zeros((8 - a.shape[0], D), F32)], axis=0) for a in dmods],
        axis=1)
    dsc_ = _mm("mod_bproj", dmod, wts["w_mod"], trans_b=True)

    def silu_bwd(cc_ref, d_ref, dm_ref, dcc_ref, scc_ref, db_ref):
        cc_ = cc_ref[...]
        s = jax.nn.sigmoid(cc_)
        dcc_ref[...] = d_ref[...] * (s * (1 + cc_ * (1 - s)))
        scc_ref[...] = (cc_ * s).astype(BF16)
        db_ref[...] = jnp.sum(dm_ref[...], axis=0, keepdims=True)
    dcc, scc, db_mod = pl.pallas_call(
        silu_bwd, out_shape=[jax.ShapeDtypeStruct((8, D), F32), jax.ShapeDtypeStruct((8, D), BF16),
                             jax.ShapeDtypeStruct((1, 9 * D), F32)], name="mod_silu_bwd")(cc, dsc_, dmod)
    grads["c_ctx"] = dcc[nb]
    grads["b_mod"] = db_mod
    grads["w_mod"] = _mm_tn("mod_wgrad", scc, dmod, col_shards=True)
    grads["g_final"] = dg_final.reshape(D)
    return loss, grad_x, grads


ANY = pl.BlockSpec(memory_space=pl.ANY)


def _place():
    x, y, c = lax.axis_index("x"), lax.axis_index("y"), lax.axis_index("c")
    chips = [(1 - x, y), (x, 1 - y), (1 - x, 1 - y)]
    return x, y, c, chips


def _remote(send_sems, recv_sems):
    def copy(k, src, dst, to):
        return pltpu.make_async_remote_copy(src_ref=src, dst_ref=dst, send_sem=send_sems.at[k],
                                            recv_sem=recv_sems.at[k], device_id=to, device_id_type=MESH)
    return copy


def _sems(n):
    return [pltpu.SemaphoreType.DMA((n,)), pltpu.SemaphoreType.DMA((n,))]


def _gather_weights(shards):
    n = len(shards)

    def body(*refs):
        ins, outs = refs[:n], refs[n:2 * n]
        copy = _remote(refs[2 * n], refs[2 * n + 1])
        x, y, c, chips = _place()
        me = 2 * x + y
        sibling = (x, y, 1 - c)
        started = []

        def rows(i, hc):
            hr = ins[i].shape[0] // 2
            return pl.ds(hc * hr, hr)

        for i in range(n):
            started.append(copy(7 * i + 6, ins[i], outs[i].at[me], sibling))
            started[-1].start()
            for j, (px, py) in enumerate(chips):
                started.append(copy(7 * i + j, ins[i].at[rows(i, c), :], outs[i].at[me, rows(i, c), :], (px, py, c)))
                started[-1].start()
        for i in range(n):
            for j, (px, py) in enumerate(chips):
                half = outs[i].at[2 * px + py, rows(i, c), :]
                copy(7 * i + j, half, half, (px, py, c)).wait_recv()
                started.append(copy(7 * i + 3 + j, half, half, sibling))
                started[-1].start()
        for i in range(n):
            copy(7 * i + 6, ins[i], outs[i].at[me], sibling).wait_recv()
            for j, (px, py) in enumerate(chips):
                other = outs[i].at[2 * px + py, rows(i, 1 - c), :]
                copy(7 * i + 3 + j, other, other, sibling).wait_recv()
        for cp in started:
            cp.wait_send()

    return pl.pallas_call(
        body, out_shape=[jax.ShapeDtypeStruct((4,) + s.shape, s.dtype) for s in shards], in_specs=[ANY] * n,
        out_specs=[ANY] * n, scratch_shapes=_sems(7 * n), name="gather_weights")(*shards)


def _swap_halves(gs):
    n = len(gs)

    def body(*refs):
        ins, outs = refs[:n], refs[n:2 * n]
        copy = _remote(refs[2 * n], refs[2 * n + 1])
        x, y, c, _ = _place()
        cps = []
        for i in range(n):
            hr = ins[i].shape[1] // 2
            cps.append(copy(i, ins[i].at[:, pl.ds((1 - c) * hr, hr), :], outs[i], (x, y, 1 - c)))
            cps[-1].start()
        for cp in cps:
            cp.wait()

    return pl.pallas_call(
        body, out_shape=[jax.ShapeDtypeStruct((4, g.shape[1] // 2, g.shape[2]), g.dtype) for g in gs],
        in_specs=[ANY] * n, out_specs=[ANY] * n, scratch_shapes=_sems(n), name="grad_swap_halves")(*gs)


def _row_tile(hr):
    return hr if hr <= 256 else _pick(hr, (256, 176, 128, 64, 32, 16))


def _add_halves(name, g, r, place):
    hr = r.shape[1]
    tr = _row_tile(hr)
    nblk = hr // tr

    def kern(p_ref, g_ref, r_ref, o_ref):
        o_ref[...] = (g_ref[...].astype(F32) + r_ref[...].astype(F32)).astype(o_ref.dtype)

    blk = (1, tr, g.shape[2])
    return pl.pallas_call(
        kern,
        grid_spec=pltpu.PrefetchScalarGridSpec(
            num_scalar_prefetch=1, grid=(4, nblk),
            in_specs=[pl.BlockSpec(blk, lambda j, i, p: (j, p[0] * nblk + i, 0)),
                      pl.BlockSpec(blk, lambda j, i, p: (j, i, 0))],
            out_specs=pl.BlockSpec(blk, lambda j, i, p: (j, i, 0))),
        out_shape=jax.ShapeDtypeStruct(r.shape, r.dtype),
        compiler_params=_cparams(("parallel", "parallel")), name=name)(place, g, r)


def _scatter_chips(cs):
    n = len(cs)

    def body(*refs):
        ins, outs = refs[:n], refs[n:2 * n]
        copy = _remote(refs[2 * n], refs[2 * n + 1])
        x, y, c, chips = _place()
        me = 2 * x + y
        sends = []
        for i in range(n):
            for j, (px, py) in enumerate(chips):
                sends.append(copy(3 * i + j, ins[i].at[2 * px + py], outs[i].at[me], (px, py, c)))
                sends[-1].start()
        for i in range(n):
            for j, (px, py) in enumerate(chips):
                src = 2 * px + py
                copy(3 * i + j, ins[i].at[src], outs[i].at[src], (px, py, c)).wait_recv()
        for cp in sends:
            cp.wait_send()

    return pl.pallas_call(
        body, out_shape=[jax.ShapeDtypeStruct(a.shape, a.dtype) for a in cs], in_specs=[ANY] * n,
        out_specs=[ANY] * n, scratch_shapes=_sems(3 * n), name="grad_scatter_chips")(*cs)


def _sum_chips(name, cs, r, place):
    hr = r.shape[1]
    tr = _row_tile(hr)
    nblk = hr // tr

    def kern(p_ref, c_ref, r0, r1, r2, r3, o_ref):
        me = p_ref[1]
        acc = None
        for k, rk in enumerate((r0, r1, r2, r3)):
            val = jnp.where(me == k, c_ref[0].astype(F32), rk[0].astype(F32))
            acc = val if acc is None else acc + val
        o_ref[...] = acc

    blk = (1, tr, r.shape[2])

    def slot(k):
        return lambda i, p: (jnp.where(p[1] == k, (k + 1) % 4, k), i, 0)

    return pl.pallas_call(
        kern,
        grid_spec=pltpu.PrefetchScalarGridSpec(
            num_scalar_prefetch=1, grid=(nblk,),
            in_specs=[pl.BlockSpec(blk, lambda i, p: (p[1], i, 0))] + [pl.BlockSpec(blk, slot(k)) for k in range(4)],
            out_specs=pl.BlockSpec((tr, r.shape[2]), lambda i, p: (p[0] * nblk + i, 0))),
        out_shape=jax.ShapeDtypeStruct((2 * hr, r.shape[2]), F32),
        compiler_params=_cparams(("parallel",)), name=name)(place, cs, r, r, r, r)


def _join_halves(fs):
    n = len(fs)

    def body(*refs):
        ins, outs = refs[:n], refs[n:2 * n]
        copy = _remote(refs[2 * n], refs[2 * n + 1])
        x, y, c, _ = _place()
        cps = []
        for i in range(n):
            hr = ins[i].shape[0] // 2
            cps.append(copy(i, ins[i].at[pl.ds(c * hr, hr), :], outs[i].at[pl.ds(c * hr, hr), :], (x, y, 1 - c)))
            cps[-1].start()
        for i in range(n):
            hr = ins[i].shape[0] // 2
            other = outs[i].at[pl.ds((1 - c) * hr, hr), :]
            copy(i, other, other, (x, y, 1 - c)).wait_recv()
        for cp in cps:
            cp.wait_send()

    return pl.pallas_call(
        body, out_shape=[jax.ShapeDtypeStruct(f.shape, f.dtype) for f in fs], in_specs=[ANY] * n,
        out_specs=[ANY] * n, input_output_aliases={i: i for i in range(n)}, scratch_shapes=_sems(n),
        name="grad_join_halves")(*fs)


def _allreduce_small(v):
    def body(x_ref, out_ref, gath, send_sems, recv_sems, local_sem):
        x, y, c, chips = _place()
        me, sibling = (x, y, c), (x, y, 1 - c)

        def slot(px, py, pc):
            return gath.at[4 * px + 2 * py + pc]

        def copy(k, block, to, src=None):
            return pltpu.make_async_remote_copy(
                src_ref=slot(*block) if src is None else src, dst_ref=slot(*block), send_sem=send_sems.at[k],
                recv_sem=recv_sems.at[k], device_id=to, device_id_type=MESH)

        mine = pltpu.make_async_copy(x_ref, slot(*me), local_sem)
        mine.start()
        first = [copy(0, me, sibling, src=x_ref)]
        first += [copy(1 + j, me, (*chip, c), src=x_ref) for j, chip in enumerate(chips)]
        for cp in first:
            cp.start()
        passed = [copy(4 + j, (*chip, c), sibling) for j, chip in enumerate(chips)]
        for j, chip in enumerate(chips):
            copy(1 + j, (*chip, c), me).wait_recv()
            passed[j].start()
        copy(0, sibling, me).wait_recv()
        for j, chip in enumerate(chips):
            copy(4 + j, (*chip, 1 - c), me).wait_recv()
        for cp in first + passed:
            cp.wait_send()
        mine.wait()
        acc = gath[0]
        for k in range(1, 8):
            acc = acc + gath[k]
        out_ref[...] = acc

    vm = pl.BlockSpec(memory_space=pltpu.VMEM)
    return pl.pallas_call(
        body, out_shape=jax.ShapeDtypeStruct(v.shape, F32), in_specs=[vm], out_specs=vm,
        scratch_shapes=[pltpu.VMEM((8,) + v.shape, F32), pltpu.SemaphoreType.DMA((7,)),
                        pltpu.SemaphoreType.DMA((7,)), pltpu.SemaphoreType.DMA(())],
        name="allreduce_small")(v)


def _adamw(name, w, g, m, v):
    r, cols = w.shape
    budget = 262144
    tr = r if r * cols <= budget else next(c for c in (256, 128, 64, 32, 16, 8) if r % c == 0 and c * cols <= budget)

    def kern(w_ref, g_ref, m_ref, v_ref, d_ref, nm_ref, nv_ref):
        gv = g_ref[...]
        nm = ADAM_B1 * m_ref[...] + (1.0 - ADAM_B1) * gv
        nv = ADAM_B2 * v_ref[...] + (1.0 - ADAM_B2) * jnp.square(gv)
        m_hat = nm / (1.0 - ADAM_B1 ** ADAM_STEP)
        v_hat = nv / (1.0 - ADAM_B2 ** ADAM_STEP)
        d_ref[...] = -ADAM_LR * (m_hat / (jnp.sqrt(v_hat) + ADAM_EPS) + ADAM_WD * w_ref[...])
        nm_ref[...] = nm
        nv_ref[...] = nv

    spec = pl.BlockSpec((tr, cols), lambda i: (i, 0))
    shp = jax.ShapeDtypeStruct((r, cols), F32)
    return pl.pallas_call(kern, grid=(r // tr,), in_specs=[spec] * 4, out_specs=[spec] * 3, out_shape=[shp] * 3,
                          compiler_params=_cparams(("parallel",)), name=name)(w, g, m, v)


SHARDED = (("w_mod", 1), ("w1_gu", 1), ("w1_down", 0), ("w_in", 1), ("dw_weight", 1), ("w_conv_out", 0),
           ("w_alpha_f", 1), ("w_alpha_b", 1), ("w_gla_out", 0), ("w_out", 0), ("w2_gu", 1), ("w2_down", 0))
REPLICATED = ("c_ctx", "b_mod", "g_ffn1", "g_mix", "dw_bias", "conv_ln_g", "conv_ln_b", "b_alpha_f", "b_alpha_b",
              "gla_norm_g", "g_ffn2", "g_final")
WEIGHTS = ("c_ctx", "w_mod", "b_mod", "g_ffn1", "w1_gu", "w1_down", "g_mix", "w_in", "dw_weight", "dw_bias",
           "conv_ln_g", "conv_ln_b", "w_conv_out", "w_alpha_f", "b_alpha_f", "w_alpha_b", "b_alpha_b", "gla_norm_g",
           "w_gla_out", "w_out", "g_ffn2", "w2_gu", "w2_down", "g_final")
MATRICES = ("w_mod", "w1_gu", "w1_down", "w_in", "w_conv_out", "w_gla_out", "w_out", "w2_gu", "w2_down")


def _pack_flat(parts, align):
    flat = jnp.concatenate([p.reshape(-1) for p in parts])
    pad = (-flat.shape[0]) % align
    return jnp.concatenate([flat, jnp.zeros((pad,), flat.dtype)]).reshape(-1, 1024)


def _unpack_flat(flat2d, shapes):
    flat = flat2d.reshape(-1)
    out, off = [], 0
    for s in shapes:
        n = math.prod(s)
        out.append(flat[off:off + n].reshape(s))
        off += n
    return out


def kernel(x, c, ctx, c_ctx, w_mod, b_mod, g_ffn1, w1_gu, w1_down, g_mix, w_in, dw_weight, dw_bias, conv_ln_g, conv_ln_b, w_conv_out, w_alpha_f, b_alpha_f, w_alpha_b, b_alpha_b, gla_norm_g, w_gla_out, w_out, g_ffn2, w2_gu, w2_down, g_final, loss_target, m_c_ctx, m_w_mod, m_b_mod, m_g_ffn1, m_w1_gu, m_w1_down, m_g_mix, m_w_in, m_dw_weight, m_dw_bias, m_conv_ln_g, m_conv_ln_b, m_w_conv_out, m_w_alpha_f, m_b_alpha_f, m_w_alpha_b, m_b_alpha_b, m_gla_norm_g, m_w_gla_out, m_w_out, m_g_ffn2, m_w2_gu, m_w2_down, m_g_final, v_c_ctx, v_w_mod, v_b_mod, v_g_ffn1, v_w1_gu, v_w1_down, v_g_mix, v_w_in, v_dw_weight, v_dw_bias, v_conv_ln_g, v_conv_ln_b, v_w_conv_out, v_w_alpha_f, v_b_alpha_f, v_w_alpha_b, v_b_alpha_b, v_gla_norm_g, v_w_gla_out, v_w_out, v_g_ffn2, v_w2_gu, v_w2_down, v_g_final):
    given = dict(locals())
    w = {n: given[n] for n in WEIGHTS}
    m = {n: given["m_" + n] for n in WEIGHTS}
    v = {n: given["v_" + n] for n in WEIGHTS}

    def small_pack(dw, af, ab):
        return jnp.concatenate([dw, jnp.zeros((1, dw.shape[1]), F32), jnp.concatenate([af, ab], axis=1)], axis=0)

    shards = [w[n][0].astype(BF16) for n in MATRICES]
    shards.append(small_pack(w["dw_weight"][0], w["w_alpha_f"][0], w["w_alpha_b"][0]))
    got = dict(zip(MATRICES + ("small",), _gather_weights(shards)))

    wts = {n: w[n] for n in REPLICATED}
    for n in ("w_mod", "w1_gu", "w2_gu"):
        wts[n] = got[n]
    for n in ("w1_down", "w2_down", "w_conv_out", "w_gla_out", "w_out"):
        wts[n] = got[n].reshape(-1, D)
    wts["w_in_p"] = _perm_in_cols(jnp.concatenate([got["w_in"][j] for j in range(4)], axis=1))
    sm = got["small"]
    wts["dw_weight"] = jnp.concatenate([sm[j, :CONV_W] for j in range(4)], axis=1)
    zpad = jnp.zeros((128, HEADS * DK), BF16)
    w_af = jnp.concatenate([sm[j, 32:32 + LOWRANK, :DK] for j in range(4)], axis=1)
    w_ab = jnp.concatenate([sm[j, 32:32 + LOWRANK, DK:] for j in range(4)], axis=1)
    wts["w_alpha_f_pad"] = zpad.at[0:LOWRANK].set(w_af.astype(BF16))
    wts["w_alpha_b_pad"] = zpad.at[LOWRANK:2 * LOWRANK].set(w_ab.astype(BF16))

    loss, grad_x, grads = _local_step(x, c, ctx, loss_target, wts)
    loss = lax.psum(loss, ("x", "y", "c"))

    def pieces(n):
        if n == "w_in":
            g = _unperm_in_cols(grads["w_in_p"])
            return jnp.transpose(g.reshape(D, 4, D_IN // 4), (1, 0, 2))
        g = grads[n]
        if n in ("w_mod", "w1_gu", "w2_gu"):
            return g
        return g.reshape(4, g.shape[0] // 4, g.shape[1])
    gs = [pieces(n) for n in MATRICES]
    gs.append(jnp.stack([small_pack(grads["dw_weight"][:, 256 * j:256 * (j + 1)],
                                    grads["w_alpha_f"][:, DK * j:DK * (j + 1)],
                                    grads["w_alpha_b"][:, DK * j:DK * (j + 1)]) for j in range(4)]))
    tags = MATRICES + ("small",)
    place = jnp.stack([lax.axis_index("c"), 2 * lax.axis_index("x") + lax.axis_index("y")]).astype(jnp.int32)
    swapped = _swap_halves(gs)
    chip_sums = [_add_halves("grad_add_" + t, g, r, place) for t, g, r in zip(tags, gs, swapped)]
    landed = _scatter_chips(chip_sums)
    halves = [_sum_chips("grad_sum_" + t, cs, r, place) for t, cs, r in zip(tags, chip_sums, landed)]
    reduced = dict(zip(tags, _join_halves(halves)))
    g_shard = {n: reduced[n] for n in MATRICES}
    g_shard["dw_weight"] = reduced["small"][:CONV_W]
    g_shard["w_alpha_f"] = reduced["small"][32:32 + LOWRANK, :DK]
    g_shard["w_alpha_b"] = reduced["small"][32:32 + LOWRANK, DK:]

    rep_shapes = [w[n].shape for n in REPLICATED]
    small = _allreduce_small(_pack_flat([grads[n].reshape(w[n].shape) for n in REPLICATED], 8 * 1024))
    g_rep = dict(zip(REPLICATED, _unpack_flat(small, rep_shapes)))

    g_out, d_out, m_out, v_out = {}, {}, {}, {}
    for n, _ in SHARDED:
        s2 = w[n].shape[1:]
        d, nm, nv = _adamw("adamw_" + n, w[n].reshape(s2), g_shard[n], m[n].reshape(s2), v[n].reshape(s2))
        g_out[n] = g_shard[n].reshape(w[n].shape)
        d_out[n], m_out[n], v_out[n] = d.reshape(w[n].shape), nm.reshape(w[n].shape), nv.reshape(w[n].shape)
    pk = lambda t: _pack_flat([t[n] for n in REPLICATED], 8 * 1024)
    d, nm, nv = _adamw("adamw_vectors", pk(w), small, pk(m), pk(v))
    for n, dd, mm, vv in zip(REPLICATED, _unpack_flat(d, rep_shapes), _unpack_flat(nm, rep_shapes),
                             _unpack_flat(nv, rep_shapes)):
        g_out[n], d_out[n], m_out[n], v_out[n] = g_rep[n], dd, mm, vv

    return (loss, grad_x, *[g_out[n] for n in WEIGHTS], *[d_out[n] for n in WEIGHTS],
            *[m_out[n] for n in WEIGHTS], *[v_out[n] for n in WEIGHTS])
```

```python
import functools
import math

import jax
import jax.numpy as jnp
from jax import lax
from jax.experimental import pallas as pl
from jax.experimental.pallas import tpu as pltpu

F32, BF16 = jnp.float32, jnp.bfloat16
MESH = pl.DeviceIdType.MESH
HIGHEST = lax.Precision.HIGHEST

D = 1024
FF = 2816
HEADS, DK, DV = 4, 128, 256
LOWRANK = 16
CONV_W = 31
CHUNK = 64
TAU = 16.0
EPS = 1e-6
Q_SCALE = DK ** -0.5
TM = 256
D_IN = 7200
D_INP = 7296
LR_COL = 7168
VMEM_LIMIT = 52 * 1024 * 1024

ADAM_LR, ADAM_B1, ADAM_B2, ADAM_EPS, ADAM_WD, ADAM_STEP = 0.001, 0.9, 0.999, 1e-08, 0.01, 10


def _silu(x):
    return x * jax.nn.sigmoid(x)


def _rms(h, g):
    return h * lax.rsqrt(jnp.mean(h * h, axis=-1, keepdims=True) + EPS) * g


def _modnorm(x, g, shift, scale):
    return _rms(x, g) * (1 + scale) + shift


def _cparams(sem=None):
    return pltpu.CompilerParams(dimension_semantics=sem, vmem_limit_bytes=VMEM_LIMIT)


def tok(arr, width=None, cb=0, clamp=None):
    return ("tok", arr, arr.shape[1] if width is None else width, cb, clamp)


def mod(arr):
    return ("mod", arr)


def const(arr):
    return ("const", arr)


def _rowmap(name, body, n_tiles, ins, outs, *, tpb, nb):
    def modrow(i):
        return jnp.minimum(i // tpb, nb)

    in_specs, args = [], []
    for spec in ins:
        if spec[0] == "tok":
            _, arr, width, cb, clamp = spec
            if clamp is None:
                im = lambda i, cb=cb: (i, cb)
            else:
                im = lambda i, cb=cb, clamp=clamp: (jnp.minimum(i, clamp), cb)
            in_specs.append(pl.BlockSpec((TM, width), im))
        elif spec[0] == "mod":
            arr = spec[1]
            in_specs.append(pl.BlockSpec((1, 1, arr.shape[2]), lambda i: (modrow(i), 0, 0)))
        else:
            arr = spec[1]
            in_specs.append(pl.BlockSpec(arr.shape, lambda i, nd=arr.ndim: (0,) * nd))
        args.append(arr)
    out_specs, out_shapes = [], []
    for o in outs:
        if o[0] == "tok":
            _, rows, width, dtype = o
            out_shapes.append(jax.ShapeDtypeStruct((rows, width), dtype))
            out_specs.append(pl.BlockSpec((TM, width), lambda i: (i, 0)))
        elif o[0] == "acc":
            _, rows, width = o
            out_shapes.append(jax.ShapeDtypeStruct((rows, width), F32))
            out_specs.append(pl.BlockSpec((rows, width), lambda i: (0, 0)))
        else:
            width = o[1]
            rows_visited = min((n_tiles - 1) // tpb, nb) + 1
            out_shapes.append(jax.ShapeDtypeStruct((rows_visited, 1, width), F32))
            out_specs.append(pl.BlockSpec((1, 1, width), lambda i: (modrow(i), 0, 0)))
    n_in = len(ins)

    def kern(*refs):
        i = pl.program_id(0)
        vals = []
        for r, spec in zip(refs[:n_in], ins):
            vals.append(r[0] if spec[0] == "mod" else r[...])
        res = body(i, *vals)
        for r, o, val in zip(refs[n_in:], outs, res):
            if o[0] == "tok":
                r[...] = val.astype(r.dtype)
            elif o[0] == "acc":
                @pl.when(i == 0)
                def _():
                    r[...] = jnp.zeros(r.shape, F32)
                r[...] += jnp.broadcast_to(val, r.shape)
            else:
                first = jnp.logical_or(i == 0, modrow(i) != modrow(jnp.maximum(i - 1, 0)))

                @pl.when(first)
                def _():
                    r[...] = jnp.zeros(r.shape, F32)
                r[0] += val

    return pl.pallas_call(
        kern, grid=(n_tiles,), in_specs=in_specs, out_specs=out_specs, out_shape=out_shapes,
        compiler_params=_cparams(("arbitrary",)), name=name)(*args)


def _pick(n, cands):
    for c in cands:
        if n % c == 0:
            return c
    return n


def _mm(name, a, b, *, trans_b=False, out_dtype=F32, a_fn=None, bias=None, rows=None):
    m, k = a.shape if a.ndim == 2 else (a.shape[1], 2 * a.shape[2])
    m = m if rows is None else rows
    shard = b.shape[2] if b.ndim == 3 else None
    if trans_b:
        n = b.shape[-2]
        tk = _pick(shard, (2816, 2304, 1408, 1024)) if shard else (
            k if k <= 2816 else _pick(k, (2816, 2432, 2304, 2048, 1536, 1408, 1024, 512, 256, 128)))
        tn = _pick(n, (512, 384, 256, 128))
    else:
        n = 4 * shard if shard else b.shape[1]
        tk = k if k <= 2816 else _pick(k, (2816, 2432, 2304, 2048, 1536, 1408, 1024, 512, 256, 128))
        tn = _pick(shard, (512, 384, 1408, 256, 128)) if shard else _pick(n, (512, 384, 256, 128))
    tm = _pick(m, (1024, 512, 256) if tk <= 1024 else (512, 256))
    nk = k // tk
    per = shard // (tk if trans_b else tn) if shard else None
    dims = (((1,), (1,)), ((), ())) if trans_b else (((1,), (0,)), ((), ()))

    def kern(*refs):
        a_ref, b_ref = refs[0], refs[1]
        bias_ref = refs[2] if bias is not None else None
        o_ref, acc_ref = refs[-2], refs[-1]
        kk = pl.program_id(2)
        av = a_ref[...]
        if a_fn is not None:
            av = a_fn(av)
        p = lax.dot_general(av.astype(BF16), b_ref[...].astype(BF16), dims, preferred_element_type=F32)

        def finish(total):
            if bias_ref is not None:
                total = total + bias_ref[...]
            o_ref[...] = total.astype(o_ref.dtype)

        if nk == 1:
            finish(p)
        else:
            @pl.when(kk == 0)
            def _():
                acc_ref[...] = p

            @pl.when(kk > 0)
            def _():
                acc_ref[...] += p

            @pl.when(kk == nk - 1)
            def _():
                finish(acc_ref[...])

    if shard and trans_b:
        b_spec = pl.BlockSpec((None, tn, tk), lambda i, j, kk: (kk // per, j, kk % per))
    elif shard:
        b_spec = pl.BlockSpec((None, tk, tn), lambda i, j, kk: (j // per, kk, j % per))
    elif trans_b:
        b_spec = pl.BlockSpec((tn, tk), lambda i, j, kk: (j, kk))
    else:
        b_spec = pl.BlockSpec((tk, tn), lambda i, j, kk: (kk, j))
    if a.ndim == 3:
        pa = a.shape[2] // tk
        a_spec = pl.BlockSpec((None, tm, tk), lambda i, j, kk: (kk // pa, i, kk % pa))
    else:
        a_spec = pl.BlockSpec((tm, tk), lambda i, j, kk: (i, kk))
    in_specs = [a_spec, b_spec]
    args = [a, b]
    if bias is not None:
        in_specs.append(pl.BlockSpec((1, tn), lambda i, j, kk: (0, j)))
        args.append(bias)
    return pl.pallas_call(
        kern, grid=(m // tm, n // tn, nk), in_specs=in_specs,
        out_specs=pl.BlockSpec((tm, tn), lambda i, j, kk: (i, j)),
        out_shape=jax.ShapeDtypeStruct((m, n), out_dtype),
        scratch_shapes=[pltpu.VMEM((tm, tn) if nk > 1 else (8, 128), F32)],
        compiler_params=_cparams(("parallel", "parallel", "arbitrary")), name=name)(*args)


def _mm_tn(name, x, dy, rows=None, col_shards=False):
    t = x.shape[0] if rows is None else rows
    k1, n1 = x.shape[1], (dy.shape[1] if dy.ndim == 2 else 2 * dy.shape[2])
    tt = _pick(t, (512, 256, 128, 64, 8))
    tk1 = _pick(k1, (1024, 1408, 512, 256, 128))
    tn = _pick(n1 // 4, (512, 384, 1408, 256, 128)) if col_shards else _pick(n1, (512, 384, 256, 128))
    per = n1 // 4 // tn
    ns = t // tt

    def kern(x_ref, dy_ref, o_ref, acc_ref):
        s = pl.program_id(2)
        p = lax.dot_general(x_ref[...].astype(BF16), dy_ref[...].astype(BF16), (((0,), (0,)), ((), ())),
                            preferred_element_type=F32)

        @pl.when(s == 0)
        def _():
            acc_ref[...] = p

        @pl.when(s > 0)
        def _():
            acc_ref[...] += p

        @pl.when(s == ns - 1)
        def _():
            o_ref[...] = acc_ref[...].astype(o_ref.dtype)

    if col_shards:
        out_spec = pl.BlockSpec((None, tk1, tn), lambda i, j, s: (j // per, i, j % per))
        out_shape = jax.ShapeDtypeStruct((4, k1, n1 // 4), BF16)
    else:
        out_spec = pl.BlockSpec((tk1, tn), lambda i, j, s: (i, j))
        out_shape = jax.ShapeDtypeStruct((k1, n1), BF16)
    if dy.ndim == 3:
        pd = dy.shape[2] // tn
        dy_spec = pl.BlockSpec((None, tt, tn), lambda i, j, s: (j // pd, s, j % pd))
    else:
        dy_spec = pl.BlockSpec((tt, tn), lambda i, j, s: (s, j))
    return pl.pallas_call(
        kern, grid=(k1 // tk1, n1 // tn, ns),
        in_specs=[pl.BlockSpec((tt, tk1), lambda i, j, s: (s, i)), dy_spec],
        out_specs=out_spec, out_shape=out_shape, scratch_shapes=[pltpu.VMEM((tk1, tn), F32)],
        compiler_params=_cparams(("parallel", "parallel", "arbitrary")), name=name)(x, dy)


def _swiglu_fwd(name, u, w_gu):
    m = u.shape[0]
    half = w_gu.shape[2]
    tm = _pick(m, (512, 256))

    def kern(u_ref, wa_ref, wb_ref, ab_ref, hm_ref):
        uv = u_ref[...]
        a = jnp.dot(uv, wa_ref[...], preferred_element_type=F32)
        b = jnp.dot(uv, wb_ref[...], preferred_element_type=F32)
        ab_ref[0] = a.astype(BF16)
        ab_ref[1] = b.astype(BF16)
        hm_ref[...] = (_silu(a) * b).astype(BF16)

    return pl.pallas_call(
        kern, grid=(m // tm, 2),
        in_specs=[pl.BlockSpec((tm, D), lambda i, j: (i, 0)),
                  pl.BlockSpec((None, D, half), lambda i, j: (j, 0, 0)),
                  pl.BlockSpec((None, D, half), lambda i, j: (2 + j, 0, 0))],
        out_specs=[pl.BlockSpec((2, tm, half), lambda i, j: (0, i, j)),
                   pl.BlockSpec((tm, half), lambda i, j: (i, j))],
        out_shape=[jax.ShapeDtypeStruct((2, m, FF), BF16), jax.ShapeDtypeStruct((m, FF), BF16)],
        compiler_params=_cparams(("parallel", "parallel")), name=name)(u, w_gu, w_gu)


def _swiglu_bwd(name, df, w_down, ab):
    m = df.shape[0]
    half = FF // 2
    tm = _pick(m, (512, 256))

    def kern(df_ref, w_ref, ab_ref, o_ref):
        dh = lax.dot_general(df_ref[...], w_ref[...], (((1,), (1,)), ((), ())), preferred_element_type=F32)
        a = ab_ref[0].astype(F32)
        b = ab_ref[1].astype(F32)
        s = jax.nn.sigmoid(a)
        o_ref[0] = (dh * b * (s * (1 + a * (1 - s)))).astype(BF16)
        o_ref[1] = (dh * (a * s)).astype(BF16)

    return pl.pallas_call(
        kern, grid=(m // tm, 2),
        in_specs=[pl.BlockSpec((tm, D), lambda i, j: (i, 0)),
                  pl.BlockSpec((half, D), lambda i, j: (j, 0)),
                  pl.BlockSpec((2, tm, half), lambda i, j: (0, i, j))],
        out_specs=pl.BlockSpec((2, tm, half), lambda i, j: (0, i, j)),
        out_shape=jax.ShapeDtypeStruct((2, m, FF), BF16),
        compiler_params=_cparams(("parallel", "parallel")), name=name)(df, w_down, ab)


def _gla_maps(bl, t, tc):
    nx, nc = t // CHUNK, tc // CHUNK
    nxb = bl * nx

    def rowblk(d, b, n):
        c_ctx = jnp.where(d == 0, n, nc - 1 - n)
        c_x = jnp.where(d == 0, n - nc, nx - 1 - (n - nc))
        return jnp.where(n < nc, nxb + b * nc + c_ctx, b * nx + c_x)

    def xblk(d, b, n):
        n2 = jnp.maximum(n, nc)
        return b * nx + jnp.where(d == 0, n2 - nc, nx - 1 - (n2 - nc))

    return nx, nc, rowblk, xblk


def _gla_chunk(m, q, k, v, g, h):
    gh = g[:, h * DK:(h + 1) * DK]
    b = jnp.dot(m, gh, precision=HIGHEST, preferred_element_type=F32)
    tot = jnp.sum(gh, axis=0, keepdims=True)
    mid = b[CHUNK // 2:CHUNK // 2 + 1, :]
    qh = q[:, h * DK:(h + 1) * DK] * Q_SCALE
    kh = k[:, h * DK:(h + 1) * DK]
    vh = v[:, h * DV:(h + 1) * DV]
    return b, tot, mid, qh, kh, vh


def _dot(a, b, ca, cb):
    return lax.dot_general(a.astype(BF16), b.astype(BF16), (((ca,), (cb,)), ((), ())),
                           preferred_element_type=F32)


def _gla_fwd(p, g2, mmats, bl, t, tc):
    nx, nc, rowblk, xblk = _gla_maps(bl, t, tc)
    ns = nx + nc

    def kern(q0, k0, v0, g0, q1, k1, v1, g1, m_ref, o0, o1, ss_ref, s_ref):
        n = pl.program_id(1)

        @pl.when(n == 0)
        def _():
            s_ref[...] = jnp.zeros(s_ref.shape, F32)
        for d, (q_ref, k_ref, v_ref, g_ref, o_ref) in enumerate(((q0, k0, v0, g0, o0), (q1, k1, v1, g1, o1))):
            m = m_ref[d]
            q, k, v, g = q_ref[...], k_ref[...], v_ref[...], g_ref[...]
            for h in range(HEADS):
                b, tot, mid, qh, kh, vh = _gla_chunk(m, q, k, v, g, h)
                s0 = s_ref[d, h * DV:(h + 1) * DV, :]
                ss_ref[d, 0, 0, h * DV:(h + 1) * DV, :] = s0
                kl = kh * jnp.exp(tot - b)
                s_ref[d, h * DV:(h + 1) * DV, :] = s0 * jnp.exp(tot) + _dot(vh, kl, 0, 0)
                qe = qh * jnp.exp(b)
                qm = qh * jnp.exp(b - mid)
                km = kh * jnp.exp(mid - b)
                att = m * _dot(qm, km, 1, 1)
                o_ref[:, h * DV:(h + 1) * DV] = _dot(qe, s0, 1, 1) + _dot(att, vh, 1, 0)

    def operands(d):
        return [pl.BlockSpec((CHUNK, 512), lambda b, n: (rowblk(d, b, n), 4)),
                pl.BlockSpec((CHUNK, 512), lambda b, n: (rowblk(d, b, n), 5)),
                pl.BlockSpec((CHUNK, 1024), lambda b, n: (rowblk(d, b, n), 3)),
                pl.BlockSpec((CHUNK, 512), lambda b, n: (rowblk(d, b, n), d))]

    o_shape = jax.ShapeDtypeStruct((bl * t, HEADS * DV), F32)
    return pl.pallas_call(
        kern, grid=(bl, ns),
        in_specs=operands(0) + operands(1) + [pl.BlockSpec((2, CHUNK, CHUNK), lambda b, n: (0, 0, 0))],
        out_specs=[pl.BlockSpec((CHUNK, 1024), lambda b, n: (xblk(0, b, n), 0)),
                   pl.BlockSpec((CHUNK, 1024), lambda b, n: (xblk(1, b, n), 0)),
                   pl.BlockSpec((2, 1, 1, HEADS * DV, DK), lambda b, n: (0, b, n, 0, 0))],
        out_shape=[o_shape, o_shape, jax.ShapeDtypeStruct((2, bl, ns, HEADS * DV, DK), F32)],
        scratch_shapes=[pltpu.VMEM((2, HEADS * DV, DK), F32)],
        compiler_params=_cparams(("parallel", "arbitrary")), name="gla_fwd")(p, p, p, g2, p, p, p, g2, mmats)


def _gla_bwd(p, g2, mmats, ssave, do, bl, t, tc):
    nx, nc, rowblk, xblk = _gla_maps(bl, t, tc)
    ns = nx + nc
    rev = lambda s: ns - 1 - s

    def kern(q0, k0, v0, g0, do0, q1, k1, v1, g1, do1, m_ref, ss_ref,
             dq0, dk0, dv0, dg0, dq1, dk1, dv1, dg1, ds_ref):
        step = pl.program_id(1)
        n = ns - 1 - step

        @pl.when(step == 0)
        def _():
            ds_ref[...] = jnp.zeros(ds_ref.shape, F32)
        live = (n >= nc).astype(F32)
        sides = ((q0, k0, v0, g0, do0, dq0, dk0, dv0, dg0), (q1, k1, v1, g1, do1, dq1, dk1, dv1, dg1))
        for d, (q_ref, k_ref, v_ref, g_ref, do_ref, dq_ref, dk_ref, dv_ref, dg_ref) in enumerate(sides):
            m = m_ref[d]
            q, k, v, g = q_ref[...], k_ref[...], v_ref[...], g_ref[...]
            dov = do_ref[...] * live
            for h in range(HEADS):
                b, tot, mid, qh, kh, vh = _gla_chunk(m, q, k, v, g, h)
                doh = dov[:, h * DV:(h + 1) * DV]
                s0 = ss_ref[d, 0, 0, h * DV:(h + 1) * DV, :]
                ds1 = ds_ref[d, h * DV:(h + 1) * DV, :]
                eb, ebm, emb, etb = jnp.exp(b), jnp.exp(b - mid), jnp.exp(mid - b), jnp.exp(tot - b)
                etot = jnp.exp(tot)
                qe, qm, km, kl = qh * eb, qh * ebm, kh * emb, kh * etb
                att = m * _dot(qm, km, 1, 1)
                ds_ref[d, h * DV:(h + 1) * DV, :] = ds1 * etot + _dot(doh, qe, 0, 0)
                dqe = _dot(doh, s0, 1, 0)
                datt = m * _dot(doh, vh, 1, 1)
                dqm = _dot(datt, km, 1, 0)
                dkm = _dot(datt, qm, 0, 0)
                dvh = _dot(att, doh, 0, 0) + _dot(kl, ds1, 1, 1)
                dkl = _dot(vh, ds1, 1, 0)
                dtot = etot * jnp.sum(ds1 * s0, axis=0, keepdims=True) + jnp.sum(dkl * kl, axis=0, keepdims=True)
                db = dqe * qe + dqm * qm - dkm * km - dkl * kl
                dgh = lax.dot_general(m, db, (((0,), (0,)), ((), ())), precision=HIGHEST,
                                      preferred_element_type=F32) + dtot
                dq_ref[:, h * DK:(h + 1) * DK] = (dqe * eb + dqm * ebm) * Q_SCALE
                dk_ref[:, h * DK:(h + 1) * DK] = dkm * emb + dkl * etb
                dv_ref[:, h * DV:(h + 1) * DV] = dvh
                dg_ref[:, h * DK:(h + 1) * DK] = dgh

    nt = p.shape[0]

    def operands(d):
        return [pl.BlockSpec((CHUNK, 512), lambda b, s: (rowblk(d, b, rev(s)), 4)),
                pl.BlockSpec((CHUNK, 512), lambda b, s: (rowblk(d, b, rev(s)), 5)),
                pl.BlockSpec((CHUNK, 1024), lambda b, s: (rowblk(d, b, rev(s)), 3)),
                pl.BlockSpec((CHUNK, 512), lambda b, s: (rowblk(d, b, rev(s)), d)),
                pl.BlockSpec((CHUNK, 1024), lambda b, s: (xblk(d, b, rev(s)), 0))]

    def results(d):
        row = lambda b, s: (rowblk(d, b, rev(s)), 0)
        return [pl.BlockSpec((CHUNK, 512), row), pl.BlockSpec((CHUNK, 512), row), pl.BlockSpec((CHUNK, 1024), row),
                pl.BlockSpec((CHUNK, 512), row)]

    shapes = [jax.ShapeDtypeStruct((nt, 512), F32), jax.ShapeDtypeStruct((nt, 512), F32),
              jax.ShapeDtypeStruct((nt, 1024), F32), jax.ShapeDtypeStruct((nt, 512), F32)]
    out = pl.pallas_call(
        kern, grid=(bl, ns),
        in_specs=operands(0) + operands(1) + [
            pl.BlockSpec((2, CHUNK, CHUNK), lambda b, s: (0, 0, 0)),
            pl.BlockSpec((2, 1, 1, HEADS * DV, DK), lambda b, s: (0, b, rev(s), 0, 0))],
        out_specs=results(0) + results(1), out_shape=shapes + shapes,
        scratch_shapes=[pltpu.VMEM((2, HEADS * DV, DK), F32)],
        compiler_params=_cparams(("parallel", "arbitrary")), name="gla_bwd")(
            p, p, p, g2, do, p, p, p, g2, do, mmats, ssave)
    return out[:4], out[4:]


CONV_CT = 256
CONV_PAD = 16
CONV_RC = 128


def _conv_fill(zp, z_ref, t):
    zp[0:CONV_PAD, :] = jnp.zeros((CONV_PAD, CONV_CT), F32)
    zp[CONV_PAD + t:2 * CONV_PAD + t, :] = jnp.zeros((CONV_PAD, CONV_CT), F32)
    zp[CONV_PAD:CONV_PAD + t, :] = z_ref[...]


def _dwconv(name, z, w, bias, bl, t, flip):
    def kern(z_ref, w_ref, b_ref, o_ref, zp):
        _conv_fill(zp, z_ref, t)
        for r in range(0, t, CONV_RC):
            acc = jnp.broadcast_to(b_ref[...], (CONV_RC, CONV_CT))
            for j in range(CONV_W):
                off = (CONV_W - j) if flip else (j + 1)
                acc = acc + w_ref[j:j + 1, :] * zp[r + off:r + off + CONV_RC, :]
            o_ref[r:r + CONV_RC, :] = acc

    return pl.pallas_call(
        kern, grid=(bl, 1024 // CONV_CT),
        in_specs=[pl.BlockSpec((t, CONV_CT), lambda b, c: (b, c)),
                  pl.BlockSpec((32, CONV_CT), lambda b, c: (0, c)),
                  pl.BlockSpec((1, CONV_CT), lambda b, c: (0, c))],
        out_specs=pl.BlockSpec((t, CONV_CT), lambda b, c: (b, c)),
        out_shape=jax.ShapeDtypeStruct(z.shape, F32),
        scratch_shapes=[pltpu.VMEM((t + 2 * CONV_PAD, CONV_CT), F32)],
        compiler_params=_cparams(("parallel", "parallel")), name=name)(z, w, bias)


def _dwconv_wgrad(z, dzc, bl, t):
    def kern(z_ref, d_ref, dw_ref, db_ref, zp):
        b = pl.program_id(1)

        @pl.when(b == 0)
        def _():
            dw_ref[...] = jnp.zeros(dw_ref.shape, F32)
            db_ref[...] = jnp.zeros(db_ref.shape, F32)
        _conv_fill(zp, z_ref, t)
        for j in range(CONV_W):
            acc = jnp.zeros((CONV_RC, CONV_CT), F32)
            for r in range(0, t, CONV_RC):
                acc = acc + d_ref[r:r + CONV_RC, :] * zp[r + j + 1:r + j + 1 + CONV_RC, :]
            dw_ref[j:j + 1, :] += jnp.sum(acc, axis=0, keepdims=True)
        db_ref[...] += jnp.sum(d_ref[...], axis=0, keepdims=True)

    return pl.pallas_call(
        kern, grid=(1024 // CONV_CT, bl),
        in_specs=[pl.BlockSpec((t, CONV_CT), lambda c, b: (b, c)),
                  pl.BlockSpec((t, CONV_CT), lambda c, b: (b, c))],
        out_specs=[pl.BlockSpec((32, CONV_CT), lambda c, b: (0, c)),
                   pl.BlockSpec((1, CONV_CT), lambda c, b: (0, c))],
        out_shape=[jax.ShapeDtypeStruct((32, 1024), F32), jax.ShapeDtypeStruct((1, 1024), F32)],
        scratch_shapes=[pltpu.VMEM((t + 2 * CONV_PAD, CONV_CT), F32)],
        compiler_params=_cparams(("parallel", "arbitrary")), name="dwconv_wgrad")(z, dzc)


def _ffn_fwd(tag, xin, n_tiles, g, sh, sc, gate, w_gu, w_down, tpb, nb):
    rows = n_tiles * TM
    rm = functools.partial(_rowmap, tpb=tpb, nb=nb)
    (u,) = rm(tag + "_norm", lambda i, x, g_, sh_, sc_: (_modnorm(x, g_, sh_, sc_),), n_tiles,
              [tok(xin), const(g), mod(sh), mod(sc)], [("tok", rows, D, BF16)])
    ab, hm = _swiglu_fwd(tag + "_gu", u, w_gu)
    f = _mm(tag + "_down", hm, w_down)
    (xout,) = rm(tag + "_res", lambda i, x, f_, gt: (x + 0.5 * gt * f_,), n_tiles,
                 [tok(xin), tok(f), mod(gate)], [("tok", rows, D, F32)])
    return xout, (u, ab, hm, f)


def _ffn_bwd(tag, xin, saved, dxout, dx_clamp, n_tiles, g, sh, sc, gate, w_gu, w_down, tpb, nb):
    u, ab, hm, f = saved
    rows = n_tiles * TM
    rm = functools.partial(_rowmap, tpb=tpb, nb=nb)

    def mask(i):
        return 1.0 if dx_clamp is None else (i <= dx_clamp).astype(F32)

    def b1(i, dx, f_, gt):
        dx = dx * mask(i)
        return (0.5 * gt * dx, jnp.sum(0.5 * f_ * dx, axis=0, keepdims=True))
    df, dgate = rm(tag + "_bres", b1, n_tiles, [tok(dxout, clamp=dx_clamp), tok(f), mod(gate)],
                   [("tok", rows, D, BF16), ("modacc", D)])
    dw_down = _mm_tn(tag + "_wdown", hm, df)
    dab = _swiglu_bwd(tag + "_bdown", df, w_down, ab)
    du = _mm(tag + "_bgu", dab, w_gu, trans_b=True)
    dw_gu = _mm_tn(tag + "_wgu", u, dab, col_shards=True)

    def b3(i, x, g_, sh_, sc_, du_, dx):
        _, vjp = jax.vjp(_modnorm, x, g_, sh_, sc_)
        dxn, dg, dsh, dsc = vjp(du_)
        return (dx * mask(i) + dxn, dg, dsh, dsc)
    dxin, dg, dsh, dsc = rm(tag + "_bnorm", b3, n_tiles,
                            [tok(xin), const(g), mod(sh), mod(sc), tok(du), tok(dxout, clamp=dx_clamp)],
                            [("tok", rows, D, F32), ("acc", 1, D), ("modacc", D), ("modacc", D)])
    return dxin, dict(g=dg, sh=dsh, sc=dsc, gate=dgate, w_gu=dw_gu, w_down=dw_down)


def _perm_in_cols(w):
    pad = jnp.zeros(w.shape[:-1] + (D_INP - D_IN,), w.dtype)
    return jnp.concatenate([w[..., :5120], w[..., 5152:7200], w[..., 5120:5152], pad], axis=-1)


def _unperm_in_cols(w):
    return jnp.concatenate([w[..., :5120], w[..., LR_COL:LR_COL + 32], w[..., 5120:LR_COL]], axis=-1)


def _local_step(x, c, ctx, target, wts):
    bl, t, _ = x.shape
    tc = ctx.shape[1]
    nx_rows, nc_rows = bl * t, bl * tc
    nt_rows = nx_rows + nc_rows
    tpb = t // TM
    nxt, ntt = nx_rows // TM, nt_rows // TM
    nb = bl
    rm = functools.partial(_rowmap, tpb=tpb, nb=nb)
    last_x = nxt - 1

    x0 = jnp.concatenate([x.reshape(nx_rows, D), ctx.reshape(nc_rows, D)], axis=0)
    tgt = target.reshape(nx_rows, D)

    cc = jnp.concatenate([c, wts["c_ctx"].reshape(1, D), jnp.zeros((8 - bl - 1, D), F32)], axis=0)
    modv = _mm("mod_fwd", cc, wts["w_mod"], a_fn=_silu, bias=wts["b_mod"])
    mods = [modv[:nb + 1, k * D:(k + 1) * D].reshape(nb + 1, 1, D) for k in range(9)]

    x1, sv1 = _ffn_fwd("ffn1", x0, ntt, wts["g_ffn1"], mods[0], mods[1], mods[2], wts["w1_gu"], wts["w1_down"],
                       tpb, nb)
    (u2,) = rm("in_norm", lambda i, x_, g_, sh_, sc_: (_modnorm(x_, g_, sh_, sc_),), ntt,
               [tok(x1), const(wts["g_mix"]), mod(mods[3]), mod(mods[4])], [("tok", nt_rows, D, BF16)])
    w_inp = wts["w_in_p"]
    p = _mm("in_proj", u2, w_inp)

    waf, wab, baf, bab = wts["w_alpha_f_pad"], wts["w_alpha_b_pad"], wts["b_alpha_f"], wts["b_alpha_b"]

    def dec_fwd(i, lr, wf, wb, bf_, bb_):
        zf = _dot(lr, wf, 1, 0) + bf_
        zb = _dot(lr, wb, 1, 0) + bb_
        return (jnp.concatenate([jax.nn.log_sigmoid(zf) / TAU, jax.nn.log_sigmoid(zb) / TAU], axis=1),)
    (gfb,) = rm("decay_fwd", dec_fwd, ntt, [tok(p, 128, LR_COL // 128), const(waf), const(wab), const(baf), const(bab)],
                [("tok", nt_rows, 1024, F32)])
    g2 = gfb
    tri = jnp.tril(jnp.ones((CHUNK, CHUNK), F32))
    mmats = jnp.stack([tri, tri.T])
    *o2, ssave = _gla_fwd(p, g2, mmats, bl, t, tc)

    gn_g = wts["gla_norm_g"]

    def gla_out(of, ob, og, gn):
        o = of + ob
        parts = []
        for h in range(HEADS):
            oh = o[:, h * DV:(h + 1) * DV]
            parts.append(oh * lax.rsqrt(jnp.mean(oh * oh, axis=-1, keepdims=True) + EPS))
        return jnp.concatenate(parts, axis=1) * gn * _silu(og)
    (yg_in,) = rm("gla_out", lambda i, of, ob, og, gn: (gla_out(of, ob, og, gn),), nxt,
                  [tok(o2[0]), tok(o2[1]), tok(p, 1024, 4), const(gn_g)], [("tok", nx_rows, D, BF16)])
    y_gla = _mm("gla_proj", yg_in, wts["w_gla_out"])

    (z,) = rm("glu", lambda i, a, b: (a * jax.nn.sigmoid(b),), nxt, [tok(p, 1024, 0), tok(p, 1024, 1)],
              [("tok", nx_rows, D, F32)])
    dw_w = jnp.concatenate([wts["dw_weight"], jnp.zeros((1, D), F32)], axis=0)
    zc = _dwconv("dwconv_fwd", z, dw_w, wts["dw_bias"], bl, t, False)

    def ln_silu(zc_, g_, b_):
        mu = jnp.mean(zc_, axis=-1, keepdims=True)
        var = jnp.mean(jnp.square(zc_ - mu), axis=-1, keepdims=True)
        return _silu((zc_ - mu) * lax.rsqrt(var + EPS) * g_ + b_)
    ln_g, ln_b = wts["conv_ln_g"], wts["conv_ln_b"]
    (zl,) = rm("conv_ln", lambda i, zc_, g_, b_: (ln_silu(zc_, g_, b_),), nxt, [tok(zc), const(ln_g), const(ln_b)],
               [("tok", nx_rows, D, BF16)])
    y_conv = _mm("conv_proj", zl, wts["w_conv_out"])

    (mg,) = rm("merge", lambda i, ga, gb, yc, yg: (jax.nn.sigmoid(ga) * yc + jax.nn.sigmoid(gb) * yg,), nxt,
               [tok(p, 1024, 5), tok(p, 1024, 6), tok(y_conv), tok(y_gla)], [("tok", nx_rows, D, BF16)])
    mix = _mm("out_proj", mg, wts["w_out"])
    (x2,) = rm("mix_res", lambda i, x_, mx_, gt: (x_ + gt * mx_,), nxt, [tok(x1), tok(mix), mod(mods[5])],
               [("tok", nx_rows, D, F32)])

    x3, sv2 = _ffn_fwd("ffn2", x2, nxt, wts["g_ffn2"], mods[6], mods[7], mods[8], wts["w2_gu"], wts["w2_down"],
                       tpb, nb)
    g_fin = wts["g_final"].reshape(1, D)

    def head(i, x_, g_, tg):
        y, vjp = jax.vjp(_rms, x_, g_)
        diff = y - tg
        dx, dg = vjp(diff * (1.0 / D))
        loss = 0.5 * jnp.sum(jnp.mean(diff * diff, axis=-1, keepdims=True))
        return dx, dg, loss
    dx3, dg_final, loss_acc = rm("loss_head", head, nxt, [tok(x3), const(g_fin), tok(tgt)],
                                 [("tok", nx_rows, D, F32), ("acc", 1, D), ("acc", 8, 128)])
    loss = loss_acc[0, 0]

    grads = {}
    dx2, gf2 = _ffn_bwd("ffn2", x2, sv2, dx3, None, nxt, wts["g_ffn2"], mods[6], mods[7], mods[8],
                        wts["w2_gu"], wts["w2_down"], tpb, nb)
    grads.update(g_ffn2=gf2["g"], w2_gu=gf2["w_gu"], w2_down=gf2["w_down"])

    dmix, dgate5 = rm("mix_bres", lambda i, dx, mx_, gt: (gt * dx, jnp.sum(mx_ * dx, axis=0, keepdims=True)), nxt,
                      [tok(dx2), tok(mix), mod(mods[5])], [("tok", nx_rows, D, BF16), ("modacc", D)])
    dmg = _mm("out_bproj", dmix, wts["w_out"], trans_b=True)
    grads["w_out"] = _mm_tn("out_wgrad", mg, dmix)

    def merge_bwd(i, dm, ga, gb, yc, yg):
        keep = (i <= last_x).astype(F32)
        dm = dm * keep
        sa, sb = jax.nn.sigmoid(ga), jax.nn.sigmoid(gb)
        return dm * sa, dm * sb, dm * yc * sa * (1 - sa), dm * yg * sb * (1 - sb)
    cl = dict(clamp=last_x)
    dyc, dyg, dga, dgb = rm("merge_bwd", merge_bwd, ntt,
                            [tok(dmg, **cl), tok(p, 1024, 5, last_x), tok(p, 1024, 6, last_x), tok(y_conv, **cl),
                             tok(y_gla, **cl)],
                            [("tok", nt_rows, D, BF16)] * 4)

    dzl = _mm("conv_bproj", dyc, wts["w_conv_out"], trans_b=True, rows=nx_rows)
    grads["w_conv_out"] = _mm_tn("conv_wgrad", zl, dyc, rows=nx_rows)

    def ln_bwd(i, zc_, g_, b_, dz_):
        _, vjp = jax.vjp(ln_silu, zc_, g_, b_)
        return vjp(dz_)
    dzc, dln_g, dln_b = rm("conv_ln_bwd", ln_bwd, nxt, [tok(zc), const(ln_g), const(ln_b), tok(dzl)],
                           [("tok", nx_rows, D, F32), ("acc", 1, D), ("acc", 1, D)])
    dz = _dwconv("dwconv_bwd", dzc, dw_w, jnp.zeros((1, D), F32), bl, t, True)
    ddw, ddb = _dwconv_wgrad(z, dzc, bl, t)
    grads.update(conv_ln_g=dln_g, conv_ln_b=dln_b, dw_weight=ddw[:CONV_W], dw_bias=ddb)

    def glu_bwd(i, dz_, a, b):
        keep = (i <= last_x).astype(F32)
        dz_ = dz_ * keep
        s = jax.nn.sigmoid(b)
        return (jnp.concatenate([dz_ * s, dz_ * a * s * (1 - s)], axis=1),)
    (dconv,) = rm("glu_bwd", glu_bwd, ntt, [tok(dz, **cl), tok(p, 1024, 0, last_x), tok(p, 1024, 1, last_x)],
                  [("tok", nt_rows, 2048, BF16)])

    dyg_in = _mm("gla_bproj", dyg, wts["w_gla_out"], trans_b=True, rows=nx_rows)
    grads["w_gla_out"] = _mm_tn("gla_wgrad", yg_in, dyg, rows=nx_rows)

    def gla_out_bwd(i, of, ob, og, gn, dy):
        _, vjp = jax.vjp(gla_out, of, ob, og, gn)
        do_, _, dog_, dgn_ = vjp(dy)
        return do_, dog_, dgn_
    do, dog_x, dgn = rm("gla_out_bwd", gla_out_bwd, nxt,
                        [tok(o2[0]), tok(o2[1]), tok(p, 1024, 4), const(gn_g), tok(dyg_in)],
                        [("tok", nx_rows, D, F32), ("tok", nx_rows, D, BF16), ("acc", 1, D)])
    grads["gla_norm_g"] = dgn
    dog = jnp.concatenate([dog_x, jnp.zeros((nc_rows, D), BF16)], axis=0)

    dq2, dk2, dv2, dg2 = zip(*_gla_bwd(p, g2, mmats, ssave, do, bl, t, tc))

    def dec_bwd(i, lr, wf, wb, bf_, bb_, dgf, dgb_):
        zf = _dot(lr, wf, 1, 0) + bf_
        zb = _dot(lr, wb, 1, 0) + bb_
        dzf = dgf * (1 - jax.nn.sigmoid(zf)) * (1.0 / TAU)
        dzb = dgb_ * (1 - jax.nn.sigmoid(zb)) * (1.0 / TAU)
        dlr = _dot(dzf, wf, 1, 1) + _dot(dzb, wb, 1, 1)
        return (dlr, _dot(lr, dzf, 0, 0), _dot(lr, dzb, 0, 0), jnp.sum(dzf, axis=0, keepdims=True),
                jnp.sum(dzb, axis=0, keepdims=True))
    dlr, dwaf, dwab, dbaf, dbab = rm(
        "decay_bwd", dec_bwd, ntt,
        [tok(p, 128, LR_COL // 128), const(waf), const(wab), const(baf), const(bab), tok(dg2[0]), tok(dg2[1])],
        [("tok", nt_rows, 128, BF16), ("acc", 128, 512), ("acc", 128, 512), ("acc", 1, 512), ("acc", 1, 512)])
    grads.update(w_alpha_f=dwaf[:LOWRANK], w_alpha_b=dwab[LOWRANK:2 * LOWRANK], b_alpha_f=dbaf, b_alpha_b=dbab)

    dq, dk, dv = rm("gla_sum", lambda i, q0, q1, k0, k1, v0, v1: (q0 + q1, k0 + k1, v0 + v1), ntt,
                    [tok(dq2[0]), tok(dq2[1]), tok(dk2[0]), tok(dk2[1]), tok(dv2[0]), tok(dv2[1])],
                    [("tok", nt_rows, 512, BF16), ("tok", nt_rows, 512, BF16), ("tok", nt_rows, 1024, BF16)])

    dp = jnp.concatenate([dconv, dq, dk, dv, dog, dga, dgb, dlr], axis=1)
    du2 = _mm("in_bproj", dp, w_inp, trans_b=True)
    grads["w_in_p"] = _mm_tn("in_wgrad", u2, dp)

    def in_norm_bwd(i, x_, g_, sh_, sc_, du_, dx):
        keep = (i <= last_x).astype(F32)
        _, vjp = jax.vjp(_modnorm, x_, g_, sh_, sc_)
        dxn, dg, dsh, dsc = vjp(du_)
        return (dx * keep + dxn, dg, dsh, dsc)
    dx1, dg_mix, dsh3, dsc4 = rm("in_norm_bwd", in_norm_bwd, ntt,
                                 [tok(x1), const(wts["g_mix"]), mod(mods[3]), mod(mods[4]), tok(du2), tok(dx2, **cl)],
                                 [("tok", nt_rows, D, F32), ("acc", 1, D), ("modacc", D), ("modacc", D)])
    grads["g_mix"] = dg_mix

    dx0, gf1 = _ffn_bwd("ffn1", x0, sv1, dx1, None, ntt, wts["g_ffn1"], mods[0], mods[1], mods[2],
                        wts["w1_gu"], wts["w1_down"], tpb, nb)
    grads.update(g_ffn1=gf1["g"], w1_gu=gf1["w_gu"], w1_down=gf1["w_down"])
    grad_x = dx0[:nx_rows].reshape(bl, t, D)

    dmods = [gf1["sh"], gf1["sc"], gf1["gate"], dsh3, dsc4, dgate5, gf2["sh"], gf2["sc"], gf2["gate"]]
    dmod = jnp.concatenate(
        [jnp.concatenate([a.reshape(a.shape[0], D), jnp.zeros((8 - a.shape[0], D), F32)], axis=0) for a in dmods],
        axis=1)
    dsc_ = _mm("mod_bproj", dmod, wts["w_mod"], trans_b=True)

    def silu_bwd(cc_ref, d_ref, dm_ref, dcc_ref, scc_ref, db_ref):
        cc_ = cc_ref[...]
        s = jax.nn.sigmoid(cc_)
        dcc_ref[...] = d_ref[...] * (s * (1 + cc_ * (1 - s)))
        scc_ref[...] = (cc_ * s).astype(BF16)
        db_ref[...] = jnp.sum(dm_ref[...], axis=0, keepdims=True)
    dcc, scc, db_mod = pl.pallas_call(
        silu_bwd, out_shape=[jax.ShapeDtypeStruct((8, D), F32), jax.ShapeDtypeStruct((8, D), BF16),
                             jax.ShapeDtypeStruct((1, 9 * D), F32)], name="mod_silu_bwd")(cc, dsc_, dmod)
    grads["c_ctx"] = dcc[nb]
    grads["b_mod"] = db_mod
    grads["w_mod"] = _mm_tn("mod_wgrad", scc, dmod, col_shards=True)
    grads["g_final"] = dg_final.reshape(D)
    return loss, grad_x, grads


ANY = pl.BlockSpec(memory_space=pl.ANY)


def _place():
    x, y, c = lax.axis_index("x"), lax.axis_index("y"), lax.axis_index("c")
    chips = [(1 - x, y), (x, 1 - y), (1 - x, 1 - y)]
    return x, y, c, chips


def _remote(send_sems, recv_sems):
    def copy(k, src, dst, to):
        return pltpu.make_async_remote_copy(src_ref=src, dst_ref=dst, send_sem=send_sems.at[k],
                                            recv_sem=recv_sems.at[k], device_id=to, device_id_type=MESH)
    return copy


def _sems(n):
    return [pltpu.SemaphoreType.DMA((n,)), pltpu.SemaphoreType.DMA((n,))]


def _gather_weights(shards):
    n = len(shards)

    def body(*refs):
        ins, outs = refs[:n], refs[n:2 * n]
        copy = _remote(refs[2 * n], refs[2 * n + 1])
        x, y, c, chips = _place()
        me = 2 * x + y
        sibling = (x, y, 1 - c)
        started = []

        def rows(i, hc):
            hr = ins[i].shape[0] // 2
            return pl.ds(hc * hr, hr)

        for i in range(n):
            started.append(copy(7 * i + 6, ins[i], outs[i].at[me], sibling))
            started[-1].start()
            for j, (px, py) in enumerate(chips):
                started.append(copy(7 * i + j, ins[i].at[rows(i, c), :], outs[i].at[me, rows(i, c), :], (px, py, c)))
                started[-1].start()
        for i in range(n):
            for j, (px, py) in enumerate(chips):
                half = outs[i].at[2 * px + py, rows(i, c), :]
                copy(7 * i + j, half, half, (px, py, c)).wait_recv()
                started.append(copy(7 * i + 3 + j, half, half, sibling))
                started[-1].start()
        for i in range(n):
            copy(7 * i + 6, ins[i], outs[i].at[me], sibling).wait_recv()
            for j, (px, py) in enumerate(chips):
                other = outs[i].at[2 * px + py, rows(i, 1 - c), :]
                copy(7 * i + 3 + j, other, other, sibling).wait_recv()
        for cp in started:
            cp.wait_send()

    return pl.pallas_call(
        body, out_shape=[jax.ShapeDtypeStruct((4,) + s.shape, s.dtype) for s in shards], in_specs=[ANY] * n,
        out_specs=[ANY] * n, scratch_shapes=_sems(7 * n), name="gather_weights")(*shards)


def _swap_halves(gs):
    n = len(gs)

    def body(*refs):
        ins, outs = refs[:n], refs[n:2 * n]
        copy = _remote(refs[2 * n], refs[2 * n + 1])
        x, y, c, _ = _place()
        cps = []
        for i in range(n):
            hr = ins[i].shape[1] // 2
            cps.append(copy(i, ins[i].at[:, pl.ds((1 - c) * hr, hr), :], outs[i], (x, y, 1 - c)))
            cps[-1].start()
        for cp in cps:
            cp.wait()

    return pl.pallas_call(
        body, out_shape=[jax.ShapeDtypeStruct((4, g.shape[1] // 2, g.shape[2]), g.dtype) for g in gs],
        in_specs=[ANY] * n, out_specs=[ANY] * n, scratch_shapes=_sems(n), name="grad_swap_halves")(*gs)


def _row_tile(hr):
    return hr if hr <= 256 else _pick(hr, (256, 176, 128, 64, 32, 16))


def _add_halves(name, g, r, place):
    hr = r.shape[1]
    tr = _row_tile(hr)
    nblk = hr // tr

    def kern(p_ref, g_ref, r_ref, o_ref):
        o_ref[...] = (g_ref[...].astype(F32) + r_ref[...].astype(F32)).astype(o_ref.dtype)

    blk = (1, tr, g.shape[2])
    return pl.pallas_call(
        kern,
        grid_spec=pltpu.PrefetchScalarGridSpec(
            num_scalar_prefetch=1, grid=(4, nblk),
            in_specs=[pl.BlockSpec(blk, lambda j, i, p: (j, p[0] * nblk + i, 0)),
                      pl.BlockSpec(blk, lambda j, i, p: (j, i, 0))],
            out_specs=pl.BlockSpec(blk, lambda j, i, p: (j, i, 0))),
        out_shape=jax.ShapeDtypeStruct(r.shape, r.dtype),
        compiler_params=_cparams(("parallel", "parallel")), name=name)(place, g, r)


def _scatter_chips(cs):
    n = len(cs)

    def body(*refs):
        ins, outs = refs[:n], refs[n:2 * n]
        copy = _remote(refs[2 * n], refs[2 * n + 1])
        x, y, c, chips = _place()
        me = 2 * x + y
        sends = []
        for i in range(n):
            for j, (px, py) in enumerate(chips):
                sends.append(copy(3 * i + j, ins[i].at[2 * px + py], outs[i].at[me], (px, py, c)))
                sends[-1].start()
        for i in range(n):
            for j, (px, py) in enumerate(chips):
                src = 2 * px + py
                copy(3 * i + j, ins[i].at[src], outs[i].at[src], (px, py, c)).wait_recv()
        for cp in sends:
            cp.wait_send()

    return pl.pallas_call(
        body, out_shape=[jax.ShapeDtypeStruct(a.shape, a.dtype) for a in cs], in_specs=[ANY] * n,
        out_specs=[ANY] * n, scratch_shapes=_sems(3 * n), name="grad_scatter_chips")(*cs)


def _sum_chips(name, cs, r, place):
    hr = r.shape[1]
    tr = _row_tile(hr)
    nblk = hr // tr

    def kern(p_ref, c_ref, r0, r1, r2, r3, o_ref):
        me = p_ref[1]
        acc = None
        for k, rk in enumerate((r0, r1, r2, r3)):
            val = jnp.where(me == k, c_ref[0].astype(F32), rk[0].astype(F32))
            acc = val if acc is None else acc + val
        o_ref[...] = acc

    blk = (1, tr, r.shape[2])

    def slot(k):
        return lambda i, p: (jnp.where(p[1] == k, (k + 1) % 4, k), i, 0)

    return pl.pallas_call(
        kern,
        grid_spec=pltpu.PrefetchScalarGridSpec(
            num_scalar_prefetch=1, grid=(nblk,),
            in_specs=[pl.BlockSpec(blk, lambda i, p: (p[1], i, 0))] + [pl.BlockSpec(blk, slot(k)) for k in range(4)],
            out_specs=pl.BlockSpec((tr, r.shape[2]), lambda i, p: (p[0] * nblk + i, 0))),
        out_shape=jax.ShapeDtypeStruct((2 * hr, r.shape[2]), F32),
        compiler_params=_cparams(("parallel",)), name=name)(place, cs, r, r, r, r)


def _join_halves(fs):
    n = len(fs)

    def body(*refs):
        ins, outs = refs[:n], refs[n:2 * n]
        copy = _remote(refs[2 * n], refs[2 * n + 1])
        x, y, c, _ = _place()
        cps = []
        for i in range(n):
            hr = ins[i].shape[0] // 2
            cps.append(copy(i, ins[i].at[pl.ds(c * hr, hr), :], outs[i].at[pl.ds(c * hr, hr), :], (x, y, 1 - c)))
            cps[-1].start()
        for i in range(n):
            hr = ins[i].shape[0] // 2
            other = outs[i].at[pl.ds((1 - c) * hr, hr), :]
            copy(i, other, other, (x, y, 1 - c)).wait_recv()
        for cp in cps:
            cp.wait_send()

    return pl.pallas_call(
        body, out_shape=[jax.ShapeDtypeStruct(f.shape, f.dtype) for f in fs], in_specs=[ANY] * n,
        out_specs=[ANY] * n, input_output_aliases={i: i for i in range(n)}, scratch_shapes=_sems(n),
        name="grad_join_halves")(*fs)


def _allreduce_small(v):
    def body(x_ref, out_ref, gath, send_sems, recv_sems, local_sem):
        x, y, c, chips = _place()
        me, sibling = (x, y, c), (x, y, 1 - c)

        def slot(px, py, pc):
            return gath.at[4 * px + 2 * py + pc]

        def copy(k, block, to, src=None):
            return pltpu.make_async_remote_copy(
                src_ref=slot(*block) if src is None else src, dst_ref=slot(*block), send_sem=send_sems.at[k],
                recv_sem=recv_sems.at[k], device_id=to, device_id_type=MESH)

        mine = pltpu.make_async_copy(x_ref, slot(*me), local_sem)
        mine.start()
        first = [copy(0, me, sibling, src=x_ref)]
        first += [copy(1 + j, me, (*chip, c), src=x_ref) for j, chip in enumerate(chips)]
        for cp in first:
            cp.start()
        passed = [copy(4 + j, (*chip, c), sibling) for j, chip in enumerate(chips)]
        for j, chip in enumerate(chips):
            copy(1 + j, (*chip, c), me).wait_recv()
            passed[j].start()
        copy(0, sibling, me).wait_recv()
        for j, chip in enumerate(chips):
            copy(4 + j, (*chip, 1 - c), me).wait_recv()
        for cp in first + passed:
            cp.wait_send()
        mine.wait()
        acc = gath[0]
        for k in range(1, 8):
            acc = acc + gath[k]
        out_ref[...] = acc

    vm = pl.BlockSpec(memory_space=pltpu.VMEM)
    return pl.pallas_call(
        body, out_shape=jax.ShapeDtypeStruct(v.shape, F32), in_specs=[vm], out_specs=vm,
        scratch_shapes=[pltpu.VMEM((8,) + v.shape, F32), pltpu.SemaphoreType.DMA((7,)),
                        pltpu.SemaphoreType.DMA((7,)), pltpu.SemaphoreType.DMA(())],
        name="allreduce_small")(v)


def _adamw(name, w, g, m, v):
    r, cols = w.shape
    budget = 262144
    tr = r if r * cols <= budget else next(c for c in (256, 128, 64, 32, 16, 8) if r % c == 0 and c * cols <= budget)

    def kern(w_ref, g_ref, m_ref, v_ref, d_ref, nm_ref, nv_ref):
        gv = g_ref[...]
        nm = ADAM_B1 * m_ref[...] + (1.0 - ADAM_B1) * gv
        nv = ADAM_B2 * v_ref[...] + (1.0 - ADAM_B2) * jnp.square(gv)
        m_hat = nm / (1.0 - ADAM_B1 ** ADAM_STEP)
        v_hat = nv / (1.0 - ADAM_B2 ** ADAM_STEP)
        d_ref[...] = -ADAM_LR * (m_hat / (jnp.sqrt(v_hat) + ADAM_EPS) + ADAM_WD * w_ref[...])
        nm_ref[...] = nm
        nv_ref[...] = nv

    spec = pl.BlockSpec((tr, cols), lambda i: (i, 0))
    shp = jax.ShapeDtypeStruct((r, cols), F32)
    return pl.pallas_call(kern, grid=(r // tr,), in_specs=[spec] * 4, out_specs=[spec] * 3, out_shape=[shp] * 3,
                          compiler_params=_cparams(("parallel",)), name=name)(w, g, m, v)


SHARDED = (("w_mod", 1), ("w1_gu", 1), ("w1_down", 0), ("w_in", 1), ("dw_weight", 1), ("w_conv_out", 0),
           ("w_alpha_f", 1), ("w_alpha_b", 1), ("w_gla_out", 0), ("w_out", 0), ("w2_gu", 1), ("w2_down", 0))
REPLICATED = ("c_ctx", "b_mod", "g_ffn1", "g_mix", "dw_bias", "conv_ln_g", "conv_ln_b", "b_alpha_f", "b_alpha_b",
              "gla_norm_g", "g_ffn2", "g_final")
WEIGHTS = ("c_ctx", "w_mod", "b_mod", "g_ffn1", "w1_gu", "w1_down", "g_mix", "w_in", "dw_weight", "dw_bias",
           "conv_ln_g", "conv_ln_b", "w_conv_out", "w_alpha_f", "b_alpha_f", "w_alpha_b", "b_alpha_b", "gla_norm_g",
           "w_gla_out", "w_out", "g_ffn2", "w2_gu", "w2_down", "g_final")
MATRICES = ("w_mod", "w1_gu", "w1_down", "w_in", "w_conv_out", "w_gla_out", "w_out", "w2_gu", "w2_down")


def _pack_flat(parts, align):
    flat = jnp.concatenate([p.reshape(-1) for p in parts])
    pad = (-flat.shape[0]) % align
    return jnp.concatenate([flat, jnp.zeros((pad,), flat.dtype)]).reshape(-1, 1024)


def _unpack_flat(flat2d, shapes):
    flat = flat2d.reshape(-1)
    out, off = [], 0
    for s in shapes:
        n = math.prod(s)
        out.append(flat[off:off + n].reshape(s))
        off += n
    return out


def kernel(x, c, ctx, c_ctx, w_mod, b_mod, g_ffn1, w1_gu, w1_down, g_mix, w_in, dw_weight, dw_bias, conv_ln_g, conv_ln_b, w_conv_out, w_alpha_f, b_alpha_f, w_alpha_b, b_alpha_b, gla_norm_g, w_gla_out, w_out, g_ffn2, w2_gu, w2_down, g_final, loss_target, m_c_ctx, m_w_mod, m_b_mod, m_g_ffn1, m_w1_gu, m_w1_down, m_g_mix, m_w_in, m_dw_weight, m_dw_bias, m_conv_ln_g, m_conv_ln_b, m_w_conv_out, m_w_alpha_f, m_b_alpha_f, m_w_alpha_b, m_b_alpha_b, m_gla_norm_g, m_w_gla_out, m_w_out, m_g_ffn2, m_w2_gu, m_w2_down, m_g_final, v_c_ctx, v_w_mod, v_b_mod, v_g_ffn1, v_w1_gu, v_w1_down, v_g_mix, v_w_in, v_dw_weight, v_dw_bias, v_conv_ln_g, v_conv_ln_b, v_w_conv_out, v_w_alpha_f, v_b_alpha_f, v_w_alpha_b, v_b_alpha_b, v_gla_norm_g, v_w_gla_out, v_w_out, v_g_ffn2, v_w2_gu, v_w2_down, v_g_final):
    given = dict(locals())
    w = {n: given[n] for n in WEIGHTS}
    m = {n: given["m_" + n] for n in WEIGHTS}
    v = {n: given["v_" + n] for n in WEIGHTS}

    def small_pack(dw, af, ab):
        return jnp.concatenate([dw, jnp.zeros((1, dw.shape[1]), F32), jnp.concatenate([af, ab], axis=1)], axis=0)

    shards = [w[n][0].astype(BF16) for n in MATRICES]
    shards.append(small_pack(w["dw_weight"][0], w["w_alpha_f"][0], w["w_alpha_b"][0]))
    got = dict(zip(MATRICES + ("small",), _gather_weights(shards)))

    wts = {n: w[n] for n in REPLICATED}
    for n in ("w_mod", "w1_gu", "w2_gu"):
        wts[n] = got[n]
    for n in ("w1_down", "w2_down", "w_conv_out", "w_gla_out", "w_out"):
        wts[n] = got[n].reshape(-1, D)
    wts["w_in_p"] = _perm_in_cols(jnp.concatenate([got["w_in"][j] for j in range(4)], axis=1))
    sm = got["small"]
    wts["dw_weight"] = jnp.concatenate([sm[j, :CONV_W] for j in range(4)], axis=1)
    zpad = jnp.zeros((128, HEADS * DK), BF16)
    w_af = jnp.concatenate([sm[j, 32:32 + LOWRANK, :DK] for j in range(4)], axis=1)
    w_ab = jnp.concatenate([sm[j, 32:32 + LOWRANK, DK:] for j in range(4)], axis=1)
    wts["w_alpha_f_pad"] = zpad.at[0:LOWRANK].set(w_af.astype(BF16))
    wts["w_alpha_b_pad"] = zpad.at[LOWRANK:2 * LOWRANK].set(w_ab.astype(BF16))

    loss, grad_x, grads = _local_step(x, c, ctx, loss_target, wts)
    loss = lax.psum(loss, ("x", "y", "c"))

    def pieces(n):
        if n == "w_in":
            g = _unperm_in_cols(grads["w_in_p"])
            return jnp.transpose(g.reshape(D, 4, D_IN // 4), (1, 0, 2))
        g = grads[n]
        if n in ("w_mod", "w1_gu", "w2_gu"):
            return g
        return g.reshape(4, g.shape[0] // 4, g.shape[1])
    gs = [pieces(n) for n in MATRICES]
    gs.append(jnp.stack([small_pack(grads["dw_weight"][:, 256 * j:256 * (j + 1)],
                                    grads["w_alpha_f"][:, DK * j:DK * (j + 1)],
                                    grads["w_alpha_b"][:, DK * j:DK * (j + 1)]) for j in range(4)]))
    tags = MATRICES + ("small",)
    place = jnp.stack([lax.axis_index("c"), 2 * lax.axis_index("x") + lax.axis_index("y")]).astype(jnp.int32)
    swapped = _swap_halves(gs)
    chip_sums = [_add_halves("grad_add_" + t, g, r, place) for t, g, r in zip(tags, gs, swapped)]
    landed = _scatter_chips(chip_sums)
    halves = [_sum_chips("grad_sum_" + t, cs, r, place) for t, cs, r in zip(tags, chip_sums, landed)]
    reduced = dict(zip(tags, _join_halves(halves)))
    g_shard = {n: reduced[n] for n in MATRICES}
    g_shard["dw_weight"] = reduced["small"][:CONV_W]
    g_shard["w_alpha_f"] = reduced["small"][32:32 + LOWRANK, :DK]
    g_shard["w_alpha_b"] = reduced["small"][32:32 + LOWRANK, DK:]

    rep_shapes = [w[n].shape for n in REPLICATED]
    small = _allreduce_small(_pack_flat([grads[n].reshape(w[n].shape) for n in REPLICATED], 8 * 1024))
    g_rep = dict(zip(REPLICATED, _unpack_flat(small, rep_shapes)))

    g_out, d_out, m_out, v_out = {}, {}, {}, {}
    for n, _ in SHARDED:
        s2 = w[n].shape[1:]
        d, nm, nv = _adamw("adamw_" + n, w[n].reshape(s2), g_shard[n], m[n].reshape(s2), v[n].reshape(s2))
        g_out[n] = g_shard[n].reshape(w[n].shape)
        d_out[n], m_out[n], v_out[n] = d.reshape(w[n].shape), nm.reshape(w[n].shape), nv.reshape(w[n].shape)
    pk = lambda t: _pack_flat([t[n] for n in REPLICATED], 8 * 1024)
    d, nm, nv = _adamw("adamw_vectors", pk(w), small, pk(m), pk(v))
    for n, dd, mm, vv in zip(REPLICATED, _unpack_flat(d, rep_shapes), _unpack_flat(nm, rep_shapes),
                             _unpack_flat(nv, rep_shapes)):
        g_out[n], d_out[n], m_out[n], v_out[n] = g_rep[n], dd, mm, vv

    return (loss, grad_x, *[g_out[n] for n in WEIGHTS], *[d_out[n] for n in WEIGHTS],
            *[m_out[n] for n in WEIGHTS], *[v_out[n] for n in WEIGHTS])
```

```python
import functools
import math

import jax
import jax.numpy as jnp
from jax import lax
from jax.experimental import pallas as pl
from jax.experimental.pallas import tpu as pltpu

F32, BF16 = jnp.float32, jnp.bfloat16
MESH = pl.DeviceIdType.MESH
HIGHEST = lax.Precision.HIGHEST

D = 1024
FF = 2816
HEADS, DK, DV = 4, 128, 256
LOWRANK = 16
CONV_W = 31
CHUNK = 64
TAU = 16.0
EPS = 1e-6
Q_SCALE = DK ** -0.5
TM = 256
D_IN = 7200
D_INP = 7296
LR_COL = 7168
VMEM_LIMIT = 52 * 1024 * 1024

ADAM_LR, ADAM_B1, ADAM_B2, ADAM_EPS, ADAM_WD, ADAM_STEP = 0.001, 0.9, 0.999, 1e-08, 0.01, 10


def _silu(x):
    return x * jax.nn.sigmoid(x)


def _rms(h, g):
    return h * lax.rsqrt(jnp.mean(h * h, axis=-1, keepdims=True) + EPS) * g


def _modnorm(x, g, shift, scale):
    return _rms(x, g) * (1 + scale) + shift


def _cparams(sem=None):
    return pltpu.CompilerParams(dimension_semantics=sem, vmem_limit_bytes=VMEM_LIMIT)


def _twice(v):
    return v, v


def tok(arr, width=None, cb=0, clamp=None):
    return ("tok", arr, arr.shape[1] if width is None else width, cb, clamp)


def mod(arr):
    return ("mod", arr)


def const(arr):
    return ("const", arr)


def _rowmap(name, body, n_tiles, ins, outs, *, tpb, nb):
    def modrow(i):
        return jnp.minimum(i // tpb, nb)

    in_specs, args = [], []
    for spec in ins:
        if spec[0] == "tok":
            _, arr, width, cb, clamp = spec
            if clamp is None:
                im = lambda i, cb=cb: (i, cb)
            else:
                im = lambda i, cb=cb, clamp=clamp: (jnp.minimum(i, clamp), cb)
            in_specs.append(pl.BlockSpec((TM, width), im))
        elif spec[0] == "mod":
            arr = spec[1]
            in_specs.append(pl.BlockSpec((1, 1, arr.shape[2]), lambda i: (modrow(i), 0, 0)))
        else:
            arr = spec[1]
            in_specs.append(pl.BlockSpec(arr.shape, lambda i, nd=arr.ndim: (0,) * nd))
        args.append(arr)
    out_specs, out_shapes = [], []
    for o in outs:
        if o[0] == "tok":
            _, rows, width, dtype = o
            out_shapes.append(jax.ShapeDtypeStruct((rows, width), dtype))
            out_specs.append(pl.BlockSpec((TM, width), lambda i: (i, 0)))
        elif o[0] == "tokT":
            _, rows, width, dtype = o
            out_shapes.append(jax.ShapeDtypeStruct((width, rows), dtype))
            out_specs.append(pl.BlockSpec((width, TM), lambda i: (0, i)))
        elif o[0] == "acc":
            _, rows, width = o
            out_shapes.append(jax.ShapeDtypeStruct((rows, width), F32))
            out_specs.append(pl.BlockSpec((rows, width), lambda i: (0, 0)))
        else:
            width = o[1]
            rows_visited = min((n_tiles - 1) // tpb, nb) + 1
            out_shapes.append(jax.ShapeDtypeStruct((rows_visited, 1, width), F32))
            out_specs.append(pl.BlockSpec((1, 1, width), lambda i: (modrow(i), 0, 0)))
    n_in = len(ins)

    def kern(*refs):
        i = pl.program_id(0)
        vals = []
        for r, spec in zip(refs[:n_in], ins):
            val = r[0] if spec[0] == "mod" else r[...]
            vals.append(val.astype(F32) if spec[0] == "tok" and val.dtype == BF16 else val)
        res = body(i, *vals)
        for r, o, val in zip(refs[n_in:], outs, res):
            if o[0] == "tok":
                r[...] = val.astype(r.dtype)
            elif o[0] == "tokT":
                r[...] = val.T.astype(r.dtype)
            elif o[0] == "acc":
                @pl.when(i == 0)
                def _():
                    r[...] = jnp.zeros(r.shape, F32)
                r[...] += jnp.broadcast_to(val, r.shape)
            else:
                first = jnp.logical_or(i == 0, modrow(i) != modrow(jnp.maximum(i - 1, 0)))

                @pl.when(first)
                def _():
                    r[...] = jnp.zeros(r.shape, F32)
                r[0] += val

    return pl.pallas_call(
        kern, grid=(n_tiles,), in_specs=in_specs, out_specs=out_specs, out_shape=out_shapes,
        compiler_params=_cparams(("arbitrary",)), name=name)(*args)


def _pick(n, cands):
    for c in cands:
        if n % c == 0:
            return c
    return n


def _mm(name, a, b, *, trans_b=False, out_dtype=F32, a_fn=None, bias=None, rows=None):
    m, k = a.shape if a.ndim == 2 else (a.shape[1], 2 * a.shape[2])
    m = m if rows is None else rows
    shard = b.shape[2] if b.ndim == 3 else None
    if trans_b:
        n = b.shape[-2]
        tk = _pick(shard, (2816, 2304, 1408, 1024)) if shard else (
            k if k <= 2816 else _pick(k, (2816, 2432, 2304, 2048, 1536, 1408, 1024, 512, 256, 128)))
        tn = _pick(n, (1024, 512, 384, 256, 128))
    else:
        n = 4 * shard if shard else b.shape[1]
        tk = k if k <= 2816 else _pick(k, (2816, 2432, 2304, 2048, 1536, 1408, 1024, 512, 256, 128))
        tn = _pick(shard, (512, 384, 1408, 256, 128)) if shard else _pick(n, (1024, 2432, 512, 384, 256, 128))
    tm = _pick(m, (1024, 512, 256))
    nk = k // tk
    per = shard // (tk if trans_b else tn) if shard else None
    dims = (((1,), (1,)), ((), ())) if trans_b else (((1,), (0,)), ((), ()))

    def kern(*refs):
        a_ref, b_ref = refs[0], refs[1]
        bias_ref = refs[2] if bias is not None else None
        o_ref, acc_ref = refs[-2], refs[-1]
        kk = pl.program_id(2)
        av = a_ref[...]
        if a_fn is not None:
            av = a_fn(av)
        p = lax.dot_general(av.astype(BF16), b_ref[...].astype(BF16), dims, preferred_element_type=F32)

        def finish(total):
            if bias_ref is not None:
                total = total + bias_ref[...]
            o_ref[...] = total.astype(o_ref.dtype)

        if nk == 1:
            finish(p)
        else:
            @pl.when(kk == 0)
            def _():
                acc_ref[...] = p

            @pl.when(kk > 0)
            def _():
                acc_ref[...] += p

            @pl.when(kk == nk - 1)
            def _():
                finish(acc_ref[...])

    if shard and trans_b:
        b_spec = pl.BlockSpec((None, tn, tk), lambda i, j, kk: (kk // per, j, kk % per))
    elif shard:
        b_spec = pl.BlockSpec((None, tk, tn), lambda i, j, kk: (j // per, kk, j % per))
    elif trans_b:
        b_spec = pl.BlockSpec((tn, tk), lambda i, j, kk: (j, kk))
    else:
        b_spec = pl.BlockSpec((tk, tn), lambda i, j, kk: (kk, j))
    if a.ndim == 3:
        pa = a.shape[2] // tk
        a_spec = pl.BlockSpec((None, tm, tk), lambda i, j, kk: (kk // pa, i, kk % pa))
    else:
        a_spec = pl.BlockSpec((tm, tk), lambda i, j, kk: (i, kk))
    in_specs = [a_spec, b_spec]
    args = [a, b]
    if bias is not None:
        in_specs.append(pl.BlockSpec((1, tn), lambda i, j, kk: (0, j)))
        args.append(bias)
    return pl.pallas_call(
        kern, grid=(m // tm, n // tn, nk), in_specs=in_specs,
        out_specs=pl.BlockSpec((tm, tn), lambda i, j, kk: (i, j)),
        out_shape=jax.ShapeDtypeStruct((m, n), out_dtype),
        scratch_shapes=[pltpu.VMEM((tm, tn) if nk > 1 else (8, 128), F32)],
        compiler_params=_cparams(("parallel", "parallel", "arbitrary")), name=name)(*args)


def _mm_tn(name, x, dy, rows=None, col_shards=False):
    t = x.shape[0] if rows is None else rows
    k1, n1 = x.shape[1], (dy.shape[1] if dy.ndim == 2 else 2 * dy.shape[2])
    tt = _pick(t, (512, 256, 128, 64, 8))
    tk1 = _pick(k1, (1024, 1408, 512, 256, 128))
    tn = _pick(n1 // 4, (512, 384, 1408, 256, 128)) if col_shards else _pick(n1, (512, 384, 256, 128))
    per = n1 // 4 // tn
    ns = t // tt

    def kern(x_ref, dy_ref, o_ref, acc_ref):
        s = pl.program_id(2)
        p = lax.dot_general(x_ref[...].astype(BF16), dy_ref[...].astype(BF16), (((0,), (0,)), ((), ())),
                            preferred_element_type=F32)

        @pl.when(s == 0)
        def _():
            acc_ref[...] = p

        @pl.when(s > 0)
        def _():
            acc_ref[...] += p

        @pl.when(s == ns - 1)
        def _():
            o_ref[...] = acc_ref[...].astype(o_ref.dtype)

    if col_shards:
        out_spec = pl.BlockSpec((None, tk1, tn), lambda i, j, s: (j // per, i, j % per))
        out_shape = jax.ShapeDtypeStruct((4, k1, n1 // 4), BF16)
    else:
        out_spec = pl.BlockSpec((tk1, tn), lambda i, j, s: (i, j))
        out_shape = jax.ShapeDtypeStruct((k1, n1), BF16)
    if dy.ndim == 3:
        pd = dy.shape[2] // tn
        dy_spec = pl.BlockSpec((None, tt, tn), lambda i, j, s: (j // pd, s, j % pd))
    else:
        dy_spec = pl.BlockSpec((tt, tn), lambda i, j, s: (s, j))
    return pl.pallas_call(
        kern, grid=(k1 // tk1, n1 // tn, ns),
        in_specs=[pl.BlockSpec((tt, tk1), lambda i, j, s: (s, i)), dy_spec],
        out_specs=out_spec, out_shape=out_shape, scratch_shapes=[pltpu.VMEM((tk1, tn), F32)],
        compiler_params=_cparams(("parallel", "parallel", "arbitrary")), name=name)(x, dy)


def _wgrad(name, xt, dy, rows=None, col_shards=False):
    k1 = xt.shape[0]
    t = xt.shape[1] if rows is None else rows
    n1 = dy.shape[1] if dy.ndim == 2 else 2 * dy.shape[2]
    tm = _pick(k1, (1024, 1408, 512, 256))
    tn = _pick(n1 // 4, (1408, 512, 384, 256, 128)) if col_shards else _pick(n1, (1024, 2432, 512, 384, 256, 128))
    tk = _pick(t, (512, 256, 128))
    ns = t // tk
    per = n1 // 4 // tn

    def kern(x_ref, dy_ref, o_ref, acc_ref):
        s = pl.program_id(2)
        p = jnp.dot(x_ref[...], dy_ref[...], preferred_element_type=F32)

        @pl.when(s == 0)
        def _():
            acc_ref[...] = p

        @pl.when(s > 0)
        def _():
            acc_ref[...] += p

        @pl.when(s == ns - 1)
        def _():
            o_ref[...] = acc_ref[...].astype(o_ref.dtype)

    if dy.ndim == 3:
        pd = dy.shape[2] // tn
        dy_spec = pl.BlockSpec((None, tk, tn), lambda i, j, s: (j // pd, s, j % pd))
    else:
        dy_spec = pl.BlockSpec((tk, tn), lambda i, j, s: (s, j))
    if col_shards:
        out_spec = pl.BlockSpec((None, tm, tn), lambda i, j, s: (j // per, i, j % per))
        out_shape = jax.ShapeDtypeStruct((4, k1, n1 // 4), BF16)
    else:
        out_spec = pl.BlockSpec((tm, tn), lambda i, j, s: (i, j))
        out_shape = jax.ShapeDtypeStruct((k1, n1), BF16)
    return pl.pallas_call(
        kern, grid=(k1 // tm, n1 // tn, ns),
        in_specs=[pl.BlockSpec((tm, tk), lambda i, j, s: (i, s)), dy_spec],
        out_specs=out_spec, out_shape=out_shape, scratch_shapes=[pltpu.VMEM((tm, tn), F32)],
        compiler_params=_cparams(("parallel", "parallel", "arbitrary")), name=name)(xt, dy)


def _swiglu_fwd(name, u, w_gu):
    m = u.shape[0]
    half = w_gu.shape[2]
    tm = _pick(m, (512, 256))

    def kern(u_ref, wa_ref, wb_ref, ab_ref, hm_ref, hmt_ref):
        uv = u_ref[...]
        a = jnp.dot(uv, wa_ref[...], preferred_element_type=F32)
        b = jnp.dot(uv, wb_ref[...], preferred_element_type=F32)
        ab_ref[0] = a.astype(BF16)
        ab_ref[1] = b.astype(BF16)
        hm = (_silu(a) * b).astype(BF16)
        hm_ref[...] = hm
        hmt_ref[...] = hm.T

    return pl.pallas_call(
        kern, grid=(m // tm, 2),
        in_specs=[pl.BlockSpec((tm, D), lambda i, j: (i, 0)),
                  pl.BlockSpec((None, D, half), lambda i, j: (j, 0, 0)),
                  pl.BlockSpec((None, D, half), lambda i, j: (2 + j, 0, 0))],
        out_specs=[pl.BlockSpec((2, tm, half), lambda i, j: (0, i, j)),
                   pl.BlockSpec((tm, half), lambda i, j: (i, j)),
                   pl.BlockSpec((half, tm), lambda i, j: (j, i))],
        out_shape=[jax.ShapeDtypeStruct((2, m, FF), BF16), jax.ShapeDtypeStruct((m, FF), BF16),
                   jax.ShapeDtypeStruct((FF, m), BF16)],
        compiler_params=_cparams(("parallel", "parallel")), name=name)(u, w_gu, w_gu)


def _swiglu_bwd(name, df, w_down, ab):
    m = df.shape[0]
    half = FF // 2
    tm = _pick(m, (512, 256))

    def kern(df_ref, w_ref, ab_ref, o_ref):
        dh = lax.dot_general(df_ref[...], w_ref[...], (((1,), (1,)), ((), ())), preferred_element_type=F32)
        a = ab_ref[0].astype(F32)
        b = ab_ref[1].astype(F32)
        s = jax.nn.sigmoid(a)
        o_ref[0] = (dh * b * (s * (1 + a * (1 - s)))).astype(BF16)
        o_ref[1] = (dh * (a * s)).astype(BF16)

    return pl.pallas_call(
        kern, grid=(m // tm, 2),
        in_specs=[pl.BlockSpec((tm, D), lambda i, j: (i, 0)),
                  pl.BlockSpec((half, D), lambda i, j: (j, 0)),
                  pl.BlockSpec((2, tm, half), lambda i, j: (0, i, j))],
        out_specs=pl.BlockSpec((2, tm, half), lambda i, j: (0, i, j)),
        out_shape=jax.ShapeDtypeStruct((2, m, FF), BF16),
        compiler_params=_cparams(("parallel", "parallel")), name=name)(df, w_down, ab)


def _gla_maps(bl, t, tc):
    nx, nc = t // CHUNK, tc // CHUNK
    nxb = bl * nx

    def rowblk(d, b, n):
        c_ctx = jnp.where(d == 0, n, nc - 1 - n)
        c_x = jnp.where(d == 0, n - nc, nx - 1 - (n - nc))
        return jnp.where(n < nc, nxb + b * nc + c_ctx, b * nx + c_x)

    def xblk(d, b, n):
        n2 = jnp.maximum(n, nc)
        return b * nx + jnp.where(d == 0, n2 - nc, nx - 1 - (n2 - nc))

    return nx, nc, rowblk, xblk


def _gla_chunk(m, q, k, v, g, h):
    gh = g[:, h * DK:(h + 1) * DK]
    b = jnp.dot(m, gh, precision=HIGHEST, preferred_element_type=F32)
    tot = jnp.sum(gh, axis=0, keepdims=True)
    mid = b[CHUNK // 2:CHUNK // 2 + 1, :]
    qh = q[:, h * DK:(h + 1) * DK] * Q_SCALE
    kh = k[:, h * DK:(h + 1) * DK]
    vh = v[:, h * DV:(h + 1) * DV]
    return b, tot, mid, qh, kh, vh


def _dot(a, b, ca, cb):
    return lax.dot_general(a.astype(BF16), b.astype(BF16), (((ca,), (cb,)), ((), ())),
                           preferred_element_type=F32)


def _gla_fwd(p, g2, mmats, bl, t, tc):
    nx, nc, rowblk, xblk = _gla_maps(bl, t, tc)
    ns = nx + nc

    def kern(q0, k0, v0, g0, q1, k1, v1, g1, m_ref, o0, o1, ss_ref, s_ref):
        n = pl.program_id(1)

        @pl.when(n == 0)
        def _():
            s_ref[...] = jnp.zeros(s_ref.shape, F32)
        for d, (q_ref, k_ref, v_ref, g_ref, o_ref) in enumerate(((q0, k0, v0, g0, o0), (q1, k1, v1, g1, o1))):
            m = m_ref[d]
            q, k, v, g = q_ref[...].astype(F32), k_ref[...].astype(F32), v_ref[...].astype(F32), g_ref[...]
            for h in range(HEADS):
                b, tot, mid, qh, kh, vh = _gla_chunk(m, q, k, v, g, h)
                s0 = s_ref[d, h * DV:(h + 1) * DV, :]
                ss_ref[d, 0, 0, h * DV:(h + 1) * DV, :] = s0
                kl = kh * jnp.exp(tot - b)
                s_ref[d, h * DV:(h + 1) * DV, :] = s0 * jnp.exp(tot) + _dot(vh, kl, 0, 0)
                qe = qh * jnp.exp(b)
                qm = qh * jnp.exp(b - mid)
                km = kh * jnp.exp(mid - b)
                att = m * _dot(qm, km, 1, 1)
                o_ref[:, h * DV:(h + 1) * DV] = _dot(qe, s0, 1, 1) + _dot(att, vh, 1, 0)

    def operands(d):
        return [pl.BlockSpec((CHUNK, 512), lambda b, n: (rowblk(d, b, n), 4)),
                pl.BlockSpec((CHUNK, 512), lambda b, n: (rowblk(d, b, n), 5)),
                pl.BlockSpec((CHUNK, 1024), lambda b, n: (rowblk(d, b, n), 3)),
                pl.BlockSpec((CHUNK, 512), lambda b, n: (rowblk(d, b, n), d))]

    o_shape = jax.ShapeDtypeStruct((bl * t, HEADS * DV), F32)
    return pl.pallas_call(
        kern, grid=(bl, ns),
        in_specs=operands(0) + operands(1) + [pl.BlockSpec((2, CHUNK, CHUNK), lambda b, n: (0, 0, 0))],
        out_specs=[pl.BlockSpec((CHUNK, 1024), lambda b, n: (xblk(0, b, n), 0)),
                   pl.BlockSpec((CHUNK, 1024), lambda b, n: (xblk(1, b, n), 0)),
                   pl.BlockSpec((2, 1, 1, HEADS * DV, DK), lambda b, n: (0, b, n, 0, 0))],
        out_shape=[o_shape, o_shape, jax.ShapeDtypeStruct((2, bl, ns, HEADS * DV, DK), F32)],
        scratch_shapes=[pltpu.VMEM((2, HEADS * DV, DK), F32)],
        compiler_params=_cparams(("parallel", "arbitrary")), name="gla_fwd")(p, p, p, g2, p, p, p, g2, mmats)


def _gla_bwd(p, g2, mmats, ssave, do, bl, t, tc):
    nx, nc, rowblk, xblk = _gla_maps(bl, t, tc)
    ns = nx + nc
    rev = lambda s: ns - 1 - s

    def kern(q0, k0, v0, g0, do0, q1, k1, v1, g1, do1, m_ref, ss_ref,
             dq0, dk0, dv0, dg0, dq1, dk1, dv1, dg1, ds_ref):
        step = pl.program_id(1)
        n = ns - 1 - step

        @pl.when(step == 0)
        def _():
            ds_ref[...] = jnp.zeros(ds_ref.shape, F32)
        live = (n >= nc).astype(F32)
        sides = ((q0, k0, v0, g0, do0, dq0, dk0, dv0, dg0), (q1, k1, v1, g1, do1, dq1, dk1, dv1, dg1))
        for d, (q_ref, k_ref, v_ref, g_ref, do_ref, dq_ref, dk_ref, dv_ref, dg_ref) in enumerate(sides):
            m = m_ref[d]
            q, k, v, g = q_ref[...].astype(F32), k_ref[...].astype(F32), v_ref[...].astype(F32), g_ref[...]
            dov = do_ref[...] * live
            for h in range(HEADS):
                b, tot, mid, qh, kh, vh = _gla_chunk(m, q, k, v, g, h)
                doh = dov[:, h * DV:(h + 1) * DV]
                s0 = ss_ref[d, 0, 0, h * DV:(h + 1) * DV, :]
                ds1 = ds_ref[d, h * DV:(h + 1) * DV, :]
                eb, ebm, emb, etb = jnp.exp(b), jnp.exp(b - mid), jnp.exp(mid - b), jnp.exp(tot - b)
                etot = jnp.exp(tot)
                qe, qm, km, kl = qh * eb, qh * ebm, kh * emb, kh * etb
                att = m * _dot(qm, km, 1, 1)
                ds_ref[d, h * DV:(h + 1) * DV, :] = ds1 * etot + _dot(doh, qe, 0, 0)
                dqe = _dot(doh, s0, 1, 0)
                datt = m * _dot(doh, vh, 1, 1)
                dqm = _dot(datt, km, 1, 0)
                dkm = _dot(datt, qm, 0, 0)
                dvh = _dot(att, doh, 0, 0) + _dot(kl, ds1, 1, 1)
                dkl = _dot(vh, ds1, 1, 0)
                dtot = etot * jnp.sum(ds1 * s0, axis=0, keepdims=True) + jnp.sum(dkl * kl, axis=0, keepdims=True)
                db = dqe * qe + dqm * qm - dkm * km - dkl * kl
                dgh = lax.dot_general(m, db, (((0,), (0,)), ((), ())), precision=HIGHEST,
                                      preferred_element_type=F32) + dtot
                dq_ref[:, h * DK:(h + 1) * DK] = (dqe * eb + dqm * ebm) * Q_SCALE
                dk_ref[:, h * DK:(h + 1) * DK] = dkm * emb + dkl * etb
                dv_ref[:, h * DV:(h + 1) * DV] = dvh
                dg_ref[:, h * DK:(h + 1) * DK] = dgh

    nt = p.shape[0]

    def operands(d):
        return [pl.BlockSpec((CHUNK, 512), lambda b, s: (rowblk(d, b, rev(s)), 4)),
                pl.BlockSpec((CHUNK, 512), lambda b, s: (rowblk(d, b, rev(s)), 5)),
                pl.BlockSpec((CHUNK, 1024), lambda b, s: (rowblk(d, b, rev(s)), 3)),
                pl.BlockSpec((CHUNK, 512), lambda b, s: (rowblk(d, b, rev(s)), d)),
                pl.BlockSpec((CHUNK, 1024), lambda b, s: (xblk(d, b, rev(s)), 0))]

    def results(d):
        row = lambda b, s: (rowblk(d, b, rev(s)), 0)
        return [pl.BlockSpec((CHUNK, 512), row), pl.BlockSpec((CHUNK, 512), row), pl.BlockSpec((CHUNK, 1024), row),
                pl.BlockSpec((CHUNK, 512), row)]

    shapes = [jax.ShapeDtypeStruct((nt, 512), F32), jax.ShapeDtypeStruct((nt, 512), F32),
              jax.ShapeDtypeStruct((nt, 1024), F32), jax.ShapeDtypeStruct((nt, 512), F32)]
    out = pl.pallas_call(
        kern, grid=(bl, ns),
        in_specs=operands(0) + operands(1) + [
            pl.BlockSpec((2, CHUNK, CHUNK), lambda b, s: (0, 0, 0)),
            pl.BlockSpec((2, 1, 1, HEADS * DV, DK), lambda b, s: (0, b, rev(s), 0, 0))],
        out_specs=results(0) + results(1), out_shape=shapes + shapes,
        scratch_shapes=[pltpu.VMEM((2, HEADS * DV, DK), F32)],
        compiler_params=_cparams(("parallel", "arbitrary")), name="gla_bwd")(
            p, p, p, g2, do, p, p, p, g2, do, mmats, ssave)
    return out[:4], out[4:]


CONV_CT = 256
CONV_PAD = 16
CONV_RC = 128


def _conv_fill(zp, z_ref, t):
    zp[0:CONV_PAD, :] = jnp.zeros((CONV_PAD, CONV_CT), F32)
    zp[CONV_PAD + t:2 * CONV_PAD + t, :] = jnp.zeros((CONV_PAD, CONV_CT), F32)
    zp[CONV_PAD:CONV_PAD + t, :] = z_ref[...]


def _dwconv(name, z, w, bias, bl, t, flip):
    def kern(z_ref, w_ref, b_ref, o_ref, zp):
        _conv_fill(zp, z_ref, t)
        for r in range(0, t, CONV_RC):
            acc = jnp.broadcast_to(b_ref[...], (CONV_RC, CONV_CT))
            for j in range(CONV_W):
                off = (CONV_W - j) if flip else (j + 1)
                acc = acc + w_ref[j:j + 1, :] * zp[r + off:r + off + CONV_RC, :]
            o_ref[r:r + CONV_RC, :] = acc

    return pl.pallas_call(
        kern, grid=(bl, 1024 // CONV_CT),
        in_specs=[pl.BlockSpec((t, CONV_CT), lambda b, c: (b, c)),
                  pl.BlockSpec((32, CONV_CT), lambda b, c: (0, c)),
                  pl.BlockSpec((1, CONV_CT), lambda b, c: (0, c))],
        out_specs=pl.BlockSpec((t, CONV_CT), lambda b, c: (b, c)),
        out_shape=jax.ShapeDtypeStruct(z.shape, F32),
        scratch_shapes=[pltpu.VMEM((t + 2 * CONV_PAD, CONV_CT), F32)],
        compiler_params=_cparams(("parallel", "parallel")), name=name)(z, w, bias)


def _dwconv_wgrad(z, dzc, bl, t):
    def kern(z_ref, d_ref, dw_ref, db_ref, zp):
        b = pl.program_id(1)

        @pl.when(b == 0)
        def _():
            dw_ref[...] = jnp.zeros(dw_ref.shape, F32)
            db_ref[...] = jnp.zeros(db_ref.shape, F32)
        _conv_fill(zp, z_ref, t)
        for j in range(CONV_W):
            acc = jnp.zeros((CONV_RC, CONV_CT), F32)
            for r in range(0, t, CONV_RC):
                acc = acc + d_ref[r:r + CONV_RC, :] * zp[r + j + 1:r + j + 1 + CONV_RC, :]
            dw_ref[j:j + 1, :] += jnp.sum(acc, axis=0, keepdims=True)
        db_ref[...] += jnp.sum(d_ref[...], axis=0, keepdims=True)

    return pl.pallas_call(
        kern, grid=(1024 // CONV_CT, bl),
        in_specs=[pl.BlockSpec((t, CONV_CT), lambda c, b: (b, c)),
                  pl.BlockSpec((t, CONV_CT), lambda c, b: (b, c))],
        out_specs=[pl.BlockSpec((32, CONV_CT), lambda c, b: (0, c)),
                   pl.BlockSpec((1, CONV_CT), lambda c, b: (0, c))],
        out_shape=[jax.ShapeDtypeStruct((32, 1024), F32), jax.ShapeDtypeStruct((1, 1024), F32)],
        scratch_shapes=[pltpu.VMEM((t + 2 * CONV_PAD, CONV_CT), F32)],
        compiler_params=_cparams(("parallel", "arbitrary")), name="dwconv_wgrad")(z, dzc)


def _ffn_fwd(tag, xin, n_tiles, g, sh, sc, gate, w_gu, w_down, tpb, nb):
    rows = n_tiles * TM
    rm = functools.partial(_rowmap, tpb=tpb, nb=nb)
    u, ut = rm(tag + "_norm", lambda i, x, g_, sh_, sc_: _twice(_modnorm(x, g_, sh_, sc_)), n_tiles,
               [tok(xin), const(g), mod(sh), mod(sc)], [("tok", rows, D, BF16), ("tokT", rows, D, BF16)])
    ab, hm, hmt = _swiglu_fwd(tag + "_gu", u, w_gu)
    f = _mm(tag + "_down", hm, w_down)
    (xout,) = rm(tag + "_res", lambda i, x, f_, gt: (x + 0.5 * gt * f_,), n_tiles,
                 [tok(xin), tok(f), mod(gate)], [("tok", rows, D, F32)])
    return xout, (ut, ab, hmt, f)


def _ffn_bwd(tag, xin, saved, dxout, dx_clamp, n_tiles, g, sh, sc, gate, w_gu, w_down, tpb, nb):
    ut, ab, hmt, f = saved
    rows = n_tiles * TM
    rm = functools.partial(_rowmap, tpb=tpb, nb=nb)

    def mask(i):
        return 1.0 if dx_clamp is None else (i <= dx_clamp).astype(F32)

    def b1(i, dx, f_, gt):
        dx = dx * mask(i)
        return (0.5 * gt * dx, jnp.sum(0.5 * f_ * dx, axis=0, keepdims=True))
    df, dgate = rm(tag + "_bres", b1, n_tiles, [tok(dxout, clamp=dx_clamp), tok(f), mod(gate)],
                   [("tok", rows, D, BF16), ("modacc", D)])
    dw_down = _wgrad(tag + "_wdown", hmt, df)
    dab = _swiglu_bwd(tag + "_bdown", df, w_down, ab)
    du = _mm(tag + "_bgu", dab, w_gu, trans_b=True)
    dw_gu = _wgrad(tag + "_wgu", ut, dab, col_shards=True)

    def b3(i, x, g_, sh_, sc_, du_, dx):
        _, vjp = jax.vjp(_modnorm, x, g_, sh_, sc_)
        dxn, dg, dsh, dsc = vjp(du_)
        return (dx * mask(i) + dxn, dg, dsh, dsc)
    dxin, dg, dsh, dsc = rm(tag + "_bnorm", b3, n_tiles,
                            [tok(xin), const(g), mod(sh), mod(sc), tok(du), tok(dxout, clamp=dx_clamp)],
                            [("tok", rows, D, F32), ("acc", 1, D), ("modacc", D), ("modacc", D)])
    return dxin, dict(g=dg, sh=dsh, sc=dsc, gate=dgate, w_gu=dw_gu, w_down=dw_down)


def _perm_in_cols(w):
    pad = jnp.zeros(w.shape[:-1] + (D_INP - D_IN,), w.dtype)
    return jnp.concatenate([w[..., :5120], w[..., 5152:7200], w[..., 5120:5152], pad], axis=-1)


def _unperm_in_cols(w):
    return jnp.concatenate([w[..., :5120], w[..., LR_COL:LR_COL + 32], w[..., 5120:LR_COL]], axis=-1)


def _local_step(x, c, ctx, target, wts):
    bl, t, _ = x.shape
    tc = ctx.shape[1]
    nx_rows, nc_rows = bl * t, bl * tc
    nt_rows = nx_rows + nc_rows
    tpb = t // TM
    nxt, ntt = nx_rows // TM, nt_rows // TM
    nb = bl
    rm = functools.partial(_rowmap, tpb=tpb, nb=nb)
    last_x = nxt - 1

    x0 = jnp.concatenate([x.reshape(nx_rows, D), ctx.reshape(nc_rows, D)], axis=0)
    tgt = target.reshape(nx_rows, D)

    cc = jnp.concatenate([c, wts["c_ctx"].reshape(1, D), jnp.zeros((8 - bl - 1, D), F32)], axis=0)
    modv = _mm("mod_fwd", cc, wts["w_mod"], a_fn=_silu, bias=wts["b_mod"])
    mods = [modv[:nb + 1, k * D:(k + 1) * D].reshape(nb + 1, 1, D) for k in range(9)]

    x1, sv1 = _ffn_fwd("ffn1", x0, ntt, wts["g_ffn1"], mods[0], mods[1], mods[2], wts["w1_gu"], wts["w1_down"],
                       tpb, nb)
    u2, u2t = rm("in_norm", lambda i, x_, g_, sh_, sc_: _twice(_modnorm(x_, g_, sh_, sc_)), ntt,
                 [tok(x1), const(wts["g_mix"]), mod(mods[3]), mod(mods[4])],
                 [("tok", nt_rows, D, BF16), ("tokT", nt_rows, D, BF16)])
    w_inp = wts["w_in_p"]
    p = _mm("in_proj", u2, w_inp, out_dtype=BF16)

    waf, wab, baf, bab = wts["w_alpha_f_pad"], wts["w_alpha_b_pad"], wts["b_alpha_f"], wts["b_alpha_b"]

    def dec_fwd(i, lr, wf, wb, bf_, bb_):
        zf = _dot(lr, wf, 1, 0) + bf_
        zb = _dot(lr, wb, 1, 0) + bb_
        return (jnp.concatenate([jax.nn.log_sigmoid(zf) / TAU, jax.nn.log_sigmoid(zb) / TAU], axis=1),)
    (gfb,) = rm("decay_fwd", dec_fwd, ntt, [tok(p, 128, LR_COL // 128), const(waf), const(wab), const(baf), const(bab)],
                [("tok", nt_rows, 1024, F32)])
    g2 = gfb
    tri = jnp.tril(jnp.ones((CHUNK, CHUNK), F32))
    mmats = jnp.stack([tri, tri.T])
    *o2, ssave = _gla_fwd(p, g2, mmats, bl, t, tc)

    gn_g = wts["gla_norm_g"]

    def gla_out(of, ob, og, gn):
        o = of + ob
        parts = []
        for h in range(HEADS):
            oh = o[:, h * DV:(h + 1) * DV]
            parts.append(oh * lax.rsqrt(jnp.mean(oh * oh, axis=-1, keepdims=True) + EPS))
        return jnp.concatenate(parts, axis=1) * gn * _silu(og)
    yg_in, yg_int = rm("gla_out", lambda i, of, ob, og, gn: _twice(gla_out(of, ob, og, gn)), nxt,
                       [tok(o2[0]), tok(o2[1]), tok(p, 1024, 4), const(gn_g)],
                       [("tok", nx_rows, D, BF16), ("tokT", nx_rows, D, BF16)])
    y_gla = _mm("gla_proj", yg_in, wts["w_gla_out"])

    (z,) = rm("glu", lambda i, a, b: (a * jax.nn.sigmoid(b),), nxt, [tok(p, 1024, 0), tok(p, 1024, 1)],
              [("tok", nx_rows, D, F32)])
    dw_w = jnp.concatenate([wts["dw_weight"], jnp.zeros((1, D), F32)], axis=0)
    zc = _dwconv("dwconv_fwd", z, dw_w, wts["dw_bias"], bl, t, False)

    def ln_silu(zc_, g_, b_):
        mu = jnp.mean(zc_, axis=-1, keepdims=True)
        var = jnp.mean(jnp.square(zc_ - mu), axis=-1, keepdims=True)
        return _silu((zc_ - mu) * lax.rsqrt(var + EPS) * g_ + b_)
    ln_g, ln_b = wts["conv_ln_g"], wts["conv_ln_b"]
    zl, zlt = rm("conv_ln", lambda i, zc_, g_, b_: _twice(ln_silu(zc_, g_, b_)), nxt,
                 [tok(zc), const(ln_g), const(ln_b)], [("tok", nx_rows, D, BF16), ("tokT", nx_rows, D, BF16)])
    y_conv = _mm("conv_proj", zl, wts["w_conv_out"])

    mg, mgt = rm("merge", lambda i, ga, gb, yc, yg: _twice(jax.nn.sigmoid(ga) * yc + jax.nn.sigmoid(gb) * yg), nxt,
                 [tok(p, 1024, 5), tok(p, 1024, 6), tok(y_conv), tok(y_gla)],
                 [("tok", nx_rows, D, BF16), ("tokT", nx_rows, D, BF16)])
    mix = _mm("out_proj", mg, wts["w_out"])
    (x2,) = rm("mix_res", lambda i, x_, mx_, gt: (x_ + gt * mx_,), nxt, [tok(x1), tok(mix), mod(mods[5])],
               [("tok", nx_rows, D, F32)])

    x3, sv2 = _ffn_fwd("ffn2", x2, nxt, wts["g_ffn2"], mods[6], mods[7], mods[8], wts["w2_gu"], wts["w2_down"],
                       tpb, nb)
    g_fin = wts["g_final"].reshape(1, D)

    def head(i, x_, g_, tg):
        y, vjp = jax.vjp(_rms, x_, g_)
        diff = y - tg
        dx, dg = vjp(diff * (1.0 / D))
        loss = 0.5 * jnp.sum(jnp.mean(diff * diff, axis=-1, keepdims=True))
        return dx, dg, loss
    dx3, dg_final, loss_acc = rm("loss_head", head, nxt, [tok(x3), const(g_fin), tok(tgt)],
                                 [("tok", nx_rows, D, F32), ("acc", 1, D), ("acc", 8, 128)])
    loss = loss_acc[0, 0]

    grads = {}
    dx2, gf2 = _ffn_bwd("ffn2", x2, sv2, dx3, None, nxt, wts["g_ffn2"], mods[6], mods[7], mods[8],
                        wts["w2_gu"], wts["w2_down"], tpb, nb)
    grads.update(g_ffn2=gf2["g"], w2_gu=gf2["w_gu"], w2_down=gf2["w_down"])

    dmix, dgate5 = rm("mix_bres", lambda i, dx, mx_, gt: (gt * dx, jnp.sum(mx_ * dx, axis=0, keepdims=True)), nxt,
                      [tok(dx2), tok(mix), mod(mods[5])], [("tok", nx_rows, D, BF16), ("modacc", D)])
    dmg = _mm("out_bproj", dmix, wts["w_out"], trans_b=True)
    grads["w_out"] = _wgrad("out_wgrad", mgt, dmix)

    def merge_bwd(i, dm, ga, gb, yc, yg):
        keep = (i <= last_x).astype(F32)
        dm = dm * keep
        sa, sb = jax.nn.sigmoid(ga), jax.nn.sigmoid(gb)
        return dm * sa, dm * sb, dm * yc * sa * (1 - sa), dm * yg * sb * (1 - sb)
    cl = dict(clamp=last_x)
    dyc, dyg, dga, dgb = rm("merge_bwd", merge_bwd, ntt,
                            [tok(dmg, **cl), tok(p, 1024, 5, last_x), tok(p, 1024, 6, last_x), tok(y_conv, **cl),
                             tok(y_gla, **cl)],
                            [("tok", nt_rows, D, BF16)] * 4)

    dzl = _mm("conv_bproj", dyc, wts["w_conv_out"], trans_b=True, rows=nx_rows)
    grads["w_conv_out"] = _wgrad("conv_wgrad", zlt, dyc, rows=nx_rows)

    def ln_bwd(i, zc_, g_, b_, dz_):
        _, vjp = jax.vjp(ln_silu, zc_, g_, b_)
        return vjp(dz_)
    dzc, dln_g, dln_b = rm("conv_ln_bwd", ln_bwd, nxt, [tok(zc), const(ln_g), const(ln_b), tok(dzl)],
                           [("tok", nx_rows, D, F32), ("acc", 1, D), ("acc", 1, D)])
    dz = _dwconv("dwconv_bwd", dzc, dw_w, jnp.zeros((1, D), F32), bl, t, True)
    ddw, ddb = _dwconv_wgrad(z, dzc, bl, t)
    grads.update(conv_ln_g=dln_g, conv_ln_b=dln_b, dw_weight=ddw[:CONV_W], dw_bias=ddb)

    def glu_bwd(i, dz_, a, b):
        keep = (i <= last_x).astype(F32)
        dz_ = dz_ * keep
        s = jax.nn.sigmoid(b)
        return (jnp.concatenate([dz_ * s, dz_ * a * s * (1 - s)], axis=1),)
    (dconv,) = rm("glu_bwd", glu_bwd, ntt, [tok(dz, **cl), tok(p, 1024, 0, last_x), tok(p, 1024, 1, last_x)],
                  [("tok", nt_rows, 2048, BF16)])

    dyg_in = _mm("gla_bproj", dyg, wts["w_gla_out"], trans_b=True, rows=nx_rows)
    grads["w_gla_out"] = _wgrad("gla_wgrad", yg_int, dyg, rows=nx_rows)

    def gla_out_bwd(i, of, ob, og, gn, dy):
        _, vjp = jax.vjp(gla_out, of, ob, og, gn)
        do_, _, dog_, dgn_ = vjp(dy)
        return do_, dog_, dgn_
    do, dog_x, dgn = rm("gla_out_bwd", gla_out_bwd, nxt,
                        [tok(o2[0]), tok(o2[1]), tok(p, 1024, 4), const(gn_g), tok(dyg_in)],
                        [("tok", nx_rows, D, F32), ("tok", nx_rows, D, BF16), ("acc", 1, D)])
    grads["gla_norm_g"] = dgn
    dog = jnp.concatenate([dog_x, jnp.zeros((nc_rows, D), BF16)], axis=0)

    dq2, dk2, dv2, dg2 = zip(*_gla_bwd(p, g2, mmats, ssave, do, bl, t, tc))

    def dec_bwd(i, lr, wf, wb, bf_, bb_, dgf, dgb_):
        zf = _dot(lr, wf, 1, 0) + bf_
        zb = _dot(lr, wb, 1, 0) + bb_
        dzf = dgf * (1 - jax.nn.sigmoid(zf)) * (1.0 / TAU)
        dzb = dgb_ * (1 - jax.nn.sigmoid(zb)) * (1.0 / TAU)
        dlr = _dot(dzf, wf, 1, 1) + _dot(dzb, wb, 1, 1)
        return (dlr, _dot(lr, dzf, 0, 0), _dot(lr, dzb, 0, 0), jnp.sum(dzf, axis=0, keepdims=True),
                jnp.sum(dzb, axis=0, keepdims=True))
    dlr, dwaf, dwab, dbaf, dbab = rm(
        "decay_bwd", dec_bwd, ntt,
        [tok(p, 128, LR_COL // 128), const(waf), const(wab), const(baf), const(bab), tok(dg2[0]), tok(dg2[1])],
        [("tok", nt_rows, 128, BF16), ("acc", 128, 512), ("acc", 128, 512), ("acc", 1, 512), ("acc", 1, 512)])
    grads.update(w_alpha_f=dwaf[:LOWRANK], w_alpha_b=dwab[LOWRANK:2 * LOWRANK], b_alpha_f=dbaf, b_alpha_b=dbab)

    dq, dk, dv = rm("gla_sum", lambda i, q0, q1, k0, k1, v0, v1: (q0 + q1, k0 + k1, v0 + v1), ntt,
                    [tok(dq2[0]), tok(dq2[1]), tok(dk2[0]), tok(dk2[1]), tok(dv2[0]), tok(dv2[1])],
                    [("tok", nt_rows, 512, BF16), ("tok", nt_rows, 512, BF16), ("tok", nt_rows, 1024, BF16)])

    dp = jnp.concatenate([dconv, dq, dk, dv, dog, dga, dgb, dlr], axis=1)
    du2 = _mm("in_bproj", dp, w_inp, trans_b=True)
    grads["w_in_p"] = _wgrad("in_wgrad", u2t, dp)

    def in_norm_bwd(i, x_, g_, sh_, sc_, du_, dx):
        keep = (i <= last_x).astype(F32)
        _, vjp = jax.vjp(_modnorm, x_, g_, sh_, sc_)
        dxn, dg, dsh, dsc = vjp(du_)
        return (dx * keep + dxn, dg, dsh, dsc)
    dx1, dg_mix, dsh3, dsc4 = rm("in_norm_bwd", in_norm_bwd, ntt,
                                 [tok(x1), const(wts["g_mix"]), mod(mods[3]), mod(mods[4]), tok(du2), tok(dx2, **cl)],
                                 [("tok", nt_rows, D, F32), ("acc", 1, D), ("modacc", D), ("modacc", D)])
    grads["g_mix"] = dg_mix

    dx0, gf1 = _ffn_bwd("ffn1", x0, sv1, dx1, None, ntt, wts["g_ffn1"], mods[0], mods[1], mods[2],
                        wts["w1_gu"], wts["w1_down"], tpb, nb)
    grads.update(g_ffn1=gf1["g"], w1_gu=gf1["w_gu"], w1_down=gf1["w_down"])
    grad_x = dx0[:nx_rows].reshape(bl, t, D)

    dmods = [gf1["sh"], gf1["sc"], gf1["gate"], dsh3, dsc4, dgate5, gf2["sh"], gf2["sc"], gf2["gate"]]
    dmod = jnp.concatenate(
        [jnp.concatenate([a.reshape(a.shape[0], D), jnp.zeros((8 - a.shape[0], D), F32)], axis=0) for a in dmods],
        axis=1)
    dsc_ = _mm("mod_bproj", dmod, wts["w_mod"], trans_b=True)

    def silu_bwd(cc_ref, d_ref, dm_ref, dcc_ref, scc_ref, db_ref):
        cc_ = cc_ref[...]
        s = jax.nn.sigmoid(cc_)
        dcc_ref[...] = d_ref[...] * (s * (1 + cc_ * (1 - s)))
        scc_ref[...] = (cc_ * s).astype(BF16)
        db_ref[...] = jnp.sum(dm_ref[...], axis=0, keepdims=True)
    dcc, scc, db_mod = pl.pallas_call(
        silu_bwd, out_shape=[jax.ShapeDtypeStruct((8, D), F32), jax.ShapeDtypeStruct((8, D), BF16),
                             jax.ShapeDtypeStruct((1, 9 * D), F32)], name="mod_silu_bwd")(cc, dsc_, dmod)
    grads["c_ctx"] = dcc[nb]
    grads["b_mod"] = db_mod
    grads["w_mod"] = _mm_tn("mod_wgrad", scc, dmod, col_shards=True)
    grads["g_final"] = dg_final.reshape(D)
    return loss, grad_x, grads


ANY = pl.BlockSpec(memory_space=pl.ANY)


def _place():
    x, y, c = lax.axis_index("x"), lax.axis_index("y"), lax.axis_index("c")
    chips = [(1 - x, y), (x, 1 - y), (1 - x, 1 - y)]
    return x, y, c, chips


def _remote(send_sems, recv_sems):
    def copy(k, src, dst, to):
        return pltpu.make_async_remote_copy(src_ref=src, dst_ref=dst, send_sem=send_sems.at[k],
                                            recv_sem=recv_sems.at[k], device_id=to, device_id_type=MESH)
    return copy


def _sems(n):
    return [pltpu.SemaphoreType.DMA((n,)), pltpu.SemaphoreType.DMA((n,))]


def _gather_weights(shards):
    n = len(shards)

    def body(*refs):
        ins, outs = refs[:n], refs[n:2 * n]
        copy = _remote(refs[2 * n], refs[2 * n + 1])
        x, y, c, chips = _place()
        me = 2 * x + y
        sibling = (x, y, 1 - c)
        started = []

        def rows(i, hc):
            hr = ins[i].shape[0] // 2
            return pl.ds(hc * hr, hr)

        for i in range(n):
            started.append(copy(7 * i + 6, ins[i], outs[i].at[me], sibling))
            started[-1].start()
            for j, (px, py) in enumerate(chips):
                started.append(copy(7 * i + j, ins[i].at[rows(i, c), :], outs[i].at[me, rows(i, c), :], (px, py, c)))
                started[-1].start()
        for i in range(n):
            for j, (px, py) in enumerate(chips):
                half = outs[i].at[2 * px + py, rows(i, c), :]
                copy(7 * i + j, half, half, (px, py, c)).wait_recv()
                started.append(copy(7 * i + 3 + j, half, half, sibling))
                started[-1].start()
        for i in range(n):
            copy(7 * i + 6, ins[i], outs[i].at[me], sibling).wait_recv()
            for j, (px, py) in enumerate(chips):
                other = outs[i].at[2 * px + py, rows(i, 1 - c), :]
                copy(7 * i + 3 + j, other, other, sibling).wait_recv()
        for cp in started:
            cp.wait_send()

    return pl.pallas_call(
        body, out_shape=[jax.ShapeDtypeStruct((4,) + s.shape, s.dtype) for s in shards], in_specs=[ANY] * n,
        out_specs=[ANY] * n, scratch_shapes=_sems(7 * n), name="gather_weights")(*shards)


def _swap_halves(gs):
    n = len(gs)

    def body(*refs):
        ins, outs = refs[:n], refs[n:2 * n]
        copy = _remote(refs[2 * n], refs[2 * n + 1])
        x, y, c, _ = _place()
        cps = []
        for i in range(n):
            hr = ins[i].shape[1] // 2
            cps.append(copy(i, ins[i].at[:, pl.ds((1 - c) * hr, hr), :], outs[i], (x, y, 1 - c)))
            cps[-1].start()
        for cp in cps:
            cp.wait()

    return pl.pallas_call(
        body, out_shape=[jax.ShapeDtypeStruct((4, g.shape[1] // 2, g.shape[2]), g.dtype) for g in gs],
        in_specs=[ANY] * n, out_specs=[ANY] * n, scratch_shapes=_sems(n), name="grad_swap_halves")(*gs)


def _row_tile(hr):
    return hr if hr <= 256 else _pick(hr, (256, 176, 128, 64, 32, 16))


def _add_halves(name, g, r, place):
    hr = r.shape[1]
    tr = _row_tile(hr)
    nblk = hr // tr

    def kern(p_ref, g_ref, r_ref, o_ref):
        o_ref[...] = (g_ref[...].astype(F32) + r_ref[...].astype(F32)).astype(o_ref.dtype)

    blk = (1, tr, g.shape[2])
    return pl.pallas_call(
        kern,
        grid_spec=pltpu.PrefetchScalarGridSpec(
            num_scalar_prefetch=1, grid=(4, nblk),
            in_specs=[pl.BlockSpec(blk, lambda j, i, p: (j, p[0] * nblk + i, 0)),
                      pl.BlockSpec(blk, lambda j, i, p: (j, i, 0))],
            out_specs=pl.BlockSpec(blk, lambda j, i, p: (j, i, 0))),
        out_shape=jax.ShapeDtypeStruct(r.shape, r.dtype),
        compiler_params=_cparams(("parallel", "parallel")), name=name)(place, g, r)


def _scatter_chips(cs):
    n = len(cs)

    def body(*refs):
        ins, outs = refs[:n], refs[n:2 * n]
        copy = _remote(refs[2 * n], refs[2 * n + 1])
        x, y, c, chips = _place()
        me = 2 * x + y
        sends = []
        for i in range(n):
            for j, (px, py) in enumerate(chips):
                sends.append(copy(3 * i + j, ins[i].at[2 * px + py], outs[i].at[me], (px, py, c)))
                sends[-1].start()
        for i in range(n):
            for j, (px, py) in enumerate(chips):
                src = 2 * px + py
                copy(3 * i + j, ins[i].at[src], outs[i].at[src], (px, py, c)).wait_recv()
        for cp in sends:
            cp.wait_send()

    return pl.pallas_call(
        body, out_shape=[jax.ShapeDtypeStruct(a.shape, a.dtype) for a in cs], in_specs=[ANY] * n,
        out_specs=[ANY] * n, scratch_shapes=_sems(3 * n), name="grad_scatter_chips")(*cs)


def _sum_chips(name, cs, r, place):
    hr = r.shape[1]
    tr = _row_tile(hr)
    nblk = hr // tr

    def kern(p_ref, c_ref, r0, r1, r2, r3, o_ref):
        me = p_ref[1]
        acc = None
        for k, rk in enumerate((r0, r1, r2, r3)):
            val = jnp.where(me == k, c_ref[0].astype(F32), rk[0].astype(F32))
            acc = val if acc is None else acc + val
        o_ref[...] = acc

    blk = (1, tr, r.shape[2])

    def slot(k):
        return lambda i, p: (jnp.where(p[1] == k, (k + 1) % 4, k), i, 0)

    return pl.pallas_call(
        kern,
        grid_spec=pltpu.PrefetchScalarGridSpec(
            num_scalar_prefetch=1, grid=(nblk,),
            in_specs=[pl.BlockSpec(blk, lambda i, p: (p[1], i, 0))] + [pl.BlockSpec(blk, slot(k)) for k in range(4)],
            out_specs=pl.BlockSpec((tr, r.shape[2]), lambda i, p: (p[0] * nblk + i, 0))),
        out_shape=jax.ShapeDtypeStruct((2 * hr, r.shape[2]), F32),
        compiler_params=_cparams(("parallel",)), name=name)(place, cs, r, r, r, r)


def _join_halves(fs):
    n = len(fs)

    def body(*refs):
        ins, outs = refs[:n], refs[n:2 * n]
        copy = _remote(refs[2 * n], refs[2 * n + 1])
        x, y, c, _ = _place()
        cps = []
        for i in range(n):
            hr = ins[i].shape[0] // 2
            cps.append(copy(i, ins[i].at[pl.ds(c * hr, hr), :], outs[i].at[pl.ds(c * hr, hr), :], (x, y, 1 - c)))
            cps[-1].start()
        for i in range(n):
            hr = ins[i].shape[0] // 2
            other = outs[i].at[pl.ds((1 - c) * hr, hr), :]
            copy(i, other, other, (x, y, 1 - c)).wait_recv()
        for cp in cps:
            cp.wait_send()

    return pl.pallas_call(
        body, out_shape=[jax.ShapeDtypeStruct(f.shape, f.dtype) for f in fs], in_specs=[ANY] * n,
        out_specs=[ANY] * n, input_output_aliases={i: i for i in range(n)}, scratch_shapes=_sems(n),
        name="grad_join_halves")(*fs)


def _allreduce_small(v):
    def body(x_ref, out_ref, gath, send_sems, recv_sems, local_sem):
        x, y, c, chips = _place()
        me, sibling = (x, y, c), (x, y, 1 - c)

        def slot(px, py, pc):
            return gath.at[4 * px + 2 * py + pc]

        def copy(k, block, to, src=None):
            return pltpu.make_async_remote_copy(
                src_ref=slot(*block) if src is None else src, dst_ref=slot(*block), send_sem=send_sems.at[k],
                recv_sem=recv_sems.at[k], device_id=to, device_id_type=MESH)

        mine = pltpu.make_async_copy(x_ref, slot(*me), local_sem)
        mine.start()
        first = [copy(0, me, sibling, src=x_ref)]
        first += [copy(1 + j, me, (*chip, c), src=x_ref) for j, chip in enumerate(chips)]
        for cp in first:
            cp.start()
        passed = [copy(4 + j, (*chip, c), sibling) for j, chip in enumerate(chips)]
        for j, chip in enumerate(chips):
            copy(1 + j, (*chip, c), me).wait_recv()
            passed[j].start()
        copy(0, sibling, me).wait_recv()
        for j, chip in enumerate(chips):
            copy(4 + j, (*chip, 1 - c), me).wait_recv()
        for cp in first + passed:
            cp.wait_send()
        mine.wait()
        acc = gath[0]
        for k in range(1, 8):
            acc = acc + gath[k]
        out_ref[...] = acc

    vm = pl.BlockSpec(memory_space=pltpu.VMEM)
    return pl.pallas_call(
        body, out_shape=jax.ShapeDtypeStruct(v.shape, F32), in_specs=[vm], out_specs=vm,
        scratch_shapes=[pltpu.VMEM((8,) + v.shape, F32), pltpu.SemaphoreType.DMA((7,)),
                        pltpu.SemaphoreType.DMA((7,)), pltpu.SemaphoreType.DMA(())],
        name="allreduce_small")(v)


def _adamw(name, w, g, m, v):
    r, cols = w.shape
    budget = 262144
    tr = r if r * cols <= budget else next(c for c in (256, 128, 64, 32, 16, 8) if r % c == 0 and c * cols <= budget)

    def kern(w_ref, g_ref, m_ref, v_ref, d_ref, nm_ref, nv_ref):
        gv = g_ref[...]
        nm = ADAM_B1 * m_ref[...] + (1.0 - ADAM_B1) * gv
        nv = ADAM_B2 * v_ref[...] + (1.0 - ADAM_B2) * jnp.square(gv)
        m_hat = nm / (1.0 - ADAM_B1 ** ADAM_STEP)
        v_hat = nv / (1.0 - ADAM_B2 ** ADAM_STEP)
        d_ref[...] = -ADAM_LR * (m_hat / (jnp.sqrt(v_hat) + ADAM_EPS) + ADAM_WD * w_ref[...])
        nm_ref[...] = nm
        nv_ref[...] = nv

    spec = pl.BlockSpec((tr, cols), lambda i: (i, 0))
    shp = jax.ShapeDtypeStruct((r, cols), F32)
    return pl.pallas_call(kern, grid=(r // tr,), in_specs=[spec] * 4, out_specs=[spec] * 3, out_shape=[shp] * 3,
                          compiler_params=_cparams(("parallel",)), name=name)(w, g, m, v)


SHARDED = (("w_mod", 1), ("w1_gu", 1), ("w1_down", 0), ("w_in", 1), ("dw_weight", 1), ("w_conv_out", 0),
           ("w_alpha_f", 1), ("w_alpha_b", 1), ("w_gla_out", 0), ("w_out", 0), ("w2_gu", 1), ("w2_down", 0))
REPLICATED = ("c_ctx", "b_mod", "g_ffn1", "g_mix", "dw_bias", "conv_ln_g", "conv_ln_b", "b_alpha_f", "b_alpha_b",
              "gla_norm_g", "g_ffn2", "g_final")
WEIGHTS = ("c_ctx", "w_mod", "b_mod", "g_ffn1", "w1_gu", "w1_down", "g_mix", "w_in", "dw_weight", "dw_bias",
           "conv_ln_g", "conv_ln_b", "w_conv_out", "w_alpha_f", "b_alpha_f", "w_alpha_b", "b_alpha_b", "gla_norm_g",
           "w_gla_out", "w_out", "g_ffn2", "w2_gu", "w2_down", "g_final")
MATRICES = ("w_mod", "w1_gu", "w1_down", "w_in", "w_conv_out", "w_gla_out", "w_out", "w2_gu", "w2_down")


def _pack_flat(parts, align):
    flat = jnp.concatenate([p.reshape(-1) for p in parts])
    pad = (-flat.shape[0]) % align
    return jnp.concatenate([flat, jnp.zeros((pad,), flat.dtype)]).reshape(-1, 1024)


def _unpack_flat(flat2d, shapes):
    flat = flat2d.reshape(-1)
    out, off = [], 0
    for s in shapes:
        n = math.prod(s)
        out.append(flat[off:off + n].reshape(s))
        off += n
    return out


def kernel(x, c, ctx, c_ctx, w_mod, b_mod, g_ffn1, w1_gu, w1_down, g_mix, w_in, dw_weight, dw_bias, conv_ln_g, conv_ln_b, w_conv_out, w_alpha_f, b_alpha_f, w_alpha_b, b_alpha_b, gla_norm_g, w_gla_out, w_out, g_ffn2, w2_gu, w2_down, g_final, loss_target, m_c_ctx, m_w_mod, m_b_mod, m_g_ffn1, m_w1_gu, m_w1_down, m_g_mix, m_w_in, m_dw_weight, m_dw_bias, m_conv_ln_g, m_conv_ln_b, m_w_conv_out, m_w_alpha_f, m_b_alpha_f, m_w_alpha_b, m_b_alpha_b, m_gla_norm_g, m_w_gla_out, m_w_out, m_g_ffn2, m_w2_gu, m_w2_down, m_g_final, v_c_ctx, v_w_mod, v_b_mod, v_g_ffn1, v_w1_gu, v_w1_down, v_g_mix, v_w_in, v_dw_weight, v_dw_bias, v_conv_ln_g, v_conv_ln_b, v_w_conv_out, v_w_alpha_f, v_b_alpha_f, v_w_alpha_b, v_b_alpha_b, v_gla_norm_g, v_w_gla_out, v_w_out, v_g_ffn2, v_w2_gu, v_w2_down, v_g_final):
    given = dict(locals())
    w = {n: given[n] for n in WEIGHTS}
    m = {n: given["m_" + n] for n in WEIGHTS}
    v = {n: given["v_" + n] for n in WEIGHTS}

    def small_pack(dw, af, ab):
        return jnp.concatenate([dw, jnp.zeros((1, dw.shape[1]), F32), jnp.concatenate([af, ab], axis=1)], axis=0)

    shards = [w[n][0].astype(BF16) for n in MATRICES]
    shards.append(small_pack(w["dw_weight"][0], w["w_alpha_f"][0], w["w_alpha_b"][0]))
    got = dict(zip(MATRICES + ("small",), _gather_weights(shards)))

    wts = {n: w[n] for n in REPLICATED}
    for n in ("w_mod", "w1_gu", "w2_gu"):
        wts[n] = got[n]
    for n in ("w1_down", "w2_down", "w_conv_out", "w_gla_out", "w_out"):
        wts[n] = got[n].reshape(-1, D)
    wts["w_in_p"] = _perm_in_cols(jnp.concatenate([got["w_in"][j] for j in range(4)], axis=1))
    sm = got["small"]
    wts["dw_weight"] = jnp.concatenate([sm[j, :CONV_W] for j in range(4)], axis=1)
    zpad = jnp.zeros((128, HEADS * DK), BF16)
    w_af = jnp.concatenate([sm[j, 32:32 + LOWRANK, :DK] for j in range(4)], axis=1)
    w_ab = jnp.concatenate([sm[j, 32:32 + LOWRANK, DK:] for j in range(4)], axis=1)
    wts["w_alpha_f_pad"] = zpad.at[0:LOWRANK].set(w_af.astype(BF16))
    wts["w_alpha_b_pad"] = zpad.at[LOWRANK:2 * LOWRANK].set(w_ab.astype(BF16))

    loss, grad_x, grads = _local_step(x, c, ctx, loss_target, wts)
    loss = lax.psum(loss, ("x", "y", "c"))

    def pieces(n):
        if n == "w_in":
            g = _unperm_in_cols(grads["w_in_p"])
            return jnp.transpose(g.reshape(D, 4, D_IN // 4), (1, 0, 2))
        g = grads[n]
        if n in ("w_mod", "w1_gu", "w2_gu"):
            return g
        return g.reshape(4, g.shape[0] // 4, g.shape[1])
    gs = [pieces(n) for n in MATRICES]
    gs.append(jnp.stack([small_pack(grads["dw_weight"][:, 256 * j:256 * (j + 1)],
                                    grads["w_alpha_f"][:, DK * j:DK * (j + 1)],
                                    grads["w_alpha_b"][:, DK * j:DK * (j + 1)]) for j in range(4)]))
    tags = MATRICES + ("small",)
    place = jnp.stack([lax.axis_index("c"), 2 * lax.axis_index("x") + lax.axis_index("y")]).astype(jnp.int32)
    swapped = _swap_halves(gs)
    chip_sums = [_add_halves("grad_add_" + t, g, r, place) for t, g, r in zip(tags, gs, swapped)]
    landed = _scatter_chips(chip_sums)
    halves = [_sum_chips("grad_sum_" + t, cs, r, place) for t, cs, r in zip(tags, chip_sums, landed)]
    reduced = dict(zip(tags, _join_halves(halves)))
    g_shard = {n: reduced[n] for n in MATRICES}
    g_shard["dw_weight"] = reduced["small"][:CONV_W]
    g_shard["w_alpha_f"] = reduced["small"][32:32 + LOWRANK, :DK]
    g_shard["w_alpha_b"] = reduced["small"][32:32 + LOWRANK, DK:]

    rep_shapes = [w[n].shape for n in REPLICATED]
    small = _allreduce_small(_pack_flat([grads[n].reshape(w[n].shape) for n in REPLICATED], 8 * 1024))
    g_rep = dict(zip(REPLICATED, _unpack_flat(small, rep_shapes)))

    g_out, d_out, m_out, v_out = {}, {}, {}, {}
    for n, _ in SHARDED:
        s2 = w[n].shape[1:]
        d, nm, nv = _adamw("adamw_" + n, w[n].reshape(s2), g_shard[n], m[n].reshape(s2), v[n].reshape(s2))
        g_out[n] = g_shard[n].reshape(w[n].shape)
        d_out[n], m_out[n], v_out[n] = d.reshape(w[n].shape), nm.reshape(w[n].shape), nv.reshape(w[n].shape)
    pk = lambda t: _pack_flat([t[n] for n in REPLICATED], 8 * 1024)
    d, nm, nv = _adamw("adamw_vectors", pk(w), small, pk(m), pk(v))
    for n, dd, mm, vv in zip(REPLICATED, _unpack_flat(d, rep_shapes), _unpack_flat(nm, rep_shapes),
                             _unpack_flat(nv, rep_shapes)):
        g_out[n], d_out[n], m_out[n], v_out[n] = g_rep[n], dd, mm, vv

    return (loss, grad_x, *[g_out[n] for n in WEIGHTS], *[d_out[n] for n in WEIGHTS],
            *[m_out[n] for n in WEIGHTS], *[v_out[n] for n in WEIGHTS])
```

```python
import functools
import math

import jax
import jax.numpy as jnp
from jax import lax
from jax.experimental import pallas as pl
from jax.experimental.pallas import tpu as pltpu

F32, BF16 = jnp.float32, jnp.bfloat16
MESH = pl.DeviceIdType.MESH
HIGHEST = lax.Precision.HIGHEST

D = 1024
FF = 2816
HEADS, DK, DV = 4, 128, 256
LOWRANK = 16
CONV_W = 31
CHUNK = 64
TAU = 16.0
EPS = 1e-6
Q_SCALE = DK ** -0.5
TM = 256
D_IN = 7200
D_INP = 7296
LR_COL = 7168
VMEM_LIMIT = 52 * 1024 * 1024
WGRAD_VMEM = 40 * 1024 * 1024

ADAM_LR, ADAM_B1, ADAM_B2, ADAM_EPS, ADAM_WD, ADAM_STEP = 0.001, 0.9, 0.999, 1e-08, 0.01, 10


def _silu(x):
    return x * jax.nn.sigmoid(x)


def _rms(h, g):
    return h * lax.rsqrt(jnp.mean(h * h, axis=-1, keepdims=True) + EPS) * g


def _modnorm(x, g, shift, scale):
    return _rms(x, g) * (1 + scale) + shift


def _cparams(sem=None):
    return pltpu.CompilerParams(dimension_semantics=sem, vmem_limit_bytes=VMEM_LIMIT)


def _twice(v):
    return v, v


def tok(arr, width=None, cb=0, clamp=None):
    return ("tok", arr, arr.shape[1] if width is None else width, cb, clamp)


def mod(arr):
    return ("mod", arr)


def const(arr):
    return ("const", arr)


def _rowmap(name, body, n_tiles, ins, outs, *, tpb, nb):
    def modrow(i):
        return jnp.minimum(i // tpb, nb)

    in_specs, args = [], []
    for spec in ins:
        if spec[0] == "tok":
            _, arr, width, cb, clamp = spec
            if clamp is None:
                im = lambda i, cb=cb: (i, cb)
            else:
                im = lambda i, cb=cb, clamp=clamp: (jnp.minimum(i, clamp), cb)
            in_specs.append(pl.BlockSpec((TM, width), im))
        elif spec[0] == "mod":
            arr = spec[1]
            in_specs.append(pl.BlockSpec((1, 1, arr.shape[2]), lambda i: (modrow(i), 0, 0)))
        else:
            arr = spec[1]
            in_specs.append(pl.BlockSpec(arr.shape, lambda i, nd=arr.ndim: (0,) * nd))
        args.append(arr)
    out_specs, out_shapes = [], []
    for o in outs:
        if o[0] == "tok":
            _, rows, width, dtype = o
            out_shapes.append(jax.ShapeDtypeStruct((rows, width), dtype))
            out_specs.append(pl.BlockSpec((TM, width), lambda i: (i, 0)))
        elif o[0] == "tokT":
            _, rows, width, dtype = o
            out_shapes.append(jax.ShapeDtypeStruct((width, rows), dtype))
            out_specs.append(pl.BlockSpec((width, TM), lambda i: (0, i)))
        elif o[0] == "acc":
            _, rows, width = o
            out_shapes.append(jax.ShapeDtypeStruct((rows, width), F32))
            out_specs.append(pl.BlockSpec((rows, width), lambda i: (0, 0)))
        else:
            width = o[1]
            rows_visited = min((n_tiles - 1) // tpb, nb) + 1
            out_shapes.append(jax.ShapeDtypeStruct((rows_visited, 1, width), F32))
            out_specs.append(pl.BlockSpec((1, 1, width), lambda i: (modrow(i), 0, 0)))
    n_in = len(ins)

    def kern(*refs):
        i = pl.program_id(0)
        vals = []
        for r, spec in zip(refs[:n_in], ins):
            val = r[0] if spec[0] == "mod" else r[...]
            vals.append(val.astype(F32) if spec[0] == "tok" and val.dtype == BF16 else val)
        res = body(i, *vals)
        for r, o, val in zip(refs[n_in:], outs, res):
            if o[0] == "tok":
                r[...] = val.astype(r.dtype)
            elif o[0] == "tokT":
                r[...] = val.T.astype(r.dtype)
            elif o[0] == "acc":
                @pl.when(i == 0)
                def _():
                    r[...] = jnp.zeros(r.shape, F32)
                r[...] += jnp.broadcast_to(val, r.shape)
            else:
                first = jnp.logical_or(i == 0, modrow(i) != modrow(jnp.maximum(i - 1, 0)))

                @pl.when(first)
                def _():
                    r[...] = jnp.zeros(r.shape, F32)
                r[0] += val

    return pl.pallas_call(
        kern, grid=(n_tiles,), in_specs=in_specs, out_specs=out_specs, out_shape=out_shapes,
        compiler_params=_cparams(("arbitrary",)), name=name)(*args)


def _pick(n, cands):
    for c in cands:
        if n % c == 0:
            return c
    return n


def _mm(name, a, b, *, trans_b=False, out_dtype=F32, a_fn=None, bias=None, rows=None):
    m, k = a.shape if a.ndim == 2 else (a.shape[1], 2 * a.shape[2])
    m = m if rows is None else rows
    shard = b.shape[2] if b.ndim == 3 else None
    if trans_b:
        n = b.shape[-2]
        tk = _pick(shard, (2816, 2304, 1408, 1024)) if shard else (
            k if k <= 2816 else _pick(k, (2816, 2432, 2304, 2048, 1536, 1408, 1024, 512, 256, 128)))
        tn = _pick(n, (1024, 512, 384, 256, 128))
    else:
        n = 4 * shard if shard else b.shape[1]
        tk = k if k <= 2816 else _pick(k, (2816, 2432, 2304, 2048, 1536, 1408, 1024, 512, 256, 128))
        tn = _pick(shard, (512, 384, 1408, 256, 128)) if shard else _pick(n, (1024, 2432, 512, 384, 256, 128))
    tm = _pick(m, (1024, 512, 256))
    nk = k // tk
    per = shard // (tk if trans_b else tn) if shard else None
    dims = (((1,), (1,)), ((), ())) if trans_b else (((1,), (0,)), ((), ()))

    def kern(*refs):
        a_ref, b_ref = refs[0], refs[1]
        bias_ref = refs[2] if bias is not None else None
        o_ref, acc_ref = refs[-2], refs[-1]
        kk = pl.program_id(2)
        av = a_ref[...]
        if a_fn is not None:
            av = a_fn(av)
        p = lax.dot_general(av.astype(BF16), b_ref[...].astype(BF16), dims, preferred_element_type=F32)

        def finish(total):
            if bias_ref is not None:
                total = total + bias_ref[...]
            o_ref[...] = total.astype(o_ref.dtype)

        if nk == 1:
            finish(p)
        else:
            @pl.when(kk == 0)
            def _():
                acc_ref[...] = p

            @pl.when(kk > 0)
            def _():
                acc_ref[...] += p

            @pl.when(kk == nk - 1)
            def _():
                finish(acc_ref[...])

    if shard and trans_b:
        b_spec = pl.BlockSpec((None, tn, tk), lambda i, j, kk: (kk // per, j, kk % per))
    elif shard:
        b_spec = pl.BlockSpec((None, tk, tn), lambda i, j, kk: (j // per, kk, j % per))
    elif trans_b:
        b_spec = pl.BlockSpec((tn, tk), lambda i, j, kk: (j, kk))
    else:
        b_spec = pl.BlockSpec((tk, tn), lambda i, j, kk: (kk, j))
    if a.ndim == 3:
        pa = a.shape[2] // tk
        a_spec = pl.BlockSpec((None, tm, tk), lambda i, j, kk: (kk // pa, i, kk % pa))
    else:
        a_spec = pl.BlockSpec((tm, tk), lambda i, j, kk: (i, kk))
    in_specs = [a_spec, b_spec]
    args = [a, b]
    if bias is not None:
        in_specs.append(pl.BlockSpec((1, tn), lambda i, j, kk: (0, j)))
        args.append(bias)
    return pl.pallas_call(
        kern, grid=(m // tm, n // tn, nk), in_specs=in_specs,
        out_specs=pl.BlockSpec((tm, tn), lambda i, j, kk: (i, j)),
        out_shape=jax.ShapeDtypeStruct((m, n), out_dtype),
        scratch_shapes=[pltpu.VMEM((tm, tn) if nk > 1 else (8, 128), F32)],
        compiler_params=_cparams(("parallel", "parallel", "arbitrary")), name=name)(*args)


def _mm_tn(name, x, dy, rows=None, col_shards=False):
    t = x.shape[0] if rows is None else rows
    k1, n1 = x.shape[1], (dy.shape[1] if dy.ndim == 2 else 2 * dy.shape[2])
    tt = _pick(t, (512, 256, 128, 64, 8))
    tk1 = _pick(k1, (1024, 1408, 512, 256, 128))
    tn = _pick(n1 // 4, (512, 384, 1408, 256, 128)) if col_shards else _pick(n1, (512, 384, 256, 128))
    per = n1 // 4 // tn
    ns = t // tt

    def kern(x_ref, dy_ref, o_ref, acc_ref):
        s = pl.program_id(2)
        p = lax.dot_general(x_ref[...].astype(BF16), dy_ref[...].astype(BF16), (((0,), (0,)), ((), ())),
                            preferred_element_type=F32)

        @pl.when(s == 0)
        def _():
            acc_ref[...] = p

        @pl.when(s > 0)
        def _():
            acc_ref[...] += p

        @pl.when(s == ns - 1)
        def _():
            o_ref[...] = acc_ref[...].astype(o_ref.dtype)

    if col_shards:
        out_spec = pl.BlockSpec((None, tk1, tn), lambda i, j, s: (j // per, i, j % per))
        out_shape = jax.ShapeDtypeStruct((4, k1, n1 // 4), BF16)
    else:
        out_spec = pl.BlockSpec((tk1, tn), lambda i, j, s: (i, j))
        out_shape = jax.ShapeDtypeStruct((k1, n1), BF16)
    if dy.ndim == 3:
        pd = dy.shape[2] // tn
        dy_spec = pl.BlockSpec((None, tt, tn), lambda i, j, s: (j // pd, s, j % pd))
    else:
        dy_spec = pl.BlockSpec((tt, tn), lambda i, j, s: (s, j))
    return pl.pallas_call(
        kern, grid=(k1 // tk1, n1 // tn, ns),
        in_specs=[pl.BlockSpec((tt, tk1), lambda i, j, s: (s, i)), dy_spec],
        out_specs=out_spec, out_shape=out_shape, scratch_shapes=[pltpu.VMEM((tk1, tn), F32)],
        compiler_params=_cparams(("parallel", "parallel", "arbitrary")), name=name)(x, dy)


def _wgrad(name, xt, dy, rows=None, col_shards=False):
    k1 = xt.shape[0]
    t = xt.shape[1] if rows is None else rows
    n1 = dy.shape[1] if dy.ndim == 2 else 2 * dy.shape[2]
    tm = _pick(k1, (1024, 1408, 512, 256))
    tn = _pick(n1 // 4, (1408, 512, 384, 256, 128)) if col_shards else _pick(n1, (1024, 2432, 512, 384, 256, 128))
    fixed = tm * tn * (4 + 4 + 2 * 2)
    tk = next((c for c in (2048, 1536, 1024, 512, 256, 128)
               if t % c == 0 and fixed + 4 * c * (tm + tn) <= WGRAD_VMEM), 128)
    ns = t // tk
    per = n1 // 4 // tn

    def kern(x_ref, dy_ref, o_ref, acc_ref):
        s = pl.program_id(2)
        p = jnp.dot(x_ref[...], dy_ref[...], preferred_element_type=F32)

        @pl.when(s == 0)
        def _():
            acc_ref[...] = p

        @pl.when(s > 0)
        def _():
            acc_ref[...] += p

        @pl.when(s == ns - 1)
        def _():
            o_ref[...] = acc_ref[...].astype(o_ref.dtype)

    if dy.ndim == 3:
        pd = dy.shape[2] // tn
        dy_spec = pl.BlockSpec((None, tk, tn), lambda i, j, s: (j // pd, s, j % pd))
    else:
        dy_spec = pl.BlockSpec((tk, tn), lambda i, j, s: (s, j))
    if col_shards:
        out_spec = pl.BlockSpec((None, tm, tn), lambda i, j, s: (j // per, i, j % per))
        out_shape = jax.ShapeDtypeStruct((4, k1, n1 // 4), BF16)
    else:
        out_spec = pl.BlockSpec((tm, tn), lambda i, j, s: (i, j))
        out_shape = jax.ShapeDtypeStruct((k1, n1), BF16)
    return pl.pallas_call(
        kern, grid=(k1 // tm, n1 // tn, ns),
        in_specs=[pl.BlockSpec((tm, tk), lambda i, j, s: (i, s)), dy_spec],
        out_specs=out_spec, out_shape=out_shape, scratch_shapes=[pltpu.VMEM((tm, tn), F32)],
        compiler_params=_cparams(("parallel", "parallel", "arbitrary")), name=name)(xt, dy)


def _swiglu_fwd(name, u, w_gu):
    m = u.shape[0]
    half = w_gu.shape[2]
    tm = _pick(m, (512, 256))

    def kern(u_ref, wa_ref, wb_ref, ab_ref, hm_ref, hmt_ref):
        uv = u_ref[...]
        a = jnp.dot(uv, wa_ref[...], preferred_element_type=F32)
        b = jnp.dot(uv, wb_ref[...], preferred_element_type=F32)
        ab_ref[0] = a.astype(BF16)
        ab_ref[1] = b.astype(BF16)
        hm = (_silu(a) * b).astype(BF16)
        hm_ref[...] = hm
        hmt_ref[...] = hm.T

    return pl.pallas_call(
        kern, grid=(m // tm, 2),
        in_specs=[pl.BlockSpec((tm, D), lambda i, j: (i, 0)),
                  pl.BlockSpec((None, D, half), lambda i, j: (j, 0, 0)),
                  pl.BlockSpec((None, D, half), lambda i, j: (2 + j, 0, 0))],
        out_specs=[pl.BlockSpec((2, tm, half), lambda i, j: (0, i, j)),
                   pl.BlockSpec((tm, half), lambda i, j: (i, j)),
                   pl.BlockSpec((half, tm), lambda i, j: (j, i))],
        out_shape=[jax.ShapeDtypeStruct((2, m, FF), BF16), jax.ShapeDtypeStruct((m, FF), BF16),
                   jax.ShapeDtypeStruct((FF, m), BF16)],
        compiler_params=_cparams(("parallel", "parallel")), name=name)(u, w_gu, w_gu)


def _swiglu_bwd(name, df, w_down, ab):
    m = df.shape[0]
    half = FF // 2
    tm = _pick(m, (512, 256))

    def kern(df_ref, w_ref, ab_ref, o_ref):
        dh = lax.dot_general(df_ref[...], w_ref[...], (((1,), (1,)), ((), ())), preferred_element_type=F32)
        a = ab_ref[0].astype(F32)
        b = ab_ref[1].astype(F32)
        s = jax.nn.sigmoid(a)
        o_ref[0] = (dh * b * (s * (1 + a * (1 - s)))).astype(BF16)
        o_ref[1] = (dh * (a * s)).astype(BF16)

    return pl.pallas_call(
        kern, grid=(m // tm, 2),
        in_specs=[pl.BlockSpec((tm, D), lambda i, j: (i, 0)),
                  pl.BlockSpec((half, D), lambda i, j: (j, 0)),
                  pl.BlockSpec((2, tm, half), lambda i, j: (0, i, j))],
        out_specs=pl.BlockSpec((2, tm, half), lambda i, j: (0, i, j)),
        out_shape=jax.ShapeDtypeStruct((2, m, FF), BF16),
        compiler_params=_cparams(("parallel", "parallel")), name=name)(df, w_down, ab)


def _gla_maps(bl, t, tc):
    nx, nc = t // CHUNK, tc // CHUNK
    nxb = bl * nx

    def rowblk(d, b, n):
        c_ctx = jnp.where(d == 0, n, nc - 1 - n)
        c_x = jnp.where(d == 0, n - nc, nx - 1 - (n - nc))
        return jnp.where(n < nc, nxb + b * nc + c_ctx, b * nx + c_x)

    def xblk(d, b, n):
        n2 = jnp.maximum(n, nc)
        return b * nx + jnp.where(d == 0, n2 - nc, nx - 1 - (n2 - nc))

    return nx, nc, rowblk, xblk


def _dot01(m, x, cm):
    x1 = x.astype(BF16)
    r1 = x - x1.astype(F32)
    x2 = r1.astype(BF16)
    x3 = (r1 - x2.astype(F32)).astype(BF16)
    w = x.shape[1]
    p = lax.dot_general(m.astype(BF16), jnp.concatenate([x1, x2, x3], axis=1), (((cm,), (0,)), ((), ())),
                        preferred_element_type=F32)
    return p[:, :w] + p[:, w:2 * w] + p[:, 2 * w:]


def _gla_chunk(m, q, k, v, g, h):
    gh = g[:, h * DK:(h + 1) * DK]
    b = _dot01(m, gh, 1)
    tot = jnp.sum(gh, axis=0, keepdims=True)
    mid = b[CHUNK // 2:CHUNK // 2 + 1, :]
    qh = q[:, h * DK:(h + 1) * DK] * Q_SCALE
    kh = k[:, h * DK:(h + 1) * DK]
    vh = v[:, h * DV:(h + 1) * DV]
    return b, tot, mid, qh, kh, vh


def _dot(a, b, ca, cb):
    return lax.dot_general(a.astype(BF16), b.astype(BF16), (((ca,), (cb,)), ((), ())),
                           preferred_element_type=F32)


def _gla_fwd(p, g2, mmats, bl, t, tc):
    nx, nc, rowblk, xblk = _gla_maps(bl, t, tc)
    ns = nx + nc

    def kern(q0, k0, v0, g0, q1, k1, v1, g1, m_ref, o0, o1, ss_ref, s_ref):
        n = pl.program_id(1)

        @pl.when(n == 0)
        def _():
            s_ref[...] = jnp.zeros(s_ref.shape, F32)
        sides = ((q0, k0, v0, g0, o0), (q1, k1, v1, g1, o1))
        loaded = [(m_ref[d], q_ref[...].astype(F32), k_ref[...].astype(F32), v_ref[...].astype(F32), g_ref[...])
                  for d, (q_ref, k_ref, v_ref, g_ref, _) in enumerate(sides)]
        chains = [(d, h) for d in range(2) for h in range(HEADS)]
        base = [_gla_chunk(*loaded[d], h) for d, h in chains]
        pre = []
        for (d, h), (b, tot, mid, qh, kh, vh) in zip(chains, base):
            s0 = s_ref[d, h * DV:(h + 1) * DV, :]
            ss_ref[d, 0, 0, h * DV:(h + 1) * DV, :] = s0
            pre.append((s0, kh * jnp.exp(tot - b), qh * jnp.exp(b), qh * jnp.exp(b - mid), kh * jnp.exp(mid - b)))
        raw = [(_dot(qm, km, 1, 1), _dot(qe, s0, 1, 1), _dot(bs[5], kl, 0, 0))
               for bs, (s0, kl, qe, qm, km) in zip(base, pre)]
        for (d, h), bs, (s0, kl, qe, qm, km), (att_raw, inter, s_add) in zip(chains, base, pre, raw):
            s_ref[d, h * DV:(h + 1) * DV, :] = s0 * jnp.exp(bs[1]) + s_add
            sides[d][4][:, h * DV:(h + 1) * DV] = inter + _dot(loaded[d][0] * att_raw, bs[5], 1, 0)

    def operands(d):
        return [pl.BlockSpec((CHUNK, 512), lambda b, n: (rowblk(d, b, n), 4)),
                pl.BlockSpec((CHUNK, 512), lambda b, n: (rowblk(d, b, n), 5)),
                pl.BlockSpec((CHUNK, 1024), lambda b, n: (rowblk(d, b, n), 3)),
                pl.BlockSpec((CHUNK, 512), lambda b, n: (rowblk(d, b, n), d))]

    o_shape = jax.ShapeDtypeStruct((bl * t, HEADS * DV), F32)
    return pl.pallas_call(
        kern, grid=(bl, ns),
        in_specs=operands(0) + operands(1) + [pl.BlockSpec((2, CHUNK, CHUNK), lambda b, n: (0, 0, 0))],
        out_specs=[pl.BlockSpec((CHUNK, 1024), lambda b, n: (xblk(0, b, n), 0)),
                   pl.BlockSpec((CHUNK, 1024), lambda b, n: (xblk(1, b, n), 0)),
                   pl.BlockSpec((2, 1, 1, HEADS * DV, DK), lambda b, n: (0, b, n, 0, 0))],
        out_shape=[o_shape, o_shape, jax.ShapeDtypeStruct((2, bl, ns, HEADS * DV, DK), F32)],
        scratch_shapes=[pltpu.VMEM((2, HEADS * DV, DK), F32)],
        compiler_params=_cparams(("parallel", "arbitrary")), name="gla_fwd")(p, p, p, g2, p, p, p, g2, mmats)


def _gla_bwd(p, g2, mmats, ssave, do, bl, t, tc):
    nx, nc, rowblk, xblk = _gla_maps(bl, t, tc)
    ns = nx + nc
    rev = lambda s: ns - 1 - s

    def kern(q0, k0, v0, g0, do0, q1, k1, v1, g1, do1, m_ref, ss_ref,
             dq0, dk0, dv0, dg0, dq1, dk1, dv1, dg1, ds_ref):
        step = pl.program_id(1)
        n = ns - 1 - step

        @pl.when(step == 0)
        def _():
            ds_ref[...] = jnp.zeros(ds_ref.shape, F32)
        live = (n >= nc).astype(F32)
        sides = ((q0, k0, v0, g0, do0, dq0, dk0, dv0, dg0), (q1, k1, v1, g1, do1, dq1, dk1, dv1, dg1))
        loaded = [(m_ref[d], s[0][...].astype(F32), s[1][...].astype(F32), s[2][...].astype(F32), s[3][...])
                  for d, s in enumerate(sides)]
        dovs = [s[4][...] * live for s in sides]
        chains = [(d, h) for d in range(2) for h in range(HEADS)]
        base = [_gla_chunk(*loaded[d], h) for d, h in chains]
        pre = []
        for (d, h), (b, tot, mid, qh, kh, vh) in zip(chains, base):
            eb, ebm, emb, etb = jnp.exp(b), jnp.exp(b - mid), jnp.exp(mid - b), jnp.exp(tot - b)
            pre.append(dict(
                eb=eb, ebm=ebm, emb=emb, etb=etb, etot=jnp.exp(tot), qe=qh * eb, qm=qh * ebm, km=kh * emb, kl=kh * etb,
                vh=vh, doh=dovs[d][:, h * DV:(h + 1) * DV], s0=ss_ref[d, 0, 0, h * DV:(h + 1) * DV, :],
                ds1=ds_ref[d, h * DV:(h + 1) * DV, :]))
        first = [dict(att=_dot(c["qm"], c["km"], 1, 1), datt=_dot(c["doh"], c["vh"], 1, 1),
                      dqe=_dot(c["doh"], c["s0"], 1, 0), ds_add=_dot(c["doh"], c["qe"], 0, 0),
                      dkl=_dot(c["vh"], c["ds1"], 1, 0), dv_s=_dot(c["kl"], c["ds1"], 1, 1)) for c in pre]
        second = []
        for (d, h), c, f in zip(chains, pre, first):
            m = loaded[d][0]
            ds_ref[d, h * DV:(h + 1) * DV, :] = c["ds1"] * c["etot"] + f["ds_add"]
            att, datt = m * f["att"], m * f["datt"]
            second.append(dict(dqm=_dot(datt, c["km"], 1, 0), dkm=_dot(datt, c["qm"], 0, 0),
                               dv_a=_dot(att, c["doh"], 0, 0)))
        for (d, h), c, f, s in zip(chains, pre, first, second):
            dq_ref, dk_ref, dv_ref, dg_ref = sides[d][5:]
            dtot = c["etot"] * jnp.sum(c["ds1"] * c["s0"], axis=0, keepdims=True) + jnp.sum(
                f["dkl"] * c["kl"], axis=0, keepdims=True)
            db = f["dqe"] * c["qe"] + s["dqm"] * c["qm"] - s["dkm"] * c["km"] - f["dkl"] * c["kl"]
            dq_ref[:, h * DK:(h + 1) * DK] = (f["dqe"] * c["eb"] + s["dqm"] * c["ebm"]) * Q_SCALE
            dk_ref[:, h * DK:(h + 1) * DK] = s["dkm"] * c["emb"] + f["dkl"] * c["etb"]
            dv_ref[:, h * DV:(h + 1) * DV] = s["dv_a"] + f["dv_s"]
            dg_ref[:, h * DK:(h + 1) * DK] = _dot01(loaded[d][0], db, 0) + dtot

    nt = p.shape[0]

    def operands(d):
        return [pl.BlockSpec((CHUNK, 512), lambda b, s: (rowblk(d, b, rev(s)), 4)),
                pl.BlockSpec((CHUNK, 512), lambda b, s: (rowblk(d, b, rev(s)), 5)),
                pl.BlockSpec((CHUNK, 1024), lambda b, s: (rowblk(d, b, rev(s)), 3)),
                pl.BlockSpec((CHUNK, 512), lambda b, s: (rowblk(d, b, rev(s)), d)),
                pl.BlockSpec((CHUNK, 1024), lambda b, s: (xblk(d, b, rev(s)), 0))]

    def results(d):
        row = lambda b, s: (rowblk(d, b, rev(s)), 0)
        return [pl.BlockSpec((CHUNK, 512), row), pl.BlockSpec((CHUNK, 512), row), pl.BlockSpec((CHUNK, 1024), row),
                pl.BlockSpec((CHUNK, 512), row)]

    shapes = [jax.ShapeDtypeStruct((nt, 512), F32), jax.ShapeDtypeStruct((nt, 512), F32),
              jax.ShapeDtypeStruct((nt, 1024), F32), jax.ShapeDtypeStruct((nt, 512), F32)]
    out = pl.pallas_call(
        kern, grid=(bl, ns),
        in_specs=operands(0) + operands(1) + [
            pl.BlockSpec((2, CHUNK, CHUNK), lambda b, s: (0, 0, 0)),
            pl.BlockSpec((2, 1, 1, HEADS * DV, DK), lambda b, s: (0, b, rev(s), 0, 0))],
        out_specs=results(0) + results(1), out_shape=shapes + shapes,
        scratch_shapes=[pltpu.VMEM((2, HEADS * DV, DK), F32)],
        compiler_params=_cparams(("parallel", "arbitrary")), name="gla_bwd")(
            p, p, p, g2, do, p, p, p, g2, do, mmats, ssave)
    return out[:4], out[4:]


CONV_CT = 256
CONV_PAD = 16
CONV_RC = 128
CONV_HALO = 24


def _conv_fill(zp, z_ref, t):
    zp[0:CONV_PAD, :] = jnp.zeros((CONV_PAD, CONV_CT), F32)
    zp[CONV_PAD + t:2 * CONV_PAD + t, :] = jnp.zeros((CONV_PAD, CONV_CT), F32)
    zp[CONV_PAD:CONV_PAD + t, :] = z_ref[...]


def _dwconv(name, z, w, bias, bl, t, flip):
    def kern(z_ref, w_ref, b_ref, o_ref, zp):
        _conv_fill(zp, z_ref, t)
        offs = [(CONV_W - j) if flip else (j + 1) for j in range(CONV_W)]
        for r in range(0, t, CONV_RC):
            acc = jnp.broadcast_to(b_ref[...], (CONV_RC, CONV_CT))
            for rot in range(8):
                win = zp[r + rot:r + rot + CONV_RC + CONV_HALO, :]
                for j in range(CONV_W):
                    if offs[j] % 8 == rot:
                        a = offs[j] - rot
                        acc = acc + w_ref[j:j + 1, :] * win[a:a + CONV_RC, :]
            o_ref[r:r + CONV_RC, :] = acc

    return pl.pallas_call(
        kern, grid=(bl, 1024 // CONV_CT),
        in_specs=[pl.BlockSpec((t, CONV_CT), lambda b, c: (b, c)),
                  pl.BlockSpec((32, CONV_CT), lambda b, c: (0, c)),
                  pl.BlockSpec((1, CONV_CT), lambda b, c: (0, c))],
        out_specs=pl.BlockSpec((t, CONV_CT), lambda b, c: (b, c)),
        out_shape=jax.ShapeDtypeStruct(z.shape, F32),
        scratch_shapes=[pltpu.VMEM((t + 2 * CONV_PAD, CONV_CT), F32)],
        compiler_params=_cparams(("parallel", "parallel")), name=name)(z, w, bias)


def _dwconv_wgrad(z, dzc, bl, t):
    def kern(z_ref, d_ref, dw_ref, db_ref, zp):
        b = pl.program_id(1)

        @pl.when(b == 0)
        def _():
            dw_ref[...] = jnp.zeros(dw_ref.shape, F32)
            db_ref[...] = jnp.zeros(db_ref.shape, F32)
        _conv_fill(zp, z_ref, t)
        for rot in range(8):
            taps = [j for j in range(CONV_W) if (j + 1) % 8 == rot]
            accs = [jnp.zeros((8, CONV_CT), F32) for _ in taps]
            for r in range(0, t, CONV_RC):
                d = d_ref[r:r + CONV_RC, :]
                win = zp[r + rot:r + rot + CONV_RC + CONV_HALO, :]
                for k, j in enumerate(taps):
                    a = j + 1 - rot
                    prod = d * win[a:a + CONV_RC, :]
                    accs[k] = accs[k] + jnp.sum(prod.reshape(CONV_RC // 8, 8, CONV_CT), axis=0)
            for k, j in enumerate(taps):
                dw_ref[j:j + 1, :] += jnp.sum(accs[k], axis=0, keepdims=True)
        db_ref[...] += jnp.sum(d_ref[...], axis=0, keepdims=True)

    return pl.pallas_call(
        kern, grid=(1024 // CONV_CT, bl),
        in_specs=[pl.BlockSpec((t, CONV_CT), lambda c, b: (b, c)),
                  pl.BlockSpec((t, CONV_CT), lambda c, b: (b, c))],
        out_specs=[pl.BlockSpec((32, CONV_CT), lambda c, b: (0, c)),
                   pl.BlockSpec((1, CONV_CT), lambda c, b: (0, c))],
        out_shape=[jax.ShapeDtypeStruct((32, 1024), F32), jax.ShapeDtypeStruct((1, 1024), F32)],
        scratch_shapes=[pltpu.VMEM((t + 2 * CONV_PAD, CONV_CT), F32)],
        compiler_params=_cparams(("parallel", "arbitrary")), name="dwconv_wgrad")(z, dzc)


def _ffn_fwd(tag, xin, n_tiles, g, sh, sc, gate, w_gu, w_down, tpb, nb):
    rows = n_tiles * TM
    rm = functools.partial(_rowmap, tpb=tpb, nb=nb)
    u, ut = rm(tag + "_norm", lambda i, x, g_, sh_, sc_: _twice(_modnorm(x, g_, sh_, sc_)), n_tiles,
               [tok(xin), const(g), mod(sh), mod(sc)], [("tok", rows, D, BF16), ("tokT", rows, D, BF16)])
    ab, hm, hmt = _swiglu_fwd(tag + "_gu", u, w_gu)
    f = _mm(tag + "_down", hm, w_down)
    (xout,) = rm(tag + "_res", lambda i, x, f_, gt: (x + 0.5 * gt * f_,), n_tiles,
                 [tok(xin), tok(f), mod(gate)], [("tok", rows, D, F32)])
    return xout, (ut, ab, hmt, f)


def _ffn_bwd(tag, xin, saved, dxout, dx_clamp, n_tiles, g, sh, sc, gate, w_gu, w_down, tpb, nb):
    ut, ab, hmt, f = saved
    rows = n_tiles * TM
    rm = functools.partial(_rowmap, tpb=tpb, nb=nb)

    def mask(i):
        return 1.0 if dx_clamp is None else (i <= dx_clamp).astype(F32)

    def b1(i, dx, f_, gt):
        dx = dx * mask(i)
        return (0.5 * gt * dx, jnp.sum(0.5 * f_ * dx, axis=0, keepdims=True))
    df, dgate = rm(tag + "_bres", b1, n_tiles, [tok(dxout, clamp=dx_clamp), tok(f), mod(gate)],
                   [("tok", rows, D, BF16), ("modacc", D)])
    dw_down = _wgrad(tag + "_wdown", hmt, df)
    dab = _swiglu_bwd(tag + "_bdown", df, w_down, ab)
    du = _mm(tag + "_bgu", dab, w_gu, trans_b=True)
    dw_gu = _wgrad(tag + "_wgu", ut, dab, col_shards=True)

    def b3(i, x, g_, sh_, sc_, du_, dx):
        _, vjp = jax.vjp(_modnorm, x, g_, sh_, sc_)
        dxn, dg, dsh, dsc = vjp(du_)
        return (dx * mask(i) + dxn, dg, dsh, dsc)
    dxin, dg, dsh, dsc = rm(tag + "_bnorm", b3, n_tiles,
                            [tok(xin), const(g), mod(sh), mod(sc), tok(du), tok(dxout, clamp=dx_clamp)],
                            [("tok", rows, D, F32), ("acc", 1, D), ("modacc", D), ("modacc", D)])
    return dxin, dict(g=dg, sh=dsh, sc=dsc, gate=dgate, w_gu=dw_gu, w_down=dw_down)


def _perm_in_cols(w):
    pad = jnp.zeros(w.shape[:-1] + (D_INP - D_IN,), w.dtype)
    return jnp.concatenate([w[..., :5120], w[..., 5152:7200], w[..., 5120:5152], pad], axis=-1)


def _unperm_in_cols(w):
    return jnp.concatenate([w[..., :5120], w[..., LR_COL:LR_COL + 32], w[..., 5120:LR_COL]], axis=-1)


def _local_step(x, c, ctx, target, wts):
    bl, t, _ = x.shape
    tc = ctx.shape[1]
    nx_rows, nc_rows = bl * t, bl * tc
    nt_rows = nx_rows + nc_rows
    tpb = t // TM
    nxt, ntt = nx_rows // TM, nt_rows // TM
    nb = bl
    rm = functools.partial(_rowmap, tpb=tpb, nb=nb)
    last_x = nxt - 1

    x0 = jnp.concatenate([x.reshape(nx_rows, D), ctx.reshape(nc_rows, D)], axis=0)
    tgt = target.reshape(nx_rows, D)

    cc = jnp.concatenate([c, wts["c_ctx"].reshape(1, D), jnp.zeros((8 - bl - 1, D), F32)], axis=0)
    modv = _mm("mod_fwd", cc, wts["w_mod"], a_fn=_silu, bias=wts["b_mod"])
    mods = [modv[:nb + 1, k * D:(k + 1) * D].reshape(nb + 1, 1, D) for k in range(9)]

    x1, sv1 = _ffn_fwd("ffn1", x0, ntt, wts["g_ffn1"], mods[0], mods[1], mods[2], wts["w1_gu"], wts["w1_down"],
                       tpb, nb)
    u2, u2t = rm("in_norm", lambda i, x_, g_, sh_, sc_: _twice(_modnorm(x_, g_, sh_, sc_)), ntt,
                 [tok(x1), const(wts["g_mix"]), mod(mods[3]), mod(mods[4])],
                 [("tok", nt_rows, D, BF16), ("tokT", nt_rows, D, BF16)])
    w_inp = wts["w_in_p"]
    p = _mm("in_proj", u2, w_inp, out_dtype=BF16)

    waf, wab, baf, bab = wts["w_alpha_f_pad"], wts["w_alpha_b_pad"], wts["b_alpha_f"], wts["b_alpha_b"]

    def dec_fwd(i, lr, wf, wb, bf_, bb_):
        zf = _dot(lr, wf, 1, 0) + bf_
        zb = _dot(lr, wb, 1, 0) + bb_
        return (jnp.concatenate([jax.nn.log_sigmoid(zf) / TAU, jax.nn.log_sigmoid(zb) / TAU], axis=1),)
    (gfb,) = rm("decay_fwd", dec_fwd, ntt, [tok(p, 128, LR_COL // 128), const(waf), const(wab), const(baf), const(bab)],
                [("tok", nt_rows, 1024, F32)])
    g2 = gfb
    tri = jnp.tril(jnp.ones((CHUNK, CHUNK), F32))
    mmats = jnp.stack([tri, tri.T])
    *o2, ssave = _gla_fwd(p, g2, mmats, bl, t, tc)

    gn_g = wts["gla_norm_g"]

    def gla_out(of, ob, og, gn):
        o = of + ob
        parts = []
        for h in range(HEADS):
            oh = o[:, h * DV:(h + 1) * DV]
            parts.append(oh * lax.rsqrt(jnp.mean(oh * oh, axis=-1, keepdims=True) + EPS))
        return jnp.concatenate(parts, axis=1) * gn * _silu(og)
    yg_in, yg_int = rm("gla_out", lambda i, of, ob, og, gn: _twice(gla_out(of, ob, og, gn)), nxt,
                       [tok(o2[0]), tok(o2[1]), tok(p, 1024, 4), const(gn_g)],
                       [("tok", nx_rows, D, BF16), ("tokT", nx_rows, D, BF16)])
    y_gla = _mm("gla_proj", yg_in, wts["w_gla_out"])

    (z,) = rm("glu", lambda i, a, b: (a * jax.nn.sigmoid(b),), nxt, [tok(p, 1024, 0), tok(p, 1024, 1)],
              [("tok", nx_rows, D, F32)])
    dw_w = jnp.concatenate([wts["dw_weight"], jnp.zeros((1, D), F32)], axis=0)
    zc = _dwconv("dwconv_fwd", z, dw_w, wts["dw_bias"], bl, t, False)

    def ln_silu(zc_, g_, b_):
        mu = jnp.mean(zc_, axis=-1, keepdims=True)
        var = jnp.mean(jnp.square(zc_ - mu), axis=-1, keepdims=True)
        return _silu((zc_ - mu) * lax.rsqrt(var + EPS) * g_ + b_)
    ln_g, ln_b = wts["conv_ln_g"], wts["conv_ln_b"]
    zl, zlt = rm("conv_ln", lambda i, zc_, g_, b_: _twice(ln_silu(zc_, g_, b_)), nxt,
                 [tok(zc), const(ln_g), const(ln_b)], [("tok", nx_rows, D, BF16), ("tokT", nx_rows, D, BF16)])
    y_conv = _mm("conv_proj", zl, wts["w_conv_out"])

    mg, mgt = rm("merge", lambda i, ga, gb, yc, yg: _twice(jax.nn.sigmoid(ga) * yc + jax.nn.sigmoid(gb) * yg), nxt,
                 [tok(p, 1024, 5), tok(p, 1024, 6), tok(y_conv), tok(y_gla)],
                 [("tok", nx_rows, D, BF16), ("tokT", nx_rows, D, BF16)])
    mix = _mm("out_proj", mg, wts["w_out"])
    (x2,) = rm("mix_res", lambda i, x_, mx_, gt: (x_ + gt * mx_,), nxt, [tok(x1), tok(mix), mod(mods[5])],
               [("tok", nx_rows, D, F32)])

    x3, sv2 = _ffn_fwd("ffn2", x2, nxt, wts["g_ffn2"], mods[6], mods[7], mods[8], wts["w2_gu"], wts["w2_down"],
                       tpb, nb)
    g_fin = wts["g_final"].reshape(1, D)

    def head(i, x_, g_, tg):
        y, vjp = jax.vjp(_rms, x_, g_)
        diff = y - tg
        dx, dg = vjp(diff * (1.0 / D))
        loss = 0.5 * jnp.sum(jnp.mean(diff * diff, axis=-1, keepdims=True))
        return dx, dg, loss
    dx3, dg_final, loss_acc = rm("loss_head", head, nxt, [tok(x3), const(g_fin), tok(tgt)],
                                 [("tok", nx_rows, D, F32), ("acc", 1, D), ("acc", 8, 128)])
    loss = loss_acc[0, 0]

    grads = {}
    dx2, gf2 = _ffn_bwd("ffn2", x2, sv2, dx3, None, nxt, wts["g_ffn2"], mods[6], mods[7], mods[8],
                        wts["w2_gu"], wts["w2_down"], tpb, nb)
    grads.update(g_ffn2=gf2["g"], w2_gu=gf2["w_gu"], w2_down=gf2["w_down"])

    dmix, dgate5 = rm("mix_bres", lambda i, dx, mx_, gt: (gt * dx, jnp.sum(mx_ * dx, axis=0, keepdims=True)), nxt,
                      [tok(dx2), tok(mix), mod(mods[5])], [("tok", nx_rows, D, BF16), ("modacc", D)])
    dmg = _mm("out_bproj", dmix, wts["w_out"], trans_b=True)
    grads["w_out"] = _wgrad("out_wgrad", mgt, dmix)

    def merge_bwd(i, dm, ga, gb, yc, yg):
        keep = (i <= last_x).astype(F32)
        dm = dm * keep
        sa, sb = jax.nn.sigmoid(ga), jax.nn.sigmoid(gb)
        return dm * sa, dm * sb, dm * yc * sa * (1 - sa), dm * yg * sb * (1 - sb)
    cl = dict(clamp=last_x)
    dyc, dyg, dga, dgb = rm("merge_bwd", merge_bwd, ntt,
                            [tok(dmg, **cl), tok(p, 1024, 5, last_x), tok(p, 1024, 6, last_x), tok(y_conv, **cl),
                             tok(y_gla, **cl)],
                            [("tok", nt_rows, D, BF16)] * 4)

    dzl = _mm("conv_bproj", dyc, wts["w_conv_out"], trans_b=True, rows=nx_rows)
    grads["w_conv_out"] = _wgrad("conv_wgrad", zlt, dyc, rows=nx_rows)

    def ln_bwd(i, zc_, g_, b_, dz_):
        _, vjp = jax.vjp(ln_silu, zc_, g_, b_)
        return vjp(dz_)
    dzc, dln_g, dln_b = rm("conv_ln_bwd", ln_bwd, nxt, [tok(zc), const(ln_g), const(ln_b), tok(dzl)],
                           [("tok", nx_rows, D, F32), ("acc", 1, D), ("acc", 1, D)])
    dz = _dwconv("dwconv_bwd", dzc, dw_w, jnp.zeros((1, D), F32), bl, t, True)
    ddw, ddb = _dwconv_wgrad(z, dzc, bl, t)
    grads.update(conv_ln_g=dln_g, conv_ln_b=dln_b, dw_weight=ddw[:CONV_W], dw_bias=ddb)

    def glu_bwd(i, dz_, a, b):
        keep = (i <= last_x).astype(F32)
        dz_ = dz_ * keep
        s = jax.nn.sigmoid(b)
        return (jnp.concatenate([dz_ * s, dz_ * a * s * (1 - s)], axis=1),)
    (dconv,) = rm("glu_bwd", glu_bwd, ntt, [tok(dz, **cl), tok(p, 1024, 0, last_x), tok(p, 1024, 1, last_x)],
                  [("tok", nt_rows, 2048, BF16)])

    dyg_in = _mm("gla_bproj", dyg, wts["w_gla_out"], trans_b=True, rows=nx_rows)
    grads["w_gla_out"] = _wgrad("gla_wgrad", yg_int, dyg, rows=nx_rows)

    def gla_out_bwd(i, of, ob, og, gn, dy):
        _, vjp = jax.vjp(gla_out, of, ob, og, gn)
        do_, _, dog_, dgn_ = vjp(dy)
        return do_, dog_, dgn_
    do, dog_x, dgn = rm("gla_out_bwd", gla_out_bwd, nxt,
                        [tok(o2[0]), tok(o2[1]), tok(p, 1024, 4), const(gn_g), tok(dyg_in)],
                        [("tok", nx_rows, D, F32), ("tok", nx_rows, D, BF16), ("acc", 1, D)])
    grads["gla_norm_g"] = dgn
    dog = jnp.concatenate([dog_x, jnp.zeros((nc_rows, D), BF16)], axis=0)

    dq2, dk2, dv2, dg2 = zip(*_gla_bwd(p, g2, mmats, ssave, do, bl, t, tc))

    def dec_bwd(i, lr, wf, wb, bf_, bb_, dgf, dgb_):
        zf = _dot(lr, wf, 1, 0) + bf_
        zb = _dot(lr, wb, 1, 0) + bb_
        dzf = dgf * (1 - jax.nn.sigmoid(zf)) * (1.0 / TAU)
        dzb = dgb_ * (1 - jax.nn.sigmoid(zb)) * (1.0 / TAU)
        dlr = _dot(dzf, wf, 1, 1) + _dot(dzb, wb, 1, 1)
        return (dlr, _dot(lr, dzf, 0, 0), _dot(lr, dzb, 0, 0), jnp.sum(dzf, axis=0, keepdims=True),
                jnp.sum(dzb, axis=0, keepdims=True))
    dlr, dwaf, dwab, dbaf, dbab = rm(
        "decay_bwd", dec_bwd, ntt,
        [tok(p, 128, LR_COL // 128), const(waf), const(wab), const(baf), const(bab), tok(dg2[0]), tok(dg2[1])],
        [("tok", nt_rows, 128, BF16), ("acc", 128, 512), ("acc", 128, 512), ("acc", 1, 512), ("acc", 1, 512)])
    grads.update(w_alpha_f=dwaf[:LOWRANK], w_alpha_b=dwab[LOWRANK:2 * LOWRANK], b_alpha_f=dbaf, b_alpha_b=dbab)

    dq, dk, dv = rm("gla_sum", lambda i, q0, q1, k0, k1, v0, v1: (q0 + q1, k0 + k1, v0 + v1), ntt,
                    [tok(dq2[0]), tok(dq2[1]), tok(dk2[0]), tok(dk2[1]), tok(dv2[0]), tok(dv2[1])],
                    [("tok", nt_rows, 512, BF16), ("tok", nt_rows, 512, BF16), ("tok", nt_rows, 1024, BF16)])

    dp = jnp.concatenate([dconv, dq, dk, dv, dog, dga, dgb, dlr], axis=1)
    du2 = _mm("in_bproj", dp, w_inp, trans_b=True)
    grads["w_in_p"] = _wgrad("in_wgrad", u2t, dp)

    def in_norm_bwd(i, x_, g_, sh_, sc_, du_, dx):
        keep = (i <= last_x).astype(F32)
        _, vjp = jax.vjp(_modnorm, x_, g_, sh_, sc_)
        dxn, dg, dsh, dsc = vjp(du_)
        return (dx * keep + dxn, dg, dsh, dsc)
    dx1, dg_mix, dsh3, dsc4 = rm("in_norm_bwd", in_norm_bwd, ntt,
                                 [tok(x1), const(wts["g_mix"]), mod(mods[3]), mod(mods[4]), tok(du2), tok(dx2, **cl)],
                                 [("tok", nt_rows, D, F32), ("acc", 1, D), ("modacc", D), ("modacc", D)])
    grads["g_mix"] = dg_mix

    dx0, gf1 = _ffn_bwd("ffn1", x0, sv1, dx1, None, ntt, wts["g_ffn1"], mods[0], mods[1], mods[2],
                        wts["w1_gu"], wts["w1_down"], tpb, nb)
    grads.update(g_ffn1=gf1["g"], w1_gu=gf1["w_gu"], w1_down=gf1["w_down"])
    grad_x = dx0[:nx_rows].reshape(bl, t, D)

    dmods = [gf1["sh"], gf1["sc"], gf1["gate"], dsh3, dsc4, dgate5, gf2["sh"], gf2["sc"], gf2["gate"]]
    dmod = jnp.concatenate(
        [jnp.concatenate([a.reshape(a.shape[0], D), jnp.zeros((8 - a.shape[0], D), F32)], axis=0) for a in dmods],
        axis=1)
    dsc_ = _mm("mod_bproj", dmod, wts["w_mod"], trans_b=True)

    def silu_bwd(cc_ref, d_ref, dm_ref, dcc_ref, scc_ref, db_ref):
        cc_ = cc_ref[...]
        s = jax.nn.sigmoid(cc_)
        dcc_ref[...] = d_ref[...] * (s * (1 + cc_ * (1 - s)))
        scc_ref[...] = (cc_ * s).astype(BF16)
        db_ref[...] = jnp.sum(dm_ref[...], axis=0, keepdims=True)
    dcc, scc, db_mod = pl.pallas_call(
        silu_bwd, out_shape=[jax.ShapeDtypeStruct((8, D), F32), jax.ShapeDtypeStruct((8, D), BF16),
                             jax.ShapeDtypeStruct((1, 9 * D), F32)], name="mod_silu_bwd")(cc, dsc_, dmod)
    grads["c_ctx"] = dcc[nb]
    grads["b_mod"] = db_mod
    grads["w_mod"] = _mm_tn("mod_wgrad", scc, dmod, col_shards=True)
    grads["g_final"] = dg_final.reshape(D)
    return loss, grad_x, grads


ANY = pl.BlockSpec(memory_space=pl.ANY)


def _place():
    x, y, c = lax.axis_index("x"), lax.axis_index("y"), lax.axis_index("c")
    chips = [(1 - x, y), (x, 1 - y), (1 - x, 1 - y)]
    return x, y, c, chips


def _remote(send_sems, recv_sems):
    def copy(k, src, dst, to):
        return pltpu.make_async_remote_copy(src_ref=src, dst_ref=dst, send_sem=send_sems.at[k],
                                            recv_sem=recv_sems.at[k], device_id=to, device_id_type=MESH)
    return copy


def _sems(n):
    return [pltpu.SemaphoreType.DMA((n,)), pltpu.SemaphoreType.DMA((n,))]


def _gather_weights(shards):
    n = len(shards)

    def body(*refs):
        ins, outs = refs[:n], refs[n:2 * n]
        copy = _remote(refs[2 * n], refs[2 * n + 1])
        x, y, c, chips = _place()
        me = 2 * x + y
        sibling = (x, y, 1 - c)
        started = []

        def rows(i, hc):
            hr = ins[i].shape[0] // 2
            return pl.ds(hc * hr, hr)

        for i in range(n):
            started.append(copy(7 * i + 6, ins[i], outs[i].at[me], sibling))
            started[-1].start()
            for j, (px, py) in enumerate(chips):
                started.append(copy(7 * i + j, ins[i].at[rows(i, c), :], outs[i].at[me, rows(i, c), :], (px, py, c)))
                started[-1].start()
        for i in range(n):
            for j, (px, py) in enumerate(chips):
                half = outs[i].at[2 * px + py, rows(i, c), :]
                copy(7 * i + j, half, half, (px, py, c)).wait_recv()
                started.append(copy(7 * i + 3 + j, half, half, sibling))
                started[-1].start()
        for i in range(n):
            copy(7 * i + 6, ins[i], outs[i].at[me], sibling).wait_recv()
            for j, (px, py) in enumerate(chips):
                other = outs[i].at[2 * px + py, rows(i, 1 - c), :]
                copy(7 * i + 3 + j, other, other, sibling).wait_recv()
        for cp in started:
            cp.wait_send()

    return pl.pallas_call(
        body, out_shape=[jax.ShapeDtypeStruct((4,) + s.shape, s.dtype) for s in shards], in_specs=[ANY] * n,
        out_specs=[ANY] * n, scratch_shapes=_sems(7 * n), name="gather_weights")(*shards)


def _swap_halves(gs):
    n = len(gs)

    def body(*refs):
        ins, outs = refs[:n], refs[n:2 * n]
        copy = _remote(refs[2 * n], refs[2 * n + 1])
        x, y, c, _ = _place()
        cps = []
        for i in range(n):
            hr = ins[i].shape[1] // 2
            cps.append(copy(i, ins[i].at[:, pl.ds((1 - c) * hr, hr), :], outs[i], (x, y, 1 - c)))
            cps[-1].start()
        for cp in cps:
            cp.wait()

    return pl.pallas_call(
        body, out_shape=[jax.ShapeDtypeStruct((4, g.shape[1] // 2, g.shape[2]), g.dtype) for g in gs],
        in_specs=[ANY] * n, out_specs=[ANY] * n, scratch_shapes=_sems(n), name="grad_swap_halves")(*gs)


def _row_tile(hr):
    return hr if hr <= 256 else _pick(hr, (256, 176, 128, 64, 32, 16))


def _add_halves(name, g, r, place):
    hr = r.shape[1]
    tr = _row_tile(hr)
    nblk = hr // tr

    def kern(p_ref, g_ref, r_ref, o_ref):
        o_ref[...] = (g_ref[...].astype(F32) + r_ref[...].astype(F32)).astype(o_ref.dtype)

    blk = (1, tr, g.shape[2])
    return pl.pallas_call(
        kern,
        grid_spec=pltpu.PrefetchScalarGridSpec(
            num_scalar_prefetch=1, grid=(4, nblk),
            in_specs=[pl.BlockSpec(blk, lambda j, i, p: (j, p[0] * nblk + i, 0)),
                      pl.BlockSpec(blk, lambda j, i, p: (j, i, 0))],
            out_specs=pl.BlockSpec(blk, lambda j, i, p: (j, i, 0))),
        out_shape=jax.ShapeDtypeStruct(r.shape, r.dtype),
        compiler_params=_cparams(("parallel", "parallel")), name=name)(place, g, r)


def _scatter_chips(cs):
    n = len(cs)

    def body(*refs):
        ins, outs = refs[:n], refs[n:2 * n]
        copy = _remote(refs[2 * n], refs[2 * n + 1])
        x, y, c, chips = _place()
        me = 2 * x + y
        sends = []
        for i in range(n):
            for j, (px, py) in enumerate(chips):
                sends.append(copy(3 * i + j, ins[i].at[2 * px + py], outs[i].at[me], (px, py, c)))
                sends[-1].start()
        for i in range(n):
            for j, (px, py) in enumerate(chips):
                src = 2 * px + py
                copy(3 * i + j, ins[i].at[src], outs[i].at[src], (px, py, c)).wait_recv()
        for cp in sends:
            cp.wait_send()

    return pl.pallas_call(
        body, out_shape=[jax.ShapeDtypeStruct(a.shape, a.dtype) for a in cs], in_specs=[ANY] * n,
        out_specs=[ANY] * n, scratch_shapes=_sems(3 * n), name="grad_scatter_chips")(*cs)


def _sum_chips(name, cs, r, place):
    hr = r.shape[1]
    tr = _row_tile(hr)
    nblk = hr // tr

    def kern(p_ref, c_ref, r0, r1, r2, r3, o_ref):
        me = p_ref[1]
        acc = None
        for k, rk in enumerate((r0, r1, r2, r3)):
            val = jnp.where(me == k, c_ref[0].astype(F32), rk[0].astype(F32))
            acc = val if acc is None else acc + val
        o_ref[...] = acc

    blk = (1, tr, r.shape[2])

    def slot(k):
        return lambda i, p: (jnp.where(p[1] == k, (k + 1) % 4, k), i, 0)

    return pl.pallas_call(
        kern,
        grid_spec=pltpu.PrefetchScalarGridSpec(
            num_scalar_prefetch=1, grid=(nblk,),
            in_specs=[pl.BlockSpec(blk, lambda i, p: (p[1], i, 0))] + [pl.BlockSpec(blk, slot(k)) for k in range(4)],
            out_specs=pl.BlockSpec((tr, r.shape[2]), lambda i, p: (p[0] * nblk + i, 0))),
        out_shape=jax.ShapeDtypeStruct((2 * hr, r.shape[2]), F32),
        compiler_params=_cparams(("parallel",)), name=name)(place, cs, r, r, r, r)


def _join_halves(fs):
    n = len(fs)

    def body(*refs):
        ins, outs = refs[:n], refs[n:2 * n]
        copy = _remote(refs[2 * n], refs[2 * n + 1])
        x, y, c, _ = _place()
        cps = []
        for i in range(n):
            hr = ins[i].shape[0] // 2
            cps.append(copy(i, ins[i].at[pl.ds(c * hr, hr), :], outs[i].at[pl.ds(c * hr, hr), :], (x, y, 1 - c)))
            cps[-1].start()
        for i in range(n):
            hr = ins[i].shape[0] // 2
            other = outs[i].at[pl.ds((1 - c) * hr, hr), :]
            copy(i, other, other, (x, y, 1 - c)).wait_recv()
        for cp in cps:
            cp.wait_send()

    return pl.pallas_call(
        body, out_shape=[jax.ShapeDtypeStruct(f.shape, f.dtype) for f in fs], in_specs=[ANY] * n,
        out_specs=[ANY] * n, input_output_aliases={i: i for i in range(n)}, scratch_shapes=_sems(n),
        name="grad_join_halves")(*fs)


def _allreduce_small(v):
    def body(x_ref, out_ref, gath, send_sems, recv_sems, local_sem):
        x, y, c, chips = _place()
        me, sibling = (x, y, c), (x, y, 1 - c)

        def slot(px, py, pc):
            return gath.at[4 * px + 2 * py + pc]

        def copy(k, block, to, src=None):
            return pltpu.make_async_remote_copy(
                src_ref=slot(*block) if src is None else src, dst_ref=slot(*block), send_sem=send_sems.at[k],
                recv_sem=recv_sems.at[k], device_id=to, device_id_type=MESH)

        mine = pltpu.make_async_copy(x_ref, slot(*me), local_sem)
        mine.start()
        first = [copy(0, me, sibling, src=x_ref)]
        first += [copy(1 + j, me, (*chip, c), src=x_ref) for j, chip in enumerate(chips)]
        for cp in first:
            cp.start()
        passed = [copy(4 + j, (*chip, c), sibling) for j, chip in enumerate(chips)]
        for j, chip in enumerate(chips):
            copy(1 + j, (*chip, c), me).wait_recv()
            passed[j].start()
        copy(0, sibling, me).wait_recv()
        for j, chip in enumerate(chips):
            copy(4 + j, (*chip, 1 - c), me).wait_recv()
        for cp in first + passed:
            cp.wait_send()
        mine.wait()
        acc = gath[0]
        for k in range(1, 8):
            acc = acc + gath[k]
        out_ref[...] = acc

    vm = pl.BlockSpec(memory_space=pltpu.VMEM)
    return pl.pallas_call(
        body, out_shape=jax.ShapeDtypeStruct(v.shape, F32), in_specs=[vm], out_specs=vm,
        scratch_shapes=[pltpu.VMEM((8,) + v.shape, F32), pltpu.SemaphoreType.DMA((7,)),
                        pltpu.SemaphoreType.DMA((7,)), pltpu.SemaphoreType.DMA(())],
        name="allreduce_small")(v)


def _adamw(name, w, g, m, v):
    r, cols = w.shape
    budget = 262144
    tr = r if r * cols <= budget else next(c for c in (256, 128, 64, 32, 16, 8) if r % c == 0 and c * cols <= budget)

    def kern(w_ref, g_ref, m_ref, v_ref, d_ref, nm_ref, nv_ref):
        gv = g_ref[...]
        nm = ADAM_B1 * m_ref[...] + (1.0 - ADAM_B1) * gv
        nv = ADAM_B2 * v_ref[...] + (1.0 - ADAM_B2) * jnp.square(gv)
        m_hat = nm / (1.0 - ADAM_B1 ** ADAM_STEP)
        v_hat = nv / (1.0 - ADAM_B2 ** ADAM_STEP)
        d_ref[...] = -ADAM_LR * (m_hat / (jnp.sqrt(v_hat) + ADAM_EPS) + ADAM_WD * w_ref[...])
        nm_ref[...] = nm
        nv_ref[...] = nv

    spec = pl.BlockSpec((tr, cols), lambda i: (i, 0))
    shp = jax.ShapeDtypeStruct((r, cols), F32)
    return pl.pallas_call(kern, grid=(r // tr,), in_specs=[spec] * 4, out_specs=[spec] * 3, out_shape=[shp] * 3,
                          compiler_params=_cparams(("parallel",)), name=name)(w, g, m, v)


SHARDED = (("w_mod", 1), ("w1_gu", 1), ("w1_down", 0), ("w_in", 1), ("dw_weight", 1), ("w_conv_out", 0),
           ("w_alpha_f", 1), ("w_alpha_b", 1), ("w_gla_out", 0), ("w_out", 0), ("w2_gu", 1), ("w2_down", 0))
REPLICATED = ("c_ctx", "b_mod", "g_ffn1", "g_mix", "dw_bias", "conv_ln_g", "conv_ln_b", "b_alpha_f", "b_alpha_b",
              "gla_norm_g", "g_ffn2", "g_final")
WEIGHTS = ("c_ctx", "w_mod", "b_mod", "g_ffn1", "w1_gu", "w1_down", "g_mix", "w_in", "dw_weight", "dw_bias",
           "conv_ln_g", "conv_ln_b", "w_conv_out", "w_alpha_f", "b_alpha_f", "w_alpha_b", "b_alpha_b", "gla_norm_g",
           "w_gla_out", "w_out", "g_ffn2", "w2_gu", "w2_down", "g_final")
MATRICES = ("w_mod", "w1_gu", "w1_down", "w_in", "w_conv_out", "w_gla_out", "w_out", "w2_gu", "w2_down")


def _pack_flat(parts, align):
    flat = jnp.concatenate([p.reshape(-1) for p in parts])
    pad = (-flat.shape[0]) % align
    return jnp.concatenate([flat, jnp.zeros((pad,), flat.dtype)]).reshape(-1, 1024)


def _unpack_flat(flat2d, shapes):
    flat = flat2d.reshape(-1)
    out, off = [], 0
    for s in shapes:
        n = math.prod(s)
        out.append(flat[off:off + n].reshape(s))
        off += n
    return out


def kernel(x, c, ctx, c_ctx, w_mod, b_mod, g_ffn1, w1_gu, w1_down, g_mix, w_in, dw_weight, dw_bias, conv_ln_g, conv_ln_b, w_conv_out, w_alpha_f, b_alpha_f, w_alpha_b, b_alpha_b, gla_norm_g, w_gla_out, w_out, g_ffn2, w2_gu, w2_down, g_final, loss_target, m_c_ctx, m_w_mod, m_b_mod, m_g_ffn1, m_w1_gu, m_w1_down, m_g_mix, m_w_in, m_dw_weight, m_dw_bias, m_conv_ln_g, m_conv_ln_b, m_w_conv_out, m_w_alpha_f, m_b_alpha_f, m_w_alpha_b, m_b_alpha_b, m_gla_norm_g, m_w_gla_out, m_w_out, m_g_ffn2, m_w2_gu, m_w2_down, m_g_final, v_c_ctx, v_w_mod, v_b_mod, v_g_ffn1, v_w1_gu, v_w1_down, v_g_mix, v_w_in, v_dw_weight, v_dw_bias, v_conv_ln_g, v_conv_ln_b, v_w_conv_out, v_w_alpha_f, v_b_alpha_f, v_w_alpha_b, v_b_alpha_b, v_gla_norm_g, v_w_gla_out, v_w_out, v_g_ffn2, v_w2_gu, v_w2_down, v_g_final):
    given = dict(locals())
    w = {n: given[n] for n in WEIGHTS}
    m = {n: given["m_" + n] for n in WEIGHTS}
    v = {n: given["v_" + n] for n in WEIGHTS}

    def small_pack(dw, af, ab):
        return jnp.concatenate([dw, jnp.zeros((1, dw.shape[1]), F32), jnp.concatenate([af, ab], axis=1)], axis=0)

    shards = [w[n][0].astype(BF16) for n in MATRICES]
    shards.append(small_pack(w["dw_weight"][0], w["w_alpha_f"][0], w["w_alpha_b"][0]))
    got = dict(zip(MATRICES + ("small",), _gather_weights(shards)))

    wts = {n: w[n] for n in REPLICATED}
    for n in ("w_mod", "w1_gu", "w2_gu"):
        wts[n] = got[n]
    for n in ("w1_down", "w2_down", "w_conv_out", "w_gla_out", "w_out"):
        wts[n] = got[n].reshape(-1, D)
    wts["w_in_p"] = _perm_in_cols(jnp.concatenate([got["w_in"][j] for j in range(4)], axis=1))
    sm = got["small"]
    wts["dw_weight"] = jnp.concatenate([sm[j, :CONV_W] for j in range(4)], axis=1)
    zpad = jnp.zeros((128, HEADS * DK), BF16)
    w_af = jnp.concatenate([sm[j, 32:32 + LOWRANK, :DK] for j in range(4)], axis=1)
    w_ab = jnp.concatenate([sm[j, 32:32 + LOWRANK, DK:] for j in range(4)], axis=1)
    wts["w_alpha_f_pad"] = zpad.at[0:LOWRANK].set(w_af.astype(BF16))
    wts["w_alpha_b_pad"] = zpad.at[LOWRANK:2 * LOWRANK].set(w_ab.astype(BF16))

    loss, grad_x, grads = _local_step(x, c, ctx, loss_target, wts)
    loss = lax.psum(loss, ("x", "y", "c"))

    def pieces(n):
        if n == "w_in":
            g = _unperm_in_cols(grads["w_in_p"])
            return jnp.transpose(g.reshape(D, 4, D_IN // 4), (1, 0, 2))
        g = grads[n]
        if n in ("w_mod", "w1_gu", "w2_gu"):
            return g
        return g.reshape(4, g.shape[0] // 4, g.shape[1])
    gs = [pieces(n) for n in MATRICES]
    gs.append(jnp.stack([small_pack(grads["dw_weight"][:, 256 * j:256 * (j + 1)],
                                    grads["w_alpha_f"][:, DK * j:DK * (j + 1)],
                                    grads["w_alpha_b"][:, DK * j:DK * (j + 1)]) for j in range(4)]))
    tags = MATRICES + ("small",)
    place = jnp.stack([lax.axis_index("c"), 2 * lax.axis_index("x") + lax.axis_index("y")]).astype(jnp.int32)
    swapped = _swap_halves(gs)
    chip_sums = [_add_halves("grad_add_" + t, g, r, place) for t, g, r in zip(tags, gs, swapped)]
    landed = _scatter_chips(chip_sums)
    halves = [_sum_chips("grad_sum_" + t, cs, r, place) for t, cs, r in zip(tags, chip_sums, landed)]
    reduced = dict(zip(tags, _join_halves(halves)))
    g_shard = {n: reduced[n] for n in MATRICES}
    g_shard["dw_weight"] = reduced["small"][:CONV_W]
    g_shard["w_alpha_f"] = reduced["small"][32:32 + LOWRANK, :DK]
    g_shard["w_alpha_b"] = reduced["small"][32:32 + LOWRANK, DK:]

    rep_shapes = [w[n].shape for n in REPLICATED]
    small = _allreduce_small(_pack_flat([grads[n].reshape(w[n].shape) for n in REPLICATED], 8 * 1024))
    g_rep = dict(zip(REPLICATED, _unpack_flat(small, rep_shapes)))

    g_out, d_out, m_out, v_out = {}, {}, {}, {}
    for n, _ in SHARDED:
        s2 = w[n].shape[1:]
        d, nm, nv = _adamw("adamw_" + n, w[n].reshape(s2), g_shard[n], m[n].reshape(s2), v[n].reshape(s2))
        g_out[n] = g_shard[n].reshape(w[n].shape)
        d_out[n], m_out[n], v_out[n] = d.reshape(w[n].shape), nm.reshape(w[n].shape), nv.reshape(w[n].shape)
    pk = lambda t: _pack_flat([t[n] for n in REPLICATED], 8 * 1024)
    d, nm, nv = _adamw("adamw_vectors", pk(w), small, pk(m), pk(v))
    for n, dd, mm, vv in zip(REPLICATED, _unpack_flat(d, rep_shapes), _unpack_flat(nm, rep_shapes),
                             _unpack_flat(nv, rep_shapes)):
        g_out[n], d_out[n], m_out[n], v_out[n] = g_rep[n], dd, mm, vv

    return (loss, grad_x, *[g_out[n] for n in WEIGHTS], *[d_out[n] for n in WEIGHTS],
            *[m_out[n] for n in WEIGHTS], *[v_out[n] for n in WEIGHTS])
```

```python
import functools
import math

import jax
import jax.numpy as jnp
from jax import lax
from jax.experimental import pallas as pl
from jax.experimental.pallas import tpu as pltpu

F32, BF16 = jnp.float32, jnp.bfloat16
MESH = pl.DeviceIdType.MESH
HIGHEST = lax.Precision.HIGHEST

D = 1024
FF = 2816
HEADS, DK, DV = 4, 128, 256
LOWRANK = 16
CONV_W = 31
CHUNK = 64
TAU = 16.0
EPS = 1e-6
Q_SCALE = DK ** -0.5
TM = 256
D_IN = 7200
D_INP = 7296
LR_COL = 7168
VMEM_LIMIT = 52 * 1024 * 1024
WGRAD_VMEM = 40 * 1024 * 1024

ADAM_LR, ADAM_B1, ADAM_B2, ADAM_EPS, ADAM_WD, ADAM_STEP = 0.001, 0.9, 0.999, 1e-08, 0.01, 10


def _silu(x):
    return x * jax.nn.sigmoid(x)


def _rms(h, g):
    return h * lax.rsqrt(jnp.mean(h * h, axis=-1, keepdims=True) + EPS) * g


def _modnorm(x, g, shift, scale):
    return _rms(x, g) * (1 + scale) + shift


def _cparams(sem=None):
    return pltpu.CompilerParams(dimension_semantics=sem, vmem_limit_bytes=VMEM_LIMIT)


def _twice(v):
    return v, v


def tok(arr, width=None, cb=0, clamp=None):
    return ("tok", arr, arr.shape[1] if width is None else width, cb, clamp)


def mod(arr):
    return ("mod", arr)


def const(arr):
    return ("const", arr)


def _rowmap(name, body, n_tiles, ins, outs, *, tpb, nb):
    def modrow(i):
        return jnp.minimum(i // tpb, nb)

    in_specs, args = [], []
    for spec in ins:
        if spec[0] == "tok":
            _, arr, width, cb, clamp = spec
            if clamp is None:
                im = lambda i, cb=cb: (i, cb)
            else:
                im = lambda i, cb=cb, clamp=clamp: (jnp.minimum(i, clamp), cb)
            in_specs.append(pl.BlockSpec((TM, width), im))
        elif spec[0] == "mod":
            arr = spec[1]
            in_specs.append(pl.BlockSpec((1, 1, arr.shape[2]), lambda i: (modrow(i), 0, 0)))
        else:
            arr = spec[1]
            in_specs.append(pl.BlockSpec(arr.shape, lambda i, nd=arr.ndim: (0,) * nd))
        args.append(arr)
    out_specs, out_shapes = [], []
    for o in outs:
        if o[0] == "tok":
            _, rows, width, dtype = o
            out_shapes.append(jax.ShapeDtypeStruct((rows, width), dtype))
            out_specs.append(pl.BlockSpec((TM, width), lambda i: (i, 0)))
        elif o[0] == "tokT":
            _, rows, width, dtype = o
            out_shapes.append(jax.ShapeDtypeStruct((width, rows), dtype))
            out_specs.append(pl.BlockSpec((width, TM), lambda i: (0, i)))
        elif o[0] == "acc":
            _, rows, width = o
            out_shapes.append(jax.ShapeDtypeStruct((rows, width), F32))
            out_specs.append(pl.BlockSpec((rows, width), lambda i: (0, 0)))
        else:
            width = o[1]
            rows_visited = min((n_tiles - 1) // tpb, nb) + 1
            out_shapes.append(jax.ShapeDtypeStruct((rows_visited, 1, width), F32))
            out_specs.append(pl.BlockSpec((1, 1, width), lambda i: (modrow(i), 0, 0)))
    n_in = len(ins)

    def kern(*refs):
        i = pl.program_id(0)
        vals = []
        for r, spec in zip(refs[:n_in], ins):
            val = r[0] if spec[0] == "mod" else r[...]
            vals.append(val.astype(F32) if spec[0] == "tok" and val.dtype == BF16 else val)
        res = body(i, *vals)
        for r, o, val in zip(refs[n_in:], outs, res):
            if o[0] == "tok":
                r[...] = val.astype(r.dtype)
            elif o[0] == "tokT":
                r[...] = val.T.astype(r.dtype)
            elif o[0] == "acc":
                @pl.when(i == 0)
                def _():
                    r[...] = jnp.zeros(r.shape, F32)
                r[...] += jnp.broadcast_to(val, r.shape)
            else:
                first = jnp.logical_or(i == 0, modrow(i) != modrow(jnp.maximum(i - 1, 0)))

                @pl.when(first)
                def _():
                    r[...] = jnp.zeros(r.shape, F32)
                r[0] += val

    return pl.pallas_call(
        kern, grid=(n_tiles,), in_specs=in_specs, out_specs=out_specs, out_shape=out_shapes,
        compiler_params=_cparams(("arbitrary",)), name=name)(*args)


def _pick(n, cands):
    for c in cands:
        if n % c == 0:
            return c
    return n


def _pallas(kern, *, grid, in_specs, out_specs, out_shape, scratch_shapes, sem, name, args, side=None):
    if side is None:
        return pl.pallas_call(kern, grid=grid, in_specs=in_specs, out_specs=out_specs, out_shape=out_shape,
                              scratch_shapes=scratch_shapes, compiler_params=_cparams(sem), name=name)(*args)
    single = not isinstance(out_shape, (list, tuple))
    shapes = [out_shape] if single else list(out_shape)
    ospecs = [out_specs] if single else list(out_specs)
    n_in, n_out, n_scr = len(in_specs), len(shapes), len(scratch_shapes)
    s_in, s_out = list(side["ins"]), list(side["outs"])

    def wrapped(*refs):
        pos = [0]

        def take(n):
            pos[0] += n
            return refs[pos[0] - n:pos[0]]
        ins, sins, outs, souts, scr, sems = take(n_in), take(len(s_in)), take(n_out), take(len(s_out)), take(n_scr), take(2)
        ids = [pl.program_id(k) for k in range(len(grid))]
        first = functools.reduce(jnp.logical_and, [i == 0 for i in ids])
        last = functools.reduce(jnp.logical_and, [i == g - 1 for i, g in zip(ids, grid)])
        copy = _remote(*sems)

        @pl.when(first)
        def _():
            side["start"](sins, souts, copy)
        kern(*ins, *outs, *scr)

        @pl.when(last)
        def _():
            side["finish"](sins, souts, copy)

    res = pl.pallas_call(
        wrapped, grid=grid, in_specs=list(in_specs) + [ANY] * len(s_in), out_specs=ospecs + [ANY] * len(s_out),
        out_shape=shapes + s_out, scratch_shapes=list(scratch_shapes) + _sems(side["nsem"]),
        input_output_aliases={n_in + a: n_out + b for a, b in side.get("alias", {}).items()},
        compiler_params=_cparams(("arbitrary",) * len(grid)), name=name)(*args, *s_in)
    main = res[:n_out]
    return (main[0] if single else main), list(res[n_out:])


def _mm(name, a, b, *, trans_b=False, out_dtype=F32, a_fn=None, bias=None, rows=None, side=None):
    m, k = a.shape if a.ndim == 2 else (a.shape[1], 2 * a.shape[2])
    m = m if rows is None else rows
    shard = b.shape[2] if b.ndim == 3 else None
    if trans_b:
        n = b.shape[-2]
        tk = _pick(shard, (2816, 2304, 1408, 1024)) if shard else (
            k if k <= 2816 else _pick(k, (2816, 2432, 2304, 2048, 1536, 1408, 1024, 512, 256, 128)))
        tn = _pick(n, (1024, 512, 384, 256, 128))
    else:
        n = 4 * shard if shard else b.shape[1]
        tk = k if k <= 2816 else _pick(k, (2816, 2432, 2304, 2048, 1536, 1408, 1024, 512, 256, 128))
        tn = _pick(shard, (512, 384, 1408, 256, 128)) if shard else _pick(n, (1024, 2432, 512, 384, 256, 128))
    tm = _pick(m, (1024, 512, 256))
    nk = k // tk
    per = shard // (tk if trans_b else tn) if shard else None
    dims = (((1,), (1,)), ((), ())) if trans_b else (((1,), (0,)), ((), ()))

    def kern(*refs):
        a_ref, b_ref = refs[0], refs[1]
        bias_ref = refs[2] if bias is not None else None
        o_ref, acc_ref = refs[-2], refs[-1]
        kk = pl.program_id(2)
        av = a_ref[...]
        if a_fn is not None:
            av = a_fn(av)
        p = lax.dot_general(av.astype(BF16), b_ref[...].astype(BF16), dims, preferred_element_type=F32)

        def finish(total):
            if bias_ref is not None:
                total = total + bias_ref[...]
            o_ref[...] = total.astype(o_ref.dtype)

        if nk == 1:
            finish(p)
        else:
            @pl.when(kk == 0)
            def _():
                acc_ref[...] = p

            @pl.when(kk > 0)
            def _():
                acc_ref[...] += p

            @pl.when(kk == nk - 1)
            def _():
                finish(acc_ref[...])

    if shard and trans_b:
        b_spec = pl.BlockSpec((None, tn, tk), lambda i, j, kk: (kk // per, j, kk % per))
    elif shard:
        b_spec = pl.BlockSpec((None, tk, tn), lambda i, j, kk: (j // per, kk, j % per))
    elif trans_b:
        b_spec = pl.BlockSpec((tn, tk), lambda i, j, kk: (j, kk))
    else:
        b_spec = pl.BlockSpec((tk, tn), lambda i, j, kk: (kk, j))
    if a.ndim == 3:
        pa = a.shape[2] // tk
        a_spec = pl.BlockSpec((None, tm, tk), lambda i, j, kk: (kk // pa, i, kk % pa))
    else:
        a_spec = pl.BlockSpec((tm, tk), lambda i, j, kk: (i, kk))
    in_specs = [a_spec, b_spec]
    args = [a, b]
    if bias is not None:
        in_specs.append(pl.BlockSpec((1, tn), lambda i, j, kk: (0, j)))
        args.append(bias)
    return _pallas(
        kern, grid=(m // tm, n // tn, nk), in_specs=in_specs,
        out_specs=pl.BlockSpec((tm, tn), lambda i, j, kk: (i, j)),
        out_shape=jax.ShapeDtypeStruct((m, n), out_dtype),
        scratch_shapes=[pltpu.VMEM((tm, tn) if nk > 1 else (8, 128), F32)],
        sem=("parallel", "parallel", "arbitrary"), name=name, args=args, side=side)


def _mm_tn(name, x, dy, rows=None, col_shards=False):
    t = x.shape[0] if rows is None else rows
    k1, n1 = x.shape[1], (dy.shape[1] if dy.ndim == 2 else 2 * dy.shape[2])
    tt = _pick(t, (512, 256, 128, 64, 8))
    tk1 = _pick(k1, (1024, 1408, 512, 256, 128))
    tn = _pick(n1 // 4, (512, 384, 1408, 256, 128)) if col_shards else _pick(n1, (512, 384, 256, 128))
    per = n1 // 4 // tn
    ns = t // tt

    def kern(x_ref, dy_ref, o_ref, acc_ref):
        s = pl.program_id(2)
        p = lax.dot_general(x_ref[...].astype(BF16), dy_ref[...].astype(BF16), (((0,), (0,)), ((), ())),
                            preferred_element_type=F32)

        @pl.when(s == 0)
        def _():
            acc_ref[...] = p

        @pl.when(s > 0)
        def _():
            acc_ref[...] += p

        @pl.when(s == ns - 1)
        def _():
            o_ref[...] = acc_ref[...].astype(o_ref.dtype)

    if col_shards:
        out_spec = pl.BlockSpec((None, tk1, tn), lambda i, j, s: (j // per, i, j % per))
        out_shape = jax.ShapeDtypeStruct((4, k1, n1 // 4), BF16)
    else:
        out_spec = pl.BlockSpec((tk1, tn), lambda i, j, s: (i, j))
        out_shape = jax.ShapeDtypeStruct((k1, n1), BF16)
    if dy.ndim == 3:
        pd = dy.shape[2] // tn
        dy_spec = pl.BlockSpec((None, tt, tn), lambda i, j, s: (j // pd, s, j % pd))
    else:
        dy_spec = pl.BlockSpec((tt, tn), lambda i, j, s: (s, j))
    return pl.pallas_call(
        kern, grid=(k1 // tk1, n1 // tn, ns),
        in_specs=[pl.BlockSpec((tt, tk1), lambda i, j, s: (s, i)), dy_spec],
        out_specs=out_spec, out_shape=out_shape, scratch_shapes=[pltpu.VMEM((tk1, tn), F32)],
        compiler_params=_cparams(("parallel", "parallel", "arbitrary")), name=name)(x, dy)


def _wgrad(name, xt, dy, rows=None, col_shards=False, side=None):
    k1 = xt.shape[0]
    t = xt.shape[1] if rows is None else rows
    n1 = dy.shape[1] if dy.ndim == 2 else 2 * dy.shape[2]
    tm = _pick(k1, (1024, 1408, 512, 256))
    tn = _pick(n1 // 4, (1408, 512, 384, 256, 128)) if col_shards else _pick(n1, (1024, 2432, 512, 384, 256, 128))
    fixed = tm * tn * (4 + 4 + 2 * 2)
    tk = next((c for c in (2048, 1536, 1024, 512, 256, 128)
               if t % c == 0 and fixed + 4 * c * (tm + tn) <= WGRAD_VMEM), 128)
    ns = t // tk
    per = n1 // 4 // tn

    def kern(x_ref, dy_ref, o_ref, acc_ref):
        s = pl.program_id(2)
        p = jnp.dot(x_ref[...], dy_ref[...], preferred_element_type=F32)

        @pl.when(s == 0)
        def _():
            acc_ref[...] = p

        @pl.when(s > 0)
        def _():
            acc_ref[...] += p

        @pl.when(s == ns - 1)
        def _():
            o_ref[...] = acc_ref[...].astype(o_ref.dtype)

    if dy.ndim == 3:
        pd = dy.shape[2] // tn
        dy_spec = pl.BlockSpec((None, tk, tn), lambda i, j, s: (j // pd, s, j % pd))
    else:
        dy_spec = pl.BlockSpec((tk, tn), lambda i, j, s: (s, j))
    if col_shards:
        out_spec = pl.BlockSpec((None, tm, tn), lambda i, j, s: (j // per, i, j % per))
        out_shape = jax.ShapeDtypeStruct((4, k1, n1 // 4), BF16)
    else:
        out_spec = pl.BlockSpec((tm, tn), lambda i, j, s: (i, j))
        out_shape = jax.ShapeDtypeStruct((k1, n1), BF16)
    return _pallas(
        kern, grid=(k1 // tm, n1 // tn, ns),
        in_specs=[pl.BlockSpec((tm, tk), lambda i, j, s: (i, s)), dy_spec],
        out_specs=out_spec, out_shape=out_shape, scratch_shapes=[pltpu.VMEM((tm, tn), F32)],
        sem=("parallel", "parallel", "arbitrary"), name=name, args=[xt, dy], side=side)


def _swiglu_fwd(name, u, w_gu, side=None):
    m = u.shape[0]
    half = w_gu.shape[2]
    tm = _pick(m, (512, 256))

    def kern(u_ref, wa_ref, wb_ref, ab_ref, hm_ref, hmt_ref):
        uv = u_ref[...]
        a = jnp.dot(uv, wa_ref[...], preferred_element_type=F32)
        b = jnp.dot(uv, wb_ref[...], preferred_element_type=F32)
        ab_ref[0] = a.astype(BF16)
        ab_ref[1] = b.astype(BF16)
        hm = (_silu(a) * b).astype(BF16)
        hm_ref[...] = hm
        hmt_ref[...] = hm.T

    return _pallas(
        kern, grid=(m // tm, 2),
        in_specs=[pl.BlockSpec((tm, D), lambda i, j: (i, 0)),
                  pl.BlockSpec((None, D, half), lambda i, j: (j, 0, 0)),
                  pl.BlockSpec((None, D, half), lambda i, j: (2 + j, 0, 0))],
        out_specs=[pl.BlockSpec((2, tm, half), lambda i, j: (0, i, j)),
                   pl.BlockSpec((tm, half), lambda i, j: (i, j)),
                   pl.BlockSpec((half, tm), lambda i, j: (j, i))],
        out_shape=[jax.ShapeDtypeStruct((2, m, FF), BF16), jax.ShapeDtypeStruct((m, FF), BF16),
                   jax.ShapeDtypeStruct((FF, m), BF16)],
        scratch_shapes=[], sem=("parallel", "parallel"), name=name, args=[u, w_gu, w_gu], side=side)


def _swiglu_bwd(name, df, w_down, ab):
    m = df.shape[0]
    half = FF // 2
    tm = _pick(m, (512, 256))

    def kern(df_ref, w_ref, ab_ref, o_ref):
        dh = lax.dot_general(df_ref[...], w_ref[...], (((1,), (1,)), ((), ())), preferred_element_type=F32)
        a = ab_ref[0].astype(F32)
        b = ab_ref[1].astype(F32)
        s = jax.nn.sigmoid(a)
        o_ref[0] = (dh * b * (s * (1 + a * (1 - s)))).astype(BF16)
        o_ref[1] = (dh * (a * s)).astype(BF16)

    return pl.pallas_call(
        kern, grid=(m // tm, 2),
        in_specs=[pl.BlockSpec((tm, D), lambda i, j: (i, 0)),
                  pl.BlockSpec((half, D), lambda i, j: (j, 0)),
                  pl.BlockSpec((2, tm, half), lambda i, j: (0, i, j))],
        out_specs=pl.BlockSpec((2, tm, half), lambda i, j: (0, i, j)),
        out_shape=jax.ShapeDtypeStruct((2, m, FF), BF16),
        compiler_params=_cparams(("parallel", "parallel")), name=name)(df, w_down, ab)


def _gla_maps(bl, t, tc):
    nx, nc = t // CHUNK, tc // CHUNK
    nxb = bl * nx

    def rowblk(d, b, n):
        c_ctx = jnp.where(d == 0, n, nc - 1 - n)
        c_x = jnp.where(d == 0, n - nc, nx - 1 - (n - nc))
        return jnp.where(n < nc, nxb + b * nc + c_ctx, b * nx + c_x)

    def xblk(d, b, n):
        n2 = jnp.maximum(n, nc)
        return b * nx + jnp.where(d == 0, n2 - nc, nx - 1 - (n2 - nc))

    return nx, nc, rowblk, xblk


def _dot01(m, x, cm):
    x1 = x.astype(BF16)
    r1 = x - x1.astype(F32)
    x2 = r1.astype(BF16)
    x3 = (r1 - x2.astype(F32)).astype(BF16)
    w = x.shape[1]
    p = lax.dot_general(m.astype(BF16), jnp.concatenate([x1, x2, x3], axis=1), (((cm,), (0,)), ((), ())),
                        preferred_element_type=F32)
    return p[:, :w] + p[:, w:2 * w] + p[:, 2 * w:]


def _gla_chunk(m, q, k, v, g, h):
    gh = g[:, h * DK:(h + 1) * DK]
    b = _dot01(m, gh, 1)
    tot = jnp.sum(gh, axis=0, keepdims=True)
    mid = b[CHUNK // 2:CHUNK // 2 + 1, :]
    qh = q[:, h * DK:(h + 1) * DK] * Q_SCALE
    kh = k[:, h * DK:(h + 1) * DK]
    vh = v[:, h * DV:(h + 1) * DV]
    return b, tot, mid, qh, kh, vh


def _dot(a, b, ca, cb):
    return lax.dot_general(a.astype(BF16), b.astype(BF16), (((ca,), (cb,)), ((), ())),
                           preferred_element_type=F32)


def _gla_fwd(p, g2, mmats, bl, t, tc):
    nx, nc, rowblk, xblk = _gla_maps(bl, t, tc)
    ns = nx + nc

    def kern(q0, k0, v0, g0, q1, k1, v1, g1, m_ref, o0, o1, ss_ref, s_ref):
        n = pl.program_id(1)

        @pl.when(n == 0)
        def _():
            s_ref[...] = jnp.zeros(s_ref.shape, F32)
        sides = ((q0, k0, v0, g0, o0), (q1, k1, v1, g1, o1))
        loaded = [(m_ref[d], q_ref[...].astype(F32), k_ref[...].astype(F32), v_ref[...].astype(F32), g_ref[...])
                  for d, (q_ref, k_ref, v_ref, g_ref, _) in enumerate(sides)]
        chains = [(d, h) for d in range(2) for h in range(HEADS)]
        base = [_gla_chunk(*loaded[d], h) for d, h in chains]
        pre = []
        for (d, h), (b, tot, mid, qh, kh, vh) in zip(chains, base):
            s0 = s_ref[d, h * DV:(h + 1) * DV, :]
            ss_ref[d, 0, 0, h * DV:(h + 1) * DV, :] = s0
            pre.append((s0, kh * jnp.exp(tot - b), qh * jnp.exp(b), qh * jnp.exp(b - mid), kh * jnp.exp(mid - b)))
        raw = [(_dot(qm, km, 1, 1), _dot(qe, s0, 1, 1), _dot(bs[5], kl, 0, 0))
               for bs, (s0, kl, qe, qm, km) in zip(base, pre)]
        for (d, h), bs, (s0, kl, qe, qm, km), (att_raw, inter, s_add) in zip(chains, base, pre, raw):
            s_ref[d, h * DV:(h + 1) * DV, :] = s0 * jnp.exp(bs[1]) + s_add
            sides[d][4][:, h * DV:(h + 1) * DV] = inter + _dot(loaded[d][0] * att_raw, bs[5], 1, 0)

    def operands(d):
        return [pl.BlockSpec((CHUNK, 512), lambda b, n: (rowblk(d, b, n), 4)),
                pl.BlockSpec((CHUNK, 512), lambda b, n: (rowblk(d, b, n), 5)),
                pl.BlockSpec((CHUNK, 1024), lambda b, n: (rowblk(d, b, n), 3)),
                pl.BlockSpec((CHUNK, 512), lambda b, n: (rowblk(d, b, n), d))]

    o_shape = jax.ShapeDtypeStruct((bl * t, HEADS * DV), F32)
    return pl.pallas_call(
        kern, grid=(bl, ns),
        in_specs=operands(0) + operands(1) + [pl.BlockSpec((2, CHUNK, CHUNK), lambda b, n: (0, 0, 0))],
        out_specs=[pl.BlockSpec((CHUNK, 1024), lambda b, n: (xblk(0, b, n), 0)),
                   pl.BlockSpec((CHUNK, 1024), lambda b, n: (xblk(1, b, n), 0)),
                   pl.BlockSpec((2, 1, 1, HEADS * DV, DK), lambda b, n: (0, b, n, 0, 0))],
        out_shape=[o_shape, o_shape, jax.ShapeDtypeStruct((2, bl, ns, HEADS * DV, DK), F32)],
        scratch_shapes=[pltpu.VMEM((2, HEADS * DV, DK), F32)],
        compiler_params=_cparams(("parallel", "arbitrary")), name="gla_fwd")(p, p, p, g2, p, p, p, g2, mmats)


def _gla_bwd(p, g2, mmats, ssave, do, bl, t, tc):
    nx, nc, rowblk, xblk = _gla_maps(bl, t, tc)
    ns = nx + nc
    rev = lambda s: ns - 1 - s

    def kern(q0, k0, v0, g0, do0, q1, k1, v1, g1, do1, m_ref, ss_ref,
             dq0, dk0, dv0, dg0, dq1, dk1, dv1, dg1, ds_ref):
        step = pl.program_id(1)
        n = ns - 1 - step

        @pl.when(step == 0)
        def _():
            ds_ref[...] = jnp.zeros(ds_ref.shape, F32)
        live = (n >= nc).astype(F32)
        sides = ((q0, k0, v0, g0, do0, dq0, dk0, dv0, dg0), (q1, k1, v1, g1, do1, dq1, dk1, dv1, dg1))
        loaded = [(m_ref[d], s[0][...].astype(F32), s[1][...].astype(F32), s[2][...].astype(F32), s[3][...])
                  for d, s in enumerate(sides)]
        dovs = [s[4][...] * live for s in sides]
        chains = [(d, h) for d in range(2) for h in range(HEADS)]
        base = [_gla_chunk(*loaded[d], h) for d, h in chains]
        pre = []
        for (d, h), (b, tot, mid, qh, kh, vh) in zip(chains, base):
            eb, ebm, emb, etb = jnp.exp(b), jnp.exp(b - mid), jnp.exp(mid - b), jnp.exp(tot - b)
            pre.append(dict(
                eb=eb, ebm=ebm, emb=emb, etb=etb, etot=jnp.exp(tot), qe=qh * eb, qm=qh * ebm, km=kh * emb, kl=kh * etb,
                vh=vh, doh=dovs[d][:, h * DV:(h + 1) * DV], s0=ss_ref[d, 0, 0, h * DV:(h + 1) * DV, :],
                ds1=ds_ref[d, h * DV:(h + 1) * DV, :]))
        first = [dict(att=_dot(c["qm"], c["km"], 1, 1), datt=_dot(c["doh"], c["vh"], 1, 1),
                      dqe=_dot(c["doh"], c["s0"], 1, 0), ds_add=_dot(c["doh"], c["qe"], 0, 0),
                      dkl=_dot(c["vh"], c["ds1"], 1, 0), dv_s=_dot(c["kl"], c["ds1"], 1, 1)) for c in pre]
        second = []
        for (d, h), c, f in zip(chains, pre, first):
            m = loaded[d][0]
            ds_ref[d, h * DV:(h + 1) * DV, :] = c["ds1"] * c["etot"] + f["ds_add"]
            att, datt = m * f["att"], m * f["datt"]
            second.append(dict(dqm=_dot(datt, c["km"], 1, 0), dkm=_dot(datt, c["qm"], 0, 0),
                               dv_a=_dot(att, c["doh"], 0, 0)))
        for (d, h), c, f, s in zip(chains, pre, first, second):
            dq_ref, dk_ref, dv_ref, dg_ref = sides[d][5:]
            dtot = c["etot"] * jnp.sum(c["ds1"] * c["s0"], axis=0, keepdims=True) + jnp.sum(
                f["dkl"] * c["kl"], axis=0, keepdims=True)
            db = f["dqe"] * c["qe"] + s["dqm"] * c["qm"] - s["dkm"] * c["km"] - f["dkl"] * c["kl"]
            dq_ref[:, h * DK:(h + 1) * DK] = (f["dqe"] * c["eb"] + s["dqm"] * c["ebm"]) * Q_SCALE
            dk_ref[:, h * DK:(h + 1) * DK] = s["dkm"] * c["emb"] + f["dkl"] * c["etb"]
            dv_ref[:, h * DV:(h + 1) * DV] = s["dv_a"] + f["dv_s"]
            dg_ref[:, h * DK:(h + 1) * DK] = _dot01(loaded[d][0], db, 0) + dtot

    nt = p.shape[0]

    def operands(d):
        return [pl.BlockSpec((CHUNK, 512), lambda b, s: (rowblk(d, b, rev(s)), 4)),
                pl.BlockSpec((CHUNK, 512), lambda b, s: (rowblk(d, b, rev(s)), 5)),
                pl.BlockSpec((CHUNK, 1024), lambda b, s: (rowblk(d, b, rev(s)), 3)),
                pl.BlockSpec((CHUNK, 512), lambda b, s: (rowblk(d, b, rev(s)), d)),
                pl.BlockSpec((CHUNK, 1024), lambda b, s: (xblk(d, b, rev(s)), 0))]

    def results(d):
        row = lambda b, s: (rowblk(d, b, rev(s)), 0)
        return [pl.BlockSpec((CHUNK, 512), row), pl.BlockSpec((CHUNK, 512), row), pl.BlockSpec((CHUNK, 1024), row),
                pl.BlockSpec((CHUNK, 512), row)]

    shapes = [jax.ShapeDtypeStruct((nt, 512), F32), jax.ShapeDtypeStruct((nt, 512), F32),
              jax.ShapeDtypeStruct((nt, 1024), F32), jax.ShapeDtypeStruct((nt, 512), F32)]
    out = pl.pallas_call(
        kern, grid=(bl, ns),
        in_specs=operands(0) + operands(1) + [
            pl.BlockSpec((2, CHUNK, CHUNK), lambda b, s: (0, 0, 0)),
            pl.BlockSpec((2, 1, 1, HEADS * DV, DK), lambda b, s: (0, b, rev(s), 0, 0))],
        out_specs=results(0) + results(1), out_shape=shapes + shapes,
        scratch_shapes=[pltpu.VMEM((2, HEADS * DV, DK), F32)],
        compiler_params=_cparams(("parallel", "arbitrary")), name="gla_bwd")(
            p, p, p, g2, do, p, p, p, g2, do, mmats, ssave)
    return out[:4], out[4:]


CONV_CT = 256
CONV_PAD = 16
CONV_RC = 128
CONV_HALO = 24


def _conv_fill(zp, z_ref, t):
    zp[0:CONV_PAD, :] = jnp.zeros((CONV_PAD, CONV_CT), F32)
    zp[CONV_PAD + t:2 * CONV_PAD + t, :] = jnp.zeros((CONV_PAD, CONV_CT), F32)
    zp[CONV_PAD:CONV_PAD + t, :] = z_ref[...]


def _dwconv(name, z, w, bias, bl, t, flip):
    def kern(z_ref, w_ref, b_ref, o_ref, zp):
        _conv_fill(zp, z_ref, t)
        offs = [(CONV_W - j) if flip else (j + 1) for j in range(CONV_W)]
        for r in range(0, t, CONV_RC):
            acc = jnp.broadcast_to(b_ref[...], (CONV_RC, CONV_CT))
            for rot in range(8):
                win = zp[r + rot:r + rot + CONV_RC + CONV_HALO, :]
                for j in range(CONV_W):
                    if offs[j] % 8 == rot:
                        a = offs[j] - rot
                        acc = acc + w_ref[j:j + 1, :] * win[a:a + CONV_RC, :]
            o_ref[r:r + CONV_RC, :] = acc

    return pl.pallas_call(
        kern, grid=(bl, 1024 // CONV_CT),
        in_specs=[pl.BlockSpec((t, CONV_CT), lambda b, c: (b, c)),
                  pl.BlockSpec((32, CONV_CT), lambda b, c: (0, c)),
                  pl.BlockSpec((1, CONV_CT), lambda b, c: (0, c))],
        out_specs=pl.BlockSpec((t, CONV_CT), lambda b, c: (b, c)),
        out_shape=jax.ShapeDtypeStruct(z.shape, F32),
        scratch_shapes=[pltpu.VMEM((t + 2 * CONV_PAD, CONV_CT), F32)],
        compiler_params=_cparams(("parallel", "parallel")), name=name)(z, w, bias)


def _dwconv_wgrad(z, dzc, bl, t):
    def kern(z_ref, d_ref, dw_ref, db_ref, zp):
        b = pl.program_id(1)

        @pl.when(b == 0)
        def _():
            dw_ref[...] = jnp.zeros(dw_ref.shape, F32)
            db_ref[...] = jnp.zeros(db_ref.shape, F32)
        _conv_fill(zp, z_ref, t)
        for rot in range(8):
            taps = [j for j in range(CONV_W) if (j + 1) % 8 == rot]
            accs = [jnp.zeros((8, CONV_CT), F32) for _ in taps]
            for r in range(0, t, CONV_RC):
                d = d_ref[r:r + CONV_RC, :]
                win = zp[r + rot:r + rot + CONV_RC + CONV_HALO, :]
                for k, j in enumerate(taps):
                    a = j + 1 - rot
                    prod = d * win[a:a + CONV_RC, :]
                    accs[k] = accs[k] + jnp.sum(prod.reshape(CONV_RC // 8, 8, CONV_CT), axis=0)
            for k, j in enumerate(taps):
                dw_ref[j:j + 1, :] += jnp.sum(accs[k], axis=0, keepdims=True)
        db_ref[...] += jnp.sum(d_ref[...], axis=0, keepdims=True)

    return pl.pallas_call(
        kern, grid=(1024 // CONV_CT, bl),
        in_specs=[pl.BlockSpec((t, CONV_CT), lambda c, b: (b, c)),
                  pl.BlockSpec((t, CONV_CT), lambda c, b: (b, c))],
        out_specs=[pl.BlockSpec((32, CONV_CT), lambda c, b: (0, c)),
                   pl.BlockSpec((1, CONV_CT), lambda c, b: (0, c))],
        out_shape=[jax.ShapeDtypeStruct((32, 1024), F32), jax.ShapeDtypeStruct((1, 1024), F32)],
        scratch_shapes=[pltpu.VMEM((t + 2 * CONV_PAD, CONV_CT), F32)],
        compiler_params=_cparams(("parallel", "arbitrary")), name="dwconv_wgrad")(z, dzc)


def _ffn_fwd(tag, xin, n_tiles, g, sh, sc, gate, w_gu, w_down, tpb, nb, comm=None):
    rows = n_tiles * TM
    rm = functools.partial(_rowmap, tpb=tpb, nb=nb)
    u, ut = rm(tag + "_norm", lambda i, x, g_, sh_, sc_: _twice(_modnorm(x, g_, sh_, sc_)), n_tiles,
               [tok(xin), const(g), mod(sh), mod(sc)], [("tok", rows, D, BF16), ("tokT", rows, D, BF16)])
    ab, hm, hmt = _with_side(comm, tag + "_gu", None, lambda s: _swiglu_fwd(tag + "_gu", u, w_gu, side=s))
    f = _with_side(comm, tag + "_down", None, lambda s: _mm(tag + "_down", hm, w_down, side=s))
    (xout,) = rm(tag + "_res", lambda i, x, f_, gt: (x + 0.5 * gt * f_,), n_tiles,
                 [tok(xin), tok(f), mod(gate)], [("tok", rows, D, F32)])
    return xout, (ut, ab, hmt, f)


def _ffn_bwd(tag, xin, saved, dxout, dx_clamp, n_tiles, g, sh, sc, gate, w_gu, w_down, tpb, nb, comm=None, grads=None,
             names=None):
    ut, ab, hmt, f = saved
    rows = n_tiles * TM
    rm = functools.partial(_rowmap, tpb=tpb, nb=nb)

    def mask(i):
        return 1.0 if dx_clamp is None else (i <= dx_clamp).astype(F32)

    def b1(i, dx, f_, gt):
        dx = dx * mask(i)
        return (0.5 * gt * dx, jnp.sum(0.5 * f_ * dx, axis=0, keepdims=True))
    df, dgate = rm(tag + "_bres", b1, n_tiles, [tok(dxout, clamp=dx_clamp), tok(f), mod(gate)],
                   [("tok", rows, D, BF16), ("modacc", D)])
    grads[names[1]] = _wgrad(tag + "_wdown", hmt, df)
    dab = _swiglu_bwd(tag + "_bdown", df, w_down, ab)
    grads[names[0]] = _with_side(comm, tag + "_wgu", grads,
                                 lambda s: _wgrad(tag + "_wgu", ut, dab, col_shards=True, side=s))
    du = _with_side(comm, tag + "_bgu", grads, lambda s: _mm(tag + "_bgu", dab, w_gu, trans_b=True, side=s))

    def b3(i, x, g_, sh_, sc_, du_, dx):
        _, vjp = jax.vjp(_modnorm, x, g_, sh_, sc_)
        dxn, dg, dsh, dsc = vjp(du_)
        return (dx * mask(i) + dxn, dg, dsh, dsc)
    dxin, dg, dsh, dsc = rm(tag + "_bnorm", b3, n_tiles,
                            [tok(xin), const(g), mod(sh), mod(sc), tok(du), tok(dxout, clamp=dx_clamp)],
                            [("tok", rows, D, F32), ("acc", 1, D), ("modacc", D), ("modacc", D)])
    return dxin, dict(g=dg, sh=dsh, sc=dsc, gate=dgate)


def _perm_in_cols(w):
    pad = jnp.zeros(w.shape[:-1] + (D_INP - D_IN,), w.dtype)
    return jnp.concatenate([w[..., :5120], w[..., 5152:7200], w[..., 5120:5152], pad], axis=-1)


def _unperm_in_cols(w):
    return jnp.concatenate([w[..., :5120], w[..., LR_COL:LR_COL + 32], w[..., 5120:LR_COL]], axis=-1)


def _local_step(x, c, ctx, target, wts, comm=None):
    bl, t, _ = x.shape
    tc = ctx.shape[1]
    nx_rows, nc_rows = bl * t, bl * tc
    nt_rows = nx_rows + nc_rows
    tpb = t // TM
    nxt, ntt = nx_rows // TM, nt_rows // TM
    nb = bl
    rm = functools.partial(_rowmap, tpb=tpb, nb=nb)
    last_x = nxt - 1

    x0 = jnp.concatenate([x.reshape(nx_rows, D), ctx.reshape(nc_rows, D)], axis=0)
    tgt = target.reshape(nx_rows, D)

    cc = jnp.concatenate([c, wts["c_ctx"].reshape(1, D), jnp.zeros((8 - bl - 1, D), F32)], axis=0)
    modv = _mm("mod_fwd", cc, wts["w_mod"], a_fn=_silu, bias=wts["b_mod"])
    mods = [modv[:nb + 1, k * D:(k + 1) * D].reshape(nb + 1, 1, D) for k in range(9)]

    x1, sv1 = _ffn_fwd("ffn1", x0, ntt, wts["g_ffn1"], mods[0], mods[1], mods[2], wts["w1_gu"], wts["w1_down"],
                       tpb, nb, comm)
    u2, u2t = rm("in_norm", lambda i, x_, g_, sh_, sc_: _twice(_modnorm(x_, g_, sh_, sc_)), ntt,
                 [tok(x1), const(wts["g_mix"]), mod(mods[3]), mod(mods[4])],
                 [("tok", nt_rows, D, BF16), ("tokT", nt_rows, D, BF16)])
    w_inp = wts["w_in_p"]
    p = _mm("in_proj", u2, w_inp, out_dtype=BF16)

    waf, wab, baf, bab = wts["w_alpha_f_pad"], wts["w_alpha_b_pad"], wts["b_alpha_f"], wts["b_alpha_b"]

    def dec_fwd(i, lr, wf, wb, bf_, bb_):
        zf = _dot(lr, wf, 1, 0) + bf_
        zb = _dot(lr, wb, 1, 0) + bb_
        return (jnp.concatenate([jax.nn.log_sigmoid(zf) / TAU, jax.nn.log_sigmoid(zb) / TAU], axis=1),)
    (gfb,) = rm("decay_fwd", dec_fwd, ntt, [tok(p, 128, LR_COL // 128), const(waf), const(wab), const(baf), const(bab)],
                [("tok", nt_rows, 1024, F32)])
    g2 = gfb
    tri = jnp.tril(jnp.ones((CHUNK, CHUNK), F32))
    mmats = jnp.stack([tri, tri.T])
    *o2, ssave = _gla_fwd(p, g2, mmats, bl, t, tc)

    gn_g = wts["gla_norm_g"]

    def gla_out(of, ob, og, gn):
        o = of + ob
        parts = []
        for h in range(HEADS):
            oh = o[:, h * DV:(h + 1) * DV]
            parts.append(oh * lax.rsqrt(jnp.mean(oh * oh, axis=-1, keepdims=True) + EPS))
        return jnp.concatenate(parts, axis=1) * gn * _silu(og)
    yg_in, yg_int = rm("gla_out", lambda i, of, ob, og, gn: _twice(gla_out(of, ob, og, gn)), nxt,
                       [tok(o2[0]), tok(o2[1]), tok(p, 1024, 4), const(gn_g)],
                       [("tok", nx_rows, D, BF16), ("tokT", nx_rows, D, BF16)])
    y_gla = _mm("gla_proj", yg_in, wts["w_gla_out"])

    (z,) = rm("glu", lambda i, a, b: (a * jax.nn.sigmoid(b),), nxt, [tok(p, 1024, 0), tok(p, 1024, 1)],
              [("tok", nx_rows, D, F32)])
    dw_w = jnp.concatenate([wts["dw_weight"], jnp.zeros((1, D), F32)], axis=0)
    zc = _dwconv("dwconv_fwd", z, dw_w, wts["dw_bias"], bl, t, False)

    def ln_silu(zc_, g_, b_):
        mu = jnp.mean(zc_, axis=-1, keepdims=True)
        var = jnp.mean(jnp.square(zc_ - mu), axis=-1, keepdims=True)
        return _silu((zc_ - mu) * lax.rsqrt(var + EPS) * g_ + b_)
    ln_g, ln_b = wts["conv_ln_g"], wts["conv_ln_b"]
    zl, zlt = rm("conv_ln", lambda i, zc_, g_, b_: _twice(ln_silu(zc_, g_, b_)), nxt,
                 [tok(zc), const(ln_g), const(ln_b)], [("tok", nx_rows, D, BF16), ("tokT", nx_rows, D, BF16)])
    y_conv = _mm("conv_proj", zl, wts["w_conv_out"])

    mg, mgt = rm("merge", lambda i, ga, gb, yc, yg: _twice(jax.nn.sigmoid(ga) * yc + jax.nn.sigmoid(gb) * yg), nxt,
                 [tok(p, 1024, 5), tok(p, 1024, 6), tok(y_conv), tok(y_gla)],
                 [("tok", nx_rows, D, BF16), ("tokT", nx_rows, D, BF16)])
    mix = _mm("out_proj", mg, wts["w_out"])
    (x2,) = rm("mix_res", lambda i, x_, mx_, gt: (x_ + gt * mx_,), nxt, [tok(x1), tok(mix), mod(mods[5])],
               [("tok", nx_rows, D, F32)])

    x3, sv2 = _ffn_fwd("ffn2", x2, nxt, wts["g_ffn2"], mods[6], mods[7], mods[8], wts["w2_gu"], wts["w2_down"],
                       tpb, nb)
    g_fin = wts["g_final"].reshape(1, D)

    def head(i, x_, g_, tg):
        y, vjp = jax.vjp(_rms, x_, g_)
        diff = y - tg
        dx, dg = vjp(diff * (1.0 / D))
        loss = 0.5 * jnp.sum(jnp.mean(diff * diff, axis=-1, keepdims=True))
        return dx, dg, loss
    dx3, dg_final, loss_acc = rm("loss_head", head, nxt, [tok(x3), const(g_fin), tok(tgt)],
                                 [("tok", nx_rows, D, F32), ("acc", 1, D), ("acc", 8, 128)])
    loss = loss_acc[0, 0]

    grads = {}
    dx2, gf2 = _ffn_bwd("ffn2", x2, sv2, dx3, None, nxt, wts["g_ffn2"], mods[6], mods[7], mods[8],
                        wts["w2_gu"], wts["w2_down"], tpb, nb, comm, grads, ("w2_gu", "w2_down"))
    grads["g_ffn2"] = gf2["g"]

    dmix, dgate5 = rm("mix_bres", lambda i, dx, mx_, gt: (gt * dx, jnp.sum(mx_ * dx, axis=0, keepdims=True)), nxt,
                      [tok(dx2), tok(mix), mod(mods[5])], [("tok", nx_rows, D, BF16), ("modacc", D)])
    dmg = _mm("out_bproj", dmix, wts["w_out"], trans_b=True)
    grads["w_out"] = _wgrad("out_wgrad", mgt, dmix)

    def merge_bwd(i, dm, ga, gb, yc, yg):
        keep = (i <= last_x).astype(F32)
        dm = dm * keep
        sa, sb = jax.nn.sigmoid(ga), jax.nn.sigmoid(gb)
        return dm * sa, dm * sb, dm * yc * sa * (1 - sa), dm * yg * sb * (1 - sb)
    cl = dict(clamp=last_x)
    dyc, dyg, dga, dgb = rm("merge_bwd", merge_bwd, ntt,
                            [tok(dmg, **cl), tok(p, 1024, 5, last_x), tok(p, 1024, 6, last_x), tok(y_conv, **cl),
                             tok(y_gla, **cl)],
                            [("tok", nt_rows, D, BF16)] * 4)

    dzl = _mm("conv_bproj", dyc, wts["w_conv_out"], trans_b=True, rows=nx_rows)
    grads["w_conv_out"] = _wgrad("conv_wgrad", zlt, dyc, rows=nx_rows)

    def ln_bwd(i, zc_, g_, b_, dz_):
        _, vjp = jax.vjp(ln_silu, zc_, g_, b_)
        return vjp(dz_)
    dzc, dln_g, dln_b = rm("conv_ln_bwd", ln_bwd, nxt, [tok(zc), const(ln_g), const(ln_b), tok(dzl)],
                           [("tok", nx_rows, D, F32), ("acc", 1, D), ("acc", 1, D)])
    dz = _dwconv("dwconv_bwd", dzc, dw_w, jnp.zeros((1, D), F32), bl, t, True)
    ddw, ddb = _dwconv_wgrad(z, dzc, bl, t)
    grads.update(conv_ln_g=dln_g, conv_ln_b=dln_b, dw_weight=ddw[:CONV_W], dw_bias=ddb)

    def glu_bwd(i, dz_, a, b):
        keep = (i <= last_x).astype(F32)
        dz_ = dz_ * keep
        s = jax.nn.sigmoid(b)
        return (jnp.concatenate([dz_ * s, dz_ * a * s * (1 - s)], axis=1),)
    (dconv,) = rm("glu_bwd", glu_bwd, ntt, [tok(dz, **cl), tok(p, 1024, 0, last_x), tok(p, 1024, 1, last_x)],
                  [("tok", nt_rows, 2048, BF16)])

    dyg_in = _mm("gla_bproj", dyg, wts["w_gla_out"], trans_b=True, rows=nx_rows)
    grads["w_gla_out"] = _wgrad("gla_wgrad", yg_int, dyg, rows=nx_rows)

    def gla_out_bwd(i, of, ob, og, gn, dy):
        _, vjp = jax.vjp(gla_out, of, ob, og, gn)
        do_, _, dog_, dgn_ = vjp(dy)
        return do_, dog_, dgn_
    do, dog_x, dgn = rm("gla_out_bwd", gla_out_bwd, nxt,
                        [tok(o2[0]), tok(o2[1]), tok(p, 1024, 4), const(gn_g), tok(dyg_in)],
                        [("tok", nx_rows, D, F32), ("tok", nx_rows, D, BF16), ("acc", 1, D)])
    grads["gla_norm_g"] = dgn
    dog = jnp.concatenate([dog_x, jnp.zeros((nc_rows, D), BF16)], axis=0)

    dq2, dk2, dv2, dg2 = zip(*_gla_bwd(p, g2, mmats, ssave, do, bl, t, tc))

    def dec_bwd(i, lr, wf, wb, bf_, bb_, dgf, dgb_):
        zf = _dot(lr, wf, 1, 0) + bf_
        zb = _dot(lr, wb, 1, 0) + bb_
        dzf = dgf * (1 - jax.nn.sigmoid(zf)) * (1.0 / TAU)
        dzb = dgb_ * (1 - jax.nn.sigmoid(zb)) * (1.0 / TAU)
        dlr = _dot(dzf, wf, 1, 1) + _dot(dzb, wb, 1, 1)
        return (dlr, _dot(lr, dzf, 0, 0), _dot(lr, dzb, 0, 0), jnp.sum(dzf, axis=0, keepdims=True),
                jnp.sum(dzb, axis=0, keepdims=True))
    dlr, dwaf, dwab, dbaf, dbab = rm(
        "decay_bwd", dec_bwd, ntt,
        [tok(p, 128, LR_COL // 128), const(waf), const(wab), const(baf), const(bab), tok(dg2[0]), tok(dg2[1])],
        [("tok", nt_rows, 128, BF16), ("acc", 128, 512), ("acc", 128, 512), ("acc", 1, 512), ("acc", 1, 512)])
    grads.update(w_alpha_f=dwaf[:LOWRANK], w_alpha_b=dwab[LOWRANK:2 * LOWRANK], b_alpha_f=dbaf, b_alpha_b=dbab)

    dq, dk, dv = rm("gla_sum", lambda i, q0, q1, k0, k1, v0, v1: (q0 + q1, k0 + k1, v0 + v1), ntt,
                    [tok(dq2[0]), tok(dq2[1]), tok(dk2[0]), tok(dk2[1]), tok(dv2[0]), tok(dv2[1])],
                    [("tok", nt_rows, 512, BF16), ("tok", nt_rows, 512, BF16), ("tok", nt_rows, 1024, BF16)])

    dp = jnp.concatenate([dconv, dq, dk, dv, dog, dga, dgb, dlr], axis=1)
    du2 = _with_side(comm, "in_bproj", grads, lambda s: _mm("in_bproj", dp, w_inp, trans_b=True, side=s))
    grads["w_in_p"] = _wgrad("in_wgrad", u2t, dp)

    def in_norm_bwd(i, x_, g_, sh_, sc_, du_, dx):
        keep = (i <= last_x).astype(F32)
        _, vjp = jax.vjp(_modnorm, x_, g_, sh_, sc_)
        dxn, dg, dsh, dsc = vjp(du_)
        return (dx * keep + dxn, dg, dsh, dsc)
    dx1, dg_mix, dsh3, dsc4 = rm("in_norm_bwd", in_norm_bwd, ntt,
                                 [tok(x1), const(wts["g_mix"]), mod(mods[3]), mod(mods[4]), tok(du2), tok(dx2, **cl)],
                                 [("tok", nt_rows, D, F32), ("acc", 1, D), ("modacc", D), ("modacc", D)])
    grads["g_mix"] = dg_mix

    dx0, gf1 = _ffn_bwd("ffn1", x0, sv1, dx1, None, ntt, wts["g_ffn1"], mods[0], mods[1], mods[2],
                        wts["w1_gu"], wts["w1_down"], tpb, nb, comm, grads, ("w1_gu", "w1_down"))
    grads["g_ffn1"] = gf1["g"]
    grad_x = dx0[:nx_rows].reshape(bl, t, D)

    dmods = [gf1["sh"], gf1["sc"], gf1["gate"], dsh3, dsc4, dgate5, gf2["sh"], gf2["sc"], gf2["gate"]]
    dmod = jnp.concatenate(
        [jnp.concatenate([a.reshape(a.shape[0], D), jnp.zeros((8 - a.shape[0], D), F32)], axis=0) for a in dmods],
        axis=1)
    dsc_ = _mm("mod_bproj", dmod, wts["w_mod"], trans_b=True)

    def silu_bwd(cc_ref, d_ref, dm_ref, dcc_ref, scc_ref, db_ref):
        cc_ = cc_ref[...]
        s = jax.nn.sigmoid(cc_)
        dcc_ref[...] = d_ref[...] * (s * (1 + cc_ * (1 - s)))
        scc_ref[...] = (cc_ * s).astype(BF16)
        db_ref[...] = jnp.sum(dm_ref[...], axis=0, keepdims=True)
    dcc, scc, db_mod = pl.pallas_call(
        silu_bwd, out_shape=[jax.ShapeDtypeStruct((8, D), F32), jax.ShapeDtypeStruct((8, D), BF16),
                             jax.ShapeDtypeStruct((1, 9 * D), F32)], name="mod_silu_bwd")(cc, dsc_, dmod)
    grads["c_ctx"] = dcc[nb]
    grads["b_mod"] = db_mod
    grads["w_mod"] = _mm_tn("mod_wgrad", scc, dmod, col_shards=True)
    grads["g_final"] = dg_final.reshape(D)
    return loss, grad_x, grads


ANY = pl.BlockSpec(memory_space=pl.ANY)


def _place():
    x, y, c = lax.axis_index("x"), lax.axis_index("y"), lax.axis_index("c")
    chips = [(1 - x, y), (x, 1 - y), (1 - x, 1 - y)]
    return x, y, c, chips


def _remote(send_sems, recv_sems):
    def copy(k, src, dst, to):
        return pltpu.make_async_remote_copy(src_ref=src, dst_ref=dst, send_sem=send_sems.at[k],
                                            recv_sem=recv_sems.at[k], device_id=to, device_id_type=MESH)
    return copy


def _sems(n):
    return [pltpu.SemaphoreType.DMA((n,)), pltpu.SemaphoreType.DMA((n,))]


def _gather_weights(shards):
    n = len(shards)

    def body(*refs):
        ins, outs = refs[:n], refs[n:2 * n]
        copy = _remote(refs[2 * n], refs[2 * n + 1])
        x, y, c, chips = _place()
        me = 2 * x + y
        sibling = (x, y, 1 - c)
        started = []

        def rows(i, hc):
            hr = ins[i].shape[0] // 2
            return pl.ds(hc * hr, hr)

        for i in range(n):
            started.append(copy(7 * i + 6, ins[i], outs[i].at[me], sibling))
            started[-1].start()
            for j, (px, py) in enumerate(chips):
                started.append(copy(7 * i + j, ins[i].at[rows(i, c), :], outs[i].at[me, rows(i, c), :], (px, py, c)))
                started[-1].start()
        for i in range(n):
            for j, (px, py) in enumerate(chips):
                half = outs[i].at[2 * px + py, rows(i, c), :]
                copy(7 * i + j, half, half, (px, py, c)).wait_recv()
                started.append(copy(7 * i + 3 + j, half, half, sibling))
                started[-1].start()
        for i in range(n):
            copy(7 * i + 6, ins[i], outs[i].at[me], sibling).wait_recv()
            for j, (px, py) in enumerate(chips):
                other = outs[i].at[2 * px + py, rows(i, 1 - c), :]
                copy(7 * i + 3 + j, other, other, sibling).wait_recv()
        for cp in started:
            cp.wait_send()

    return pl.pallas_call(
        body, out_shape=[jax.ShapeDtypeStruct((4,) + s.shape, s.dtype) for s in shards], in_specs=[ANY] * n,
        out_specs=[ANY] * n, scratch_shapes=_sems(7 * n), name="gather_weights")(*shards)


def _swap_halves(name, gs):
    n = len(gs)

    def body(*refs):
        ins, outs = refs[:n], refs[n:2 * n]
        copy = _remote(refs[2 * n], refs[2 * n + 1])
        x, y, c, _ = _place()
        cps = []
        for i in range(n):
            hr = ins[i].shape[1] // 2
            cps.append(copy(i, ins[i].at[:, pl.ds((1 - c) * hr, hr), :], outs[i], (x, y, 1 - c)))
            cps[-1].start()
        for cp in cps:
            cp.wait()

    return pl.pallas_call(
        body, out_shape=[jax.ShapeDtypeStruct((4, g.shape[1] // 2, g.shape[2]), g.dtype) for g in gs],
        in_specs=[ANY] * n, out_specs=[ANY] * n, scratch_shapes=_sems(n), name=name)(*gs)


def _row_tile(hr):
    return hr if hr <= 256 else _pick(hr, (256, 176, 128, 64, 32, 16))


def _add_halves(name, g, r, place):
    hr = r.shape[1]
    tr = _row_tile(hr)
    nblk = hr // tr

    def kern(p_ref, g_ref, r_ref, o_ref):
        o_ref[...] = (g_ref[...].astype(F32) + r_ref[...].astype(F32)).astype(o_ref.dtype)

    blk = (1, tr, g.shape[2])
    return pl.pallas_call(
        kern,
        grid_spec=pltpu.PrefetchScalarGridSpec(
            num_scalar_prefetch=1, grid=(4, nblk),
            in_specs=[pl.BlockSpec(blk, lambda j, i, p: (j, p[0] * nblk + i, 0)),
                      pl.BlockSpec(blk, lambda j, i, p: (j, i, 0))],
            out_specs=pl.BlockSpec(blk, lambda j, i, p: (j, i, 0))),
        out_shape=jax.ShapeDtypeStruct(r.shape, r.dtype),
        compiler_params=_cparams(("parallel", "parallel")), name=name)(place, g, r)


def _scatter_chips(cs):
    n = len(cs)

    def body(*refs):
        ins, outs = refs[:n], refs[n:2 * n]
        copy = _remote(refs[2 * n], refs[2 * n + 1])
        x, y, c, chips = _place()
        me = 2 * x + y
        sends = []
        for i in range(n):
            for j, (px, py) in enumerate(chips):
                sends.append(copy(3 * i + j, ins[i].at[2 * px + py], outs[i].at[me], (px, py, c)))
                sends[-1].start()
        for i in range(n):
            for j, (px, py) in enumerate(chips):
                src = 2 * px + py
                copy(3 * i + j, ins[i].at[src], outs[i].at[src], (px, py, c)).wait_recv()
        for cp in sends:
            cp.wait_send()

    return pl.pallas_call(
        body, out_shape=[jax.ShapeDtypeStruct(a.shape, a.dtype) for a in cs], in_specs=[ANY] * n,
        out_specs=[ANY] * n, scratch_shapes=_sems(3 * n), name="grad_scatter_chips")(*cs)


def _sum_chips(name, cs, r, place):
    hr = r.shape[1]
    tr = _row_tile(hr)
    nblk = hr // tr

    def kern(p_ref, c_ref, r0, r1, r2, r3, o_ref):
        me = p_ref[1]
        acc = None
        for k, rk in enumerate((r0, r1, r2, r3)):
            val = jnp.where(me == k, c_ref[0].astype(F32), rk[0].astype(F32))
            acc = val if acc is None else acc + val
        o_ref[...] = acc

    blk = (1, tr, r.shape[2])

    def slot(k):
        return lambda i, p: (jnp.where(p[1] == k, (k + 1) % 4, k), i, 0)

    return pl.pallas_call(
        kern,
        grid_spec=pltpu.PrefetchScalarGridSpec(
            num_scalar_prefetch=1, grid=(nblk,),
            in_specs=[pl.BlockSpec(blk, lambda i, p: (p[1], i, 0))] + [pl.BlockSpec(blk, slot(k)) for k in range(4)],
            out_specs=pl.BlockSpec((tr, r.shape[2]), lambda i, p: (p[0] * nblk + i, 0))),
        out_shape=jax.ShapeDtypeStruct((2 * hr, r.shape[2]), F32),
        compiler_params=_cparams(("parallel",)), name=name)(place, cs, r, r, r, r)


def _join_halves(fs):
    n = len(fs)

    def body(*refs):
        ins, outs = refs[:n], refs[n:2 * n]
        copy = _remote(refs[2 * n], refs[2 * n + 1])
        x, y, c, _ = _place()
        cps = []
        for i in range(n):
            hr = ins[i].shape[0] // 2
            cps.append(copy(i, ins[i].at[pl.ds(c * hr, hr), :], outs[i].at[pl.ds(c * hr, hr), :], (x, y, 1 - c)))
            cps[-1].start()
        for i in range(n):
            hr = ins[i].shape[0] // 2
            other = outs[i].at[pl.ds((1 - c) * hr, hr), :]
            copy(i, other, other, (x, y, 1 - c)).wait_recv()
        for cp in cps:
            cp.wait_send()

    return pl.pallas_call(
        body, out_shape=[jax.ShapeDtypeStruct(f.shape, f.dtype) for f in fs], in_specs=[ANY] * n,
        out_specs=[ANY] * n, input_output_aliases={i: i for i in range(n)}, scratch_shapes=_sems(n),
        name="grad_join_halves")(*fs)


def _half_rows(n, hc):
    return pl.ds(hc * (n // 2), n // 2)


def _gather_ici_side(shards):
    n = len(shards)

    def copies(ins, outs, copy):
        x, y, c, chips = _place()
        me = 2 * x + y
        for i in range(n):
            rows = _half_rows(ins[i].shape[0], c)
            for j, (px, py) in enumerate(chips):
                yield (copy(3 * i + j, ins[i].at[rows, :], outs[i].at[me, rows, :], (px, py, c)),
                       outs[i].at[2 * px + py, rows, :])

    def start(ins, outs, copy):
        for cp, _ in copies(ins, outs, copy):
            cp.start()

    def finish(ins, outs, copy):
        x, y, c, chips = _place()
        k = 0
        for cp, landing in copies(ins, outs, copy):
            copy(k, landing, landing, (x, y, c)).wait_recv()
            k += 1
        for cp, _ in copies(ins, outs, copy):
            cp.wait_send()

    return dict(ins=list(shards), outs=[jax.ShapeDtypeStruct((4,) + s.shape, s.dtype) for s in shards], nsem=3 * n,
                start=start, finish=finish)


def _gather_d2d_side(shards, bufs):
    n = len(shards)

    def copies(ins, outs, copy):
        x, y, c, chips = _place()
        me = 2 * x + y
        sibling = (x, y, 1 - c)
        for i in range(n):
            a = ins[i].shape[0]
            yield copy(4 * i + 3, ins[i], outs[i].at[me], sibling), outs[i].at[me]
            for j, (px, py) in enumerate(chips):
                src = 2 * px + py
                mine = outs[i].at[src, _half_rows(a, c), :]
                yield copy(4 * i + j, mine, mine, sibling), outs[i].at[src, _half_rows(a, 1 - c), :]

    def start(ins, outs, copy):
        for cp, _ in copies(ins, outs, copy):
            cp.start()

    def finish(ins, outs, copy):
        x, y, c, _ = _place()
        for i in range(n):
            for k, (cp, landing) in enumerate(list(copies(ins, outs, copy))[4 * i:4 * i + 4]):
                sem = 4 * i + 3 if k == 0 else 4 * i + k - 1
                copy(sem, landing, landing, (x, y, 1 - c)).wait_recv()
        for cp, _ in copies(ins, outs, copy):
            cp.wait_send()

    return dict(ins=list(shards) + list(bufs), outs=[jax.ShapeDtypeStruct(b.shape, b.dtype) for b in bufs],
                nsem=4 * n, alias={n + i: i for i in range(n)}, start=start, finish=finish)


def _scatter_side(cs):
    n = len(cs)

    def copies(ins, outs, copy):
        x, y, c, chips = _place()
        me = 2 * x + y
        for i in range(n):
            for j, (px, py) in enumerate(chips):
                yield copy(3 * i + j, ins[i].at[2 * px + py], outs[i].at[me], (px, py, c)), outs[i].at[2 * px + py]

    def start(ins, outs, copy):
        for cp, _ in copies(ins, outs, copy):
            cp.start()

    def finish(ins, outs, copy):
        x, y, c, _ = _place()
        for k, (cp, landing) in enumerate(copies(ins, outs, copy)):
            copy(k, landing, landing, (x, y, c)).wait_recv()
        for cp, _ in copies(ins, outs, copy):
            cp.wait_send()

    return dict(ins=list(cs), outs=[jax.ShapeDtypeStruct(a.shape, a.dtype) for a in cs], nsem=3 * n,
                start=start, finish=finish)


def _small_pack(dw, af, ab):
    return jnp.concatenate([dw, jnp.zeros((1, dw.shape[1]), F32), jnp.concatenate([af, ab], axis=1)], axis=0)


def _grad_pieces(n, grads):
    if n == "small":
        return jnp.stack([_small_pack(grads["dw_weight"][:, 256 * j:256 * (j + 1)],
                                      grads["w_alpha_f"][:, DK * j:DK * (j + 1)],
                                      grads["w_alpha_b"][:, DK * j:DK * (j + 1)]) for j in range(4)])
    if n == "w_in":
        g = _unperm_in_cols(grads["w_in_p"])
        return jnp.transpose(g.reshape(D, 4, D_IN // 4), (1, 0, 2))
    g = grads[n]
    return g if g.ndim == 3 else g.reshape(4, g.shape[0] // 4, g.shape[1])


def _chip_sums(tag, names, grads, place):
    gs = [_grad_pieces(n, grads) for n in names]
    swapped = _swap_halves("grad_swap_" + tag, gs)
    return [_add_halves("grad_add_" + n, g, r, place) for n, g, r in zip(names, gs, swapped)]


class _Overlap:
    SCATTER = {"in_bproj": ("w2_down", "w2_gu", "w_out"),
               "ffn1_wgu": ("w_conv_out", "w_gla_out", "w_in", "small"),
               "ffn1_bgu": ("w1_down", "w1_gu")}

    def __init__(self, late_shards, install, place):
        self.late, self.install, self.place = late_shards, install, place
        self.bufs, self.pending, self.landed = None, None, {}

    def side(self, tag, grads):
        if tag == "ffn1_gu":
            return _gather_ici_side(self.late)
        if tag == "ffn1_down":
            return _gather_d2d_side(self.late, self.bufs)
        if tag in self.SCATTER:
            names = self.SCATTER[tag]
            self.pending = (names, _chip_sums(tag, names, grads, self.place))
            return _scatter_side(self.pending[1])
        return None

    def done(self, tag, outs):
        if tag == "ffn1_gu":
            self.bufs = outs
        elif tag == "ffn1_down":
            self.install(outs)
        else:
            for n, cs, r in zip(*self.pending, outs):
                self.landed[n] = (cs, r)


def _with_side(comm, tag, grads, call):
    side = comm.side(tag, grads) if comm is not None else None
    if side is None:
        return call(None)
    res, outs = call(side)
    comm.done(tag, outs)
    return res


def _allreduce_small(v):
    def body(x_ref, out_ref, gath, send_sems, recv_sems, local_sem):
        x, y, c, chips = _place()
        me, sibling = (x, y, c), (x, y, 1 - c)

        def slot(px, py, pc):
            return gath.at[4 * px + 2 * py + pc]

        def copy(k, block, to, src=None):
            return pltpu.make_async_remote_copy(
                src_ref=slot(*block) if src is None else src, dst_ref=slot(*block), send_sem=send_sems.at[k],
                recv_sem=recv_sems.at[k], device_id=to, device_id_type=MESH)

        mine = pltpu.make_async_copy(x_ref, slot(*me), local_sem)
        mine.start()
        first = [copy(0, me, sibling, src=x_ref)]
        first += [copy(1 + j, me, (*chip, c), src=x_ref) for j, chip in enumerate(chips)]
        for cp in first:
            cp.start()
        passed = [copy(4 + j, (*chip, c), sibling) for j, chip in enumerate(chips)]
        for j, chip in enumerate(chips):
            copy(1 + j, (*chip, c), me).wait_recv()
            passed[j].start()
        copy(0, sibling, me).wait_recv()
        for j, chip in enumerate(chips):
            copy(4 + j, (*chip, 1 - c), me).wait_recv()
        for cp in first + passed:
            cp.wait_send()
        mine.wait()
        acc = gath[0]
        for k in range(1, 8):
            acc = acc + gath[k]
        out_ref[...] = acc

    vm = pl.BlockSpec(memory_space=pltpu.VMEM)
    return pl.pallas_call(
        body, out_shape=jax.ShapeDtypeStruct(v.shape, F32), in_specs=[vm], out_specs=vm,
        scratch_shapes=[pltpu.VMEM((8,) + v.shape, F32), pltpu.SemaphoreType.DMA((7,)),
                        pltpu.SemaphoreType.DMA((7,)), pltpu.SemaphoreType.DMA(())],
        name="allreduce_small")(v)


def _adamw(name, w, g, m, v):
    r, cols = w.shape
    budget = 262144
    tr = r if r * cols <= budget else next(c for c in (256, 128, 64, 32, 16, 8) if r % c == 0 and c * cols <= budget)

    def kern(w_ref, g_ref, m_ref, v_ref, d_ref, nm_ref, nv_ref):
        gv = g_ref[...]
        nm = ADAM_B1 * m_ref[...] + (1.0 - ADAM_B1) * gv
        nv = ADAM_B2 * v_ref[...] + (1.0 - ADAM_B2) * jnp.square(gv)
        m_hat = nm / (1.0 - ADAM_B1 ** ADAM_STEP)
        v_hat = nv / (1.0 - ADAM_B2 ** ADAM_STEP)
        d_ref[...] = -ADAM_LR * (m_hat / (jnp.sqrt(v_hat) + ADAM_EPS) + ADAM_WD * w_ref[...])
        nm_ref[...] = nm
        nv_ref[...] = nv

    spec = pl.BlockSpec((tr, cols), lambda i: (i, 0))
    shp = jax.ShapeDtypeStruct((r, cols), F32)
    return pl.pallas_call(kern, grid=(r // tr,), in_specs=[spec] * 4, out_specs=[spec] * 3, out_shape=[shp] * 3,
                          compiler_params=_cparams(("parallel",)), name=name)(w, g, m, v)


SHARDED = (("w_mod", 1), ("w1_gu", 1), ("w1_down", 0), ("w_in", 1), ("dw_weight", 1), ("w_conv_out", 0),
           ("w_alpha_f", 1), ("w_alpha_b", 1), ("w_gla_out", 0), ("w_out", 0), ("w2_gu", 1), ("w2_down", 0))
REPLICATED = ("c_ctx", "b_mod", "g_ffn1", "g_mix", "dw_bias", "conv_ln_g", "conv_ln_b", "b_alpha_f", "b_alpha_b",
              "gla_norm_g", "g_ffn2", "g_final")
WEIGHTS = ("c_ctx", "w_mod", "b_mod", "g_ffn1", "w1_gu", "w1_down", "g_mix", "w_in", "dw_weight", "dw_bias",
           "conv_ln_g", "conv_ln_b", "w_conv_out", "w_alpha_f", "b_alpha_f", "w_alpha_b", "b_alpha_b", "gla_norm_g",
           "w_gla_out", "w_out", "g_ffn2", "w2_gu", "w2_down", "g_final")
MATRICES = ("w_mod", "w1_gu", "w1_down", "w_in", "w_conv_out", "w_gla_out", "w_out", "w2_gu", "w2_down")


def _pack_flat(parts, align):
    flat = jnp.concatenate([p.reshape(-1) for p in parts])
    pad = (-flat.shape[0]) % align
    return jnp.concatenate([flat, jnp.zeros((pad,), flat.dtype)]).reshape(-1, 1024)


def _unpack_flat(flat2d, shapes):
    flat = flat2d.reshape(-1)
    out, off = [], 0
    for s in shapes:
        n = math.prod(s)
        out.append(flat[off:off + n].reshape(s))
        off += n
    return out


def kernel(x, c, ctx, c_ctx, w_mod, b_mod, g_ffn1, w1_gu, w1_down, g_mix, w_in, dw_weight, dw_bias, conv_ln_g, conv_ln_b, w_conv_out, w_alpha_f, b_alpha_f, w_alpha_b, b_alpha_b, gla_norm_g, w_gla_out, w_out, g_ffn2, w2_gu, w2_down, g_final, loss_target, m_c_ctx, m_w_mod, m_b_mod, m_g_ffn1, m_w1_gu, m_w1_down, m_g_mix, m_w_in, m_dw_weight, m_dw_bias, m_conv_ln_g, m_conv_ln_b, m_w_conv_out, m_w_alpha_f, m_b_alpha_f, m_w_alpha_b, m_b_alpha_b, m_gla_norm_g, m_w_gla_out, m_w_out, m_g_ffn2, m_w2_gu, m_w2_down, m_g_final, v_c_ctx, v_w_mod, v_b_mod, v_g_ffn1, v_w1_gu, v_w1_down, v_g_mix, v_w_in, v_dw_weight, v_dw_bias, v_conv_ln_g, v_conv_ln_b, v_w_conv_out, v_w_alpha_f, v_b_alpha_f, v_w_alpha_b, v_b_alpha_b, v_gla_norm_g, v_w_gla_out, v_w_out, v_g_ffn2, v_w2_gu, v_w2_down, v_g_final):
    given = dict(locals())
    w = {n: given[n] for n in WEIGHTS}
    m = {n: given["m_" + n] for n in WEIGHTS}
    v = {n: given["v_" + n] for n in WEIGHTS}

    def bf16_shard(n):
        return w[n][0].astype(BF16)

    def install(wts, got):
        for n in ("w_mod", "w1_gu", "w2_gu"):
            if n in got:
                wts[n] = got[n]
        for n in ("w1_down", "w2_down", "w_conv_out", "w_gla_out", "w_out"):
            if n in got:
                wts[n] = got[n].reshape(-1, D)
        if "w_in" in got:
            wts["w_in_p"] = _perm_in_cols(jnp.concatenate([got["w_in"][j] for j in range(4)], axis=1))

    early = ("w_mod", "w1_gu", "w1_down")
    late = tuple(n for n in MATRICES if n not in early)
    shards = [bf16_shard(n) for n in early] + [_small_pack(w["dw_weight"][0], w["w_alpha_f"][0], w["w_alpha_b"][0])]
    got = dict(zip(early + ("small",), _gather_weights(shards)))
    wts = {n: w[n] for n in REPLICATED}
    install(wts, got)
    sm = got["small"]
    wts["dw_weight"] = jnp.concatenate([sm[j, :CONV_W] for j in range(4)], axis=1)
    zpad = jnp.zeros((128, HEADS * DK), BF16)
    w_af = jnp.concatenate([sm[j, 32:32 + LOWRANK, :DK] for j in range(4)], axis=1)
    w_ab = jnp.concatenate([sm[j, 32:32 + LOWRANK, DK:] for j in range(4)], axis=1)
    wts["w_alpha_f_pad"] = zpad.at[0:LOWRANK].set(w_af.astype(BF16))
    wts["w_alpha_b_pad"] = zpad.at[LOWRANK:2 * LOWRANK].set(w_ab.astype(BF16))

    place = jnp.stack([lax.axis_index("c"), 2 * lax.axis_index("x") + lax.axis_index("y")]).astype(jnp.int32)
    comm = _Overlap([bf16_shard(n) for n in late], lambda bufs: install(wts, dict(zip(late, bufs))), place)
    loss, grad_x, grads = _local_step(x, c, ctx, loss_target, wts, comm)
    loss = lax.psum(loss, ("x", "y", "c"))

    tags = MATRICES + ("small",)
    rest = tuple(n for n in tags if n not in comm.landed)
    rest_sums = _chip_sums("rest", rest, grads, place)
    for n, cs, r in zip(rest, rest_sums, _scatter_chips(rest_sums)):
        comm.landed[n] = (cs, r)
    halves = [_sum_chips("grad_sum_" + t, *comm.landed[t], place) for t in tags]
    reduced = dict(zip(tags, _join_halves(halves)))
    g_shard = {n: reduced[n] for n in MATRICES}
    g_shard["dw_weight"] = reduced["small"][:CONV_W]
    g_shard["w_alpha_f"] = reduced["small"][32:32 + LOWRANK, :DK]
    g_shard["w_alpha_b"] = reduced["small"][32:32 + LOWRANK, DK:]

    rep_shapes = [w[n].shape for n in REPLICATED]
    small = _allreduce_small(_pack_flat([grads[n].reshape(w[n].shape) for n in REPLICATED], 8 * 1024))
    g_rep = dict(zip(REPLICATED, _unpack_flat(small, rep_shapes)))

    g_out, d_out, m_out, v_out = {}, {}, {}, {}
    for n, _ in SHARDED:
        s2 = w[n].shape[1:]
        d, nm, nv = _adamw("adamw_" + n, w[n].reshape(s2), g_shard[n], m[n].reshape(s2), v[n].reshape(s2))
        g_out[n] = g_shard[n].reshape(w[n].shape)
        d_out[n], m_out[n], v_out[n] = d.reshape(w[n].shape), nm.reshape(w[n].shape), nv.reshape(w[n].shape)
    pk = lambda t: _pack_flat([t[n] for n in REPLICATED], 8 * 1024)
    d, nm, nv = _adamw("adamw_vectors", pk(w), small, pk(m), pk(v))
    for n, dd, mm, vv in zip(REPLICATED, _unpack_flat(d, rep_shapes), _unpack_flat(nm, rep_shapes),
                             _unpack_flat(nv, rep_shapes)):
        g_out[n], d_out[n], m_out[n], v_out[n] = g_rep[n], dd, mm, vv

    return (loss, grad_x, *[g_out[n] for n in WEIGHTS], *[d_out[n] for n in WEIGHTS],
            *[m_out[n] for n in WEIGHTS], *[v_out[n] for n in WEIGHTS])
```

```python
import functools
import math

import jax
import jax.numpy as jnp
from jax import lax
from jax.experimental import pallas as pl
from jax.experimental.pallas import tpu as pltpu

F32, BF16 = jnp.float32, jnp.bfloat16
MESH = pl.DeviceIdType.MESH
HIGHEST = lax.Precision.HIGHEST

D = 1024
FF = 2816
HEADS, DK, DV = 4, 128, 256
LOWRANK = 16
CONV_W = 31
CHUNK = 64
TAU = 16.0
EPS = 1e-6
Q_SCALE = DK ** -0.5
TM = 256
D_IN = 7200
D_INP = 7296
LR_COL = 7168
OG_CB, GA_CB, GB_CB = 6, 4, 5
VMEM_LIMIT = 52 * 1024 * 1024
WGRAD_VMEM = 40 * 1024 * 1024

ADAM_LR, ADAM_B1, ADAM_B2, ADAM_EPS, ADAM_WD, ADAM_STEP = 0.001, 0.9, 0.999, 1e-08, 0.01, 10


def _silu(x):
    return x * jax.nn.sigmoid(x)


def _rms(h, g):
    return h * lax.rsqrt(jnp.mean(h * h, axis=-1, keepdims=True) + EPS) * g


def _modnorm(x, g, shift, scale):
    return _rms(x, g) * (1 + scale) + shift


def _cparams(sem=None):
    return pltpu.CompilerParams(dimension_semantics=sem, vmem_limit_bytes=VMEM_LIMIT)


def _twice(v):
    return v, v


def tok(arr, width=None, cb=0, clamp=None):
    return ("tok", arr, arr.shape[1] if width is None else width, cb, clamp)


def mod(arr):
    return ("mod", arr)


def const(arr):
    return ("const", arr)


def _rowmap(name, body, n_tiles, ins, outs, *, tpb, nb):
    def modrow(i):
        return jnp.minimum(i // tpb, nb)

    in_specs, args = [], []
    for spec in ins:
        if spec[0] == "tok":
            _, arr, width, cb, clamp = spec
            if clamp is None:
                im = lambda i, cb=cb: (i, cb)
            else:
                im = lambda i, cb=cb, clamp=clamp: (jnp.minimum(i, clamp), cb)
            in_specs.append(pl.BlockSpec((TM, width), im))
        elif spec[0] == "mod":
            arr = spec[1]
            in_specs.append(pl.BlockSpec((1, 1, arr.shape[2]), lambda i: (modrow(i), 0, 0)))
        else:
            arr = spec[1]
            in_specs.append(pl.BlockSpec(arr.shape, lambda i, nd=arr.ndim: (0,) * nd))
        args.append(arr)
    out_specs, out_shapes, aliases = [], [], {}
    for o in outs:
        if o[0] == "tok":
            _, rows, width, dtype = o
            out_shapes.append(jax.ShapeDtypeStruct((rows, width), dtype))
            out_specs.append(pl.BlockSpec((TM, width), lambda i: (i, 0)))
        elif o[0] == "cols":
            buf, width, cb = o[1:4]
            first_tile = o[4] if len(o) > 4 else 0
            if not isinstance(buf, jax.ShapeDtypeStruct):
                aliases[len(args)] = len(out_shapes)
                in_specs.append(ANY)
                args.append(buf)
            out_shapes.append(jax.ShapeDtypeStruct(buf.shape, buf.dtype))
            out_specs.append(pl.BlockSpec((TM, width), lambda i, cb=cb, t0=first_tile: (i + t0, cb)))
        elif o[0] == "tokT":
            _, rows, width, dtype = o
            out_shapes.append(jax.ShapeDtypeStruct((width, rows), dtype))
            out_specs.append(pl.BlockSpec((width, TM), lambda i: (0, i)))
        elif o[0] == "acc":
            _, rows, width = o
            out_shapes.append(jax.ShapeDtypeStruct((rows, width), F32))
            out_specs.append(pl.BlockSpec((rows, width), lambda i: (0, 0)))
        else:
            width = o[1]
            rows_visited = min((n_tiles - 1) // tpb, nb) + 1
            out_shapes.append(jax.ShapeDtypeStruct((rows_visited, 1, width), F32))
            out_specs.append(pl.BlockSpec((1, 1, width), lambda i: (modrow(i), 0, 0)))
    n_in = len(ins)

    def kern(*refs):
        i = pl.program_id(0)
        vals = []
        for r, spec in zip(refs[:n_in], ins):
            val = r[0] if spec[0] == "mod" else r[...]
            vals.append(val.astype(F32) if spec[0] == "tok" and val.dtype == BF16 else val)
        res = body(i, *vals)
        for r, o, val in zip(refs[len(args):], outs, res):
            if o[0] in ("tok", "cols"):
                r[...] = val.astype(r.dtype)
            elif o[0] == "tokT":
                r[...] = val.T.astype(r.dtype)
            elif o[0] == "acc":
                @pl.when(i == 0)
                def _():
                    r[...] = jnp.zeros(r.shape, F32)
                r[...] += jnp.broadcast_to(val, r.shape)
            else:
                first = jnp.logical_or(i == 0, modrow(i) != modrow(jnp.maximum(i - 1, 0)))

                @pl.when(first)
                def _():
                    r[...] = jnp.zeros(r.shape, F32)
                r[0] += val

    return pl.pallas_call(
        kern, grid=(n_tiles,), in_specs=in_specs, out_specs=out_specs, out_shape=out_shapes,
        input_output_aliases=aliases, compiler_params=_cparams(("arbitrary",)), name=name)(*args)


def _pick(n, cands):
    for c in cands:
        if n % c == 0:
            return c
    return n


def _pallas(kern, *, grid, in_specs, out_specs, out_shape, scratch_shapes, sem, name, args, side=None):
    if side is None:
        return pl.pallas_call(kern, grid=grid, in_specs=in_specs, out_specs=out_specs, out_shape=out_shape,
                              scratch_shapes=scratch_shapes, compiler_params=_cparams(sem), name=name)(*args)
    single = not isinstance(out_shape, (list, tuple))
    shapes = [out_shape] if single else list(out_shape)
    ospecs = [out_specs] if single else list(out_specs)
    n_in, n_out, n_scr = len(in_specs), len(shapes), len(scratch_shapes)
    s_in, s_out = list(side["ins"]), list(side["outs"])

    def wrapped(*refs):
        pos = [0]

        def take(n):
            pos[0] += n
            return refs[pos[0] - n:pos[0]]
        ins, sins, outs, souts, scr, sems = take(n_in), take(len(s_in)), take(n_out), take(len(s_out)), take(n_scr), take(2)
        ids = [pl.program_id(k) for k in range(len(grid))]
        first = functools.reduce(jnp.logical_and, [i == 0 for i in ids])
        last = functools.reduce(jnp.logical_and, [i == g - 1 for i, g in zip(ids, grid)])
        copy = _remote(*sems)

        @pl.when(first)
        def _():
            side["start"](sins, souts, copy)
        kern(*ins, *outs, *scr)

        @pl.when(last)
        def _():
            side["finish"](sins, souts, copy)

    res = pl.pallas_call(
        wrapped, grid=grid, in_specs=list(in_specs) + [ANY] * len(s_in), out_specs=ospecs + [ANY] * len(s_out),
        out_shape=shapes + s_out, scratch_shapes=list(scratch_shapes) + _sems(side["nsem"]),
        input_output_aliases={n_in + a: n_out + b for a, b in side.get("alias", {}).items()},
        compiler_params=_cparams(("arbitrary",) * len(grid)), name=name)(*args, *s_in)
    main = res[:n_out]
    return (main[0] if single else main), list(res[n_out:])


def _mm(name, a, b, *, trans_b=False, out_dtype=F32, a_fn=None, bias=None, rows=None, side=None):
    m, k = a.shape if a.ndim == 2 else (a.shape[1], 2 * a.shape[2])
    m = m if rows is None else rows
    shard = b.shape[2] if b.ndim == 3 else None
    if trans_b:
        n = b.shape[-2]
        tk = _pick(shard, (2816, 2304, 1408, 1024)) if shard else (
            k if k <= 2816 else _pick(k, (2816, 2432, 2304, 2048, 1536, 1408, 1024, 512, 256, 128)))
        tn = _pick(n, (1024, 512, 384, 256, 128))
    else:
        n = 4 * shard if shard else b.shape[1]
        tk = k if k <= 2816 else _pick(k, (2816, 2432, 2304, 2048, 1536, 1408, 1024, 512, 256, 128))
        tn = _pick(shard, (512, 384, 1408, 256, 128)) if shard else _pick(n, (1024, 2432, 512, 384, 256, 128))
    tm = _pick(m, (1024, 512, 256))
    nk = k // tk
    per = shard // (tk if trans_b else tn) if shard else None
    dims = (((1,), (1,)), ((), ())) if trans_b else (((1,), (0,)), ((), ()))

    def kern(*refs):
        a_ref, b_ref = refs[0], refs[1]
        bias_ref = refs[2] if bias is not None else None
        o_ref, acc_ref = refs[-2], refs[-1]
        kk = pl.program_id(2)
        av = a_ref[...]
        if a_fn is not None:
            av = a_fn(av)
        p = lax.dot_general(av.astype(BF16), b_ref[...].astype(BF16), dims, preferred_element_type=F32)

        def finish(total):
            if bias_ref is not None:
                total = total + bias_ref[...]
            o_ref[...] = total.astype(o_ref.dtype)

        if nk == 1:
            finish(p)
        else:
            @pl.when(kk == 0)
            def _():
                acc_ref[...] = p

            @pl.when(kk > 0)
            def _():
                acc_ref[...] += p

            @pl.when(kk == nk - 1)
            def _():
                finish(acc_ref[...])

    if shard and trans_b:
        b_spec = pl.BlockSpec((None, tn, tk), lambda i, j, kk: (kk // per, j, kk % per))
    elif shard:
        b_spec = pl.BlockSpec((None, tk, tn), lambda i, j, kk: (j // per, kk, j % per))
    elif trans_b:
        b_spec = pl.BlockSpec((tn, tk), lambda i, j, kk: (j, kk))
    else:
        b_spec = pl.BlockSpec((tk, tn), lambda i, j, kk: (kk, j))
    if a.ndim == 3:
        pa = a.shape[2] // tk
        a_spec = pl.BlockSpec((None, tm, tk), lambda i, j, kk: (kk // pa, i, kk % pa))
    else:
        a_spec = pl.BlockSpec((tm, tk), lambda i, j, kk: (i, kk))
    in_specs = [a_spec, b_spec]
    args = [a, b]
    if bias is not None:
        in_specs.append(pl.BlockSpec((1, tn), lambda i, j, kk: (0, j)))
        args.append(bias)
    return _pallas(
        kern, grid=(m // tm, n // tn, nk), in_specs=in_specs,
        out_specs=pl.BlockSpec((tm, tn), lambda i, j, kk: (i, j)),
        out_shape=jax.ShapeDtypeStruct((m, n), out_dtype),
        scratch_shapes=[pltpu.VMEM((tm, tn) if nk > 1 else (8, 128), F32)],
        sem=("parallel", "parallel", "arbitrary"), name=name, args=args, side=side)


def _mm_tn(name, x, dy, rows=None, col_shards=False):
    t = x.shape[0] if rows is None else rows
    k1, n1 = x.shape[1], (dy.shape[1] if dy.ndim == 2 else 2 * dy.shape[2])
    tt = _pick(t, (512, 256, 128, 64, 8))
    tk1 = _pick(k1, (1024, 1408, 512, 256, 128))
    tn = _pick(n1 // 4, (512, 384, 1408, 256, 128)) if col_shards else _pick(n1, (512, 384, 256, 128))
    per = n1 // 4 // tn
    ns = t // tt

    def kern(x_ref, dy_ref, o_ref, acc_ref):
        s = pl.program_id(2)
        p = lax.dot_general(x_ref[...].astype(BF16), dy_ref[...].astype(BF16), (((0,), (0,)), ((), ())),
                            preferred_element_type=F32)

        @pl.when(s == 0)
        def _():
            acc_ref[...] = p

        @pl.when(s > 0)
        def _():
            acc_ref[...] += p

        @pl.when(s == ns - 1)
        def _():
            o_ref[...] = acc_ref[...].astype(o_ref.dtype)

    if col_shards:
        out_spec = pl.BlockSpec((None, tk1, tn), lambda i, j, s: (j // per, i, j % per))
        out_shape = jax.ShapeDtypeStruct((4, k1, n1 // 4), BF16)
    else:
        out_spec = pl.BlockSpec((tk1, tn), lambda i, j, s: (i, j))
        out_shape = jax.ShapeDtypeStruct((k1, n1), BF16)
    if dy.ndim == 3:
        pd = dy.shape[2] // tn
        dy_spec = pl.BlockSpec((None, tt, tn), lambda i, j, s: (j // pd, s, j % pd))
    else:
        dy_spec = pl.BlockSpec((tt, tn), lambda i, j, s: (s, j))
    return pl.pallas_call(
        kern, grid=(k1 // tk1, n1 // tn, ns),
        in_specs=[pl.BlockSpec((tt, tk1), lambda i, j, s: (s, i)), dy_spec],
        out_specs=out_spec, out_shape=out_shape, scratch_shapes=[pltpu.VMEM((tk1, tn), F32)],
        compiler_params=_cparams(("parallel", "parallel", "arbitrary")), name=name)(x, dy)


def _wgrad(name, xt, dy, rows=None, col_shards=False, side=None):
    k1 = xt.shape[0]
    t = xt.shape[1] if rows is None else rows
    n1 = dy.shape[1] if dy.ndim == 2 else 2 * dy.shape[2]
    tm = _pick(k1, (1024, 1408, 512, 256))
    tn = _pick(n1 // 4, (1408, 512, 384, 256, 128)) if col_shards else _pick(n1, (1024, 2432, 512, 384, 256, 128))
    fixed = tm * tn * (4 + 4 + 2 * 2)
    tk = next((c for c in (2048, 1536, 1024, 512, 256, 128)
               if t % c == 0 and fixed + 4 * c * (tm + tn) <= WGRAD_VMEM), 128)
    ns = t // tk
    per = n1 // 4 // tn

    def kern(x_ref, dy_ref, o_ref, acc_ref):
        s = pl.program_id(2)
        p = jnp.dot(x_ref[...], dy_ref[...], preferred_element_type=F32)

        @pl.when(s == 0)
        def _():
            acc_ref[...] = p

        @pl.when(s > 0)
        def _():
            acc_ref[...] += p

        @pl.when(s == ns - 1)
        def _():
            o_ref[...] = acc_ref[...].astype(o_ref.dtype)

    if dy.ndim == 3:
        pd = dy.shape[2] // tn
        dy_spec = pl.BlockSpec((None, tk, tn), lambda i, j, s: (j // pd, s, j % pd))
    else:
        dy_spec = pl.BlockSpec((tk, tn), lambda i, j, s: (s, j))
    if col_shards:
        out_spec = pl.BlockSpec((None, tm, tn), lambda i, j, s: (j // per, i, j % per))
        out_shape = jax.ShapeDtypeStruct((4, k1, n1 // 4), BF16)
    else:
        out_spec = pl.BlockSpec((tm, tn), lambda i, j, s: (i, j))
        out_shape = jax.ShapeDtypeStruct((k1, n1), BF16)
    return _pallas(
        kern, grid=(k1 // tm, n1 // tn, ns),
        in_specs=[pl.BlockSpec((tm, tk), lambda i, j, s: (i, s)), dy_spec],
        out_specs=out_spec, out_shape=out_shape, scratch_shapes=[pltpu.VMEM((tm, tn), F32)],
        sem=("parallel", "parallel", "arbitrary"), name=name, args=[xt, dy], side=side)


def _swiglu_fwd(name, u, w_gu, side=None):
    m = u.shape[0]
    half = w_gu.shape[2]
    tm = _pick(m, (512, 256))

    def kern(u_ref, wa_ref, wb_ref, ab_ref, hm_ref, hmt_ref):
        uv = u_ref[...]
        a = jnp.dot(uv, wa_ref[...], preferred_element_type=F32)
        b = jnp.dot(uv, wb_ref[...], preferred_element_type=F32)
        ab_ref[0] = a.astype(BF16)
        ab_ref[1] = b.astype(BF16)
        hm = (_silu(a) * b).astype(BF16)
        hm_ref[...] = hm
        hmt_ref[...] = hm.T

    return _pallas(
        kern, grid=(m // tm, 2),
        in_specs=[pl.BlockSpec((tm, D), lambda i, j: (i, 0)),
                  pl.BlockSpec((None, D, half), lambda i, j: (j, 0, 0)),
                  pl.BlockSpec((None, D, half), lambda i, j: (2 + j, 0, 0))],
        out_specs=[pl.BlockSpec((2, tm, half), lambda i, j: (0, i, j)),
                   pl.BlockSpec((tm, half), lambda i, j: (i, j)),
                   pl.BlockSpec((half, tm), lambda i, j: (j, i))],
        out_shape=[jax.ShapeDtypeStruct((2, m, FF), BF16), jax.ShapeDtypeStruct((m, FF), BF16),
                   jax.ShapeDtypeStruct((FF, m), BF16)],
        scratch_shapes=[], sem=("parallel", "parallel"), name=name, args=[u, w_gu, w_gu], side=side)


def _swiglu_bwd(name, df, w_down, ab):
    m = df.shape[0]
    half = FF // 2
    tm = _pick(m, (512, 256))

    def kern(df_ref, w_ref, ab_ref, o_ref):
        dh = lax.dot_general(df_ref[...], w_ref[...], (((1,), (1,)), ((), ())), preferred_element_type=F32)
        a = ab_ref[0].astype(F32)
        b = ab_ref[1].astype(F32)
        s = jax.nn.sigmoid(a)
        o_ref[0] = (dh * b * (s * (1 + a * (1 - s)))).astype(BF16)
        o_ref[1] = (dh * (a * s)).astype(BF16)

    return pl.pallas_call(
        kern, grid=(m // tm, 2),
        in_specs=[pl.BlockSpec((tm, D), lambda i, j: (i, 0)),
                  pl.BlockSpec((half, D), lambda i, j: (j, 0)),
                  pl.BlockSpec((2, tm, half), lambda i, j: (0, i, j))],
        out_specs=pl.BlockSpec((2, tm, half), lambda i, j: (0, i, j)),
        out_shape=jax.ShapeDtypeStruct((2, m, FF), BF16),
        compiler_params=_cparams(("parallel", "parallel")), name=name)(df, w_down, ab)


def _gla_maps(bl, t, tc):
    nx, nc = t // CHUNK, tc // CHUNK
    nxb = bl * nx

    def rowblk(d, b, n):
        c_ctx = jnp.where(d == 0, n, nc - 1 - n)
        c_x = jnp.where(d == 0, n - nc, nx - 1 - (n - nc))
        return jnp.where(n < nc, nxb + b * nc + c_ctx, b * nx + c_x)

    def xblk(d, b, n):
        n2 = jnp.maximum(n, nc)
        return b * nx + jnp.where(d == 0, n2 - nc, nx - 1 - (n2 - nc))

    return nx, nc, rowblk, xblk


def _dot01(m, x, cm):
    x1 = x.astype(BF16)
    r1 = x - x1.astype(F32)
    x2 = r1.astype(BF16)
    x3 = (r1 - x2.astype(F32)).astype(BF16)
    w = x.shape[1]
    p = lax.dot_general(m.astype(BF16), jnp.concatenate([x1, x2, x3], axis=1), (((cm,), (0,)), ((), ())),
                        preferred_element_type=F32)
    return p[:, :w] + p[:, w:2 * w] + p[:, 2 * w:]


def _gla_chunk(m, q, k, v, g, h):
    gh = g[:, h * DK:(h + 1) * DK]
    b = _dot01(m, gh, 1)
    tot = jnp.sum(gh, axis=0, keepdims=True)
    mid = b[CHUNK // 2:CHUNK // 2 + 1, :]
    qh = q[:, h * DK:(h + 1) * DK] * Q_SCALE
    kh = k[:, h * DK:(h + 1) * DK]
    vh = v[:, h * DV:(h + 1) * DV]
    return b, tot, mid, qh, kh, vh


def _dot(a, b, ca, cb):
    return lax.dot_general(a.astype(BF16), b.astype(BF16), (((ca,), (cb,)), ((), ())),
                           preferred_element_type=F32)


def _gla_fwd(p, g2, mmats, bl, t, tc):
    nx, nc, rowblk, xblk = _gla_maps(bl, t, tc)
    ns = nx + nc

    def kern(q0, k0, v0, g0, q1, k1, v1, g1, m_ref, o0, o1, ss_ref, s_ref):
        n = pl.program_id(1)

        @pl.when(n == 0)
        def _():
            s_ref[...] = jnp.zeros(s_ref.shape, F32)
        sides = ((q0, k0, v0, g0, o0), (q1, k1, v1, g1, o1))
        loaded = [(m_ref[d], q_ref[...].astype(F32), k_ref[...].astype(F32), v_ref[...].astype(F32), g_ref[...])
                  for d, (q_ref, k_ref, v_ref, g_ref, _) in enumerate(sides)]
        chains = [(d, h) for d in range(2) for h in range(HEADS)]
        base = [_gla_chunk(*loaded[d], h) for d, h in chains]
        pre = []
        for (d, h), (b, tot, mid, qh, kh, vh) in zip(chains, base):
            s0 = s_ref[d, h * DV:(h + 1) * DV, :]
            ss_ref[d, 0, 0, h * DV:(h + 1) * DV, :] = s0.astype(BF16)
            pre.append((s0, kh * jnp.exp(tot - b), qh * jnp.exp(b), qh * jnp.exp(b - mid), kh * jnp.exp(mid - b)))
        raw = [(_dot(qm, km, 1, 1), _dot(qe, s0, 1, 1), _dot(bs[5], kl, 0, 0))
               for bs, (s0, kl, qe, qm, km) in zip(base, pre)]
        for (d, h), bs, (s0, kl, qe, qm, km), (att_raw, inter, s_add) in zip(chains, base, pre, raw):
            s_ref[d, h * DV:(h + 1) * DV, :] = s0 * jnp.exp(bs[1]) + s_add
            sides[d][4][:, h * DV:(h + 1) * DV] = inter + _dot(loaded[d][0] * att_raw, bs[5], 1, 0)

    def operands(d):
        return [pl.BlockSpec((CHUNK, 512), lambda b, n: (rowblk(d, b, n), 4)),
                pl.BlockSpec((CHUNK, 512), lambda b, n: (rowblk(d, b, n), 5)),
                pl.BlockSpec((CHUNK, 1024), lambda b, n: (rowblk(d, b, n), 3)),
                pl.BlockSpec((CHUNK, 512), lambda b, n: (rowblk(d, b, n), d))]

    o_shape = jax.ShapeDtypeStruct((bl * t, HEADS * DV), F32)
    return pl.pallas_call(
        kern, grid=(bl, ns),
        in_specs=operands(0) + operands(1) + [pl.BlockSpec((2, CHUNK, CHUNK), lambda b, n: (0, 0, 0))],
        out_specs=[pl.BlockSpec((CHUNK, 1024), lambda b, n: (xblk(0, b, n), 0)),
                   pl.BlockSpec((CHUNK, 1024), lambda b, n: (xblk(1, b, n), 0)),
                   pl.BlockSpec((2, 1, 1, HEADS * DV, DK), lambda b, n: (0, b, n, 0, 0))],
        out_shape=[o_shape, o_shape, jax.ShapeDtypeStruct((2, bl, ns, HEADS * DV, DK), BF16)],
        scratch_shapes=[pltpu.VMEM((2, HEADS * DV, DK), F32)],
        compiler_params=_cparams(("parallel", "arbitrary")), name="gla_fwd")(p, p, p, g2, p, p, p, g2, mmats)


def _gla_bwd(p, g2, mmats, ssave, do, bl, t, tc):
    nx, nc, rowblk, xblk = _gla_maps(bl, t, tc)
    ns = nx + nc
    rev = lambda s: ns - 1 - s

    def kern(q0, k0, v0, g0, do0, q1, k1, v1, g1, do1, m_ref, ss_ref,
             dq0, dk0, dv0, dg0, dq1, dk1, dv1, dg1, ds_ref):
        step = pl.program_id(1)
        n = ns - 1 - step

        @pl.when(step == 0)
        def _():
            ds_ref[...] = jnp.zeros(ds_ref.shape, F32)
        live = (n >= nc).astype(F32)
        sides = ((q0, k0, v0, g0, do0, dq0, dk0, dv0, dg0), (q1, k1, v1, g1, do1, dq1, dk1, dv1, dg1))
        loaded = [(m_ref[d], s[0][...].astype(F32), s[1][...].astype(F32), s[2][...].astype(F32), s[3][...])
                  for d, s in enumerate(sides)]
        dovs = [s[4][...] * live for s in sides]
        chains = [(d, h) for d in range(2) for h in range(HEADS)]
        base = [_gla_chunk(*loaded[d], h) for d, h in chains]
        pre = []
        for (d, h), (b, tot, mid, qh, kh, vh) in zip(chains, base):
            eb, ebm, emb, etb = jnp.exp(b), jnp.exp(b - mid), jnp.exp(mid - b), jnp.exp(tot - b)
            pre.append(dict(
                eb=eb, ebm=ebm, emb=emb, etb=etb, etot=jnp.exp(tot), qe=qh * eb, qm=qh * ebm, km=kh * emb, kl=kh * etb,
                vh=vh, doh=dovs[d][:, h * DV:(h + 1) * DV], s0=ss_ref[d, 0, 0, h * DV:(h + 1) * DV, :].astype(F32),
                ds1=ds_ref[d, h * DV:(h + 1) * DV, :]))
        first = [dict(att=_dot(c["qm"], c["km"], 1, 1), datt=_dot(c["doh"], c["vh"], 1, 1),
                      dqe=_dot(c["doh"], c["s0"], 1, 0), ds_add=_dot(c["doh"], c["qe"], 0, 0),
                      dkl=_dot(c["vh"], c["ds1"], 1, 0), dv_s=_dot(c["kl"], c["ds1"], 1, 1)) for c in pre]
        second = []
        for (d, h), c, f in zip(chains, pre, first):
            m = loaded[d][0]
            ds_ref[d, h * DV:(h + 1) * DV, :] = c["ds1"] * c["etot"] + f["ds_add"]
            att, datt = m * f["att"], m * f["datt"]
            second.append(dict(dqm=_dot(datt, c["km"], 1, 0), dkm=_dot(datt, c["qm"], 0, 0),
                               dv_a=_dot(att, c["doh"], 0, 0)))
        for (d, h), c, f, s in zip(chains, pre, first, second):
            dq_ref, dk_ref, dv_ref, dg_ref = sides[d][5:]
            dtot = c["etot"] * jnp.sum(c["ds1"] * c["s0"], axis=0, keepdims=True) + jnp.sum(
                f["dkl"] * c["kl"], axis=0, keepdims=True)
            db = f["dqe"] * c["qe"] + s["dqm"] * c["qm"] - s["dkm"] * c["km"] - f["dkl"] * c["kl"]
            dq_ref[:, h * DK:(h + 1) * DK] = (f["dqe"] * c["eb"] + s["dqm"] * c["ebm"]) * Q_SCALE
            dk_ref[:, h * DK:(h + 1) * DK] = s["dkm"] * c["emb"] + f["dkl"] * c["etb"]
            dv_ref[:, h * DV:(h + 1) * DV] = s["dv_a"] + f["dv_s"]
            dg_ref[:, h * DK:(h + 1) * DK] = _dot01(loaded[d][0], db, 0) + dtot

    nt = p.shape[0]

    def operands(d):
        return [pl.BlockSpec((CHUNK, 512), lambda b, s: (rowblk(d, b, rev(s)), 4)),
                pl.BlockSpec((CHUNK, 512), lambda b, s: (rowblk(d, b, rev(s)), 5)),
                pl.BlockSpec((CHUNK, 1024), lambda b, s: (rowblk(d, b, rev(s)), 3)),
                pl.BlockSpec((CHUNK, 512), lambda b, s: (rowblk(d, b, rev(s)), d)),
                pl.BlockSpec((CHUNK, 1024), lambda b, s: (xblk(d, b, rev(s)), 0))]

    def results(d):
        row = lambda b, s: (rowblk(d, b, rev(s)), 0)
        return [pl.BlockSpec((CHUNK, 512), row), pl.BlockSpec((CHUNK, 512), row), pl.BlockSpec((CHUNK, 1024), row),
                pl.BlockSpec((CHUNK, 512), row)]

    shapes = [jax.ShapeDtypeStruct((nt, 512), F32), jax.ShapeDtypeStruct((nt, 512), F32),
              jax.ShapeDtypeStruct((nt, 1024), F32), jax.ShapeDtypeStruct((nt, 512), F32)]
    out = pl.pallas_call(
        kern, grid=(bl, ns),
        in_specs=operands(0) + operands(1) + [
            pl.BlockSpec((2, CHUNK, CHUNK), lambda b, s: (0, 0, 0)),
            pl.BlockSpec((2, 1, 1, HEADS * DV, DK), lambda b, s: (0, b, rev(s), 0, 0))],
        out_specs=results(0) + results(1), out_shape=shapes + shapes,
        scratch_shapes=[pltpu.VMEM((2, HEADS * DV, DK), F32)],
        compiler_params=_cparams(("parallel", "arbitrary")), name="gla_bwd")(
            p, p, p, g2, do, p, p, p, g2, do, mmats, ssave)
    return out[:4], out[4:]


CONV_CT = 256
CONV_PAD = 16
CONV_RC = 128
CONV_HALO = 24


def _conv_fill(zp, z_ref, t):
    zp[0:CONV_PAD, :] = jnp.zeros((CONV_PAD, CONV_CT), F32)
    zp[CONV_PAD + t:2 * CONV_PAD + t, :] = jnp.zeros((CONV_PAD, CONV_CT), F32)
    zp[CONV_PAD:CONV_PAD + t, :] = z_ref[...]


def _dwconv(name, z, w, bias, bl, t, flip):
    def kern(z_ref, w_ref, b_ref, o_ref, zp):
        _conv_fill(zp, z_ref, t)
        offs = [(CONV_W - j) if flip else (j + 1) for j in range(CONV_W)]
        for r in range(0, t, CONV_RC):
            acc = jnp.broadcast_to(b_ref[...], (CONV_RC, CONV_CT))
            for rot in range(8):
                win = zp[r + rot:r + rot + CONV_RC + CONV_HALO, :]
                for j in range(CONV_W):
                    if offs[j] % 8 == rot:
                        a = offs[j] - rot
                        acc = acc + w_ref[j:j + 1, :] * win[a:a + CONV_RC, :]
            o_ref[r:r + CONV_RC, :] = acc

    return pl.pallas_call(
        kern, grid=(bl, 1024 // CONV_CT),
        in_specs=[pl.BlockSpec((t, CONV_CT), lambda b, c: (b, c)),
                  pl.BlockSpec((32, CONV_CT), lambda b, c: (0, c)),
                  pl.BlockSpec((1, CONV_CT), lambda b, c: (0, c))],
        out_specs=pl.BlockSpec((t, CONV_CT), lambda b, c: (b, c)),
        out_shape=jax.ShapeDtypeStruct(z.shape, F32),
        scratch_shapes=[pltpu.VMEM((t + 2 * CONV_PAD, CONV_CT), F32)],
        compiler_params=_cparams(("parallel", "parallel")), name=name)(z, w, bias)


def _dwconv_wgrad(z, dzc, bl, t):
    def kern(z_ref, d_ref, dw_ref, db_ref, zp):
        b = pl.program_id(1)

        @pl.when(b == 0)
        def _():
            dw_ref[...] = jnp.zeros(dw_ref.shape, F32)
            db_ref[...] = jnp.zeros(db_ref.shape, F32)
        _conv_fill(zp, z_ref, t)
        for rot in range(8):
            taps = [j for j in range(CONV_W) if (j + 1) % 8 == rot]
            accs = [jnp.zeros((8, CONV_CT), F32) for _ in taps]
            for r in range(0, t, CONV_RC):
                d = d_ref[r:r + CONV_RC, :]
                win = zp[r + rot:r + rot + CONV_RC + CONV_HALO, :]
                for k, j in enumerate(taps):
                    a = j + 1 - rot
                    prod = d * win[a:a + CONV_RC, :]
                    accs[k] = accs[k] + jnp.sum(prod.reshape(CONV_RC // 8, 8, CONV_CT), axis=0)
            for k, j in enumerate(taps):
                dw_ref[j:j + 1, :] += jnp.sum(accs[k], axis=0, keepdims=True)
        db_ref[...] += jnp.sum(d_ref[...], axis=0, keepdims=True)

    return pl.pallas_call(
        kern, grid=(1024 // CONV_CT, bl),
        in_specs=[pl.BlockSpec((t, CONV_CT), lambda c, b: (b, c)),
                  pl.BlockSpec((t, CONV_CT), lambda c, b: (b, c))],
        out_specs=[pl.BlockSpec((32, CONV_CT), lambda c, b: (0, c)),
                   pl.BlockSpec((1, CONV_CT), lambda c, b: (0, c))],
        out_shape=[jax.ShapeDtypeStruct((32, 1024), F32), jax.ShapeDtypeStruct((1, 1024), F32)],
        scratch_shapes=[pltpu.VMEM((t + 2 * CONV_PAD, CONV_CT), F32)],
        compiler_params=_cparams(("parallel", "arbitrary")), name="dwconv_wgrad")(z, dzc)


def _ffn_fwd(tag, xin, n_tiles, g, sh, sc, gate, w_gu, w_down, tpb, nb, comm=None):
    rows = n_tiles * TM
    rm = functools.partial(_rowmap, tpb=tpb, nb=nb)
    u, ut = rm(tag + "_norm", lambda i, x, g_, sh_, sc_: _twice(_modnorm(x, g_, sh_, sc_)), n_tiles,
               [tok(xin), const(g), mod(sh), mod(sc)], [("tok", rows, D, BF16), ("tokT", rows, D, BF16)])
    ab, hm, hmt = _with_side(comm, tag + "_gu", None, lambda s: _swiglu_fwd(tag + "_gu", u, w_gu, side=s))
    f = _with_side(comm, tag + "_down", None, lambda s: _mm(tag + "_down", hm, w_down, side=s))
    (xout,) = rm(tag + "_res", lambda i, x, f_, gt: (x + 0.5 * gt * f_,), n_tiles,
                 [tok(xin), tok(f), mod(gate)], [("tok", rows, D, F32)])
    return xout, (ut, ab, hmt, f)


def _ffn_bwd(tag, xin, saved, dxout, dx_clamp, n_tiles, g, sh, sc, gate, w_gu, w_down, tpb, nb, comm=None, grads=None,
             names=None):
    ut, ab, hmt, f = saved
    rows = n_tiles * TM
    rm = functools.partial(_rowmap, tpb=tpb, nb=nb)

    def mask(i):
        return 1.0 if dx_clamp is None else (i <= dx_clamp).astype(F32)

    def b1(i, dx, f_, gt):
        dx = dx * mask(i)
        return (0.5 * gt * dx, jnp.sum(0.5 * f_ * dx, axis=0, keepdims=True))
    df, dgate = rm(tag + "_bres", b1, n_tiles, [tok(dxout, clamp=dx_clamp), tok(f), mod(gate)],
                   [("tok", rows, D, BF16), ("modacc", D)])
    grads[names[1]] = _wgrad(tag + "_wdown", hmt, df)
    dab = _swiglu_bwd(tag + "_bdown", df, w_down, ab)
    grads[names[0]] = _with_side(comm, tag + "_wgu", grads,
                                 lambda s: _wgrad(tag + "_wgu", ut, dab, col_shards=True, side=s))
    du = _with_side(comm, tag + "_bgu", grads, lambda s: _mm(tag + "_bgu", dab, w_gu, trans_b=True, side=s))

    def b3(i, x, g_, sh_, sc_, du_, dx):
        _, vjp = jax.vjp(_modnorm, x, g_, sh_, sc_)
        dxn, dg, dsh, dsc = vjp(du_)
        return (dx * mask(i) + dxn, dg, dsh, dsc)
    dxin, dg, dsh, dsc = rm(tag + "_bnorm", b3, n_tiles,
                            [tok(xin), const(g), mod(sh), mod(sc), tok(du), tok(dxout, clamp=dx_clamp)],
                            [("tok", rows, D, F32), ("acc", 1, D), ("modacc", D), ("modacc", D)])
    return dxin, dict(g=dg, sh=dsh, sc=dsc, gate=dgate)


def _perm_in_cols(w):
    pad = jnp.zeros(w.shape[:-1] + (D_INP - D_IN,), w.dtype)
    return jnp.concatenate([w[..., :4096], w[..., 5152:7200], w[..., 4096:5120], w[..., 5120:5152], pad], axis=-1)


def _unperm_in_cols(w):
    return jnp.concatenate([w[..., :4096], w[..., 6144:LR_COL], w[..., LR_COL:LR_COL + 32], w[..., 4096:6144]], axis=-1)


def _local_step(x, c, ctx, target, wts, comm=None):
    bl, t, _ = x.shape
    tc = ctx.shape[1]
    nx_rows, nc_rows = bl * t, bl * tc
    nt_rows = nx_rows + nc_rows
    tpb = t // TM
    nxt, ntt = nx_rows // TM, nt_rows // TM
    nb = bl
    rm = functools.partial(_rowmap, tpb=tpb, nb=nb)
    last_x = nxt - 1

    x0 = jnp.concatenate([x.reshape(nx_rows, D), ctx.reshape(nc_rows, D)], axis=0)
    tgt = target.reshape(nx_rows, D)

    cc = jnp.concatenate([c, wts["c_ctx"].reshape(1, D), jnp.zeros((8 - bl - 1, D), F32)], axis=0)
    modv = _mm("mod_fwd", cc, wts["w_mod"], a_fn=_silu, bias=wts["b_mod"])
    mods = [modv[:nb + 1, k * D:(k + 1) * D].reshape(nb + 1, 1, D) for k in range(9)]

    x1, sv1 = _ffn_fwd("ffn1", x0, ntt, wts["g_ffn1"], mods[0], mods[1], mods[2], wts["w1_gu"], wts["w1_down"],
                       tpb, nb, comm)
    u2, u2t = rm("in_norm", lambda i, x_, g_, sh_, sc_: _twice(_modnorm(x_, g_, sh_, sc_)), ntt,
                 [tok(x1), const(wts["g_mix"]), mod(mods[3]), mod(mods[4])],
                 [("tok", nt_rows, D, BF16), ("tokT", nt_rows, D, BF16)])
    w_inp = wts["w_in_p"]
    p = _with_side(comm, "in_proj", None, lambda s: _mm("in_proj", u2, w_inp, out_dtype=BF16, side=s))

    waf, wab, baf, bab = wts["w_alpha_f_pad"], wts["w_alpha_b_pad"], wts["b_alpha_f"], wts["b_alpha_b"]

    def dec_fwd(i, lr, wf, wb, bf_, bb_):
        zf = _dot(lr, wf, 1, 0) + bf_
        zb = _dot(lr, wb, 1, 0) + bb_
        return (jnp.concatenate([jax.nn.log_sigmoid(zf) / TAU, jax.nn.log_sigmoid(zb) / TAU], axis=1),)
    (gfb,) = rm("decay_fwd", dec_fwd, ntt, [tok(p, 128, LR_COL // 128), const(waf), const(wab), const(baf), const(bab)],
                [("tok", nt_rows, 1024, F32)])
    g2 = gfb
    tri = jnp.tril(jnp.ones((CHUNK, CHUNK), F32))
    mmats = jnp.stack([tri, tri.T])
    *o2, ssave = _gla_fwd(p, g2, mmats, bl, t, tc)

    gn_g = wts["gla_norm_g"]

    def gla_out(of, ob, og, gn):
        o = of + ob
        parts = []
        for h in range(HEADS):
            oh = o[:, h * DV:(h + 1) * DV]
            parts.append(oh * lax.rsqrt(jnp.mean(oh * oh, axis=-1, keepdims=True) + EPS))
        return jnp.concatenate(parts, axis=1) * gn * _silu(og)
    yg_in, yg_int = rm("gla_out", lambda i, of, ob, og, gn: _twice(gla_out(of, ob, og, gn)), nxt,
                       [tok(o2[0]), tok(o2[1]), tok(p, 1024, OG_CB), const(gn_g)],
                       [("tok", nx_rows, D, BF16), ("tokT", nx_rows, D, BF16)])
    y_gla = _with_side(comm, "gla_proj", None, lambda s: _mm("gla_proj", yg_in, wts["w_gla_out"], side=s))

    (z,) = rm("glu", lambda i, a, b: (a * jax.nn.sigmoid(b),), nxt, [tok(p, 1024, 0), tok(p, 1024, 1)],
              [("tok", nx_rows, D, F32)])
    dw_w = jnp.concatenate([wts["dw_weight"], jnp.zeros((1, D), F32)], axis=0)
    zc = _dwconv("dwconv_fwd", z, dw_w, wts["dw_bias"], bl, t, False)

    def ln_silu(zc_, g_, b_):
        mu = jnp.mean(zc_, axis=-1, keepdims=True)
        var = jnp.mean(jnp.square(zc_ - mu), axis=-1, keepdims=True)
        return _silu((zc_ - mu) * lax.rsqrt(var + EPS) * g_ + b_)
    ln_g, ln_b = wts["conv_ln_g"], wts["conv_ln_b"]
    zl, zlt = rm("conv_ln", lambda i, zc_, g_, b_: _twice(ln_silu(zc_, g_, b_)), nxt,
                 [tok(zc), const(ln_g), const(ln_b)], [("tok", nx_rows, D, BF16), ("tokT", nx_rows, D, BF16)])
    y_conv = _mm("conv_proj", zl, wts["w_conv_out"])

    mg, mgt = rm("merge", lambda i, ga, gb, yc, yg: _twice(jax.nn.sigmoid(ga) * yc + jax.nn.sigmoid(gb) * yg), nxt,
                 [tok(p, 1024, GA_CB), tok(p, 1024, GB_CB), tok(y_conv), tok(y_gla)],
                 [("tok", nx_rows, D, BF16), ("tokT", nx_rows, D, BF16)])
    mix = _mm("out_proj", mg, wts["w_out"])
    (x2,) = rm("mix_res", lambda i, x_, mx_, gt: (x_ + gt * mx_,), nxt, [tok(x1), tok(mix), mod(mods[5])],
               [("tok", nx_rows, D, F32)])

    x3, sv2 = _ffn_fwd("ffn2", x2, nxt, wts["g_ffn2"], mods[6], mods[7], mods[8], wts["w2_gu"], wts["w2_down"],
                       tpb, nb)
    g_fin = wts["g_final"].reshape(1, D)

    def head(i, x_, g_, tg):
        y, vjp = jax.vjp(_rms, x_, g_)
        diff = y - tg
        dx, dg = vjp(diff * (1.0 / D))
        loss = 0.5 * jnp.sum(jnp.mean(diff * diff, axis=-1, keepdims=True))
        return dx, dg, loss
    dx3, dg_final, loss_acc = rm("loss_head", head, nxt, [tok(x3), const(g_fin), tok(tgt)],
                                 [("tok", nx_rows, D, F32), ("acc", 1, D), ("acc", 8, 128)])
    loss = loss_acc[0, 0]

    grads = {}
    dx2, gf2 = _ffn_bwd("ffn2", x2, sv2, dx3, None, nxt, wts["g_ffn2"], mods[6], mods[7], mods[8],
                        wts["w2_gu"], wts["w2_down"], tpb, nb, comm, grads, ("w2_gu", "w2_down"))
    grads["g_ffn2"] = gf2["g"]

    dmix, dgate5 = rm("mix_bres", lambda i, dx, mx_, gt: (gt * dx, jnp.sum(mx_ * dx, axis=0, keepdims=True)), nxt,
                      [tok(dx2), tok(mix), mod(mods[5])], [("tok", nx_rows, D, BF16), ("modacc", D)])
    dmg = _mm("out_bproj", dmix, wts["w_out"], trans_b=True)
    grads["w_out"] = _wgrad("out_wgrad", mgt, dmix)

    def merge_bwd(i, dm, ga, gb, yc, yg):
        keep = (i <= last_x).astype(F32)
        dm = dm * keep
        sa, sb = jax.nn.sigmoid(ga), jax.nn.sigmoid(gb)
        return dm * sa, dm * sb, jnp.concatenate([dm * yc * sa * (1 - sa), dm * yg * sb * (1 - sb)], axis=1)
    cl = dict(clamp=last_x)
    dyc, dyg, dp = rm("merge_bwd", merge_bwd, ntt,
                      [tok(dmg, **cl), tok(p, 1024, GA_CB, last_x), tok(p, 1024, GB_CB, last_x), tok(y_conv, **cl),
                       tok(y_gla, **cl)],
                      [("tok", nt_rows, D, BF16), ("tok", nt_rows, D, BF16),
                       ("cols", jax.ShapeDtypeStruct((nt_rows, D_INP), BF16), 2048, 2)])

    dzl = _mm("conv_bproj", dyc, wts["w_conv_out"], trans_b=True, rows=nx_rows)
    grads["w_conv_out"] = _wgrad("conv_wgrad", zlt, dyc, rows=nx_rows)

    def ln_bwd(i, zc_, g_, b_, dz_):
        _, vjp = jax.vjp(ln_silu, zc_, g_, b_)
        return vjp(dz_)
    dzc, dln_g, dln_b = rm("conv_ln_bwd", ln_bwd, nxt, [tok(zc), const(ln_g), const(ln_b), tok(dzl)],
                           [("tok", nx_rows, D, F32), ("acc", 1, D), ("acc", 1, D)])
    dz = _dwconv("dwconv_bwd", dzc, dw_w, jnp.zeros((1, D), F32), bl, t, True)
    ddw, ddb = _dwconv_wgrad(z, dzc, bl, t)
    grads.update(conv_ln_g=dln_g, conv_ln_b=dln_b, dw_weight=ddw[:CONV_W], dw_bias=ddb)

    def glu_bwd(i, dz_, a, b):
        keep = (i <= last_x).astype(F32)
        dz_ = dz_ * keep
        s = jax.nn.sigmoid(b)
        return (jnp.concatenate([dz_ * s, dz_ * a * s * (1 - s)], axis=1),)
    (dp,) = rm("glu_bwd", glu_bwd, ntt, [tok(dz, **cl), tok(p, 1024, 0, last_x), tok(p, 1024, 1, last_x)],
               [("cols", dp, 2048, 0)])

    dyg_in = _mm("gla_bproj", dyg, wts["w_gla_out"], trans_b=True, rows=nx_rows)
    grads["w_gla_out"] = _wgrad("gla_wgrad", yg_int, dyg, rows=nx_rows)

    def gla_out_bwd(i, of, ob, og, gn, dy):
        _, vjp = jax.vjp(gla_out, of, ob, og, gn)
        do_, _, dog_, dgn_ = vjp(dy)
        return do_, dog_, dgn_
    do, dp, dgn = rm("gla_out_bwd", gla_out_bwd, nxt,
                     [tok(o2[0]), tok(o2[1]), tok(p, 1024, OG_CB), const(gn_g), tok(dyg_in)],
                     [("tok", nx_rows, D, F32), ("cols", dp, 1024, OG_CB), ("acc", 1, D)])
    grads["gla_norm_g"] = dgn
    (dp,) = rm("og_ctx_zero", lambda i: (jnp.zeros((TM, D), F32),), ntt - nxt, [], [("cols", dp, 1024, OG_CB, nxt)])

    dq2, dk2, dv2, dg2 = zip(*_gla_bwd(p, g2, mmats, ssave, do, bl, t, tc))

    def dec_bwd(i, lr, wf, wb, bf_, bb_, dgf, dgb_):
        zf = _dot(lr, wf, 1, 0) + bf_
        zb = _dot(lr, wb, 1, 0) + bb_
        dzf = dgf * (1 - jax.nn.sigmoid(zf)) * (1.0 / TAU)
        dzb = dgb_ * (1 - jax.nn.sigmoid(zb)) * (1.0 / TAU)
        dlr = _dot(dzf, wf, 1, 1) + _dot(dzb, wb, 1, 1)
        return (dlr, _dot(lr, dzf, 0, 0), _dot(lr, dzb, 0, 0), jnp.sum(dzf, axis=0, keepdims=True),
                jnp.sum(dzb, axis=0, keepdims=True))
    dp, dwaf, dwab, dbaf, dbab = rm(
        "decay_bwd", dec_bwd, ntt,
        [tok(p, 128, LR_COL // 128), const(waf), const(wab), const(baf), const(bab), tok(dg2[0]), tok(dg2[1])],
        [("cols", dp, 128, LR_COL // 128), ("acc", 128, 512), ("acc", 128, 512), ("acc", 1, 512), ("acc", 1, 512)])
    grads.update(w_alpha_f=dwaf[:LOWRANK], w_alpha_b=dwab[LOWRANK:2 * LOWRANK], b_alpha_f=dbaf, b_alpha_b=dbab)

    (dp,) = rm("gla_sum",
               lambda i, q0, q1, k0, k1, v0, v1: (jnp.concatenate([q0 + q1, k0 + k1, v0 + v1], axis=1),), ntt,
               [tok(dq2[0]), tok(dq2[1]), tok(dk2[0]), tok(dk2[1]), tok(dv2[0]), tok(dv2[1])],
               [("cols", dp, 2048, 1)])
    du2 = _with_side(comm, "in_bproj", grads, lambda s: _mm("in_bproj", dp, w_inp, trans_b=True, side=s))
    grads["w_in_p"] = _wgrad("in_wgrad", u2t, dp)

    def in_norm_bwd(i, x_, g_, sh_, sc_, du_, dx):
        keep = (i <= last_x).astype(F32)
        _, vjp = jax.vjp(_modnorm, x_, g_, sh_, sc_)
        dxn, dg, dsh, dsc = vjp(du_)
        return (dx * keep + dxn, dg, dsh, dsc)
    dx1, dg_mix, dsh3, dsc4 = rm("in_norm_bwd", in_norm_bwd, ntt,
                                 [tok(x1), const(wts["g_mix"]), mod(mods[3]), mod(mods[4]), tok(du2), tok(dx2, **cl)],
                                 [("tok", nt_rows, D, F32), ("acc", 1, D), ("modacc", D), ("modacc", D)])
    grads["g_mix"] = dg_mix

    dx0, gf1 = _ffn_bwd("ffn1", x0, sv1, dx1, None, ntt, wts["g_ffn1"], mods[0], mods[1], mods[2],
                        wts["w1_gu"], wts["w1_down"], tpb, nb, comm, grads, ("w1_gu", "w1_down"))
    grads["g_ffn1"] = gf1["g"]
    grad_x = dx0[:nx_rows].reshape(bl, t, D)

    dmods = [gf1["sh"], gf1["sc"], gf1["gate"], dsh3, dsc4, dgate5, gf2["sh"], gf2["sc"], gf2["gate"]]
    dmod = jnp.concatenate(
        [jnp.concatenate([a.reshape(a.shape[0], D), jnp.zeros((8 - a.shape[0], D), F32)], axis=0) for a in dmods],
        axis=1)
    dsc_ = _mm("mod_bproj", dmod, wts["w_mod"], trans_b=True)

    def silu_bwd(cc_ref, d_ref, dm_ref, dcc_ref, scc_ref, db_ref):
        cc_ = cc_ref[...]
        s = jax.nn.sigmoid(cc_)
        dcc_ref[...] = d_ref[...] * (s * (1 + cc_ * (1 - s)))
        scc_ref[...] = (cc_ * s).astype(BF16)
        db_ref[...] = jnp.sum(dm_ref[...], axis=0, keepdims=True)
    dcc, scc, db_mod = pl.pallas_call(
        silu_bwd, out_shape=[jax.ShapeDtypeStruct((8, D), F32), jax.ShapeDtypeStruct((8, D), BF16),
                             jax.ShapeDtypeStruct((1, 9 * D), F32)], name="mod_silu_bwd")(cc, dsc_, dmod)
    grads["c_ctx"] = dcc[nb]
    grads["b_mod"] = db_mod
    grads["w_mod"] = _mm_tn("mod_wgrad", scc, dmod, col_shards=True)
    grads["g_final"] = dg_final.reshape(D)
    return loss, grad_x, grads


ANY = pl.BlockSpec(memory_space=pl.ANY)


def _place():
    x, y, c = lax.axis_index("x"), lax.axis_index("y"), lax.axis_index("c")
    chips = [(1 - x, y), (x, 1 - y), (1 - x, 1 - y)]
    return x, y, c, chips


def _remote(send_sems, recv_sems):
    def copy(k, src, dst, to):
        return pltpu.make_async_remote_copy(src_ref=src, dst_ref=dst, send_sem=send_sems.at[k],
                                            recv_sem=recv_sems.at[k], device_id=to, device_id_type=MESH)
    return copy


def _sems(n):
    return [pltpu.SemaphoreType.DMA((n,)), pltpu.SemaphoreType.DMA((n,))]


def _gather_weights(shards):
    n = len(shards)

    def body(*refs):
        ins, outs = refs[:n], refs[n:2 * n]
        copy = _remote(refs[2 * n], refs[2 * n + 1])
        x, y, c, chips = _place()
        me = 2 * x + y
        sibling = (x, y, 1 - c)
        started = []

        def rows(i, hc):
            hr = ins[i].shape[0] // 2
            return pl.ds(hc * hr, hr)

        for i in range(n):
            started.append(copy(7 * i + 6, ins[i], outs[i].at[me], sibling))
            started[-1].start()
            for j, (px, py) in enumerate(chips):
                started.append(copy(7 * i + j, ins[i].at[rows(i, c), :], outs[i].at[me, rows(i, c), :], (px, py, c)))
                started[-1].start()
        for i in range(n):
            for j, (px, py) in enumerate(chips):
                half = outs[i].at[2 * px + py, rows(i, c), :]
                copy(7 * i + j, half, half, (px, py, c)).wait_recv()
                started.append(copy(7 * i + 3 + j, half, half, sibling))
                started[-1].start()
        for i in range(n):
            copy(7 * i + 6, ins[i], outs[i].at[me], sibling).wait_recv()
            for j, (px, py) in enumerate(chips):
                other = outs[i].at[2 * px + py, rows(i, 1 - c), :]
                copy(7 * i + 3 + j, other, other, sibling).wait_recv()
        for cp in started:
            cp.wait_send()

    return pl.pallas_call(
        body, out_shape=[jax.ShapeDtypeStruct((4,) + s.shape, s.dtype) for s in shards], in_specs=[ANY] * n,
        out_specs=[ANY] * n, scratch_shapes=_sems(7 * n), name="gather_weights")(*shards)


def _swap_halves(name, gs):
    n = len(gs)

    def body(*refs):
        ins, outs = refs[:n], refs[n:2 * n]
        copy = _remote(refs[2 * n], refs[2 * n + 1])
        x, y, c, _ = _place()
        cps = []
        for i in range(n):
            hr = ins[i].shape[1] // 2
            cps.append(copy(i, ins[i].at[:, pl.ds((1 - c) * hr, hr), :], outs[i], (x, y, 1 - c)))
            cps[-1].start()
        for cp in cps:
            cp.wait()

    return pl.pallas_call(
        body, out_shape=[jax.ShapeDtypeStruct((4, g.shape[1] // 2, g.shape[2]), g.dtype) for g in gs],
        in_specs=[ANY] * n, out_specs=[ANY] * n, scratch_shapes=_sems(n), name=name)(*gs)


def _row_tile(hr):
    return hr if hr <= 256 else _pick(hr, (256, 176, 128, 64, 32, 16))


def _add_halves(name, g, r, place):
    hr = r.shape[1]
    tr = _row_tile(hr)
    nblk = hr // tr

    def kern(p_ref, g_ref, r_ref, o_ref):
        o_ref[...] = (g_ref[...].astype(F32) + r_ref[...].astype(F32)).astype(o_ref.dtype)

    blk = (1, tr, g.shape[2])
    return pl.pallas_call(
        kern,
        grid_spec=pltpu.PrefetchScalarGridSpec(
            num_scalar_prefetch=1, grid=(4, nblk),
            in_specs=[pl.BlockSpec(blk, lambda j, i, p: (j, p[0] * nblk + i, 0)),
                      pl.BlockSpec(blk, lambda j, i, p: (j, i, 0))],
            out_specs=pl.BlockSpec(blk, lambda j, i, p: (j, i, 0))),
        out_shape=jax.ShapeDtypeStruct(r.shape, r.dtype),
        compiler_params=_cparams(("parallel", "parallel")), name=name)(place, g, r)


def _scatter_chips(cs):
    n = len(cs)

    def body(*refs):
        ins, outs = refs[:n], refs[n:2 * n]
        copy = _remote(refs[2 * n], refs[2 * n + 1])
        x, y, c, chips = _place()
        me = 2 * x + y
        sends = []
        for i in range(n):
            for j, (px, py) in enumerate(chips):
                sends.append(copy(3 * i + j, ins[i].at[2 * px + py], outs[i].at[me], (px, py, c)))
                sends[-1].start()
        for i in range(n):
            for j, (px, py) in enumerate(chips):
                src = 2 * px + py
                copy(3 * i + j, ins[i].at[src], outs[i].at[src], (px, py, c)).wait_recv()
        for cp in sends:
            cp.wait_send()

    return pl.pallas_call(
        body, out_shape=[jax.ShapeDtypeStruct(a.shape, a.dtype) for a in cs], in_specs=[ANY] * n,
        out_specs=[ANY] * n, scratch_shapes=_sems(3 * n), name="grad_scatter_chips")(*cs)


def _sum_chips(name, cs, r, place):
    hr = r.shape[1]
    tr = _row_tile(hr)
    nblk = hr // tr

    def kern(p_ref, c_ref, r0, r1, r2, r3, o_ref):
        me = p_ref[1]
        acc = None
        for k, rk in enumerate((r0, r1, r2, r3)):
            val = jnp.where(me == k, c_ref[0].astype(F32), rk[0].astype(F32))
            acc = val if acc is None else acc + val
        o_ref[...] = acc

    blk = (1, tr, r.shape[2])

    def slot(k):
        return lambda i, p: (jnp.where(p[1] == k, (k + 1) % 4, k), i, 0)

    return pl.pallas_call(
        kern,
        grid_spec=pltpu.PrefetchScalarGridSpec(
            num_scalar_prefetch=1, grid=(nblk,),
            in_specs=[pl.BlockSpec(blk, lambda i, p: (p[1], i, 0))] + [pl.BlockSpec(blk, slot(k)) for k in range(4)],
            out_specs=pl.BlockSpec((tr, r.shape[2]), lambda i, p: (p[0] * nblk + i, 0))),
        out_shape=jax.ShapeDtypeStruct((2 * hr, r.shape[2]), F32),
        compiler_params=_cparams(("parallel",)), name=name)(place, cs, r, r, r, r)


def _join_halves(fs):
    n = len(fs)

    def body(*refs):
        ins, outs = refs[:n], refs[n:2 * n]
        copy = _remote(refs[2 * n], refs[2 * n + 1])
        x, y, c, _ = _place()
        cps = []
        for i in range(n):
            hr = ins[i].shape[0] // 2
            cps.append(copy(i, ins[i].at[pl.ds(c * hr, hr), :], outs[i].at[pl.ds(c * hr, hr), :], (x, y, 1 - c)))
            cps[-1].start()
        for i in range(n):
            hr = ins[i].shape[0] // 2
            other = outs[i].at[pl.ds((1 - c) * hr, hr), :]
            copy(i, other, other, (x, y, 1 - c)).wait_recv()
        for cp in cps:
            cp.wait_send()

    return pl.pallas_call(
        body, out_shape=[jax.ShapeDtypeStruct(f.shape, f.dtype) for f in fs], in_specs=[ANY] * n,
        out_specs=[ANY] * n, input_output_aliases={i: i for i in range(n)}, scratch_shapes=_sems(n),
        name="grad_join_halves")(*fs)


def _half_rows(n, hc):
    return pl.ds(hc * (n // 2), n // 2)


def _gather_ici_side(shards):
    n = len(shards)

    def copies(ins, outs, copy):
        x, y, c, chips = _place()
        me = 2 * x + y
        for i in range(n):
            rows = _half_rows(ins[i].shape[0], c)
            for j, (px, py) in enumerate(chips):
                yield (copy(3 * i + j, ins[i].at[rows, :], outs[i].at[me, rows, :], (px, py, c)),
                       outs[i].at[2 * px + py, rows, :])

    def start(ins, outs, copy):
        for cp, _ in copies(ins, outs, copy):
            cp.start()

    def finish(ins, outs, copy):
        x, y, c, chips = _place()
        k = 0
        for cp, landing in copies(ins, outs, copy):
            copy(k, landing, landing, (x, y, c)).wait_recv()
            k += 1
        for cp, _ in copies(ins, outs, copy):
            cp.wait_send()

    return dict(ins=list(shards), outs=[jax.ShapeDtypeStruct((4,) + s.shape, s.dtype) for s in shards], nsem=3 * n,
                start=start, finish=finish)


def _gather_d2d_side(shards, bufs):
    n = len(shards)

    def copies(ins, outs, copy):
        x, y, c, chips = _place()
        me = 2 * x + y
        sibling = (x, y, 1 - c)
        for i in range(n):
            a = ins[i].shape[0]
            yield copy(4 * i + 3, ins[i], outs[i].at[me], sibling), outs[i].at[me]
            for j, (px, py) in enumerate(chips):
                src = 2 * px + py
                mine = outs[i].at[src, _half_rows(a, c), :]
                yield copy(4 * i + j, mine, mine, sibling), outs[i].at[src, _half_rows(a, 1 - c), :]

    def start(ins, outs, copy):
        for cp, _ in copies(ins, outs, copy):
            cp.start()

    def finish(ins, outs, copy):
        x, y, c, _ = _place()
        for i in range(n):
            for k, (cp, landing) in enumerate(list(copies(ins, outs, copy))[4 * i:4 * i + 4]):
                sem = 4 * i + 3 if k == 0 else 4 * i + k - 1
                copy(sem, landing, landing, (x, y, 1 - c)).wait_recv()
        for cp, _ in copies(ins, outs, copy):
            cp.wait_send()

    return dict(ins=list(shards) + list(bufs), outs=[jax.ShapeDtypeStruct(b.shape, b.dtype) for b in bufs],
                nsem=4 * n, alias={n + i: i for i in range(n)}, start=start, finish=finish)


def _scatter_side(cs):
    n = len(cs)

    def copies(ins, outs, copy):
        x, y, c, chips = _place()
        me = 2 * x + y
        for i in range(n):
            for j, (px, py) in enumerate(chips):
                yield copy(3 * i + j, ins[i].at[2 * px + py], outs[i].at[me], (px, py, c)), outs[i].at[2 * px + py]

    def start(ins, outs, copy):
        for cp, _ in copies(ins, outs, copy):
            cp.start()

    def finish(ins, outs, copy):
        x, y, c, _ = _place()
        for k, (cp, landing) in enumerate(copies(ins, outs, copy)):
            copy(k, landing, landing, (x, y, c)).wait_recv()
        for cp, _ in copies(ins, outs, copy):
            cp.wait_send()

    return dict(ins=list(cs), outs=[jax.ShapeDtypeStruct(a.shape, a.dtype) for a in cs], nsem=3 * n,
                start=start, finish=finish)


def _small_pack(dw, af, ab):
    return jnp.concatenate([dw, jnp.zeros((1, dw.shape[1]), F32), jnp.concatenate([af, ab], axis=1)], axis=0)


def _grad_pieces(n, grads):
    if n == "small":
        return jnp.stack([_small_pack(grads["dw_weight"][:, 256 * j:256 * (j + 1)],
                                      grads["w_alpha_f"][:, DK * j:DK * (j + 1)],
                                      grads["w_alpha_b"][:, DK * j:DK * (j + 1)]) for j in range(4)])
    if n == "w_in":
        g = _unperm_in_cols(grads["w_in_p"])
        return jnp.transpose(g.reshape(D, 4, D_IN // 4), (1, 0, 2))
    g = grads[n]
    return g if g.ndim == 3 else g.reshape(4, g.shape[0] // 4, g.shape[1])


def _chip_sums(tag, names, grads, place):
    gs = [_grad_pieces(n, grads) for n in names]
    swapped = _swap_halves("grad_swap_" + tag, gs)
    return [_add_halves("grad_add_" + n, g, r, place) for n, g, r in zip(names, gs, swapped)]


class _Overlap:
    SCATTER = {"in_bproj": ("w2_down", "w2_gu", "w_out"),
               "ffn1_wgu": ("w_conv_out", "w_gla_out", "w_in", "small"),
               "ffn1_bgu": ("w1_down", "w1_gu")}
    GATHER = {("ffn1_gu", "ffn1_down"): ("w_in", "w_conv_out", "w_gla_out", "w_out"),
              ("in_proj", "gla_proj"): ("w2_gu", "w2_down")}
    LATE = tuple(n for names in GATHER.values() for n in names)

    def __init__(self, shard_of, install, place):
        self.shard_of, self.install, self.place = shard_of, install, place
        self.bufs, self.pending, self.landed = {}, None, {}

    def side(self, tag, grads):
        for (ici, d2d), names in self.GATHER.items():
            shards = [self.shard_of(n) for n in names]
            if tag == ici:
                return _gather_ici_side(shards)
            if tag == d2d:
                return _gather_d2d_side(shards, self.bufs[ici])
        if tag in self.SCATTER:
            names = self.SCATTER[tag]
            self.pending = (names, _chip_sums(tag, names, grads, self.place))
            return _scatter_side(self.pending[1])
        return None

    def done(self, tag, outs):
        for (ici, d2d), names in self.GATHER.items():
            if tag == ici:
                self.bufs[ici] = outs
                return
            if tag == d2d:
                self.install(dict(zip(names, outs)))
                return
        for n, cs, r in zip(*self.pending, outs):
            self.landed[n] = (cs, r)


def _with_side(comm, tag, grads, call):
    side = comm.side(tag, grads) if comm is not None else None
    if side is None:
        return call(None)
    res, outs = call(side)
    comm.done(tag, outs)
    return res


def _allreduce_small(v):
    def body(x_ref, out_ref, gath, send_sems, recv_sems, local_sem):
        x, y, c, chips = _place()
        me, sibling = (x, y, c), (x, y, 1 - c)

        def slot(px, py, pc):
            return gath.at[4 * px + 2 * py + pc]

        def copy(k, block, to, src=None):
            return pltpu.make_async_remote_copy(
                src_ref=slot(*block) if src is None else src, dst_ref=slot(*block), send_sem=send_sems.at[k],
                recv_sem=recv_sems.at[k], device_id=to, device_id_type=MESH)

        mine = pltpu.make_async_copy(x_ref, slot(*me), local_sem)
        mine.start()
        first = [copy(0, me, sibling, src=x_ref)]
        first += [copy(1 + j, me, (*chip, c), src=x_ref) for j, chip in enumerate(chips)]
        for cp in first:
            cp.start()
        passed = [copy(4 + j, (*chip, c), sibling) for j, chip in enumerate(chips)]
        for j, chip in enumerate(chips):
            copy(1 + j, (*chip, c), me).wait_recv()
            passed[j].start()
        copy(0, sibling, me).wait_recv()
        for j, chip in enumerate(chips):
            copy(4 + j, (*chip, 1 - c), me).wait_recv()
        for cp in first + passed:
            cp.wait_send()
        mine.wait()
        acc = gath[0]
        for k in range(1, 8):
            acc = acc + gath[k]
        out_ref[...] = acc

    vm = pl.BlockSpec(memory_space=pltpu.VMEM)
    return pl.pallas_call(
        body, out_shape=jax.ShapeDtypeStruct(v.shape, F32), in_specs=[vm], out_specs=vm,
        scratch_shapes=[pltpu.VMEM((8,) + v.shape, F32), pltpu.SemaphoreType.DMA((7,)),
                        pltpu.SemaphoreType.DMA((7,)), pltpu.SemaphoreType.DMA(())],
        name="allreduce_small")(v)


def _adamw(name, w, g, m, v):
    r, cols = w.shape
    budget = 262144
    tr = r if r * cols <= budget else next(c for c in (256, 128, 64, 32, 16, 8) if r % c == 0 and c * cols <= budget)

    def kern(w_ref, g_ref, m_ref, v_ref, d_ref, nm_ref, nv_ref):
        gv = g_ref[...]
        nm = ADAM_B1 * m_ref[...] + (1.0 - ADAM_B1) * gv
        nv = ADAM_B2 * v_ref[...] + (1.0 - ADAM_B2) * jnp.square(gv)
        m_hat = nm / (1.0 - ADAM_B1 ** ADAM_STEP)
        v_hat = nv / (1.0 - ADAM_B2 ** ADAM_STEP)
        d_ref[...] = -ADAM_LR * (m_hat / (jnp.sqrt(v_hat) + ADAM_EPS) + ADAM_WD * w_ref[...])
        nm_ref[...] = nm
        nv_ref[...] = nv

    spec = pl.BlockSpec((tr, cols), lambda i: (i, 0))
    shp = jax.ShapeDtypeStruct((r, cols), F32)
    return pl.pallas_call(kern, grid=(r // tr,), in_specs=[spec] * 4, out_specs=[spec] * 3, out_shape=[shp] * 3,
                          compiler_params=_cparams(("parallel",)), name=name)(w, g, m, v)


SHARDED = (("w_mod", 1), ("w1_gu", 1), ("w1_down", 0), ("w_in", 1), ("dw_weight", 1), ("w_conv_out", 0),
           ("w_alpha_f", 1), ("w_alpha_b", 1), ("w_gla_out", 0), ("w_out", 0), ("w2_gu", 1), ("w2_down", 0))
REPLICATED = ("c_ctx", "b_mod", "g_ffn1", "g_mix", "dw_bias", "conv_ln_g", "conv_ln_b", "b_alpha_f", "b_alpha_b",
              "gla_norm_g", "g_ffn2", "g_final")
WEIGHTS = ("c_ctx", "w_mod", "b_mod", "g_ffn1", "w1_gu", "w1_down", "g_mix", "w_in", "dw_weight", "dw_bias",
           "conv_ln_g", "conv_ln_b", "w_conv_out", "w_alpha_f", "b_alpha_f", "w_alpha_b", "b_alpha_b", "gla_norm_g",
           "w_gla_out", "w_out", "g_ffn2", "w2_gu", "w2_down", "g_final")
MATRICES = ("w_mod", "w1_gu", "w1_down", "w_in", "w_conv_out", "w_gla_out", "w_out", "w2_gu", "w2_down")


def _pack_flat(parts, align):
    flat = jnp.concatenate([p.reshape(-1) for p in parts])
    pad = (-flat.shape[0]) % align
    return jnp.concatenate([flat, jnp.zeros((pad,), flat.dtype)]).reshape(-1, 1024)


def _unpack_flat(flat2d, shapes):
    flat = flat2d.reshape(-1)
    out, off = [], 0
    for s in shapes:
        n = math.prod(s)
        out.append(flat[off:off + n].reshape(s))
        off += n
    return out


def kernel(x, c, ctx, c_ctx, w_mod, b_mod, g_ffn1, w1_gu, w1_down, g_mix, w_in, dw_weight, dw_bias, conv_ln_g, conv_ln_b, w_conv_out, w_alpha_f, b_alpha_f, w_alpha_b, b_alpha_b, gla_norm_g, w_gla_out, w_out, g_ffn2, w2_gu, w2_down, g_final, loss_target, m_c_ctx, m_w_mod, m_b_mod, m_g_ffn1, m_w1_gu, m_w1_down, m_g_mix, m_w_in, m_dw_weight, m_dw_bias, m_conv_ln_g, m_conv_ln_b, m_w_conv_out, m_w_alpha_f, m_b_alpha_f, m_w_alpha_b, m_b_alpha_b, m_gla_norm_g, m_w_gla_out, m_w_out, m_g_ffn2, m_w2_gu, m_w2_down, m_g_final, v_c_ctx, v_w_mod, v_b_mod, v_g_ffn1, v_w1_gu, v_w1_down, v_g_mix, v_w_in, v_dw_weight, v_dw_bias, v_conv_ln_g, v_conv_ln_b, v_w_conv_out, v_w_alpha_f, v_b_alpha_f, v_w_alpha_b, v_b_alpha_b, v_gla_norm_g, v_w_gla_out, v_w_out, v_g_ffn2, v_w2_gu, v_w2_down, v_g_final):
    given = dict(locals())
    w = {n: given[n] for n in WEIGHTS}
    m = {n: given["m_" + n] for n in WEIGHTS}
    v = {n: given["v_" + n] for n in WEIGHTS}

    def bf16_shard(n):
        return w[n][0].astype(BF16)

    def install(wts, got):
        for n in ("w_mod", "w1_gu", "w2_gu"):
            if n in got:
                wts[n] = got[n]
        for n in ("w1_down", "w2_down", "w_conv_out", "w_gla_out", "w_out"):
            if n in got:
                wts[n] = got[n].reshape(-1, D)
        if "w_in" in got:
            wts["w_in_p"] = _perm_in_cols(jnp.concatenate([got["w_in"][j] for j in range(4)], axis=1))

    early = tuple(n for n in MATRICES if n not in _Overlap.LATE)
    shards =[bf16_shard(n) for n in early] + [_small_pack(w["dw_weight"][0], w["w_alpha_f"][0], w["w_alpha_b"][0])]
    got = dict(zip(early + ("small",), _gather_weights(shards)))
    wts = {n: w[n] for n in REPLICATED}
    install(wts, got)
    sm = got["small"]
    wts["dw_weight"] = jnp.concatenate([sm[j, :CONV_W] for j in range(4)], axis=1)
    zpad = jnp.zeros((128, HEADS * DK), BF16)
    w_af = jnp.concatenate([sm[j, 32:32 + LOWRANK, :DK] for j in range(4)], axis=1)
    w_ab = jnp.concatenate([sm[j, 32:32 + LOWRANK, DK:] for j in range(4)], axis=1)
    wts["w_alpha_f_pad"] = zpad.at[0:LOWRANK].set(w_af.astype(BF16))
    wts["w_alpha_b_pad"] = zpad.at[LOWRANK:2 * LOWRANK].set(w_ab.astype(BF16))

    place = jnp.stack([lax.axis_index("c"), 2 * lax.axis_index("x") + lax.axis_index("y")]).astype(jnp.int32)
    comm = _Overlap(bf16_shard, lambda got_late: install(wts, got_late), place)
    loss, grad_x, grads = _local_step(x, c, ctx, loss_target, wts, comm)
    loss = lax.psum(loss, ("x", "y", "c"))

    tags = MATRICES + ("small",)
    rest = tuple(n for n in tags if n not in comm.landed)
    rest_sums = _chip_sums("rest", rest, grads, place)
    for n, cs, r in zip(rest, rest_sums, _scatter_chips(rest_sums)):
        comm.landed[n] = (cs, r)
    halves = [_sum_chips("grad_sum_" + t, *comm.landed[t], place) for t in tags]
    reduced = dict(zip(tags, _join_halves(halves)))
    g_shard = {n: reduced[n] for n in MATRICES}
    g_shard["dw_weight"] = reduced["small"][:CONV_W]
    g_shard["w_alpha_f"] = reduced["small"][32:32 + LOWRANK, :DK]
    g_shard["w_alpha_b"] = reduced["small"][32:32 + LOWRANK, DK:]

    rep_shapes = [w[n].shape for n in REPLICATED]
    small = _allreduce_small(_pack_flat([grads[n].reshape(w[n].shape) for n in REPLICATED], 8 * 1024))
    g_rep = dict(zip(REPLICATED, _unpack_flat(small, rep_shapes)))

    g_out, d_out, m_out, v_out = {}, {}, {}, {}
    for n, _ in SHARDED:
        s2 = w[n].shape[1:]
        d, nm, nv = _adamw("adamw_" + n, w[n].reshape(s2), g_shard[n], m[n].reshape(s2), v[n].reshape(s2))
        g_out[n] = g_shard[n].reshape(w[n].shape)
        d_out[n], m_out[n], v_out[n] = d.reshape(w[n].shape), nm.reshape(w[n].shape), nv.reshape(w[n].shape)
    pk = lambda t: _pack_flat([t[n] for n in REPLICATED], 8 * 1024)
    d, nm, nv = _adamw("adamw_vectors", pk(w), small, pk(m), pk(v))
    for n, dd, mm, vv in zip(REPLICATED, _unpack_flat(d, rep_shapes), _unpack_flat(nm, rep_shapes),
                             _unpack_flat(nv, rep_shapes)):
        g_out[n], d_out[n], m_out[n], v_out[n] = g_rep[n], dd, mm, vv

    return (loss, grad_x, *[g_out[n] for n in WEIGHTS], *[d_out[n] for n in WEIGHTS],
            *[m_out[n] for n in WEIGHTS], *[v_out[n] for n in WEIGHTS])
```

```python
import functools
import math

import jax
import jax.numpy as jnp
from jax import lax
from jax.experimental import pallas as pl
from jax.experimental.pallas import tpu as pltpu

F32, BF16 = jnp.float32, jnp.bfloat16
MESH = pl.DeviceIdType.MESH
HIGHEST = lax.Precision.HIGHEST

D = 1024
FF = 2816
HEADS, DK, DV = 4, 128, 256
LOWRANK = 16
CONV_W = 31
CHUNK = 64
TAU = 16.0
EPS = 1e-6
Q_SCALE = DK ** -0.5
TM = 256
D_IN = 7200
D_INP = 7296
LR_COL = 7168
OG_CB, GA_CB, GB_CB = 6, 4, 5
VMEM_LIMIT = 52 * 1024 * 1024
WGRAD_VMEM = 40 * 1024 * 1024

ADAM_LR, ADAM_B1, ADAM_B2, ADAM_EPS, ADAM_WD, ADAM_STEP = 0.001, 0.9, 0.999, 1e-08, 0.01, 10


def _silu(x):
    return x * jax.nn.sigmoid(x)


def _rms(h, g):
    return h * lax.rsqrt(jnp.mean(h * h, axis=-1, keepdims=True) + EPS) * g


def _modnorm(x, g, shift, scale):
    return _rms(x, g) * (1 + scale) + shift


def _cparams(sem=None):
    return pltpu.CompilerParams(dimension_semantics=sem, vmem_limit_bytes=VMEM_LIMIT)


def _twice(v):
    return v, v


def tok(arr, width=None, cb=0, clamp=None):
    return ("tok", arr, arr.shape[1] if width is None else width, cb, clamp)


def mod(arr):
    return ("mod", arr)


def const(arr):
    return ("const", arr)


def _rowmap(name, body, n_tiles, ins, outs, *, tpb, nb):
    def modrow(i):
        return jnp.minimum(i // tpb, nb)

    in_specs, args = [], []
    for spec in ins:
        if spec[0] == "tok":
            _, arr, width, cb, clamp = spec
            if clamp is None:
                im = lambda i, cb=cb: (i, cb)
            else:
                im = lambda i, cb=cb, clamp=clamp: (jnp.minimum(i, clamp), cb)
            in_specs.append(pl.BlockSpec((TM, width), im))
        elif spec[0] == "mod":
            arr = spec[1]
            in_specs.append(pl.BlockSpec((1, 1, arr.shape[2]), lambda i: (modrow(i), 0, 0)))
        else:
            arr = spec[1]
            in_specs.append(pl.BlockSpec(arr.shape, lambda i, nd=arr.ndim: (0,) * nd))
        args.append(arr)
    out_specs, out_shapes, aliases = [], [], {}
    for o in outs:
        if o[0] == "tok":
            _, rows, width, dtype = o
            out_shapes.append(jax.ShapeDtypeStruct((rows, width), dtype))
            out_specs.append(pl.BlockSpec((TM, width), lambda i: (i, 0)))
        elif o[0] == "cols":
            buf, width, cb = o[1:4]
            first_tile = o[4] if len(o) > 4 else 0
            if not isinstance(buf, jax.ShapeDtypeStruct):
                aliases[len(args)] = len(out_shapes)
                in_specs.append(ANY)
                args.append(buf)
            out_shapes.append(jax.ShapeDtypeStruct(buf.shape, buf.dtype))
            out_specs.append(pl.BlockSpec((TM, width), lambda i, cb=cb, t0=first_tile: (i + t0, cb)))
        elif o[0] == "tok_head":
            _, rows, width, dtype, last = o
            out_shapes.append(jax.ShapeDtypeStruct((rows, width), dtype))
            out_specs.append(pl.BlockSpec((TM, width), lambda i, last=last: (jnp.minimum(i, last), 0)))
        elif o[0] == "tokT":
            _, rows, width, dtype = o
            out_shapes.append(jax.ShapeDtypeStruct((width, rows), dtype))
            out_specs.append(pl.BlockSpec((width, TM), lambda i: (0, i)))
        elif o[0] == "acc":
            _, rows, width = o
            out_shapes.append(jax.ShapeDtypeStruct((rows, width), F32))
            out_specs.append(pl.BlockSpec((rows, width), lambda i: (0, 0)))
        else:
            width = o[1]
            rows_visited = min((n_tiles - 1) // tpb, nb) + 1
            out_shapes.append(jax.ShapeDtypeStruct((rows_visited, 1, width), F32))
            out_specs.append(pl.BlockSpec((1, 1, width), lambda i: (modrow(i), 0, 0)))
    n_in = len(ins)

    def kern(*refs):
        i = pl.program_id(0)
        vals = []
        for r, spec in zip(refs[:n_in], ins):
            val = r[0] if spec[0] == "mod" else r[...]
            vals.append(val.astype(F32) if spec[0] == "tok" and val.dtype == BF16 else val)
        res = body(i, *vals)
        for r, o, val in zip(refs[len(args):], outs, res):
            if o[0] in ("tok", "cols"):
                r[...] = val.astype(r.dtype)
            elif o[0] == "tok_head":
                @pl.when(i <= o[4])
                def _():
                    r[...] = val.astype(r.dtype)
            elif o[0] == "tokT":
                r[...] = val.T.astype(r.dtype)
            elif o[0] == "acc":
                @pl.when(i == 0)
                def _():
                    r[...] = jnp.zeros(r.shape, F32)
                r[...] += jnp.broadcast_to(val, r.shape)
            else:
                first = jnp.logical_or(i == 0, modrow(i) != modrow(jnp.maximum(i - 1, 0)))

                @pl.when(first)
                def _():
                    r[...] = jnp.zeros(r.shape, F32)
                r[0] += val

    return pl.pallas_call(
        kern, grid=(n_tiles,), in_specs=in_specs, out_specs=out_specs, out_shape=out_shapes,
        input_output_aliases=aliases, compiler_params=_cparams(("arbitrary",)), name=name)(*args)


def _pick(n, cands):
    for c in cands:
        if n % c == 0:
            return c
    return n


def _pallas(kern, *, grid, in_specs, out_specs, out_shape, scratch_shapes, sem, name, args, side=None):
    if side is None:
        return pl.pallas_call(kern, grid=grid, in_specs=in_specs, out_specs=out_specs, out_shape=out_shape,
                              scratch_shapes=scratch_shapes, compiler_params=_cparams(sem), name=name)(*args)
    single = not isinstance(out_shape, (list, tuple))
    shapes = [out_shape] if single else list(out_shape)
    ospecs = [out_specs] if single else list(out_specs)
    n_in, n_out, n_scr = len(in_specs), len(shapes), len(scratch_shapes)
    s_in, s_out = list(side["ins"]), list(side["outs"])

    def wrapped(*refs):
        pos = [0]

        def take(n):
            pos[0] += n
            return refs[pos[0] - n:pos[0]]
        ins, sins, outs, souts, scr, sems = take(n_in), take(len(s_in)), take(n_out), take(len(s_out)), take(n_scr), take(2)
        ids = [pl.program_id(k) for k in range(len(grid))]
        first = functools.reduce(jnp.logical_and, [i == 0 for i in ids])
        last = functools.reduce(jnp.logical_and, [i == g - 1 for i, g in zip(ids, grid)])
        copy = _remote(*sems)

        @pl.when(first)
        def _():
            side["start"](sins, souts, copy)
        kern(*ins, *outs, *scr)

        @pl.when(last)
        def _():
            side["finish"](sins, souts, copy)

    res = pl.pallas_call(
        wrapped, grid=grid, in_specs=list(in_specs) + [ANY] * len(s_in), out_specs=ospecs + [ANY] * len(s_out),
        out_shape=shapes + s_out, scratch_shapes=list(scratch_shapes) + _sems(side["nsem"]),
        input_output_aliases={n_in + a: n_out + b for a, b in side.get("alias", {}).items()},
        compiler_params=_cparams(("arbitrary",) * len(grid)), name=name)(*args, *s_in)
    main = res[:n_out]
    return (main[0] if single else main), list(res[n_out:])


def _mm(name, a, b, *, trans_b=False, out_dtype=F32, a_fn=None, bias=None, rows=None, side=None, residual=None):
    m, k = a.shape if a.ndim == 2 else (a.shape[1], 2 * a.shape[2])
    m = m if rows is None else rows
    shard = b.shape[2] if b.ndim == 3 else None
    if trans_b:
        n = b.shape[-2]
        tk = _pick(shard, (2816, 2304, 1408, 1024)) if shard else (
            k if k <= 2816 else _pick(k, (2816, 2432, 2304, 2048, 1536, 1408, 1024, 512, 256, 128)))
        tn = _pick(n, (1024, 512, 384, 256, 128))
    else:
        n = 4 * shard if shard else b.shape[1]
        tk = k if k <= 2816 else _pick(k, (2816, 2432, 2304, 2048, 1536, 1408, 1024, 512, 256, 128))
        tn = _pick(shard, (512, 384, 1408, 256, 128)) if shard else _pick(n, (1024, 2432, 512, 384, 256, 128))
    tm = _pick(m, (512, 256) if residual is not None else (1024, 512, 256))
    nk = k // tk
    per = shard // (tk if trans_b else tn) if shard else None
    dims = (((1,), (1,)), ((), ())) if trans_b else (((1,), (0,)), ((), ()))

    def kern(*refs):
        a_ref, b_ref = refs[0], refs[1]
        bias_ref = refs[2] if bias is not None else None
        acc_ref = refs[-1]
        if residual is not None:
            x_ref, gate_ref, o_ref, xo_ref = refs[-5:-1]
        else:
            o_ref = refs[-2]
        kk = pl.program_id(2)
        av = a_ref[...]
        if a_fn is not None:
            av = a_fn(av)
        p = lax.dot_general(av.astype(BF16), b_ref[...].astype(BF16), dims, preferred_element_type=F32)

        def finish(total):
            if bias_ref is not None:
                total = total + bias_ref[...]
            o_ref[...] = total.astype(o_ref.dtype)
            if residual is not None:
                xo_ref[...] = x_ref[...] + residual["scale"] * gate_ref[0] * total

        if nk == 1:
            finish(p)
        else:
            @pl.when(kk == 0)
            def _():
                acc_ref[...] = p

            @pl.when(kk > 0)
            def _():
                acc_ref[...] += p

            @pl.when(kk == nk - 1)
            def _():
                finish(acc_ref[...])

    if shard and trans_b:
        b_spec = pl.BlockSpec((None, tn, tk), lambda i, j, kk: (kk // per, j, kk % per))
    elif shard:
        b_spec = pl.BlockSpec((None, tk, tn), lambda i, j, kk: (j // per, kk, j % per))
    elif trans_b:
        b_spec = pl.BlockSpec((tn, tk), lambda i, j, kk: (j, kk))
    else:
        b_spec = pl.BlockSpec((tk, tn), lambda i, j, kk: (kk, j))
    if a.ndim == 3:
        pa = a.shape[2] // tk
        a_spec = pl.BlockSpec((None, tm, tk), lambda i, j, kk: (kk // pa, i, kk % pa))
    else:
        a_spec = pl.BlockSpec((tm, tk), lambda i, j, kk: (i, kk))
    in_specs = [a_spec, b_spec]
    args = [a, b]
    if bias is not None:
        in_specs.append(pl.BlockSpec((1, tn), lambda i, j, kk: (0, j)))
        args.append(bias)
    out_specs = pl.BlockSpec((tm, tn), lambda i, j, kk: (i, j))
    out_shape = jax.ShapeDtypeStruct((m, n), out_dtype)
    if residual is not None:
        tiles, nb = residual["tpb"] * TM // tm, residual["gate"].shape[0] - 1
        in_specs += [out_specs, pl.BlockSpec((1, 1, tn), lambda i, j, kk: (jnp.minimum(i // tiles, nb), 0, j))]
        args += [residual["x"], residual["gate"]]
        out_specs, out_shape = [out_specs, out_specs], [out_shape, jax.ShapeDtypeStruct((m, n), F32)]
    return _pallas(
        kern, grid=(m // tm, n // tn, nk), in_specs=in_specs, out_specs=out_specs, out_shape=out_shape,
        scratch_shapes=[pltpu.VMEM((tm, tn) if nk > 1 else (8, 128), F32)],
        sem=("parallel", "parallel", "arbitrary"), name=name, args=args, side=side)


def _mm_tn(name, x, dy, rows=None, col_shards=False):
    t = x.shape[0] if rows is None else rows
    k1, n1 = x.shape[1], (dy.shape[1] if dy.ndim == 2 else 2 * dy.shape[2])
    tt = _pick(t, (512, 256, 128, 64, 8))
    tk1 = _pick(k1, (1024, 1408, 512, 256, 128))
    tn = _pick(n1 // 4, (512, 384, 1408, 256, 128)) if col_shards else _pick(n1, (512, 384, 256, 128))
    per = n1 // 4 // tn
    ns = t // tt

    def kern(x_ref, dy_ref, o_ref, acc_ref):
        s = pl.program_id(2)
        p = lax.dot_general(x_ref[...].astype(BF16), dy_ref[...].astype(BF16), (((0,), (0,)), ((), ())),
                            preferred_element_type=F32)

        @pl.when(s == 0)
        def _():
            acc_ref[...] = p

        @pl.when(s > 0)
        def _():
            acc_ref[...] += p

        @pl.when(s == ns - 1)
        def _():
            o_ref[...] = acc_ref[...].astype(o_ref.dtype)

    if col_shards:
        out_spec = pl.BlockSpec((None, tk1, tn), lambda i, j, s: (j // per, i, j % per))
        out_shape = jax.ShapeDtypeStruct((4, k1, n1 // 4), BF16)
    else:
        out_spec = pl.BlockSpec((tk1, tn), lambda i, j, s: (i, j))
        out_shape = jax.ShapeDtypeStruct((k1, n1), BF16)
    if dy.ndim == 3:
        pd = dy.shape[2] // tn
        dy_spec = pl.BlockSpec((None, tt, tn), lambda i, j, s: (j // pd, s, j % pd))
    else:
        dy_spec = pl.BlockSpec((tt, tn), lambda i, j, s: (s, j))
    return pl.pallas_call(
        kern, grid=(k1 // tk1, n1 // tn, ns),
        in_specs=[pl.BlockSpec((tt, tk1), lambda i, j, s: (s, i)), dy_spec],
        out_specs=out_spec, out_shape=out_shape, scratch_shapes=[pltpu.VMEM((tk1, tn), F32)],
        compiler_params=_cparams(("parallel", "parallel", "arbitrary")), name=name)(x, dy)


def _wgrad(name, xt, dy, rows=None, col_shards=False, side=None):
    k1 = xt.shape[0]
    t = xt.shape[1] if rows is None else rows
    n1 = dy.shape[1] if dy.ndim == 2 else 2 * dy.shape[2]
    tm = _pick(k1, (1024, 1408, 512, 256))
    tn = _pick(n1 // 4, (1408, 512, 384, 256, 128)) if col_shards else _pick(n1, (1024, 2432, 512, 384, 256, 128))
    fixed = tm * tn * (4 + 4 + 2 * 2)
    tk = next((c for c in (2048, 1536, 1024, 512, 256, 128)
               if t % c == 0 and fixed + 4 * c * (tm + tn) <= WGRAD_VMEM), 128)
    ns = t // tk
    per = n1 // 4 // tn

    def kern(x_ref, dy_ref, o_ref, acc_ref):
        s = pl.program_id(2)
        p = jnp.dot(x_ref[...], dy_ref[...], preferred_element_type=F32)

        @pl.when(s == 0)
        def _():
            acc_ref[...] = p

        @pl.when(s > 0)
        def _():
            acc_ref[...] += p

        @pl.when(s == ns - 1)
        def _():
            o_ref[...] = acc_ref[...].astype(o_ref.dtype)

    if dy.ndim == 3:
        pd = dy.shape[2] // tn
        dy_spec = pl.BlockSpec((None, tk, tn), lambda i, j, s: (j // pd, s, j % pd))
    else:
        dy_spec = pl.BlockSpec((tk, tn), lambda i, j, s: (s, j))
    if col_shards:
        out_spec = pl.BlockSpec((None, tm, tn), lambda i, j, s: (j // per, i, j % per))
        out_shape = jax.ShapeDtypeStruct((4, k1, n1 // 4), BF16)
    else:
        out_spec = pl.BlockSpec((tm, tn), lambda i, j, s: (i, j))
        out_shape = jax.ShapeDtypeStruct((k1, n1), BF16)
    return _pallas(
        kern, grid=(k1 // tm, n1 // tn, ns),
        in_specs=[pl.BlockSpec((tm, tk), lambda i, j, s: (i, s)), dy_spec],
        out_specs=out_spec, out_shape=out_shape, scratch_shapes=[pltpu.VMEM((tm, tn), F32)],
        sem=("parallel", "parallel", "arbitrary"), name=name, args=[xt, dy], side=side)


def _swiglu_fwd(name, u, w_gu, side=None):
    m = u.shape[0]
    half = w_gu.shape[2]
    tm = _pick(m, (512, 256))

    def kern(u_ref, wa_ref, wb_ref, ab_ref, hm_ref, hmt_ref):
        uv = u_ref[...]
        a = jnp.dot(uv, wa_ref[...], preferred_element_type=F32)
        b = jnp.dot(uv, wb_ref[...], preferred_element_type=F32)
        s = jax.nn.sigmoid(a)
        silu_a = a * s
        ab_ref[0] = (b * (s * (1 + a * (1 - s)))).astype(BF16)
        ab_ref[1] = silu_a.astype(BF16)
        hm = (silu_a * b).astype(BF16)
        hm_ref[...] = hm
        hmt_ref[...] = hm.T

    return _pallas(
        kern, grid=(m // tm, 2),
        in_specs=[pl.BlockSpec((tm, D), lambda i, j: (i, 0)),
                  pl.BlockSpec((None, D, half), lambda i, j: (j, 0, 0)),
                  pl.BlockSpec((None, D, half), lambda i, j: (2 + j, 0, 0))],
        out_specs=[pl.BlockSpec((2, tm, half), lambda i, j: (0, i, j)),
                   pl.BlockSpec((tm, half), lambda i, j: (i, j)),
                   pl.BlockSpec((half, tm), lambda i, j: (j, i))],
        out_shape=[jax.ShapeDtypeStruct((2, m, FF), BF16), jax.ShapeDtypeStruct((m, FF), BF16),
                   jax.ShapeDtypeStruct((FF, m), BF16)],
        scratch_shapes=[], sem=("parallel", "parallel"), name=name, args=[u, w_gu, w_gu], side=side)


def _swiglu_bwd(name, df, w_down, ab):
    m = df.shape[0]
    half = FF // 2
    tm = _pick(m, (512, 256))

    def kern(df_ref, w_ref, ab_ref, o_ref):
        dh = lax.dot_general(df_ref[...], w_ref[...], (((1,), (1,)), ((), ())), preferred_element_type=F32)
        o_ref[0] = (dh * ab_ref[0].astype(F32)).astype(BF16)
        o_ref[1] = (dh * ab_ref[1].astype(F32)).astype(BF16)

    return pl.pallas_call(
        kern, grid=(m // tm, 2),
        in_specs=[pl.BlockSpec((tm, D), lambda i, j: (i, 0)),
                  pl.BlockSpec((half, D), lambda i, j: (j, 0)),
                  pl.BlockSpec((2, tm, half), lambda i, j: (0, i, j))],
        out_specs=pl.BlockSpec((2, tm, half), lambda i, j: (0, i, j)),
        out_shape=jax.ShapeDtypeStruct((2, m, FF), BF16),
        compiler_params=_cparams(("parallel", "parallel")), name=name)(df, w_down, ab)


def _gla_maps(bl, t, tc):
    nx, nc = t // CHUNK, tc // CHUNK
    nxb = bl * nx

    def rowblk(d, b, n):
        c_ctx = jnp.where(d == 0, n, nc - 1 - n)
        c_x = jnp.where(d == 0, n - nc, nx - 1 - (n - nc))
        return jnp.where(n < nc, nxb + b * nc + c_ctx, b * nx + c_x)

    def xblk(d, b, n):
        n2 = jnp.maximum(n, nc)
        return b * nx + jnp.where(d == 0, n2 - nc, nx - 1 - (n2 - nc))

    return nx, nc, rowblk, xblk


def _dot01(m, x, cm):
    x1 = x.astype(BF16)
    r1 = x - x1.astype(F32)
    x2 = r1.astype(BF16)
    x3 = (r1 - x2.astype(F32)).astype(BF16)
    w = x.shape[1]
    p = lax.dot_general(m.astype(BF16), jnp.concatenate([x1, x2, x3], axis=1), (((cm,), (0,)), ((), ())),
                        preferred_element_type=F32)
    return p[:, :w] + p[:, w:2 * w] + p[:, 2 * w:]


def _gla_chunk(m, q, k, v, g, h):
    gh = g[:, h * DK:(h + 1) * DK]
    b = _dot01(m, gh, 1)
    tot = jnp.sum(gh, axis=0, keepdims=True)
    mid = b[CHUNK // 2:CHUNK // 2 + 1, :]
    qh = q[:, h * DK:(h + 1) * DK] * Q_SCALE
    kh = k[:, h * DK:(h + 1) * DK]
    vh = v[:, h * DV:(h + 1) * DV]
    return b, tot, mid, qh, kh, vh


def _dot(a, b, ca, cb):
    return lax.dot_general(a.astype(BF16), b.astype(BF16), (((ca,), (cb,)), ((), ())),
                           preferred_element_type=F32)


def _gla_fwd(p, g2, mmats, bl, t, tc):
    nx, nc, rowblk, xblk = _gla_maps(bl, t, tc)
    ns = nx + nc

    def kern(q0, k0, v0, g0, q1, k1, v1, g1, m_ref, o0, o1, ss_ref, s_ref):
        n = pl.program_id(1)

        @pl.when(n == 0)
        def _():
            s_ref[...] = jnp.zeros(s_ref.shape, F32)
        sides = ((q0, k0, v0, g0, o0), (q1, k1, v1, g1, o1))
        loaded = [(m_ref[d], q_ref[...].astype(F32), k_ref[...].astype(F32), v_ref[...].astype(F32), g_ref[...])
                  for d, (q_ref, k_ref, v_ref, g_ref, _) in enumerate(sides)]
        chains = [(d, h) for d in range(2) for h in range(HEADS)]
        base = [_gla_chunk(*loaded[d], h) for d, h in chains]
        pre = []
        for (d, h), (b, tot, mid, qh, kh, vh) in zip(chains, base):
            s0 = s_ref[d, h * DV:(h + 1) * DV, :]
            ss_ref[d, 0, 0, h * DV:(h + 1) * DV, :] = s0.astype(BF16)
            pre.append((s0, kh * jnp.exp(tot - b), qh * jnp.exp(b), qh * jnp.exp(b - mid), kh * jnp.exp(mid - b)))
        raw = [(_dot(qm, km, 1, 1), _dot(qe, s0, 1, 1), _dot(bs[5], kl, 0, 0))
               for bs, (s0, kl, qe, qm, km) in zip(base, pre)]
        for (d, h), bs, (s0, kl, qe, qm, km), (att_raw, inter, s_add) in zip(chains, base, pre, raw):
            s_ref[d, h * DV:(h + 1) * DV, :] = s0 * jnp.exp(bs[1]) + s_add
            sides[d][4][:, h * DV:(h + 1) * DV] = inter + _dot(loaded[d][0] * att_raw, bs[5], 1, 0)

    def operands(d):
        return [pl.BlockSpec((CHUNK, 512), lambda b, n: (rowblk(d, b, n), 4)),
                pl.BlockSpec((CHUNK, 512), lambda b, n: (rowblk(d, b, n), 5)),
                pl.BlockSpec((CHUNK, 1024), lambda b, n: (rowblk(d, b, n), 3)),
                pl.BlockSpec((CHUNK, 512), lambda b, n: (rowblk(d, b, n), d))]

    o_shape = jax.ShapeDtypeStruct((bl * t, HEADS * DV), F32)
    return pl.pallas_call(
        kern, grid=(bl, ns),
        in_specs=operands(0) + operands(1) + [pl.BlockSpec((2, CHUNK, CHUNK), lambda b, n: (0, 0, 0))],
        out_specs=[pl.BlockSpec((CHUNK, 1024), lambda b, n: (xblk(0, b, n), 0)),
                   pl.BlockSpec((CHUNK, 1024), lambda b, n: (xblk(1, b, n), 0)),
                   pl.BlockSpec((2, 1, 1, HEADS * DV, DK), lambda b, n: (0, b, n, 0, 0))],
        out_shape=[o_shape, o_shape, jax.ShapeDtypeStruct((2, bl, ns, HEADS * DV, DK), BF16)],
        scratch_shapes=[pltpu.VMEM((2, HEADS * DV, DK), F32)],
        compiler_params=_cparams(("parallel", "arbitrary")), name="gla_fwd")(p, p, p, g2, p, p, p, g2, mmats)


def _gla_bwd(p, g2, mmats, ssave, do, bl, t, tc):
    nx, nc, rowblk, xblk = _gla_maps(bl, t, tc)
    ns = nx + nc
    rev = lambda s: ns - 1 - s

    def kern(q0, k0, v0, g0, do0, q1, k1, v1, g1, do1, m_ref, ss_ref,
             dq0, dk0, dv0, dg0, dq1, dk1, dv1, dg1, ds_ref):
        step = pl.program_id(1)
        n = ns - 1 - step

        @pl.when(step == 0)
        def _():
            ds_ref[...] = jnp.zeros(ds_ref.shape, F32)
        live = (n >= nc).astype(F32)
        sides = ((q0, k0, v0, g0, do0, dq0, dk0, dv0, dg0), (q1, k1, v1, g1, do1, dq1, dk1, dv1, dg1))
        loaded = [(m_ref[d], s[0][...].astype(F32), s[1][...].astype(F32), s[2][...].astype(F32), s[3][...])
                  for d, s in enumerate(sides)]
        dovs = [s[4][...] * live for s in sides]
        chains = [(d, h) for d in range(2) for h in range(HEADS)]
        base = [_gla_chunk(*loaded[d], h) for d, h in chains]
        pre = []
        for (d, h), (b, tot, mid, qh, kh, vh) in zip(chains, base):
            eb, ebm, emb, etb = jnp.exp(b), jnp.exp(b - mid), jnp.exp(mid - b), jnp.exp(tot - b)
            pre.append(dict(
                eb=eb, ebm=ebm, emb=emb, etb=etb, etot=jnp.exp(tot), qe=qh * eb, qm=qh * ebm, km=kh * emb, kl=kh * etb,
                vh=vh, doh=dovs[d][:, h * DV:(h + 1) * DV], s0=ss_ref[d, 0, 0, h * DV:(h + 1) * DV, :].astype(F32),
                ds1=ds_ref[d, h * DV:(h + 1) * DV, :]))
        first = [dict(att=_dot(c["qm"], c["km"], 1, 1), datt=_dot(c["doh"], c["vh"], 1, 1),
                      dqe=_dot(c["doh"], c["s0"], 1, 0), ds_add=_dot(c["doh"], c["qe"], 0, 0),
                      dkl=_dot(c["vh"], c["ds1"], 1, 0), dv_s=_dot(c["kl"], c["ds1"], 1, 1)) for c in pre]
        second = []
        for (d, h), c, f in zip(chains, pre, first):
            m = loaded[d][0]
            ds_ref[d, h * DV:(h + 1) * DV, :] = c["ds1"] * c["etot"] + f["ds_add"]
            att, datt = m * f["att"], m * f["datt"]
            second.append(dict(dqm=_dot(datt, c["km"], 1, 0), dkm=_dot(datt, c["qm"], 0, 0),
                               dv_a=_dot(att, c["doh"], 0, 0)))
        for (d, h), c, f, s in zip(chains, pre, first, second):
            dq_ref, dk_ref, dv_ref, dg_ref = sides[d][5:]
            dtot = c["etot"] * jnp.sum(c["ds1"] * c["s0"], axis=0, keepdims=True) + jnp.sum(
                f["dkl"] * c["kl"], axis=0, keepdims=True)
            db = f["dqe"] * c["qe"] + s["dqm"] * c["qm"] - s["dkm"] * c["km"] - f["dkl"] * c["kl"]
            dq_ref[:, h * DK:(h + 1) * DK] = ((f["dqe"] * c["eb"] + s["dqm"] * c["ebm"]) * Q_SCALE).astype(BF16)
            dk_ref[:, h * DK:(h + 1) * DK] = (s["dkm"] * c["emb"] + f["dkl"] * c["etb"]).astype(BF16)
            dv_ref[:, h * DV:(h + 1) * DV] = (s["dv_a"] + f["dv_s"]).astype(BF16)
            dg_ref[:, h * DK:(h + 1) * DK] = _dot01(loaded[d][0], db, 0) + dtot

    nt = p.shape[0]

    def operands(d):
        return [pl.BlockSpec((CHUNK, 512), lambda b, s: (rowblk(d, b, rev(s)), 4)),
                pl.BlockSpec((CHUNK, 512), lambda b, s: (rowblk(d, b, rev(s)), 5)),
                pl.BlockSpec((CHUNK, 1024), lambda b, s: (rowblk(d, b, rev(s)), 3)),
                pl.BlockSpec((CHUNK, 512), lambda b, s: (rowblk(d, b, rev(s)), d)),
                pl.BlockSpec((CHUNK, 1024), lambda b, s: (xblk(d, b, rev(s)), 0))]

    def results(d):
        row = lambda b, s: (rowblk(d, b, rev(s)), 0)
        return [pl.BlockSpec((CHUNK, 512), row), pl.BlockSpec((CHUNK, 512), row), pl.BlockSpec((CHUNK, 1024), row),
                pl.BlockSpec((CHUNK, 512), row)]

    shapes = [jax.ShapeDtypeStruct((nt, 512), BF16), jax.ShapeDtypeStruct((nt, 512), BF16),
              jax.ShapeDtypeStruct((nt, 1024), BF16), jax.ShapeDtypeStruct((nt, 512), F32)]
    out = pl.pallas_call(
        kern, grid=(bl, ns),
        in_specs=operands(0) + operands(1) + [
            pl.BlockSpec((2, CHUNK, CHUNK), lambda b, s: (0, 0, 0)),
            pl.BlockSpec((2, 1, 1, HEADS * DV, DK), lambda b, s: (0, b, rev(s), 0, 0))],
        out_specs=results(0) + results(1), out_shape=shapes + shapes,
        scratch_shapes=[pltpu.VMEM((2, HEADS * DV, DK), F32)],
        compiler_params=_cparams(("parallel", "arbitrary")), name="gla_bwd")(
            p, p, p, g2, do, p, p, p, g2, do, mmats, ssave)
    return out[:4], out[4:]


CONV_CT = 256
CONV_PAD = 16
CONV_RC = 128
CONV_HALO = 24


def _conv_fill(zp, z_ref, t):
    zp[0:CONV_PAD, :] = jnp.zeros((CONV_PAD, CONV_CT), F32)
    zp[CONV_PAD + t:2 * CONV_PAD + t, :] = jnp.zeros((CONV_PAD, CONV_CT), F32)
    zp[CONV_PAD:CONV_PAD + t, :] = z_ref[...]


def _dwconv(name, z, w, bias, bl, t, flip):
    def kern(z_ref, w_ref, b_ref, o_ref, zp):
        _conv_fill(zp, z_ref, t)
        offs = [(CONV_W - j) if flip else (j + 1) for j in range(CONV_W)]
        for r in range(0, t, CONV_RC):
            acc = jnp.broadcast_to(b_ref[...], (CONV_RC, CONV_CT))
            for rot in range(8):
                win = zp[r + rot:r + rot + CONV_RC + CONV_HALO, :]
                for j in range(CONV_W):
                    if offs[j] % 8 == rot:
                        a = offs[j] - rot
                        acc = acc + w_ref[j:j + 1, :] * win[a:a + CONV_RC, :]
            o_ref[r:r + CONV_RC, :] = acc

    return pl.pallas_call(
        kern, grid=(bl, 1024 // CONV_CT),
        in_specs=[pl.BlockSpec((t, CONV_CT), lambda b, c: (b, c)),
                  pl.BlockSpec((32, CONV_CT), lambda b, c: (0, c)),
                  pl.BlockSpec((1, CONV_CT), lambda b, c: (0, c))],
        out_specs=pl.BlockSpec((t, CONV_CT), lambda b, c: (b, c)),
        out_shape=jax.ShapeDtypeStruct(z.shape, F32),
        scratch_shapes=[pltpu.VMEM((t + 2 * CONV_PAD, CONV_CT), F32)],
        compiler_params=_cparams(("parallel", "parallel")), name=name)(z, w, bias)


def _dwconv_wgrad(z, dzc, bl, t):
    def kern(z_ref, d_ref, dw_ref, db_ref, zp):
        b = pl.program_id(1)

        @pl.when(b == 0)
        def _():
            dw_ref[...] = jnp.zeros(dw_ref.shape, F32)
            db_ref[...] = jnp.zeros(db_ref.shape, F32)
        _conv_fill(zp, z_ref, t)
        for rot in range(8):
            taps = [j for j in range(CONV_W) if (j + 1) % 8 == rot]
            accs = [jnp.zeros((8, CONV_CT), F32) for _ in taps]
            for r in range(0, t, CONV_RC):
                d = d_ref[r:r + CONV_RC, :]
                win = zp[r + rot:r + rot + CONV_RC + CONV_HALO, :]
                for k, j in enumerate(taps):
                    a = j + 1 - rot
                    prod = d * win[a:a + CONV_RC, :]
                    accs[k] = accs[k] + jnp.sum(prod.reshape(CONV_RC // 8, 8, CONV_CT), axis=0)
            for k, j in enumerate(taps):
                dw_ref[j:j + 1, :] += jnp.sum(accs[k], axis=0, keepdims=True)
        db_ref[...] += jnp.sum(d_ref[...], axis=0, keepdims=True)

    return pl.pallas_call(
        kern, grid=(1024 // CONV_CT, bl),
        in_specs=[pl.BlockSpec((t, CONV_CT), lambda c, b: (b, c)),
                  pl.BlockSpec((t, CONV_CT), lambda c, b: (b, c))],
        out_specs=[pl.BlockSpec((32, CONV_CT), lambda c, b: (0, c)),
                   pl.BlockSpec((1, CONV_CT), lambda c, b: (0, c))],
        out_shape=[jax.ShapeDtypeStruct((32, 1024), F32), jax.ShapeDtypeStruct((1, 1024), F32)],
        scratch_shapes=[pltpu.VMEM((t + 2 * CONV_PAD, CONV_CT), F32)],
        compiler_params=_cparams(("parallel", "arbitrary")), name="dwconv_wgrad")(z, dzc)


def _ffn_fwd(tag, xin, n_tiles, g, sh, sc, gate, w_gu, w_down, tpb, nb, comm=None):
    rows = n_tiles * TM
    rm = functools.partial(_rowmap, tpb=tpb, nb=nb)
    u, ut = rm(tag + "_norm", lambda i, x, g_, sh_, sc_: _twice(_modnorm(x, g_, sh_, sc_)), n_tiles,
               [tok(xin), const(g), mod(sh), mod(sc)], [("tok", rows, D, BF16), ("tokT", rows, D, BF16)])
    ab, hm, hmt = _with_side(comm, tag + "_gu", None, lambda s: _swiglu_fwd(tag + "_gu", u, w_gu, side=s))
    res = dict(x=xin, gate=gate, scale=0.5, tpb=tpb)
    f, xout = _with_side(comm, tag + "_down", None,
                         lambda s: _mm(tag + "_down", hm, w_down, out_dtype=BF16, side=s, residual=res))
    return xout, (ut, ab, hmt, f)


def _ffn_bwd(tag, xin, saved, dxout, dx_clamp, n_tiles, g, sh, sc, gate, w_gu, w_down, tpb, nb, comm=None, grads=None,
             names=None, keep_tiles=None):
    ut, ab, hmt, f = saved
    rows = n_tiles * TM
    rm = functools.partial(_rowmap, tpb=tpb, nb=nb)

    def mask(i):
        return 1.0 if dx_clamp is None else (i <= dx_clamp).astype(F32)

    def b1(i, dx, f_, gt):
        dx = dx * mask(i)
        return (0.5 * gt * dx, jnp.sum(0.5 * f_ * dx, axis=0, keepdims=True))
    df, dgate = rm(tag + "_bres", b1, n_tiles, [tok(dxout, clamp=dx_clamp), tok(f), mod(gate)],
                   [("tok", rows, D, BF16), ("modacc", D)])
    grads[names[1]] = _wgrad(tag + "_wdown", hmt, df)
    dab = _swiglu_bwd(tag + "_bdown", df, w_down, ab)
    grads[names[0]] = _with_side(comm, tag + "_wgu", grads,
                                 lambda s: _wgrad(tag + "_wgu", ut, dab, col_shards=True, side=s))
    du = _with_side(comm, tag + "_bgu", grads, lambda s: _mm(tag + "_bgu", dab, w_gu, trans_b=True, side=s))

    def b3(i, x, g_, sh_, sc_, du_, dx):
        _, vjp = jax.vjp(_modnorm, x, g_, sh_, sc_)
        dxn, dg, dsh, dsc = vjp(du_)
        return (dx * mask(i) + dxn, dg, dsh, dsc)
    dx_out = ("tok", rows, D, F32) if keep_tiles is None else ("tok_head", keep_tiles * TM, D, F32, keep_tiles - 1)
    dxin, dg, dsh, dsc = rm(tag + "_bnorm", b3, n_tiles,
                            [tok(xin), const(g), mod(sh), mod(sc), tok(du), tok(dxout, clamp=dx_clamp)],
                            [dx_out, ("acc", 1, D), ("modacc", D), ("modacc", D)])
    return dxin, dict(g=dg, sh=dsh, sc=dsc, gate=dgate)


def _perm_in_cols(w):
    pad = jnp.zeros(w.shape[:-1] + (D_INP - D_IN,), w.dtype)
    return jnp.concatenate([w[..., :4096], w[..., 5152:7200], w[..., 4096:5120], w[..., 5120:5152], pad], axis=-1)


def _unperm_in_cols(w):
    return jnp.concatenate([w[..., :4096], w[..., 6144:LR_COL], w[..., LR_COL:LR_COL + 32], w[..., 4096:6144]], axis=-1)


def _local_step(x, c, ctx, target, wts, comm=None):
    bl, t, _ = x.shape
    tc = ctx.shape[1]
    nx_rows, nc_rows = bl * t, bl * tc
    nt_rows = nx_rows + nc_rows
    tpb = t // TM
    nxt, ntt = nx_rows // TM, nt_rows // TM
    nb = bl
    rm = functools.partial(_rowmap, tpb=tpb, nb=nb)
    last_x = nxt - 1

    x0 = jnp.concatenate([x.reshape(nx_rows, D), ctx.reshape(nc_rows, D)], axis=0)
    tgt = target.reshape(nx_rows, D)

    cc = jnp.concatenate([c, wts["c_ctx"].reshape(1, D), jnp.zeros((8 - bl - 1, D), F32)], axis=0)
    modv = _mm("mod_fwd", cc, wts["w_mod"], a_fn=_silu, bias=wts["b_mod"])
    mods = [modv[:nb + 1, k * D:(k + 1) * D].reshape(nb + 1, 1, D) for k in range(9)]

    x1, sv1 = _ffn_fwd("ffn1", x0, ntt, wts["g_ffn1"], mods[0], mods[1], mods[2], wts["w1_gu"], wts["w1_down"],
                       tpb, nb, comm)
    u2, u2t = rm("in_norm", lambda i, x_, g_, sh_, sc_: _twice(_modnorm(x_, g_, sh_, sc_)), ntt,
                 [tok(x1), const(wts["g_mix"]), mod(mods[3]), mod(mods[4])],
                 [("tok", nt_rows, D, BF16), ("tokT", nt_rows, D, BF16)])
    w_inp = wts["w_in_p"]
    p = _with_side(comm, "in_proj", None, lambda s: _mm("in_proj", u2, w_inp, out_dtype=BF16, side=s))

    waf, wab, baf, bab = wts["w_alpha_f_pad"], wts["w_alpha_b_pad"], wts["b_alpha_f"], wts["b_alpha_b"]

    def dec_fwd(i, lr, wf, wb, bf_, bb_):
        zf = _dot(lr, wf, 1, 0) + bf_
        zb = _dot(lr, wb, 1, 0) + bb_
        return (jnp.concatenate([jax.nn.log_sigmoid(zf) / TAU, jax.nn.log_sigmoid(zb) / TAU], axis=1),)
    (gfb,) = rm("decay_fwd", dec_fwd, ntt, [tok(p, 128, LR_COL // 128), const(waf), const(wab), const(baf), const(bab)],
                [("tok", nt_rows, 1024, F32)])
    g2 = gfb
    tri = jnp.tril(jnp.ones((CHUNK, CHUNK), F32))
    mmats = jnp.stack([tri, tri.T])
    *o2, ssave = _gla_fwd(p, g2, mmats, bl, t, tc)

    gn_g = wts["gla_norm_g"]

    def gla_out(of, ob, og, gn):
        o = of + ob
        parts = []
        for h in range(HEADS):
            oh = o[:, h * DV:(h + 1) * DV]
            parts.append(oh * lax.rsqrt(jnp.mean(oh * oh, axis=-1, keepdims=True) + EPS))
        return jnp.concatenate(parts, axis=1) * gn * _silu(og)
    yg_in, yg_int = rm("gla_out", lambda i, of, ob, og, gn: _twice(gla_out(of, ob, og, gn)), nxt,
                       [tok(o2[0]), tok(o2[1]), tok(p, 1024, OG_CB), const(gn_g)],
                       [("tok", nx_rows, D, BF16), ("tokT", nx_rows, D, BF16)])
    y_gla = _with_side(comm, "gla_proj", None,
                       lambda s: _mm("gla_proj", yg_in, wts["w_gla_out"], out_dtype=BF16, side=s))

    (z,) = rm("glu", lambda i, a, b: (a * jax.nn.sigmoid(b),), nxt, [tok(p, 1024, 0), tok(p, 1024, 1)],
              [("tok", nx_rows, D, F32)])
    dw_w = jnp.concatenate([wts["dw_weight"], jnp.zeros((1, D), F32)], axis=0)
    zc = _dwconv("dwconv_fwd", z, dw_w, wts["dw_bias"], bl, t, False)

    def ln_silu(zc_, g_, b_):
        mu = jnp.mean(zc_, axis=-1, keepdims=True)
        var = jnp.mean(jnp.square(zc_ - mu), axis=-1, keepdims=True)
        return _silu((zc_ - mu) * lax.rsqrt(var + EPS) * g_ + b_)
    ln_g, ln_b = wts["conv_ln_g"], wts["conv_ln_b"]
    zl, zlt = rm("conv_ln", lambda i, zc_, g_, b_: _twice(ln_silu(zc_, g_, b_)), nxt,
                 [tok(zc), const(ln_g), const(ln_b)], [("tok", nx_rows, D, BF16), ("tokT", nx_rows, D, BF16)])
    y_conv = _mm("conv_proj", zl, wts["w_conv_out"], out_dtype=BF16)

    mg, mgt = rm("merge", lambda i, ga, gb, yc, yg: _twice(jax.nn.sigmoid(ga) * yc + jax.nn.sigmoid(gb) * yg), nxt,
                 [tok(p, 1024, GA_CB), tok(p, 1024, GB_CB), tok(y_conv), tok(y_gla)],
                 [("tok", nx_rows, D, BF16), ("tokT", nx_rows, D, BF16)])
    mix, x2 = _mm("out_proj", mg, wts["w_out"], out_dtype=BF16,
                  residual=dict(x=x1, gate=mods[5], scale=1.0, tpb=tpb))

    x3, sv2 = _ffn_fwd("ffn2", x2, nxt, wts["g_ffn2"], mods[6], mods[7], mods[8], wts["w2_gu"], wts["w2_down"],
                       tpb, nb)
    g_fin = wts["g_final"].reshape(1, D)

    def head(i, x_, g_, tg):
        y, vjp = jax.vjp(_rms, x_, g_)
        diff = y - tg
        dx, dg = vjp(diff * (1.0 / D))
        loss = 0.5 * jnp.sum(jnp.mean(diff * diff, axis=-1, keepdims=True))
        return dx, dg, loss
    dx3, dg_final, loss_acc = rm("loss_head", head, nxt, [tok(x3), const(g_fin), tok(tgt)],
                                 [("tok", nx_rows, D, F32), ("acc", 1, D), ("acc", 8, 128)])
    loss = loss_acc[0, 0]

    grads = {}
    dx2, gf2 = _ffn_bwd("ffn2", x2, sv2, dx3, None, nxt, wts["g_ffn2"], mods[6], mods[7], mods[8],
                        wts["w2_gu"], wts["w2_down"], tpb, nb, comm, grads, ("w2_gu", "w2_down"))
    grads["g_ffn2"] = gf2["g"]

    dmix, dgate5 = rm("mix_bres", lambda i, dx, mx_, gt: (gt * dx, jnp.sum(mx_ * dx, axis=0, keepdims=True)), nxt,
                      [tok(dx2), tok(mix), mod(mods[5])], [("tok", nx_rows, D, BF16), ("modacc", D)])
    dmg = _mm("out_bproj", dmix, wts["w_out"], trans_b=True)
    grads["w_out"] = _wgrad("out_wgrad", mgt, dmix)

    def merge_bwd(i, dm, ga, gb, yc, yg):
        keep = (i <= last_x).astype(F32)
        dm = dm * keep
        sa, sb = jax.nn.sigmoid(ga), jax.nn.sigmoid(gb)
        return dm * sa, dm * sb, jnp.concatenate([dm * yc * sa * (1 - sa), dm * yg * sb * (1 - sb)], axis=1)
    cl = dict(clamp=last_x)
    dyc, dyg, dp = rm("merge_bwd", merge_bwd, ntt,
                      [tok(dmg, **cl), tok(p, 1024, GA_CB, last_x), tok(p, 1024, GB_CB, last_x), tok(y_conv, **cl),
                       tok(y_gla, **cl)],
                      [("tok", nt_rows, D, BF16), ("tok", nt_rows, D, BF16),
                       ("cols", jax.ShapeDtypeStruct((nt_rows, D_INP), BF16), 2048, 2)])

    dzl = _mm("conv_bproj", dyc, wts["w_conv_out"], trans_b=True, rows=nx_rows)
    grads["w_conv_out"] = _wgrad("conv_wgrad", zlt, dyc, rows=nx_rows)

    def ln_bwd(i, zc_, g_, b_, dz_):
        _, vjp = jax.vjp(ln_silu, zc_, g_, b_)
        return vjp(dz_)
    dzc, dln_g, dln_b = rm("conv_ln_bwd", ln_bwd, nxt, [tok(zc), const(ln_g), const(ln_b), tok(dzl)],
                           [("tok", nx_rows, D, F32), ("acc", 1, D), ("acc", 1, D)])
    dz = _dwconv("dwconv_bwd", dzc, dw_w, jnp.zeros((1, D), F32), bl, t, True)
    ddw, ddb = _dwconv_wgrad(z, dzc, bl, t)
    grads.update(conv_ln_g=dln_g, conv_ln_b=dln_b, dw_weight=ddw[:CONV_W], dw_bias=ddb)

    def glu_bwd(i, dz_, a, b):
        keep = (i <= last_x).astype(F32)
        dz_ = dz_ * keep
        s = jax.nn.sigmoid(b)
        return (jnp.concatenate([dz_ * s, dz_ * a * s * (1 - s)], axis=1),)
    (dp,) = rm("glu_bwd", glu_bwd, ntt, [tok(dz, **cl), tok(p, 1024, 0, last_x), tok(p, 1024, 1, last_x)],
               [("cols", dp, 2048, 0)])

    dyg_in = _mm("gla_bproj", dyg, wts["w_gla_out"], trans_b=True, rows=nx_rows)
    grads["w_gla_out"] = _wgrad("gla_wgrad", yg_int, dyg, rows=nx_rows)

    def gla_out_bwd(i, of, ob, og, gn, dy):
        _, vjp = jax.vjp(gla_out, of, ob, og, gn)
        do_, _, dog_, dgn_ = vjp(dy)
        return do_, dog_, dgn_
    do, dp, dgn = rm("gla_out_bwd", gla_out_bwd, nxt,
                     [tok(o2[0]), tok(o2[1]), tok(p, 1024, OG_CB), const(gn_g), tok(dyg_in)],
                     [("tok", nx_rows, D, F32), ("cols", dp, 1024, OG_CB), ("acc", 1, D)])
    grads["gla_norm_g"] = dgn
    (dp,) = rm("og_ctx_zero", lambda i: (jnp.zeros((TM, D), F32),), ntt - nxt, [], [("cols", dp, 1024, OG_CB, nxt)])

    dq2, dk2, dv2, dg2 = zip(*_gla_bwd(p, g2, mmats, ssave, do, bl, t, tc))

    def dec_bwd(i, lr, wf, wb, bf_, bb_, dgf, dgb_):
        zf = _dot(lr, wf, 1, 0) + bf_
        zb = _dot(lr, wb, 1, 0) + bb_
        dzf = dgf * (1 - jax.nn.sigmoid(zf)) * (1.0 / TAU)
        dzb = dgb_ * (1 - jax.nn.sigmoid(zb)) * (1.0 / TAU)
        dlr = _dot(dzf, wf, 1, 1) + _dot(dzb, wb, 1, 1)
        return (dlr, _dot(lr, dzf, 0, 0), _dot(lr, dzb, 0, 0), jnp.sum(dzf, axis=0, keepdims=True),
                jnp.sum(dzb, axis=0, keepdims=True))
    dp, dwaf, dwab, dbaf, dbab = rm(
        "decay_bwd", dec_bwd, ntt,
        [tok(p, 128, LR_COL // 128), const(waf), const(wab), const(baf), const(bab), tok(dg2[0]), tok(dg2[1])],
        [("cols", dp, 128, LR_COL // 128), ("acc", 128, 512), ("acc", 128, 512), ("acc", 1, 512), ("acc", 1, 512)])
    grads.update(w_alpha_f=dwaf[:LOWRANK], w_alpha_b=dwab[LOWRANK:2 * LOWRANK], b_alpha_f=dbaf, b_alpha_b=dbab)

    (dp,) = rm("gla_sum",
               lambda i, q0, q1, k0, k1, v0, v1: (jnp.concatenate([q0 + q1, k0 + k1, v0 + v1], axis=1),), ntt,
               [tok(dq2[0]), tok(dq2[1]), tok(dk2[0]), tok(dk2[1]), tok(dv2[0]), tok(dv2[1])],
               [("cols", dp, 2048, 1)])
    du2 = _with_side(comm, "in_bproj", grads, lambda s: _mm("in_bproj", dp, w_inp, trans_b=True, side=s))
    grads["w_in_p"] = _wgrad("in_wgrad", u2t, dp)

    def in_norm_bwd(i, x_, g_, sh_, sc_, du_, dx):
        keep = (i <= last_x).astype(F32)
        _, vjp = jax.vjp(_modnorm, x_, g_, sh_, sc_)
        dxn, dg, dsh, dsc = vjp(du_)
        return (dx * keep + dxn, dg, dsh, dsc)
    dx1, dg_mix, dsh3, dsc4 = rm("in_norm_bwd", in_norm_bwd, ntt,
                                 [tok(x1), const(wts["g_mix"]), mod(mods[3]), mod(mods[4]), tok(du2), tok(dx2, **cl)],
                                 [("tok", nt_rows, D, F32), ("acc", 1, D), ("modacc", D), ("modacc", D)])
    grads["g_mix"] = dg_mix

    dx0, gf1 = _ffn_bwd("ffn1", x0, sv1, dx1, None, ntt, wts["g_ffn1"], mods[0], mods[1], mods[2],
                        wts["w1_gu"], wts["w1_down"], tpb, nb, comm, grads, ("w1_gu", "w1_down"), keep_tiles=nxt)
    grads["g_ffn1"] = gf1["g"]
    grad_x = dx0.reshape(bl, t, D)

    dmods = [gf1["sh"], gf1["sc"], gf1["gate"], dsh3, dsc4, dgate5, gf2["sh"], gf2["sc"], gf2["gate"]]
    dmod = jnp.concatenate(
        [jnp.concatenate([a.reshape(a.shape[0], D), jnp.zeros((8 - a.shape[0], D), F32)], axis=0) for a in dmods],
        axis=1)
    dsc_ = _mm("mod_bproj", dmod, wts["w_mod"], trans_b=True)

    def silu_bwd(cc_ref, d_ref, dm_ref, dcc_ref, scc_ref, db_ref):
        cc_ = cc_ref[...]
        s = jax.nn.sigmoid(cc_)
        dcc_ref[...] = d_ref[...] * (s * (1 + cc_ * (1 - s)))
        scc_ref[...] = (cc_ * s).astype(BF16)
        db_ref[...] = jnp.sum(dm_ref[...], axis=0, keepdims=True)
    dcc, scc, db_mod = pl.pallas_call(
        silu_bwd, out_shape=[jax.ShapeDtypeStruct((8, D), F32), jax.ShapeDtypeStruct((8, D), BF16),
                             jax.ShapeDtypeStruct((1, 9 * D), F32)], name="mod_silu_bwd")(cc, dsc_, dmod)
    grads["c_ctx"] = dcc[nb]
    grads["b_mod"] = db_mod
    grads["w_mod"] = _mm_tn("mod_wgrad", scc, dmod, col_shards=True)
    grads["g_final"] = dg_final.reshape(D)
    return loss, grad_x, grads


ANY = pl.BlockSpec(memory_space=pl.ANY)


def _place():
    x, y, c = lax.axis_index("x"), lax.axis_index("y"), lax.axis_index("c")
    chips = [(1 - x, y), (x, 1 - y), (1 - x, 1 - y)]
    return x, y, c, chips


def _remote(send_sems, recv_sems):
    def copy(k, src, dst, to):
        return pltpu.make_async_remote_copy(src_ref=src, dst_ref=dst, send_sem=send_sems.at[k],
                                            recv_sem=recv_sems.at[k], device_id=to, device_id_type=MESH)
    return copy


def _sems(n):
    return [pltpu.SemaphoreType.DMA((n,)), pltpu.SemaphoreType.DMA((n,))]


def _gather_weights(shards):
    n = len(shards)

    def body(*refs):
        ins, outs = refs[:n], refs[n:2 * n]
        copy = _remote(refs[2 * n], refs[2 * n + 1])
        x, y, c, chips = _place()
        me = 2 * x + y
        sibling = (x, y, 1 - c)
        started = []

        def rows(i, hc):
            hr = ins[i].shape[0] // 2
            return pl.ds(hc * hr, hr)

        for i in range(n):
            started.append(copy(7 * i + 6, ins[i], outs[i].at[me], sibling))
            started[-1].start()
            for j, (px, py) in enumerate(chips):
                started.append(copy(7 * i + j, ins[i].at[rows(i, c), :], outs[i].at[me, rows(i, c), :], (px, py, c)))
                started[-1].start()
        for i in range(n):
            for j, (px, py) in enumerate(chips):
                half = outs[i].at[2 * px + py, rows(i, c), :]
                copy(7 * i + j, half, half, (px, py, c)).wait_recv()
                started.append(copy(7 * i + 3 + j, half, half, sibling))
                started[-1].start()
        for i in range(n):
            copy(7 * i + 6, ins[i], outs[i].at[me], sibling).wait_recv()
            for j, (px, py) in enumerate(chips):
                other = outs[i].at[2 * px + py, rows(i, 1 - c), :]
                copy(7 * i + 3 + j, other, other, sibling).wait_recv()
        for cp in started:
            cp.wait_send()

    return pl.pallas_call(
        body, out_shape=[jax.ShapeDtypeStruct((4,) + s.shape, s.dtype) for s in shards], in_specs=[ANY] * n,
        out_specs=[ANY] * n, scratch_shapes=_sems(7 * n), name="gather_weights")(*shards)


def _swap_halves(name, gs):
    n = len(gs)

    def body(*refs):
        ins, outs = refs[:n], refs[n:2 * n]
        copy = _remote(refs[2 * n], refs[2 * n + 1])
        x, y, c, _ = _place()
        cps = []
        for i in range(n):
            hr = ins[i].shape[1] // 2
            cps.append(copy(i, ins[i].at[:, pl.ds((1 - c) * hr, hr), :], outs[i], (x, y, 1 - c)))
            cps[-1].start()
        for cp in cps:
            cp.wait()

    return pl.pallas_call(
        body, out_shape=[jax.ShapeDtypeStruct((4, g.shape[1] // 2, g.shape[2]), g.dtype) for g in gs],
        in_specs=[ANY] * n, out_specs=[ANY] * n, scratch_shapes=_sems(n), name=name)(*gs)


def _row_tile(hr):
    return hr if hr <= 256 else _pick(hr, (256, 176, 128, 64, 32, 16))


def _add_halves(name, g, r, place):
    hr = r.shape[1]
    tr = _row_tile(hr)
    nblk = hr // tr

    def kern(p_ref, g_ref, r_ref, o_ref):
        o_ref[...] = (g_ref[...].astype(F32) + r_ref[...].astype(F32)).astype(o_ref.dtype)

    blk = (1, tr, g.shape[2])
    return pl.pallas_call(
        kern,
        grid_spec=pltpu.PrefetchScalarGridSpec(
            num_scalar_prefetch=1, grid=(4, nblk),
            in_specs=[pl.BlockSpec(blk, lambda j, i, p: (j, p[0] * nblk + i, 0)),
                      pl.BlockSpec(blk, lambda j, i, p: (j, i, 0))],
            out_specs=pl.BlockSpec(blk, lambda j, i, p: (j, i, 0))),
        out_shape=jax.ShapeDtypeStruct(r.shape, r.dtype),
        compiler_params=_cparams(("parallel", "parallel")), name=name)(place, g, r)


def _scatter_chips(cs):
    n = len(cs)

    def body(*refs):
        ins, outs = refs[:n], refs[n:2 * n]
        copy = _remote(refs[2 * n], refs[2 * n + 1])
        x, y, c, chips = _place()
        me = 2 * x + y
        sends = []
        for i in range(n):
            for j, (px, py) in enumerate(chips):
                sends.append(copy(3 * i + j, ins[i].at[2 * px + py], outs[i].at[me], (px, py, c)))
                sends[-1].start()
        for i in range(n):
            for j, (px, py) in enumerate(chips):
                src = 2 * px + py
                copy(3 * i + j, ins[i].at[src], outs[i].at[src], (px, py, c)).wait_recv()
        for cp in sends:
            cp.wait_send()

    return pl.pallas_call(
        body, out_shape=[jax.ShapeDtypeStruct(a.shape, a.dtype) for a in cs], in_specs=[ANY] * n,
        out_specs=[ANY] * n, scratch_shapes=_sems(3 * n), name="grad_scatter_chips")(*cs)


def _sum_chips(name, cs, r, place):
    hr = r.shape[1]
    tr = _row_tile(hr)
    nblk = hr // tr

    def kern(p_ref, c_ref, r0, r1, r2, r3, o_ref):
        me = p_ref[1]
        acc = None
        for k, rk in enumerate((r0, r1, r2, r3)):
            val = jnp.where(me == k, c_ref[0].astype(F32), rk[0].astype(F32))
            acc = val if acc is None else acc + val
        o_ref[...] = acc

    blk = (1, tr, r.shape[2])

    def slot(k):
        return lambda i, p: (jnp.where(p[1] == k, (k + 1) % 4, k), i, 0)

    return pl.pallas_call(
        kern,
        grid_spec=pltpu.PrefetchScalarGridSpec(
            num_scalar_prefetch=1, grid=(nblk,),
            in_specs=[pl.BlockSpec(blk, lambda i, p: (p[1], i, 0))] + [pl.BlockSpec(blk, slot(k)) for k in range(4)],
            out_specs=pl.BlockSpec((tr, r.shape[2]), lambda i, p: (p[0] * nblk + i, 0))),
        out_shape=jax.ShapeDtypeStruct((2 * hr, r.shape[2]), F32),
        compiler_params=_cparams(("parallel",)), name=name)(place, cs, r, r, r, r)


def _join_halves(fs):
    n = len(fs)

    def body(*refs):
        ins, outs = refs[:n], refs[n:2 * n]
        copy = _remote(refs[2 * n], refs[2 * n + 1])
        x, y, c, _ = _place()
        cps = []
        for i in range(n):
            hr = ins[i].shape[0] // 2
            cps.append(copy(i, ins[i].at[pl.ds(c * hr, hr), :], outs[i].at[pl.ds(c * hr, hr), :], (x, y, 1 - c)))
            cps[-1].start()
        for i in range(n):
            hr = ins[i].shape[0] // 2
            other = outs[i].at[pl.ds((1 - c) * hr, hr), :]
            copy(i, other, other, (x, y, 1 - c)).wait_recv()
        for cp in cps:
            cp.wait_send()

    return pl.pallas_call(
        body, out_shape=[jax.ShapeDtypeStruct(f.shape, f.dtype) for f in fs], in_specs=[ANY] * n,
        out_specs=[ANY] * n, input_output_aliases={i: i for i in range(n)}, scratch_shapes=_sems(n),
        name="grad_join_halves")(*fs)


def _half_rows(n, hc):
    return pl.ds(hc * (n // 2), n // 2)


def _gather_ici_side(shards):
    n = len(shards)

    def copies(ins, outs, copy):
        x, y, c, chips = _place()
        me = 2 * x + y
        for i in range(n):
            rows = _half_rows(ins[i].shape[0], c)
            for j, (px, py) in enumerate(chips):
                yield (copy(3 * i + j, ins[i].at[rows, :], outs[i].at[me, rows, :], (px, py, c)),
                       outs[i].at[2 * px + py, rows, :])

    def start(ins, outs, copy):
        for cp, _ in copies(ins, outs, copy):
            cp.start()

    def finish(ins, outs, copy):
        x, y, c, chips = _place()
        k = 0
        for cp, landing in copies(ins, outs, copy):
            copy(k, landing, landing, (x, y, c)).wait_recv()
            k += 1
        for cp, _ in copies(ins, outs, copy):
            cp.wait_send()

    return dict(ins=list(shards), outs=[jax.ShapeDtypeStruct((4,) + s.shape, s.dtype) for s in shards], nsem=3 * n,
                start=start, finish=finish)


def _gather_d2d_side(shards, bufs):
    n = len(shards)

    def copies(ins, outs, copy):
        x, y, c, chips = _place()
        me = 2 * x + y
        sibling = (x, y, 1 - c)
        for i in range(n):
            a = ins[i].shape[0]
            yield copy(4 * i + 3, ins[i], outs[i].at[me], sibling), outs[i].at[me]
            for j, (px, py) in enumerate(chips):
                src = 2 * px + py
                mine = outs[i].at[src, _half_rows(a, c), :]
                yield copy(4 * i + j, mine, mine, sibling), outs[i].at[src, _half_rows(a, 1 - c), :]

    def start(ins, outs, copy):
        for cp, _ in copies(ins, outs, copy):
            cp.start()

    def finish(ins, outs, copy):
        x, y, c, _ = _place()
        for i in range(n):
            for k, (cp, landing) in enumerate(list(copies(ins, outs, copy))[4 * i:4 * i + 4]):
                sem = 4 * i + 3 if k == 0 else 4 * i + k - 1
                copy(sem, landing, landing, (x, y, 1 - c)).wait_recv()
        for cp, _ in copies(ins, outs, copy):
            cp.wait_send()

    return dict(ins=list(shards) + list(bufs), outs=[jax.ShapeDtypeStruct(b.shape, b.dtype) for b in bufs],
                nsem=4 * n, alias={n + i: i for i in range(n)}, start=start, finish=finish)


def _scatter_side(cs):
    n = len(cs)

    def copies(ins, outs, copy):
        x, y, c, chips = _place()
        me = 2 * x + y
        for i in range(n):
            for j, (px, py) in enumerate(chips):
                yield copy(3 * i + j, ins[i].at[2 * px + py], outs[i].at[me], (px, py, c)), outs[i].at[2 * px + py]

    def start(ins, outs, copy):
        for cp, _ in copies(ins, outs, copy):
            cp.start()

    def finish(ins, outs, copy):
        x, y, c, _ = _place()
        for k, (cp, landing) in enumerate(copies(ins, outs, copy)):
            copy(k, landing, landing, (x, y, c)).wait_recv()
        for cp, _ in copies(ins, outs, copy):
            cp.wait_send()

    return dict(ins=list(cs), outs=[jax.ShapeDtypeStruct(a.shape, a.dtype) for a in cs], nsem=3 * n,
                start=start, finish=finish)


def _small_pack(dw, af, ab):
    return jnp.concatenate([dw, jnp.zeros((1, dw.shape[1]), F32), jnp.concatenate([af, ab], axis=1)], axis=0)


def _grad_pieces(n, grads):
    if n == "small":
        return jnp.stack([_small_pack(grads["dw_weight"][:, 256 * j:256 * (j + 1)],
                                      grads["w_alpha_f"][:, DK * j:DK * (j + 1)],
                                      grads["w_alpha_b"][:, DK * j:DK * (j + 1)]) for j in range(4)])
    if n == "w_in":
        g = _unperm_in_cols(grads["w_in_p"])
        return jnp.transpose(g.reshape(D, 4, D_IN // 4), (1, 0, 2))
    g = grads[n]
    return g if g.ndim == 3 else g.reshape(4, g.shape[0] // 4, g.shape[1])


def _chip_sums(tag, names, grads, place):
    gs = [_grad_pieces(n, grads) for n in names]
    swapped = _swap_halves("grad_swap_" + tag, gs)
    return [_add_halves("grad_add_" + n, g, r, place) for n, g, r in zip(names, gs, swapped)]


class _Overlap:
    SCATTER = {"in_bproj": ("w2_down", "w2_gu", "w_out"),
               "ffn1_wgu": ("w_conv_out", "w_gla_out", "w_in", "small"),
               "ffn1_bgu": ("w1_down", "w1_gu")}
    GATHER = {("ffn1_gu", "ffn1_down"): ("w_in", "w_conv_out", "w_gla_out", "w_out"),
              ("in_proj", "gla_proj"): ("w2_gu", "w2_down")}
    LATE = tuple(n for names in GATHER.values() for n in names)

    def __init__(self, shard_of, install, place):
        self.shard_of, self.install, self.place = shard_of, install, place
        self.bufs, self.pending, self.landed = {}, None, {}

    def side(self, tag, grads):
        for (ici, d2d), names in self.GATHER.items():
            shards = [self.shard_of(n) for n in names]
            if tag == ici:
                return _gather_ici_side(shards)
            if tag == d2d:
                return _gather_d2d_side(shards, self.bufs[ici])
        if tag in self.SCATTER:
            names = self.SCATTER[tag]
            self.pending = (names, _chip_sums(tag, names, grads, self.place))
            return _scatter_side(self.pending[1])
        return None

    def done(self, tag, outs):
        for (ici, d2d), names in self.GATHER.items():
            if tag == ici:
                self.bufs[ici] = outs
                return
            if tag == d2d:
                self.install(dict(zip(names, outs)))
                return
        for n, cs, r in zip(*self.pending, outs):
            self.landed[n] = (cs, r)


def _with_side(comm, tag, grads, call):
    side = comm.side(tag, grads) if comm is not None else None
    if side is None:
        return call(None)
    res, outs = call(side)
    comm.done(tag, outs)
    return res


def _allreduce_small(v):
    def body(x_ref, out_ref, gath, send_sems, recv_sems, local_sem):
        x, y, c, chips = _place()
        me, sibling = (x, y, c), (x, y, 1 - c)

        def slot(px, py, pc):
            return gath.at[4 * px + 2 * py + pc]

        def copy(k, block, to, src=None):
            return pltpu.make_async_remote_copy(
                src_ref=slot(*block) if src is None else src, dst_ref=slot(*block), send_sem=send_sems.at[k],
                recv_sem=recv_sems.at[k], device_id=to, device_id_type=MESH)

        mine = pltpu.make_async_copy(x_ref, slot(*me), local_sem)
        mine.start()
        first = [copy(0, me, sibling, src=x_ref)]
        first += [copy(1 + j, me, (*chip, c), src=x_ref) for j, chip in enumerate(chips)]
        for cp in first:
            cp.start()
        passed = [copy(4 + j, (*chip, c), sibling) for j, chip in enumerate(chips)]
        for j, chip in enumerate(chips):
            copy(1 + j, (*chip, c), me).wait_recv()
            passed[j].start()
        copy(0, sibling, me).wait_recv()
        for j, chip in enumerate(chips):
            copy(4 + j, (*chip, 1 - c), me).wait_recv()
        for cp in first + passed:
            cp.wait_send()
        mine.wait()
        acc = gath[0]
        for k in range(1, 8):
            acc = acc + gath[k]
        out_ref[...] = acc

    vm = pl.BlockSpec(memory_space=pltpu.VMEM)
    return pl.pallas_call(
        body, out_shape=jax.ShapeDtypeStruct(v.shape, F32), in_specs=[vm], out_specs=vm,
        scratch_shapes=[pltpu.VMEM((8,) + v.shape, F32), pltpu.SemaphoreType.DMA((7,)),
                        pltpu.SemaphoreType.DMA((7,)), pltpu.SemaphoreType.DMA(())],
        name="allreduce_small")(v)


def _adamw(name, w, g, m, v):
    r, cols = w.shape
    budget = 262144
    tr = r if r * cols <= budget else next(c for c in (256, 128, 64, 32, 16, 8) if r % c == 0 and c * cols <= budget)

    def kern(w_ref, g_ref, m_ref, v_ref, d_ref, nm_ref, nv_ref):
        gv = g_ref[...]
        nm = ADAM_B1 * m_ref[...] + (1.0 - ADAM_B1) * gv
        nv = ADAM_B2 * v_ref[...] + (1.0 - ADAM_B2) * jnp.square(gv)
        m_hat = nm / (1.0 - ADAM_B1 ** ADAM_STEP)
        v_hat = nv / (1.0 - ADAM_B2 ** ADAM_STEP)
        d_ref[...] = -ADAM_LR * (m_hat / (jnp.sqrt(v_hat) + ADAM_EPS) + ADAM_WD * w_ref[...])
        nm_ref[...] = nm
        nv_ref[...] = nv

    spec = pl.BlockSpec((tr, cols), lambda i: (i, 0))
    shp = jax.ShapeDtypeStruct((r, cols), F32)
    return pl.pallas_call(kern, grid=(r // tr,), in_specs=[spec] * 4, out_specs=[spec] * 3, out_shape=[shp] * 3,
                          compiler_params=_cparams(("parallel",)), name=name)(w, g, m, v)


SHARDED = (("w_mod", 1), ("w1_gu", 1), ("w1_down", 0), ("w_in", 1), ("dw_weight", 1), ("w_conv_out", 0),
           ("w_alpha_f", 1), ("w_alpha_b", 1), ("w_gla_out", 0), ("w_out", 0), ("w2_gu", 1), ("w2_down", 0))
REPLICATED = ("c_ctx", "b_mod", "g_ffn1", "g_mix", "dw_bias", "conv_ln_g", "conv_ln_b", "b_alpha_f", "b_alpha_b",
              "gla_norm_g", "g_ffn2", "g_final")
WEIGHTS = ("c_ctx", "w_mod", "b_mod", "g_ffn1", "w1_gu", "w1_down", "g_mix", "w_in", "dw_weight", "dw_bias",
           "conv_ln_g", "conv_ln_b", "w_conv_out", "w_alpha_f", "b_alpha_f", "w_alpha_b", "b_alpha_b", "gla_norm_g",
           "w_gla_out", "w_out", "g_ffn2", "w2_gu", "w2_down", "g_final")
MATRICES = ("w_mod", "w1_gu", "w1_down", "w_in", "w_conv_out", "w_gla_out", "w_out", "w2_gu", "w2_down")


def _pack_flat(parts, align):
    flat = jnp.concatenate([p.reshape(-1) for p in parts])
    pad = (-flat.shape[0]) % align
    return jnp.concatenate([flat, jnp.zeros((pad,), flat.dtype)]).reshape(-1, 1024)


def _unpack_flat(flat2d, shapes):
    flat = flat2d.reshape(-1)
    out, off = [], 0
    for s in shapes:
        n = math.prod(s)
        out.append(flat[off:off + n].reshape(s))
        off += n
    return out


def kernel(x, c, ctx, c_ctx, w_mod, b_mod, g_ffn1, w1_gu, w1_down, g_mix, w_in, dw_weight, dw_bias, conv_ln_g, conv_ln_b, w_conv_out, w_alpha_f, b_alpha_f, w_alpha_b, b_alpha_b, gla_norm_g, w_gla_out, w_out, g_ffn2, w2_gu, w2_down, g_final, loss_target, m_c_ctx, m_w_mod, m_b_mod, m_g_ffn1, m_w1_gu, m_w1_down, m_g_mix, m_w_in, m_dw_weight, m_dw_bias, m_conv_ln_g, m_conv_ln_b, m_w_conv_out, m_w_alpha_f, m_b_alpha_f, m_w_alpha_b, m_b_alpha_b, m_gla_norm_g, m_w_gla_out, m_w_out, m_g_ffn2, m_w2_gu, m_w2_down, m_g_final, v_c_ctx, v_w_mod, v_b_mod, v_g_ffn1, v_w1_gu, v_w1_down, v_g_mix, v_w_in, v_dw_weight, v_dw_bias, v_conv_ln_g, v_conv_ln_b, v_w_conv_out, v_w_alpha_f, v_b_alpha_f, v_w_alpha_b, v_b_alpha_b, v_gla_norm_g, v_w_gla_out, v_w_out, v_g_ffn2, v_w2_gu, v_w2_down, v_g_final):
    given = dict(locals())
    w = {n: given[n] for n in WEIGHTS}
    m = {n: given["m_" + n] for n in WEIGHTS}
    v = {n: given["v_" + n] for n in WEIGHTS}

    def bf16_shard(n):
        return w[n][0].astype(BF16)

    def install(wts, got):
        for n in ("w_mod", "w1_gu", "w2_gu"):
            if n in got:
                wts[n] = got[n]
        for n in ("w1_down", "w2_down", "w_conv_out", "w_gla_out", "w_out"):
            if n in got:
                wts[n] = got[n].reshape(-1, D)
        if "w_in" in got:
            wts["w_in_p"] = _perm_in_cols(jnp.concatenate([got["w_in"][j] for j in range(4)], axis=1))

    early = tuple(n for n in MATRICES if n not in _Overlap.LATE)
    shards =[bf16_shard(n) for n in early] + [_small_pack(w["dw_weight"][0], w["w_alpha_f"][0], w["w_alpha_b"][0])]
    got = dict(zip(early + ("small",), _gather_weights(shards)))
    wts = {n: w[n] for n in REPLICATED}
    install(wts, got)
    sm = got["small"]
    wts["dw_weight"] = jnp.concatenate([sm[j, :CONV_W] for j in range(4)], axis=1)
    zpad = jnp.zeros((128, HEADS * DK), BF16)
    w_af = jnp.concatenate([sm[j, 32:32 + LOWRANK, :DK] for j in range(4)], axis=1)
    w_ab = jnp.concatenate([sm[j, 32:32 + LOWRANK, DK:] for j in range(4)], axis=1)
    wts["w_alpha_f_pad"] = zpad.at[0:LOWRANK].set(w_af.astype(BF16))
    wts["w_alpha_b_pad"] = zpad.at[LOWRANK:2 * LOWRANK].set(w_ab.astype(BF16))

    place = jnp.stack([lax.axis_index("c"), 2 * lax.axis_index("x") + lax.axis_index("y")]).astype(jnp.int32)
    comm = _Overlap(bf16_shard, lambda got_late: install(wts, got_late), place)
    loss, grad_x, grads = _local_step(x, c, ctx, loss_target, wts, comm)
    loss = lax.psum(loss, ("x", "y", "c"))

    tags = MATRICES + ("small",)
    rest = tuple(n for n in tags if n not in comm.landed)
    rest_sums = _chip_sums("rest", rest, grads, place)
    for n, cs, r in zip(rest, rest_sums, _scatter_chips(rest_sums)):
        comm.landed[n] = (cs, r)
    halves = [_sum_chips("grad_sum_" + t, *comm.landed[t], place) for t in tags]
    reduced = dict(zip(tags, _join_halves(halves)))
    g_shard = {n: reduced[n] for n in MATRICES}
    g_shard["dw_weight"] = reduced["small"][:CONV_W]
    g_shard["w_alpha_f"] = reduced["small"][32:32 + LOWRANK, :DK]
    g_shard["w_alpha_b"] = reduced["small"][32:32 + LOWRANK, DK:]

    rep_shapes = [w[n].shape for n in REPLICATED]
    small = _allreduce_small(_pack_flat([grads[n].reshape(w[n].shape) for n in REPLICATED], 8 * 1024))
    g_rep = dict(zip(REPLICATED, _unpack_flat(small, rep_shapes)))

    g_out, d_out, m_out, v_out = {}, {}, {}, {}
    for n, _ in SHARDED:
        s2 = w[n].shape[1:]
        d, nm, nv = _adamw("adamw_" + n, w[n].reshape(s2), g_shard[n], m[n].reshape(s2), v[n].reshape(s2))
        g_out[n] = g_shard[n].reshape(w[n].shape)
        d_out[n], m_out[n], v_out[n] = d.reshape(w[n].shape), nm.reshape(w[n].shape), nv.reshape(w[n].shape)
    pk = lambda t: _pack_flat([t[n] for n in REPLICATED], 8 * 1024)
    d, nm, nv = _adamw("adamw_vectors", pk(w), small, pk(m), pk(v))
    for n, dd, mm, vv in zip(REPLICATED, _unpack_flat(d, rep_shapes), _unpack_flat(nm, rep_shapes),
                             _unpack_flat(nv, rep_shapes)):
        g_out[n], d_out[n], m_out[n], v_out[n] = g_rep[n], dd, mm, vv

    return (loss, grad_x, *[g_out[n] for n in WEIGHTS], *[d_out[n] for n in WEIGHTS],
            *[m_out[n] for n in WEIGHTS], *[v_out[n] for n in WEIGHTS])
```

```python
import functools
import math

import jax
import jax.numpy as jnp
from jax import lax
from jax.experimental import pallas as pl
from jax.experimental.pallas import tpu as pltpu

F32, BF16 = jnp.float32, jnp.bfloat16
MESH = pl.DeviceIdType.MESH
HIGHEST = lax.Precision.HIGHEST

D = 1024
FF = 2816
HEADS, DK, DV = 4, 128, 256
LOWRANK = 16
CONV_W = 31
CHUNK = 64
TAU = 16.0
EPS = 1e-6
Q_SCALE = DK ** -0.5
TM = 256
D_IN = 7200
D_INP = 7296
LR_COL = 7168
OG_CB, GA_CB, GB_CB = 6, 4, 5
VMEM_LIMIT = 52 * 1024 * 1024
WGRAD_VMEM = 40 * 1024 * 1024

ADAM_LR, ADAM_B1, ADAM_B2, ADAM_EPS, ADAM_WD, ADAM_STEP = 0.001, 0.9, 0.999, 1e-08, 0.01, 10


def _silu(x):
    return x * jax.nn.sigmoid(x)


def _rms(h, g):
    return h * lax.rsqrt(jnp.mean(h * h, axis=-1, keepdims=True) + EPS) * g


def _modnorm(x, g, shift, scale):
    return _rms(x, g) * (1 + scale) + shift


def _cparams(sem=None):
    return pltpu.CompilerParams(dimension_semantics=sem, vmem_limit_bytes=VMEM_LIMIT)


def _twice(v):
    return v, v


def tok(arr, width=None, cb=0, clamp=None):
    return ("tok", arr, arr.shape[1] if width is None else width, cb, clamp)


def mod(arr):
    return ("mod", arr)


def const(arr):
    return ("const", arr)


def _rowmap(name, body, n_tiles, ins, outs, *, tpb, nb):
    def modrow(i):
        return jnp.minimum(i // tpb, nb)

    in_specs, args = [], []
    for spec in ins:
        if spec[0] == "tok":
            _, arr, width, cb, clamp = spec
            if clamp is None:
                im = lambda i, cb=cb: (i, cb)
            else:
                im = lambda i, cb=cb, clamp=clamp: (jnp.minimum(i, clamp), cb)
            in_specs.append(pl.BlockSpec((TM, width), im))
        elif spec[0] == "mod":
            arr = spec[1]
            in_specs.append(pl.BlockSpec((1, 1, arr.shape[2]), lambda i: (modrow(i), 0, 0)))
        else:
            arr = spec[1]
            in_specs.append(pl.BlockSpec(arr.shape, lambda i, nd=arr.ndim: (0,) * nd))
        args.append(arr)
    out_specs, out_shapes, aliases = [], [], {}
    for o in outs:
        if o[0] == "tok":
            _, rows, width, dtype = o
            out_shapes.append(jax.ShapeDtypeStruct((rows, width), dtype))
            out_specs.append(pl.BlockSpec((TM, width), lambda i: (i, 0)))
        elif o[0] == "cols":
            buf, width, cb = o[1:4]
            first_tile = o[4] if len(o) > 4 else 0
            if not isinstance(buf, jax.ShapeDtypeStruct):
                aliases[len(args)] = len(out_shapes)
                in_specs.append(ANY)
                args.append(buf)
            out_shapes.append(jax.ShapeDtypeStruct(buf.shape, buf.dtype))
            out_specs.append(pl.BlockSpec((TM, width), lambda i, cb=cb, t0=first_tile: (i + t0, cb)))
        elif o[0] == "tok_head":
            _, rows, width, dtype, last = o
            out_shapes.append(jax.ShapeDtypeStruct((rows, width), dtype))
            out_specs.append(pl.BlockSpec((TM, width), lambda i, last=last: (jnp.minimum(i, last), 0)))
        elif o[0] == "tokT":
            _, rows, width, dtype = o
            out_shapes.append(jax.ShapeDtypeStruct((width, rows), dtype))
            out_specs.append(pl.BlockSpec((width, TM), lambda i: (0, i)))
        elif o[0] == "acc":
            _, rows, width = o
            out_shapes.append(jax.ShapeDtypeStruct((rows, width), F32))
            out_specs.append(pl.BlockSpec((rows, width), lambda i: (0, 0)))
        else:
            width = o[1]
            rows_visited = min((n_tiles - 1) // tpb, nb) + 1
            out_shapes.append(jax.ShapeDtypeStruct((rows_visited, 1, width), F32))
            out_specs.append(pl.BlockSpec((1, 1, width), lambda i: (modrow(i), 0, 0)))
    n_in = len(ins)

    def kern(*refs):
        i = pl.program_id(0)
        vals = []
        for r, spec in zip(refs[:n_in], ins):
            val = r[0] if spec[0] == "mod" else r[...]
            vals.append(val.astype(F32) if spec[0] == "tok" and val.dtype == BF16 else val)
        res = body(i, *vals)
        for r, o, val in zip(refs[len(args):], outs, res):
            if o[0] in ("tok", "cols"):
                r[...] = val.astype(r.dtype)
            elif o[0] == "tok_head":
                @pl.when(i <= o[4])
                def _():
                    r[...] = val.astype(r.dtype)
            elif o[0] == "tokT":
                r[...] = val.T.astype(r.dtype)
            elif o[0] == "acc":
                @pl.when(i == 0)
                def _():
                    r[...] = jnp.zeros(r.shape, F32)
                r[...] += jnp.broadcast_to(val, r.shape)
            else:
                first = jnp.logical_or(i == 0, modrow(i) != modrow(jnp.maximum(i - 1, 0)))

                @pl.when(first)
                def _():
                    r[...] = jnp.zeros(r.shape, F32)
                r[0] += val

    return pl.pallas_call(
        kern, grid=(n_tiles,), in_specs=in_specs, out_specs=out_specs, out_shape=out_shapes,
        input_output_aliases=aliases, compiler_params=_cparams(("arbitrary",)), name=name)(*args)


def _pick(n, cands):
    for c in cands:
        if n % c == 0:
            return c
    return n


def _pallas(kern, *, grid, in_specs, out_specs, out_shape, scratch_shapes, sem, name, args, side=None):
    if side is None:
        return pl.pallas_call(kern, grid=grid, in_specs=in_specs, out_specs=out_specs, out_shape=out_shape,
                              scratch_shapes=scratch_shapes, compiler_params=_cparams(sem), name=name)(*args)
    single = not isinstance(out_shape, (list, tuple))
    shapes = [out_shape] if single else list(out_shape)
    ospecs = [out_specs] if single else list(out_specs)
    n_in, n_out, n_scr = len(in_specs), len(shapes), len(scratch_shapes)
    s_in, s_out = list(side["ins"]), list(side["outs"])

    def wrapped(*refs):
        pos = [0]

        def take(n):
            pos[0] += n
            return refs[pos[0] - n:pos[0]]
        ins, sins, outs, souts, scr, sems = take(n_in), take(len(s_in)), take(n_out), take(len(s_out)), take(n_scr), take(2)
        ids = [pl.program_id(k) for k in range(len(grid))]
        first = functools.reduce(jnp.logical_and, [i == 0 for i in ids])
        last = functools.reduce(jnp.logical_and, [i == g - 1 for i, g in zip(ids, grid)])
        copy = _remote(*sems)

        @pl.when(first)
        def _():
            side["start"](sins, souts, copy)
        kern(*ins, *outs, *scr)

        @pl.when(last)
        def _():
            side["finish"](sins, souts, copy)

    res = pl.pallas_call(
        wrapped, grid=grid, in_specs=list(in_specs) + [ANY] * len(s_in), out_specs=ospecs + [ANY] * len(s_out),
        out_shape=shapes + s_out, scratch_shapes=list(scratch_shapes) + _sems(side["nsem"]),
        input_output_aliases={n_in + a: n_out + b for a, b in side.get("alias", {}).items()},
        compiler_params=_cparams(("arbitrary",) * len(grid)), name=name)(*args, *s_in)
    main = res[:n_out]
    return (main[0] if single else main), list(res[n_out:])


def _mm(name, a, b, *, trans_b=False, out_dtype=F32, a_fn=None, bias=None, rows=None, side=None, residual=None):
    m, k = a.shape if a.ndim == 2 else (a.shape[1], 2 * a.shape[2])
    m = m if rows is None else rows
    shard = b.shape[2] if b.ndim == 3 else None
    if trans_b:
        n = b.shape[-2]
        tk = _pick(shard, (2816, 2304, 1408, 1024)) if shard else (
            k if k <= 2816 else _pick(k, (2816, 2432, 2304, 2048, 1536, 1408, 1024, 512, 256, 128)))
        tn = _pick(n, (1024, 512, 384, 256, 128))
    else:
        n = 4 * shard if shard else b.shape[1]
        tk = k if k <= 2816 else _pick(k, (2816, 2432, 2304, 2048, 1536, 1408, 1024, 512, 256, 128))
        tn = _pick(shard, (512, 384, 1408, 256, 128)) if shard else _pick(n, (1024, 2432, 512, 384, 256, 128))
    tm = _pick(m, (512, 256) if residual is not None else (1024, 512, 256))
    nk = k // tk
    per = shard // (tk if trans_b else tn) if shard else None
    dims = (((1,), (1,)), ((), ())) if trans_b else (((1,), (0,)), ((), ()))

    def kern(*refs):
        a_ref, b_ref = refs[0], refs[1]
        bias_ref = refs[2] if bias is not None else None
        acc_ref = refs[-1]
        if residual is not None:
            x_ref, gate_ref, o_ref, xo_ref = refs[-5:-1]
        else:
            o_ref = refs[-2]
        kk = pl.program_id(2)
        av = a_ref[...]
        if a_fn is not None:
            av = a_fn(av)
        p = lax.dot_general(av.astype(BF16), b_ref[...].astype(BF16), dims, preferred_element_type=F32)

        def finish(total):
            if bias_ref is not None:
                total = total + bias_ref[...]
            o_ref[...] = total.astype(o_ref.dtype)
            if residual is not None:
                xo_ref[...] = x_ref[...] + residual["scale"] * gate_ref[0] * total

        if nk == 1:
            finish(p)
        else:
            @pl.when(kk == 0)
            def _():
                acc_ref[...] = p

            @pl.when(kk > 0)
            def _():
                acc_ref[...] += p

            @pl.when(kk == nk - 1)
            def _():
                finish(acc_ref[...])

    if shard and trans_b:
        b_spec = pl.BlockSpec((None, tn, tk), lambda i, j, kk: (kk // per, j, kk % per))
    elif shard:
        b_spec = pl.BlockSpec((None, tk, tn), lambda i, j, kk: (j // per, kk, j % per))
    elif trans_b:
        b_spec = pl.BlockSpec((tn, tk), lambda i, j, kk: (j, kk))
    else:
        b_spec = pl.BlockSpec((tk, tn), lambda i, j, kk: (kk, j))
    if a.ndim == 3:
        pa = a.shape[2] // tk
        a_spec = pl.BlockSpec((None, tm, tk), lambda i, j, kk: (kk // pa, i, kk % pa))
    else:
        a_spec = pl.BlockSpec((tm, tk), lambda i, j, kk: (i, kk))
    in_specs = [a_spec, b_spec]
    args = [a, b]
    if bias is not None:
        in_specs.append(pl.BlockSpec((1, tn), lambda i, j, kk: (0, j)))
        args.append(bias)
    out_specs = pl.BlockSpec((tm, tn), lambda i, j, kk: (i, j))
    out_shape = jax.ShapeDtypeStruct((m, n), out_dtype)
    if residual is not None:
        tiles, nb = residual["tpb"] * TM // tm, residual["gate"].shape[0] - 1
        in_specs += [out_specs, pl.BlockSpec((1, 1, tn), lambda i, j, kk: (jnp.minimum(i // tiles, nb), 0, j))]
        args += [residual["x"], residual["gate"]]
        out_specs, out_shape = [out_specs, out_specs], [out_shape, jax.ShapeDtypeStruct((m, n), F32)]
    return _pallas(
        kern, grid=(m // tm, n // tn, nk), in_specs=in_specs, out_specs=out_specs, out_shape=out_shape,
        scratch_shapes=[pltpu.VMEM((tm, tn) if nk > 1 else (8, 128), F32)],
        sem=("parallel", "parallel", "arbitrary"), name=name, args=args, side=side)


def _mm_tn(name, x, dy, rows=None, col_shards=False):
    t = x.shape[0] if rows is None else rows
    k1, n1 = x.shape[1], (dy.shape[1] if dy.ndim == 2 else 2 * dy.shape[2])
    tt = _pick(t, (512, 256, 128, 64, 8))
    tk1 = _pick(k1, (1024, 1408, 512, 256, 128))
    tn = _pick(n1 // 4, (512, 384, 1408, 256, 128)) if col_shards else _pick(n1, (512, 384, 256, 128))
    per = n1 // 4 // tn
    ns = t // tt

    def kern(x_ref, dy_ref, o_ref, acc_ref):
        s = pl.program_id(2)
        p = lax.dot_general(x_ref[...].astype(BF16), dy_ref[...].astype(BF16), (((0,), (0,)), ((), ())),
                            preferred_element_type=F32)

        @pl.when(s == 0)
        def _():
            acc_ref[...] = p

        @pl.when(s > 0)
        def _():
            acc_ref[...] += p

        @pl.when(s == ns - 1)
        def _():
            o_ref[...] = acc_ref[...].astype(o_ref.dtype)

    if col_shards:
        out_spec = pl.BlockSpec((None, tk1, tn), lambda i, j, s: (j // per, i, j % per))
        out_shape = jax.ShapeDtypeStruct((4, k1, n1 // 4), BF16)
    else:
        out_spec = pl.BlockSpec((tk1, tn), lambda i, j, s: (i, j))
        out_shape = jax.ShapeDtypeStruct((k1, n1), BF16)
    if dy.ndim == 3:
        pd = dy.shape[2] // tn
        dy_spec = pl.BlockSpec((None, tt, tn), lambda i, j, s: (j // pd, s, j % pd))
    else:
        dy_spec = pl.BlockSpec((tt, tn), lambda i, j, s: (s, j))
    return pl.pallas_call(
        kern, grid=(k1 // tk1, n1 // tn, ns),
        in_specs=[pl.BlockSpec((tt, tk1), lambda i, j, s: (s, i)), dy_spec],
        out_specs=out_spec, out_shape=out_shape, scratch_shapes=[pltpu.VMEM((tk1, tn), F32)],
        compiler_params=_cparams(("parallel", "parallel", "arbitrary")), name=name)(x, dy)


def _wgrad(name, xt, dy, rows=None, col_shards=False, side=None):
    k1 = xt.shape[0]
    t = xt.shape[1] if rows is None else rows
    n1 = dy.shape[1] if dy.ndim == 2 else 2 * dy.shape[2]
    tm = _pick(k1, (1024, 1408, 512, 256))
    tn = _pick(n1 // 4, (1408, 512, 384, 256, 128)) if col_shards else _pick(n1, (1024, 2432, 512, 384, 256, 128))
    fixed = tm * tn * (4 + 4 + 2 * 2)
    tk = next((c for c in (2048, 1536, 1024, 512, 256, 128)
               if t % c == 0 and fixed + 4 * c * (tm + tn) <= WGRAD_VMEM), 128)
    ns = t // tk
    per = n1 // 4 // tn

    def kern(x_ref, dy_ref, o_ref, acc_ref):
        s = pl.program_id(2)
        p = jnp.dot(x_ref[...], dy_ref[...], preferred_element_type=F32)

        @pl.when(s == 0)
        def _():
            acc_ref[...] = p

        @pl.when(s > 0)
        def _():
            acc_ref[...] += p

        @pl.when(s == ns - 1)
        def _():
            o_ref[...] = acc_ref[...].astype(o_ref.dtype)

    if dy.ndim == 3:
        pd = dy.shape[2] // tn
        dy_spec = pl.BlockSpec((None, tk, tn), lambda i, j, s: (j // pd, s, j % pd))
    else:
        dy_spec = pl.BlockSpec((tk, tn), lambda i, j, s: (s, j))
    if col_shards:
        out_spec = pl.BlockSpec((None, tm, tn), lambda i, j, s: (j // per, i, j % per))
        out_shape = jax.ShapeDtypeStruct((4, k1, n1 // 4), BF16)
    else:
        out_spec = pl.BlockSpec((tm, tn), lambda i, j, s: (i, j))
        out_shape = jax.ShapeDtypeStruct((k1, n1), BF16)
    return _pallas(
        kern, grid=(k1 // tm, n1 // tn, ns),
        in_specs=[pl.BlockSpec((tm, tk), lambda i, j, s: (i, s)), dy_spec],
        out_specs=out_spec, out_shape=out_shape, scratch_shapes=[pltpu.VMEM((tm, tn), F32)],
        sem=("parallel", "parallel", "arbitrary"), name=name, args=[xt, dy], side=side)


def _swiglu_fwd(name, u, w_gu, side=None):
    m = u.shape[0]
    half = w_gu.shape[2]
    tm = _pick(m, (512, 256))

    def kern(u_ref, wa_ref, wb_ref, ab_ref, hm_ref, hmt_ref):
        uv = u_ref[...]
        a = jnp.dot(uv, wa_ref[...], preferred_element_type=F32)
        b = jnp.dot(uv, wb_ref[...], preferred_element_type=F32)
        s = jax.nn.sigmoid(a)
        silu_a = a * s
        ab_ref[0] = (b * (s * (1 + a * (1 - s)))).astype(BF16)
        ab_ref[1] = silu_a.astype(BF16)
        hm = (silu_a * b).astype(BF16)
        hm_ref[...] = hm
        hmt_ref[...] = hm.T

    return _pallas(
        kern, grid=(2, m // tm),
        in_specs=[pl.BlockSpec((tm, D), lambda j, i: (i, 0)),
                  pl.BlockSpec((None, D, half), lambda j, i: (j, 0, 0)),
                  pl.BlockSpec((None, D, half), lambda j, i: (2 + j, 0, 0))],
        out_specs=[pl.BlockSpec((2, tm, half), lambda j, i: (0, i, j)),
                   pl.BlockSpec((tm, half), lambda j, i: (i, j)),
                   pl.BlockSpec((half, tm), lambda j, i: (j, i))],
        out_shape=[jax.ShapeDtypeStruct((2, m, FF), BF16), jax.ShapeDtypeStruct((m, FF), BF16),
                   jax.ShapeDtypeStruct((FF, m), BF16)],
        scratch_shapes=[], sem=("parallel", "parallel"), name=name, args=[u, w_gu, w_gu], side=side)


def _swiglu_bwd(name, df, w_down, ab):
    m = df.shape[0]
    half = FF // 2
    tm = _pick(m, (512, 256))

    def kern(df_ref, w_ref, ab_ref, o_ref):
        dh = lax.dot_general(df_ref[...], w_ref[...], (((1,), (1,)), ((), ())), preferred_element_type=F32)
        o_ref[0] = (dh * ab_ref[0].astype(F32)).astype(BF16)
        o_ref[1] = (dh * ab_ref[1].astype(F32)).astype(BF16)

    return pl.pallas_call(
        kern, grid=(2, m // tm),
        in_specs=[pl.BlockSpec((tm, D), lambda j, i: (i, 0)),
                  pl.BlockSpec((half, D), lambda j, i: (j, 0)),
                  pl.BlockSpec((2, tm, half), lambda j, i: (0, i, j))],
        out_specs=pl.BlockSpec((2, tm, half), lambda j, i: (0, i, j)),
        out_shape=jax.ShapeDtypeStruct((2, m, FF), BF16),
        compiler_params=_cparams(("parallel", "parallel")), name=name)(df, w_down, ab)


def _gla_maps(bl, t, tc):
    nx, nc = t // CHUNK, tc // CHUNK
    nxb = bl * nx

    def rowblk(d, b, n):
        c_ctx = jnp.where(d == 0, n, nc - 1 - n)
        c_x = jnp.where(d == 0, n - nc, nx - 1 - (n - nc))
        return jnp.where(n < nc, nxb + b * nc + c_ctx, b * nx + c_x)

    def xblk(d, b, n):
        n2 = jnp.maximum(n, nc)
        return b * nx + jnp.where(d == 0, n2 - nc, nx - 1 - (n2 - nc))

    return nx, nc, rowblk, xblk


def _dot01(m, x, cm):
    x1 = x.astype(BF16)
    r1 = x - x1.astype(F32)
    x2 = r1.astype(BF16)
    x3 = (r1 - x2.astype(F32)).astype(BF16)
    w = x.shape[1]
    p = lax.dot_general(m.astype(BF16), jnp.concatenate([x1, x2, x3], axis=1), (((cm,), (0,)), ((), ())),
                        preferred_element_type=F32)
    return p[:, :w] + p[:, w:2 * w] + p[:, 2 * w:]


def _gla_chunk(m, q, k, v, g, h):
    gh = g[:, h * DK:(h + 1) * DK]
    b = _dot01(m, gh, 1)
    tot = jnp.sum(gh, axis=0, keepdims=True)
    mid = b[CHUNK // 2:CHUNK // 2 + 1, :]
    qh = q[:, h * DK:(h + 1) * DK] * Q_SCALE
    kh = k[:, h * DK:(h + 1) * DK]
    vh = v[:, h * DV:(h + 1) * DV]
    return b, tot, mid, qh, kh, vh


def _dot(a, b, ca, cb):
    return lax.dot_general(a.astype(BF16), b.astype(BF16), (((ca,), (cb,)), ((), ())),
                           preferred_element_type=F32)


def _gla_fwd(p, g2, mmats, bl, t, tc):
    nx, nc, rowblk, xblk = _gla_maps(bl, t, tc)
    ns = nx + nc

    def kern(q0, k0, v0, g0, q1, k1, v1, g1, m_ref, o0, o1, ss_ref, s_ref):
        n = pl.program_id(1)

        @pl.when(n == 0)
        def _():
            s_ref[...] = jnp.zeros(s_ref.shape, F32)
        sides = ((q0, k0, v0, g0, o0), (q1, k1, v1, g1, o1))
        loaded = [(m_ref[d], q_ref[...].astype(F32), k_ref[...].astype(F32), v_ref[...].astype(F32), g_ref[...])
                  for d, (q_ref, k_ref, v_ref, g_ref, _) in enumerate(sides)]
        chains = [(d, h) for d in range(2) for h in range(HEADS)]
        base = [_gla_chunk(*loaded[d], h) for d, h in chains]
        pre = []
        for (d, h), (b, tot, mid, qh, kh, vh) in zip(chains, base):
            s0 = s_ref[d, h * DV:(h + 1) * DV, :]
            ss_ref[d, 0, 0, h * DV:(h + 1) * DV, :] = s0.astype(BF16)
            pre.append((s0, kh * jnp.exp(tot - b), qh * jnp.exp(b), qh * jnp.exp(b - mid), kh * jnp.exp(mid - b)))
        raw = [(_dot(qm, km, 1, 1), _dot(qe, s0, 1, 1), _dot(bs[5], kl, 0, 0))
               for bs, (s0, kl, qe, qm, km) in zip(base, pre)]
        for (d, h), bs, (s0, kl, qe, qm, km), (att_raw, inter, s_add) in zip(chains, base, pre, raw):
            s_ref[d, h * DV:(h + 1) * DV, :] = s0 * jnp.exp(bs[1]) + s_add
            sides[d][4][:, h * DV:(h + 1) * DV] = inter + _dot(loaded[d][0] * att_raw, bs[5], 1, 0)

    def operands(d):
        return [pl.BlockSpec((CHUNK, 512), lambda b, n: (rowblk(d, b, n), 4)),
                pl.BlockSpec((CHUNK, 512), lambda b, n: (rowblk(d, b, n), 5)),
                pl.BlockSpec((CHUNK, 1024), lambda b, n: (rowblk(d, b, n), 3)),
                pl.BlockSpec((CHUNK, 512), lambda b, n: (rowblk(d, b, n), d))]

    o_shape = jax.ShapeDtypeStruct((bl * t, HEADS * DV), F32)
    return pl.pallas_call(
        kern, grid=(bl, ns),
        in_specs=operands(0) + operands(1) + [pl.BlockSpec((2, CHUNK, CHUNK), lambda b, n: (0, 0, 0))],
        out_specs=[pl.BlockSpec((CHUNK, 1024), lambda b, n: (xblk(0, b, n), 0)),
                   pl.BlockSpec((CHUNK, 1024), lambda b, n: (xblk(1, b, n), 0)),
                   pl.BlockSpec((2, 1, 1, HEADS * DV, DK), lambda b, n: (0, b, n, 0, 0))],
        out_shape=[o_shape, o_shape, jax.ShapeDtypeStruct((2, bl, ns, HEADS * DV, DK), BF16)],
        scratch_shapes=[pltpu.VMEM((2, HEADS * DV, DK), F32)],
        compiler_params=_cparams(("parallel", "arbitrary")), name="gla_fwd")(p, p, p, g2, p, p, p, g2, mmats)


def _gla_bwd(p, g2, mmats, ssave, do, bl, t, tc):
    nx, nc, rowblk, xblk = _gla_maps(bl, t, tc)
    ns = nx + nc
    rev = lambda s: ns - 1 - s

    def kern(q0, k0, v0, g0, do0, q1, k1, v1, g1, do1, m_ref, ss_ref,
             dq0, dk0, dv0, dg0, dq1, dk1, dv1, dg1, ds_ref):
        step = pl.program_id(1)
        n = ns - 1 - step

        @pl.when(step == 0)
        def _():
            ds_ref[...] = jnp.zeros(ds_ref.shape, F32)
        live = (n >= nc).astype(F32)
        sides = ((q0, k0, v0, g0, do0, dq0, dk0, dv0, dg0), (q1, k1, v1, g1, do1, dq1, dk1, dv1, dg1))
        loaded = [(m_ref[d], s[0][...].astype(F32), s[1][...].astype(F32), s[2][...].astype(F32), s[3][...])
                  for d, s in enumerate(sides)]
        dovs = [s[4][...] * live for s in sides]
        chains = [(d, h) for d in range(2) for h in range(HEADS)]
        base = [_gla_chunk(*loaded[d], h) for d, h in chains]
        pre = []
        for (d, h), (b, tot, mid, qh, kh, vh) in zip(chains, base):
            eb, ebm, emb, etb = jnp.exp(b), jnp.exp(b - mid), jnp.exp(mid - b), jnp.exp(tot - b)
            pre.append(dict(
                eb=eb, ebm=ebm, emb=emb, etb=etb, etot=jnp.exp(tot), qe=qh * eb, qm=qh * ebm, km=kh * emb, kl=kh * etb,
                vh=vh, doh=dovs[d][:, h * DV:(h + 1) * DV], s0=ss_ref[d, 0, 0, h * DV:(h + 1) * DV, :].astype(F32),
                ds1=ds_ref[d, h * DV:(h + 1) * DV, :]))
        first = [dict(att=_dot(c["qm"], c["km"], 1, 1), datt=_dot(c["doh"], c["vh"], 1, 1),
                      dqe=_dot(c["doh"], c["s0"], 1, 0), ds_add=_dot(c["doh"], c["qe"], 0, 0),
                      dkl=_dot(c["vh"], c["ds1"], 1, 0), dv_s=_dot(c["kl"], c["ds1"], 1, 1)) for c in pre]
        second = []
        for (d, h), c, f in zip(chains, pre, first):
            m = loaded[d][0]
            ds_ref[d, h * DV:(h + 1) * DV, :] = c["ds1"] * c["etot"] + f["ds_add"]
            att, datt = m * f["att"], m * f["datt"]
            second.append(dict(dqm=_dot(datt, c["km"], 1, 0), dkm=_dot(datt, c["qm"], 0, 0),
                               dv_a=_dot(att, c["doh"], 0, 0)))
        for (d, h), c, f, s in zip(chains, pre, first, second):
            dq_ref, dk_ref, dv_ref, dg_ref = sides[d][5:]
            dtot = c["etot"] * jnp.sum(c["ds1"] * c["s0"], axis=0, keepdims=True) + jnp.sum(
                f["dkl"] * c["kl"], axis=0, keepdims=True)
            db = f["dqe"] * c["qe"] + s["dqm"] * c["qm"] - s["dkm"] * c["km"] - f["dkl"] * c["kl"]
            dq_ref[:, h * DK:(h + 1) * DK] = ((f["dqe"] * c["eb"] + s["dqm"] * c["ebm"]) * Q_SCALE).astype(BF16)
            dk_ref[:, h * DK:(h + 1) * DK] = (s["dkm"] * c["emb"] + f["dkl"] * c["etb"]).astype(BF16)
            dv_ref[:, h * DV:(h + 1) * DV] = (s["dv_a"] + f["dv_s"]).astype(BF16)
            dg_ref[:, h * DK:(h + 1) * DK] = _dot01(loaded[d][0], db, 0) + dtot

    nt = p.shape[0]

    def operands(d):
        return [pl.BlockSpec((CHUNK, 512), lambda b, s: (rowblk(d, b, rev(s)), 4)),
                pl.BlockSpec((CHUNK, 512), lambda b, s: (rowblk(d, b, rev(s)), 5)),
                pl.BlockSpec((CHUNK, 1024), lambda b, s: (rowblk(d, b, rev(s)), 3)),
                pl.BlockSpec((CHUNK, 512), lambda b, s: (rowblk(d, b, rev(s)), d)),
                pl.BlockSpec((CHUNK, 1024), lambda b, s: (xblk(d, b, rev(s)), 0))]

    def results(d):
        row = lambda b, s: (rowblk(d, b, rev(s)), 0)
        return [pl.BlockSpec((CHUNK, 512), row), pl.BlockSpec((CHUNK, 512), row), pl.BlockSpec((CHUNK, 1024), row),
                pl.BlockSpec((CHUNK, 512), row)]

    shapes = [jax.ShapeDtypeStruct((nt, 512), BF16), jax.ShapeDtypeStruct((nt, 512), BF16),
              jax.ShapeDtypeStruct((nt, 1024), BF16), jax.ShapeDtypeStruct((nt, 512), F32)]
    out = pl.pallas_call(
        kern, grid=(bl, ns),
        in_specs=operands(0) + operands(1) + [
            pl.BlockSpec((2, CHUNK, CHUNK), lambda b, s: (0, 0, 0)),
            pl.BlockSpec((2, 1, 1, HEADS * DV, DK), lambda b, s: (0, b, rev(s), 0, 0))],
        out_specs=results(0) + results(1), out_shape=shapes + shapes,
        scratch_shapes=[pltpu.VMEM((2, HEADS * DV, DK), F32)],
        compiler_params=_cparams(("parallel", "arbitrary")), name="gla_bwd")(
            p, p, p, g2, do, p, p, p, g2, do, mmats, ssave)
    return out[:4], out[4:]


CONV_CT = 256
CONV_PAD = 16
CONV_RC = 128
CONV_HALO = 24


def _conv_fill(zp, z_ref, t):
    zp[0:CONV_PAD, :] = jnp.zeros((CONV_PAD, CONV_CT), F32)
    zp[CONV_PAD + t:2 * CONV_PAD + t, :] = jnp.zeros((CONV_PAD, CONV_CT), F32)
    zp[CONV_PAD:CONV_PAD + t, :] = z_ref[...]


def _dwconv(name, z, w, bias, bl, t, flip):
    def kern(z_ref, w_ref, b_ref, o_ref, zp):
        _conv_fill(zp, z_ref, t)
        offs = [(CONV_W - j) if flip else (j + 1) for j in range(CONV_W)]
        for r in range(0, t, CONV_RC):
            acc = jnp.broadcast_to(b_ref[...], (CONV_RC, CONV_CT))
            for rot in range(8):
                win = zp[r + rot:r + rot + CONV_RC + CONV_HALO, :]
                for j in range(CONV_W):
                    if offs[j] % 8 == rot:
                        a = offs[j] - rot
                        acc = acc + w_ref[j:j + 1, :] * win[a:a + CONV_RC, :]
            o_ref[r:r + CONV_RC, :] = acc

    return pl.pallas_call(
        kern, grid=(bl, 1024 // CONV_CT),
        in_specs=[pl.BlockSpec((t, CONV_CT), lambda b, c: (b, c)),
                  pl.BlockSpec((32, CONV_CT), lambda b, c: (0, c)),
                  pl.BlockSpec((1, CONV_CT), lambda b, c: (0, c))],
        out_specs=pl.BlockSpec((t, CONV_CT), lambda b, c: (b, c)),
        out_shape=jax.ShapeDtypeStruct(z.shape, F32),
        scratch_shapes=[pltpu.VMEM((t + 2 * CONV_PAD, CONV_CT), F32)],
        compiler_params=_cparams(("parallel", "parallel")), name=name)(z, w, bias)


def _dwconv_wgrad(z, dzc, bl, t):
    def kern(z_ref, d_ref, dw_ref, db_ref, zp):
        b = pl.program_id(1)

        @pl.when(b == 0)
        def _():
            dw_ref[...] = jnp.zeros(dw_ref.shape, F32)
            db_ref[...] = jnp.zeros(db_ref.shape, F32)
        _conv_fill(zp, z_ref, t)
        for rot in range(8):
            taps = [j for j in range(CONV_W) if (j + 1) % 8 == rot]
            accs = [jnp.zeros((8, CONV_CT), F32) for _ in taps]
            for r in range(0, t, CONV_RC):
                d = d_ref[r:r + CONV_RC, :]
                win = zp[r + rot:r + rot + CONV_RC + CONV_HALO, :]
                for k, j in enumerate(taps):
                    a = j + 1 - rot
                    prod = d * win[a:a + CONV_RC, :]
                    accs[k] = accs[k] + jnp.sum(prod.reshape(CONV_RC // 8, 8, CONV_CT), axis=0)
            for k, j in enumerate(taps):
                dw_ref[j:j + 1, :] += jnp.sum(accs[k], axis=0, keepdims=True)
        db_ref[...] += jnp.sum(d_ref[...], axis=0, keepdims=True)

    return pl.pallas_call(
        kern, grid=(1024 // CONV_CT, bl),
        in_specs=[pl.BlockSpec((t, CONV_CT), lambda c, b: (b, c)),
                  pl.BlockSpec((t, CONV_CT), lambda c, b: (b, c))],
        out_specs=[pl.BlockSpec((32, CONV_CT), lambda c, b: (0, c)),
                   pl.BlockSpec((1, CONV_CT), lambda c, b: (0, c))],
        out_shape=[jax.ShapeDtypeStruct((32, 1024), F32), jax.ShapeDtypeStruct((1, 1024), F32)],
        scratch_shapes=[pltpu.VMEM((t + 2 * CONV_PAD, CONV_CT), F32)],
        compiler_params=_cparams(("parallel", "arbitrary")), name="dwconv_wgrad")(z, dzc)


def _ffn_fwd(tag, xin, n_tiles, g, sh, sc, gate, w_gu, w_down, tpb, nb, comm=None):
    rows = n_tiles * TM
    rm = functools.partial(_rowmap, tpb=tpb, nb=nb)
    u, ut = rm(tag + "_norm", lambda i, x, g_, sh_, sc_: _twice(_modnorm(x, g_, sh_, sc_)), n_tiles,
               [tok(xin), const(g), mod(sh), mod(sc)], [("tok", rows, D, BF16), ("tokT", rows, D, BF16)])
    ab, hm, hmt = _with_side(comm, tag + "_gu", None, lambda s: _swiglu_fwd(tag + "_gu", u, w_gu, side=s))
    res = dict(x=xin, gate=gate, scale=0.5, tpb=tpb)
    f, xout = _with_side(comm, tag + "_down", None,
                         lambda s: _mm(tag + "_down", hm, w_down, out_dtype=BF16, side=s, residual=res))
    return xout, (ut, ab, hmt, f)


def _ffn_bwd(tag, xin, saved, dxout, dx_clamp, n_tiles, g, sh, sc, gate, w_gu, w_down, tpb, nb, comm=None, grads=None,
             names=None, keep_tiles=None):
    ut, ab, hmt, f = saved
    rows = n_tiles * TM
    rm = functools.partial(_rowmap, tpb=tpb, nb=nb)

    def mask(i):
        return 1.0 if dx_clamp is None else (i <= dx_clamp).astype(F32)

    def b1(i, dx, f_, gt):
        dx = dx * mask(i)
        return (0.5 * gt * dx, jnp.sum(0.5 * f_ * dx, axis=0, keepdims=True))
    df, dgate = rm(tag + "_bres", b1, n_tiles, [tok(dxout, clamp=dx_clamp), tok(f), mod(gate)],
                   [("tok", rows, D, BF16), ("modacc", D)])
    grads[names[1]] = _wgrad(tag + "_wdown", hmt, df)
    dab = _swiglu_bwd(tag + "_bdown", df, w_down, ab)
    grads[names[0]] = _with_side(comm, tag + "_wgu", grads,
                                 lambda s: _wgrad(tag + "_wgu", ut, dab, col_shards=True, side=s))
    du = _with_side(comm, tag + "_bgu", grads, lambda s: _mm(tag + "_bgu", dab, w_gu, trans_b=True, side=s))

    def b3(i, x, g_, sh_, sc_, du_, dx):
        _, vjp = jax.vjp(_modnorm, x, g_, sh_, sc_)
        dxn, dg, dsh, dsc = vjp(du_)
        return (dx * mask(i) + dxn, dg, dsh, dsc)
    dx_out = ("tok", rows, D, F32) if keep_tiles is None else ("tok_head", keep_tiles * TM, D, F32, keep_tiles - 1)
    dxin, dg, dsh, dsc = rm(tag + "_bnorm", b3, n_tiles,
                            [tok(xin), const(g), mod(sh), mod(sc), tok(du), tok(dxout, clamp=dx_clamp)],
                            [dx_out, ("acc", 1, D), ("modacc", D), ("modacc", D)])
    return dxin, dict(g=dg, sh=dsh, sc=dsc, gate=dgate)


def _perm_in_cols(w):
    pad = jnp.zeros(w.shape[:-1] + (D_INP - D_IN,), w.dtype)
    return jnp.concatenate([w[..., :4096], w[..., 5152:7200], w[..., 4096:5120], w[..., 5120:5152], pad], axis=-1)


def _unperm_in_cols(w):
    return jnp.concatenate([w[..., :4096], w[..., 6144:LR_COL], w[..., LR_COL:LR_COL + 32], w[..., 4096:6144]], axis=-1)


def _local_step(x, c, ctx, target, wts, comm=None):
    bl, t, _ = x.shape
    tc = ctx.shape[1]
    nx_rows, nc_rows = bl * t, bl * tc
    nt_rows = nx_rows + nc_rows
    tpb = t // TM
    nxt, ntt = nx_rows // TM, nt_rows // TM
    nb = bl
    rm = functools.partial(_rowmap, tpb=tpb, nb=nb)
    last_x = nxt - 1

    x0 = jnp.concatenate([x.reshape(nx_rows, D), ctx.reshape(nc_rows, D)], axis=0)
    tgt = target.reshape(nx_rows, D)

    cc = jnp.concatenate([c, wts["c_ctx"].reshape(1, D), jnp.zeros((8 - bl - 1, D), F32)], axis=0)
    modv = _mm("mod_fwd", cc, wts["w_mod"], a_fn=_silu, bias=wts["b_mod"])
    mods = [modv[:nb + 1, k * D:(k + 1) * D].reshape(nb + 1, 1, D) for k in range(9)]

    x1, sv1 = _ffn_fwd("ffn1", x0, ntt, wts["g_ffn1"], mods[0], mods[1], mods[2], wts["w1_gu"], wts["w1_down"],
                       tpb, nb, comm)
    u2, u2t = rm("in_norm", lambda i, x_, g_, sh_, sc_: _twice(_modnorm(x_, g_, sh_, sc_)), ntt,
                 [tok(x1), const(wts["g_mix"]), mod(mods[3]), mod(mods[4])],
                 [("tok", nt_rows, D, BF16), ("tokT", nt_rows, D, BF16)])
    w_inp = wts["w_in_p"]
    p = _with_side(comm, "in_proj", None, lambda s: _mm("in_proj", u2, w_inp, out_dtype=BF16, side=s))

    waf, wab, baf, bab = wts["w_alpha_f_pad"], wts["w_alpha_b_pad"], wts["b_alpha_f"], wts["b_alpha_b"]

    def dec_fwd(i, lr, wf, wb, bf_, bb_):
        zf = _dot(lr, wf, 1, 0) + bf_
        zb = _dot(lr, wb, 1, 0) + bb_
        return (jnp.concatenate([jax.nn.log_sigmoid(zf) / TAU, jax.nn.log_sigmoid(zb) / TAU], axis=1),)
    (gfb,) = rm("decay_fwd", dec_fwd, ntt, [tok(p, 128, LR_COL // 128), const(waf), const(wab), const(baf), const(bab)],
                [("tok", nt_rows, 1024, F32)])
    g2 = gfb
    tri = jnp.tril(jnp.ones((CHUNK, CHUNK), F32))
    mmats = jnp.stack([tri, tri.T])
    *o2, ssave = _gla_fwd(p, g2, mmats, bl, t, tc)

    gn_g = wts["gla_norm_g"]

    def gla_out(of, ob, og, gn):
        o = of + ob
        parts = []
        for h in range(HEADS):
            oh = o[:, h * DV:(h + 1) * DV]
            parts.append(oh * lax.rsqrt(jnp.mean(oh * oh, axis=-1, keepdims=True) + EPS))
        return jnp.concatenate(parts, axis=1) * gn * _silu(og)
    yg_in, yg_int = rm("gla_out", lambda i, of, ob, og, gn: _twice(gla_out(of, ob, og, gn)), nxt,
                       [tok(o2[0]), tok(o2[1]), tok(p, 1024, OG_CB), const(gn_g)],
                       [("tok", nx_rows, D, BF16), ("tokT", nx_rows, D, BF16)])
    y_gla = _with_side(comm, "gla_proj", None,
                       lambda s: _mm("gla_proj", yg_in, wts["w_gla_out"], out_dtype=BF16, side=s))

    (z,) = rm("glu", lambda i, a, b: (a * jax.nn.sigmoid(b),), nxt, [tok(p, 1024, 0), tok(p, 1024, 1)],
              [("tok", nx_rows, D, F32)])
    dw_w = jnp.concatenate([wts["dw_weight"], jnp.zeros((1, D), F32)], axis=0)
    zc = _dwconv("dwconv_fwd", z, dw_w, wts["dw_bias"], bl, t, False)

    def ln_silu(zc_, g_, b_):
        mu = jnp.mean(zc_, axis=-1, keepdims=True)
        var = jnp.mean(jnp.square(zc_ - mu), axis=-1, keepdims=True)
        return _silu((zc_ - mu) * lax.rsqrt(var + EPS) * g_ + b_)
    ln_g, ln_b = wts["conv_ln_g"], wts["conv_ln_b"]
    zl, zlt = rm("conv_ln", lambda i, zc_, g_, b_: _twice(ln_silu(zc_, g_, b_)), nxt,
                 [tok(zc), const(ln_g), const(ln_b)], [("tok", nx_rows, D, BF16), ("tokT", nx_rows, D, BF16)])
    y_conv = _mm("conv_proj", zl, wts["w_conv_out"], out_dtype=BF16)

    mg, mgt = rm("merge", lambda i, ga, gb, yc, yg: _twice(jax.nn.sigmoid(ga) * yc + jax.nn.sigmoid(gb) * yg), nxt,
                 [tok(p, 1024, GA_CB), tok(p, 1024, GB_CB), tok(y_conv), tok(y_gla)],
                 [("tok", nx_rows, D, BF16), ("tokT", nx_rows, D, BF16)])
    mix, x2 = _mm("out_proj", mg, wts["w_out"], out_dtype=BF16,
                  residual=dict(x=x1, gate=mods[5], scale=1.0, tpb=tpb))

    x3, sv2 = _ffn_fwd("ffn2", x2, nxt, wts["g_ffn2"], mods[6], mods[7], mods[8], wts["w2_gu"], wts["w2_down"],
                       tpb, nb)
    g_fin = wts["g_final"].reshape(1, D)

    def head(i, x_, g_, tg):
        y, vjp = jax.vjp(_rms, x_, g_)
        diff = y - tg
        dx, dg = vjp(diff * (1.0 / D))
        loss = 0.5 * jnp.sum(jnp.mean(diff * diff, axis=-1, keepdims=True))
        return dx, dg, loss
    dx3, dg_final, loss_acc = rm("loss_head", head, nxt, [tok(x3), const(g_fin), tok(tgt)],
                                 [("tok", nx_rows, D, F32), ("acc", 1, D), ("acc", 8, 128)])
    loss = loss_acc[0, 0]

    grads = {}
    dx2, gf2 = _ffn_bwd("ffn2", x2, sv2, dx3, None, nxt, wts["g_ffn2"], mods[6], mods[7], mods[8],
                        wts["w2_gu"], wts["w2_down"], tpb, nb, comm, grads, ("w2_gu", "w2_down"))
    grads["g_ffn2"] = gf2["g"]

    dmix, dgate5 = rm("mix_bres", lambda i, dx, mx_, gt: (gt * dx, jnp.sum(mx_ * dx, axis=0, keepdims=True)), nxt,
                      [tok(dx2), tok(mix), mod(mods[5])], [("tok", nx_rows, D, BF16), ("modacc", D)])
    dmg = _mm("out_bproj", dmix, wts["w_out"], trans_b=True)
    grads["w_out"] = _wgrad("out_wgrad", mgt, dmix)

    def merge_bwd(i, dm, ga, gb, yc, yg):
        keep = (i <= last_x).astype(F32)
        dm = dm * keep
        sa, sb = jax.nn.sigmoid(ga), jax.nn.sigmoid(gb)
        return dm * sa, dm * sb, jnp.concatenate([dm * yc * sa * (1 - sa), dm * yg * sb * (1 - sb)], axis=1)
    cl = dict(clamp=last_x)
    dyc, dyg, dp = rm("merge_bwd", merge_bwd, ntt,
                      [tok(dmg, **cl), tok(p, 1024, GA_CB, last_x), tok(p, 1024, GB_CB, last_x), tok(y_conv, **cl),
                       tok(y_gla, **cl)],
                      [("tok", nt_rows, D, BF16), ("tok", nt_rows, D, BF16),
                       ("cols", jax.ShapeDtypeStruct((nt_rows, D_INP), BF16), 2048, 2)])

    dzl = _mm("conv_bproj", dyc, wts["w_conv_out"], trans_b=True, rows=nx_rows)
    grads["w_conv_out"] = _wgrad("conv_wgrad", zlt, dyc, rows=nx_rows)

    def ln_bwd(i, zc_, g_, b_, dz_):
        _, vjp = jax.vjp(ln_silu, zc_, g_, b_)
        return vjp(dz_)
    dzc, dln_g, dln_b = rm("conv_ln_bwd", ln_bwd, nxt, [tok(zc), const(ln_g), const(ln_b), tok(dzl)],
                           [("tok", nx_rows, D, F32), ("acc", 1, D), ("acc", 1, D)])
    dz = _dwconv("dwconv_bwd", dzc, dw_w, jnp.zeros((1, D), F32), bl, t, True)
    ddw, ddb = _dwconv_wgrad(z, dzc, bl, t)
    grads.update(conv_ln_g=dln_g, conv_ln_b=dln_b, dw_weight=ddw[:CONV_W], dw_bias=ddb)

    def glu_bwd(i, dz_, a, b):
        keep = (i <= last_x).astype(F32)
        dz_ = dz_ * keep
        s = jax.nn.sigmoid(b)
        return (jnp.concatenate([dz_ * s, dz_ * a * s * (1 - s)], axis=1),)
    (dp,) = rm("glu_bwd", glu_bwd, ntt, [tok(dz, **cl), tok(p, 1024, 0, last_x), tok(p, 1024, 1, last_x)],
               [("cols", dp, 2048, 0)])

    dyg_in = _mm("gla_bproj", dyg, wts["w_gla_out"], trans_b=True, rows=nx_rows)
    grads["w_gla_out"] = _wgrad("gla_wgrad", yg_int, dyg, rows=nx_rows)

    def gla_out_bwd(i, of, ob, og, gn, dy):
        _, vjp = jax.vjp(gla_out, of, ob, og, gn)
        do_, _, dog_, dgn_ = vjp(dy)
        return do_, dog_, dgn_
    do, dp, dgn = rm("gla_out_bwd", gla_out_bwd, nxt,
                     [tok(o2[0]), tok(o2[1]), tok(p, 1024, OG_CB), const(gn_g), tok(dyg_in)],
                     [("tok", nx_rows, D, F32), ("cols", dp, 1024, OG_CB), ("acc", 1, D)])
    grads["gla_norm_g"] = dgn
    (dp,) = rm("og_ctx_zero", lambda i: (jnp.zeros((TM, D), F32),), ntt - nxt, [], [("cols", dp, 1024, OG_CB, nxt)])

    dq2, dk2, dv2, dg2 = zip(*_gla_bwd(p, g2, mmats, ssave, do, bl, t, tc))

    def dec_bwd(i, lr, wf, wb, bf_, bb_, dgf, dgb_):
        zf = _dot(lr, wf, 1, 0) + bf_
        zb = _dot(lr, wb, 1, 0) + bb_
        dzf = dgf * (1 - jax.nn.sigmoid(zf)) * (1.0 / TAU)
        dzb = dgb_ * (1 - jax.nn.sigmoid(zb)) * (1.0 / TAU)
        dlr = _dot(dzf, wf, 1, 1) + _dot(dzb, wb, 1, 1)
        return (dlr, _dot(lr, dzf, 0, 0), _dot(lr, dzb, 0, 0), jnp.sum(dzf, axis=0, keepdims=True),
                jnp.sum(dzb, axis=0, keepdims=True))
    dp, dwaf, dwab, dbaf, dbab = rm(
        "decay_bwd", dec_bwd, ntt,
        [tok(p, 128, LR_COL // 128), const(waf), const(wab), const(baf), const(bab), tok(dg2[0]), tok(dg2[1])],
        [("cols", dp, 128, LR_COL // 128), ("acc", 128, 512), ("acc", 128, 512), ("acc", 1, 512), ("acc", 1, 512)])
    grads.update(w_alpha_f=dwaf[:LOWRANK], w_alpha_b=dwab[LOWRANK:2 * LOWRANK], b_alpha_f=dbaf, b_alpha_b=dbab)

    (dp,) = rm("gla_sum",
               lambda i, q0, q1, k0, k1, v0, v1: (jnp.concatenate([q0 + q1, k0 + k1, v0 + v1], axis=1),), ntt,
               [tok(dq2[0]), tok(dq2[1]), tok(dk2[0]), tok(dk2[1]), tok(dv2[0]), tok(dv2[1])],
               [("cols", dp, 2048, 1)])
    du2 = _with_side(comm, "in_bproj", grads, lambda s: _mm("in_bproj", dp, w_inp, trans_b=True, side=s))
    grads["w_in_p"] = _wgrad("in_wgrad", u2t, dp)

    def in_norm_bwd(i, x_, g_, sh_, sc_, du_, dx):
        keep = (i <= last_x).astype(F32)
        _, vjp = jax.vjp(_modnorm, x_, g_, sh_, sc_)
        dxn, dg, dsh, dsc = vjp(du_)
        return (dx * keep + dxn, dg, dsh, dsc)
    dx1, dg_mix, dsh3, dsc4 = rm("in_norm_bwd", in_norm_bwd, ntt,
                                 [tok(x1), const(wts["g_mix"]), mod(mods[3]), mod(mods[4]), tok(du2), tok(dx2, **cl)],
                                 [("tok", nt_rows, D, F32), ("acc", 1, D), ("modacc", D), ("modacc", D)])
    grads["g_mix"] = dg_mix

    dx0, gf1 = _ffn_bwd("ffn1", x0, sv1, dx1, None, ntt, wts["g_ffn1"], mods[0], mods[1], mods[2],
                        wts["w1_gu"], wts["w1_down"], tpb, nb, comm, grads, ("w1_gu", "w1_down"), keep_tiles=nxt)
    grads["g_ffn1"] = gf1["g"]
    grad_x = dx0.reshape(bl, t, D)

    dmods = [gf1["sh"], gf1["sc"], gf1["gate"], dsh3, dsc4, dgate5, gf2["sh"], gf2["sc"], gf2["gate"]]
    dmod = jnp.concatenate(
        [jnp.concatenate([a.reshape(a.shape[0], D), jnp.zeros((8 - a.shape[0], D), F32)], axis=0) for a in dmods],
        axis=1)
    dsc_ = _mm("mod_bproj", dmod, wts["w_mod"], trans_b=True)

    def silu_bwd(cc_ref, d_ref, dm_ref, dcc_ref, scc_ref, db_ref):
        cc_ = cc_ref[...]
        s = jax.nn.sigmoid(cc_)
        dcc_ref[...] = d_ref[...] * (s * (1 + cc_ * (1 - s)))
        scc_ref[...] = (cc_ * s).astype(BF16)
        db_ref[...] = jnp.sum(dm_ref[...], axis=0, keepdims=True)
    dcc, scc, db_mod = pl.pallas_call(
        silu_bwd, out_shape=[jax.ShapeDtypeStruct((8, D), F32), jax.ShapeDtypeStruct((8, D), BF16),
                             jax.ShapeDtypeStruct((1, 9 * D), F32)], name="mod_silu_bwd")(cc, dsc_, dmod)
    grads["c_ctx"] = dcc[nb]
    grads["b_mod"] = db_mod
    grads["w_mod"] = _mm_tn("mod_wgrad", scc, dmod, col_shards=True)
    grads["g_final"] = dg_final.reshape(D)
    return loss, grad_x, grads


ANY = pl.BlockSpec(memory_space=pl.ANY)


def _place():
    x, y, c = lax.axis_index("x"), lax.axis_index("y"), lax.axis_index("c")
    chips = [(1 - x, y), (x, 1 - y), (1 - x, 1 - y)]
    return x, y, c, chips


def _remote(send_sems, recv_sems):
    def copy(k, src, dst, to):
        return pltpu.make_async_remote_copy(src_ref=src, dst_ref=dst, send_sem=send_sems.at[k],
                                            recv_sem=recv_sems.at[k], device_id=to, device_id_type=MESH)
    return copy


def _sems(n):
    return [pltpu.SemaphoreType.DMA((n,)), pltpu.SemaphoreType.DMA((n,))]


def _gather_weights(shards):
    n = len(shards)

    def body(*refs):
        ins, outs = refs[:n], refs[n:2 * n]
        copy = _remote(refs[2 * n], refs[2 * n + 1])
        x, y, c, chips = _place()
        me = 2 * x + y
        sibling = (x, y, 1 - c)
        started = []

        def rows(i, hc):
            hr = ins[i].shape[0] // 2
            return pl.ds(hc * hr, hr)

        for i in range(n):
            started.append(copy(7 * i + 6, ins[i], outs[i].at[me], sibling))
            started[-1].start()
            for j, (px, py) in enumerate(chips):
                started.append(copy(7 * i + j, ins[i].at[rows(i, c), :], outs[i].at[me, rows(i, c), :], (px, py, c)))
                started[-1].start()
        for i in range(n):
            for j, (px, py) in enumerate(chips):
                half = outs[i].at[2 * px + py, rows(i, c), :]
                copy(7 * i + j, half, half, (px, py, c)).wait_recv()
                started.append(copy(7 * i + 3 + j, half, half, sibling))
                started[-1].start()
        for i in range(n):
            copy(7 * i + 6, ins[i], outs[i].at[me], sibling).wait_recv()
            for j, (px, py) in enumerate(chips):
                other = outs[i].at[2 * px + py, rows(i, 1 - c), :]
                copy(7 * i + 3 + j, other, other, sibling).wait_recv()
        for cp in started:
            cp.wait_send()

    return pl.pallas_call(
        body, out_shape=[jax.ShapeDtypeStruct((4,) + s.shape, s.dtype) for s in shards], in_specs=[ANY] * n,
        out_specs=[ANY] * n, scratch_shapes=_sems(7 * n), name="gather_weights")(*shards)


def _swap_halves(name, gs):
    n = len(gs)

    def body(*refs):
        ins, outs = refs[:n], refs[n:2 * n]
        copy = _remote(refs[2 * n], refs[2 * n + 1])
        x, y, c, _ = _place()
        cps = []
        for i in range(n):
            hr = ins[i].shape[1] // 2
            cps.append(copy(i, ins[i].at[:, pl.ds((1 - c) * hr, hr), :], outs[i], (x, y, 1 - c)))
            cps[-1].start()
        for cp in cps:
            cp.wait()

    return pl.pallas_call(
        body, out_shape=[jax.ShapeDtypeStruct((4, g.shape[1] // 2, g.shape[2]), g.dtype) for g in gs],
        in_specs=[ANY] * n, out_specs=[ANY] * n, scratch_shapes=_sems(n), name=name)(*gs)


def _row_tile(hr):
    return hr if hr <= 256 else _pick(hr, (256, 176, 128, 64, 32, 16))


def _add_halves(name, g, r, place):
    hr = r.shape[1]
    tr = _row_tile(hr)
    nblk = hr // tr

    def kern(p_ref, g_ref, r_ref, o_ref):
        o_ref[...] = (g_ref[...].astype(F32) + r_ref[...].astype(F32)).astype(o_ref.dtype)

    blk = (1, tr, g.shape[2])
    return pl.pallas_call(
        kern,
        grid_spec=pltpu.PrefetchScalarGridSpec(
            num_scalar_prefetch=1, grid=(4, nblk),
            in_specs=[pl.BlockSpec(blk, lambda j, i, p: (j, p[0] * nblk + i, 0)),
                      pl.BlockSpec(blk, lambda j, i, p: (j, i, 0))],
            out_specs=pl.BlockSpec(blk, lambda j, i, p: (j, i, 0))),
        out_shape=jax.ShapeDtypeStruct(r.shape, r.dtype),
        compiler_params=_cparams(("parallel", "parallel")), name=name)(place, g, r)


def _scatter_chips(cs):
    n = len(cs)

    def body(*refs):
        ins, outs = refs[:n], refs[n:2 * n]
        copy = _remote(refs[2 * n], refs[2 * n + 1])
        x, y, c, chips = _place()
        me = 2 * x + y
        sends = []
        for i in range(n):
            for j, (px, py) in enumerate(chips):
                sends.append(copy(3 * i + j, ins[i].at[2 * px + py], outs[i].at[me], (px, py, c)))
                sends[-1].start()
        for i in range(n):
            for j, (px, py) in enumerate(chips):
                src = 2 * px + py
                copy(3 * i + j, ins[i].at[src], outs[i].at[src], (px, py, c)).wait_recv()
        for cp in sends:
            cp.wait_send()

    return pl.pallas_call(
        body, out_shape=[jax.ShapeDtypeStruct(a.shape, a.dtype) for a in cs], in_specs=[ANY] * n,
        out_specs=[ANY] * n, scratch_shapes=_sems(3 * n), name="grad_scatter_chips")(*cs)


def _sum_chips(name, cs, r, place):
    hr = r.shape[1]
    tr = _row_tile(hr)
    nblk = hr // tr

    def kern(p_ref, c_ref, r0, r1, r2, r3, o_ref):
        me = p_ref[1]
        acc = None
        for k, rk in enumerate((r0, r1, r2, r3)):
            val = jnp.where(me == k, c_ref[0].astype(F32), rk[0].astype(F32))
            acc = val if acc is None else acc + val
        o_ref[...] = acc

    blk = (1, tr, r.shape[2])

    def slot(k):
        return lambda i, p: (jnp.where(p[1] == k, (k + 1) % 4, k), i, 0)

    return pl.pallas_call(
        kern,
        grid_spec=pltpu.PrefetchScalarGridSpec(
            num_scalar_prefetch=1, grid=(nblk,),
            in_specs=[pl.BlockSpec(blk, lambda i, p: (p[1], i, 0))] + [pl.BlockSpec(blk, slot(k)) for k in range(4)],
            out_specs=pl.BlockSpec((tr, r.shape[2]), lambda i, p: (p[0] * nblk + i, 0))),
        out_shape=jax.ShapeDtypeStruct((2 * hr, r.shape[2]), F32),
        compiler_params=_cparams(("parallel",)), name=name)(place, cs, r, r, r, r)


def _join_halves(fs):
    n = len(fs)

    def body(*refs):
        ins, outs = refs[:n], refs[n:2 * n]
        copy = _remote(refs[2 * n], refs[2 * n + 1])
        x, y, c, _ = _place()
        cps = []
        for i in range(n):
            hr = ins[i].shape[0] // 2
            cps.append(copy(i, ins[i].at[pl.ds(c * hr, hr), :], outs[i].at[pl.ds(c * hr, hr), :], (x, y, 1 - c)))
            cps[-1].start()
        for i in range(n):
            hr = ins[i].shape[0] // 2
            other = outs[i].at[pl.ds((1 - c) * hr, hr), :]
            copy(i, other, other, (x, y, 1 - c)).wait_recv()
        for cp in cps:
            cp.wait_send()

    return pl.pallas_call(
        body, out_shape=[jax.ShapeDtypeStruct(f.shape, f.dtype) for f in fs], in_specs=[ANY] * n,
        out_specs=[ANY] * n, input_output_aliases={i: i for i in range(n)}, scratch_shapes=_sems(n),
        name="grad_join_halves")(*fs)


def _half_rows(n, hc):
    return pl.ds(hc * (n // 2), n // 2)


def _gather_ici_side(shards):
    n = len(shards)

    def copies(ins, outs, copy):
        x, y, c, chips = _place()
        me = 2 * x + y
        for i in range(n):
            rows = _half_rows(ins[i].shape[0], c)
            for j, (px, py) in enumerate(chips):
                yield (copy(3 * i + j, ins[i].at[rows, :], outs[i].at[me, rows, :], (px, py, c)),
                       outs[i].at[2 * px + py, rows, :])

    def start(ins, outs, copy):
        for cp, _ in copies(ins, outs, copy):
            cp.start()

    def finish(ins, outs, copy):
        x, y, c, chips = _place()
        k = 0
        for cp, landing in copies(ins, outs, copy):
            copy(k, landing, landing, (x, y, c)).wait_recv()
            k += 1
        for cp, _ in copies(ins, outs, copy):
            cp.wait_send()

    return dict(ins=list(shards), outs=[jax.ShapeDtypeStruct((4,) + s.shape, s.dtype) for s in shards], nsem=3 * n,
                start=start, finish=finish)


def _gather_d2d_side(shards, bufs):
    n = len(shards)

    def copies(ins, outs, copy):
        x, y, c, chips = _place()
        me = 2 * x + y
        sibling = (x, y, 1 - c)
        for i in range(n):
            a = ins[i].shape[0]
            yield copy(4 * i + 3, ins[i], outs[i].at[me], sibling), outs[i].at[me]
            for j, (px, py) in enumerate(chips):
                src = 2 * px + py
                mine = outs[i].at[src, _half_rows(a, c), :]
                yield copy(4 * i + j, mine, mine, sibling), outs[i].at[src, _half_rows(a, 1 - c), :]

    def start(ins, outs, copy):
        for cp, _ in copies(ins, outs, copy):
            cp.start()

    def finish(ins, outs, copy):
        x, y, c, _ = _place()
        for i in range(n):
            for k, (cp, landing) in enumerate(list(copies(ins, outs, copy))[4 * i:4 * i + 4]):
                sem = 4 * i + 3 if k == 0 else 4 * i + k - 1
                copy(sem, landing, landing, (x, y, 1 - c)).wait_recv()
        for cp, _ in copies(ins, outs, copy):
            cp.wait_send()

    return dict(ins=list(shards) + list(bufs), outs=[jax.ShapeDtypeStruct(b.shape, b.dtype) for b in bufs],
                nsem=4 * n, alias={n + i: i for i in range(n)}, start=start, finish=finish)


def _scatter_side(cs):
    n = len(cs)

    def copies(ins, outs, copy):
        x, y, c, chips = _place()
        me = 2 * x + y
        for i in range(n):
            for j, (px, py) in enumerate(chips):
                yield copy(3 * i + j, ins[i].at[2 * px + py], outs[i].at[me], (px, py, c)), outs[i].at[2 * px + py]

    def start(ins, outs, copy):
        for cp, _ in copies(ins, outs, copy):
            cp.start()

    def finish(ins, outs, copy):
        x, y, c, _ = _place()
        for k, (cp, landing) in enumerate(copies(ins, outs, copy)):
            copy(k, landing, landing, (x, y, c)).wait_recv()
        for cp, _ in copies(ins, outs, copy):
            cp.wait_send()

    return dict(ins=list(cs), outs=[jax.ShapeDtypeStruct(a.shape, a.dtype) for a in cs], nsem=3 * n,
                start=start, finish=finish)


def _small_pack(dw, af, ab):
    return jnp.concatenate([dw, jnp.zeros((1, dw.shape[1]), F32), jnp.concatenate([af, ab], axis=1)], axis=0)


def _grad_pieces(n, grads):
    if n == "small":
        return jnp.stack([_small_pack(grads["dw_weight"][:, 256 * j:256 * (j + 1)],
                                      grads["w_alpha_f"][:, DK * j:DK * (j + 1)],
                                      grads["w_alpha_b"][:, DK * j:DK * (j + 1)]) for j in range(4)])
    if n == "w_in":
        g = _unperm_in_cols(grads["w_in_p"])
        return jnp.transpose(g.reshape(D, 4, D_IN // 4), (1, 0, 2))
    g = grads[n]
    return g if g.ndim == 3 else g.reshape(4, g.shape[0] // 4, g.shape[1])


def _chip_sums(tag, names, grads, place):
    gs = [_grad_pieces(n, grads) for n in names]
    swapped = _swap_halves("grad_swap_" + tag, gs)
    return [_add_halves("grad_add_" + n, g, r, place) for n, g, r in zip(names, gs, swapped)]


class _Overlap:
    SCATTER = {"in_bproj": ("w2_down", "w2_gu", "w_out"),
               "ffn1_wgu": ("w_conv_out", "w_gla_out", "w_in", "small"),
               "ffn1_bgu": ("w1_down", "w1_gu")}
    GATHER = {("ffn1_gu", "ffn1_down"): ("w_in", "w_conv_out", "w_gla_out", "w_out"),
              ("in_proj", "gla_proj"): ("w2_gu", "w2_down")}
    LATE = tuple(n for names in GATHER.values() for n in names)

    def __init__(self, shard_of, install, place):
        self.shard_of, self.install, self.place = shard_of, install, place
        self.bufs, self.pending, self.landed = {}, None, {}

    def side(self, tag, grads):
        for (ici, d2d), names in self.GATHER.items():
            shards = [self.shard_of(n) for n in names]
            if tag == ici:
                return _gather_ici_side(shards)
            if tag == d2d:
                return _gather_d2d_side(shards, self.bufs[ici])
        if tag in self.SCATTER:
            names = self.SCATTER[tag]
            self.pending = (names, _chip_sums(tag, names, grads, self.place))
            return _scatter_side(self.pending[1])
        return None

    def done(self, tag, outs):
        for (ici, d2d), names in self.GATHER.items():
            if tag == ici:
                self.bufs[ici] = outs
                return
            if tag == d2d:
                self.install(dict(zip(names, outs)))
                return
        for n, cs, r in zip(*self.pending, outs):
            self.landed[n] = (cs, r)


def _with_side(comm, tag, grads, call):
    side = comm.side(tag, grads) if comm is not None else None
    if side is None:
        return call(None)
    res, outs = call(side)
    comm.done(tag, outs)
    return res


def _allreduce_small(v):
    def body(x_ref, out_ref, gath, send_sems, recv_sems, local_sem):
        x, y, c, chips = _place()
        me, sibling = (x, y, c), (x, y, 1 - c)

        def slot(px, py, pc):
            return gath.at[4 * px + 2 * py + pc]

        def copy(k, block, to, src=None):
            return pltpu.make_async_remote_copy(
                src_ref=slot(*block) if src is None else src, dst_ref=slot(*block), send_sem=send_sems.at[k],
                recv_sem=recv_sems.at[k], device_id=to, device_id_type=MESH)

        mine = pltpu.make_async_copy(x_ref, slot(*me), local_sem)
        mine.start()
        first = [copy(0, me, sibling, src=x_ref)]
        first += [copy(1 + j, me, (*chip, c), src=x_ref) for j, chip in enumerate(chips)]
        for cp in first:
            cp.start()
        passed = [copy(4 + j, (*chip, c), sibling) for j, chip in enumerate(chips)]
        for j, chip in enumerate(chips):
            copy(1 + j, (*chip, c), me).wait_recv()
            passed[j].start()
        copy(0, sibling, me).wait_recv()
        for j, chip in enumerate(chips):
            copy(4 + j, (*chip, 1 - c), me).wait_recv()
        for cp in first + passed:
            cp.wait_send()
        mine.wait()
        acc = gath[0]
        for k in range(1, 8):
            acc = acc + gath[k]
        out_ref[...] = acc

    vm = pl.BlockSpec(memory_space=pltpu.VMEM)
    return pl.pallas_call(
        body, out_shape=jax.ShapeDtypeStruct(v.shape, F32), in_specs=[vm], out_specs=vm,
        scratch_shapes=[pltpu.VMEM((8,) + v.shape, F32), pltpu.SemaphoreType.DMA((7,)),
                        pltpu.SemaphoreType.DMA((7,)), pltpu.SemaphoreType.DMA(())],
        name="allreduce_small")(v)


def _adamw(name, w, g, m, v):
    r, cols = w.shape
    budget = 262144
    tr = r if r * cols <= budget else next(c for c in (256, 128, 64, 32, 16, 8) if r % c == 0 and c * cols <= budget)

    def kern(w_ref, g_ref, m_ref, v_ref, d_ref, nm_ref, nv_ref):
        gv = g_ref[...]
        nm = ADAM_B1 * m_ref[...] + (1.0 - ADAM_B1) * gv
        nv = ADAM_B2 * v_ref[...] + (1.0 - ADAM_B2) * jnp.square(gv)
        m_hat = nm / (1.0 - ADAM_B1 ** ADAM_STEP)
        v_hat = nv / (1.0 - ADAM_B2 ** ADAM_STEP)
        d_ref[...] = -ADAM_LR * (m_hat / (jnp.sqrt(v_hat) + ADAM_EPS) + ADAM_WD * w_ref[...])
        nm_ref[...] = nm
        nv_ref[...] = nv

    spec = pl.BlockSpec((tr, cols), lambda i: (i, 0))
    shp = jax.ShapeDtypeStruct((r, cols), F32)
    return pl.pallas_call(kern, grid=(r // tr,), in_specs=[spec] * 4, out_specs=[spec] * 3, out_shape=[shp] * 3,
                          compiler_params=_cparams(("parallel",)), name=name)(w, g, m, v)


SHARDED = (("w_mod", 1), ("w1_gu", 1), ("w1_down", 0), ("w_in", 1), ("dw_weight", 1), ("w_conv_out", 0),
           ("w_alpha_f", 1), ("w_alpha_b", 1), ("w_gla_out", 0), ("w_out", 0), ("w2_gu", 1), ("w2_down", 0))
REPLICATED = ("c_ctx", "b_mod", "g_ffn1", "g_mix", "dw_bias", "conv_ln_g", "conv_ln_b", "b_alpha_f", "b_alpha_b",
              "gla_norm_g", "g_ffn2", "g_final")
WEIGHTS = ("c_ctx", "w_mod", "b_mod", "g_ffn1", "w1_gu", "w1_down", "g_mix", "w_in", "dw_weight", "dw_bias",
           "conv_ln_g", "conv_ln_b", "w_conv_out", "w_alpha_f", "b_alpha_f", "w_alpha_b", "b_alpha_b", "gla_norm_g",
           "w_gla_out", "w_out", "g_ffn2", "w2_gu", "w2_down", "g_final")
MATRICES = ("w_mod", "w1_gu", "w1_down", "w_in", "w_conv_out", "w_gla_out", "w_out", "w2_gu", "w2_down")


def _pack_flat(parts, align):
    flat = jnp.concatenate([p.reshape(-1) for p in parts])
    pad = (-flat.shape[0]) % align
    return jnp.concatenate([flat, jnp.zeros((pad,), flat.dtype)]).reshape(-1, 1024)


def _unpack_flat(flat2d, shapes):
    flat = flat2d.reshape(-1)
    out, off = [], 0
    for s in shapes:
        n = math.prod(s)
        out.append(flat[off:off + n].reshape(s))
        off += n
    return out


def kernel(x, c, ctx, c_ctx, w_mod, b_mod, g_ffn1, w1_gu, w1_down, g_mix, w_in, dw_weight, dw_bias, conv_ln_g, conv_ln_b, w_conv_out, w_alpha_f, b_alpha_f, w_alpha_b, b_alpha_b, gla_norm_g, w_gla_out, w_out, g_ffn2, w2_gu, w2_down, g_final, loss_target, m_c_ctx, m_w_mod, m_b_mod, m_g_ffn1, m_w1_gu, m_w1_down, m_g_mix, m_w_in, m_dw_weight, m_dw_bias, m_conv_ln_g, m_conv_ln_b, m_w_conv_out, m_w_alpha_f, m_b_alpha_f, m_w_alpha_b, m_b_alpha_b, m_gla_norm_g, m_w_gla_out, m_w_out, m_g_ffn2, m_w2_gu, m_w2_down, m_g_final, v_c_ctx, v_w_mod, v_b_mod, v_g_ffn1, v_w1_gu, v_w1_down, v_g_mix, v_w_in, v_dw_weight, v_dw_bias, v_conv_ln_g, v_conv_ln_b, v_w_conv_out, v_w_alpha_f, v_b_alpha_f, v_w_alpha_b, v_b_alpha_b, v_gla_norm_g, v_w_gla_out, v_w_out, v_g_ffn2, v_w2_gu, v_w2_down, v_g_final):
    given = dict(locals())
    w = {n: given[n] for n in WEIGHTS}
    m = {n: given["m_" + n] for n in WEIGHTS}
    v = {n: given["v_" + n] for n in WEIGHTS}

    def bf16_shard(n):
        return w[n][0].astype(BF16)

    def install(wts, got):
        for n in ("w_mod", "w1_gu", "w2_gu"):
            if n in got:
                wts[n] = got[n]
        for n in ("w1_down", "w2_down", "w_conv_out", "w_gla_out", "w_out"):
            if n in got:
                wts[n] = got[n].reshape(-1, D)
        if "w_in" in got:
            wts["w_in_p"] = _perm_in_cols(jnp.concatenate([got["w_in"][j] for j in range(4)], axis=1))

    early = tuple(n for n in MATRICES if n not in _Overlap.LATE)
    shards =[bf16_shard(n) for n in early] + [_small_pack(w["dw_weight"][0], w["w_alpha_f"][0], w["w_alpha_b"][0])]
    got = dict(zip(early + ("small",), _gather_weights(shards)))
    wts = {n: w[n] for n in REPLICATED}
    install(wts, got)
    sm = got["small"]
    wts["dw_weight"] = jnp.concatenate([sm[j, :CONV_W] for j in range(4)], axis=1)
    zpad = jnp.zeros((128, HEADS * DK), BF16)
    w_af = jnp.concatenate([sm[j, 32:32 + LOWRANK, :DK] for j in range(4)], axis=1)
    w_ab = jnp.concatenate([sm[j, 32:32 + LOWRANK, DK:] for j in range(4)], axis=1)
    wts["w_alpha_f_pad"] = zpad.at[0:LOWRANK].set(w_af.astype(BF16))
    wts["w_alpha_b_pad"] = zpad.at[LOWRANK:2 * LOWRANK].set(w_ab.astype(BF16))

    place = jnp.stack([lax.axis_index("c"), 2 * lax.axis_index("x") + lax.axis_index("y")]).astype(jnp.int32)
    comm = _Overlap(bf16_shard, lambda got_late: install(wts, got_late), place)
    loss, grad_x, grads = _local_step(x, c, ctx, loss_target, wts, comm)
    loss = lax.psum(loss, ("x", "y", "c"))

    tags = MATRICES + ("small",)
    rest = tuple(n for n in tags if n not in comm.landed)
    rest_sums = _chip_sums("rest", rest, grads, place)
    for n, cs, r in zip(rest, rest_sums, _scatter_chips(rest_sums)):
        comm.landed[n] = (cs, r)
    halves = [_sum_chips("grad_sum_" + t, *comm.landed[t], place) for t in tags]
    reduced = dict(zip(tags, _join_halves(halves)))
    g_shard = {n: reduced[n] for n in MATRICES}
    g_shard["dw_weight"] = reduced["small"][:CONV_W]
    g_shard["w_alpha_f"] = reduced["small"][32:32 + LOWRANK, :DK]
    g_shard["w_alpha_b"] = reduced["small"][32:32 + LOWRANK, DK:]

    rep_shapes = [w[n].shape for n in REPLICATED]
    small = _allreduce_small(_pack_flat([grads[n].reshape(w[n].shape) for n in REPLICATED], 8 * 1024))
    g_rep = dict(zip(REPLICATED, _unpack_flat(small, rep_shapes)))

    g_out, d_out, m_out, v_out = {}, {}, {}, {}
    for n, _ in SHARDED:
        s2 = w[n].shape[1:]
        d, nm, nv = _adamw("adamw_" + n, w[n].reshape(s2), g_shard[n], m[n].reshape(s2), v[n].reshape(s2))
        g_out[n] = g_shard[n].reshape(w[n].shape)
        d_out[n], m_out[n], v_out[n] = d.reshape(w[n].shape), nm.reshape(w[n].shape), nv.reshape(w[n].shape)
    pk = lambda t: _pack_flat([t[n] for n in REPLICATED], 8 * 1024)
    d, nm, nv = _adamw("adamw_vectors", pk(w), small, pk(m), pk(v))
    for n, dd, mm, vv in zip(REPLICATED, _unpack_flat(d, rep_shapes), _unpack_flat(nm, rep_shapes),
                             _unpack_flat(nv, rep_shapes)):
        g_out[n], d_out[n], m_out[n], v_out[n] = g_rep[n], dd, mm, vv

    return (loss, grad_x, *[g_out[n] for n in WEIGHTS], *[d_out[n] for n in WEIGHTS],
            *[m_out[n] for n in WEIGHTS], *[v_out[n] for n in WEIGHTS])
```

```python
import functools
import math

import jax
import jax.numpy as jnp
from jax import lax
from jax.experimental import pallas as pl
from jax.experimental.pallas import tpu as pltpu

F32, BF16 = jnp.float32, jnp.bfloat16
MESH = pl.DeviceIdType.MESH
HIGHEST = lax.Precision.HIGHEST

D = 1024
FF = 2816
HEADS, DK, DV = 4, 128, 256
LOWRANK = 16
CONV_W = 31
CHUNK = 64
TAU = 16.0
EPS = 1e-6
Q_SCALE = DK ** -0.5
TM = 256
D_IN = 7200
D_INP = 7296
LR_COL = 7168
OG_CB, GA_CB, GB_CB = 6, 4, 5
VMEM_LIMIT = 52 * 1024 * 1024
WGRAD_VMEM = 40 * 1024 * 1024

ADAM_LR, ADAM_B1, ADAM_B2, ADAM_EPS, ADAM_WD, ADAM_STEP = 0.001, 0.9, 0.999, 1e-08, 0.01, 10


def _silu(x):
    return x * jax.nn.sigmoid(x)


def _rms(h, g):
    return h * lax.rsqrt(jnp.mean(h * h, axis=-1, keepdims=True) + EPS) * g


def _modnorm(x, g, shift, scale):
    return _rms(x, g) * (1 + scale) + shift


def _cparams(sem=None):
    return pltpu.CompilerParams(dimension_semantics=sem, vmem_limit_bytes=VMEM_LIMIT)


def _twice(v):
    return v, v


def tok(arr, width=None, cb=0, clamp=None):
    return ("tok", arr, arr.shape[1] if width is None else width, cb, clamp)


def mod(arr):
    return ("mod", arr)


def const(arr):
    return ("const", arr)


def _rowmap(name, body, n_tiles, ins, outs, *, tpb, nb):
    def modrow(i):
        return jnp.minimum(i // tpb, nb)

    in_specs, args = [], []
    for spec in ins:
        if spec[0] == "tok":
            _, arr, width, cb, clamp = spec
            if clamp is None:
                im = lambda i, cb=cb: (i, cb)
            else:
                im = lambda i, cb=cb, clamp=clamp: (jnp.minimum(i, clamp), cb)
            in_specs.append(pl.BlockSpec((TM, width), im))
        elif spec[0] == "mod":
            arr = spec[1]
            in_specs.append(pl.BlockSpec((1, 1, arr.shape[2]), lambda i: (modrow(i), 0, 0)))
        else:
            arr = spec[1]
            in_specs.append(pl.BlockSpec(arr.shape, lambda i, nd=arr.ndim: (0,) * nd))
        args.append(arr)
    out_specs, out_shapes, aliases = [], [], {}
    for o in outs:
        if o[0] == "tok":
            _, rows, width, dtype = o
            out_shapes.append(jax.ShapeDtypeStruct((rows, width), dtype))
            out_specs.append(pl.BlockSpec((TM, width), lambda i: (i, 0)))
        elif o[0] == "cols":
            buf, width, cb = o[1:4]
            first_tile = o[4] if len(o) > 4 else 0
            if not isinstance(buf, jax.ShapeDtypeStruct):
                aliases[len(args)] = len(out_shapes)
                in_specs.append(ANY)
                args.append(buf)
            out_shapes.append(jax.ShapeDtypeStruct(buf.shape, buf.dtype))
            out_specs.append(pl.BlockSpec((TM, width), lambda i, cb=cb, t0=first_tile: (i + t0, cb)))
        elif o[0] == "tok_head":
            _, rows, width, dtype, last = o
            out_shapes.append(jax.ShapeDtypeStruct((rows, width), dtype))
            out_specs.append(pl.BlockSpec((TM, width), lambda i, last=last: (jnp.minimum(i, last), 0)))
        elif o[0] == "tokT":
            _, rows, width, dtype = o
            out_shapes.append(jax.ShapeDtypeStruct((width, rows), dtype))
            out_specs.append(pl.BlockSpec((width, TM), lambda i: (0, i)))
        elif o[0] == "acc":
            _, rows, width = o
            out_shapes.append(jax.ShapeDtypeStruct((rows, width), F32))
            out_specs.append(pl.BlockSpec((rows, width), lambda i: (0, 0)))
        else:
            width = o[1]
            rows_visited = min((n_tiles - 1) // tpb, nb) + 1
            out_shapes.append(jax.ShapeDtypeStruct((rows_visited, 1, width), F32))
            out_specs.append(pl.BlockSpec((1, 1, width), lambda i: (modrow(i), 0, 0)))
    n_in = len(ins)

    def kern(*refs):
        i = pl.program_id(0)
        vals = []
        for r, spec in zip(refs[:n_in], ins):
            val = r[0] if spec[0] == "mod" else r[...]
            vals.append(val.astype(F32) if spec[0] == "tok" and val.dtype == BF16 else val)
        res = body(i, *vals)
        for r, o, val in zip(refs[len(args):], outs, res):
            if o[0] in ("tok", "cols"):
                r[...] = val.astype(r.dtype)
            elif o[0] == "tok_head":
                @pl.when(i <= o[4])
                def _():
                    r[...] = val.astype(r.dtype)
            elif o[0] == "tokT":
                r[...] = val.T.astype(r.dtype)
            elif o[0] == "acc":
                @pl.when(i == 0)
                def _():
                    r[...] = jnp.zeros(r.shape, F32)
                r[...] += jnp.broadcast_to(val, r.shape)
            else:
                first = jnp.logical_or(i == 0, modrow(i) != modrow(jnp.maximum(i - 1, 0)))

                @pl.when(first)
                def _():
                    r[...] = jnp.zeros(r.shape, F32)
                r[0] += val

    return pl.pallas_call(
        kern, grid=(n_tiles,), in_specs=in_specs, out_specs=out_specs, out_shape=out_shapes,
        input_output_aliases=aliases, compiler_params=_cparams(("arbitrary",)), name=name)(*args)


def _pick(n, cands):
    for c in cands:
        if n % c == 0:
            return c
    return n


def _pallas(kern, *, grid, in_specs, out_specs, out_shape, scratch_shapes, sem, name, args, side=None):
    if side is None:
        return pl.pallas_call(kern, grid=grid, in_specs=in_specs, out_specs=out_specs, out_shape=out_shape,
                              scratch_shapes=scratch_shapes, compiler_params=_cparams(sem), name=name)(*args)
    single = not isinstance(out_shape, (list, tuple))
    shapes = [out_shape] if single else list(out_shape)
    ospecs = [out_specs] if single else list(out_specs)
    n_in, n_out, n_scr = len(in_specs), len(shapes), len(scratch_shapes)
    s_in, s_out = list(side["ins"]), list(side["outs"])

    def wrapped(*refs):
        pos = [0]

        def take(n):
            pos[0] += n
            return refs[pos[0] - n:pos[0]]
        ins, sins, outs, souts, scr, sems = take(n_in), take(len(s_in)), take(n_out), take(len(s_out)), take(n_scr), take(2)
        ids = [pl.program_id(k) for k in range(len(grid))]
        first = functools.reduce(jnp.logical_and, [i == 0 for i in ids])
        last = functools.reduce(jnp.logical_and, [i == g - 1 for i, g in zip(ids, grid)])
        copy = _remote(*sems)

        @pl.when(first)
        def _():
            side["start"](sins, souts, copy)
        kern(*ins, *outs, *scr)

        @pl.when(last)
        def _():
            side["finish"](sins, souts, copy)

    res = pl.pallas_call(
        wrapped, grid=grid, in_specs=list(in_specs) + [ANY] * len(s_in), out_specs=ospecs + [ANY] * len(s_out),
        out_shape=shapes + s_out, scratch_shapes=list(scratch_shapes) + _sems(side["nsem"]),
        input_output_aliases={n_in + a: n_out + b for a, b in side.get("alias", {}).items()},
        compiler_params=_cparams(("arbitrary",) * len(grid)), name=name)(*args, *s_in)
    main = res[:n_out]
    return (main[0] if single else main), list(res[n_out:])


def _mm(name, a, b, *, trans_b=False, out_dtype=F32, a_fn=None, bias=None, rows=None, side=None, residual=None):
    m, k = a.shape if a.ndim == 2 else (a.shape[1], 2 * a.shape[2])
    m = m if rows is None else rows
    shard = b.shape[2] if b.ndim == 3 else None
    if trans_b:
        n = b.shape[-2]
        tk = _pick(shard, (2816, 2304, 1408, 1024)) if shard else (
            k if k <= 2816 else _pick(k, (2816, 2432, 2304, 2048, 1536, 1408, 1024, 512, 256, 128)))
        tn = _pick(n, (1024, 512, 384, 256, 128))
    else:
        n = 4 * shard if shard else b.shape[1]
        tk = k if k <= 2816 else _pick(k, (2816, 2432, 2304, 2048, 1536, 1408, 1024, 512, 256, 128))
        tn = _pick(shard, (512, 384, 1408, 256, 128)) if shard else _pick(n, (1024, 2432, 512, 384, 256, 128))
    tm = _pick(m, (512, 256) if residual is not None else (1024, 512, 256))
    nk = k // tk
    per = shard // (tk if trans_b else tn) if shard else None
    dims = (((1,), (1,)), ((), ())) if trans_b else (((1,), (0,)), ((), ()))

    def kern(*refs):
        a_ref, b_ref = refs[0], refs[1]
        bias_ref = refs[2] if bias is not None else None
        acc_ref = refs[-1]
        if residual is not None:
            x_ref, gate_ref, o_ref, xo_ref = refs[-5:-1]
        else:
            o_ref = refs[-2]
        kk = pl.program_id(2)
        av = a_ref[...]
        if a_fn is not None:
            av = a_fn(av)
        p = lax.dot_general(av.astype(BF16), b_ref[...].astype(BF16), dims, preferred_element_type=F32)

        def finish(total):
            if bias_ref is not None:
                total = total + bias_ref[...]
            o_ref[...] = total.astype(o_ref.dtype)
            if residual is not None:
                xo_ref[...] = x_ref[...] + residual["scale"] * gate_ref[0] * total

        if nk == 1:
            finish(p)
        else:
            @pl.when(kk == 0)
            def _():
                acc_ref[...] = p

            @pl.when(kk > 0)
            def _():
                acc_ref[...] += p

            @pl.when(kk == nk - 1)
            def _():
                finish(acc_ref[...])

    if shard and trans_b:
        b_spec = pl.BlockSpec((None, tn, tk), lambda i, j, kk: (kk // per, j, kk % per))
    elif shard:
        b_spec = pl.BlockSpec((None, tk, tn), lambda i, j, kk: (j // per, kk, j % per))
    elif trans_b:
        b_spec = pl.BlockSpec((tn, tk), lambda i, j, kk: (j, kk))
    else:
        b_spec = pl.BlockSpec((tk, tn), lambda i, j, kk: (kk, j))
    if a.ndim == 3:
        pa = a.shape[2] // tk
        a_spec = pl.BlockSpec((None, tm, tk), lambda i, j, kk: (kk // pa, i, kk % pa))
    else:
        a_spec = pl.BlockSpec((tm, tk), lambda i, j, kk: (i, kk))
    in_specs = [a_spec, b_spec]
    args = [a, b]
    if bias is not None:
        in_specs.append(pl.BlockSpec((1, tn), lambda i, j, kk: (0, j)))
        args.append(bias)
    out_specs = pl.BlockSpec((tm, tn), lambda i, j, kk: (i, j))
    out_shape = jax.ShapeDtypeStruct((m, n), out_dtype)
    if residual is not None:
        tiles, nb = residual["tpb"] * TM // tm, residual["gate"].shape[0] - 1
        in_specs += [out_specs, pl.BlockSpec((1, 1, tn), lambda i, j, kk: (jnp.minimum(i // tiles, nb), 0, j))]
        args += [residual["x"], residual["gate"]]
        out_specs, out_shape = [out_specs, out_specs], [out_shape, jax.ShapeDtypeStruct((m, n), F32)]
    return _pallas(
        kern, grid=(m // tm, n // tn, nk), in_specs=in_specs, out_specs=out_specs, out_shape=out_shape,
        scratch_shapes=[pltpu.VMEM((tm, tn) if nk > 1 else (8, 128), F32)],
        sem=("parallel", "parallel", "arbitrary"), name=name, args=args, side=side)


def _mm_tn(name, x, dy):
    t, k1, n1 = x.shape[0], x.shape[1], dy.shape[1]
    tt = _pick(t, (512, 256, 128, 64, 8))
    tk1 = _pick(k1, (1024, 512, 256, 128))
    tn = _pick(n1, (512, 384, 256, 128))
    ns = t // tt

    def kern(x_ref, dy_ref, o_ref, acc_ref):
        s = pl.program_id(2)
        p = lax.dot_general(x_ref[...].astype(BF16), dy_ref[...].astype(BF16), (((0,), (0,)), ((), ())),
                            preferred_element_type=F32)

        @pl.when(s == 0)
        def _():
            acc_ref[...] = p

        @pl.when(s > 0)
        def _():
            acc_ref[...] += p

        @pl.when(s == ns - 1)
        def _():
            o_ref[...] = acc_ref[...].astype(o_ref.dtype)

    return pl.pallas_call(
        kern, grid=(k1 // tk1, n1 // tn, ns),
        in_specs=[pl.BlockSpec((tt, tk1), lambda i, j, s: (s, i)), pl.BlockSpec((tt, tn), lambda i, j, s: (s, j))],
        out_specs=pl.BlockSpec((tk1, tn), lambda i, j, s: (i, j)), out_shape=jax.ShapeDtypeStruct((k1, n1), F32),
        scratch_shapes=[pltpu.VMEM((tk1, tn), F32)],
        compiler_params=_cparams(("parallel", "parallel", "arbitrary")), name=name)(x, dy)


def _wgrad(name, xt, dy, rows=None, col_shards=False, side=None):
    k1 = xt.shape[0]
    t = xt.shape[1] if rows is None else rows
    n1 = dy.shape[1] if dy.ndim == 2 else 2 * dy.shape[2]
    tm = _pick(k1, (1024, 1408, 512, 256))
    tn = _pick(n1 // 4, (1408, 512, 384, 256, 128)) if col_shards else _pick(n1, (1024, 2432, 512, 384, 256, 128))
    fixed = tm * tn * (4 + 4 + 2 * 2)
    tk = next((c for c in (2048, 1536, 1024, 512, 256, 128)
               if t % c == 0 and fixed + 4 * c * (tm + tn) <= WGRAD_VMEM), 128)
    ns = t // tk
    per = n1 // 4 // tn

    def kern(x_ref, dy_ref, o_ref, acc_ref):
        s = pl.program_id(2)
        p = jnp.dot(x_ref[...], dy_ref[...], preferred_element_type=F32)

        @pl.when(s == 0)
        def _():
            acc_ref[...] = p

        @pl.when(s > 0)
        def _():
            acc_ref[...] += p

        @pl.when(s == ns - 1)
        def _():
            o_ref[...] = acc_ref[...].astype(o_ref.dtype)

    if dy.ndim == 3:
        pd = dy.shape[2] // tn
        dy_spec = pl.BlockSpec((None, tk, tn), lambda i, j, s: (j // pd, s, j % pd))
    else:
        dy_spec = pl.BlockSpec((tk, tn), lambda i, j, s: (s, j))
    if col_shards:
        out_spec = pl.BlockSpec((None, tm, tn), lambda i, j, s: (j // per, i, j % per))
        out_shape = jax.ShapeDtypeStruct((4, k1, n1 // 4), BF16)
    else:
        out_spec = pl.BlockSpec((tm, tn), lambda i, j, s: (i, j))
        out_shape = jax.ShapeDtypeStruct((k1, n1), BF16)
    return _pallas(
        kern, grid=(k1 // tm, n1 // tn, ns),
        in_specs=[pl.BlockSpec((tm, tk), lambda i, j, s: (i, s)), dy_spec],
        out_specs=out_spec, out_shape=out_shape, scratch_shapes=[pltpu.VMEM((tm, tn), F32)],
        sem=("parallel", "parallel", "arbitrary"), name=name, args=[xt, dy], side=side)


def _swiglu_fwd(name, u, w_gu, side=None):
    m = u.shape[0]
    half = w_gu.shape[2]
    tm = _pick(m, (512, 256))

    def kern(u_ref, wa_ref, wb_ref, ab_ref, hm_ref, hmt_ref):
        uv = u_ref[...]
        a = jnp.dot(uv, wa_ref[...], preferred_element_type=F32)
        b = jnp.dot(uv, wb_ref[...], preferred_element_type=F32)
        s = jax.nn.sigmoid(a)
        silu_a = a * s
        ab_ref[0] = (b * (s * (1 + a * (1 - s)))).astype(BF16)
        ab_ref[1] = silu_a.astype(BF16)
        hm = (silu_a * b).astype(BF16)
        hm_ref[...] = hm
        hmt_ref[...] = hm.T

    return _pallas(
        kern, grid=(2, m // tm),
        in_specs=[pl.BlockSpec((tm, D), lambda j, i: (i, 0)),
                  pl.BlockSpec((None, D, half), lambda j, i: (j, 0, 0)),
                  pl.BlockSpec((None, D, half), lambda j, i: (2 + j, 0, 0))],
        out_specs=[pl.BlockSpec((2, tm, half), lambda j, i: (0, i, j)),
                   pl.BlockSpec((tm, half), lambda j, i: (i, j)),
                   pl.BlockSpec((half, tm), lambda j, i: (j, i))],
        out_shape=[jax.ShapeDtypeStruct((2, m, FF), BF16), jax.ShapeDtypeStruct((m, FF), BF16),
                   jax.ShapeDtypeStruct((FF, m), BF16)],
        scratch_shapes=[], sem=("parallel", "parallel"), name=name, args=[u, w_gu, w_gu], side=side)


def _swiglu_bwd(name, df, w_down, ab):
    m = df.shape[0]
    half = FF // 2
    tm = _pick(m, (512, 256))

    def kern(df_ref, w_ref, ab_ref, o_ref):
        dh = lax.dot_general(df_ref[...], w_ref[...], (((1,), (1,)), ((), ())), preferred_element_type=F32)
        o_ref[0] = (dh * ab_ref[0].astype(F32)).astype(BF16)
        o_ref[1] = (dh * ab_ref[1].astype(F32)).astype(BF16)

    return pl.pallas_call(
        kern, grid=(2, m // tm),
        in_specs=[pl.BlockSpec((tm, D), lambda j, i: (i, 0)),
                  pl.BlockSpec((half, D), lambda j, i: (j, 0)),
                  pl.BlockSpec((2, tm, half), lambda j, i: (0, i, j))],
        out_specs=pl.BlockSpec((2, tm, half), lambda j, i: (0, i, j)),
        out_shape=jax.ShapeDtypeStruct((2, m, FF), BF16),
        compiler_params=_cparams(("parallel", "parallel")), name=name)(df, w_down, ab)


def _gla_maps(bl, t, tc):
    nx, nc = t // CHUNK, tc // CHUNK
    nxb = bl * nx

    def rowblk(d, b, n):
        c_ctx = jnp.where(d == 0, n, nc - 1 - n)
        c_x = jnp.where(d == 0, n - nc, nx - 1 - (n - nc))
        return jnp.where(n < nc, nxb + b * nc + c_ctx, b * nx + c_x)

    def xblk(d, b, n):
        n2 = jnp.maximum(n, nc)
        return b * nx + jnp.where(d == 0, n2 - nc, nx - 1 - (n2 - nc))

    return nx, nc, rowblk, xblk


def _dot01(m, x, cm):
    x1 = x.astype(BF16)
    r1 = x - x1.astype(F32)
    x2 = r1.astype(BF16)
    x3 = (r1 - x2.astype(F32)).astype(BF16)
    w = x.shape[1]
    p = lax.dot_general(m.astype(BF16), jnp.concatenate([x1, x2, x3], axis=1), (((cm,), (0,)), ((), ())),
                        preferred_element_type=F32)
    return p[:, :w] + p[:, w:2 * w] + p[:, 2 * w:]


def _gla_chunk(m, q, k, v, g, h):
    gh = g[:, h * DK:(h + 1) * DK]
    b = _dot01(m, gh, 1)
    tot = jnp.sum(gh, axis=0, keepdims=True)
    mid = b[CHUNK // 2:CHUNK // 2 + 1, :]
    qh = q[:, h * DK:(h + 1) * DK] * Q_SCALE
    kh = k[:, h * DK:(h + 1) * DK]
    vh = v[:, h * DV:(h + 1) * DV]
    return b, tot, mid, qh, kh, vh


def _dot(a, b, ca, cb):
    return lax.dot_general(a.astype(BF16), b.astype(BF16), (((ca,), (cb,)), ((), ())),
                           preferred_element_type=F32)


def _gla_fwd(p, g2, mmats, bl, t, tc):
    nx, nc, rowblk, xblk = _gla_maps(bl, t, tc)
    ns = nx + nc

    def kern(q0, k0, v0, g0, q1, k1, v1, g1, m_ref, o0, o1, ss_ref, s_ref):
        n = pl.program_id(1)

        @pl.when(n == 0)
        def _():
            s_ref[...] = jnp.zeros(s_ref.shape, F32)
        sides = ((q0, k0, v0, g0, o0), (q1, k1, v1, g1, o1))
        loaded = [(m_ref[d], q_ref[...].astype(F32), k_ref[...].astype(F32), v_ref[...].astype(F32), g_ref[...])
                  for d, (q_ref, k_ref, v_ref, g_ref, _) in enumerate(sides)]
        chains = [(d, h) for d in range(2) for h in range(HEADS)]
        base = [_gla_chunk(*loaded[d], h) for d, h in chains]
        pre = []
        for (d, h), (b, tot, mid, qh, kh, vh) in zip(chains, base):
            s0 = s_ref[d, h * DV:(h + 1) * DV, :]
            ss_ref[d, 0, 0, h * DV:(h + 1) * DV, :] = s0.astype(BF16)
            pre.append((s0, kh * jnp.exp(tot - b), qh * jnp.exp(b), qh * jnp.exp(b - mid), kh * jnp.exp(mid - b)))
        raw = [(_dot(qm, km, 1, 1), _dot(qe, s0, 1, 1), _dot(bs[5], kl, 0, 0))
               for bs, (s0, kl, qe, qm, km) in zip(base, pre)]
        for (d, h), bs, (s0, kl, qe, qm, km), (att_raw, inter, s_add) in zip(chains, base, pre, raw):
            s_ref[d, h * DV:(h + 1) * DV, :] = s0 * jnp.exp(bs[1]) + s_add
            sides[d][4][:, h * DV:(h + 1) * DV] = inter + _dot(loaded[d][0] * att_raw, bs[5], 1, 0)

    def operands(d):
        return [pl.BlockSpec((CHUNK, 512), lambda b, n: (rowblk(d, b, n), 4)),
                pl.BlockSpec((CHUNK, 512), lambda b, n: (rowblk(d, b, n), 5)),
                pl.BlockSpec((CHUNK, 1024), lambda b, n: (rowblk(d, b, n), 3)),
                pl.BlockSpec((CHUNK, 512), lambda b, n: (rowblk(d, b, n), d))]

    o_shape = jax.ShapeDtypeStruct((bl * t, HEADS * DV), F32)
    return pl.pallas_call(
        kern, grid=(bl, ns),
        in_specs=operands(0) + operands(1) + [pl.BlockSpec((2, CHUNK, CHUNK), lambda b, n: (0, 0, 0))],
        out_specs=[pl.BlockSpec((CHUNK, 1024), lambda b, n: (xblk(0, b, n), 0)),
                   pl.BlockSpec((CHUNK, 1024), lambda b, n: (xblk(1, b, n), 0)),
                   pl.BlockSpec((2, 1, 1, HEADS * DV, DK), lambda b, n: (0, b, n, 0, 0))],
        out_shape=[o_shape, o_shape, jax.ShapeDtypeStruct((2, bl, ns, HEADS * DV, DK), BF16)],
        scratch_shapes=[pltpu.VMEM((2, HEADS * DV, DK), F32)],
        compiler_params=_cparams(("parallel", "arbitrary")), name="gla_fwd")(p, p, p, g2, p, p, p, g2, mmats)


def _gla_bwd(p, g2, mmats, ssave, do, bl, t, tc):
    nx, nc, rowblk, xblk = _gla_maps(bl, t, tc)
    ns = nx + nc
    rev = lambda s: ns - 1 - s

    def kern(q0, k0, v0, g0, do0, q1, k1, v1, g1, do1, m_ref, ss_ref,
             dq0, dk0, dv0, dg0, dq1, dk1, dv1, dg1, ds_ref):
        step = pl.program_id(1)
        n = ns - 1 - step

        @pl.when(step == 0)
        def _():
            ds_ref[...] = jnp.zeros(ds_ref.shape, F32)
        live = (n >= nc).astype(F32)
        sides = ((q0, k0, v0, g0, do0, dq0, dk0, dv0, dg0), (q1, k1, v1, g1, do1, dq1, dk1, dv1, dg1))
        loaded = [(m_ref[d], s[0][...].astype(F32), s[1][...].astype(F32), s[2][...].astype(F32), s[3][...])
                  for d, s in enumerate(sides)]
        dovs = [s[4][...] * live for s in sides]
        chains = [(d, h) for d in range(2) for h in range(HEADS)]
        base = [_gla_chunk(*loaded[d], h) for d, h in chains]
        pre = []
        for (d, h), (b, tot, mid, qh, kh, vh) in zip(chains, base):
            eb, ebm, emb, etb = jnp.exp(b), jnp.exp(b - mid), jnp.exp(mid - b), jnp.exp(tot - b)
            pre.append(dict(
                eb=eb, ebm=ebm, emb=emb, etb=etb, etot=jnp.exp(tot), qe=qh * eb, qm=qh * ebm, km=kh * emb, kl=kh * etb,
                vh=vh, doh=dovs[d][:, h * DV:(h + 1) * DV], s0=ss_ref[d, 0, 0, h * DV:(h + 1) * DV, :].astype(F32),
                ds1=ds_ref[d, h * DV:(h + 1) * DV, :]))
        first = [dict(att=_dot(c["qm"], c["km"], 1, 1), datt=_dot(c["doh"], c["vh"], 1, 1),
                      dqe=_dot(c["doh"], c["s0"], 1, 0), ds_add=_dot(c["doh"], c["qe"], 0, 0),
                      dkl=_dot(c["vh"], c["ds1"], 1, 0), dv_s=_dot(c["kl"], c["ds1"], 1, 1)) for c in pre]
        second = []
        for (d, h), c, f in zip(chains, pre, first):
            m = loaded[d][0]
            ds_ref[d, h * DV:(h + 1) * DV, :] = c["ds1"] * c["etot"] + f["ds_add"]
            att, datt = m * f["att"], m * f["datt"]
            second.append(dict(dqm=_dot(datt, c["km"], 1, 0), dkm=_dot(datt, c["qm"], 0, 0),
                               dv_a=_dot(att, c["doh"], 0, 0)))
        for (d, h), c, f, s in zip(chains, pre, first, second):
            dq_ref, dk_ref, dv_ref, dg_ref = sides[d][5:]
            dtot = c["etot"] * jnp.sum(c["ds1"] * c["s0"], axis=0, keepdims=True) + jnp.sum(
                f["dkl"] * c["kl"], axis=0, keepdims=True)
            db = f["dqe"] * c["qe"] + s["dqm"] * c["qm"] - s["dkm"] * c["km"] - f["dkl"] * c["kl"]
            dq_ref[:, h * DK:(h + 1) * DK] = ((f["dqe"] * c["eb"] + s["dqm"] * c["ebm"]) * Q_SCALE).astype(BF16)
            dk_ref[:, h * DK:(h + 1) * DK] = (s["dkm"] * c["emb"] + f["dkl"] * c["etb"]).astype(BF16)
            dv_ref[:, h * DV:(h + 1) * DV] = (s["dv_a"] + f["dv_s"]).astype(BF16)
            dg_ref[:, h * DK:(h + 1) * DK] = _dot01(loaded[d][0], db, 0) + dtot

    nt = p.shape[0]

    def operands(d):
        return [pl.BlockSpec((CHUNK, 512), lambda b, s: (rowblk(d, b, rev(s)), 4)),
                pl.BlockSpec((CHUNK, 512), lambda b, s: (rowblk(d, b, rev(s)), 5)),
                pl.BlockSpec((CHUNK, 1024), lambda b, s: (rowblk(d, b, rev(s)), 3)),
                pl.BlockSpec((CHUNK, 512), lambda b, s: (rowblk(d, b, rev(s)), d)),
                pl.BlockSpec((CHUNK, 1024), lambda b, s: (xblk(d, b, rev(s)), 0))]

    def results(d):
        row = lambda b, s: (rowblk(d, b, rev(s)), 0)
        return [pl.BlockSpec((CHUNK, 512), row), pl.BlockSpec((CHUNK, 512), row), pl.BlockSpec((CHUNK, 1024), row),
                pl.BlockSpec((CHUNK, 512), row)]

    shapes = [jax.ShapeDtypeStruct((nt, 512), BF16), jax.ShapeDtypeStruct((nt, 512), BF16),
              jax.ShapeDtypeStruct((nt, 1024), BF16), jax.ShapeDtypeStruct((nt, 512), F32)]
    out = pl.pallas_call(
        kern, grid=(bl, ns),
        in_specs=operands(0) + operands(1) + [
            pl.BlockSpec((2, CHUNK, CHUNK), lambda b, s: (0, 0, 0)),
            pl.BlockSpec((2, 1, 1, HEADS * DV, DK), lambda b, s: (0, b, rev(s), 0, 0))],
        out_specs=results(0) + results(1), out_shape=shapes + shapes,
        scratch_shapes=[pltpu.VMEM((2, HEADS * DV, DK), F32)],
        compiler_params=_cparams(("parallel", "arbitrary")), name="gla_bwd")(
            p, p, p, g2, do, p, p, p, g2, do, mmats, ssave)
    return out[:4], out[4:]


CONV_CT = 256
CONV_PAD = 16
CONV_RC = 128
CONV_HALO = 24


def _conv_fill(zp, z_ref, t):
    zp[0:CONV_PAD, :] = jnp.zeros((CONV_PAD, CONV_CT), F32)
    zp[CONV_PAD + t:2 * CONV_PAD + t, :] = jnp.zeros((CONV_PAD, CONV_CT), F32)
    zp[CONV_PAD:CONV_PAD + t, :] = z_ref[...]


def _dwconv(name, z, w, bias, bl, t, flip):
    def kern(z_ref, w_ref, b_ref, o_ref, zp):
        _conv_fill(zp, z_ref, t)
        offs = [(CONV_W - j) if flip else (j + 1) for j in range(CONV_W)]
        for r in range(0, t, CONV_RC):
            acc = jnp.broadcast_to(b_ref[...], (CONV_RC, CONV_CT))
            for rot in range(8):
                win = zp[r + rot:r + rot + CONV_RC + CONV_HALO, :]
                for j in range(CONV_W):
                    if offs[j] % 8 == rot:
                        a = offs[j] - rot
                        acc = acc + w_ref[j:j + 1, :] * win[a:a + CONV_RC, :]
            o_ref[r:r + CONV_RC, :] = acc

    return pl.pallas_call(
        kern, grid=(bl, 1024 // CONV_CT),
        in_specs=[pl.BlockSpec((t, CONV_CT), lambda b, c: (b, c)),
                  pl.BlockSpec((32, CONV_CT), lambda b, c: (0, c)),
                  pl.BlockSpec((1, CONV_CT), lambda b, c: (0, c))],
        out_specs=pl.BlockSpec((t, CONV_CT), lambda b, c: (b, c)),
        out_shape=jax.ShapeDtypeStruct(z.shape, F32),
        scratch_shapes=[pltpu.VMEM((t + 2 * CONV_PAD, CONV_CT), F32)],
        compiler_params=_cparams(("parallel", "parallel")), name=name)(z, w, bias)


def _dwconv_wgrad(z, dzc, bl, t):
    def kern(z_ref, d_ref, dw_ref, db_ref, zp):
        b = pl.program_id(1)

        @pl.when(b == 0)
        def _():
            dw_ref[...] = jnp.zeros(dw_ref.shape, F32)
            db_ref[...] = jnp.zeros(db_ref.shape, F32)
        _conv_fill(zp, z_ref, t)
        for rot in range(8):
            taps = [j for j in range(CONV_W) if (j + 1) % 8 == rot]
            accs = [jnp.zeros((8, CONV_CT), F32) for _ in taps]
            for r in range(0, t, CONV_RC):
                d = d_ref[r:r + CONV_RC, :]
                win = zp[r + rot:r + rot + CONV_RC + CONV_HALO, :]
                for k, j in enumerate(taps):
                    a = j + 1 - rot
                    prod = d * win[a:a + CONV_RC, :]
                    accs[k] = accs[k] + jnp.sum(prod.reshape(CONV_RC // 8, 8, CONV_CT), axis=0)
            for k, j in enumerate(taps):
                dw_ref[j:j + 1, :] += jnp.sum(accs[k], axis=0, keepdims=True)
        db_ref[...] += jnp.sum(d_ref[...], axis=0, keepdims=True)

    return pl.pallas_call(
        kern, grid=(1024 // CONV_CT, bl),
        in_specs=[pl.BlockSpec((t, CONV_CT), lambda c, b: (b, c)),
                  pl.BlockSpec((t, CONV_CT), lambda c, b: (b, c))],
        out_specs=[pl.BlockSpec((32, CONV_CT), lambda c, b: (0, c)),
                   pl.BlockSpec((1, CONV_CT), lambda c, b: (0, c))],
        out_shape=[jax.ShapeDtypeStruct((32, 1024), F32), jax.ShapeDtypeStruct((1, 1024), F32)],
        scratch_shapes=[pltpu.VMEM((t + 2 * CONV_PAD, CONV_CT), F32)],
        compiler_params=_cparams(("parallel", "arbitrary")), name="dwconv_wgrad")(z, dzc)


def _ffn_fwd(tag, xin, n_tiles, g, sh, sc, gate, w_gu, w_down, tpb, nb, comm=None):
    rows = n_tiles * TM
    rm = functools.partial(_rowmap, tpb=tpb, nb=nb)
    u, ut = rm(tag + "_norm", lambda i, x, g_, sh_, sc_: _twice(_modnorm(x, g_, sh_, sc_)), n_tiles,
               [tok(xin), const(g), mod(sh), mod(sc)], [("tok", rows, D, BF16), ("tokT", rows, D, BF16)])
    ab, hm, hmt = _with_side(comm, tag + "_gu", None, lambda s: _swiglu_fwd(tag + "_gu", u, w_gu, side=s))
    res = dict(x=xin, gate=gate, scale=0.5, tpb=tpb)
    f, xout = _with_side(comm, tag + "_down", None,
                         lambda s: _mm(tag + "_down", hm, w_down, out_dtype=BF16, side=s, residual=res))
    return xout, (ut, ab, hmt, f)


def _ffn_bwd(tag, xin, saved, dxout, dx_clamp, n_tiles, g, sh, sc, gate, w_gu, w_down, tpb, nb, comm=None, grads=None,
             names=None, keep_tiles=None):
    ut, ab, hmt, f = saved
    rows = n_tiles * TM
    rm = functools.partial(_rowmap, tpb=tpb, nb=nb)

    def mask(i):
        return 1.0 if dx_clamp is None else (i <= dx_clamp).astype(F32)

    def b1(i, dx, f_, gt):
        dx = dx * mask(i)
        return (0.5 * gt * dx, jnp.sum(0.5 * f_ * dx, axis=0, keepdims=True))
    df, dgate = rm(tag + "_bres", b1, n_tiles, [tok(dxout, clamp=dx_clamp), tok(f), mod(gate)],
                   [("tok", rows, D, BF16), ("modacc", D)])
    grads[names[1]] = _wgrad(tag + "_wdown", hmt, df)
    dab = _swiglu_bwd(tag + "_bdown", df, w_down, ab)
    grads[names[0]] = _with_side(comm, tag + "_wgu", grads,
                                 lambda s: _wgrad(tag + "_wgu", ut, dab, col_shards=True, side=s))
    du = _with_side(comm, tag + "_bgu", grads, lambda s: _mm(tag + "_bgu", dab, w_gu, trans_b=True, side=s))

    def b3(i, x, g_, sh_, sc_, du_, dx):
        _, vjp = jax.vjp(_modnorm, x, g_, sh_, sc_)
        dxn, dg, dsh, dsc = vjp(du_)
        return (dx * mask(i) + dxn, dg, dsh, dsc)
    dx_out = ("tok", rows, D, F32) if keep_tiles is None else ("tok_head", keep_tiles * TM, D, F32, keep_tiles - 1)
    dxin, dg, dsh, dsc = rm(tag + "_bnorm", b3, n_tiles,
                            [tok(xin), const(g), mod(sh), mod(sc), tok(du), tok(dxout, clamp=dx_clamp)],
                            [dx_out, ("acc", 1, D), ("modacc", D), ("modacc", D)])
    return dxin, dict(g=dg, sh=dsh, sc=dsc, gate=dgate)


def _perm_in_cols(w):
    pad = jnp.zeros(w.shape[:-1] + (D_INP - D_IN,), w.dtype)
    return jnp.concatenate([w[..., :4096], w[..., 5152:7200], w[..., 4096:5120], w[..., 5120:5152], pad], axis=-1)


def _unperm_in_cols(w):
    return jnp.concatenate([w[..., :4096], w[..., 6144:LR_COL], w[..., LR_COL:LR_COL + 32], w[..., 4096:6144]], axis=-1)


def _local_step(x, c, ctx, target, wts, comm=None):
    bl, t, _ = x.shape
    tc = ctx.shape[1]
    nx_rows, nc_rows = bl * t, bl * tc
    nt_rows = nx_rows + nc_rows
    tpb = t // TM
    nxt, ntt = nx_rows // TM, nt_rows // TM
    nb = bl
    rm = functools.partial(_rowmap, tpb=tpb, nb=nb)
    last_x = nxt - 1

    x0 = jnp.concatenate([x.reshape(nx_rows, D), ctx.reshape(nc_rows, D)], axis=0)
    tgt = target.reshape(nx_rows, D)

    cc = jnp.concatenate([c, wts["c_ctx"].reshape(1, D), jnp.zeros((8 - bl - 1, D), F32)], axis=0)
    modv, cc_all = _mod_forward(cc, wts["w_mod_shard"], wts["b_mod_shard"])
    mods = [modv[:nb + 1, k * D:(k + 1) * D].reshape(nb + 1, 1, D) for k in range(9)]

    x1, sv1 = _ffn_fwd("ffn1", x0, ntt, wts["g_ffn1"], mods[0], mods[1], mods[2], wts["w1_gu"], wts["w1_down"],
                       tpb, nb, comm)
    u2, u2t = rm("in_norm", lambda i, x_, g_, sh_, sc_: _twice(_modnorm(x_, g_, sh_, sc_)), ntt,
                 [tok(x1), const(wts["g_mix"]), mod(mods[3]), mod(mods[4])],
                 [("tok", nt_rows, D, BF16), ("tokT", nt_rows, D, BF16)])
    w_inp = wts["w_in_p"]
    p = _with_side(comm, "in_proj", None, lambda s: _mm("in_proj", u2, w_inp, out_dtype=BF16, side=s))

    waf, wab, baf, bab = wts["w_alpha_f_pad"], wts["w_alpha_b_pad"], wts["b_alpha_f"], wts["b_alpha_b"]

    def dec_fwd(i, lr, wf, wb, bf_, bb_):
        zf = _dot(lr, wf, 1, 0) + bf_
        zb = _dot(lr, wb, 1, 0) + bb_
        return (jnp.concatenate([jax.nn.log_sigmoid(zf) / TAU, jax.nn.log_sigmoid(zb) / TAU], axis=1),)
    (gfb,) = rm("decay_fwd", dec_fwd, ntt, [tok(p, 128, LR_COL // 128), const(waf), const(wab), const(baf), const(bab)],
                [("tok", nt_rows, 1024, F32)])
    g2 = gfb
    tri = jnp.tril(jnp.ones((CHUNK, CHUNK), F32))
    mmats = jnp.stack([tri, tri.T])
    *o2, ssave = _gla_fwd(p, g2, mmats, bl, t, tc)

    gn_g = wts["gla_norm_g"]

    def gla_out(of, ob, og, gn):
        o = of + ob
        parts = []
        for h in range(HEADS):
            oh = o[:, h * DV:(h + 1) * DV]
            parts.append(oh * lax.rsqrt(jnp.mean(oh * oh, axis=-1, keepdims=True) + EPS))
        return jnp.concatenate(parts, axis=1) * gn * _silu(og)
    yg_in, yg_int = rm("gla_out", lambda i, of, ob, og, gn: _twice(gla_out(of, ob, og, gn)), nxt,
                       [tok(o2[0]), tok(o2[1]), tok(p, 1024, OG_CB), const(gn_g)],
                       [("tok", nx_rows, D, BF16), ("tokT", nx_rows, D, BF16)])
    y_gla = _with_side(comm, "gla_proj", None,
                       lambda s: _mm("gla_proj", yg_in, wts["w_gla_out"], out_dtype=BF16, side=s))

    (z,) = rm("glu", lambda i, a, b: (a * jax.nn.sigmoid(b),), nxt, [tok(p, 1024, 0), tok(p, 1024, 1)],
              [("tok", nx_rows, D, F32)])
    dw_w = jnp.concatenate([wts["dw_weight"], jnp.zeros((1, D), F32)], axis=0)
    zc = _dwconv("dwconv_fwd", z, dw_w, wts["dw_bias"], bl, t, False)

    def ln_silu(zc_, g_, b_):
        mu = jnp.mean(zc_, axis=-1, keepdims=True)
        var = jnp.mean(jnp.square(zc_ - mu), axis=-1, keepdims=True)
        return _silu((zc_ - mu) * lax.rsqrt(var + EPS) * g_ + b_)
    ln_g, ln_b = wts["conv_ln_g"], wts["conv_ln_b"]
    zl, zlt = rm("conv_ln", lambda i, zc_, g_, b_: _twice(ln_silu(zc_, g_, b_)), nxt,
                 [tok(zc), const(ln_g), const(ln_b)], [("tok", nx_rows, D, BF16), ("tokT", nx_rows, D, BF16)])
    y_conv = _mm("conv_proj", zl, wts["w_conv_out"], out_dtype=BF16)

    mg, mgt = rm("merge", lambda i, ga, gb, yc, yg: _twice(jax.nn.sigmoid(ga) * yc + jax.nn.sigmoid(gb) * yg), nxt,
                 [tok(p, 1024, GA_CB), tok(p, 1024, GB_CB), tok(y_conv), tok(y_gla)],
                 [("tok", nx_rows, D, BF16), ("tokT", nx_rows, D, BF16)])
    mix, x2 = _mm("out_proj", mg, wts["w_out"], out_dtype=BF16,
                  residual=dict(x=x1, gate=mods[5], scale=1.0, tpb=tpb))

    x3, sv2 = _ffn_fwd("ffn2", x2, nxt, wts["g_ffn2"], mods[6], mods[7], mods[8], wts["w2_gu"], wts["w2_down"],
                       tpb, nb)
    g_fin = wts["g_final"].reshape(1, D)

    def head(i, x_, g_, tg):
        y, vjp = jax.vjp(_rms, x_, g_)
        diff = y - tg
        dx, dg = vjp(diff * (1.0 / D))
        loss = 0.5 * jnp.sum(jnp.mean(diff * diff, axis=-1, keepdims=True))
        return dx, dg, loss
    dx3, dg_final, loss_acc = rm("loss_head", head, nxt, [tok(x3), const(g_fin), tok(tgt)],
                                 [("tok", nx_rows, D, F32), ("acc", 1, D), ("acc", 8, 128)])
    loss = loss_acc[0, 0]

    grads = {}
    dx2, gf2 = _ffn_bwd("ffn2", x2, sv2, dx3, None, nxt, wts["g_ffn2"], mods[6], mods[7], mods[8],
                        wts["w2_gu"], wts["w2_down"], tpb, nb, comm, grads, ("w2_gu", "w2_down"))
    grads["g_ffn2"] = gf2["g"]

    dmix, dgate5 = rm("mix_bres", lambda i, dx, mx_, gt: (gt * dx, jnp.sum(mx_ * dx, axis=0, keepdims=True)), nxt,
                      [tok(dx2), tok(mix), mod(mods[5])], [("tok", nx_rows, D, BF16), ("modacc", D)])
    dmg = _mm("out_bproj", dmix, wts["w_out"], trans_b=True)
    grads["w_out"] = _wgrad("out_wgrad", mgt, dmix)

    def merge_bwd(i, dm, ga, gb, yc, yg):
        keep = (i <= last_x).astype(F32)
        dm = dm * keep
        sa, sb = jax.nn.sigmoid(ga), jax.nn.sigmoid(gb)
        return dm * sa, dm * sb, jnp.concatenate([dm * yc * sa * (1 - sa), dm * yg * sb * (1 - sb)], axis=1)
    cl = dict(clamp=last_x)
    dyc, dyg, dp = rm("merge_bwd", merge_bwd, ntt,
                      [tok(dmg, **cl), tok(p, 1024, GA_CB, last_x), tok(p, 1024, GB_CB, last_x), tok(y_conv, **cl),
                       tok(y_gla, **cl)],
                      [("tok", nt_rows, D, BF16), ("tok", nt_rows, D, BF16),
                       ("cols", jax.ShapeDtypeStruct((nt_rows, D_INP), BF16), 2048, 2)])

    dzl = _mm("conv_bproj", dyc, wts["w_conv_out"], trans_b=True, rows=nx_rows)
    grads["w_conv_out"] = _wgrad("conv_wgrad", zlt, dyc, rows=nx_rows)

    def ln_bwd(i, zc_, g_, b_, dz_):
        _, vjp = jax.vjp(ln_silu, zc_, g_, b_)
        return vjp(dz_)
    dzc, dln_g, dln_b = rm("conv_ln_bwd", ln_bwd, nxt, [tok(zc), const(ln_g), const(ln_b), tok(dzl)],
                           [("tok", nx_rows, D, F32), ("acc", 1, D), ("acc", 1, D)])
    dz = _dwconv("dwconv_bwd", dzc, dw_w, jnp.zeros((1, D), F32), bl, t, True)
    ddw, ddb = _dwconv_wgrad(z, dzc, bl, t)
    grads.update(conv_ln_g=dln_g, conv_ln_b=dln_b, dw_weight=ddw[:CONV_W], dw_bias=ddb)

    def glu_bwd(i, dz_, a, b):
        keep = (i <= last_x).astype(F32)
        dz_ = dz_ * keep
        s = jax.nn.sigmoid(b)
        return (jnp.concatenate([dz_ * s, dz_ * a * s * (1 - s)], axis=1),)
    (dp,) = rm("glu_bwd", glu_bwd, ntt, [tok(dz, **cl), tok(p, 1024, 0, last_x), tok(p, 1024, 1, last_x)],
               [("cols", dp, 2048, 0)])

    dyg_in = _mm("gla_bproj", dyg, wts["w_gla_out"], trans_b=True, rows=nx_rows)
    grads["w_gla_out"] = _wgrad("gla_wgrad", yg_int, dyg, rows=nx_rows)

    def gla_out_bwd(i, of, ob, og, gn, dy):
        _, vjp = jax.vjp(gla_out, of, ob, og, gn)
        do_, _, dog_, dgn_ = vjp(dy)
        return do_, dog_, dgn_
    do, dp, dgn = rm("gla_out_bwd", gla_out_bwd, nxt,
                     [tok(o2[0]), tok(o2[1]), tok(p, 1024, OG_CB), const(gn_g), tok(dyg_in)],
                     [("tok", nx_rows, D, F32), ("cols", dp, 1024, OG_CB), ("acc", 1, D)])
    grads["gla_norm_g"] = dgn
    (dp,) = rm("og_ctx_zero", lambda i: (jnp.zeros((TM, D), F32),), ntt - nxt, [], [("cols", dp, 1024, OG_CB, nxt)])

    dq2, dk2, dv2, dg2 = zip(*_gla_bwd(p, g2, mmats, ssave, do, bl, t, tc))

    def dec_bwd(i, lr, wf, wb, bf_, bb_, dgf, dgb_):
        zf = _dot(lr, wf, 1, 0) + bf_
        zb = _dot(lr, wb, 1, 0) + bb_
        dzf = dgf * (1 - jax.nn.sigmoid(zf)) * (1.0 / TAU)
        dzb = dgb_ * (1 - jax.nn.sigmoid(zb)) * (1.0 / TAU)
        dlr = _dot(dzf, wf, 1, 1) + _dot(dzb, wb, 1, 1)
        return (dlr, _dot(lr, dzf, 0, 0), _dot(lr, dzb, 0, 0), jnp.sum(dzf, axis=0, keepdims=True),
                jnp.sum(dzb, axis=0, keepdims=True))
    dp, dwaf, dwab, dbaf, dbab = rm(
        "decay_bwd", dec_bwd, ntt,
        [tok(p, 128, LR_COL // 128), const(waf), const(wab), const(baf), const(bab), tok(dg2[0]), tok(dg2[1])],
        [("cols", dp, 128, LR_COL // 128), ("acc", 128, 512), ("acc", 128, 512), ("acc", 1, 512), ("acc", 1, 512)])
    grads.update(w_alpha_f=dwaf[:LOWRANK], w_alpha_b=dwab[LOWRANK:2 * LOWRANK], b_alpha_f=dbaf, b_alpha_b=dbab)

    (dp,) = rm("gla_sum",
               lambda i, q0, q1, k0, k1, v0, v1: (jnp.concatenate([q0 + q1, k0 + k1, v0 + v1], axis=1),), ntt,
               [tok(dq2[0]), tok(dq2[1]), tok(dk2[0]), tok(dk2[1]), tok(dv2[0]), tok(dv2[1])],
               [("cols", dp, 2048, 1)])
    du2 = _with_side(comm, "in_bproj", grads, lambda s: _mm("in_bproj", dp, w_inp, trans_b=True, side=s))
    grads["w_in_p"] = _wgrad("in_wgrad", u2t, dp)

    def in_norm_bwd(i, x_, g_, sh_, sc_, du_, dx):
        keep = (i <= last_x).astype(F32)
        _, vjp = jax.vjp(_modnorm, x_, g_, sh_, sc_)
        dxn, dg, dsh, dsc = vjp(du_)
        return (dx * keep + dxn, dg, dsh, dsc)
    dx1, dg_mix, dsh3, dsc4 = rm("in_norm_bwd", in_norm_bwd, ntt,
                                 [tok(x1), const(wts["g_mix"]), mod(mods[3]), mod(mods[4]), tok(du2), tok(dx2, **cl)],
                                 [("tok", nt_rows, D, F32), ("acc", 1, D), ("modacc", D), ("modacc", D)])
    grads["g_mix"] = dg_mix

    dx0, gf1 = _ffn_bwd("ffn1", x0, sv1, dx1, None, ntt, wts["g_ffn1"], mods[0], mods[1], mods[2],
                        wts["w1_gu"], wts["w1_down"], tpb, nb, comm, grads, ("w1_gu", "w1_down"), keep_tiles=nxt)
    grads["g_ffn1"] = gf1["g"]
    grad_x = dx0.reshape(bl, t, D)

    dmods = [gf1["sh"], gf1["sc"], gf1["gate"], dsh3, dsc4, dgate5, gf2["sh"], gf2["sc"], gf2["gate"]]
    dmod = jnp.concatenate(
        [jnp.concatenate([a.reshape(a.shape[0], D), jnp.zeros((8 - a.shape[0], D), F32)], axis=0) for a in dmods],
        axis=1)
    grads["w_mod"], grads["c_ctx"], grads["b_mod"] = _mod_backward(dmod, cc_all, wts["w_mod_shard"], nb)
    grads["g_final"] = dg_final.reshape(D)
    return loss, grad_x, grads


ANY = pl.BlockSpec(memory_space=pl.ANY)


def _place():
    x, y, c = lax.axis_index("x"), lax.axis_index("y"), lax.axis_index("c")
    chips = [(1 - x, y), (x, 1 - y), (1 - x, 1 - y)]
    return x, y, c, chips


def _remote(send_sems, recv_sems):
    def copy(k, src, dst, to):
        return pltpu.make_async_remote_copy(src_ref=src, dst_ref=dst, send_sem=send_sems.at[k],
                                            recv_sem=recv_sems.at[k], device_id=to, device_id_type=MESH)
    return copy


def _sems(n):
    return [pltpu.SemaphoreType.DMA((n,)), pltpu.SemaphoreType.DMA((n,))]


def _gather_weights(shards):
    n = len(shards)

    def body(*refs):
        ins, outs = refs[:n], refs[n:2 * n]
        copy = _remote(refs[2 * n], refs[2 * n + 1])
        x, y, c, chips = _place()
        me = 2 * x + y
        sibling = (x, y, 1 - c)
        started = []

        def rows(i, hc):
            hr = ins[i].shape[0] // 2
            return pl.ds(hc * hr, hr)

        for i in range(n):
            started.append(copy(7 * i + 6, ins[i], outs[i].at[me], sibling))
            started[-1].start()
            for j, (px, py) in enumerate(chips):
                started.append(copy(7 * i + j, ins[i].at[rows(i, c), :], outs[i].at[me, rows(i, c), :], (px, py, c)))
                started[-1].start()
        for i in range(n):
            for j, (px, py) in enumerate(chips):
                half = outs[i].at[2 * px + py, rows(i, c), :]
                copy(7 * i + j, half, half, (px, py, c)).wait_recv()
                started.append(copy(7 * i + 3 + j, half, half, sibling))
                started[-1].start()
        for i in range(n):
            copy(7 * i + 6, ins[i], outs[i].at[me], sibling).wait_recv()
            for j, (px, py) in enumerate(chips):
                other = outs[i].at[2 * px + py, rows(i, 1 - c), :]
                copy(7 * i + 3 + j, other, other, sibling).wait_recv()
        for cp in started:
            cp.wait_send()

    return pl.pallas_call(
        body, out_shape=[jax.ShapeDtypeStruct((4,) + s.shape, s.dtype) for s in shards], in_specs=[ANY] * n,
        out_specs=[ANY] * n, scratch_shapes=_sems(7 * n), name="gather_weights")(*shards)


def _swap_halves(name, gs):
    n = len(gs)

    def body(*refs):
        ins, outs = refs[:n], refs[n:2 * n]
        copy = _remote(refs[2 * n], refs[2 * n + 1])
        x, y, c, _ = _place()
        cps = []
        for i in range(n):
            hr = ins[i].shape[1] // 2
            cps.append(copy(i, ins[i].at[:, pl.ds((1 - c) * hr, hr), :], outs[i], (x, y, 1 - c)))
            cps[-1].start()
        for cp in cps:
            cp.wait()

    return pl.pallas_call(
        body, out_shape=[jax.ShapeDtypeStruct((4, g.shape[1] // 2, g.shape[2]), g.dtype) for g in gs],
        in_specs=[ANY] * n, out_specs=[ANY] * n, scratch_shapes=_sems(n), name=name)(*gs)


def _row_tile(hr):
    return hr if hr <= 256 else _pick(hr, (256, 176, 128, 64, 32, 16))


def _add_halves(name, g, r, place):
    hr = r.shape[1]
    tr = _row_tile(hr)
    nblk = hr // tr

    def kern(p_ref, g_ref, r_ref, o_ref):
        o_ref[...] = (g_ref[...].astype(F32) + r_ref[...].astype(F32)).astype(o_ref.dtype)

    blk = (1, tr, g.shape[2])
    return pl.pallas_call(
        kern,
        grid_spec=pltpu.PrefetchScalarGridSpec(
            num_scalar_prefetch=1, grid=(4, nblk),
            in_specs=[pl.BlockSpec(blk, lambda j, i, p: (j, p[0] * nblk + i, 0)),
                      pl.BlockSpec(blk, lambda j, i, p: (j, i, 0))],
            out_specs=pl.BlockSpec(blk, lambda j, i, p: (j, i, 0))),
        out_shape=jax.ShapeDtypeStruct(r.shape, r.dtype),
        compiler_params=_cparams(("parallel", "parallel")), name=name)(place, g, r)


def _scatter_chips(cs):
    n = len(cs)

    def body(*refs):
        ins, outs = refs[:n], refs[n:2 * n]
        copy = _remote(refs[2 * n], refs[2 * n + 1])
        x, y, c, chips = _place()
        me = 2 * x + y
        sends = []
        for i in range(n):
            for j, (px, py) in enumerate(chips):
                sends.append(copy(3 * i + j, ins[i].at[2 * px + py], outs[i].at[me], (px, py, c)))
                sends[-1].start()
        for i in range(n):
            for j, (px, py) in enumerate(chips):
                src = 2 * px + py
                copy(3 * i + j, ins[i].at[src], outs[i].at[src], (px, py, c)).wait_recv()
        for cp in sends:
            cp.wait_send()

    return pl.pallas_call(
        body, out_shape=[jax.ShapeDtypeStruct(a.shape, a.dtype) for a in cs], in_specs=[ANY] * n,
        out_specs=[ANY] * n, scratch_shapes=_sems(3 * n), name="grad_scatter_chips")(*cs)


def _sum_chips(name, cs, r, place):
    hr = r.shape[1]
    tr = _row_tile(hr)
    nblk = hr // tr

    def kern(p_ref, c_ref, r0, r1, r2, r3, o_ref):
        me = p_ref[1]
        acc = None
        for k, rk in enumerate((r0, r1, r2, r3)):
            val = jnp.where(me == k, c_ref[0].astype(F32), rk[0].astype(F32))
            acc = val if acc is None else acc + val
        o_ref[...] = acc

    blk = (1, tr, r.shape[2])

    def slot(k):
        return lambda i, p: (jnp.where(p[1] == k, (k + 1) % 4, k), i, 0)

    return pl.pallas_call(
        kern,
        grid_spec=pltpu.PrefetchScalarGridSpec(
            num_scalar_prefetch=1, grid=(nblk,),
            in_specs=[pl.BlockSpec(blk, lambda i, p: (p[1], i, 0))] + [pl.BlockSpec(blk, slot(k)) for k in range(4)],
            out_specs=pl.BlockSpec((tr, r.shape[2]), lambda i, p: (p[0] * nblk + i, 0))),
        out_shape=jax.ShapeDtypeStruct((2 * hr, r.shape[2]), F32),
        compiler_params=_cparams(("parallel",)), name=name)(place, cs, r, r, r, r)


def _join_halves(fs):
    n = len(fs)

    def body(*refs):
        ins, outs = refs[:n], refs[n:2 * n]
        copy = _remote(refs[2 * n], refs[2 * n + 1])
        x, y, c, _ = _place()
        cps = []
        for i in range(n):
            hr = ins[i].shape[0] // 2
            cps.append(copy(i, ins[i].at[pl.ds(c * hr, hr), :], outs[i].at[pl.ds(c * hr, hr), :], (x, y, 1 - c)))
            cps[-1].start()
        for i in range(n):
            hr = ins[i].shape[0] // 2
            other = outs[i].at[pl.ds((1 - c) * hr, hr), :]
            copy(i, other, other, (x, y, 1 - c)).wait_recv()
        for cp in cps:
            cp.wait_send()

    return pl.pallas_call(
        body, out_shape=[jax.ShapeDtypeStruct(f.shape, f.dtype) for f in fs], in_specs=[ANY] * n,
        out_specs=[ANY] * n, input_output_aliases={i: i for i in range(n)}, scratch_shapes=_sems(n),
        name="grad_join_halves")(*fs)


def _half_rows(n, hc):
    return pl.ds(hc * (n // 2), n // 2)


def _gather_ici_side(shards):
    n = len(shards)

    def copies(ins, outs, copy):
        x, y, c, chips = _place()
        me = 2 * x + y
        for i in range(n):
            rows = _half_rows(ins[i].shape[0], c)
            for j, (px, py) in enumerate(chips):
                yield (copy(3 * i + j, ins[i].at[rows, :], outs[i].at[me, rows, :], (px, py, c)),
                       outs[i].at[2 * px + py, rows, :])

    def start(ins, outs, copy):
        for cp, _ in copies(ins, outs, copy):
            cp.start()

    def finish(ins, outs, copy):
        x, y, c, chips = _place()
        k = 0
        for cp, landing in copies(ins, outs, copy):
            copy(k, landing, landing, (x, y, c)).wait_recv()
            k += 1
        for cp, _ in copies(ins, outs, copy):
            cp.wait_send()

    return dict(ins=list(shards), outs=[jax.ShapeDtypeStruct((4,) + s.shape, s.dtype) for s in shards], nsem=3 * n,
                start=start, finish=finish)


def _gather_d2d_side(shards, bufs):
    n = len(shards)

    def copies(ins, outs, copy):
        x, y, c, chips = _place()
        me = 2 * x + y
        sibling = (x, y, 1 - c)
        for i in range(n):
            a = ins[i].shape[0]
            yield copy(4 * i + 3, ins[i], outs[i].at[me], sibling), outs[i].at[me]
            for j, (px, py) in enumerate(chips):
                src = 2 * px + py
                mine = outs[i].at[src, _half_rows(a, c), :]
                yield copy(4 * i + j, mine, mine, sibling), outs[i].at[src, _half_rows(a, 1 - c), :]

    def start(ins, outs, copy):
        for cp, _ in copies(ins, outs, copy):
            cp.start()

    def finish(ins, outs, copy):
        x, y, c, _ = _place()
        for i in range(n):
            for k, (cp, landing) in enumerate(list(copies(ins, outs, copy))[4 * i:4 * i + 4]):
                sem = 4 * i + 3 if k == 0 else 4 * i + k - 1
                copy(sem, landing, landing, (x, y, 1 - c)).wait_recv()
        for cp, _ in copies(ins, outs, copy):
            cp.wait_send()

    return dict(ins=list(shards) + list(bufs), outs=[jax.ShapeDtypeStruct(b.shape, b.dtype) for b in bufs],
                nsem=4 * n, alias={n + i: i for i in range(n)}, start=start, finish=finish)


def _scatter_side(cs):
    n = len(cs)

    def copies(ins, outs, copy):
        x, y, c, chips = _place()
        me = 2 * x + y
        for i in range(n):
            for j, (px, py) in enumerate(chips):
                yield copy(3 * i + j, ins[i].at[2 * px + py], outs[i].at[me], (px, py, c)), outs[i].at[2 * px + py]

    def start(ins, outs, copy):
        for cp, _ in copies(ins, outs, copy):
            cp.start()

    def finish(ins, outs, copy):
        x, y, c, _ = _place()
        for k, (cp, landing) in enumerate(copies(ins, outs, copy)):
            copy(k, landing, landing, (x, y, c)).wait_recv()
        for cp, _ in copies(ins, outs, copy):
            cp.wait_send()

    return dict(ins=list(cs), outs=[jax.ShapeDtypeStruct(a.shape, a.dtype) for a in cs], nsem=3 * n,
                start=start, finish=finish)


def _mod_forward(cc, w_shard, b_shard):
    cc_all = _allreduce_small(cc, "cond_gather", reduce=False).reshape(64, D)
    part = _mm("mod_fwd", cc_all, w_shard, a_fn=_silu, bias=b_shard)
    got = _mod_rows_exchange(part)
    return jnp.concatenate([got[j] for j in range(4)], axis=1), cc_all


def _mod_backward(dmod, cc_all, w_shard, ctx_row):
    w = w_shard.shape[1]
    blocks = _dmod_exchange(jnp.transpose(dmod.reshape(8, 4, w), (1, 0, 2))).reshape(64, w)
    dsc = _mm("mod_bproj", blocks, w_shard, trans_b=True)

    def tail(cc_ref, dsc_ref, dm_ref, s_ref, dctx_ref, db_ref):
        cc_ = cc_ref[...]
        s = jax.nn.sigmoid(cc_)
        s_ref[...] = (cc_ * s).astype(BF16)
        is_ctx = ((lax.broadcasted_iota(jnp.int32, (64, 1), 0) & 7) == ctx_row).astype(F32)
        dctx_ref[...] = 0.5 * jnp.sum(dsc_ref[...] * (s * (1 + cc_ * (1 - s))) * is_ctx, axis=0, keepdims=True)
        db_ref[...] = jnp.sum(dm_ref[...], axis=0, keepdims=True)

    s_all, dctx, db_mod = pl.pallas_call(
        tail, out_shape=[jax.ShapeDtypeStruct((64, D), BF16), jax.ShapeDtypeStruct((1, D), F32),
                         jax.ShapeDtypeStruct((1, dmod.shape[1]), F32)], name="mod_tail")(cc_all, dsc, dmod)
    return _mm_tn("mod_wgrad", s_all, blocks), dctx[0], db_mod


def _small_pack(dw, af, ab):
    return jnp.concatenate([dw, jnp.zeros((1, dw.shape[1]), F32), jnp.concatenate([af, ab], axis=1)], axis=0)


def _grad_pieces(n, grads):
    if n == "small":
        return jnp.stack([_small_pack(grads["dw_weight"][:, 256 * j:256 * (j + 1)],
                                      grads["w_alpha_f"][:, DK * j:DK * (j + 1)],
                                      grads["w_alpha_b"][:, DK * j:DK * (j + 1)]) for j in range(4)])
    if n == "w_in":
        g = _unperm_in_cols(grads["w_in_p"])
        return jnp.transpose(g.reshape(D, 4, D_IN // 4), (1, 0, 2))
    g = grads[n]
    return g if g.ndim == 3 else g.reshape(4, g.shape[0] // 4, g.shape[1])


def _chip_sums(tag, names, grads, place):
    gs = [_grad_pieces(n, grads) for n in names]
    swapped = _swap_halves("grad_swap_" + tag, gs)
    return [_add_halves("grad_add_" + n, g, r, place) for n, g, r in zip(names, gs, swapped)]


class _Overlap:
    SCATTER = {"in_bproj": ("w2_down", "w2_gu", "w_out"),
               "ffn1_wgu": ("w_conv_out", "w_gla_out", "w_in", "small"),
               "ffn1_bgu": ("w1_down", "w1_gu")}
    GATHER = {("ffn1_gu", "ffn1_down"): ("w_in", "w_conv_out", "w_gla_out", "w_out"),
              ("in_proj", "gla_proj"): ("w2_gu", "w2_down")}
    LATE = tuple(n for names in GATHER.values() for n in names)

    def __init__(self, shard_of, install, place):
        self.shard_of, self.install, self.place = shard_of, install, place
        self.bufs, self.pending, self.landed = {}, None, {}

    def side(self, tag, grads):
        for (ici, d2d), names in self.GATHER.items():
            shards = [self.shard_of(n) for n in names]
            if tag == ici:
                return _gather_ici_side(shards)
            if tag == d2d:
                return _gather_d2d_side(shards, self.bufs[ici])
        if tag in self.SCATTER:
            names = self.SCATTER[tag]
            self.pending = (names, _chip_sums(tag, names, grads, self.place))
            return _scatter_side(self.pending[1])
        return None

    def done(self, tag, outs):
        for (ici, d2d), names in self.GATHER.items():
            if tag == ici:
                self.bufs[ici] = outs
                return
            if tag == d2d:
                self.install(dict(zip(names, outs)))
                return
        for n, cs, r in zip(*self.pending, outs):
            self.landed[n] = (cs, r)


def _with_side(comm, tag, grads, call):
    side = comm.side(tag, grads) if comm is not None else None
    if side is None:
        return call(None)
    res, outs = call(side)
    comm.done(tag, outs)
    return res


def _allreduce_small(v, name="allreduce_small", reduce=True):
    def body(x_ref, out_ref, *scratch):
        gath = out_ref if not reduce else scratch[0]
        send_sems, recv_sems, local_sem = scratch[-3:]
        x, y, c, chips = _place()
        me, sibling = (x, y, c), (x, y, 1 - c)

        def slot(px, py, pc):
            return gath.at[4 * px + 2 * py + pc]

        def copy(k, block, to, src=None):
            return pltpu.make_async_remote_copy(
                src_ref=slot(*block) if src is None else src, dst_ref=slot(*block), send_sem=send_sems.at[k],
                recv_sem=recv_sems.at[k], device_id=to, device_id_type=MESH)

        mine = pltpu.make_async_copy(x_ref, slot(*me), local_sem)
        mine.start()
        first = [copy(0, me, sibling, src=x_ref)]
        first += [copy(1 + j, me, (*chip, c), src=x_ref) for j, chip in enumerate(chips)]
        for cp in first:
            cp.start()
        passed = [copy(4 + j, (*chip, c), sibling) for j, chip in enumerate(chips)]
        for j, chip in enumerate(chips):
            copy(1 + j, (*chip, c), me).wait_recv()
            passed[j].start()
        copy(0, sibling, me).wait_recv()
        for j, chip in enumerate(chips):
            copy(4 + j, (*chip, 1 - c), me).wait_recv()
        for cp in first + passed:
            cp.wait_send()
        mine.wait()
        if reduce:
            acc = gath[0]
            for k in range(1, 8):
                acc = acc + gath[k]
            out_ref[...] = acc

    vm = pl.BlockSpec(memory_space=pltpu.VMEM)
    sems = [pltpu.SemaphoreType.DMA((7,)), pltpu.SemaphoreType.DMA((7,)), pltpu.SemaphoreType.DMA(())]
    return pl.pallas_call(
        body, out_shape=jax.ShapeDtypeStruct(v.shape if reduce else (8,) + v.shape, F32), in_specs=[vm], out_specs=vm,
        scratch_shapes=([pltpu.VMEM((8,) + v.shape, F32)] if reduce else []) + sems, name=name)(v)


def _mod_rows_exchange(part):
    w = part.shape[1]

    def body(p_ref, out_ref, send_sems, recv_sems):
        x, y, c, chips = _place()
        me = 2 * x + y
        copy = _remote(send_sems, recv_sems)
        sends = []
        for j, (px, py) in enumerate(chips):
            rows = pl.ds(pl.multiple_of(8 * (4 * px + 2 * py + c), 8), 8)
            sends.append(copy(j, p_ref.at[rows, :], out_ref.at[me], (px, py, c)))
            sends[-1].start()
        out_ref[me] = p_ref[pl.ds(pl.multiple_of(8 * (4 * x + 2 * y + c), 8), 8), :]
        for j, (px, py) in enumerate(chips):
            landing = out_ref.at[2 * px + py]
            copy(j, landing, landing, (px, py, c)).wait_recv()
        for cp in sends:
            cp.wait_send()

    vm = pl.BlockSpec(memory_space=pltpu.VMEM)
    return pl.pallas_call(body, out_shape=jax.ShapeDtypeStruct((4, 8, w), F32), in_specs=[vm], out_specs=vm,
                          scratch_shapes=_sems(3), name="mod_rows_exchange")(part)


def _dmod_exchange(dm):
    w = dm.shape[2]

    def body(d_ref, out_ref, send_sems, recv_sems):
        x, y, c, _ = _place()
        copy = _remote(send_sems, recv_sems)
        mine = 4 * x + 2 * y + c
        sends = []
        for r in range(1, 8):
            tx, ty, tc = x ^ (r >> 2), y ^ ((r >> 1) & 1), c ^ (r & 1)
            sends.append(copy(r - 1, d_ref.at[2 * tx + ty], out_ref.at[mine], (tx, ty, tc)))
            sends[-1].start()
        out_ref[mine] = d_ref[2 * x + y]
        for r in range(1, 8):
            tx, ty, tc = x ^ (r >> 2), y ^ ((r >> 1) & 1), c ^ (r & 1)
            landing = out_ref.at[4 * tx + 2 * ty + tc]
            copy(r - 1, landing, landing, (tx, ty, tc)).wait_recv()
        for cp in sends:
            cp.wait_send()

    vm = pl.BlockSpec(memory_space=pltpu.VMEM)
    return pl.pallas_call(body, out_shape=jax.ShapeDtypeStruct((8, 8, w), F32), in_specs=[vm], out_specs=vm,
                          scratch_shapes=_sems(7), name="dmod_exchange")(dm)


def _adamw(name, w, g, m, v):
    r, cols = w.shape
    budget = 262144
    tr = r if r * cols <= budget else next(c for c in (256, 128, 64, 32, 16, 8) if r % c == 0 and c * cols <= budget)

    def kern(w_ref, g_ref, m_ref, v_ref, d_ref, nm_ref, nv_ref):
        gv = g_ref[...]
        nm = ADAM_B1 * m_ref[...] + (1.0 - ADAM_B1) * gv
        nv = ADAM_B2 * v_ref[...] + (1.0 - ADAM_B2) * jnp.square(gv)
        m_hat = nm / (1.0 - ADAM_B1 ** ADAM_STEP)
        v_hat = nv / (1.0 - ADAM_B2 ** ADAM_STEP)
        d_ref[...] = -ADAM_LR * (m_hat / (jnp.sqrt(v_hat) + ADAM_EPS) + ADAM_WD * w_ref[...])
        nm_ref[...] = nm
        nv_ref[...] = nv

    spec = pl.BlockSpec((tr, cols), lambda i: (i, 0))
    shp = jax.ShapeDtypeStruct((r, cols), F32)
    return pl.pallas_call(kern, grid=(r // tr,), in_specs=[spec] * 4, out_specs=[spec] * 3, out_shape=[shp] * 3,
                          compiler_params=_cparams(("parallel",)), name=name)(w, g, m, v)


SHARDED = (("w_mod", 1), ("w1_gu", 1), ("w1_down", 0), ("w_in", 1), ("dw_weight", 1), ("w_conv_out", 0),
           ("w_alpha_f", 1), ("w_alpha_b", 1), ("w_gla_out", 0), ("w_out", 0), ("w2_gu", 1), ("w2_down", 0))
REPLICATED = ("c_ctx", "b_mod", "g_ffn1", "g_mix", "dw_bias", "conv_ln_g", "conv_ln_b", "b_alpha_f", "b_alpha_b",
              "gla_norm_g", "g_ffn2", "g_final")
WEIGHTS = ("c_ctx", "w_mod", "b_mod", "g_ffn1", "w1_gu", "w1_down", "g_mix", "w_in", "dw_weight", "dw_bias",
           "conv_ln_g", "conv_ln_b", "w_conv_out", "w_alpha_f", "b_alpha_f", "w_alpha_b", "b_alpha_b", "gla_norm_g",
           "w_gla_out", "w_out", "g_ffn2", "w2_gu", "w2_down", "g_final")
MATRICES = ("w1_gu", "w1_down", "w_in", "w_conv_out", "w_gla_out", "w_out", "w2_gu", "w2_down")


def _pack_flat(parts, align):
    flat = jnp.concatenate([p.reshape(-1) for p in parts])
    pad = (-flat.shape[0]) % align
    return jnp.concatenate([flat, jnp.zeros((pad,), flat.dtype)]).reshape(-1, 1024)


def _unpack_flat(flat2d, shapes):
    flat = flat2d.reshape(-1)
    out, off = [], 0
    for s in shapes:
        n = math.prod(s)
        out.append(flat[off:off + n].reshape(s))
        off += n
    return out


def kernel(x, c, ctx, c_ctx, w_mod, b_mod, g_ffn1, w1_gu, w1_down, g_mix, w_in, dw_weight, dw_bias, conv_ln_g, conv_ln_b, w_conv_out, w_alpha_f, b_alpha_f, w_alpha_b, b_alpha_b, gla_norm_g, w_gla_out, w_out, g_ffn2, w2_gu, w2_down, g_final, loss_target, m_c_ctx, m_w_mod, m_b_mod, m_g_ffn1, m_w1_gu, m_w1_down, m_g_mix, m_w_in, m_dw_weight, m_dw_bias, m_conv_ln_g, m_conv_ln_b, m_w_conv_out, m_w_alpha_f, m_b_alpha_f, m_w_alpha_b, m_b_alpha_b, m_gla_norm_g, m_w_gla_out, m_w_out, m_g_ffn2, m_w2_gu, m_w2_down, m_g_final, v_c_ctx, v_w_mod, v_b_mod, v_g_ffn1, v_w1_gu, v_w1_down, v_g_mix, v_w_in, v_dw_weight, v_dw_bias, v_conv_ln_g, v_conv_ln_b, v_w_conv_out, v_w_alpha_f, v_b_alpha_f, v_w_alpha_b, v_b_alpha_b, v_gla_norm_g, v_w_gla_out, v_w_out, v_g_ffn2, v_w2_gu, v_w2_down, v_g_final):
    given = dict(locals())
    w = {n: given[n] for n in WEIGHTS}
    m = {n: given["m_" + n] for n in WEIGHTS}
    v = {n: given["v_" + n] for n in WEIGHTS}

    def bf16_shard(n):
        return w[n][0].astype(BF16)

    def install(wts, got):
        for n in ("w1_gu", "w2_gu"):
            if n in got:
                wts[n] = got[n]
        for n in ("w1_down", "w2_down", "w_conv_out", "w_gla_out", "w_out"):
            if n in got:
                wts[n] = got[n].reshape(-1, D)
        if "w_in" in got:
            wts["w_in_p"] = _perm_in_cols(jnp.concatenate([got["w_in"][j] for j in range(4)], axis=1))

    early = tuple(n for n in MATRICES if n not in _Overlap.LATE)
    shards =[bf16_shard(n) for n in early] + [_small_pack(w["dw_weight"][0], w["w_alpha_f"][0], w["w_alpha_b"][0])]
    got = dict(zip(early + ("small",), _gather_weights(shards)))
    wts = {n: w[n] for n in REPLICATED}
    install(wts, got)
    chip = 2 * lax.axis_index("x") + lax.axis_index("y")
    mod_cols = w["w_mod"].shape[2]
    wts["w_mod_shard"] = bf16_shard("w_mod")
    wts["b_mod_shard"] = lax.dynamic_slice(w["b_mod"], (0, chip * mod_cols), (1, mod_cols))
    sm = got["small"]
    wts["dw_weight"] = jnp.concatenate([sm[j, :CONV_W] for j in range(4)], axis=1)
    zpad = jnp.zeros((128, HEADS * DK), BF16)
    w_af = jnp.concatenate([sm[j, 32:32 + LOWRANK, :DK] for j in range(4)], axis=1)
    w_ab = jnp.concatenate([sm[j, 32:32 + LOWRANK, DK:] for j in range(4)], axis=1)
    wts["w_alpha_f_pad"] = zpad.at[0:LOWRANK].set(w_af.astype(BF16))
    wts["w_alpha_b_pad"] = zpad.at[LOWRANK:2 * LOWRANK].set(w_ab.astype(BF16))

    place = jnp.stack([lax.axis_index("c"), 2 * lax.axis_index("x") + lax.axis_index("y")]).astype(jnp.int32)
    comm = _Overlap(bf16_shard, lambda got_late: install(wts, got_late), place)
    loss, grad_x, grads = _local_step(x, c, ctx, loss_target, wts, comm)
    loss = lax.psum(loss, ("x", "y", "c"))

    tags = MATRICES + ("small",)
    rest = tuple(n for n in tags if n not in comm.landed)
    if rest:
        rest_sums = _chip_sums("rest", rest, grads, place)
        for n, cs, r in zip(rest, rest_sums, _scatter_chips(rest_sums)):
            comm.landed[n] = (cs, r)
    halves = [_sum_chips("grad_sum_" + t, *comm.landed[t], place) for t in tags]
    reduced = dict(zip(tags, _join_halves(halves)))
    g_shard = {n: reduced[n] for n in MATRICES}
    g_shard["w_mod"] = grads["w_mod"]
    g_shard["dw_weight"] = reduced["small"][:CONV_W]
    g_shard["w_alpha_f"] = reduced["small"][32:32 + LOWRANK, :DK]
    g_shard["w_alpha_b"] = reduced["small"][32:32 + LOWRANK, DK:]

    rep_shapes = [w[n].shape for n in REPLICATED]
    small = _allreduce_small(_pack_flat([grads[n].reshape(w[n].shape) for n in REPLICATED], 8 * 1024))
    g_rep = dict(zip(REPLICATED, _unpack_flat(small, rep_shapes)))

    g_out, d_out, m_out, v_out = {}, {}, {}, {}
    for n, _ in SHARDED:
        s2 = w[n].shape[1:]
        d, nm, nv = _adamw("adamw_" + n, w[n].reshape(s2), g_shard[n], m[n].reshape(s2), v[n].reshape(s2))
        g_out[n] = g_shard[n].reshape(w[n].shape)
        d_out[n], m_out[n], v_out[n] = d.reshape(w[n].shape), nm.reshape(w[n].shape), nv.reshape(w[n].shape)
    pk = lambda t: _pack_flat([t[n] for n in REPLICATED], 8 * 1024)
    d, nm, nv = _adamw("adamw_vectors", pk(w), small, pk(m), pk(v))
    for n, dd, mm, vv in zip(REPLICATED, _unpack_flat(d, rep_shapes), _unpack_flat(nm, rep_shapes),
                             _unpack_flat(nv, rep_shapes)):
        g_out[n], d_out[n], m_out[n], v_out[n] = g_rep[n], dd, mm, vv

    return (loss, grad_x, *[g_out[n] for n in WEIGHTS], *[d_out[n] for n in WEIGHTS],
            *[m_out[n] for n in WEIGHTS], *[v_out[n] for n in WEIGHTS])
```

```python
import functools
import math

import jax
import jax.numpy as jnp
from jax import lax
from jax.experimental import pallas as pl
from jax.experimental.pallas import tpu as pltpu

F32, BF16 = jnp.float32, jnp.bfloat16
MESH = pl.DeviceIdType.MESH
HIGHEST = lax.Precision.HIGHEST

D = 1024
FF = 2816
HEADS, DK, DV = 4, 128, 256
LOWRANK = 16
CONV_W = 31
CHUNK = 64
TAU = 16.0
EPS = 1e-6
Q_SCALE = DK ** -0.5
TM = 256
D_IN = 7200
D_INP = 7296
LR_COL = 7168
OG_CB, GA_CB, GB_CB = 6, 4, 5
VMEM_LIMIT = 52 * 1024 * 1024
WGRAD_VMEM = 40 * 1024 * 1024

ADAM_LR, ADAM_B1, ADAM_B2, ADAM_EPS, ADAM_WD, ADAM_STEP = 0.001, 0.9, 0.999, 1e-08, 0.01, 10


def _silu(x):
    return x * jax.nn.sigmoid(x)


def _rms(h, g):
    return h * lax.rsqrt(jnp.mean(h * h, axis=-1, keepdims=True) + EPS) * g


def _modnorm(x, g, shift, scale):
    return _rms(x, g) * (1 + scale) + shift


def _cparams(sem=None):
    return pltpu.CompilerParams(dimension_semantics=sem, vmem_limit_bytes=VMEM_LIMIT)


def _twice(v):
    return v, v


def tok(arr, width=None, cb=0, clamp=None):
    return ("tok", arr, arr.shape[1] if width is None else width, cb, clamp)


def mod(arr):
    return ("mod", arr)


def const(arr):
    return ("const", arr)


def _rowmap(name, body, n_tiles, ins, outs, *, tpb, nb):
    def modrow(i):
        return jnp.minimum(i // tpb, nb)

    in_specs, args = [], []
    for spec in ins:
        if spec[0] == "tok":
            _, arr, width, cb, clamp = spec
            if clamp is None:
                im = lambda i, cb=cb: (i, cb)
            else:
                im = lambda i, cb=cb, clamp=clamp: (jnp.minimum(i, clamp), cb)
            in_specs.append(pl.BlockSpec((TM, width), im))
        elif spec[0] == "mod":
            arr = spec[1]
            in_specs.append(pl.BlockSpec((1, 1, arr.shape[2]), lambda i: (modrow(i), 0, 0)))
        else:
            arr = spec[1]
            in_specs.append(pl.BlockSpec(arr.shape, lambda i, nd=arr.ndim: (0,) * nd))
        args.append(arr)
    out_specs, out_shapes, aliases = [], [], {}
    for o in outs:
        if o[0] == "tok":
            _, rows, width, dtype = o
            out_shapes.append(jax.ShapeDtypeStruct((rows, width), dtype))
            out_specs.append(pl.BlockSpec((TM, width), lambda i: (i, 0)))
        elif o[0] == "cols":
            buf, width, cb = o[1:4]
            first_tile = o[4] if len(o) > 4 else 0
            if not isinstance(buf, jax.ShapeDtypeStruct):
                aliases[len(args)] = len(out_shapes)
                in_specs.append(ANY)
                args.append(buf)
            out_shapes.append(jax.ShapeDtypeStruct(buf.shape, buf.dtype))
            out_specs.append(pl.BlockSpec((TM, width), lambda i, cb=cb, t0=first_tile: (i + t0, cb)))
        elif o[0] == "tok_head":
            _, rows, width, dtype, last = o
            out_shapes.append(jax.ShapeDtypeStruct((rows, width), dtype))
            out_specs.append(pl.BlockSpec((TM, width), lambda i, last=last: (jnp.minimum(i, last), 0)))
        elif o[0] == "tokT":
            _, rows, width, dtype = o
            out_shapes.append(jax.ShapeDtypeStruct((width, rows), dtype))
            out_specs.append(pl.BlockSpec((width, TM), lambda i: (0, i)))
        elif o[0] == "acc":
            _, rows, width = o
            out_shapes.append(jax.ShapeDtypeStruct((rows, width), F32))
            out_specs.append(pl.BlockSpec((rows, width), lambda i: (0, 0)))
        else:
            width = o[1]
            rows_visited = min((n_tiles - 1) // tpb, nb) + 1
            out_shapes.append(jax.ShapeDtypeStruct((rows_visited, 1, width), F32))
            out_specs.append(pl.BlockSpec((1, 1, width), lambda i: (modrow(i), 0, 0)))
    n_in = len(ins)

    def kern(*refs):
        i = pl.program_id(0)
        vals = []
        for r, spec in zip(refs[:n_in], ins):
            val = r[0] if spec[0] == "mod" else r[...]
            vals.append(val.astype(F32) if spec[0] == "tok" and val.dtype == BF16 else val)
        res = body(i, *vals)
        for r, o, val in zip(refs[len(args):], outs, res):
            if o[0] in ("tok", "cols"):
                r[...] = val.astype(r.dtype)
            elif o[0] == "tok_head":
                @pl.when(i <= o[4])
                def _():
                    r[...] = val.astype(r.dtype)
            elif o[0] == "tokT":
                r[...] = val.T.astype(r.dtype)
            elif o[0] == "acc":
                @pl.when(i == 0)
                def _():
                    r[...] = jnp.zeros(r.shape, F32)
                r[...] += jnp.broadcast_to(val, r.shape)
            else:
                first = jnp.logical_or(i == 0, modrow(i) != modrow(jnp.maximum(i - 1, 0)))

                @pl.when(first)
                def _():
                    r[...] = jnp.zeros(r.shape, F32)
                r[0] += val

    return pl.pallas_call(
        kern, grid=(n_tiles,), in_specs=in_specs, out_specs=out_specs, out_shape=out_shapes,
        input_output_aliases=aliases, compiler_params=_cparams(("arbitrary",)), name=name)(*args)


def _pick(n, cands):
    for c in cands:
        if n % c == 0:
            return c
    return n


def _pallas(kern, *, grid, in_specs, out_specs, out_shape, scratch_shapes, sem, name, args, side=None):
    if side is None:
        return pl.pallas_call(kern, grid=grid, in_specs=in_specs, out_specs=out_specs, out_shape=out_shape,
                              scratch_shapes=scratch_shapes, compiler_params=_cparams(sem), name=name)(*args)
    single = not isinstance(out_shape, (list, tuple))
    shapes = [out_shape] if single else list(out_shape)
    ospecs = [out_specs] if single else list(out_specs)
    n_in, n_out, n_scr = len(in_specs), len(shapes), len(scratch_shapes)
    s_in, s_out = list(side["ins"]), list(side["outs"])

    def wrapped(*refs):
        pos = [0]

        def take(n):
            pos[0] += n
            return refs[pos[0] - n:pos[0]]
        ins, sins, outs, souts, scr, sems = take(n_in), take(len(s_in)), take(n_out), take(len(s_out)), take(n_scr), take(2)
        ids = [pl.program_id(k) for k in range(len(grid))]
        first = functools.reduce(jnp.logical_and, [i == 0 for i in ids])
        last = functools.reduce(jnp.logical_and, [i == g - 1 for i, g in zip(ids, grid)])
        copy = _remote(*sems)

        @pl.when(first)
        def _():
            side["start"](sins, souts, copy)
        kern(*ins, *outs, *scr)

        @pl.when(last)
        def _():
            side["finish"](sins, souts, copy)

    res = pl.pallas_call(
        wrapped, grid=grid, in_specs=list(in_specs) + [ANY] * len(s_in), out_specs=ospecs + [ANY] * len(s_out),
        out_shape=shapes + s_out, scratch_shapes=list(scratch_shapes) + _sems(side["nsem"]),
        input_output_aliases={n_in + a: n_out + b for a, b in side.get("alias", {}).items()},
        compiler_params=_cparams(("arbitrary",) * len(grid)), name=name)(*args, *s_in)
    main = res[:n_out]
    return (main[0] if single else main), list(res[n_out:])


def _mm(name, a, b, *, trans_b=False, out_dtype=F32, a_fn=None, bias=None, rows=None, side=None, residual=None):
    m, k = a.shape if a.ndim == 2 else (a.shape[1], 2 * a.shape[2])
    m = m if rows is None else rows
    shard = b.shape[2] if b.ndim == 3 else None
    if trans_b:
        n = b.shape[-2]
        tk = _pick(shard, (2816, 2304, 1408, 1024)) if shard else (
            k if k <= 2816 else _pick(k, (2816, 2432, 2304, 2048, 1536, 1408, 1024, 512, 256, 128)))
        tn = _pick(n, (1024, 512, 384, 256, 128))
    else:
        n = 4 * shard if shard else b.shape[1]
        tk = k if k <= 2816 else _pick(k, (2816, 2432, 2304, 2048, 1536, 1408, 1024, 512, 256, 128))
        tn = _pick(shard, (512, 384, 1408, 256, 128)) if shard else _pick(n, (1024, 2432, 512, 384, 256, 128))
    tm = _pick(m, (512, 256) if residual is not None else (1024, 512, 256))
    nk = k // tk
    per = shard // (tk if trans_b else tn) if shard else None
    dims = (((1,), (1,)), ((), ())) if trans_b else (((1,), (0,)), ((), ()))

    def kern(*refs):
        a_ref, b_ref = refs[0], refs[1]
        bias_ref = refs[2] if bias is not None else None
        acc_ref = refs[-1]
        if residual is not None:
            x_ref, gate_ref, o_ref, xo_ref = refs[-5:-1]
        else:
            o_ref = refs[-2]
        kk = pl.program_id(2)
        av = a_ref[...]
        if a_fn is not None:
            av = a_fn(av)
        p = lax.dot_general(av.astype(BF16), b_ref[...].astype(BF16), dims, preferred_element_type=F32)

        def finish(total):
            if bias_ref is not None:
                total = total + bias_ref[...]
            o_ref[...] = total.astype(o_ref.dtype)
            if residual is not None:
                xo_ref[...] = x_ref[...] + residual["scale"] * gate_ref[0] * total

        if nk == 1:
            finish(p)
        else:
            @pl.when(kk == 0)
            def _():
                acc_ref[...] = p

            @pl.when(kk > 0)
            def _():
                acc_ref[...] += p

            @pl.when(kk == nk - 1)
            def _():
                finish(acc_ref[...])

    if shard and trans_b:
        b_spec = pl.BlockSpec((None, tn, tk), lambda i, j, kk: (kk // per, j, kk % per))
    elif shard:
        b_spec = pl.BlockSpec((None, tk, tn), lambda i, j, kk: (j // per, kk, j % per))
    elif trans_b:
        b_spec = pl.BlockSpec((tn, tk), lambda i, j, kk: (j, kk))
    else:
        b_spec = pl.BlockSpec((tk, tn), lambda i, j, kk: (kk, j))
    if a.ndim == 3:
        pa = a.shape[2] // tk
        a_spec = pl.BlockSpec((None, tm, tk), lambda i, j, kk: (kk // pa, i, kk % pa))
    else:
        a_spec = pl.BlockSpec((tm, tk), lambda i, j, kk: (i, kk))
    in_specs = [a_spec, b_spec]
    args = [a, b]
    if bias is not None:
        in_specs.append(pl.BlockSpec((1, tn), lambda i, j, kk: (0, j)))
        args.append(bias)
    out_specs = pl.BlockSpec((tm, tn), lambda i, j, kk: (i, j))
    out_shape = jax.ShapeDtypeStruct((m, n), out_dtype)
    if residual is not None:
        tiles, nb = residual["tpb"] * TM // tm, residual["gate"].shape[0] - 1
        in_specs += [out_specs, pl.BlockSpec((1, 1, tn), lambda i, j, kk: (jnp.minimum(i // tiles, nb), 0, j))]
        args += [residual["x"], residual["gate"]]
        out_specs, out_shape = [out_specs, out_specs], [out_shape, jax.ShapeDtypeStruct((m, n), F32)]
    return _pallas(
        kern, grid=(m // tm, n // tn, nk), in_specs=in_specs, out_specs=out_specs, out_shape=out_shape,
        scratch_shapes=[pltpu.VMEM((tm, tn) if nk > 1 else (8, 128), F32)],
        sem=("parallel", "parallel", "arbitrary"), name=name, args=args, side=side)


def _mm_tn(name, x, dy):
    t, k1, n1 = x.shape[0], x.shape[1], dy.shape[1]
    tt = _pick(t, (512, 256, 128, 64, 8))
    tk1 = _pick(k1, (1024, 512, 256, 128))
    tn = _pick(n1, (512, 384, 256, 128))
    ns = t // tt

    def kern(x_ref, dy_ref, o_ref, acc_ref):
        s = pl.program_id(2)
        p = lax.dot_general(x_ref[...].astype(BF16), dy_ref[...].astype(BF16), (((0,), (0,)), ((), ())),
                            preferred_element_type=F32)

        @pl.when(s == 0)
        def _():
            acc_ref[...] = p

        @pl.when(s > 0)
        def _():
            acc_ref[...] += p

        @pl.when(s == ns - 1)
        def _():
            o_ref[...] = acc_ref[...].astype(o_ref.dtype)

    return pl.pallas_call(
        kern, grid=(k1 // tk1, n1 // tn, ns),
        in_specs=[pl.BlockSpec((tt, tk1), lambda i, j, s: (s, i)), pl.BlockSpec((tt, tn), lambda i, j, s: (s, j))],
        out_specs=pl.BlockSpec((tk1, tn), lambda i, j, s: (i, j)), out_shape=jax.ShapeDtypeStruct((k1, n1), F32),
        scratch_shapes=[pltpu.VMEM((tk1, tn), F32)],
        compiler_params=_cparams(("parallel", "parallel", "arbitrary")), name=name)(x, dy)


def _wgrad(name, xt, dy, rows=None, col_shards=False, side=None):
    k1 = xt.shape[0]
    t = xt.shape[1] if rows is None else rows
    n1 = dy.shape[1] if dy.ndim == 2 else 2 * dy.shape[2]
    tm = _pick(k1, (1024, 1408, 512, 256))
    tn = _pick(n1 // 4, (1408, 512, 384, 256, 128)) if col_shards else _pick(n1, (1024, 2432, 512, 384, 256, 128))
    fixed = tm * tn * (4 + 4 + 2 * 2)
    tk = next((c for c in (2048, 1536, 1024, 512, 256, 128)
               if t % c == 0 and fixed + 4 * c * (tm + tn) <= WGRAD_VMEM), 128)
    ns = t // tk
    per = n1 // 4 // tn

    def kern(x_ref, dy_ref, o_ref, acc_ref):
        s = pl.program_id(2)
        p = jnp.dot(x_ref[...], dy_ref[...], preferred_element_type=F32)

        @pl.when(s == 0)
        def _():
            acc_ref[...] = p

        @pl.when(s > 0)
        def _():
            acc_ref[...] += p

        @pl.when(s == ns - 1)
        def _():
            o_ref[...] = acc_ref[...].astype(o_ref.dtype)

    if dy.ndim == 3:
        pd = dy.shape[2] // tn
        dy_spec = pl.BlockSpec((None, tk, tn), lambda i, j, s: (j // pd, s, j % pd))
    else:
        dy_spec = pl.BlockSpec((tk, tn), lambda i, j, s: (s, j))
    if col_shards:
        out_spec = pl.BlockSpec((None, tm, tn), lambda i, j, s: (j // per, i, j % per))
        out_shape = jax.ShapeDtypeStruct((4, k1, n1 // 4), BF16)
    else:
        out_spec = pl.BlockSpec((tm, tn), lambda i, j, s: (i, j))
        out_shape = jax.ShapeDtypeStruct((k1, n1), BF16)
    return _pallas(
        kern, grid=(k1 // tm, n1 // tn, ns),
        in_specs=[pl.BlockSpec((tm, tk), lambda i, j, s: (i, s)), dy_spec],
        out_specs=out_spec, out_shape=out_shape, scratch_shapes=[pltpu.VMEM((tm, tn), F32)],
        sem=("parallel", "parallel", "arbitrary"), name=name, args=[xt, dy], side=side)


def _swiglu_fwd(name, u, w_gu, side=None):
    m = u.shape[0]
    half = w_gu.shape[2]
    tm = _pick(m, (512, 256))

    def kern(u_ref, wa_ref, wb_ref, ab_ref, hm_ref, hmt_ref):
        uv = u_ref[...]
        a = jnp.dot(uv, wa_ref[...], preferred_element_type=F32)
        b = jnp.dot(uv, wb_ref[...], preferred_element_type=F32)
        s = jax.nn.sigmoid(a)
        silu_a = a * s
        ab_ref[0] = (b * (s * (1 + a * (1 - s)))).astype(BF16)
        ab_ref[1] = silu_a.astype(BF16)
        hm = (silu_a * b).astype(BF16)
        hm_ref[...] = hm
        hmt_ref[...] = hm.T

    return _pallas(
        kern, grid=(2, m // tm),
        in_specs=[pl.BlockSpec((tm, D), lambda j, i: (i, 0)),
                  pl.BlockSpec((None, D, half), lambda j, i: (j, 0, 0)),
                  pl.BlockSpec((None, D, half), lambda j, i: (2 + j, 0, 0))],
        out_specs=[pl.BlockSpec((2, tm, half), lambda j, i: (0, i, j)),
                   pl.BlockSpec((tm, half), lambda j, i: (i, j)),
                   pl.BlockSpec((half, tm), lambda j, i: (j, i))],
        out_shape=[jax.ShapeDtypeStruct((2, m, FF), BF16), jax.ShapeDtypeStruct((m, FF), BF16),
                   jax.ShapeDtypeStruct((FF, m), BF16)],
        scratch_shapes=[], sem=("parallel", "parallel"), name=name, args=[u, w_gu, w_gu], side=side)


def _swiglu_bwd(name, df, w_down, ab):
    m = df.shape[0]
    half = FF // 2
    tm = _pick(m, (512, 256))

    def kern(df_ref, w_ref, ab_ref, o_ref):
        dh = lax.dot_general(df_ref[...], w_ref[...], (((1,), (1,)), ((), ())), preferred_element_type=F32)
        o_ref[0] = (dh * ab_ref[0].astype(F32)).astype(BF16)
        o_ref[1] = (dh * ab_ref[1].astype(F32)).astype(BF16)

    return pl.pallas_call(
        kern, grid=(2, m // tm),
        in_specs=[pl.BlockSpec((tm, D), lambda j, i: (i, 0)),
                  pl.BlockSpec((half, D), lambda j, i: (j, 0)),
                  pl.BlockSpec((2, tm, half), lambda j, i: (0, i, j))],
        out_specs=pl.BlockSpec((2, tm, half), lambda j, i: (0, i, j)),
        out_shape=jax.ShapeDtypeStruct((2, m, FF), BF16),
        compiler_params=_cparams(("parallel", "parallel")), name=name)(df, w_down, ab)


def _gla_maps(bl, t, tc):
    nx, nc = t // CHUNK, tc // CHUNK
    nxb = bl * nx

    def rowblk(d, b, n):
        c_ctx = jnp.where(d == 0, n, nc - 1 - n)
        c_x = jnp.where(d == 0, n - nc, nx - 1 - (n - nc))
        return jnp.where(n < nc, nxb + b * nc + c_ctx, b * nx + c_x)

    def xblk(d, b, n):
        n2 = jnp.maximum(n, nc)
        return b * nx + jnp.where(d == 0, n2 - nc, nx - 1 - (n2 - nc))

    return nx, nc, rowblk, xblk


def _dot01(m, x, cm):
    x1 = x.astype(BF16)
    r1 = x - x1.astype(F32)
    x2 = r1.astype(BF16)
    x3 = (r1 - x2.astype(F32)).astype(BF16)
    w = x.shape[1]
    p = lax.dot_general(m.astype(BF16), jnp.concatenate([x1, x2, x3], axis=1), (((cm,), (0,)), ((), ())),
                        preferred_element_type=F32)
    return p[:, :w] + p[:, w:2 * w] + p[:, 2 * w:]


def _gla_chunk(m, q, k, v, g, h):
    gh = g[:, h * DK:(h + 1) * DK]
    b = _dot01(m, gh, 1)
    tot = jnp.sum(gh, axis=0, keepdims=True)
    mid = b[CHUNK // 2:CHUNK // 2 + 1, :]
    qh = q[:, h * DK:(h + 1) * DK] * Q_SCALE
    kh = k[:, h * DK:(h + 1) * DK]
    vh = v[:, h * DV:(h + 1) * DV]
    return b, tot, mid, qh, kh, vh


def _dot(a, b, ca, cb):
    return lax.dot_general(a.astype(BF16), b.astype(BF16), (((ca,), (cb,)), ((), ())),
                           preferred_element_type=F32)


def _gla_fwd(p, g2, mmats, bl, t, tc):
    nx, nc, rowblk, xblk = _gla_maps(bl, t, tc)
    ns = nx + nc

    def kern(q0, k0, v0, g0, q1, k1, v1, g1, m_ref, o0, o1, ss_ref, s_ref):
        n = pl.program_id(1)

        @pl.when(n == 0)
        def _():
            s_ref[...] = jnp.zeros(s_ref.shape, F32)
        sides = ((q0, k0, v0, g0, o0), (q1, k1, v1, g1, o1))
        loaded = [(m_ref[d], q_ref[...].astype(F32), k_ref[...].astype(F32), v_ref[...].astype(F32), g_ref[...])
                  for d, (q_ref, k_ref, v_ref, g_ref, _) in enumerate(sides)]
        chains = [(d, h) for d in range(2) for h in range(HEADS)]
        base = [_gla_chunk(*loaded[d], h) for d, h in chains]
        pre = []
        for (d, h), (b, tot, mid, qh, kh, vh) in zip(chains, base):
            s0 = s_ref[d, h * DV:(h + 1) * DV, :]
            ss_ref[d, 0, 0, h * DV:(h + 1) * DV, :] = s0.astype(BF16)
            pre.append((s0, kh * jnp.exp(tot - b), qh * jnp.exp(b), qh * jnp.exp(b - mid), kh * jnp.exp(mid - b)))
        raw = [(_dot(qm, km, 1, 1), _dot(qe, s0, 1, 1), _dot(bs[5], kl, 0, 0))
               for bs, (s0, kl, qe, qm, km) in zip(base, pre)]
        for (d, h), bs, (s0, kl, qe, qm, km), (att_raw, inter, s_add) in zip(chains, base, pre, raw):
            s_ref[d, h * DV:(h + 1) * DV, :] = s0 * jnp.exp(bs[1]) + s_add
            sides[d][4][:, h * DV:(h + 1) * DV] = inter + _dot(loaded[d][0] * att_raw, bs[5], 1, 0)

    def operands(d):
        return [pl.BlockSpec((CHUNK, 512), lambda b, n: (rowblk(d, b, n), 4)),
                pl.BlockSpec((CHUNK, 512), lambda b, n: (rowblk(d, b, n), 5)),
                pl.BlockSpec((CHUNK, 1024), lambda b, n: (rowblk(d, b, n), 3)),
                pl.BlockSpec((CHUNK, 512), lambda b, n: (rowblk(d, b, n), d))]

    o_shape = jax.ShapeDtypeStruct((bl * t, HEADS * DV), F32)
    return pl.pallas_call(
        kern, grid=(bl, ns),
        in_specs=operands(0) + operands(1) + [pl.BlockSpec((2, CHUNK, CHUNK), lambda b, n: (0, 0, 0))],
        out_specs=[pl.BlockSpec((CHUNK, 1024), lambda b, n: (xblk(0, b, n), 0)),
                   pl.BlockSpec((CHUNK, 1024), lambda b, n: (xblk(1, b, n), 0)),
                   pl.BlockSpec((2, 1, 1, HEADS * DV, DK), lambda b, n: (0, b, n, 0, 0))],
        out_shape=[o_shape, o_shape, jax.ShapeDtypeStruct((2, bl, ns, HEADS * DV, DK), BF16)],
        scratch_shapes=[pltpu.VMEM((2, HEADS * DV, DK), F32)],
        compiler_params=_cparams(("parallel", "arbitrary")), name="gla_fwd")(p, p, p, g2, p, p, p, g2, mmats)


def _gla_bwd(p, g2, mmats, ssave, do, bl, t, tc):
    nx, nc, rowblk, xblk = _gla_maps(bl, t, tc)
    ns = nx + nc
    rev = lambda s: ns - 1 - s

    def kern(q0, k0, v0, g0, do0, q1, k1, v1, g1, do1, m_ref, ss_ref,
             dq0, dk0, dv0, dg0, dq1, dk1, dv1, dg1, ds_ref):
        step = pl.program_id(1)
        n = ns - 1 - step

        @pl.when(step == 0)
        def _():
            ds_ref[...] = jnp.zeros(ds_ref.shape, F32)
        live = (n >= nc).astype(F32)
        sides = ((q0, k0, v0, g0, do0, dq0, dk0, dv0, dg0), (q1, k1, v1, g1, do1, dq1, dk1, dv1, dg1))
        loaded = [(m_ref[d], s[0][...].astype(F32), s[1][...].astype(F32), s[2][...].astype(F32), s[3][...])
                  for d, s in enumerate(sides)]
        dovs = [s[4][...] * live for s in sides]
        chains = [(d, h) for d in range(2) for h in range(HEADS)]
        base = [_gla_chunk(*loaded[d], h) for d, h in chains]
        pre = []
        for (d, h), (b, tot, mid, qh, kh, vh) in zip(chains, base):
            eb, ebm, emb, etb = jnp.exp(b), jnp.exp(b - mid), jnp.exp(mid - b), jnp.exp(tot - b)
            pre.append(dict(
                eb=eb, ebm=ebm, emb=emb, etb=etb, etot=jnp.exp(tot), qe=qh * eb, qm=qh * ebm, km=kh * emb, kl=kh * etb,
                vh=vh, doh=dovs[d][:, h * DV:(h + 1) * DV], s0=ss_ref[d, 0, 0, h * DV:(h + 1) * DV, :].astype(F32),
                ds1=ds_ref[d, h * DV:(h + 1) * DV, :]))
        first = [dict(att=_dot(c["qm"], c["km"], 1, 1), datt=_dot(c["doh"], c["vh"], 1, 1),
                      dqe=_dot(c["doh"], c["s0"], 1, 0), ds_add=_dot(c["doh"], c["qe"], 0, 0),
                      dkl=_dot(c["vh"], c["ds1"], 1, 0), dv_s=_dot(c["kl"], c["ds1"], 1, 1)) for c in pre]
        second = []
        for (d, h), c, f in zip(chains, pre, first):
            m = loaded[d][0]
            ds_ref[d, h * DV:(h + 1) * DV, :] = c["ds1"] * c["etot"] + f["ds_add"]
            att, datt = m * f["att"], m * f["datt"]
            second.append(dict(dqm=_dot(datt, c["km"], 1, 0), dkm=_dot(datt, c["qm"], 0, 0),
                               dv_a=_dot(att, c["doh"], 0, 0)))
        for (d, h), c, f, s in zip(chains, pre, first, second):
            dq_ref, dk_ref, dv_ref, dg_ref = sides[d][5:]
            dtot = c["etot"] * jnp.sum(c["ds1"] * c["s0"], axis=0, keepdims=True) + jnp.sum(
                f["dkl"] * c["kl"], axis=0, keepdims=True)
            db = f["dqe"] * c["qe"] + s["dqm"] * c["qm"] - s["dkm"] * c["km"] - f["dkl"] * c["kl"]
            dq_ref[:, h * DK:(h + 1) * DK] = ((f["dqe"] * c["eb"] + s["dqm"] * c["ebm"]) * Q_SCALE).astype(BF16)
            dk_ref[:, h * DK:(h + 1) * DK] = (s["dkm"] * c["emb"] + f["dkl"] * c["etb"]).astype(BF16)
            dv_ref[:, h * DV:(h + 1) * DV] = (s["dv_a"] + f["dv_s"]).astype(BF16)
            dg_ref[:, h * DK:(h + 1) * DK] = _dot01(loaded[d][0], db, 0) + dtot

    nt = p.shape[0]

    def operands(d):
        return [pl.BlockSpec((CHUNK, 512), lambda b, s: (rowblk(d, b, rev(s)), 4)),
                pl.BlockSpec((CHUNK, 512), lambda b, s: (rowblk(d, b, rev(s)), 5)),
                pl.BlockSpec((CHUNK, 1024), lambda b, s: (rowblk(d, b, rev(s)), 3)),
                pl.BlockSpec((CHUNK, 512), lambda b, s: (rowblk(d, b, rev(s)), d)),
                pl.BlockSpec((CHUNK, 1024), lambda b, s: (xblk(d, b, rev(s)), 0))]

    def results(d):
        row = lambda b, s: (rowblk(d, b, rev(s)), 0)
        return [pl.BlockSpec((CHUNK, 512), row), pl.BlockSpec((CHUNK, 512), row), pl.BlockSpec((CHUNK, 1024), row),
                pl.BlockSpec((CHUNK, 512), row)]

    shapes = [jax.ShapeDtypeStruct((nt, 512), BF16), jax.ShapeDtypeStruct((nt, 512), BF16),
              jax.ShapeDtypeStruct((nt, 1024), BF16), jax.ShapeDtypeStruct((nt, 512), F32)]
    out = pl.pallas_call(
        kern, grid=(bl, ns),
        in_specs=operands(0) + operands(1) + [
            pl.BlockSpec((2, CHUNK, CHUNK), lambda b, s: (0, 0, 0)),
            pl.BlockSpec((2, 1, 1, HEADS * DV, DK), lambda b, s: (0, b, rev(s), 0, 0))],
        out_specs=results(0) + results(1), out_shape=shapes + shapes,
        scratch_shapes=[pltpu.VMEM((2, HEADS * DV, DK), F32)],
        compiler_params=_cparams(("parallel", "arbitrary")), name="gla_bwd")(
            p, p, p, g2, do, p, p, p, g2, do, mmats, ssave)
    return out[:4], out[4:]


CONV_CT = 256
CONV_PAD = 16
CONV_RC = 128
CONV_HALO = 24


def _conv_fill(zp, z_ref, t):
    zp[0:CONV_PAD, :] = jnp.zeros((CONV_PAD, CONV_CT), F32)
    zp[CONV_PAD + t:2 * CONV_PAD + t, :] = jnp.zeros((CONV_PAD, CONV_CT), F32)
    zp[CONV_PAD:CONV_PAD + t, :] = z_ref[...]


def _dwconv(name, z, w, bias, bl, t, flip):
    def kern(z_ref, w_ref, b_ref, o_ref, zp):
        _conv_fill(zp, z_ref, t)
        offs = [(CONV_W - j) if flip else (j + 1) for j in range(CONV_W)]
        for r in range(0, t, CONV_RC):
            acc = jnp.broadcast_to(b_ref[...], (CONV_RC, CONV_CT))
            for rot in range(8):
                win = zp[r + rot:r + rot + CONV_RC + CONV_HALO, :]
                for j in range(CONV_W):
                    if offs[j] % 8 == rot:
                        a = offs[j] - rot
                        acc = acc + w_ref[j:j + 1, :] * win[a:a + CONV_RC, :]
            o_ref[r:r + CONV_RC, :] = acc

    return pl.pallas_call(
        kern, grid=(bl, 1024 // CONV_CT),
        in_specs=[pl.BlockSpec((t, CONV_CT), lambda b, c: (b, c)),
                  pl.BlockSpec((32, CONV_CT), lambda b, c: (0, c)),
                  pl.BlockSpec((1, CONV_CT), lambda b, c: (0, c))],
        out_specs=pl.BlockSpec((t, CONV_CT), lambda b, c: (b, c)),
        out_shape=jax.ShapeDtypeStruct(z.shape, F32),
        scratch_shapes=[pltpu.VMEM((t + 2 * CONV_PAD, CONV_CT), F32)],
        compiler_params=_cparams(("parallel", "parallel")), name=name)(z, w, bias)


def _dwconv_wgrad(z, dzc, bl, t):
    def kern(z_ref, d_ref, dw_ref, db_ref, zp):
        b = pl.program_id(1)

        @pl.when(b == 0)
        def _():
            dw_ref[...] = jnp.zeros(dw_ref.shape, F32)
            db_ref[...] = jnp.zeros(db_ref.shape, F32)
        _conv_fill(zp, z_ref, t)
        for rot in range(8):
            taps = [j for j in range(CONV_W) if (j + 1) % 8 == rot]
            accs = [jnp.zeros((8, CONV_CT), F32) for _ in taps]
            for r in range(0, t, CONV_RC):
                d = d_ref[r:r + CONV_RC, :]
                win = zp[r + rot:r + rot + CONV_RC + CONV_HALO, :]
                for k, j in enumerate(taps):
                    a = j + 1 - rot
                    prod = d * win[a:a + CONV_RC, :]
                    accs[k] = accs[k] + jnp.sum(prod.reshape(CONV_RC // 8, 8, CONV_CT), axis=0)
            for k, j in enumerate(taps):
                dw_ref[j:j + 1, :] += jnp.sum(accs[k], axis=0, keepdims=True)
        db_ref[...] += jnp.sum(d_ref[...], axis=0, keepdims=True)

    return pl.pallas_call(
        kern, grid=(1024 // CONV_CT, bl),
        in_specs=[pl.BlockSpec((t, CONV_CT), lambda c, b: (b, c)),
                  pl.BlockSpec((t, CONV_CT), lambda c, b: (b, c))],
        out_specs=[pl.BlockSpec((32, CONV_CT), lambda c, b: (0, c)),
                   pl.BlockSpec((1, CONV_CT), lambda c, b: (0, c))],
        out_shape=[jax.ShapeDtypeStruct((32, 1024), F32), jax.ShapeDtypeStruct((1, 1024), F32)],
        scratch_shapes=[pltpu.VMEM((t + 2 * CONV_PAD, CONV_CT), F32)],
        compiler_params=_cparams(("parallel", "arbitrary")), name="dwconv_wgrad")(z, dzc)


def _ffn_fwd(tag, xin, n_tiles, g, sh, sc, gate, w_gu, w_down, tpb, nb, comm=None):
    rows = n_tiles * TM
    rm = functools.partial(_rowmap, tpb=tpb, nb=nb)
    u, ut = rm(tag + "_norm", lambda i, x, g_, sh_, sc_: _twice(_modnorm(x, g_, sh_, sc_)), n_tiles,
               [tok(xin), const(g), mod(sh), mod(sc)], [("tok", rows, D, BF16), ("tokT", rows, D, BF16)])
    ab, hm, hmt = _with_side(comm, tag + "_gu", None, lambda s: _swiglu_fwd(tag + "_gu", u, w_gu, side=s))
    res = dict(x=xin, gate=gate, scale=0.5, tpb=tpb)
    f, xout = _with_side(comm, tag + "_down", None,
                         lambda s: _mm(tag + "_down", hm, w_down, out_dtype=BF16, side=s, residual=res))
    return xout, (ut, ab, hmt, f)


def _ffn_bwd(tag, xin, saved, dxout, dx_clamp, n_tiles, g, sh, sc, gate, w_gu, w_down, tpb, nb, comm=None, grads=None,
             names=None, keep_tiles=None):
    ut, ab, hmt, f = saved
    rows = n_tiles * TM
    rm = functools.partial(_rowmap, tpb=tpb, nb=nb)

    def mask(i):
        return 1.0 if dx_clamp is None else (i <= dx_clamp).astype(F32)

    def b1(i, dx, f_, gt):
        dx = dx * mask(i)
        return (0.5 * gt * dx, jnp.sum(0.5 * f_ * dx, axis=0, keepdims=True))
    df, dgate = rm(tag + "_bres", b1, n_tiles, [tok(dxout, clamp=dx_clamp), tok(f), mod(gate)],
                   [("tok", rows, D, BF16), ("modacc", D)])
    grads[names[1]] = _wgrad(tag + "_wdown", hmt, df)
    dab = _swiglu_bwd(tag + "_bdown", df, w_down, ab)
    grads[names[0]] = _with_side(comm, tag + "_wgu", grads,
                                 lambda s: _wgrad(tag + "_wgu", ut, dab, col_shards=True, side=s))
    du = _with_side(comm, tag + "_bgu", grads, lambda s: _mm(tag + "_bgu", dab, w_gu, trans_b=True, side=s))

    def b3(i, x, g_, sh_, sc_, du_, dx):
        _, vjp = jax.vjp(_modnorm, x, g_, sh_, sc_)
        dxn, dg, dsh, dsc = vjp(du_)
        return (dx * mask(i) + dxn, dg, dsh, dsc)
    dx_out = ("tok", rows, D, F32) if keep_tiles is None else ("tok_head", keep_tiles * TM, D, F32, keep_tiles - 1)
    dxin, dg, dsh, dsc = rm(tag + "_bnorm", b3, n_tiles,
                            [tok(xin), const(g), mod(sh), mod(sc), tok(du), tok(dxout, clamp=dx_clamp)],
                            [dx_out, ("acc", 1, D), ("modacc", D), ("modacc", D)])
    return dxin, dict(g=dg, sh=dsh, sc=dsc, gate=dgate)


def _perm_in_cols(w):
    pad = jnp.zeros(w.shape[:-1] + (D_INP - D_IN,), w.dtype)
    return jnp.concatenate([w[..., :4096], w[..., 5152:7200], w[..., 4096:5120], w[..., 5120:5152], pad], axis=-1)


def _unperm_in_cols(w):
    return jnp.concatenate([w[..., :4096], w[..., 6144:LR_COL], w[..., LR_COL:LR_COL + 32], w[..., 4096:6144]], axis=-1)


def _local_step(x, c, ctx, target, wts, comm=None):
    bl, t, _ = x.shape
    tc = ctx.shape[1]
    nx_rows, nc_rows = bl * t, bl * tc
    nt_rows = nx_rows + nc_rows
    tpb = t // TM
    nxt, ntt = nx_rows // TM, nt_rows // TM
    nb = bl
    rm = functools.partial(_rowmap, tpb=tpb, nb=nb)
    last_x = nxt - 1

    x0 = jnp.concatenate([x.reshape(nx_rows, D), ctx.reshape(nc_rows, D)], axis=0)
    tgt = target.reshape(nx_rows, D)

    cc = jnp.concatenate([c, wts["c_ctx"].reshape(1, D), jnp.zeros((8 - bl - 1, D), F32)], axis=0)
    modv, cc_all = _mod_forward(cc, wts["w_mod_shard"], wts["b_mod_shard"])
    mods = [modv[:nb + 1, k * D:(k + 1) * D].reshape(nb + 1, 1, D) for k in range(9)]

    x1, sv1 = _ffn_fwd("ffn1", x0, ntt, wts["g_ffn1"], mods[0], mods[1], mods[2], wts["w1_gu"], wts["w1_down"],
                       tpb, nb, comm)
    u2, u2t = rm("in_norm", lambda i, x_, g_, sh_, sc_: _twice(_modnorm(x_, g_, sh_, sc_)), ntt,
                 [tok(x1), const(wts["g_mix"]), mod(mods[3]), mod(mods[4])],
                 [("tok", nt_rows, D, BF16), ("tokT", nt_rows, D, BF16)])
    w_inp = wts["w_in_p"]
    p = _with_side(comm, "in_proj", None, lambda s: _mm("in_proj", u2, w_inp, out_dtype=BF16, side=s))

    waf, wab, baf, bab = wts["w_alpha_f_pad"], wts["w_alpha_b_pad"], wts["b_alpha_f"], wts["b_alpha_b"]

    def dec_fwd(i, lr, wf, wb, bf_, bb_):
        zf = _dot(lr, wf, 1, 0) + bf_
        zb = _dot(lr, wb, 1, 0) + bb_
        return (jnp.concatenate([jax.nn.log_sigmoid(zf) / TAU, jax.nn.log_sigmoid(zb) / TAU], axis=1),)
    (gfb,) = rm("decay_fwd", dec_fwd, ntt, [tok(p, 128, LR_COL // 128), const(waf), const(wab), const(baf), const(bab)],
                [("tok", nt_rows, 1024, F32)])
    g2 = gfb
    tri = jnp.tril(jnp.ones((CHUNK, CHUNK), F32))
    mmats = jnp.stack([tri, tri.T])
    *o2, ssave = _gla_fwd(p, g2, mmats, bl, t, tc)

    gn_g = wts["gla_norm_g"]

    def gla_out(of, ob, og, gn):
        o = of + ob
        parts = []
        for h in range(HEADS):
            oh = o[:, h * DV:(h + 1) * DV]
            parts.append(oh * lax.rsqrt(jnp.mean(oh * oh, axis=-1, keepdims=True) + EPS))
        return jnp.concatenate(parts, axis=1) * gn * _silu(og)
    yg_in, yg_int = rm("gla_out", lambda i, of, ob, og, gn: _twice(gla_out(of, ob, og, gn)), nxt,
                       [tok(o2[0]), tok(o2[1]), tok(p, 1024, OG_CB), const(gn_g)],
                       [("tok", nx_rows, D, BF16), ("tokT", nx_rows, D, BF16)])
    y_gla = _with_side(comm, "gla_proj", None,
                       lambda s: _mm("gla_proj", yg_in, wts["w_gla_out"], out_dtype=BF16, side=s))

    (z,) = rm("glu", lambda i, a, b: (a * jax.nn.sigmoid(b),), nxt, [tok(p, 1024, 0), tok(p, 1024, 1)],
              [("tok", nx_rows, D, F32)])
    dw_w = jnp.concatenate([wts["dw_weight"], jnp.zeros((1, D), F32)], axis=0)
    zc = _dwconv("dwconv_fwd", z, dw_w, wts["dw_bias"], bl, t, False)

    def ln_silu(zc_, g_, b_):
        mu = jnp.mean(zc_, axis=-1, keepdims=True)
        var = jnp.mean(jnp.square(zc_ - mu), axis=-1, keepdims=True)
        return _silu((zc_ - mu) * lax.rsqrt(var + EPS) * g_ + b_)
    ln_g, ln_b = wts["conv_ln_g"], wts["conv_ln_b"]
    zl, zlt = rm("conv_ln", lambda i, zc_, g_, b_: _twice(ln_silu(zc_, g_, b_)), nxt,
                 [tok(zc), const(ln_g), const(ln_b)], [("tok", nx_rows, D, BF16), ("tokT", nx_rows, D, BF16)])
    y_conv = _mm("conv_proj", zl, wts["w_conv_out"], out_dtype=BF16)

    mg, mgt = rm("merge", lambda i, ga, gb, yc, yg: _twice(jax.nn.sigmoid(ga) * yc + jax.nn.sigmoid(gb) * yg), nxt,
                 [tok(p, 1024, GA_CB), tok(p, 1024, GB_CB), tok(y_conv), tok(y_gla)],
                 [("tok", nx_rows, D, BF16), ("tokT", nx_rows, D, BF16)])
    mix, x2 = _mm("out_proj", mg, wts["w_out"], out_dtype=BF16,
                  residual=dict(x=x1, gate=mods[5], scale=1.0, tpb=tpb))

    x3, sv2 = _ffn_fwd("ffn2", x2, nxt, wts["g_ffn2"], mods[6], mods[7], mods[8], wts["w2_gu"], wts["w2_down"],
                       tpb, nb)
    g_fin = wts["g_final"].reshape(1, D)

    def head(i, x_, g_, tg):
        y, vjp = jax.vjp(_rms, x_, g_)
        diff = y - tg
        dx, dg = vjp(diff * (1.0 / D))
        loss = 0.5 * jnp.sum(jnp.mean(diff * diff, axis=-1, keepdims=True))
        return dx, dg, loss
    dx3, dg_final, loss_acc = rm("loss_head", head, nxt, [tok(x3), const(g_fin), tok(tgt)],
                                 [("tok", nx_rows, D, F32), ("acc", 1, D), ("acc", 8, 128)])
    loss = loss_acc[0, 0]

    grads = {}
    dx2, gf2 = _ffn_bwd("ffn2", x2, sv2, dx3, None, nxt, wts["g_ffn2"], mods[6], mods[7], mods[8],
                        wts["w2_gu"], wts["w2_down"], tpb, nb, comm, grads, ("w2_gu", "w2_down"))
    grads["g_ffn2"] = gf2["g"]

    dmix, dgate5 = rm("mix_bres", lambda i, dx, mx_, gt: (gt * dx, jnp.sum(mx_ * dx, axis=0, keepdims=True)), nxt,
                      [tok(dx2), tok(mix), mod(mods[5])], [("tok", nx_rows, D, BF16), ("modacc", D)])
    dmg = _mm("out_bproj", dmix, wts["w_out"], trans_b=True, out_dtype=BF16)
    grads["w_out"] = _wgrad("out_wgrad", mgt, dmix)

    def merge_bwd(i, dm, ga, gb, yc, yg):
        keep = (i <= last_x).astype(F32)
        dm = dm * keep
        sa, sb = jax.nn.sigmoid(ga), jax.nn.sigmoid(gb)
        return dm * sa, dm * sb, jnp.concatenate([dm * yc * sa * (1 - sa), dm * yg * sb * (1 - sb)], axis=1)
    cl = dict(clamp=last_x)
    dyc, dyg, dp = rm("merge_bwd", merge_bwd, ntt,
                      [tok(dmg, **cl), tok(p, 1024, GA_CB, last_x), tok(p, 1024, GB_CB, last_x), tok(y_conv, **cl),
                       tok(y_gla, **cl)],
                      [("tok", nt_rows, D, BF16), ("tok", nt_rows, D, BF16),
                       ("cols", jax.ShapeDtypeStruct((nt_rows, D_INP), BF16), 2048, 2)])

    dzl = _mm("conv_bproj", dyc, wts["w_conv_out"], trans_b=True, rows=nx_rows, out_dtype=BF16)
    grads["w_conv_out"] = _wgrad("conv_wgrad", zlt, dyc, rows=nx_rows)

    def ln_bwd(i, zc_, g_, b_, dz_):
        _, vjp = jax.vjp(ln_silu, zc_, g_, b_)
        return vjp(dz_)
    dzc, dln_g, dln_b = rm("conv_ln_bwd", ln_bwd, nxt, [tok(zc), const(ln_g), const(ln_b), tok(dzl)],
                           [("tok", nx_rows, D, F32), ("acc", 1, D), ("acc", 1, D)])
    dz = _dwconv("dwconv_bwd", dzc, dw_w, jnp.zeros((1, D), F32), bl, t, True)
    ddw, ddb = _dwconv_wgrad(z, dzc, bl, t)
    grads.update(conv_ln_g=dln_g, conv_ln_b=dln_b, dw_weight=ddw[:CONV_W], dw_bias=ddb)

    def glu_bwd(i, dz_, a, b):
        keep = (i <= last_x).astype(F32)
        dz_ = dz_ * keep
        s = jax.nn.sigmoid(b)
        return (jnp.concatenate([dz_ * s, dz_ * a * s * (1 - s)], axis=1),)
    (dp,) = rm("glu_bwd", glu_bwd, ntt, [tok(dz, **cl), tok(p, 1024, 0, last_x), tok(p, 1024, 1, last_x)],
               [("cols", dp, 2048, 0)])

    dyg_in = _mm("gla_bproj", dyg, wts["w_gla_out"], trans_b=True, rows=nx_rows, out_dtype=BF16)
    grads["w_gla_out"] = _wgrad("gla_wgrad", yg_int, dyg, rows=nx_rows)

    def gla_out_bwd(i, of, ob, og, gn, dy):
        _, vjp = jax.vjp(gla_out, of, ob, og, gn)
        do_, _, dog_, dgn_ = vjp(dy)
        return do_, dog_, dgn_
    do, dp, dgn = rm("gla_out_bwd", gla_out_bwd, nxt,
                     [tok(o2[0]), tok(o2[1]), tok(p, 1024, OG_CB), const(gn_g), tok(dyg_in)],
                     [("tok", nx_rows, D, BF16), ("cols", dp, 1024, OG_CB), ("acc", 1, D)])
    grads["gla_norm_g"] = dgn
    (dp,) = rm("og_ctx_zero", lambda i: (jnp.zeros((TM, D), F32),), ntt - nxt, [], [("cols", dp, 1024, OG_CB, nxt)])

    dq2, dk2, dv2, dg2 = zip(*_gla_bwd(p, g2, mmats, ssave, do, bl, t, tc))

    def dec_bwd(i, lr, wf, wb, bf_, bb_, dgf, dgb_):
        zf = _dot(lr, wf, 1, 0) + bf_
        zb = _dot(lr, wb, 1, 0) + bb_
        dzf = dgf * (1 - jax.nn.sigmoid(zf)) * (1.0 / TAU)
        dzb = dgb_ * (1 - jax.nn.sigmoid(zb)) * (1.0 / TAU)
        dlr = _dot(dzf, wf, 1, 1) + _dot(dzb, wb, 1, 1)
        return (dlr, _dot(lr, dzf, 0, 0), _dot(lr, dzb, 0, 0), jnp.sum(dzf, axis=0, keepdims=True),
                jnp.sum(dzb, axis=0, keepdims=True))
    dp, dwaf, dwab, dbaf, dbab = rm(
        "decay_bwd", dec_bwd, ntt,
        [tok(p, 128, LR_COL // 128), const(waf), const(wab), const(baf), const(bab), tok(dg2[0]), tok(dg2[1])],
        [("cols", dp, 128, LR_COL // 128), ("acc", 128, 512), ("acc", 128, 512), ("acc", 1, 512), ("acc", 1, 512)])
    grads.update(w_alpha_f=dwaf[:LOWRANK], w_alpha_b=dwab[LOWRANK:2 * LOWRANK], b_alpha_f=dbaf, b_alpha_b=dbab)

    (dp,) = rm("gla_sum",
               lambda i, q0, q1, k0, k1, v0, v1: (jnp.concatenate([q0 + q1, k0 + k1, v0 + v1], axis=1),), ntt,
               [tok(dq2[0]), tok(dq2[1]), tok(dk2[0]), tok(dk2[1]), tok(dv2[0]), tok(dv2[1])],
               [("cols", dp, 2048, 1)])
    du2 = _with_side(comm, "in_bproj", grads, lambda s: _mm("in_bproj", dp, w_inp, trans_b=True, side=s))
    grads["w_in_p"] = _wgrad("in_wgrad", u2t, dp)

    def in_norm_bwd(i, x_, g_, sh_, sc_, du_, dx):
        keep = (i <= last_x).astype(F32)
        _, vjp = jax.vjp(_modnorm, x_, g_, sh_, sc_)
        dxn, dg, dsh, dsc = vjp(du_)
        return (dx * keep + dxn, dg, dsh, dsc)
    dx1, dg_mix, dsh3, dsc4 = rm("in_norm_bwd", in_norm_bwd, ntt,
                                 [tok(x1), const(wts["g_mix"]), mod(mods[3]), mod(mods[4]), tok(du2), tok(dx2, **cl)],
                                 [("tok", nt_rows, D, F32), ("acc", 1, D), ("modacc", D), ("modacc", D)])
    grads["g_mix"] = dg_mix

    dx0, gf1 = _ffn_bwd("ffn1", x0, sv1, dx1, None, ntt, wts["g_ffn1"], mods[0], mods[1], mods[2],
                        wts["w1_gu"], wts["w1_down"], tpb, nb, comm, grads, ("w1_gu", "w1_down"), keep_tiles=nxt)
    grads["g_ffn1"] = gf1["g"]
    grad_x = dx0.reshape(bl, t, D)

    dmods = [gf1["sh"], gf1["sc"], gf1["gate"], dsh3, dsc4, dgate5, gf2["sh"], gf2["sc"], gf2["gate"]]
    dmod = jnp.concatenate(
        [jnp.concatenate([a.reshape(a.shape[0], D), jnp.zeros((8 - a.shape[0], D), F32)], axis=0) for a in dmods],
        axis=1)
    grads["w_mod"], grads["c_ctx"], grads["b_mod"] = _mod_backward(dmod, cc_all, wts["w_mod_shard"], nb)
    grads["g_final"] = dg_final.reshape(D)
    return loss, grad_x, grads


ANY = pl.BlockSpec(memory_space=pl.ANY)


def _place():
    x, y, c = lax.axis_index("x"), lax.axis_index("y"), lax.axis_index("c")
    chips = [(1 - x, y), (x, 1 - y), (1 - x, 1 - y)]
    return x, y, c, chips


def _remote(send_sems, recv_sems):
    def copy(k, src, dst, to):
        return pltpu.make_async_remote_copy(src_ref=src, dst_ref=dst, send_sem=send_sems.at[k],
                                            recv_sem=recv_sems.at[k], device_id=to, device_id_type=MESH)
    return copy


def _sems(n):
    return [pltpu.SemaphoreType.DMA((n,)), pltpu.SemaphoreType.DMA((n,))]


def _gather_weights(shards):
    n = len(shards)

    def body(*refs):
        ins, outs = refs[:n], refs[n:2 * n]
        copy = _remote(refs[2 * n], refs[2 * n + 1])
        x, y, c, chips = _place()
        me = 2 * x + y
        sibling = (x, y, 1 - c)
        started = []

        def rows(i, hc):
            hr = ins[i].shape[0] // 2
            return pl.ds(hc * hr, hr)

        for i in range(n):
            started.append(copy(7 * i + 6, ins[i], outs[i].at[me], sibling))
            started[-1].start()
            for j, (px, py) in enumerate(chips):
                started.append(copy(7 * i + j, ins[i].at[rows(i, c), :], outs[i].at[me, rows(i, c), :], (px, py, c)))
                started[-1].start()
        for i in range(n):
            for j, (px, py) in enumerate(chips):
                half = outs[i].at[2 * px + py, rows(i, c), :]
                copy(7 * i + j, half, half, (px, py, c)).wait_recv()
                started.append(copy(7 * i + 3 + j, half, half, sibling))
                started[-1].start()
        for i in range(n):
            copy(7 * i + 6, ins[i], outs[i].at[me], sibling).wait_recv()
            for j, (px, py) in enumerate(chips):
                other = outs[i].at[2 * px + py, rows(i, 1 - c), :]
                copy(7 * i + 3 + j, other, other, sibling).wait_recv()
        for cp in started:
            cp.wait_send()

    return pl.pallas_call(
        body, out_shape=[jax.ShapeDtypeStruct((4,) + s.shape, s.dtype) for s in shards], in_specs=[ANY] * n,
        out_specs=[ANY] * n, scratch_shapes=_sems(7 * n), name="gather_weights")(*shards)


def _swap_halves(name, gs):
    n = len(gs)

    def body(*refs):
        ins, outs = refs[:n], refs[n:2 * n]
        copy = _remote(refs[2 * n], refs[2 * n + 1])
        x, y, c, _ = _place()
        cps = []
        for i in range(n):
            hr = ins[i].shape[1] // 2
            cps.append(copy(i, ins[i].at[:, pl.ds((1 - c) * hr, hr), :], outs[i], (x, y, 1 - c)))
            cps[-1].start()
        for cp in cps:
            cp.wait()

    return pl.pallas_call(
        body, out_shape=[jax.ShapeDtypeStruct((4, g.shape[1] // 2, g.shape[2]), g.dtype) for g in gs],
        in_specs=[ANY] * n, out_specs=[ANY] * n, scratch_shapes=_sems(n), name=name)(*gs)


def _row_tile(hr):
    return hr if hr <= 256 else _pick(hr, (256, 176, 128, 64, 32, 16))


def _add_halves(name, g, r, place):
    hr = r.shape[1]
    tr = _row_tile(hr)
    nblk = hr // tr

    def kern(p_ref, g_ref, r_ref, o_ref):
        o_ref[...] = (g_ref[...].astype(F32) + r_ref[...].astype(F32)).astype(o_ref.dtype)

    blk = (1, tr, g.shape[2])
    return pl.pallas_call(
        kern,
        grid_spec=pltpu.PrefetchScalarGridSpec(
            num_scalar_prefetch=1, grid=(4, nblk),
            in_specs=[pl.BlockSpec(blk, lambda j, i, p: (j, p[0] * nblk + i, 0)),
                      pl.BlockSpec(blk, lambda j, i, p: (j, i, 0))],
            out_specs=pl.BlockSpec(blk, lambda j, i, p: (j, i, 0))),
        out_shape=jax.ShapeDtypeStruct(r.shape, r.dtype),
        compiler_params=_cparams(("parallel", "parallel")), name=name)(place, g, r)


def _scatter_chips(cs):
    n = len(cs)

    def body(*refs):
        ins, outs = refs[:n], refs[n:2 * n]
        copy = _remote(refs[2 * n], refs[2 * n + 1])
        x, y, c, chips = _place()
        me = 2 * x + y
        sends = []
        for i in range(n):
            for j, (px, py) in enumerate(chips):
                sends.append(copy(3 * i + j, ins[i].at[2 * px + py], outs[i].at[me], (px, py, c)))
                sends[-1].start()
        for i in range(n):
            for j, (px, py) in enumerate(chips):
                src = 2 * px + py
                copy(3 * i + j, ins[i].at[src], outs[i].at[src], (px, py, c)).wait_recv()
        for cp in sends:
            cp.wait_send()

    return pl.pallas_call(
        body, out_shape=[jax.ShapeDtypeStruct(a.shape, a.dtype) for a in cs], in_specs=[ANY] * n,
        out_specs=[ANY] * n, scratch_shapes=_sems(3 * n), name="grad_scatter_chips")(*cs)


def _sum_chips(name, cs, r, place):
    hr = r.shape[1]
    tr = _row_tile(hr)
    nblk = hr // tr

    def kern(p_ref, c_ref, r0, r1, r2, r3, o_ref):
        me = p_ref[1]
        acc = None
        for k, rk in enumerate((r0, r1, r2, r3)):
            val = jnp.where(me == k, c_ref[0].astype(F32), rk[0].astype(F32))
            acc = val if acc is None else acc + val
        o_ref[...] = acc

    blk = (1, tr, r.shape[2])

    def slot(k):
        return lambda i, p: (jnp.where(p[1] == k, (k + 1) % 4, k), i, 0)

    return pl.pallas_call(
        kern,
        grid_spec=pltpu.PrefetchScalarGridSpec(
            num_scalar_prefetch=1, grid=(nblk,),
            in_specs=[pl.BlockSpec(blk, lambda i, p: (p[1], i, 0))] + [pl.BlockSpec(blk, slot(k)) for k in range(4)],
            out_specs=pl.BlockSpec((tr, r.shape[2]), lambda i, p: (p[0] * nblk + i, 0))),
        out_shape=jax.ShapeDtypeStruct((2 * hr, r.shape[2]), F32),
        compiler_params=_cparams(("parallel",)), name=name)(place, cs, r, r, r, r)


def _join_halves(fs):
    n = len(fs)

    def body(*refs):
        ins, outs = refs[:n], refs[n:2 * n]
        copy = _remote(refs[2 * n], refs[2 * n + 1])
        x, y, c, _ = _place()
        cps = []
        for i in range(n):
            hr = ins[i].shape[0] // 2
            cps.append(copy(i, ins[i].at[pl.ds(c * hr, hr), :], outs[i].at[pl.ds(c * hr, hr), :], (x, y, 1 - c)))
            cps[-1].start()
        for i in range(n):
            hr = ins[i].shape[0] // 2
            other = outs[i].at[pl.ds((1 - c) * hr, hr), :]
            copy(i, other, other, (x, y, 1 - c)).wait_recv()
        for cp in cps:
            cp.wait_send()

    return pl.pallas_call(
        body, out_shape=[jax.ShapeDtypeStruct(f.shape, f.dtype) for f in fs], in_specs=[ANY] * n,
        out_specs=[ANY] * n, input_output_aliases={i: i for i in range(n)}, scratch_shapes=_sems(n),
        name="grad_join_halves")(*fs)


def _half_rows(n, hc):
    return pl.ds(hc * (n // 2), n // 2)


def _gather_ici_side(shards):
    n = len(shards)

    def copies(ins, outs, copy):
        x, y, c, chips = _place()
        me = 2 * x + y
        for i in range(n):
            rows = _half_rows(ins[i].shape[0], c)
            for j, (px, py) in enumerate(chips):
                yield (copy(3 * i + j, ins[i].at[rows, :], outs[i].at[me, rows, :], (px, py, c)),
                       outs[i].at[2 * px + py, rows, :])

    def start(ins, outs, copy):
        for cp, _ in copies(ins, outs, copy):
            cp.start()

    def finish(ins, outs, copy):
        x, y, c, chips = _place()
        k = 0
        for cp, landing in copies(ins, outs, copy):
            copy(k, landing, landing, (x, y, c)).wait_recv()
            k += 1
        for cp, _ in copies(ins, outs, copy):
            cp.wait_send()

    return dict(ins=list(shards), outs=[jax.ShapeDtypeStruct((4,) + s.shape, s.dtype) for s in shards], nsem=3 * n,
                start=start, finish=finish)


def _gather_d2d_side(shards, bufs):
    n = len(shards)

    def copies(ins, outs, copy):
        x, y, c, chips = _place()
        me = 2 * x + y
        sibling = (x, y, 1 - c)
        for i in range(n):
            a = ins[i].shape[0]
            yield copy(4 * i + 3, ins[i], outs[i].at[me], sibling), outs[i].at[me]
            for j, (px, py) in enumerate(chips):
                src = 2 * px + py
                mine = outs[i].at[src, _half_rows(a, c), :]
                yield copy(4 * i + j, mine, mine, sibling), outs[i].at[src, _half_rows(a, 1 - c), :]

    def start(ins, outs, copy):
        for cp, _ in copies(ins, outs, copy):
            cp.start()

    def finish(ins, outs, copy):
        x, y, c, _ = _place()
        for i in range(n):
            for k, (cp, landing) in enumerate(list(copies(ins, outs, copy))[4 * i:4 * i + 4]):
                sem = 4 * i + 3 if k == 0 else 4 * i + k - 1
                copy(sem, landing, landing, (x, y, 1 - c)).wait_recv()
        for cp, _ in copies(ins, outs, copy):
            cp.wait_send()

    return dict(ins=list(shards) + list(bufs), outs=[jax.ShapeDtypeStruct(b.shape, b.dtype) for b in bufs],
                nsem=4 * n, alias={n + i: i for i in range(n)}, start=start, finish=finish)


def _scatter_side(cs):
    n = len(cs)

    def copies(ins, outs, copy):
        x, y, c, chips = _place()
        me = 2 * x + y
        for i in range(n):
            for j, (px, py) in enumerate(chips):
                yield copy(3 * i + j, ins[i].at[2 * px + py], outs[i].at[me], (px, py, c)), outs[i].at[2 * px + py]

    def start(ins, outs, copy):
        for cp, _ in copies(ins, outs, copy):
            cp.start()

    def finish(ins, outs, copy):
        x, y, c, _ = _place()
        for k, (cp, landing) in enumerate(copies(ins, outs, copy)):
            copy(k, landing, landing, (x, y, c)).wait_recv()
        for cp, _ in copies(ins, outs, copy):
            cp.wait_send()

    return dict(ins=list(cs), outs=[jax.ShapeDtypeStruct(a.shape, a.dtype) for a in cs], nsem=3 * n,
                start=start, finish=finish)


def _mod_forward(cc, w_shard, b_shard):
    cc_all = _allreduce_small(cc, "cond_gather", reduce=False).reshape(64, D)
    part = _mm("mod_fwd", cc_all, w_shard, a_fn=_silu, bias=b_shard)
    got = _mod_rows_exchange(part)
    return jnp.concatenate([got[j] for j in range(4)], axis=1), cc_all


def _mod_backward(dmod, cc_all, w_shard, ctx_row):
    w = w_shard.shape[1]
    blocks = _dmod_exchange(jnp.transpose(dmod.reshape(8, 4, w), (1, 0, 2))).reshape(64, w)
    dsc = _mm("mod_bproj", blocks, w_shard, trans_b=True)

    def tail(cc_ref, dsc_ref, dm_ref, s_ref, dctx_ref, db_ref):
        cc_ = cc_ref[...]
        s = jax.nn.sigmoid(cc_)
        s_ref[...] = (cc_ * s).astype(BF16)
        is_ctx = ((lax.broadcasted_iota(jnp.int32, (64, 1), 0) & 7) == ctx_row).astype(F32)
        dctx_ref[...] = 0.5 * jnp.sum(dsc_ref[...] * (s * (1 + cc_ * (1 - s))) * is_ctx, axis=0, keepdims=True)
        db_ref[...] = jnp.sum(dm_ref[...], axis=0, keepdims=True)

    s_all, dctx, db_mod = pl.pallas_call(
        tail, out_shape=[jax.ShapeDtypeStruct((64, D), BF16), jax.ShapeDtypeStruct((1, D), F32),
                         jax.ShapeDtypeStruct((1, dmod.shape[1]), F32)], name="mod_tail")(cc_all, dsc, dmod)
    return _mm_tn("mod_wgrad", s_all, blocks), dctx[0], db_mod


def _small_pack(dw, af, ab):
    return jnp.concatenate([dw, jnp.zeros((1, dw.shape[1]), F32), jnp.concatenate([af, ab], axis=1)], axis=0)


def _grad_pieces(n, grads):
    if n == "small":
        return jnp.stack([_small_pack(grads["dw_weight"][:, 256 * j:256 * (j + 1)],
                                      grads["w_alpha_f"][:, DK * j:DK * (j + 1)],
                                      grads["w_alpha_b"][:, DK * j:DK * (j + 1)]) for j in range(4)])
    if n == "w_in":
        g = _unperm_in_cols(grads["w_in_p"])
        return jnp.transpose(g.reshape(D, 4, D_IN // 4), (1, 0, 2))
    g = grads[n]
    return g if g.ndim == 3 else g.reshape(4, g.shape[0] // 4, g.shape[1])


def _chip_sums(tag, names, grads, place):
    gs = [_grad_pieces(n, grads) for n in names]
    swapped = _swap_halves("grad_swap_" + tag, gs)
    return [_add_halves("grad_add_" + n, g, r, place) for n, g, r in zip(names, gs, swapped)]


class _Overlap:
    SCATTER = {"in_bproj": ("w2_down", "w2_gu", "w_out"),
               "ffn1_wgu": ("w_conv_out", "w_gla_out", "w_in", "small"),
               "ffn1_bgu": ("w1_down", "w1_gu")}
    GATHER = {("ffn1_gu", "ffn1_down"): ("w_in", "w_conv_out", "w_gla_out", "w_out"),
              ("in_proj", "gla_proj"): ("w2_gu", "w2_down")}
    LATE = tuple(n for names in GATHER.values() for n in names)

    def __init__(self, shard_of, install, place):
        self.shard_of, self.install, self.place = shard_of, install, place
        self.bufs, self.pending, self.landed = {}, None, {}

    def side(self, tag, grads):
        for (ici, d2d), names in self.GATHER.items():
            shards = [self.shard_of(n) for n in names]
            if tag == ici:
                return _gather_ici_side(shards)
            if tag == d2d:
                return _gather_d2d_side(shards, self.bufs[ici])
        if tag in self.SCATTER:
            names = self.SCATTER[tag]
            self.pending = (names, _chip_sums(tag, names, grads, self.place))
            return _scatter_side(self.pending[1])
        return None

    def done(self, tag, outs):
        for (ici, d2d), names in self.GATHER.items():
            if tag == ici:
                self.bufs[ici] = outs
                return
            if tag == d2d:
                self.install(dict(zip(names, outs)))
                return
        for n, cs, r in zip(*self.pending, outs):
            self.landed[n] = (cs, r)


def _with_side(comm, tag, grads, call):
    side = comm.side(tag, grads) if comm is not None else None
    if side is None:
        return call(None)
    res, outs = call(side)
    comm.done(tag, outs)
    return res


def _allreduce_small(v, name="allreduce_small", reduce=True):
    def body(x_ref, out_ref, *scratch):
        gath = out_ref if not reduce else scratch[0]
        send_sems, recv_sems, local_sem = scratch[-3:]
        x, y, c, chips = _place()
        me, sibling = (x, y, c), (x, y, 1 - c)

        def slot(px, py, pc):
            return gath.at[4 * px + 2 * py + pc]

        def copy(k, block, to, src=None):
            return pltpu.make_async_remote_copy(
                src_ref=slot(*block) if src is None else src, dst_ref=slot(*block), send_sem=send_sems.at[k],
                recv_sem=recv_sems.at[k], device_id=to, device_id_type=MESH)

        mine = pltpu.make_async_copy(x_ref, slot(*me), local_sem)
        mine.start()
        first = [copy(0, me, sibling, src=x_ref)]
        first += [copy(1 + j, me, (*chip, c), src=x_ref) for j, chip in enumerate(chips)]
        for cp in first:
            cp.start()
        passed = [copy(4 + j, (*chip, c), sibling) for j, chip in enumerate(chips)]
        for j, chip in enumerate(chips):
            copy(1 + j, (*chip, c), me).wait_recv()
            passed[j].start()
        copy(0, sibling, me).wait_recv()
        for j, chip in enumerate(chips):
            copy(4 + j, (*chip, 1 - c), me).wait_recv()
        for cp in first + passed:
            cp.wait_send()
        mine.wait()
        if reduce:
            acc = gath[0]
            for k in range(1, 8):
                acc = acc + gath[k]
            out_ref[...] = acc

    vm = pl.BlockSpec(memory_space=pltpu.VMEM)
    sems = [pltpu.SemaphoreType.DMA((7,)), pltpu.SemaphoreType.DMA((7,)), pltpu.SemaphoreType.DMA(())]
    return pl.pallas_call(
        body, out_shape=jax.ShapeDtypeStruct(v.shape if reduce else (8,) + v.shape, F32), in_specs=[vm], out_specs=vm,
        scratch_shapes=([pltpu.VMEM((8,) + v.shape, F32)] if reduce else []) + sems, name=name)(v)


def _mod_rows_exchange(part):
    w = part.shape[1]

    def body(p_ref, out_ref, send_sems, recv_sems):
        x, y, c, chips = _place()
        me = 2 * x + y
        copy = _remote(send_sems, recv_sems)
        sends = []
        for j, (px, py) in enumerate(chips):
            rows = pl.ds(pl.multiple_of(8 * (4 * px + 2 * py + c), 8), 8)
            sends.append(copy(j, p_ref.at[rows, :], out_ref.at[me], (px, py, c)))
            sends[-1].start()
        out_ref[me] = p_ref[pl.ds(pl.multiple_of(8 * (4 * x + 2 * y + c), 8), 8), :]
        for j, (px, py) in enumerate(chips):
            landing = out_ref.at[2 * px + py]
            copy(j, landing, landing, (px, py, c)).wait_recv()
        for cp in sends:
            cp.wait_send()

    vm = pl.BlockSpec(memory_space=pltpu.VMEM)
    return pl.pallas_call(body, out_shape=jax.ShapeDtypeStruct((4, 8, w), F32), in_specs=[vm], out_specs=vm,
                          scratch_shapes=_sems(3), name="mod_rows_exchange")(part)


def _dmod_exchange(dm):
    w = dm.shape[2]

    def body(d_ref, out_ref, send_sems, recv_sems):
        x, y, c, _ = _place()
        copy = _remote(send_sems, recv_sems)
        mine = 4 * x + 2 * y + c
        sends = []
        for r in range(1, 8):
            tx, ty, tc = x ^ (r >> 2), y ^ ((r >> 1) & 1), c ^ (r & 1)
            sends.append(copy(r - 1, d_ref.at[2 * tx + ty], out_ref.at[mine], (tx, ty, tc)))
            sends[-1].start()
        out_ref[mine] = d_ref[2 * x + y]
        for r in range(1, 8):
            tx, ty, tc = x ^ (r >> 2), y ^ ((r >> 1) & 1), c ^ (r & 1)
            landing = out_ref.at[4 * tx + 2 * ty + tc]
            copy(r - 1, landing, landing, (tx, ty, tc)).wait_recv()
        for cp in sends:
            cp.wait_send()

    vm = pl.BlockSpec(memory_space=pltpu.VMEM)
    return pl.pallas_call(body, out_shape=jax.ShapeDtypeStruct((8, 8, w), F32), in_specs=[vm], out_specs=vm,
                          scratch_shapes=_sems(7), name="dmod_exchange")(dm)


def _adamw(name, w, g, m, v):
    r, cols = w.shape
    budget = 262144
    tr = r if r * cols <= budget else next(c for c in (256, 128, 64, 32, 16, 8) if r % c == 0 and c * cols <= budget)

    def kern(w_ref, g_ref, m_ref, v_ref, d_ref, nm_ref, nv_ref):
        gv = g_ref[...]
        nm = ADAM_B1 * m_ref[...] + (1.0 - ADAM_B1) * gv
        nv = ADAM_B2 * v_ref[...] + (1.0 - ADAM_B2) * jnp.square(gv)
        m_hat = nm / (1.0 - ADAM_B1 ** ADAM_STEP)
        v_hat = nv / (1.0 - ADAM_B2 ** ADAM_STEP)
        d_ref[...] = -ADAM_LR * (m_hat / (jnp.sqrt(v_hat) + ADAM_EPS) + ADAM_WD * w_ref[...])
        nm_ref[...] = nm
        nv_ref[...] = nv

    spec = pl.BlockSpec((tr, cols), lambda i: (i, 0))
    shp = jax.ShapeDtypeStruct((r, cols), F32)
    return pl.pallas_call(kern, grid=(r // tr,), in_specs=[spec] * 4, out_specs=[spec] * 3, out_shape=[shp] * 3,
                          compiler_params=_cparams(("parallel",)), name=name)(w, g, m, v)


SHARDED = (("w_mod", 1), ("w1_gu", 1), ("w1_down", 0), ("w_in", 1), ("dw_weight", 1), ("w_conv_out", 0),
           ("w_alpha_f", 1), ("w_alpha_b", 1), ("w_gla_out", 0), ("w_out", 0), ("w2_gu", 1), ("w2_down", 0))
REPLICATED = ("c_ctx", "b_mod", "g_ffn1", "g_mix", "dw_bias", "conv_ln_g", "conv_ln_b", "b_alpha_f", "b_alpha_b",
              "gla_norm_g", "g_ffn2", "g_final")
WEIGHTS = ("c_ctx", "w_mod", "b_mod", "g_ffn1", "w1_gu", "w1_down", "g_mix", "w_in", "dw_weight", "dw_bias",
           "conv_ln_g", "conv_ln_b", "w_conv_out", "w_alpha_f", "b_alpha_f", "w_alpha_b", "b_alpha_b", "gla_norm_g",
           "w_gla_out", "w_out", "g_ffn2", "w2_gu", "w2_down", "g_final")
MATRICES = ("w1_gu", "w1_down", "w_in", "w_conv_out", "w_gla_out", "w_out", "w2_gu", "w2_down")


def _pack_flat(parts, align):
    flat = jnp.concatenate([p.reshape(-1) for p in parts])
    pad = (-flat.shape[0]) % align
    return jnp.concatenate([flat, jnp.zeros((pad,), flat.dtype)]).reshape(-1, 1024)


def _unpack_flat(flat2d, shapes):
    flat = flat2d.reshape(-1)
    out, off = [], 0
    for s in shapes:
        n = math.prod(s)
        out.append(flat[off:off + n].reshape(s))
        off += n
    return out


def kernel(x, c, ctx, c_ctx, w_mod, b_mod, g_ffn1, w1_gu, w1_down, g_mix, w_in, dw_weight, dw_bias, conv_ln_g, conv_ln_b, w_conv_out, w_alpha_f, b_alpha_f, w_alpha_b, b_alpha_b, gla_norm_g, w_gla_out, w_out, g_ffn2, w2_gu, w2_down, g_final, loss_target, m_c_ctx, m_w_mod, m_b_mod, m_g_ffn1, m_w1_gu, m_w1_down, m_g_mix, m_w_in, m_dw_weight, m_dw_bias, m_conv_ln_g, m_conv_ln_b, m_w_conv_out, m_w_alpha_f, m_b_alpha_f, m_w_alpha_b, m_b_alpha_b, m_gla_norm_g, m_w_gla_out, m_w_out, m_g_ffn2, m_w2_gu, m_w2_down, m_g_final, v_c_ctx, v_w_mod, v_b_mod, v_g_ffn1, v_w1_gu, v_w1_down, v_g_mix, v_w_in, v_dw_weight, v_dw_bias, v_conv_ln_g, v_conv_ln_b, v_w_conv_out, v_w_alpha_f, v_b_alpha_f, v_w_alpha_b, v_b_alpha_b, v_gla_norm_g, v_w_gla_out, v_w_out, v_g_ffn2, v_w2_gu, v_w2_down, v_g_final):
    given = dict(locals())
    w = {n: given[n] for n in WEIGHTS}
    m = {n: given["m_" + n] for n in WEIGHTS}
    v = {n: given["v_" + n] for n in WEIGHTS}

    def bf16_shard(n):
        return w[n][0].astype(BF16)

    def install(wts, got):
        for n in ("w1_gu", "w2_gu"):
            if n in got:
                wts[n] = got[n]
        for n in ("w1_down", "w2_down", "w_conv_out", "w_gla_out", "w_out"):
            if n in got:
                wts[n] = got[n].reshape(-1, D)
        if "w_in" in got:
            wts["w_in_p"] = _perm_in_cols(jnp.concatenate([got["w_in"][j] for j in range(4)], axis=1))

    early = tuple(n for n in MATRICES if n not in _Overlap.LATE)
    shards =[bf16_shard(n) for n in early] + [_small_pack(w["dw_weight"][0], w["w_alpha_f"][0], w["w_alpha_b"][0])]
    got = dict(zip(early + ("small",), _gather_weights(shards)))
    wts = {n: w[n] for n in REPLICATED}
    install(wts, got)
    chip = 2 * lax.axis_index("x") + lax.axis_index("y")
    mod_cols = w["w_mod"].shape[2]
    wts["w_mod_shard"] = bf16_shard("w_mod")
    wts["b_mod_shard"] = lax.dynamic_slice(w["b_mod"], (0, chip * mod_cols), (1, mod_cols))
    sm = got["small"]
    wts["dw_weight"] = jnp.concatenate([sm[j, :CONV_W] for j in range(4)], axis=1)
    zpad = jnp.zeros((128, HEADS * DK), BF16)
    w_af = jnp.concatenate([sm[j, 32:32 + LOWRANK, :DK] for j in range(4)], axis=1)
    w_ab = jnp.concatenate([sm[j, 32:32 + LOWRANK, DK:] for j in range(4)], axis=1)
    wts["w_alpha_f_pad"] = zpad.at[0:LOWRANK].set(w_af.astype(BF16))
    wts["w_alpha_b_pad"] = zpad.at[LOWRANK:2 * LOWRANK].set(w_ab.astype(BF16))

    place = jnp.stack([lax.axis_index("c"), 2 * lax.axis_index("x") + lax.axis_index("y")]).astype(jnp.int32)
    comm = _Overlap(bf16_shard, lambda got_late: install(wts, got_late), place)
    loss, grad_x, grads = _local_step(x, c, ctx, loss_target, wts, comm)
    loss = lax.psum(loss, ("x", "y", "c"))

    tags = MATRICES + ("small",)
    rest = tuple(n for n in tags if n not in comm.landed)
    if rest:
        rest_sums = _chip_sums("rest", rest, grads, place)
        for n, cs, r in zip(rest, rest_sums, _scatter_chips(rest_sums)):
            comm.landed[n] = (cs, r)
    halves = [_sum_chips("grad_sum_" + t, *comm.landed[t], place) for t in tags]
    reduced = dict(zip(tags, _join_halves(halves)))
    g_shard = {n: reduced[n] for n in MATRICES}
    g_shard["w_mod"] = grads["w_mod"]
    g_shard["dw_weight"] = reduced["small"][:CONV_W]
    g_shard["w_alpha_f"] = reduced["small"][32:32 + LOWRANK, :DK]
    g_shard["w_alpha_b"] = reduced["small"][32:32 + LOWRANK, DK:]

    rep_shapes = [w[n].shape for n in REPLICATED]
    small = _allreduce_small(_pack_flat([grads[n].reshape(w[n].shape) for n in REPLICATED], 8 * 1024))
    g_rep = dict(zip(REPLICATED, _unpack_flat(small, rep_shapes)))

    g_out, d_out, m_out, v_out = {}, {}, {}, {}
    for n, _ in SHARDED:
        s2 = w[n].shape[1:]
        d, nm, nv = _adamw("adamw_" + n, w[n].reshape(s2), g_shard[n], m[n].reshape(s2), v[n].reshape(s2))
        g_out[n] = g_shard[n].reshape(w[n].shape)
        d_out[n], m_out[n], v_out[n] = d.reshape(w[n].shape), nm.reshape(w[n].shape), nv.reshape(w[n].shape)
    pk = lambda t: _pack_flat([t[n] for n in REPLICATED], 8 * 1024)
    d, nm, nv = _adamw("adamw_vectors", pk(w), small, pk(m), pk(v))
    for n, dd, mm, vv in zip(REPLICATED, _unpack_flat(d, rep_shapes), _unpack_flat(nm, rep_shapes),
                             _unpack_flat(nv, rep_shapes)):
        g_out[n], d_out[n], m_out[n], v_out[n] = g_rep[n], dd, mm, vv

    return (loss, grad_x, *[g_out[n] for n in WEIGHTS], *[d_out[n] for n in WEIGHTS],
            *[m_out[n] for n in WEIGHTS], *[v_out[n] for n in WEIGHTS])
```

```python
import functools
import math

import jax
import jax.numpy as jnp
from jax import lax
from jax.experimental import pallas as pl
from jax.experimental.pallas import tpu as pltpu

F32, BF16 = jnp.float32, jnp.bfloat16
MESH = pl.DeviceIdType.MESH
HIGHEST = lax.Precision.HIGHEST

D = 1024
FF = 2816
HEADS, DK, DV = 4, 128, 256
LOWRANK = 16
CONV_W = 31
CHUNK = 64
TAU = 16.0
EPS = 1e-6
Q_SCALE = DK ** -0.5
TM = 256
D_IN = 7200
D_INP = 7296
LR_COL = 7168
OG_CB, GA_CB, GB_CB = 6, 4, 5
VMEM_LIMIT = 52 * 1024 * 1024
WGRAD_VMEM = 40 * 1024 * 1024

ADAM_LR, ADAM_B1, ADAM_B2, ADAM_EPS, ADAM_WD, ADAM_STEP = 0.001, 0.9, 0.999, 1e-08, 0.01, 10


def _silu(x):
    return x * jax.nn.sigmoid(x)


def _rms(h, g):
    return h * lax.rsqrt(jnp.mean(h * h, axis=-1, keepdims=True) + EPS) * g


def _modnorm(x, g, shift, scale):
    return _rms(x, g) * (1 + scale) + shift


def _cparams(sem=None):
    return pltpu.CompilerParams(dimension_semantics=sem, vmem_limit_bytes=VMEM_LIMIT)


def _twice(v):
    return v, v


def tok(arr, width=None, cb=0, clamp=None):
    return ("tok", arr, arr.shape[1] if width is None else width, cb, clamp)


def mod(arr):
    return ("mod", arr)


def const(arr):
    return ("const", arr)


def _rowmap(name, body, n_tiles, ins, outs, *, tpb, nb):
    def modrow(i):
        return jnp.minimum(i // tpb, nb)

    in_specs, args = [], []
    for spec in ins:
        if spec[0] == "tok":
            _, arr, width, cb, clamp = spec
            if clamp is None:
                im = lambda i, cb=cb: (i, cb)
            else:
                im = lambda i, cb=cb, clamp=clamp: (jnp.minimum(i, clamp), cb)
            in_specs.append(pl.BlockSpec((TM, width), im))
        elif spec[0] == "mod":
            arr = spec[1]
            in_specs.append(pl.BlockSpec((1, 1, arr.shape[2]), lambda i: (modrow(i), 0, 0)))
        else:
            arr = spec[1]
            in_specs.append(pl.BlockSpec(arr.shape, lambda i, nd=arr.ndim: (0,) * nd))
        args.append(arr)
    out_specs, out_shapes, aliases = [], [], {}
    for o in outs:
        if o[0] == "tok":
            _, rows, width, dtype = o
            out_shapes.append(jax.ShapeDtypeStruct((rows, width), dtype))
            out_specs.append(pl.BlockSpec((TM, width), lambda i: (i, 0)))
        elif o[0] == "cols":
            buf, width, cb = o[1:4]
            first_tile = o[4] if len(o) > 4 else 0
            if not isinstance(buf, jax.ShapeDtypeStruct):
                aliases[len(args)] = len(out_shapes)
                in_specs.append(ANY)
                args.append(buf)
            out_shapes.append(jax.ShapeDtypeStruct(buf.shape, buf.dtype))
            out_specs.append(pl.BlockSpec((TM, width), lambda i, cb=cb, t0=first_tile: (i + t0, cb)))
        elif o[0] == "tok_head":
            _, rows, width, dtype, last = o
            out_shapes.append(jax.ShapeDtypeStruct((rows, width), dtype))
            out_specs.append(pl.BlockSpec((TM, width), lambda i, last=last: (jnp.minimum(i, last), 0)))
        elif o[0] == "tokT":
            _, rows, width, dtype = o
            out_shapes.append(jax.ShapeDtypeStruct((width, rows), dtype))
            out_specs.append(pl.BlockSpec((width, TM), lambda i: (0, i)))
        elif o[0] == "acc":
            _, rows, width = o
            out_shapes.append(jax.ShapeDtypeStruct((rows, width), F32))
            out_specs.append(pl.BlockSpec((rows, width), lambda i: (0, 0)))
        else:
            width = o[1]
            rows_visited = min((n_tiles - 1) // tpb, nb) + 1
            out_shapes.append(jax.ShapeDtypeStruct((rows_visited, 1, width), F32))
            out_specs.append(pl.BlockSpec((1, 1, width), lambda i: (modrow(i), 0, 0)))
    n_in = len(ins)

    def kern(*refs):
        i = pl.program_id(0)
        vals = []
        for r, spec in zip(refs[:n_in], ins):
            val = r[0] if spec[0] == "mod" else r[...]
            vals.append(val.astype(F32) if spec[0] == "tok" and val.dtype == BF16 else val)
        res = body(i, *vals)
        for r, o, val in zip(refs[len(args):], outs, res):
            if o[0] in ("tok", "cols"):
                r[...] = val.astype(r.dtype)
            elif o[0] == "tok_head":
                @pl.when(i <= o[4])
                def _():
                    r[...] = val.astype(r.dtype)
            elif o[0] == "tokT":
                r[...] = val.T.astype(r.dtype)
            elif o[0] == "acc":
                @pl.when(i == 0)
                def _():
                    r[...] = jnp.zeros(r.shape, F32)
                r[...] += jnp.broadcast_to(val, r.shape)
            else:
                first = jnp.logical_or(i == 0, modrow(i) != modrow(jnp.maximum(i - 1, 0)))

                @pl.when(first)
                def _():
                    r[...] = jnp.zeros(r.shape, F32)
                r[0] += val

    return pl.pallas_call(
        kern, grid=(n_tiles,), in_specs=in_specs, out_specs=out_specs, out_shape=out_shapes,
        input_output_aliases=aliases, compiler_params=_cparams(("arbitrary",)), name=name)(*args)


def _pick(n, cands):
    for c in cands:
        if n % c == 0:
            return c
    return n


def _pallas(kern, *, grid, in_specs, out_specs, out_shape, scratch_shapes, sem, name, args, side=None):
    if side is None:
        return pl.pallas_call(kern, grid=grid, in_specs=in_specs, out_specs=out_specs, out_shape=out_shape,
                              scratch_shapes=scratch_shapes, compiler_params=_cparams(sem), name=name)(*args)
    single = not isinstance(out_shape, (list, tuple))
    shapes = [out_shape] if single else list(out_shape)
    ospecs = [out_specs] if single else list(out_specs)
    n_in, n_out, n_scr = len(in_specs), len(shapes), len(scratch_shapes)
    s_in, s_out = list(side["ins"]), list(side["outs"])

    def wrapped(*refs):
        pos = [0]

        def take(n):
            pos[0] += n
            return refs[pos[0] - n:pos[0]]
        ins, sins, outs, souts, scr, sems = take(n_in), take(len(s_in)), take(n_out), take(len(s_out)), take(n_scr), take(2)
        ids = [pl.program_id(k) for k in range(len(grid))]
        first = functools.reduce(jnp.logical_and, [i == 0 for i in ids])
        last = functools.reduce(jnp.logical_and, [i == g - 1 for i, g in zip(ids, grid)])
        copy = _remote(*sems)

        @pl.when(first)
        def _():
            side["start"](sins, souts, copy)
        kern(*ins, *outs, *scr)

        @pl.when(last)
        def _():
            side["finish"](sins, souts, copy)

    res = pl.pallas_call(
        wrapped, grid=grid, in_specs=list(in_specs) + [ANY] * len(s_in), out_specs=ospecs + [ANY] * len(s_out),
        out_shape=shapes + s_out, scratch_shapes=list(scratch_shapes) + _sems(side["nsem"]),
        input_output_aliases={n_in + a: n_out + b for a, b in side.get("alias", {}).items()},
        compiler_params=_cparams(("arbitrary",) * len(grid)), name=name)(*args, *s_in)
    main = res[:n_out]
    return (main[0] if single else main), list(res[n_out:])


def _mm(name, a, b, *, trans_b=False, out_dtype=F32, a_fn=None, bias=None, rows=None, side=None, residual=None):
    m, k = a.shape if a.ndim == 2 else (a.shape[1], 2 * a.shape[2])
    m = m if rows is None else rows
    shard = b.shape[2] if b.ndim == 3 else None
    if trans_b:
        n = b.shape[-2]
        tk = _pick(shard, (2816, 2304, 1408, 1024)) if shard else (
            k if k <= 2816 else _pick(k, (2816, 2432, 2304, 2048, 1536, 1408, 1024, 512, 256, 128)))
        tn = _pick(n, (1024, 512, 384, 256, 128))
    else:
        n = 4 * shard if shard else b.shape[1]
        tk = k if k <= 2816 else _pick(k, (2816, 2432, 2304, 2048, 1536, 1408, 1024, 512, 256, 128))
        tn = _pick(shard, (512, 384, 1408, 256, 128)) if shard else _pick(n, (1024, 2432, 512, 384, 256, 128))
    if residual is not None:
        tm = next(c for c in (512, 256) if m % c == 0 and (residual["tpb"] * TM) % c == 0)
    else:
        tm = _pick(m, (1024, 512, 256))
    nk = k // tk
    per = shard // (tk if trans_b else tn) if shard else None
    dims = (((1,), (1,)), ((), ())) if trans_b else (((1,), (0,)), ((), ()))

    def kern(*refs):
        a_ref, b_ref = refs[0], refs[1]
        bias_ref = refs[2] if bias is not None else None
        acc_ref = refs[-1]
        if residual is not None:
            x_ref, gate_ref, o_ref, xo_ref = refs[-5:-1]
        else:
            o_ref = refs[-2]
        kk = pl.program_id(2)
        av = a_ref[...]
        if a_fn is not None:
            av = a_fn(av)
        p = lax.dot_general(av.astype(BF16), b_ref[...].astype(BF16), dims, preferred_element_type=F32)

        def finish(total):
            if bias_ref is not None:
                total = total + bias_ref[...]
            o_ref[...] = total.astype(o_ref.dtype)
            if residual is not None:
                xo_ref[...] = x_ref[...] + residual["scale"] * gate_ref[0] * total

        if nk == 1:
            finish(p)
        else:
            @pl.when(kk == 0)
            def _():
                acc_ref[...] = p

            @pl.when(kk > 0)
            def _():
                acc_ref[...] += p

            @pl.when(kk == nk - 1)
            def _():
                finish(acc_ref[...])

    if shard and trans_b:
        b_spec = pl.BlockSpec((None, tn, tk), lambda i, j, kk: (kk // per, j, kk % per))
    elif shard:
        b_spec = pl.BlockSpec((None, tk, tn), lambda i, j, kk: (j // per, kk, j % per))
    elif trans_b:
        b_spec = pl.BlockSpec((tn, tk), lambda i, j, kk: (j, kk))
    else:
        b_spec = pl.BlockSpec((tk, tn), lambda i, j, kk: (kk, j))
    if a.ndim == 3:
        pa = a.shape[2] // tk
        a_spec = pl.BlockSpec((None, tm, tk), lambda i, j, kk: (kk // pa, i, kk % pa))
    else:
        a_spec = pl.BlockSpec((tm, tk), lambda i, j, kk: (i, kk))
    in_specs = [a_spec, b_spec]
    args = [a, b]
    if bias is not None:
        in_specs.append(pl.BlockSpec((1, tn), lambda i, j, kk: (0, j)))
        args.append(bias)
    out_specs = pl.BlockSpec((tm, tn), lambda i, j, kk: (i, j))
    out_shape = jax.ShapeDtypeStruct((m, n), out_dtype)
    if residual is not None:
        tiles, nb = residual["tpb"] * TM // tm, residual["gate"].shape[0] - 1
        in_specs += [out_specs, pl.BlockSpec((1, 1, tn), lambda i, j, kk: (jnp.minimum(i // tiles, nb), 0, j))]
        args += [residual["x"], residual["gate"]]
        out_specs, out_shape = [out_specs, out_specs], [out_shape, jax.ShapeDtypeStruct((m, n), F32)]
    return _pallas(
        kern, grid=(m // tm, n // tn, nk), in_specs=in_specs, out_specs=out_specs, out_shape=out_shape,
        scratch_shapes=[pltpu.VMEM((tm, tn) if nk > 1 else (8, 128), F32)],
        sem=("parallel", "parallel", "arbitrary"), name=name, args=args, side=side)


def _mm_tn(name, x, dy):
    t, k1, n1 = x.shape[0], x.shape[1], dy.shape[1]
    tt = _pick(t, (512, 256, 128, 64, 8))
    tk1 = _pick(k1, (1024, 512, 256, 128))
    tn = _pick(n1, (512, 384, 256, 128))
    ns = t // tt

    def kern(x_ref, dy_ref, o_ref, acc_ref):
        s = pl.program_id(2)
        p = lax.dot_general(x_ref[...].astype(BF16), dy_ref[...].astype(BF16), (((0,), (0,)), ((), ())),
                            preferred_element_type=F32)

        @pl.when(s == 0)
        def _():
            acc_ref[...] = p

        @pl.when(s > 0)
        def _():
            acc_ref[...] += p

        @pl.when(s == ns - 1)
        def _():
            o_ref[...] = acc_ref[...].astype(o_ref.dtype)

    return pl.pallas_call(
        kern, grid=(k1 // tk1, n1 // tn, ns),
        in_specs=[pl.BlockSpec((tt, tk1), lambda i, j, s: (s, i)), pl.BlockSpec((tt, tn), lambda i, j, s: (s, j))],
        out_specs=pl.BlockSpec((tk1, tn), lambda i, j, s: (i, j)), out_shape=jax.ShapeDtypeStruct((k1, n1), F32),
        scratch_shapes=[pltpu.VMEM((tk1, tn), F32)],
        compiler_params=_cparams(("parallel", "parallel", "arbitrary")), name=name)(x, dy)


def _wgrad(name, xt, dy, rows=None, col_shards=False, side=None):
    k1 = xt.shape[0]
    t = xt.shape[1] if rows is None else rows
    n1 = dy.shape[1] if dy.ndim == 2 else 2 * dy.shape[2]
    tn = _pick(n1 // 4, (1408, 512, 384, 256, 128)) if col_shards else _pick(n1, (1024, 2432, 512, 384, 256, 128))

    def token_tile(tm):
        fixed = tm * tn * (4 + 4 + 2 * 2)
        return next((c for c in (2048, 1536, 1024, 512, 256, 128)
                     if t % c == 0 and fixed + 4 * c * (tm + tn) <= WGRAD_VMEM), 128)
    tm = next((c for c in (1024, 1408, 512, 256) if k1 % c == 0 and token_tile(c) >= 1024),
              _pick(k1, (1024, 1408, 512, 256)))
    tk = token_tile(tm)
    ns = t // tk
    per = n1 // 4 // tn

    def kern(x_ref, dy_ref, o_ref, acc_ref):
        s = pl.program_id(2)
        p = jnp.dot(x_ref[...], dy_ref[...], preferred_element_type=F32)

        @pl.when(s == 0)
        def _():
            acc_ref[...] = p

        @pl.when(s > 0)
        def _():
            acc_ref[...] += p

        @pl.when(s == ns - 1)
        def _():
            o_ref[...] = acc_ref[...].astype(o_ref.dtype)

    if dy.ndim == 3:
        pd = dy.shape[2] // tn
        dy_spec = pl.BlockSpec((None, tk, tn), lambda i, j, s: (j // pd, s, j % pd))
    else:
        dy_spec = pl.BlockSpec((tk, tn), lambda i, j, s: (s, j))
    if col_shards:
        out_spec = pl.BlockSpec((None, tm, tn), lambda i, j, s: (j // per, i, j % per))
        out_shape = jax.ShapeDtypeStruct((4, k1, n1 // 4), BF16)
    else:
        out_spec = pl.BlockSpec((tm, tn), lambda i, j, s: (i, j))
        out_shape = jax.ShapeDtypeStruct((k1, n1), BF16)
    return _pallas(
        kern, grid=(k1 // tm, n1 // tn, ns),
        in_specs=[pl.BlockSpec((tm, tk), lambda i, j, s: (i, s)), dy_spec],
        out_specs=out_spec, out_shape=out_shape, scratch_shapes=[pltpu.VMEM((tm, tn), F32)],
        sem=("parallel", "parallel", "arbitrary"), name=name, args=[xt, dy], side=side)


def _swiglu_fwd(name, u, w_gu, side=None):
    m = u.shape[0]
    half = w_gu.shape[2]
    tm = _pick(m, (512, 256))

    def kern(u_ref, wa_ref, wb_ref, ab_ref, hm_ref, hmt_ref):
        uv = u_ref[...]
        a = jnp.dot(uv, wa_ref[...], preferred_element_type=F32)
        b = jnp.dot(uv, wb_ref[...], preferred_element_type=F32)
        s = jax.nn.sigmoid(a)
        silu_a = a * s
        ab_ref[0] = (b * (s * (1 + a * (1 - s)))).astype(BF16)
        ab_ref[1] = silu_a.astype(BF16)
        hm = (silu_a * b).astype(BF16)
        hm_ref[...] = hm
        hmt_ref[...] = hm.T

    return _pallas(
        kern, grid=(2, m // tm),
        in_specs=[pl.BlockSpec((tm, D), lambda j, i: (i, 0)),
                  pl.BlockSpec((None, D, half), lambda j, i: (j, 0, 0)),
                  pl.BlockSpec((None, D, half), lambda j, i: (2 + j, 0, 0))],
        out_specs=[pl.BlockSpec((2, tm, half), lambda j, i: (0, i, j)),
                   pl.BlockSpec((tm, half), lambda j, i: (i, j)),
                   pl.BlockSpec((half, tm), lambda j, i: (j, i))],
        out_shape=[jax.ShapeDtypeStruct((2, m, FF), BF16), jax.ShapeDtypeStruct((m, FF), BF16),
                   jax.ShapeDtypeStruct((FF, m), BF16)],
        scratch_shapes=[], sem=("parallel", "parallel"), name=name, args=[u, w_gu, w_gu], side=side)


def _swiglu_bwd(name, df, w_down, ab):
    m = df.shape[0]
    half = FF // 2
    tm = _pick(m, (512, 256))

    def kern(df_ref, w_ref, ab_ref, o_ref):
        dh = lax.dot_general(df_ref[...], w_ref[...], (((1,), (1,)), ((), ())), preferred_element_type=F32)
        o_ref[0] = (dh * ab_ref[0].astype(F32)).astype(BF16)
        o_ref[1] = (dh * ab_ref[1].astype(F32)).astype(BF16)

    return pl.pallas_call(
        kern, grid=(2, m // tm),
        in_specs=[pl.BlockSpec((tm, D), lambda j, i: (i, 0)),
                  pl.BlockSpec((half, D), lambda j, i: (j, 0)),
                  pl.BlockSpec((2, tm, half), lambda j, i: (0, i, j))],
        out_specs=pl.BlockSpec((2, tm, half), lambda j, i: (0, i, j)),
        out_shape=jax.ShapeDtypeStruct((2, m, FF), BF16),
        compiler_params=_cparams(("parallel", "parallel")), name=name)(df, w_down, ab)


def _gla_maps(bl, t, tc):
    nx, nc = t // CHUNK, tc // CHUNK
    nxb = bl * nx

    def rowblk(d, b, n):
        c_ctx = jnp.where(d == 0, n, nc - 1 - n)
        c_x = jnp.where(d == 0, n - nc, nx - 1 - (n - nc))
        return jnp.where(n < nc, nxb + b * nc + c_ctx, b * nx + c_x)

    def xblk(d, b, n):
        n2 = jnp.maximum(n, nc)
        return b * nx + jnp.where(d == 0, n2 - nc, nx - 1 - (n2 - nc))

    return nx, nc, rowblk, xblk


def _dot01(m, x, cm):
    x1 = x.astype(BF16)
    r1 = x - x1.astype(F32)
    x2 = r1.astype(BF16)
    x3 = (r1 - x2.astype(F32)).astype(BF16)
    w = x.shape[1]
    p = lax.dot_general(m.astype(BF16), jnp.concatenate([x1, x2, x3], axis=1), (((cm,), (0,)), ((), ())),
                        preferred_element_type=F32)
    return p[:, :w] + p[:, w:2 * w] + p[:, 2 * w:]


def _gla_chunk(m, q, k, v, g, h):
    gh = g[:, h * DK:(h + 1) * DK]
    b = _dot01(m, gh, 1)
    tot = jnp.sum(gh, axis=0, keepdims=True)
    mid = b[CHUNK // 2:CHUNK // 2 + 1, :]
    qh = q[:, h * DK:(h + 1) * DK] * Q_SCALE
    kh = k[:, h * DK:(h + 1) * DK]
    vh = v[:, h * DV:(h + 1) * DV]
    return b, tot, mid, qh, kh, vh


def _dot(a, b, ca, cb):
    return lax.dot_general(a.astype(BF16), b.astype(BF16), (((ca,), (cb,)), ((), ())),
                           preferred_element_type=F32)


def _gla_fwd(p, g2, mmats, bl, t, tc):
    nx, nc, rowblk, xblk = _gla_maps(bl, t, tc)
    ns = nx + nc

    def kern(q0, k0, v0, g0, q1, k1, v1, g1, m_ref, o0, o1, ss_ref, s_ref):
        n = pl.program_id(1)

        @pl.when(n == 0)
        def _():
            s_ref[...] = jnp.zeros(s_ref.shape, F32)
        sides = ((q0, k0, v0, g0, o0), (q1, k1, v1, g1, o1))
        loaded = [(m_ref[d], q_ref[...].astype(F32), k_ref[...].astype(F32), v_ref[...].astype(F32), g_ref[...])
                  for d, (q_ref, k_ref, v_ref, g_ref, _) in enumerate(sides)]
        chains = [(d, h) for d in range(2) for h in range(HEADS)]
        base = [_gla_chunk(*loaded[d], h) for d, h in chains]
        pre = []
        for (d, h), (b, tot, mid, qh, kh, vh) in zip(chains, base):
            s0 = s_ref[d, h * DV:(h + 1) * DV, :]
            ss_ref[d, 0, 0, h * DV:(h + 1) * DV, :] = s0.astype(BF16)
            pre.append((s0, kh * jnp.exp(tot - b), qh * jnp.exp(b), qh * jnp.exp(b - mid), kh * jnp.exp(mid - b)))
        raw = [(_dot(qm, km, 1, 1), _dot(qe, s0, 1, 1), _dot(bs[5], kl, 0, 0))
               for bs, (s0, kl, qe, qm, km) in zip(base, pre)]
        for (d, h), bs, (s0, kl, qe, qm, km), (att_raw, inter, s_add) in zip(chains, base, pre, raw):
            s_ref[d, h * DV:(h + 1) * DV, :] = s0 * jnp.exp(bs[1]) + s_add
            sides[d][4][:, h * DV:(h + 1) * DV] = inter + _dot(loaded[d][0] * att_raw, bs[5], 1, 0)

    def operands(d):
        return [pl.BlockSpec((CHUNK, 512), lambda b, n: (rowblk(d, b, n), 4)),
                pl.BlockSpec((CHUNK, 512), lambda b, n: (rowblk(d, b, n), 5)),
                pl.BlockSpec((CHUNK, 1024), lambda b, n: (rowblk(d, b, n), 3)),
                pl.BlockSpec((CHUNK, 512), lambda b, n: (rowblk(d, b, n), d))]

    o_shape = jax.ShapeDtypeStruct((bl * t, HEADS * DV), F32)
    return pl.pallas_call(
        kern, grid=(bl, ns),
        in_specs=operands(0) + operands(1) + [pl.BlockSpec((2, CHUNK, CHUNK), lambda b, n: (0, 0, 0))],
        out_specs=[pl.BlockSpec((CHUNK, 1024), lambda b, n: (xblk(0, b, n), 0)),
                   pl.BlockSpec((CHUNK, 1024), lambda b, n: (xblk(1, b, n), 0)),
                   pl.BlockSpec((2, 1, 1, HEADS * DV, DK), lambda b, n: (0, b, n, 0, 0))],
        out_shape=[o_shape, o_shape, jax.ShapeDtypeStruct((2, bl, ns, HEADS * DV, DK), BF16)],
        scratch_shapes=[pltpu.VMEM((2, HEADS * DV, DK), F32)],
        compiler_params=_cparams(("parallel", "arbitrary")), name="gla_fwd")(p, p, p, g2, p, p, p, g2, mmats)


def _gla_bwd(p, g2, mmats, ssave, do, bl, t, tc):
    nx, nc, rowblk, xblk = _gla_maps(bl, t, tc)
    ns = nx + nc
    rev = lambda s: ns - 1 - s

    def kern(q0, k0, v0, g0, do0, q1, k1, v1, g1, do1, m_ref, ss_ref,
             dq0, dk0, dv0, dg0, dq1, dk1, dv1, dg1, ds_ref):
        step = pl.program_id(1)
        n = ns - 1 - step

        @pl.when(step == 0)
        def _():
            ds_ref[...] = jnp.zeros(ds_ref.shape, F32)
        live = (n >= nc).astype(F32)
        sides = ((q0, k0, v0, g0, do0, dq0, dk0, dv0, dg0), (q1, k1, v1, g1, do1, dq1, dk1, dv1, dg1))
        loaded = [(m_ref[d], s[0][...].astype(F32), s[1][...].astype(F32), s[2][...].astype(F32), s[3][...])
                  for d, s in enumerate(sides)]
        dovs = [s[4][...] * live for s in sides]
        chains = [(d, h) for d in range(2) for h in range(HEADS)]
        base = [_gla_chunk(*loaded[d], h) for d, h in chains]
        pre = []
        for (d, h), (b, tot, mid, qh, kh, vh) in zip(chains, base):
            eb, ebm, emb, etb = jnp.exp(b), jnp.exp(b - mid), jnp.exp(mid - b), jnp.exp(tot - b)
            pre.append(dict(
                eb=eb, ebm=ebm, emb=emb, etb=etb, etot=jnp.exp(tot), qe=qh * eb, qm=qh * ebm, km=kh * emb, kl=kh * etb,
                vh=vh, doh=dovs[d][:, h * DV:(h + 1) * DV], s0=ss_ref[d, 0, 0, h * DV:(h + 1) * DV, :].astype(F32),
                ds1=ds_ref[d, h * DV:(h + 1) * DV, :]))
        first = [dict(att=_dot(c["qm"], c["km"], 1, 1), datt=_dot(c["doh"], c["vh"], 1, 1),
                      dqe=_dot(c["doh"], c["s0"], 1, 0), ds_add=_dot(c["doh"], c["qe"], 0, 0),
                      dkl=_dot(c["vh"], c["ds1"], 1, 0), dv_s=_dot(c["kl"], c["ds1"], 1, 1)) for c in pre]
        second = []
        for (d, h), c, f in zip(chains, pre, first):
            m = loaded[d][0]
            ds_ref[d, h * DV:(h + 1) * DV, :] = c["ds1"] * c["etot"] + f["ds_add"]
            att, datt = m * f["att"], m * f["datt"]
            second.append(dict(dqm=_dot(datt, c["km"], 1, 0), dkm=_dot(datt, c["qm"], 0, 0),
                               dv_a=_dot(att, c["doh"], 0, 0)))
        for (d, h), c, f, s in zip(chains, pre, first, second):
            dq_ref, dk_ref, dv_ref, dg_ref = sides[d][5:]
            dtot = c["etot"] * jnp.sum(c["ds1"] * c["s0"], axis=0, keepdims=True) + jnp.sum(
                f["dkl"] * c["kl"], axis=0, keepdims=True)
            db = f["dqe"] * c["qe"] + s["dqm"] * c["qm"] - s["dkm"] * c["km"] - f["dkl"] * c["kl"]
            dq_ref[:, h * DK:(h + 1) * DK] = ((f["dqe"] * c["eb"] + s["dqm"] * c["ebm"]) * Q_SCALE).astype(BF16)
            dk_ref[:, h * DK:(h + 1) * DK] = (s["dkm"] * c["emb"] + f["dkl"] * c["etb"]).astype(BF16)
            dv_ref[:, h * DV:(h + 1) * DV] = (s["dv_a"] + f["dv_s"]).astype(BF16)
            dg_ref[:, h * DK:(h + 1) * DK] = _dot01(loaded[d][0], db, 0) + dtot

    nt = p.shape[0]

    def operands(d):
        return [pl.BlockSpec((CHUNK, 512), lambda b, s: (rowblk(d, b, rev(s)), 4)),
                pl.BlockSpec((CHUNK, 512), lambda b, s: (rowblk(d, b, rev(s)), 5)),
                pl.BlockSpec((CHUNK, 1024), lambda b, s: (rowblk(d, b, rev(s)), 3)),
                pl.BlockSpec((CHUNK, 512), lambda b, s: (rowblk(d, b, rev(s)), d)),
                pl.BlockSpec((CHUNK, 1024), lambda b, s: (xblk(d, b, rev(s)), 0))]

    def results(d):
        row = lambda b, s: (rowblk(d, b, rev(s)), 0)
        return [pl.BlockSpec((CHUNK, 512), row), pl.BlockSpec((CHUNK, 512), row), pl.BlockSpec((CHUNK, 1024), row),
                pl.BlockSpec((CHUNK, 512), row)]

    shapes = [jax.ShapeDtypeStruct((nt, 512), BF16), jax.ShapeDtypeStruct((nt, 512), BF16),
              jax.ShapeDtypeStruct((nt, 1024), BF16), jax.ShapeDtypeStruct((nt, 512), F32)]
    out = pl.pallas_call(
        kern, grid=(bl, ns),
        in_specs=operands(0) + operands(1) + [
            pl.BlockSpec((2, CHUNK, CHUNK), lambda b, s: (0, 0, 0)),
            pl.BlockSpec((2, 1, 1, HEADS * DV, DK), lambda b, s: (0, b, rev(s), 0, 0))],
        out_specs=results(0) + results(1), out_shape=shapes + shapes,
        scratch_shapes=[pltpu.VMEM((2, HEADS * DV, DK), F32)],
        compiler_params=_cparams(("parallel", "arbitrary")), name="gla_bwd")(
            p, p, p, g2, do, p, p, p, g2, do, mmats, ssave)
    return out[:4], out[4:]


CONV_CT = 256
CONV_PAD = 16
CONV_RC = 128
CONV_HALO = 24


def _conv_fill(zp, z_ref, t):
    zp[0:CONV_PAD, :] = jnp.zeros((CONV_PAD, CONV_CT), F32)
    zp[CONV_PAD + t:2 * CONV_PAD + t, :] = jnp.zeros((CONV_PAD, CONV_CT), F32)
    zp[CONV_PAD:CONV_PAD + t, :] = z_ref[...]


def _dwconv(name, z, w, bias, bl, t, flip):
    def kern(z_ref, w_ref, b_ref, o_ref, zp):
        _conv_fill(zp, z_ref, t)
        offs = [(CONV_W - j) if flip else (j + 1) for j in range(CONV_W)]
        for r in range(0, t, CONV_RC):
            acc = jnp.broadcast_to(b_ref[...], (CONV_RC, CONV_CT))
            for rot in range(8):
                win = zp[r + rot:r + rot + CONV_RC + CONV_HALO, :]
                for j in range(CONV_W):
                    if offs[j] % 8 == rot:
                        a = offs[j] - rot
                        acc = acc + w_ref[j:j + 1, :] * win[a:a + CONV_RC, :]
            o_ref[r:r + CONV_RC, :] = acc

    return pl.pallas_call(
        kern, grid=(bl, 1024 // CONV_CT),
        in_specs=[pl.BlockSpec((t, CONV_CT), lambda b, c: (b, c)),
                  pl.BlockSpec((32, CONV_CT), lambda b, c: (0, c)),
                  pl.BlockSpec((1, CONV_CT), lambda b, c: (0, c))],
        out_specs=pl.BlockSpec((t, CONV_CT), lambda b, c: (b, c)),
        out_shape=jax.ShapeDtypeStruct(z.shape, F32),
        scratch_shapes=[pltpu.VMEM((t + 2 * CONV_PAD, CONV_CT), F32)],
        compiler_params=_cparams(("parallel", "parallel")), name=name)(z, w, bias)


def _dwconv_wgrad(z, dzc, bl, t):
    def kern(z_ref, d_ref, dw_ref, db_ref, zp):
        b = pl.program_id(1)

        @pl.when(b == 0)
        def _():
            dw_ref[...] = jnp.zeros(dw_ref.shape, F32)
            db_ref[...] = jnp.zeros(db_ref.shape, F32)
        _conv_fill(zp, z_ref, t)
        for rot in range(8):
            taps = [j for j in range(CONV_W) if (j + 1) % 8 == rot]
            accs = [jnp.zeros((8, CONV_CT), F32) for _ in taps]
            for r in range(0, t, CONV_RC):
                d = d_ref[r:r + CONV_RC, :]
                win = zp[r + rot:r + rot + CONV_RC + CONV_HALO, :]
                for k, j in enumerate(taps):
                    a = j + 1 - rot
                    prod = d * win[a:a + CONV_RC, :]
                    accs[k] = accs[k] + jnp.sum(prod.reshape(CONV_RC // 8, 8, CONV_CT), axis=0)
            for k, j in enumerate(taps):
                dw_ref[j:j + 1, :] += jnp.sum(accs[k], axis=0, keepdims=True)
        db_ref[...] += jnp.sum(d_ref[...], axis=0, keepdims=True)

    return pl.pallas_call(
        kern, grid=(1024 // CONV_CT, bl),
        in_specs=[pl.BlockSpec((t, CONV_CT), lambda c, b: (b, c)),
                  pl.BlockSpec((t, CONV_CT), lambda c, b: (b, c))],
        out_specs=[pl.BlockSpec((32, CONV_CT), lambda c, b: (0, c)),
                   pl.BlockSpec((1, CONV_CT), lambda c, b: (0, c))],
        out_shape=[jax.ShapeDtypeStruct((32, 1024), F32), jax.ShapeDtypeStruct((1, 1024), F32)],
        scratch_shapes=[pltpu.VMEM((t + 2 * CONV_PAD, CONV_CT), F32)],
        compiler_params=_cparams(("parallel", "arbitrary")), name="dwconv_wgrad")(z, dzc)


def _ffn_fwd(tag, xin, n_tiles, g, sh, sc, gate, w_gu, w_down, tpb, nb, comm=None):
    rows = n_tiles * TM
    rm = functools.partial(_rowmap, tpb=tpb, nb=nb)
    u, ut = rm(tag + "_norm", lambda i, x, g_, sh_, sc_: _twice(_modnorm(x, g_, sh_, sc_)), n_tiles,
               [tok(xin), const(g), mod(sh), mod(sc)], [("tok", rows, D, BF16), ("tokT", rows, D, BF16)])
    ab, hm, hmt = _with_side(comm, tag + "_gu", None, lambda s: _swiglu_fwd(tag + "_gu", u, w_gu, side=s))
    res = dict(x=xin, gate=gate, scale=0.5, tpb=tpb)
    f, xout = _with_side(comm, tag + "_down", None,
                         lambda s: _mm(tag + "_down", hm, w_down, out_dtype=BF16, side=s, residual=res))
    return xout, (ut, ab, hmt, f)


def _ffn_bwd(tag, xin, saved, dxout, dx_clamp, n_tiles, g, sh, sc, gate, w_gu, w_down, tpb, nb, comm=None, grads=None,
             names=None, keep_tiles=None):
    ut, ab, hmt, f = saved
    rows = n_tiles * TM
    rm = functools.partial(_rowmap, tpb=tpb, nb=nb)

    def mask(i):
        return 1.0 if dx_clamp is None else (i <= dx_clamp).astype(F32)

    def b1(i, dx, f_, gt):
        dx = dx * mask(i)
        return (0.5 * gt * dx, jnp.sum(0.5 * f_ * dx, axis=0, keepdims=True))
    df, dgate = rm(tag + "_bres", b1, n_tiles, [tok(dxout, clamp=dx_clamp), tok(f), mod(gate)],
                   [("tok", rows, D, BF16), ("modacc", D)])
    grads[names[1]] = _wgrad(tag + "_wdown", hmt, df)
    dab = _swiglu_bwd(tag + "_bdown", df, w_down, ab)
    grads[names[0]] = _with_side(comm, tag + "_wgu", grads,
                                 lambda s: _wgrad(tag + "_wgu", ut, dab, col_shards=True, side=s))
    du = _with_side(comm, tag + "_bgu", grads, lambda s: _mm(tag + "_bgu", dab, w_gu, trans_b=True, side=s))

    def b3(i, x, g_, sh_, sc_, du_, dx):
        _, vjp = jax.vjp(_modnorm, x, g_, sh_, sc_)
        dxn, dg, dsh, dsc = vjp(du_)
        return (dx * mask(i) + dxn, dg, dsh, dsc)
    dx_out = ("tok", rows, D, F32) if keep_tiles is None else ("tok_head", keep_tiles * TM, D, F32, keep_tiles - 1)
    dxin, dg, dsh, dsc = rm(tag + "_bnorm", b3, n_tiles,
                            [tok(xin), const(g), mod(sh), mod(sc), tok(du), tok(dxout, clamp=dx_clamp)],
                            [dx_out, ("acc", 1, D), ("modacc", D), ("modacc", D)])
    return dxin, dict(g=dg, sh=dsh, sc=dsc, gate=dgate)


def _perm_in_cols(w):
    pad = jnp.zeros(w.shape[:-1] + (D_INP - D_IN,), w.dtype)
    return jnp.concatenate([w[..., :4096], w[..., 5152:7200], w[..., 4096:5120], w[..., 5120:5152], pad], axis=-1)


def _unperm_in_cols(w):
    return jnp.concatenate([w[..., :4096], w[..., 6144:LR_COL], w[..., LR_COL:LR_COL + 32], w[..., 4096:6144]], axis=-1)


def _local_step(x, c, ctx, target, wts, comm=None):
    bl, t, _ = x.shape
    tc = ctx.shape[1]
    nx_rows, nc_rows = bl * t, bl * tc
    nt_rows = nx_rows + nc_rows
    tpb = t // TM
    nxt, ntt = nx_rows // TM, nt_rows // TM
    nb = bl
    rm = functools.partial(_rowmap, tpb=tpb, nb=nb)
    last_x = nxt - 1

    x0 = jnp.concatenate([x.reshape(nx_rows, D), ctx.reshape(nc_rows, D)], axis=0)
    tgt = target.reshape(nx_rows, D)

    cc = jnp.concatenate([c, wts["c_ctx"].reshape(1, D), jnp.zeros((8 - bl - 1, D), F32)], axis=0)
    modv, cc_all = _mod_forward(cc, wts["w_mod_shard"], wts["b_mod_shard"])
    mods = [modv[:nb + 1, k * D:(k + 1) * D].reshape(nb + 1, 1, D) for k in range(9)]

    x1, sv1 = _ffn_fwd("ffn1", x0, ntt, wts["g_ffn1"], mods[0], mods[1], mods[2], wts["w1_gu"], wts["w1_down"],
                       tpb, nb, comm)
    u2, u2t = rm("in_norm", lambda i, x_, g_, sh_, sc_: _twice(_modnorm(x_, g_, sh_, sc_)), ntt,
                 [tok(x1), const(wts["g_mix"]), mod(mods[3]), mod(mods[4])],
                 [("tok", nt_rows, D, BF16), ("tokT", nt_rows, D, BF16)])
    w_inp = wts["w_in_p"]
    p = _with_side(comm, "in_proj", None, lambda s: _mm("in_proj", u2, w_inp, out_dtype=BF16, side=s))

    waf, wab, baf, bab = wts["w_alpha_f_pad"], wts["w_alpha_b_pad"], wts["b_alpha_f"], wts["b_alpha_b"]

    def dec_fwd(i, lr, wf, wb, bf_, bb_):
        zf = _dot(lr, wf, 1, 0) + bf_
        zb = _dot(lr, wb, 1, 0) + bb_
        return (jnp.concatenate([jax.nn.log_sigmoid(zf) / TAU, jax.nn.log_sigmoid(zb) / TAU], axis=1),)
    (gfb,) = rm("decay_fwd", dec_fwd, ntt, [tok(p, 128, LR_COL // 128), const(waf), const(wab), const(baf), const(bab)],
                [("tok", nt_rows, 1024, F32)])
    g2 = gfb
    tri = jnp.tril(jnp.ones((CHUNK, CHUNK), F32))
    mmats = jnp.stack([tri, tri.T])
    *o2, ssave = _gla_fwd(p, g2, mmats, bl, t, tc)

    gn_g = wts["gla_norm_g"]

    def gla_out(of, ob, og, gn):
        o = of + ob
        parts = []
        for h in range(HEADS):
            oh = o[:, h * DV:(h + 1) * DV]
            parts.append(oh * lax.rsqrt(jnp.mean(oh * oh, axis=-1, keepdims=True) + EPS))
        return jnp.concatenate(parts, axis=1) * gn * _silu(og)
    yg_in, yg_int = rm("gla_out", lambda i, of, ob, og, gn: _twice(gla_out(of, ob, og, gn)), nxt,
                       [tok(o2[0]), tok(o2[1]), tok(p, 1024, OG_CB), const(gn_g)],
                       [("tok", nx_rows, D, BF16), ("tokT", nx_rows, D, BF16)])
    y_gla = _with_side(comm, "gla_proj", None,
                       lambda s: _mm("gla_proj", yg_in, wts["w_gla_out"], out_dtype=BF16, side=s))

    (z,) = rm("glu", lambda i, a, b: (a * jax.nn.sigmoid(b),), nxt, [tok(p, 1024, 0), tok(p, 1024, 1)],
              [("tok", nx_rows, D, F32)])
    dw_w = jnp.concatenate([wts["dw_weight"], jnp.zeros((1, D), F32)], axis=0)
    zc = _dwconv("dwconv_fwd", z, dw_w, wts["dw_bias"], bl, t, False)

    def ln_silu(zc_, g_, b_):
        mu = jnp.mean(zc_, axis=-1, keepdims=True)
        var = jnp.mean(jnp.square(zc_ - mu), axis=-1, keepdims=True)
        return _silu((zc_ - mu) * lax.rsqrt(var + EPS) * g_ + b_)
    ln_g, ln_b = wts["conv_ln_g"], wts["conv_ln_b"]
    zl, zlt = rm("conv_ln", lambda i, zc_, g_, b_: _twice(ln_silu(zc_, g_, b_)), nxt,
                 [tok(zc), const(ln_g), const(ln_b)], [("tok", nx_rows, D, BF16), ("tokT", nx_rows, D, BF16)])
    y_conv = _mm("conv_proj", zl, wts["w_conv_out"], out_dtype=BF16)

    mg, mgt = rm("merge", lambda i, ga, gb, yc, yg: _twice(jax.nn.sigmoid(ga) * yc + jax.nn.sigmoid(gb) * yg), nxt,
                 [tok(p, 1024, GA_CB), tok(p, 1024, GB_CB), tok(y_conv), tok(y_gla)],
                 [("tok", nx_rows, D, BF16), ("tokT", nx_rows, D, BF16)])
    mix, x2 = _mm("out_proj", mg, wts["w_out"], out_dtype=BF16,
                  residual=dict(x=x1, gate=mods[5], scale=1.0, tpb=tpb))

    x3, sv2 = _ffn_fwd("ffn2", x2, nxt, wts["g_ffn2"], mods[6], mods[7], mods[8], wts["w2_gu"], wts["w2_down"],
                       tpb, nb)
    g_fin = wts["g_final"].reshape(1, D)

    def head(i, x_, g_, tg):
        y, vjp = jax.vjp(_rms, x_, g_)
        diff = y - tg
        dx, dg = vjp(diff * (1.0 / D))
        loss = 0.5 * jnp.sum(jnp.mean(diff * diff, axis=-1, keepdims=True))
        return dx, dg, loss
    dx3, dg_final, loss_acc = rm("loss_head", head, nxt, [tok(x3), const(g_fin), tok(tgt)],
                                 [("tok", nx_rows, D, F32), ("acc", 1, D), ("acc", 8, 128)])
    loss = loss_acc[0, 0]

    grads = {}
    dx2, gf2 = _ffn_bwd("ffn2", x2, sv2, dx3, None, nxt, wts["g_ffn2"], mods[6], mods[7], mods[8],
                        wts["w2_gu"], wts["w2_down"], tpb, nb, comm, grads, ("w2_gu", "w2_down"))
    grads["g_ffn2"] = gf2["g"]

    dmix, dgate5 = rm("mix_bres", lambda i, dx, mx_, gt: (gt * dx, jnp.sum(mx_ * dx, axis=0, keepdims=True)), nxt,
                      [tok(dx2), tok(mix), mod(mods[5])], [("tok", nx_rows, D, BF16), ("modacc", D)])
    dmg = _mm("out_bproj", dmix, wts["w_out"], trans_b=True, out_dtype=BF16)
    grads["w_out"] = _wgrad("out_wgrad", mgt, dmix)

    def merge_bwd(i, dm, ga, gb, yc, yg):
        keep = (i <= last_x).astype(F32)
        dm = dm * keep
        sa, sb = jax.nn.sigmoid(ga), jax.nn.sigmoid(gb)
        return dm * sa, dm * sb, jnp.concatenate([dm * yc * sa * (1 - sa), dm * yg * sb * (1 - sb)], axis=1)
    cl = dict(clamp=last_x)
    dyc, dyg, dp = rm("merge_bwd", merge_bwd, ntt,
                      [tok(dmg, **cl), tok(p, 1024, GA_CB, last_x), tok(p, 1024, GB_CB, last_x), tok(y_conv, **cl),
                       tok(y_gla, **cl)],
                      [("tok", nt_rows, D, BF16), ("tok", nt_rows, D, BF16),
                       ("cols", jax.ShapeDtypeStruct((nt_rows, D_INP), BF16), 2048, 2)])

    dzl = _mm("conv_bproj", dyc, wts["w_conv_out"], trans_b=True, rows=nx_rows, out_dtype=BF16)
    grads["w_conv_out"] = _wgrad("conv_wgrad", zlt, dyc, rows=nx_rows)

    def ln_bwd(i, zc_, g_, b_, dz_):
        _, vjp = jax.vjp(ln_silu, zc_, g_, b_)
        return vjp(dz_)
    dzc, dln_g, dln_b = rm("conv_ln_bwd", ln_bwd, nxt, [tok(zc), const(ln_g), const(ln_b), tok(dzl)],
                           [("tok", nx_rows, D, F32), ("acc", 1, D), ("acc", 1, D)])
    dz = _dwconv("dwconv_bwd", dzc, dw_w, jnp.zeros((1, D), F32), bl, t, True)
    ddw, ddb = _dwconv_wgrad(z, dzc, bl, t)
    grads.update(conv_ln_g=dln_g, conv_ln_b=dln_b, dw_weight=ddw[:CONV_W], dw_bias=ddb)

    def glu_bwd(i, dz_, a, b):
        keep = (i <= last_x).astype(F32)
        dz_ = dz_ * keep
        s = jax.nn.sigmoid(b)
        return (jnp.concatenate([dz_ * s, dz_ * a * s * (1 - s)], axis=1),)
    (dp,) = rm("glu_bwd", glu_bwd, ntt, [tok(dz, **cl), tok(p, 1024, 0, last_x), tok(p, 1024, 1, last_x)],
               [("cols", dp, 2048, 0)])

    dyg_in = _mm("gla_bproj", dyg, wts["w_gla_out"], trans_b=True, rows=nx_rows, out_dtype=BF16)
    grads["w_gla_out"] = _wgrad("gla_wgrad", yg_int, dyg, rows=nx_rows)

    def gla_out_bwd(i, of, ob, og, gn, dy):
        _, vjp = jax.vjp(gla_out, of, ob, og, gn)
        do_, _, dog_, dgn_ = vjp(dy)
        return do_, dog_, dgn_
    do, dp, dgn = rm("gla_out_bwd", gla_out_bwd, nxt,
                     [tok(o2[0]), tok(o2[1]), tok(p, 1024, OG_CB), const(gn_g), tok(dyg_in)],
                     [("tok", nx_rows, D, BF16), ("cols", dp, 1024, OG_CB), ("acc", 1, D)])
    grads["gla_norm_g"] = dgn
    (dp,) = rm("og_ctx_zero", lambda i: (jnp.zeros((TM, D), F32),), ntt - nxt, [], [("cols", dp, 1024, OG_CB, nxt)])

    dq2, dk2, dv2, dg2 = zip(*_gla_bwd(p, g2, mmats, ssave, do, bl, t, tc))

    def dec_bwd(i, lr, wf, wb, bf_, bb_, dgf, dgb_):
        zf = _dot(lr, wf, 1, 0) + bf_
        zb = _dot(lr, wb, 1, 0) + bb_
        dzf = dgf * (1 - jax.nn.sigmoid(zf)) * (1.0 / TAU)
        dzb = dgb_ * (1 - jax.nn.sigmoid(zb)) * (1.0 / TAU)
        dlr = _dot(dzf, wf, 1, 1) + _dot(dzb, wb, 1, 1)
        return (dlr, _dot(lr, dzf, 0, 0), _dot(lr, dzb, 0, 0), jnp.sum(dzf, axis=0, keepdims=True),
                jnp.sum(dzb, axis=0, keepdims=True))
    dp, dwaf, dwab, dbaf, dbab = rm(
        "decay_bwd", dec_bwd, ntt,
        [tok(p, 128, LR_COL // 128), const(waf), const(wab), const(baf), const(bab), tok(dg2[0]), tok(dg2[1])],
        [("cols", dp, 128, LR_COL // 128), ("acc", 128, 512), ("acc", 128, 512), ("acc", 1, 512), ("acc", 1, 512)])
    grads.update(w_alpha_f=dwaf[:LOWRANK], w_alpha_b=dwab[LOWRANK:2 * LOWRANK], b_alpha_f=dbaf, b_alpha_b=dbab)

    (dp,) = rm("gla_sum",
               lambda i, q0, q1, k0, k1, v0, v1: (jnp.concatenate([q0 + q1, k0 + k1, v0 + v1], axis=1),), ntt,
               [tok(dq2[0]), tok(dq2[1]), tok(dk2[0]), tok(dk2[1]), tok(dv2[0]), tok(dv2[1])],
               [("cols", dp, 2048, 1)])
    du2 = _with_side(comm, "in_bproj", grads, lambda s: _mm("in_bproj", dp, w_inp, trans_b=True, side=s))
    grads["w_in_p"] = _wgrad("in_wgrad", u2t, dp)

    def in_norm_bwd(i, x_, g_, sh_, sc_, du_, dx):
        keep = (i <= last_x).astype(F32)
        _, vjp = jax.vjp(_modnorm, x_, g_, sh_, sc_)
        dxn, dg, dsh, dsc = vjp(du_)
        return (dx * keep + dxn, dg, dsh, dsc)
    dx1, dg_mix, dsh3, dsc4 = rm("in_norm_bwd", in_norm_bwd, ntt,
                                 [tok(x1), const(wts["g_mix"]), mod(mods[3]), mod(mods[4]), tok(du2), tok(dx2, **cl)],
                                 [("tok", nt_rows, D, F32), ("acc", 1, D), ("modacc", D), ("modacc", D)])
    grads["g_mix"] = dg_mix

    dx0, gf1 = _ffn_bwd("ffn1", x0, sv1, dx1, None, ntt, wts["g_ffn1"], mods[0], mods[1], mods[2],
                        wts["w1_gu"], wts["w1_down"], tpb, nb, comm, grads, ("w1_gu", "w1_down"), keep_tiles=nxt)
    grads["g_ffn1"] = gf1["g"]
    grad_x = dx0.reshape(bl, t, D)

    dmods = [gf1["sh"], gf1["sc"], gf1["gate"], dsh3, dsc4, dgate5, gf2["sh"], gf2["sc"], gf2["gate"]]
    dmod = jnp.concatenate(
        [jnp.concatenate([a.reshape(a.shape[0], D), jnp.zeros((8 - a.shape[0], D), F32)], axis=0) for a in dmods],
        axis=1)
    grads["w_mod"], grads["c_ctx"], grads["b_mod"] = _mod_backward(dmod, cc_all, wts["w_mod_shard"], nb)
    grads["g_final"] = dg_final.reshape(D)
    return loss, grad_x, grads


ANY = pl.BlockSpec(memory_space=pl.ANY)


def _place():
    x, y, c = lax.axis_index("x"), lax.axis_index("y"), lax.axis_index("c")
    chips = [(1 - x, y), (x, 1 - y), (1 - x, 1 - y)]
    return x, y, c, chips


def _remote(send_sems, recv_sems):
    def copy(k, src, dst, to):
        return pltpu.make_async_remote_copy(src_ref=src, dst_ref=dst, send_sem=send_sems.at[k],
                                            recv_sem=recv_sems.at[k], device_id=to, device_id_type=MESH)
    return copy


def _sems(n):
    return [pltpu.SemaphoreType.DMA((n,)), pltpu.SemaphoreType.DMA((n,))]


def _gather_weights(shards):
    n = len(shards)

    def body(*refs):
        ins, outs = refs[:n], refs[n:2 * n]
        copy = _remote(refs[2 * n], refs[2 * n + 1])
        x, y, c, chips = _place()
        me = 2 * x + y
        sibling = (x, y, 1 - c)
        started = []

        def rows(i, hc):
            hr = ins[i].shape[0] // 2
            return pl.ds(hc * hr, hr)

        for i in range(n):
            started.append(copy(7 * i + 6, ins[i], outs[i].at[me], sibling))
            started[-1].start()
            for j, (px, py) in enumerate(chips):
                started.append(copy(7 * i + j, ins[i].at[rows(i, c), :], outs[i].at[me, rows(i, c), :], (px, py, c)))
                started[-1].start()
        for i in range(n):
            for j, (px, py) in enumerate(chips):
                half = outs[i].at[2 * px + py, rows(i, c), :]
                copy(7 * i + j, half, half, (px, py, c)).wait_recv()
                started.append(copy(7 * i + 3 + j, half, half, sibling))
                started[-1].start()
        for i in range(n):
            copy(7 * i + 6, ins[i], outs[i].at[me], sibling).wait_recv()
            for j, (px, py) in enumerate(chips):
                other = outs[i].at[2 * px + py, rows(i, 1 - c), :]
                copy(7 * i + 3 + j, other, other, sibling).wait_recv()
        for cp in started:
            cp.wait_send()

    return pl.pallas_call(
        body, out_shape=[jax.ShapeDtypeStruct((4,) + s.shape, s.dtype) for s in shards], in_specs=[ANY] * n,
        out_specs=[ANY] * n, scratch_shapes=_sems(7 * n), name="gather_weights")(*shards)


def _swap_halves(name, gs):
    n = len(gs)

    def body(*refs):
        ins, outs = refs[:n], refs[n:2 * n]
        copy = _remote(refs[2 * n], refs[2 * n + 1])
        x, y, c, _ = _place()
        cps = []
        for i in range(n):
            hr = ins[i].shape[1] // 2
            cps.append(copy(i, ins[i].at[:, pl.ds((1 - c) * hr, hr), :], outs[i], (x, y, 1 - c)))
            cps[-1].start()
        for cp in cps:
            cp.wait()

    return pl.pallas_call(
        body, out_shape=[jax.ShapeDtypeStruct((4, g.shape[1] // 2, g.shape[2]), g.dtype) for g in gs],
        in_specs=[ANY] * n, out_specs=[ANY] * n, scratch_shapes=_sems(n), name=name)(*gs)


def _row_tile(hr):
    return hr if hr <= 256 else _pick(hr, (256, 176, 128, 64, 32, 16))


def _add_halves(name, g, r, place):
    hr = r.shape[1]
    tr = _row_tile(hr)
    nblk = hr // tr

    def kern(p_ref, g_ref, r_ref, o_ref):
        o_ref[...] = (g_ref[...].astype(F32) + r_ref[...].astype(F32)).astype(o_ref.dtype)

    blk = (1, tr, g.shape[2])
    return pl.pallas_call(
        kern,
        grid_spec=pltpu.PrefetchScalarGridSpec(
            num_scalar_prefetch=1, grid=(4, nblk),
            in_specs=[pl.BlockSpec(blk, lambda j, i, p: (j, p[0] * nblk + i, 0)),
                      pl.BlockSpec(blk, lambda j, i, p: (j, i, 0))],
            out_specs=pl.BlockSpec(blk, lambda j, i, p: (j, i, 0))),
        out_shape=jax.ShapeDtypeStruct(r.shape, r.dtype),
        compiler_params=_cparams(("parallel", "parallel")), name=name)(place, g, r)


def _scatter_chips(cs):
    n = len(cs)

    def body(*refs):
        ins, outs = refs[:n], refs[n:2 * n]
        copy = _remote(refs[2 * n], refs[2 * n + 1])
        x, y, c, chips = _place()
        me = 2 * x + y
        sends = []
        for i in range(n):
            for j, (px, py) in enumerate(chips):
                sends.append(copy(3 * i + j, ins[i].at[2 * px + py], outs[i].at[me], (px, py, c)))
                sends[-1].start()
        for i in range(n):
            for j, (px, py) in enumerate(chips):
                src = 2 * px + py
                copy(3 * i + j, ins[i].at[src], outs[i].at[src], (px, py, c)).wait_recv()
        for cp in sends:
            cp.wait_send()

    return pl.pallas_call(
        body, out_shape=[jax.ShapeDtypeStruct(a.shape, a.dtype) for a in cs], in_specs=[ANY] * n,
        out_specs=[ANY] * n, scratch_shapes=_sems(3 * n), name="grad_scatter_chips")(*cs)


def _sum_chips(name, cs, r, place):
    hr = r.shape[1]
    tr = _row_tile(hr)
    nblk = hr // tr

    def kern(p_ref, c_ref, r0, r1, r2, r3, o_ref):
        me = p_ref[1]
        acc = None
        for k, rk in enumerate((r0, r1, r2, r3)):
            val = jnp.where(me == k, c_ref[0].astype(F32), rk[0].astype(F32))
            acc = val if acc is None else acc + val
        o_ref[...] = acc

    blk = (1, tr, r.shape[2])

    def slot(k):
        return lambda i, p: (jnp.where(p[1] == k, (k + 1) % 4, k), i, 0)

    return pl.pallas_call(
        kern,
        grid_spec=pltpu.PrefetchScalarGridSpec(
            num_scalar_prefetch=1, grid=(nblk,),
            in_specs=[pl.BlockSpec(blk, lambda i, p: (p[1], i, 0))] + [pl.BlockSpec(blk, slot(k)) for k in range(4)],
            out_specs=pl.BlockSpec((tr, r.shape[2]), lambda i, p: (p[0] * nblk + i, 0))),
        out_shape=jax.ShapeDtypeStruct((2 * hr, r.shape[2]), F32),
        compiler_params=_cparams(("parallel",)), name=name)(place, cs, r, r, r, r)


def _join_halves(fs):
    n = len(fs)

    def body(*refs):
        ins, outs = refs[:n], refs[n:2 * n]
        copy = _remote(refs[2 * n], refs[2 * n + 1])
        x, y, c, _ = _place()
        cps = []
        for i in range(n):
            hr = ins[i].shape[0] // 2
            cps.append(copy(i, ins[i].at[pl.ds(c * hr, hr), :], outs[i].at[pl.ds(c * hr, hr), :], (x, y, 1 - c)))
            cps[-1].start()
        for i in range(n):
            hr = ins[i].shape[0] // 2
            other = outs[i].at[pl.ds((1 - c) * hr, hr), :]
            copy(i, other, other, (x, y, 1 - c)).wait_recv()
        for cp in cps:
            cp.wait_send()

    return pl.pallas_call(
        body, out_shape=[jax.ShapeDtypeStruct(f.shape, f.dtype) for f in fs], in_specs=[ANY] * n,
        out_specs=[ANY] * n, input_output_aliases={i: i for i in range(n)}, scratch_shapes=_sems(n),
        name="grad_join_halves")(*fs)


def _half_rows(n, hc):
    return pl.ds(hc * (n // 2), n // 2)


def _gather_ici_side(shards):
    n = len(shards)

    def copies(ins, outs, copy):
        x, y, c, chips = _place()
        me = 2 * x + y
        for i in range(n):
            rows = _half_rows(ins[i].shape[0], c)
            for j, (px, py) in enumerate(chips):
                yield (copy(3 * i + j, ins[i].at[rows, :], outs[i].at[me, rows, :], (px, py, c)),
                       outs[i].at[2 * px + py, rows, :])

    def start(ins, outs, copy):
        for cp, _ in copies(ins, outs, copy):
            cp.start()

    def finish(ins, outs, copy):
        x, y, c, chips = _place()
        k = 0
        for cp, landing in copies(ins, outs, copy):
            copy(k, landing, landing, (x, y, c)).wait_recv()
            k += 1
        for cp, _ in copies(ins, outs, copy):
            cp.wait_send()

    return dict(ins=list(shards), outs=[jax.ShapeDtypeStruct((4,) + s.shape, s.dtype) for s in shards], nsem=3 * n,
                start=start, finish=finish)


def _gather_d2d_side(shards, bufs):
    n = len(shards)

    def copies(ins, outs, copy):
        x, y, c, chips = _place()
        me = 2 * x + y
        sibling = (x, y, 1 - c)
        for i in range(n):
            a = ins[i].shape[0]
            yield copy(4 * i + 3, ins[i], outs[i].at[me], sibling), outs[i].at[me]
            for j, (px, py) in enumerate(chips):
                src = 2 * px + py
                mine = outs[i].at[src, _half_rows(a, c), :]
                yield copy(4 * i + j, mine, mine, sibling), outs[i].at[src, _half_rows(a, 1 - c), :]

    def start(ins, outs, copy):
        for cp, _ in copies(ins, outs, copy):
            cp.start()

    def finish(ins, outs, copy):
        x, y, c, _ = _place()
        for i in range(n):
            for k, (cp, landing) in enumerate(list(copies(ins, outs, copy))[4 * i:4 * i + 4]):
                sem = 4 * i + 3 if k == 0 else 4 * i + k - 1
                copy(sem, landing, landing, (x, y, 1 - c)).wait_recv()
        for cp, _ in copies(ins, outs, copy):
            cp.wait_send()

    return dict(ins=list(shards) + list(bufs), outs=[jax.ShapeDtypeStruct(b.shape, b.dtype) for b in bufs],
                nsem=4 * n, alias={n + i: i for i in range(n)}, start=start, finish=finish)


def _scatter_side(cs):
    n = len(cs)

    def copies(ins, outs, copy):
        x, y, c, chips = _place()
        me = 2 * x + y
        for i in range(n):
            for j, (px, py) in enumerate(chips):
                yield copy(3 * i + j, ins[i].at[2 * px + py], outs[i].at[me], (px, py, c)), outs[i].at[2 * px + py]

    def start(ins, outs, copy):
        for cp, _ in copies(ins, outs, copy):
            cp.start()

    def finish(ins, outs, copy):
        x, y, c, _ = _place()
        for k, (cp, landing) in enumerate(copies(ins, outs, copy)):
            copy(k, landing, landing, (x, y, c)).wait_recv()
        for cp, _ in copies(ins, outs, copy):
            cp.wait_send()

    return dict(ins=list(cs), outs=[jax.ShapeDtypeStruct(a.shape, a.dtype) for a in cs], nsem=3 * n,
                start=start, finish=finish)


def _mod_forward(cc, w_shard, b_shard):
    cc_all = _allreduce_small(cc, "cond_gather", reduce=False).reshape(64, D)
    part = _mm("mod_fwd", cc_all, w_shard, a_fn=_silu, bias=b_shard)
    got = _mod_rows_exchange(part)
    return jnp.concatenate([got[j] for j in range(4)], axis=1), cc_all


def _mod_backward(dmod, cc_all, w_shard, ctx_row):
    w = w_shard.shape[1]
    blocks = _dmod_exchange(jnp.transpose(dmod.reshape(8, 4, w), (1, 0, 2))).reshape(64, w)
    dsc = _mm("mod_bproj", blocks, w_shard, trans_b=True)

    def tail(cc_ref, dsc_ref, dm_ref, s_ref, dctx_ref, db_ref):
        cc_ = cc_ref[...]
        s = jax.nn.sigmoid(cc_)
        s_ref[...] = (cc_ * s).astype(BF16)
        is_ctx = ((lax.broadcasted_iota(jnp.int32, (64, 1), 0) & 7) == ctx_row).astype(F32)
        dctx_ref[...] = 0.5 * jnp.sum(dsc_ref[...] * (s * (1 + cc_ * (1 - s))) * is_ctx, axis=0, keepdims=True)
        db_ref[...] = jnp.sum(dm_ref[...], axis=0, keepdims=True)

    s_all, dctx, db_mod = pl.pallas_call(
        tail, out_shape=[jax.ShapeDtypeStruct((64, D), BF16), jax.ShapeDtypeStruct((1, D), F32),
                         jax.ShapeDtypeStruct((1, dmod.shape[1]), F32)], name="mod_tail")(cc_all, dsc, dmod)
    return _mm_tn("mod_wgrad", s_all, blocks), dctx[0], db_mod


def _small_pack(dw, af, ab):
    return jnp.concatenate([dw, jnp.zeros((1, dw.shape[1]), F32), jnp.concatenate([af, ab], axis=1)], axis=0)


def _grad_pieces(n, grads):
    if n == "small":
        return jnp.stack([_small_pack(grads["dw_weight"][:, 256 * j:256 * (j + 1)],
                                      grads["w_alpha_f"][:, DK * j:DK * (j + 1)],
                                      grads["w_alpha_b"][:, DK * j:DK * (j + 1)]) for j in range(4)])
    if n == "w_in":
        g = _unperm_in_cols(grads["w_in_p"])
        return jnp.transpose(g.reshape(D, 4, D_IN // 4), (1, 0, 2))
    g = grads[n]
    return g if g.ndim == 3 else g.reshape(4, g.shape[0] // 4, g.shape[1])


def _chip_sums(tag, names, grads, place):
    gs = [_grad_pieces(n, grads) for n in names]
    swapped = _swap_halves("grad_swap_" + tag, gs)
    return [_add_halves("grad_add_" + n, g, r, place) for n, g, r in zip(names, gs, swapped)]


class _Overlap:
    SCATTER = {"in_bproj": ("w2_down", "w2_gu", "w_out"),
               "ffn1_wgu": ("w_conv_out", "w_gla_out", "w_in", "small"),
               "ffn1_bgu": ("w1_down", "w1_gu")}
    GATHER = {("ffn1_gu", "ffn1_down"): ("w_in", "w_conv_out", "w_gla_out", "w_out"),
              ("in_proj", "gla_proj"): ("w2_gu", "w2_down")}
    LATE = tuple(n for names in GATHER.values() for n in names)

    def __init__(self, shard_of, install, place):
        self.shard_of, self.install, self.place = shard_of, install, place
        self.bufs, self.pending, self.landed = {}, None, {}

    def side(self, tag, grads):
        for (ici, d2d), names in self.GATHER.items():
            shards = [self.shard_of(n) for n in names]
            if tag == ici:
                return _gather_ici_side(shards)
            if tag == d2d:
                return _gather_d2d_side(shards, self.bufs[ici])
        if tag in self.SCATTER:
            names = self.SCATTER[tag]
            self.pending = (names, _chip_sums(tag, names, grads, self.place))
            return _scatter_side(self.pending[1])
        return None

    def done(self, tag, outs):
        for (ici, d2d), names in self.GATHER.items():
            if tag == ici:
                self.bufs[ici] = outs
                return
            if tag == d2d:
                self.install(dict(zip(names, outs)))
                return
        for n, cs, r in zip(*self.pending, outs):
            self.landed[n] = (cs, r)


def _with_side(comm, tag, grads, call):
    side = comm.side(tag, grads) if comm is not None else None
    if side is None:
        return call(None)
    res, outs = call(side)
    comm.done(tag, outs)
    return res


def _allreduce_small(v, name="allreduce_small", reduce=True):
    def body(x_ref, out_ref, *scratch):
        gath = out_ref if not reduce else scratch[0]
        send_sems, recv_sems, local_sem = scratch[-3:]
        x, y, c, chips = _place()
        me, sibling = (x, y, c), (x, y, 1 - c)

        def slot(px, py, pc):
            return gath.at[4 * px + 2 * py + pc]

        def copy(k, block, to, src=None):
            return pltpu.make_async_remote_copy(
                src_ref=slot(*block) if src is None else src, dst_ref=slot(*block), send_sem=send_sems.at[k],
                recv_sem=recv_sems.at[k], device_id=to, device_id_type=MESH)

        mine = pltpu.make_async_copy(x_ref, slot(*me), local_sem)
        mine.start()
        first = [copy(0, me, sibling, src=x_ref)]
        first += [copy(1 + j, me, (*chip, c), src=x_ref) for j, chip in enumerate(chips)]
        for cp in first:
            cp.start()
        passed = [copy(4 + j, (*chip, c), sibling) for j, chip in enumerate(chips)]
        for j, chip in enumerate(chips):
            copy(1 + j, (*chip, c), me).wait_recv()
            passed[j].start()
        copy(0, sibling, me).wait_recv()
        for j, chip in enumerate(chips):
            copy(4 + j, (*chip, 1 - c), me).wait_recv()
        for cp in first + passed:
            cp.wait_send()
        mine.wait()
        if reduce:
            acc = gath[0]
            for k in range(1, 8):
                acc = acc + gath[k]
            out_ref[...] = acc

    vm = pl.BlockSpec(memory_space=pltpu.VMEM)
    sems = [pltpu.SemaphoreType.DMA((7,)), pltpu.SemaphoreType.DMA((7,)), pltpu.SemaphoreType.DMA(())]
    return pl.pallas_call(
        body, out_shape=jax.ShapeDtypeStruct(v.shape if reduce else (8,) + v.shape, F32), in_specs=[vm], out_specs=vm,
        scratch_shapes=([pltpu.VMEM((8,) + v.shape, F32)] if reduce else []) + sems, name=name)(v)


def _mod_rows_exchange(part):
    w = part.shape[1]

    def body(p_ref, out_ref, send_sems, recv_sems):
        x, y, c, chips = _place()
        me = 2 * x + y
        copy = _remote(send_sems, recv_sems)
        sends = []
        for j, (px, py) in enumerate(chips):
            rows = pl.ds(pl.multiple_of(8 * (4 * px + 2 * py + c), 8), 8)
            sends.append(copy(j, p_ref.at[rows, :], out_ref.at[me], (px, py, c)))
            sends[-1].start()
        out_ref[me] = p_ref[pl.ds(pl.multiple_of(8 * (4 * x + 2 * y + c), 8), 8), :]
        for j, (px, py) in enumerate(chips):
            landing = out_ref.at[2 * px + py]
            copy(j, landing, landing, (px, py, c)).wait_recv()
        for cp in sends:
            cp.wait_send()

    vm = pl.BlockSpec(memory_space=pltpu.VMEM)
    return pl.pallas_call(body, out_shape=jax.ShapeDtypeStruct((4, 8, w), F32), in_specs=[vm], out_specs=vm,
                          scratch_shapes=_sems(3), name="mod_rows_exchange")(part)


def _dmod_exchange(dm):
    w = dm.shape[2]

    def body(d_ref, out_ref, send_sems, recv_sems):
        x, y, c, _ = _place()
        copy = _remote(send_sems, recv_sems)
        mine = 4 * x + 2 * y + c
        sends = []
        for r in range(1, 8):
            tx, ty, tc = x ^ (r >> 2), y ^ ((r >> 1) & 1), c ^ (r & 1)
            sends.append(copy(r - 1, d_ref.at[2 * tx + ty], out_ref.at[mine], (tx, ty, tc)))
            sends[-1].start()
        out_ref[mine] = d_ref[2 * x + y]
        for r in range(1, 8):
            tx, ty, tc = x ^ (r >> 2), y ^ ((r >> 1) & 1), c ^ (r & 1)
            landing = out_ref.at[4 * tx + 2 * ty + tc]
            copy(r - 1, landing, landing, (tx, ty, tc)).wait_recv()
        for cp in sends:
            cp.wait_send()

    vm = pl.BlockSpec(memory_space=pltpu.VMEM)
    return pl.pallas_call(body, out_shape=jax.ShapeDtypeStruct((8, 8, w), F32), in_specs=[vm], out_specs=vm,
                          scratch_shapes=_sems(7), name="dmod_exchange")(dm)


def _adamw(name, w, g, m, v):
    r, cols = w.shape
    budget = 262144
    tr = r if r * cols <= budget else next(c for c in (256, 128, 64, 32, 16, 8) if r % c == 0 and c * cols <= budget)

    def kern(w_ref, g_ref, m_ref, v_ref, go_ref, d_ref, nm_ref, nv_ref):
        gv = g_ref[...]
        go_ref[...] = gv
        nm = ADAM_B1 * m_ref[...] + (1.0 - ADAM_B1) * gv
        nv = ADAM_B2 * v_ref[...] + (1.0 - ADAM_B2) * jnp.square(gv)
        m_hat = nm / (1.0 - ADAM_B1 ** ADAM_STEP)
        v_hat = nv / (1.0 - ADAM_B2 ** ADAM_STEP)
        d_ref[...] = -ADAM_LR * (m_hat / (jnp.sqrt(v_hat) + ADAM_EPS) + ADAM_WD * w_ref[...])
        nm_ref[...] = nm
        nv_ref[...] = nv

    spec = pl.BlockSpec((tr, cols), lambda i: (i, 0))
    shp = jax.ShapeDtypeStruct((r, cols), F32)
    return pl.pallas_call(kern, grid=(r // tr,), in_specs=[spec] * 4, out_specs=[spec] * 4, out_shape=[shp] * 4,
                          compiler_params=_cparams(("parallel",)), name=name)(w, g, m, v)


SHARDED = (("w_mod", 1), ("w1_gu", 1), ("w1_down", 0), ("w_in", 1), ("dw_weight", 1), ("w_conv_out", 0),
           ("w_alpha_f", 1), ("w_alpha_b", 1), ("w_gla_out", 0), ("w_out", 0), ("w2_gu", 1), ("w2_down", 0))
REPLICATED = ("c_ctx", "b_mod", "g_ffn1", "g_mix", "dw_bias", "conv_ln_g", "conv_ln_b", "b_alpha_f", "b_alpha_b",
              "gla_norm_g", "g_ffn2", "g_final")
WEIGHTS = ("c_ctx", "w_mod", "b_mod", "g_ffn1", "w1_gu", "w1_down", "g_mix", "w_in", "dw_weight", "dw_bias",
           "conv_ln_g", "conv_ln_b", "w_conv_out", "w_alpha_f", "b_alpha_f", "w_alpha_b", "b_alpha_b", "gla_norm_g",
           "w_gla_out", "w_out", "g_ffn2", "w2_gu", "w2_down", "g_final")
MATRICES = ("w1_gu", "w1_down", "w_in", "w_conv_out", "w_gla_out", "w_out", "w2_gu", "w2_down")


def _pack_flat(parts, align):
    flat = jnp.concatenate([p.reshape(-1) for p in parts])
    pad = (-flat.shape[0]) % align
    return jnp.concatenate([flat, jnp.zeros((pad,), flat.dtype)]).reshape(-1, 1024)


def _unpack_flat(flat2d, shapes):
    flat = flat2d.reshape(-1)
    out, off = [], 0
    for s in shapes:
        n = math.prod(s)
        out.append(flat[off:off + n].reshape(s))
        off += n
    return out


def kernel(x, c, ctx, c_ctx, w_mod, b_mod, g_ffn1, w1_gu, w1_down, g_mix, w_in, dw_weight, dw_bias, conv_ln_g, conv_ln_b, w_conv_out, w_alpha_f, b_alpha_f, w_alpha_b, b_alpha_b, gla_norm_g, w_gla_out, w_out, g_ffn2, w2_gu, w2_down, g_final, loss_target, m_c_ctx, m_w_mod, m_b_mod, m_g_ffn1, m_w1_gu, m_w1_down, m_g_mix, m_w_in, m_dw_weight, m_dw_bias, m_conv_ln_g, m_conv_ln_b, m_w_conv_out, m_w_alpha_f, m_b_alpha_f, m_w_alpha_b, m_b_alpha_b, m_gla_norm_g, m_w_gla_out, m_w_out, m_g_ffn2, m_w2_gu, m_w2_down, m_g_final, v_c_ctx, v_w_mod, v_b_mod, v_g_ffn1, v_w1_gu, v_w1_down, v_g_mix, v_w_in, v_dw_weight, v_dw_bias, v_conv_ln_g, v_conv_ln_b, v_w_conv_out, v_w_alpha_f, v_b_alpha_f, v_w_alpha_b, v_b_alpha_b, v_gla_norm_g, v_w_gla_out, v_w_out, v_g_ffn2, v_w2_gu, v_w2_down, v_g_final):
    given = dict(locals())
    w = {n: given[n] for n in WEIGHTS}
    m = {n: given["m_" + n] for n in WEIGHTS}
    v = {n: given["v_" + n] for n in WEIGHTS}

    def bf16_shard(n):
        return w[n][0].astype(BF16)

    def install(wts, got):
        for n in ("w1_gu", "w2_gu"):
            if n in got:
                wts[n] = got[n]
        for n in ("w1_down", "w2_down", "w_conv_out", "w_gla_out", "w_out"):
            if n in got:
                wts[n] = got[n].reshape(-1, D)
        if "w_in" in got:
            wts["w_in_p"] = _perm_in_cols(jnp.concatenate([got["w_in"][j] for j in range(4)], axis=1))

    early = tuple(n for n in MATRICES if n not in _Overlap.LATE)
    shards =[bf16_shard(n) for n in early] + [_small_pack(w["dw_weight"][0], w["w_alpha_f"][0], w["w_alpha_b"][0])]
    got = dict(zip(early + ("small",), _gather_weights(shards)))
    wts = {n: w[n] for n in REPLICATED}
    install(wts, got)
    chip = 2 * lax.axis_index("x") + lax.axis_index("y")
    mod_cols = w["w_mod"].shape[2]
    wts["w_mod_shard"] = bf16_shard("w_mod")
    wts["b_mod_shard"] = lax.dynamic_slice(w["b_mod"], (0, chip * mod_cols), (1, mod_cols))
    sm = got["small"]
    wts["dw_weight"] = jnp.concatenate([sm[j, :CONV_W] for j in range(4)], axis=1)
    zpad = jnp.zeros((128, HEADS * DK), BF16)
    w_af = jnp.concatenate([sm[j, 32:32 + LOWRANK, :DK] for j in range(4)], axis=1)
    w_ab = jnp.concatenate([sm[j, 32:32 + LOWRANK, DK:] for j in range(4)], axis=1)
    wts["w_alpha_f_pad"] = zpad.at[0:LOWRANK].set(w_af.astype(BF16))
    wts["w_alpha_b_pad"] = zpad.at[LOWRANK:2 * LOWRANK].set(w_ab.astype(BF16))

    place = jnp.stack([lax.axis_index("c"), 2 * lax.axis_index("x") + lax.axis_index("y")]).astype(jnp.int32)
    comm = _Overlap(bf16_shard, lambda got_late: install(wts, got_late), place)
    loss, grad_x, grads = _local_step(x, c, ctx, loss_target, wts, comm)
    loss = lax.psum(loss, ("x", "y", "c"))

    tags = MATRICES + ("small",)
    rest = tuple(n for n in tags if n not in comm.landed)
    if rest:
        rest_sums = _chip_sums("rest", rest, grads, place)
        for n, cs, r in zip(rest, rest_sums, _scatter_chips(rest_sums)):
            comm.landed[n] = (cs, r)
    halves = [_sum_chips("grad_sum_" + t, *comm.landed[t], place) for t in tags]
    reduced = dict(zip(tags, _join_halves(halves)))
    g_shard = {n: reduced[n] for n in MATRICES}
    g_shard["w_mod"] = grads["w_mod"]
    g_shard["dw_weight"] = reduced["small"][:CONV_W]
    g_shard["w_alpha_f"] = reduced["small"][32:32 + LOWRANK, :DK]
    g_shard["w_alpha_b"] = reduced["small"][32:32 + LOWRANK, DK:]

    rep_shapes = [w[n].shape for n in REPLICATED]
    small = _allreduce_small(_pack_flat([grads[n].reshape(w[n].shape) for n in REPLICATED], 8 * 1024))

    g_out, d_out, m_out, v_out = {}, {}, {}, {}
    for n, _ in SHARDED:
        s2 = w[n].shape[1:]
        go, d, nm, nv = _adamw("adamw_" + n, w[n].reshape(s2), g_shard[n], m[n].reshape(s2), v[n].reshape(s2))
        g_out[n], d_out[n] = go.reshape(w[n].shape), d.reshape(w[n].shape)
        m_out[n], v_out[n] = nm.reshape(w[n].shape), nv.reshape(w[n].shape)
    pk = lambda t: _pack_flat([t[n] for n in REPLICATED], 8 * 1024)
    go, d, nm, nv = _adamw("adamw_vectors", pk(w), small, pk(m), pk(v))
    for n, gg, dd, mm, vv in zip(REPLICATED, _unpack_flat(go, rep_shapes), _unpack_flat(d, rep_shapes),
                                 _unpack_flat(nm, rep_shapes), _unpack_flat(nv, rep_shapes)):
        g_out[n], d_out[n], m_out[n], v_out[n] = gg, dd, mm, vv

    return (loss, grad_x, *[g_out[n] for n in WEIGHTS], *[d_out[n] for n in WEIGHTS],
            *[m_out[n] for n in WEIGHTS], *[v_out[n] for n in WEIGHTS])
```

```python
import functools
import math

import jax
import jax.numpy as jnp
from jax import lax
from jax.experimental import pallas as pl
from jax.experimental.pallas import tpu as pltpu

F32, BF16 = jnp.float32, jnp.bfloat16
MESH = pl.DeviceIdType.MESH
HIGHEST = lax.Precision.HIGHEST

D = 1024
FF = 2816
HEADS, DK, DV = 4, 128, 256
LOWRANK = 16
CONV_W = 31
CHUNK = 64
TAU = 16.0
EPS = 1e-6
Q_SCALE = DK ** -0.5
TM = 256
D_IN = 7200
D_INP = 7296
LR_COL = 7168
OG_CB, GA_CB, GB_CB = 6, 4, 5
VMEM_LIMIT = 52 * 1024 * 1024
WGRAD_VMEM = 40 * 1024 * 1024

ADAM_LR, ADAM_B1, ADAM_B2, ADAM_EPS, ADAM_WD, ADAM_STEP = 0.001, 0.9, 0.999, 1e-08, 0.01, 10


def _silu(x):
    return x * jax.nn.sigmoid(x)


def _rms(h, g):
    return h * lax.rsqrt(jnp.mean(h * h, axis=-1, keepdims=True) + EPS) * g


def _modnorm(x, g, shift, scale):
    return _rms(x, g) * (1 + scale) + shift


def _cparams(sem=None):
    return pltpu.CompilerParams(dimension_semantics=sem, vmem_limit_bytes=VMEM_LIMIT)


def _twice(v):
    return v, v


def tok(arr, width=None, cb=0, clamp=None):
    return ("tok", arr, arr.shape[1] if width is None else width, cb, clamp)


def mod(arr):
    return ("mod", arr)


def const(arr):
    return ("const", arr)


def _rowmap(name, body, n_tiles, ins, outs, *, tpb, nb):
    def modrow(i):
        return jnp.minimum(i // tpb, nb)

    in_specs, args = [], []
    for spec in ins:
        if spec[0] == "tok":
            _, arr, width, cb, clamp = spec
            if clamp is None:
                im = lambda i, cb=cb: (i, cb)
            else:
                im = lambda i, cb=cb, clamp=clamp: (jnp.minimum(i, clamp), cb)
            in_specs.append(pl.BlockSpec((TM, width), im))
        elif spec[0] == "mod":
            arr = spec[1]
            in_specs.append(pl.BlockSpec((1, 1, arr.shape[2]), lambda i: (modrow(i), 0, 0)))
        else:
            arr = spec[1]
            in_specs.append(pl.BlockSpec(arr.shape, lambda i, nd=arr.ndim: (0,) * nd))
        args.append(arr)
    out_specs, out_shapes, aliases = [], [], {}
    for o in outs:
        if o[0] == "tok":
            _, rows, width, dtype = o
            out_shapes.append(jax.ShapeDtypeStruct((rows, width), dtype))
            out_specs.append(pl.BlockSpec((TM, width), lambda i: (i, 0)))
        elif o[0] == "cols":
            buf, width, cb = o[1:4]
            first_tile = o[4] if len(o) > 4 else 0
            if not isinstance(buf, jax.ShapeDtypeStruct):
                aliases[len(args)] = len(out_shapes)
                in_specs.append(ANY)
                args.append(buf)
            out_shapes.append(jax.ShapeDtypeStruct(buf.shape, buf.dtype))
            out_specs.append(pl.BlockSpec((TM, width), lambda i, cb=cb, t0=first_tile: (i + t0, cb)))
        elif o[0] == "tok_head":
            _, rows, width, dtype, last = o
            out_shapes.append(jax.ShapeDtypeStruct((rows, width), dtype))
            out_specs.append(pl.BlockSpec((TM, width), lambda i, last=last: (jnp.minimum(i, last), 0)))
        elif o[0] == "tokT":
            _, rows, width, dtype = o
            out_shapes.append(jax.ShapeDtypeStruct((width, rows), dtype))
            out_specs.append(pl.BlockSpec((width, TM), lambda i: (0, i)))
        elif o[0] == "acc":
            _, rows, width = o
            out_shapes.append(jax.ShapeDtypeStruct((rows, width), F32))
            out_specs.append(pl.BlockSpec((rows, width), lambda i: (0, 0)))
        else:
            width = o[1]
            rows_visited = min((n_tiles - 1) // tpb, nb) + 1
            out_shapes.append(jax.ShapeDtypeStruct((rows_visited, 1, width), F32))
            out_specs.append(pl.BlockSpec((1, 1, width), lambda i: (modrow(i), 0, 0)))
    n_in = len(ins)

    def kern(*refs):
        i = pl.program_id(0)
        vals = []
        for r, spec in zip(refs[:n_in], ins):
            val = r[0] if spec[0] == "mod" else r[...]
            vals.append(val.astype(F32) if spec[0] == "tok" and val.dtype == BF16 else val)
        res = body(i, *vals)
        for r, o, val in zip(refs[len(args):], outs, res):
            if o[0] in ("tok", "cols"):
                r[...] = val.astype(r.dtype)
            elif o[0] == "tok_head":
                @pl.when(i <= o[4])
                def _():
                    r[...] = val.astype(r.dtype)
            elif o[0] == "tokT":
                r[...] = val.T.astype(r.dtype)
            elif o[0] == "acc":
                @pl.when(i == 0)
                def _():
                    r[...] = jnp.zeros(r.shape, F32)
                r[...] += jnp.broadcast_to(val, r.shape)
            else:
                first = jnp.logical_or(i == 0, modrow(i) != modrow(jnp.maximum(i - 1, 0)))

                @pl.when(first)
                def _():
                    r[...] = jnp.zeros(r.shape, F32)
                r[0] += val

    return pl.pallas_call(
        kern, grid=(n_tiles,), in_specs=in_specs, out_specs=out_specs, out_shape=out_shapes,
        input_output_aliases=aliases, compiler_params=_cparams(("arbitrary",)), name=name)(*args)


def _pick(n, cands):
    for c in cands:
        if n % c == 0:
            return c
    return n


def _pallas(kern, *, grid, in_specs, out_specs, out_shape, scratch_shapes, sem, name, args, side=None):
    if side is None:
        return pl.pallas_call(kern, grid=grid, in_specs=in_specs, out_specs=out_specs, out_shape=out_shape,
                              scratch_shapes=scratch_shapes, compiler_params=_cparams(sem), name=name)(*args)
    single = not isinstance(out_shape, (list, tuple))
    shapes = [out_shape] if single else list(out_shape)
    ospecs = [out_specs] if single else list(out_specs)
    n_in, n_out, n_scr = len(in_specs), len(shapes), len(scratch_shapes)
    s_in, s_out = list(side["ins"]), list(side["outs"])

    def wrapped(*refs):
        pos = [0]

        def take(n):
            pos[0] += n
            return refs[pos[0] - n:pos[0]]
        ins, sins, outs, souts, scr, sems = take(n_in), take(len(s_in)), take(n_out), take(len(s_out)), take(n_scr), take(2)
        ids = [pl.program_id(k) for k in range(len(grid))]
        first = functools.reduce(jnp.logical_and, [i == 0 for i in ids])
        last = functools.reduce(jnp.logical_and, [i == g - 1 for i, g in zip(ids, grid)])
        copy = _remote(*sems)

        @pl.when(first)
        def _():
            side["start"](sins, souts, copy)
        kern(*ins, *outs, *scr)

        @pl.when(last)
        def _():
            side["finish"](sins, souts, copy)

    res = pl.pallas_call(
        wrapped, grid=grid, in_specs=list(in_specs) + [ANY] * len(s_in), out_specs=ospecs + [ANY] * len(s_out),
        out_shape=shapes + s_out, scratch_shapes=list(scratch_shapes) + _sems(side["nsem"]),
        input_output_aliases={n_in + a: n_out + b for a, b in side.get("alias", {}).items()},
        compiler_params=_cparams(("arbitrary",) * len(grid)), name=name)(*args, *s_in)
    main = res[:n_out]
    return (main[0] if single else main), list(res[n_out:])


def _mm(name, a, b, *, trans_b=False, out_dtype=F32, a_fn=None, bias=None, rows=None, side=None, residual=None):
    m, k = a.shape if a.ndim == 2 else (a.shape[1], 2 * a.shape[2])
    m = m if rows is None else rows
    shard = b.shape[2] if b.ndim == 3 else None
    if trans_b:
        n = b.shape[-2]
        tk = _pick(shard, (2816, 2304, 1408, 1024)) if shard else (
            k if k <= 2816 else _pick(k, (2816, 2432, 2304, 2048, 1536, 1408, 1024, 512, 256, 128)))
        tn = _pick(n, (1024, 512, 384, 256, 128))
    else:
        n = 4 * shard if shard else b.shape[1]
        tk = k if k <= 2816 else _pick(k, (2816, 2432, 2304, 2048, 1536, 1408, 1024, 512, 256, 128))
        tn = _pick(shard, (512, 384, 1408, 256, 128)) if shard else _pick(n, (1024, 2432, 512, 384, 256, 128))
    if residual is not None:
        tm = next(c for c in (512, 256) if m % c == 0 and (residual["tpb"] * TM) % c == 0)
    else:
        tm = _pick(m, (1024, 512, 256))
    nk = k // tk
    per = shard // (tk if trans_b else tn) if shard else None
    dims = (((1,), (1,)), ((), ())) if trans_b else (((1,), (0,)), ((), ()))

    def kern(*refs):
        a_ref, b_ref = refs[0], refs[1]
        bias_ref = refs[2] if bias is not None else None
        acc_ref = refs[-1]
        if residual is not None:
            x_ref, gate_ref, o_ref, xo_ref = refs[-5:-1]
        else:
            o_ref = refs[-2]
        kk = pl.program_id(2)
        av = a_ref[...]
        if a_fn is not None:
            av = a_fn(av)
        p = lax.dot_general(av.astype(BF16), b_ref[...].astype(BF16), dims, preferred_element_type=F32)

        def finish(total):
            if bias_ref is not None:
                total = total + bias_ref[...]
            o_ref[...] = total.astype(o_ref.dtype)
            if residual is not None:
                xo_ref[...] = x_ref[...] + residual["scale"] * gate_ref[0] * total

        if nk == 1:
            finish(p)
        else:
            @pl.when(kk == 0)
            def _():
                acc_ref[...] = p

            @pl.when(kk > 0)
            def _():
                acc_ref[...] += p

            @pl.when(kk == nk - 1)
            def _():
                finish(acc_ref[...])

    if shard and trans_b:
        b_spec = pl.BlockSpec((None, tn, tk), lambda i, j, kk: (kk // per, j, kk % per))
    elif shard:
        b_spec = pl.BlockSpec((None, tk, tn), lambda i, j, kk: (j // per, kk, j % per))
    elif trans_b:
        b_spec = pl.BlockSpec((tn, tk), lambda i, j, kk: (j, kk))
    else:
        b_spec = pl.BlockSpec((tk, tn), lambda i, j, kk: (kk, j))
    if a.ndim == 3:
        pa = a.shape[2] // tk
        a_spec = pl.BlockSpec((None, tm, tk), lambda i, j, kk: (kk // pa, i, kk % pa))
    else:
        a_spec = pl.BlockSpec((tm, tk), lambda i, j, kk: (i, kk))
    in_specs = [a_spec, b_spec]
    args = [a, b]
    if bias is not None:
        in_specs.append(pl.BlockSpec((1, tn), lambda i, j, kk: (0, j)))
        args.append(bias)
    out_specs = pl.BlockSpec((tm, tn), lambda i, j, kk: (i, j))
    out_shape = jax.ShapeDtypeStruct((m, n), out_dtype)
    if residual is not None:
        tiles, nb = residual["tpb"] * TM // tm, residual["gate"].shape[0] - 1
        in_specs += [out_specs, pl.BlockSpec((1, 1, tn), lambda i, j, kk: (jnp.minimum(i // tiles, nb), 0, j))]
        args += [residual["x"], residual["gate"]]
        out_specs, out_shape = [out_specs, out_specs], [out_shape, jax.ShapeDtypeStruct((m, n), F32)]
    return _pallas(
        kern, grid=(m // tm, n // tn, nk), in_specs=in_specs, out_specs=out_specs, out_shape=out_shape,
        scratch_shapes=[pltpu.VMEM((tm, tn) if nk > 1 else (8, 128), F32)],
        sem=("parallel", "parallel", "arbitrary"), name=name, args=args, side=side)


def _mm_tn(name, x, dy):
    t, k1, n1 = x.shape[0], x.shape[1], dy.shape[1]
    tt = _pick(t, (512, 256, 128, 64, 8))
    tk1 = _pick(k1, (1024, 512, 256, 128))
    tn = _pick(n1, (512, 384, 256, 128))
    ns = t // tt

    def kern(x_ref, dy_ref, o_ref, acc_ref):
        s = pl.program_id(2)
        p = lax.dot_general(x_ref[...].astype(BF16), dy_ref[...].astype(BF16), (((0,), (0,)), ((), ())),
                            preferred_element_type=F32)

        @pl.when(s == 0)
        def _():
            acc_ref[...] = p

        @pl.when(s > 0)
        def _():
            acc_ref[...] += p

        @pl.when(s == ns - 1)
        def _():
            o_ref[...] = acc_ref[...].astype(o_ref.dtype)

    return pl.pallas_call(
        kern, grid=(k1 // tk1, n1 // tn, ns),
        in_specs=[pl.BlockSpec((tt, tk1), lambda i, j, s: (s, i)), pl.BlockSpec((tt, tn), lambda i, j, s: (s, j))],
        out_specs=pl.BlockSpec((tk1, tn), lambda i, j, s: (i, j)), out_shape=jax.ShapeDtypeStruct((k1, n1), F32),
        scratch_shapes=[pltpu.VMEM((tk1, tn), F32)],
        compiler_params=_cparams(("parallel", "parallel", "arbitrary")), name=name)(x, dy)


def _wgrad(name, xt, dy, rows=None, col_shards=False, side=None):
    k1 = xt.shape[0]
    t = xt.shape[1] if rows is None else rows
    n1 = dy.shape[1] if dy.ndim == 2 else 2 * dy.shape[2]
    tn = _pick(n1 // 4, (1408, 512, 384, 256, 128)) if col_shards else _pick(n1, (1024, 2432, 512, 384, 256, 128))

    def token_tile(tm):
        fixed = tm * tn * (4 + 4 + 2 * 2)
        return next((c for c in (2048, 1536, 1024, 512, 256, 128)
                     if t % c == 0 and fixed + 4 * c * (tm + tn) <= WGRAD_VMEM), 128)
    tm = next((c for c in (1024, 1408, 512, 256) if k1 % c == 0 and token_tile(c) >= 1024),
              _pick(k1, (1024, 1408, 512, 256)))
    tk = token_tile(tm)
    ns = t // tk
    per = n1 // 4 // tn

    def kern(x_ref, dy_ref, o_ref, acc_ref):
        s = pl.program_id(2)
        p = jnp.dot(x_ref[...], dy_ref[...], preferred_element_type=F32)

        @pl.when(s == 0)
        def _():
            acc_ref[...] = p

        @pl.when(s > 0)
        def _():
            acc_ref[...] += p

        @pl.when(s == ns - 1)
        def _():
            o_ref[...] = acc_ref[...].astype(o_ref.dtype)

    if dy.ndim == 3:
        pd = dy.shape[2] // tn
        dy_spec = pl.BlockSpec((None, tk, tn), lambda i, j, s: (j // pd, s, j % pd))
    else:
        dy_spec = pl.BlockSpec((tk, tn), lambda i, j, s: (s, j))
    if col_shards:
        out_spec = pl.BlockSpec((None, tm, tn), lambda i, j, s: (j // per, i, j % per))
        out_shape = jax.ShapeDtypeStruct((4, k1, n1 // 4), BF16)
    else:
        out_spec = pl.BlockSpec((tm, tn), lambda i, j, s: (i, j))
        out_shape = jax.ShapeDtypeStruct((k1, n1), BF16)
    return _pallas(
        kern, grid=(k1 // tm, n1 // tn, ns),
        in_specs=[pl.BlockSpec((tm, tk), lambda i, j, s: (i, s)), dy_spec],
        out_specs=out_spec, out_shape=out_shape, scratch_shapes=[pltpu.VMEM((tm, tn), F32)],
        sem=("parallel", "parallel", "arbitrary"), name=name, args=[xt, dy], side=side)


def _swiglu_fwd(name, u, w_gu, side=None):
    m = u.shape[0]
    half = w_gu.shape[2]
    tm = _pick(m, (512, 256))

    def kern(u_ref, wa_ref, wb_ref, ab_ref, hm_ref, hmt_ref):
        uv = u_ref[...]
        a = jnp.dot(uv, wa_ref[...], preferred_element_type=F32)
        b = jnp.dot(uv, wb_ref[...], preferred_element_type=F32)
        s = jax.nn.sigmoid(a)
        silu_a = a * s
        ab_ref[0] = (b * (s * (1 + a * (1 - s)))).astype(BF16)
        ab_ref[1] = silu_a.astype(BF16)
        hm = (silu_a * b).astype(BF16)
        hm_ref[...] = hm
        hmt_ref[...] = hm.T

    return _pallas(
        kern, grid=(2, m // tm),
        in_specs=[pl.BlockSpec((tm, D), lambda j, i: (i, 0)),
                  pl.BlockSpec((None, D, half), lambda j, i: (j, 0, 0)),
                  pl.BlockSpec((None, D, half), lambda j, i: (2 + j, 0, 0))],
        out_specs=[pl.BlockSpec((2, tm, half), lambda j, i: (0, i, j)),
                   pl.BlockSpec((tm, half), lambda j, i: (i, j)),
                   pl.BlockSpec((half, tm), lambda j, i: (j, i))],
        out_shape=[jax.ShapeDtypeStruct((2, m, FF), BF16), jax.ShapeDtypeStruct((m, FF), BF16),
                   jax.ShapeDtypeStruct((FF, m), BF16)],
        scratch_shapes=[], sem=("parallel", "parallel"), name=name, args=[u, w_gu, w_gu], side=side)


def _swiglu_bwd(name, df, w_down, ab):
    m = df.shape[0]
    half = FF // 2
    tm = _pick(m, (512, 256))

    def kern(df_ref, w_ref, ab_ref, o_ref):
        dh = lax.dot_general(df_ref[...], w_ref[...], (((1,), (1,)), ((), ())), preferred_element_type=F32)
        o_ref[0] = (dh * ab_ref[0].astype(F32)).astype(BF16)
        o_ref[1] = (dh * ab_ref[1].astype(F32)).astype(BF16)

    return pl.pallas_call(
        kern, grid=(2, m // tm),
        in_specs=[pl.BlockSpec((tm, D), lambda j, i: (i, 0)),
                  pl.BlockSpec((half, D), lambda j, i: (j, 0)),
                  pl.BlockSpec((2, tm, half), lambda j, i: (0, i, j))],
        out_specs=pl.BlockSpec((2, tm, half), lambda j, i: (0, i, j)),
        out_shape=jax.ShapeDtypeStruct((2, m, FF), BF16),
        compiler_params=_cparams(("parallel", "parallel")), name=name)(df, w_down, ab)


def _gla_maps(bl, t, tc):
    nx, nc = t // CHUNK, tc // CHUNK
    nxb = bl * nx

    def rowblk(d, b, n):
        c_ctx = jnp.where(d == 0, n, nc - 1 - n)
        c_x = jnp.where(d == 0, n - nc, nx - 1 - (n - nc))
        return jnp.where(n < nc, nxb + b * nc + c_ctx, b * nx + c_x)

    def xblk(d, b, n):
        n2 = jnp.maximum(n, nc)
        return b * nx + jnp.where(d == 0, n2 - nc, nx - 1 - (n2 - nc))

    return nx, nc, rowblk, xblk


def _dot01(m, x, cm):
    x1 = x.astype(BF16)
    r1 = x - x1.astype(F32)
    x2 = r1.astype(BF16)
    x3 = (r1 - x2.astype(F32)).astype(BF16)
    w = x.shape[1]
    p = lax.dot_general(m.astype(BF16), jnp.concatenate([x1, x2, x3], axis=1), (((cm,), (0,)), ((), ())),
                        preferred_element_type=F32)
    return p[:, :w] + p[:, w:2 * w] + p[:, 2 * w:]


def _gla_chunk(m, q, k, v, g, h):
    gh = g[:, h * DK:(h + 1) * DK]
    b = _dot01(m, gh, 1)
    tot = jnp.sum(gh, axis=0, keepdims=True)
    mid = b[CHUNK // 2:CHUNK // 2 + 1, :]
    qh = q[:, h * DK:(h + 1) * DK] * Q_SCALE
    kh = k[:, h * DK:(h + 1) * DK]
    vh = v[:, h * DV:(h + 1) * DV]
    return b, tot, mid, qh, kh, vh


def _dot(a, b, ca, cb):
    return lax.dot_general(a.astype(BF16), b.astype(BF16), (((ca,), (cb,)), ((), ())),
                           preferred_element_type=F32)


def _gla_fwd(p, g2, mmats, bl, t, tc):
    nx, nc, rowblk, xblk = _gla_maps(bl, t, tc)
    ns = nx + nc

    def kern(q0, k0, v0, g0, q1, k1, v1, g1, m_ref, o0, o1, ss_ref, s_ref):
        n = pl.program_id(1)

        @pl.when(n == 0)
        def _():
            s_ref[...] = jnp.zeros(s_ref.shape, F32)
        sides = ((q0, k0, v0, g0, o0), (q1, k1, v1, g1, o1))
        loaded = [(m_ref[d], q_ref[...].astype(F32), k_ref[...].astype(F32), v_ref[...].astype(F32), g_ref[...])
                  for d, (q_ref, k_ref, v_ref, g_ref, _) in enumerate(sides)]
        chains = [(d, h) for d in range(2) for h in range(HEADS)]
        base = [_gla_chunk(*loaded[d], h) for d, h in chains]
        pre = []
        for (d, h), (b, tot, mid, qh, kh, vh) in zip(chains, base):
            s0 = s_ref[d, h * DV:(h + 1) * DV, :]
            ss_ref[d, 0, 0, h * DV:(h + 1) * DV, :] = s0.astype(BF16)
            pre.append((s0, kh * jnp.exp(tot - b), qh * jnp.exp(b), qh * jnp.exp(b - mid), kh * jnp.exp(mid - b)))
        raw = [(_dot(qm, km, 1, 1), _dot(qe, s0, 1, 1), _dot(bs[5], kl, 0, 0))
               for bs, (s0, kl, qe, qm, km) in zip(base, pre)]
        for (d, h), bs, (s0, kl, qe, qm, km), (att_raw, inter, s_add) in zip(chains, base, pre, raw):
            s_ref[d, h * DV:(h + 1) * DV, :] = s0 * jnp.exp(bs[1]) + s_add
            sides[d][4][:, h * DV:(h + 1) * DV] = (inter + _dot(loaded[d][0] * att_raw, bs[5], 1, 0)).astype(BF16)

    def operands(d):
        return [pl.BlockSpec((CHUNK, 512), lambda b, n: (rowblk(d, b, n), 4)),
                pl.BlockSpec((CHUNK, 512), lambda b, n: (rowblk(d, b, n), 5)),
                pl.BlockSpec((CHUNK, 1024), lambda b, n: (rowblk(d, b, n), 3)),
                pl.BlockSpec((CHUNK, 512), lambda b, n: (rowblk(d, b, n), d))]

    o_shape = jax.ShapeDtypeStruct((bl * t, HEADS * DV), BF16)
    return pl.pallas_call(
        kern, grid=(bl, ns),
        in_specs=operands(0) + operands(1) + [pl.BlockSpec((2, CHUNK, CHUNK), lambda b, n: (0, 0, 0))],
        out_specs=[pl.BlockSpec((CHUNK, 1024), lambda b, n: (xblk(0, b, n), 0)),
                   pl.BlockSpec((CHUNK, 1024), lambda b, n: (xblk(1, b, n), 0)),
                   pl.BlockSpec((2, 1, 1, HEADS * DV, DK), lambda b, n: (0, b, n, 0, 0))],
        out_shape=[o_shape, o_shape, jax.ShapeDtypeStruct((2, bl, ns, HEADS * DV, DK), BF16)],
        scratch_shapes=[pltpu.VMEM((2, HEADS * DV, DK), F32)],
        compiler_params=_cparams(("parallel", "arbitrary")), name="gla_fwd")(p, p, p, g2, p, p, p, g2, mmats)


def _gla_bwd(p, g2, mmats, ssave, do, bl, t, tc):
    nx, nc, rowblk, xblk = _gla_maps(bl, t, tc)
    ns = nx + nc
    rev = lambda s: ns - 1 - s

    def kern(q0, k0, v0, g0, do0, q1, k1, v1, g1, do1, m_ref, ss_ref,
             dq0, dk0, dv0, dg0, dq1, dk1, dv1, dg1, ds_ref):
        step = pl.program_id(1)
        n = ns - 1 - step

        @pl.when(step == 0)
        def _():
            ds_ref[...] = jnp.zeros(ds_ref.shape, F32)
        live = (n >= nc).astype(F32)
        sides = ((q0, k0, v0, g0, do0, dq0, dk0, dv0, dg0), (q1, k1, v1, g1, do1, dq1, dk1, dv1, dg1))
        loaded = [(m_ref[d], s[0][...].astype(F32), s[1][...].astype(F32), s[2][...].astype(F32), s[3][...])
                  for d, s in enumerate(sides)]
        dovs = [s[4][...] * live for s in sides]
        chains = [(d, h) for d in range(2) for h in range(HEADS)]
        base = [_gla_chunk(*loaded[d], h) for d, h in chains]
        pre = []
        for (d, h), (b, tot, mid, qh, kh, vh) in zip(chains, base):
            eb, ebm, emb, etb = jnp.exp(b), jnp.exp(b - mid), jnp.exp(mid - b), jnp.exp(tot - b)
            pre.append(dict(
                eb=eb, ebm=ebm, emb=emb, etb=etb, etot=jnp.exp(tot), qe=qh * eb, qm=qh * ebm, km=kh * emb, kl=kh * etb,
                vh=vh, doh=dovs[d][:, h * DV:(h + 1) * DV], s0=ss_ref[d, 0, 0, h * DV:(h + 1) * DV, :].astype(F32),
                ds1=ds_ref[d, h * DV:(h + 1) * DV, :]))
        first = [dict(att=_dot(c["qm"], c["km"], 1, 1), datt=_dot(c["doh"], c["vh"], 1, 1),
                      dqe=_dot(c["doh"], c["s0"], 1, 0), ds_add=_dot(c["doh"], c["qe"], 0, 0),
                      dkl=_dot(c["vh"], c["ds1"], 1, 0), dv_s=_dot(c["kl"], c["ds1"], 1, 1)) for c in pre]
        second = []
        for (d, h), c, f in zip(chains, pre, first):
            m = loaded[d][0]
            ds_ref[d, h * DV:(h + 1) * DV, :] = c["ds1"] * c["etot"] + f["ds_add"]
            att, datt = m * f["att"], m * f["datt"]
            second.append(dict(dqm=_dot(datt, c["km"], 1, 0), dkm=_dot(datt, c["qm"], 0, 0),
                               dv_a=_dot(att, c["doh"], 0, 0)))
        for (d, h), c, f, s in zip(chains, pre, first, second):
            dq_ref, dk_ref, dv_ref, dg_ref = sides[d][5:]
            dtot = c["etot"] * jnp.sum(c["ds1"] * c["s0"], axis=0, keepdims=True) + jnp.sum(
                f["dkl"] * c["kl"], axis=0, keepdims=True)
            db = f["dqe"] * c["qe"] + s["dqm"] * c["qm"] - s["dkm"] * c["km"] - f["dkl"] * c["kl"]
            dq_ref[:, h * DK:(h + 1) * DK] = ((f["dqe"] * c["eb"] + s["dqm"] * c["ebm"]) * Q_SCALE).astype(BF16)
            dk_ref[:, h * DK:(h + 1) * DK] = (s["dkm"] * c["emb"] + f["dkl"] * c["etb"]).astype(BF16)
            dv_ref[:, h * DV:(h + 1) * DV] = (s["dv_a"] + f["dv_s"]).astype(BF16)
            dg_ref[:, h * DK:(h + 1) * DK] = _dot01(loaded[d][0], db, 0) + dtot

    nt = p.shape[0]

    def operands(d):
        return [pl.BlockSpec((CHUNK, 512), lambda b, s: (rowblk(d, b, rev(s)), 4)),
                pl.BlockSpec((CHUNK, 512), lambda b, s: (rowblk(d, b, rev(s)), 5)),
                pl.BlockSpec((CHUNK, 1024), lambda b, s: (rowblk(d, b, rev(s)), 3)),
                pl.BlockSpec((CHUNK, 512), lambda b, s: (rowblk(d, b, rev(s)), d)),
                pl.BlockSpec((CHUNK, 1024), lambda b, s: (xblk(d, b, rev(s)), 0))]

    def results(d):
        row = lambda b, s: (rowblk(d, b, rev(s)), 0)
        return [pl.BlockSpec((CHUNK, 512), row), pl.BlockSpec((CHUNK, 512), row), pl.BlockSpec((CHUNK, 1024), row),
                pl.BlockSpec((CHUNK, 512), row)]

    shapes = [jax.ShapeDtypeStruct((nt, 512), BF16), jax.ShapeDtypeStruct((nt, 512), BF16),
              jax.ShapeDtypeStruct((nt, 1024), BF16), jax.ShapeDtypeStruct((nt, 512), F32)]
    out = pl.pallas_call(
        kern, grid=(bl, ns),
        in_specs=operands(0) + operands(1) + [
            pl.BlockSpec((2, CHUNK, CHUNK), lambda b, s: (0, 0, 0)),
            pl.BlockSpec((2, 1, 1, HEADS * DV, DK), lambda b, s: (0, b, rev(s), 0, 0))],
        out_specs=results(0) + results(1), out_shape=shapes + shapes,
        scratch_shapes=[pltpu.VMEM((2, HEADS * DV, DK), F32)],
        compiler_params=_cparams(("parallel", "arbitrary")), name="gla_bwd")(
            p, p, p, g2, do, p, p, p, g2, do, mmats, ssave)
    return out[:4], out[4:]


CONV_CT = 256
CONV_PAD = 16
CONV_RC = 128
CONV_HALO = 24


def _conv_fill(zp, z_ref, t):
    zp[0:CONV_PAD, :] = jnp.zeros((CONV_PAD, CONV_CT), F32)
    zp[CONV_PAD + t:2 * CONV_PAD + t, :] = jnp.zeros((CONV_PAD, CONV_CT), F32)
    zp[CONV_PAD:CONV_PAD + t, :] = z_ref[...]


def _dwconv(name, z, w, bias, bl, t, flip):
    def kern(z_ref, w_ref, b_ref, o_ref, zp):
        _conv_fill(zp, z_ref, t)
        offs = [(CONV_W - j) if flip else (j + 1) for j in range(CONV_W)]
        for r in range(0, t, CONV_RC):
            acc = jnp.broadcast_to(b_ref[...], (CONV_RC, CONV_CT))
            for rot in range(8):
                win = zp[r + rot:r + rot + CONV_RC + CONV_HALO, :]
                for j in range(CONV_W):
                    if offs[j] % 8 == rot:
                        a = offs[j] - rot
                        acc = acc + w_ref[j:j + 1, :] * win[a:a + CONV_RC, :]
            o_ref[r:r + CONV_RC, :] = acc

    return pl.pallas_call(
        kern, grid=(bl, 1024 // CONV_CT),
        in_specs=[pl.BlockSpec((t, CONV_CT), lambda b, c: (b, c)),
                  pl.BlockSpec((32, CONV_CT), lambda b, c: (0, c)),
                  pl.BlockSpec((1, CONV_CT), lambda b, c: (0, c))],
        out_specs=pl.BlockSpec((t, CONV_CT), lambda b, c: (b, c)),
        out_shape=jax.ShapeDtypeStruct(z.shape, F32),
        scratch_shapes=[pltpu.VMEM((t + 2 * CONV_PAD, CONV_CT), F32)],
        compiler_params=_cparams(("parallel", "parallel")), name=name)(z, w, bias)


def _dwconv_wgrad(z, dzc, bl, t):
    def kern(z_ref, d_ref, dw_ref, db_ref, zp):
        b = pl.program_id(1)

        @pl.when(b == 0)
        def _():
            dw_ref[...] = jnp.zeros(dw_ref.shape, F32)
            db_ref[...] = jnp.zeros(db_ref.shape, F32)
        _conv_fill(zp, z_ref, t)
        for rot in range(8):
            taps = [j for j in range(CONV_W) if (j + 1) % 8 == rot]
            accs = [jnp.zeros((8, CONV_CT), F32) for _ in taps]
            for r in range(0, t, CONV_RC):
                d = d_ref[r:r + CONV_RC, :]
                win = zp[r + rot:r + rot + CONV_RC + CONV_HALO, :]
                for k, j in enumerate(taps):
                    a = j + 1 - rot
                    prod = d * win[a:a + CONV_RC, :]
                    accs[k] = accs[k] + jnp.sum(prod.reshape(CONV_RC // 8, 8, CONV_CT), axis=0)
            for k, j in enumerate(taps):
                dw_ref[j:j + 1, :] += jnp.sum(accs[k], axis=0, keepdims=True)
        db_ref[...] += jnp.sum(d_ref[...], axis=0, keepdims=True)

    return pl.pallas_call(
        kern, grid=(1024 // CONV_CT, bl),
        in_specs=[pl.BlockSpec((t, CONV_CT), lambda c, b: (b, c)),
                  pl.BlockSpec((t, CONV_CT), lambda c, b: (b, c))],
        out_specs=[pl.BlockSpec((32, CONV_CT), lambda c, b: (0, c)),
                   pl.BlockSpec((1, CONV_CT), lambda c, b: (0, c))],
        out_shape=[jax.ShapeDtypeStruct((32, 1024), F32), jax.ShapeDtypeStruct((1, 1024), F32)],
        scratch_shapes=[pltpu.VMEM((t + 2 * CONV_PAD, CONV_CT), F32)],
        compiler_params=_cparams(("parallel", "arbitrary")), name="dwconv_wgrad")(z, dzc)


def _ffn_fwd(tag, xin, n_tiles, g, sh, sc, gate, w_gu, w_down, tpb, nb, comm=None):
    rows = n_tiles * TM
    rm = functools.partial(_rowmap, tpb=tpb, nb=nb)
    u, ut = rm(tag + "_norm", lambda i, x, g_, sh_, sc_: _twice(_modnorm(x, g_, sh_, sc_)), n_tiles,
               [tok(xin), const(g), mod(sh), mod(sc)], [("tok", rows, D, BF16), ("tokT", rows, D, BF16)])
    ab, hm, hmt = _with_side(comm, tag + "_gu", None, lambda s: _swiglu_fwd(tag + "_gu", u, w_gu, side=s))
    res = dict(x=xin, gate=gate, scale=0.5, tpb=tpb)
    f, xout = _with_side(comm, tag + "_down", None,
                         lambda s: _mm(tag + "_down", hm, w_down, out_dtype=BF16, side=s, residual=res))
    return xout, (ut, ab, hmt, f)


def _ffn_bwd(tag, xin, saved, dxout, dx_clamp, n_tiles, g, sh, sc, gate, w_gu, w_down, tpb, nb, comm=None, grads=None,
             names=None, keep_tiles=None):
    ut, ab, hmt, f = saved
    rows = n_tiles * TM
    rm = functools.partial(_rowmap, tpb=tpb, nb=nb)

    def mask(i):
        return 1.0 if dx_clamp is None else (i <= dx_clamp).astype(F32)

    def b1(i, dx, f_, gt):
        dx = dx * mask(i)
        return (0.5 * gt * dx, jnp.sum(0.5 * f_ * dx, axis=0, keepdims=True))
    df, dgate = rm(tag + "_bres", b1, n_tiles, [tok(dxout, clamp=dx_clamp), tok(f), mod(gate)],
                   [("tok", rows, D, BF16), ("modacc", D)])
    grads[names[1]] = _with_side(comm, tag + "_wdown", grads, lambda s: _wgrad(tag + "_wdown", hmt, df, side=s))
    dab = _swiglu_bwd(tag + "_bdown", df, w_down, ab)
    grads[names[0]] = _with_side(comm, tag + "_wgu", grads,
                                 lambda s: _wgrad(tag + "_wgu", ut, dab, col_shards=True, side=s))
    du = _with_side(comm, tag + "_bgu", grads, lambda s: _mm(tag + "_bgu", dab, w_gu, trans_b=True, side=s))

    def b3(i, x, g_, sh_, sc_, du_, dx):
        _, vjp = jax.vjp(_modnorm, x, g_, sh_, sc_)
        dxn, dg, dsh, dsc = vjp(du_)
        return (dx * mask(i) + dxn, dg, dsh, dsc)
    dx_out = ("tok", rows, D, F32) if keep_tiles is None else ("tok_head", keep_tiles * TM, D, F32, keep_tiles - 1)
    dxin, dg, dsh, dsc = rm(tag + "_bnorm", b3, n_tiles,
                            [tok(xin), const(g), mod(sh), mod(sc), tok(du), tok(dxout, clamp=dx_clamp)],
                            [dx_out, ("acc", 1, D), ("modacc", D), ("modacc", D)])
    return dxin, dict(g=dg, sh=dsh, sc=dsc, gate=dgate)


def _perm_in_cols(w):
    pad = jnp.zeros(w.shape[:-1] + (D_INP - D_IN,), w.dtype)
    return jnp.concatenate([w[..., :4096], w[..., 5152:7200], w[..., 4096:5120], w[..., 5120:5152], pad], axis=-1)


def _unperm_in_cols(w):
    return jnp.concatenate([w[..., :4096], w[..., 6144:LR_COL], w[..., LR_COL:LR_COL + 32], w[..., 4096:6144]], axis=-1)


def _local_step(x, c, ctx, target, wts, comm=None):
    bl, t, _ = x.shape
    tc = ctx.shape[1]
    nx_rows, nc_rows = bl * t, bl * tc
    nt_rows = nx_rows + nc_rows
    tpb = t // TM
    nxt, ntt = nx_rows // TM, nt_rows // TM
    nb = bl
    rm = functools.partial(_rowmap, tpb=tpb, nb=nb)
    last_x = nxt - 1

    x0 = jnp.concatenate([x.reshape(nx_rows, D), ctx.reshape(nc_rows, D)], axis=0)
    tgt = target.reshape(nx_rows, D)

    cc = jnp.concatenate([c, wts["c_ctx"].reshape(1, D), jnp.zeros((8 - bl - 1, D), F32)], axis=0)
    modv, cc_all = _mod_forward(cc, wts["w_mod_shard"], wts["b_mod_shard"])
    mods = [modv[:nb + 1, k * D:(k + 1) * D].reshape(nb + 1, 1, D) for k in range(9)]

    x1, sv1 = _ffn_fwd("ffn1", x0, ntt, wts["g_ffn1"], mods[0], mods[1], mods[2], wts["w1_gu"], wts["w1_down"],
                       tpb, nb, comm)
    u2, u2t = rm("in_norm", lambda i, x_, g_, sh_, sc_: _twice(_modnorm(x_, g_, sh_, sc_)), ntt,
                 [tok(x1), const(wts["g_mix"]), mod(mods[3]), mod(mods[4])],
                 [("tok", nt_rows, D, BF16), ("tokT", nt_rows, D, BF16)])
    w_inp = wts["w_in_p"]
    p = _with_side(comm, "in_proj", None, lambda s: _mm("in_proj", u2, w_inp, out_dtype=BF16, side=s))

    waf, wab, baf, bab = wts["w_alpha_f_pad"], wts["w_alpha_b_pad"], wts["b_alpha_f"], wts["b_alpha_b"]

    def dec_fwd(i, lr, wf, wb, bf_, bb_):
        zf = _dot(lr, wf, 1, 0) + bf_
        zb = _dot(lr, wb, 1, 0) + bb_
        return (jnp.concatenate([jax.nn.log_sigmoid(zf) / TAU, jax.nn.log_sigmoid(zb) / TAU], axis=1),)
    (gfb,) = rm("decay_fwd", dec_fwd, ntt, [tok(p, 128, LR_COL // 128), const(waf), const(wab), const(baf), const(bab)],
                [("tok", nt_rows, 1024, F32)])
    g2 = gfb
    tri = jnp.tril(jnp.ones((CHUNK, CHUNK), F32))
    mmats = jnp.stack([tri, tri.T])
    *o2, ssave = _gla_fwd(p, g2, mmats, bl, t, tc)

    gn_g = wts["gla_norm_g"]

    def gla_out(of, ob, og, gn):
        o = of + ob
        parts = []
        for h in range(HEADS):
            oh = o[:, h * DV:(h + 1) * DV]
            parts.append(oh * lax.rsqrt(jnp.mean(oh * oh, axis=-1, keepdims=True) + EPS))
        return jnp.concatenate(parts, axis=1) * gn * _silu(og)
    yg_in, yg_int = rm("gla_out", lambda i, of, ob, og, gn: _twice(gla_out(of, ob, og, gn)), nxt,
                       [tok(o2[0]), tok(o2[1]), tok(p, 1024, OG_CB), const(gn_g)],
                       [("tok", nx_rows, D, BF16), ("tokT", nx_rows, D, BF16)])
    y_gla = _with_side(comm, "gla_proj", None,
                       lambda s: _mm("gla_proj", yg_in, wts["w_gla_out"], out_dtype=BF16, side=s))

    (z,) = rm("glu", lambda i, a, b: (a * jax.nn.sigmoid(b),), nxt, [tok(p, 1024, 0), tok(p, 1024, 1)],
              [("tok", nx_rows, D, F32)])
    dw_w = jnp.concatenate([wts["dw_weight"], jnp.zeros((1, D), F32)], axis=0)
    zc = _dwconv("dwconv_fwd", z, dw_w, wts["dw_bias"], bl, t, False)

    def ln_silu(zc_, g_, b_):
        mu = jnp.mean(zc_, axis=-1, keepdims=True)
        var = jnp.mean(jnp.square(zc_ - mu), axis=-1, keepdims=True)
        return _silu((zc_ - mu) * lax.rsqrt(var + EPS) * g_ + b_)
    ln_g, ln_b = wts["conv_ln_g"], wts["conv_ln_b"]
    zl, zlt = rm("conv_ln", lambda i, zc_, g_, b_: _twice(ln_silu(zc_, g_, b_)), nxt,
                 [tok(zc), const(ln_g), const(ln_b)], [("tok", nx_rows, D, BF16), ("tokT", nx_rows, D, BF16)])
    y_conv = _mm("conv_proj", zl, wts["w_conv_out"], out_dtype=BF16)

    mg, mgt = rm("merge", lambda i, ga, gb, yc, yg: _twice(jax.nn.sigmoid(ga) * yc + jax.nn.sigmoid(gb) * yg), nxt,
                 [tok(p, 1024, GA_CB), tok(p, 1024, GB_CB), tok(y_conv), tok(y_gla)],
                 [("tok", nx_rows, D, BF16), ("tokT", nx_rows, D, BF16)])
    mix, x2 = _mm("out_proj", mg, wts["w_out"], out_dtype=BF16,
                  residual=dict(x=x1, gate=mods[5], scale=1.0, tpb=tpb))

    x3, sv2 = _ffn_fwd("ffn2", x2, nxt, wts["g_ffn2"], mods[6], mods[7], mods[8], wts["w2_gu"], wts["w2_down"],
                       tpb, nb)
    g_fin = wts["g_final"].reshape(1, D)

    def head(i, x_, g_, tg):
        y, vjp = jax.vjp(_rms, x_, g_)
        diff = y - tg
        dx, dg = vjp(diff * (1.0 / D))
        loss = 0.5 * jnp.sum(jnp.mean(diff * diff, axis=-1, keepdims=True))
        return dx, dg, loss
    dx3, dg_final, loss_acc = rm("loss_head", head, nxt, [tok(x3), const(g_fin), tok(tgt)],
                                 [("tok", nx_rows, D, F32), ("acc", 1, D), ("acc", 8, 128)])
    loss = loss_acc[0, 0]

    grads = {}
    dx2, gf2 = _ffn_bwd("ffn2", x2, sv2, dx3, None, nxt, wts["g_ffn2"], mods[6], mods[7], mods[8],
                        wts["w2_gu"], wts["w2_down"], tpb, nb, comm, grads, ("w2_gu", "w2_down"))
    grads["g_ffn2"] = gf2["g"]

    dmix, dgate5 = rm("mix_bres", lambda i, dx, mx_, gt: (gt * dx, jnp.sum(mx_ * dx, axis=0, keepdims=True)), nxt,
                      [tok(dx2), tok(mix), mod(mods[5])], [("tok", nx_rows, D, BF16), ("modacc", D)])
    dmg = _mm("out_bproj", dmix, wts["w_out"], trans_b=True, out_dtype=BF16)
    grads["w_out"] = _wgrad("out_wgrad", mgt, dmix)

    def merge_bwd(i, dm, ga, gb, yc, yg):
        keep = (i <= last_x).astype(F32)
        dm = dm * keep
        sa, sb = jax.nn.sigmoid(ga), jax.nn.sigmoid(gb)
        return dm * sa, dm * sb, jnp.concatenate([dm * yc * sa * (1 - sa), dm * yg * sb * (1 - sb)], axis=1)
    cl = dict(clamp=last_x)
    dyc, dyg, dp = rm("merge_bwd", merge_bwd, ntt,
                      [tok(dmg, **cl), tok(p, 1024, GA_CB, last_x), tok(p, 1024, GB_CB, last_x), tok(y_conv, **cl),
                       tok(y_gla, **cl)],
                      [("tok", nt_rows, D, BF16), ("tok", nt_rows, D, BF16),
                       ("cols", jax.ShapeDtypeStruct((nt_rows, D_INP), BF16), 2048, 2)])

    dzl = _with_side(comm, "conv_bproj", grads, lambda s: _mm("conv_bproj", dyc, wts["w_conv_out"], trans_b=True,
                                                              rows=nx_rows, out_dtype=BF16, side=s))
    grads["w_conv_out"] = _wgrad("conv_wgrad", zlt, dyc, rows=nx_rows)

    def ln_bwd(i, zc_, g_, b_, dz_):
        _, vjp = jax.vjp(ln_silu, zc_, g_, b_)
        return vjp(dz_)
    dzc, dln_g, dln_b = rm("conv_ln_bwd", ln_bwd, nxt, [tok(zc), const(ln_g), const(ln_b), tok(dzl)],
                           [("tok", nx_rows, D, F32), ("acc", 1, D), ("acc", 1, D)])
    dz = _dwconv("dwconv_bwd", dzc, dw_w, jnp.zeros((1, D), F32), bl, t, True)
    ddw, ddb = _dwconv_wgrad(z, dzc, bl, t)
    grads.update(conv_ln_g=dln_g, conv_ln_b=dln_b, dw_weight=ddw[:CONV_W], dw_bias=ddb)

    def glu_bwd(i, dz_, a, b):
        keep = (i <= last_x).astype(F32)
        dz_ = dz_ * keep
        s = jax.nn.sigmoid(b)
        return (jnp.concatenate([dz_ * s, dz_ * a * s * (1 - s)], axis=1),)
    (dp,) = rm("glu_bwd", glu_bwd, ntt, [tok(dz, **cl), tok(p, 1024, 0, last_x), tok(p, 1024, 1, last_x)],
               [("cols", dp, 2048, 0)])

    dyg_in = _mm("gla_bproj", dyg, wts["w_gla_out"], trans_b=True, rows=nx_rows, out_dtype=BF16)
    grads["w_gla_out"] = _wgrad("gla_wgrad", yg_int, dyg, rows=nx_rows)

    def gla_out_bwd(i, of, ob, og, gn, dy):
        _, vjp = jax.vjp(gla_out, of, ob, og, gn)
        do_, _, dog_, dgn_ = vjp(dy)
        return do_, dog_, dgn_
    do, dp, dgn = rm("gla_out_bwd", gla_out_bwd, nxt,
                     [tok(o2[0]), tok(o2[1]), tok(p, 1024, OG_CB), const(gn_g), tok(dyg_in)],
                     [("tok", nx_rows, D, BF16), ("cols", dp, 1024, OG_CB), ("acc", 1, D)])
    grads["gla_norm_g"] = dgn
    (dp,) = rm("og_ctx_zero", lambda i: (jnp.zeros((TM, D), F32),), ntt - nxt, [], [("cols", dp, 1024, OG_CB, nxt)])

    dq2, dk2, dv2, dg2 = zip(*_gla_bwd(p, g2, mmats, ssave, do, bl, t, tc))

    def dec_bwd(i, lr, wf, wb, bf_, bb_, dgf, dgb_):
        zf = _dot(lr, wf, 1, 0) + bf_
        zb = _dot(lr, wb, 1, 0) + bb_
        dzf = dgf * (1 - jax.nn.sigmoid(zf)) * (1.0 / TAU)
        dzb = dgb_ * (1 - jax.nn.sigmoid(zb)) * (1.0 / TAU)
        dlr = _dot(dzf, wf, 1, 1) + _dot(dzb, wb, 1, 1)
        return (dlr, _dot(lr, dzf, 0, 0), _dot(lr, dzb, 0, 0), jnp.sum(dzf, axis=0, keepdims=True),
                jnp.sum(dzb, axis=0, keepdims=True))
    dp, dwaf, dwab, dbaf, dbab = rm(
        "decay_bwd", dec_bwd, ntt,
        [tok(p, 128, LR_COL // 128), const(waf), const(wab), const(baf), const(bab), tok(dg2[0]), tok(dg2[1])],
        [("cols", dp, 128, LR_COL // 128), ("acc", 128, 512), ("acc", 128, 512), ("acc", 1, 512), ("acc", 1, 512)])
    grads.update(w_alpha_f=dwaf[:LOWRANK], w_alpha_b=dwab[LOWRANK:2 * LOWRANK], b_alpha_f=dbaf, b_alpha_b=dbab)

    (dp,) = rm("gla_sum",
               lambda i, q0, q1, k0, k1, v0, v1: (jnp.concatenate([q0 + q1, k0 + k1, v0 + v1], axis=1),), ntt,
               [tok(dq2[0]), tok(dq2[1]), tok(dk2[0]), tok(dk2[1]), tok(dv2[0]), tok(dv2[1])],
               [("cols", dp, 2048, 1)])
    du2 = _with_side(comm, "in_bproj", grads, lambda s: _mm("in_bproj", dp, w_inp, trans_b=True, side=s))
    grads["w_in_p"] = _wgrad("in_wgrad", u2t, dp)

    def in_norm_bwd(i, x_, g_, sh_, sc_, du_, dx):
        keep = (i <= last_x).astype(F32)
        _, vjp = jax.vjp(_modnorm, x_, g_, sh_, sc_)
        dxn, dg, dsh, dsc = vjp(du_)
        return (dx * keep + dxn, dg, dsh, dsc)
    dx1, dg_mix, dsh3, dsc4 = rm("in_norm_bwd", in_norm_bwd, ntt,
                                 [tok(x1), const(wts["g_mix"]), mod(mods[3]), mod(mods[4]), tok(du2), tok(dx2, **cl)],
                                 [("tok", nt_rows, D, F32), ("acc", 1, D), ("modacc", D), ("modacc", D)])
    grads["g_mix"] = dg_mix

    dx0, gf1 = _ffn_bwd("ffn1", x0, sv1, dx1, None, ntt, wts["g_ffn1"], mods[0], mods[1], mods[2],
                        wts["w1_gu"], wts["w1_down"], tpb, nb, comm, grads, ("w1_gu", "w1_down"), keep_tiles=nxt)
    grads["g_ffn1"] = gf1["g"]
    grad_x = dx0.reshape(bl, t, D)

    dmods = [gf1["sh"], gf1["sc"], gf1["gate"], dsh3, dsc4, dgate5, gf2["sh"], gf2["sc"], gf2["gate"]]
    dmod = jnp.concatenate(
        [jnp.concatenate([a.reshape(a.shape[0], D), jnp.zeros((8 - a.shape[0], D), F32)], axis=0) for a in dmods],
        axis=1)
    grads["w_mod"], grads["c_ctx"], grads["b_mod"] = _mod_backward(dmod, cc_all, wts["w_mod_shard"], nb)
    grads["g_final"] = dg_final.reshape(D)
    return loss, grad_x, grads


ANY = pl.BlockSpec(memory_space=pl.ANY)


def _place():
    x, y, c = lax.axis_index("x"), lax.axis_index("y"), lax.axis_index("c")
    chips = [(1 - x, y), (x, 1 - y), (1 - x, 1 - y)]
    return x, y, c, chips


def _remote(send_sems, recv_sems):
    def copy(k, src, dst, to):
        return pltpu.make_async_remote_copy(src_ref=src, dst_ref=dst, send_sem=send_sems.at[k],
                                            recv_sem=recv_sems.at[k], device_id=to, device_id_type=MESH)
    return copy


def _sems(n):
    return [pltpu.SemaphoreType.DMA((n,)), pltpu.SemaphoreType.DMA((n,))]


def _gather_weights(shards):
    n = len(shards)

    def body(*refs):
        ins, outs = refs[:n], refs[n:2 * n]
        copy = _remote(refs[2 * n], refs[2 * n + 1])
        x, y, c, chips = _place()
        me = 2 * x + y
        sibling = (x, y, 1 - c)
        started = []

        def rows(i, hc):
            hr = ins[i].shape[0] // 2
            return pl.ds(hc * hr, hr)

        for i in range(n):
            started.append(copy(7 * i + 6, ins[i], outs[i].at[me], sibling))
            started[-1].start()
            for j, (px, py) in enumerate(chips):
                started.append(copy(7 * i + j, ins[i].at[rows(i, c), :], outs[i].at[me, rows(i, c), :], (px, py, c)))
                started[-1].start()
        for i in range(n):
            for j, (px, py) in enumerate(chips):
                half = outs[i].at[2 * px + py, rows(i, c), :]
                copy(7 * i + j, half, half, (px, py, c)).wait_recv()
                started.append(copy(7 * i + 3 + j, half, half, sibling))
                started[-1].start()
        for i in range(n):
            copy(7 * i + 6, ins[i], outs[i].at[me], sibling).wait_recv()
            for j, (px, py) in enumerate(chips):
                other = outs[i].at[2 * px + py, rows(i, 1 - c), :]
                copy(7 * i + 3 + j, other, other, sibling).wait_recv()
        for cp in started:
            cp.wait_send()

    return pl.pallas_call(
        body, out_shape=[jax.ShapeDtypeStruct((4,) + s.shape, s.dtype) for s in shards], in_specs=[ANY] * n,
        out_specs=[ANY] * n, scratch_shapes=_sems(7 * n), name="gather_weights")(*shards)


def _swap_halves(name, gs):
    n = len(gs)

    def body(*refs):
        ins, outs = refs[:n], refs[n:2 * n]
        copy = _remote(refs[2 * n], refs[2 * n + 1])
        x, y, c, _ = _place()
        cps = []
        for i in range(n):
            hr = ins[i].shape[1] // 2
            cps.append(copy(i, ins[i].at[:, pl.ds((1 - c) * hr, hr), :], outs[i], (x, y, 1 - c)))
            cps[-1].start()
        for cp in cps:
            cp.wait()

    return pl.pallas_call(
        body, out_shape=[jax.ShapeDtypeStruct((4, g.shape[1] // 2, g.shape[2]), g.dtype) for g in gs],
        in_specs=[ANY] * n, out_specs=[ANY] * n, scratch_shapes=_sems(n), name=name)(*gs)


def _row_tile(hr):
    return hr if hr <= 256 else _pick(hr, (256, 176, 128, 64, 32, 16))


def _add_halves(name, g, r, place):
    hr = r.shape[1]
    tr = _row_tile(hr)
    nblk = hr // tr

    def kern(p_ref, g_ref, r_ref, o_ref):
        o_ref[...] = (g_ref[...].astype(F32) + r_ref[...].astype(F32)).astype(o_ref.dtype)

    blk = (1, tr, g.shape[2])
    return pl.pallas_call(
        kern,
        grid_spec=pltpu.PrefetchScalarGridSpec(
            num_scalar_prefetch=1, grid=(4, nblk),
            in_specs=[pl.BlockSpec(blk, lambda j, i, p: (j, p[0] * nblk + i, 0)),
                      pl.BlockSpec(blk, lambda j, i, p: (j, i, 0))],
            out_specs=pl.BlockSpec(blk, lambda j, i, p: (j, i, 0))),
        out_shape=jax.ShapeDtypeStruct(r.shape, r.dtype),
        compiler_params=_cparams(("parallel", "parallel")), name=name)(place, g, r)


def _scatter_chips(cs):
    n = len(cs)

    def body(*refs):
        ins, outs = refs[:n], refs[n:2 * n]
        copy = _remote(refs[2 * n], refs[2 * n + 1])
        x, y, c, chips = _place()
        me = 2 * x + y
        sends = []
        for i in range(n):
            for j, (px, py) in enumerate(chips):
                sends.append(copy(3 * i + j, ins[i].at[2 * px + py], outs[i].at[me], (px, py, c)))
                sends[-1].start()
        for i in range(n):
            for j, (px, py) in enumerate(chips):
                src = 2 * px + py
                copy(3 * i + j, ins[i].at[src], outs[i].at[src], (px, py, c)).wait_recv()
        for cp in sends:
            cp.wait_send()

    return pl.pallas_call(
        body, out_shape=[jax.ShapeDtypeStruct(a.shape, a.dtype) for a in cs], in_specs=[ANY] * n,
        out_specs=[ANY] * n, scratch_shapes=_sems(3 * n), name="grad_scatter_chips")(*cs)


def _sum_chips(name, cs, r, place):
    hr = r.shape[1]
    tr = _row_tile(hr)
    nblk = hr // tr

    def kern(p_ref, c_ref, r0, r1, r2, r3, o_ref):
        me = p_ref[1]
        acc = None
        for k, rk in enumerate((r0, r1, r2, r3)):
            val = jnp.where(me == k, c_ref[0].astype(F32), rk[0].astype(F32))
            acc = val if acc is None else acc + val
        o_ref[...] = acc

    blk = (1, tr, r.shape[2])

    def slot(k):
        return lambda i, p: (jnp.where(p[1] == k, (k + 1) % 4, k), i, 0)

    return pl.pallas_call(
        kern,
        grid_spec=pltpu.PrefetchScalarGridSpec(
            num_scalar_prefetch=1, grid=(nblk,),
            in_specs=[pl.BlockSpec(blk, lambda i, p: (p[1], i, 0))] + [pl.BlockSpec(blk, slot(k)) for k in range(4)],
            out_specs=pl.BlockSpec((tr, r.shape[2]), lambda i, p: (p[0] * nblk + i, 0))),
        out_shape=jax.ShapeDtypeStruct((2 * hr, r.shape[2]), F32),
        compiler_params=_cparams(("parallel",)), name=name)(place, cs, r, r, r, r)


def _join_halves(fs):
    n = len(fs)

    def body(*refs):
        ins, outs = refs[:n], refs[n:2 * n]
        copy = _remote(refs[2 * n], refs[2 * n + 1])
        x, y, c, _ = _place()
        cps = []
        for i in range(n):
            hr = ins[i].shape[0] // 2
            cps.append(copy(i, ins[i].at[pl.ds(c * hr, hr), :], outs[i].at[pl.ds(c * hr, hr), :], (x, y, 1 - c)))
            cps[-1].start()
        for i in range(n):
            hr = ins[i].shape[0] // 2
            other = outs[i].at[pl.ds((1 - c) * hr, hr), :]
            copy(i, other, other, (x, y, 1 - c)).wait_recv()
        for cp in cps:
            cp.wait_send()

    return pl.pallas_call(
        body, out_shape=[jax.ShapeDtypeStruct(f.shape, f.dtype) for f in fs], in_specs=[ANY] * n,
        out_specs=[ANY] * n, input_output_aliases={i: i for i in range(n)}, scratch_shapes=_sems(n),
        name="grad_join_halves")(*fs)


def _half_rows(n, hc):
    return pl.ds(hc * (n // 2), n // 2)


def _gather_ici_side(shards):
    n = len(shards)

    def copies(ins, outs, copy):
        x, y, c, chips = _place()
        me = 2 * x + y
        for i in range(n):
            rows = _half_rows(ins[i].shape[0], c)
            for j, (px, py) in enumerate(chips):
                yield (copy(3 * i + j, ins[i].at[rows, :], outs[i].at[me, rows, :], (px, py, c)),
                       outs[i].at[2 * px + py, rows, :])

    def start(ins, outs, copy):
        for cp, _ in copies(ins, outs, copy):
            cp.start()

    def finish(ins, outs, copy):
        x, y, c, chips = _place()
        k = 0
        for cp, landing in copies(ins, outs, copy):
            copy(k, landing, landing, (x, y, c)).wait_recv()
            k += 1
        for cp, _ in copies(ins, outs, copy):
            cp.wait_send()

    return dict(ins=list(shards), outs=[jax.ShapeDtypeStruct((4,) + s.shape, s.dtype) for s in shards], nsem=3 * n,
                start=start, finish=finish)


def _gather_d2d_side(shards, bufs):
    n = len(shards)

    def copies(ins, outs, copy):
        x, y, c, chips = _place()
        me = 2 * x + y
        sibling = (x, y, 1 - c)
        for i in range(n):
            a = ins[i].shape[0]
            yield copy(4 * i + 3, ins[i], outs[i].at[me], sibling), outs[i].at[me]
            for j, (px, py) in enumerate(chips):
                src = 2 * px + py
                mine = outs[i].at[src, _half_rows(a, c), :]
                yield copy(4 * i + j, mine, mine, sibling), outs[i].at[src, _half_rows(a, 1 - c), :]

    def start(ins, outs, copy):
        for cp, _ in copies(ins, outs, copy):
            cp.start()

    def finish(ins, outs, copy):
        x, y, c, _ = _place()
        for i in range(n):
            for k, (cp, landing) in enumerate(list(copies(ins, outs, copy))[4 * i:4 * i + 4]):
                sem = 4 * i + 3 if k == 0 else 4 * i + k - 1
                copy(sem, landing, landing, (x, y, 1 - c)).wait_recv()
        for cp, _ in copies(ins, outs, copy):
            cp.wait_send()

    return dict(ins=list(shards) + list(bufs), outs=[jax.ShapeDtypeStruct(b.shape, b.dtype) for b in bufs],
                nsem=4 * n, alias={n + i: i for i in range(n)}, start=start, finish=finish)


def _scatter_side(cs):
    n = len(cs)

    def copies(ins, outs, copy):
        x, y, c, chips = _place()
        me = 2 * x + y
        for i in range(n):
            for j, (px, py) in enumerate(chips):
                yield copy(3 * i + j, ins[i].at[2 * px + py], outs[i].at[me], (px, py, c)), outs[i].at[2 * px + py]

    def start(ins, outs, copy):
        for cp, _ in copies(ins, outs, copy):
            cp.start()

    def finish(ins, outs, copy):
        x, y, c, _ = _place()
        for k, (cp, landing) in enumerate(copies(ins, outs, copy)):
            copy(k, landing, landing, (x, y, c)).wait_recv()
        for cp, _ in copies(ins, outs, copy):
            cp.wait_send()

    return dict(ins=list(cs), outs=[jax.ShapeDtypeStruct(a.shape, a.dtype) for a in cs], nsem=3 * n,
                start=start, finish=finish)


def _mod_forward(cc, w_shard, b_shard):
    cc_all = _allreduce_small(cc, "cond_gather", reduce=False).reshape(64, D)
    part = _mm("mod_fwd", cc_all, w_shard, a_fn=_silu, bias=b_shard)
    got = _mod_rows_exchange(part)
    return jnp.concatenate([got[j] for j in range(4)], axis=1), cc_all


def _mod_backward(dmod, cc_all, w_shard, ctx_row):
    w = w_shard.shape[1]
    blocks = _dmod_exchange(jnp.transpose(dmod.reshape(8, 4, w), (1, 0, 2))).reshape(64, w)
    dsc = _mm("mod_bproj", blocks, w_shard, trans_b=True)

    def tail(cc_ref, dsc_ref, dm_ref, s_ref, dctx_ref, db_ref):
        cc_ = cc_ref[...]
        s = jax.nn.sigmoid(cc_)
        s_ref[...] = (cc_ * s).astype(BF16)
        is_ctx = ((lax.broadcasted_iota(jnp.int32, (64, 1), 0) & 7) == ctx_row).astype(F32)
        dctx_ref[...] = 0.5 * jnp.sum(dsc_ref[...] * (s * (1 + cc_ * (1 - s))) * is_ctx, axis=0, keepdims=True)
        db_ref[...] = jnp.sum(dm_ref[...], axis=0, keepdims=True)

    s_all, dctx, db_mod = pl.pallas_call(
        tail, out_shape=[jax.ShapeDtypeStruct((64, D), BF16), jax.ShapeDtypeStruct((1, D), F32),
                         jax.ShapeDtypeStruct((1, dmod.shape[1]), F32)], name="mod_tail")(cc_all, dsc, dmod)
    return _mm_tn("mod_wgrad", s_all, blocks), dctx[0], db_mod


def _small_pack(dw, af, ab):
    return jnp.concatenate([dw, jnp.zeros((1, dw.shape[1]), F32), jnp.concatenate([af, ab], axis=1)], axis=0)


def _grad_pieces(n, grads):
    if n == "small":
        return jnp.stack([_small_pack(grads["dw_weight"][:, 256 * j:256 * (j + 1)],
                                      grads["w_alpha_f"][:, DK * j:DK * (j + 1)],
                                      grads["w_alpha_b"][:, DK * j:DK * (j + 1)]) for j in range(4)])
    if n == "w_in":
        g = _unperm_in_cols(grads["w_in_p"])
        return jnp.transpose(g.reshape(D, 4, D_IN // 4), (1, 0, 2))
    g = grads[n]
    return g if g.ndim == 3 else g.reshape(4, g.shape[0] // 4, g.shape[1])


def _swap_side(gs):
    n = len(gs)

    def copies(ins, outs, copy):
        x, y, c, _ = _place()
        for i in range(n):
            yield copy(i, ins[i].at[:, _half_rows(ins[i].shape[1], 1 - c), :], outs[i], (x, y, 1 - c))

    def start(ins, outs, copy):
        for cp in copies(ins, outs, copy):
            cp.start()

    def finish(ins, outs, copy):
        for cp in copies(ins, outs, copy):
            cp.wait()

    return dict(ins=list(gs), outs=[jax.ShapeDtypeStruct((4, g.shape[1] // 2, g.shape[2]), g.dtype) for g in gs],
                nsem=n, start=start, finish=finish)


def _chip_sums(tag, names, grads, place, swapped=None):
    gs = [_grad_pieces(n, grads) for n in names] if swapped is None else swapped[0]
    got = _swap_halves("grad_swap_" + tag, gs) if swapped is None else swapped[1]
    return [_add_halves("grad_add_" + n, g, r, place) for n, g, r in zip(names, gs, got)]


class _Overlap:
    SCATTER = {"in_bproj": ("w2_down", "w2_gu", "w_out"),
               "ffn1_wgu": ("w_conv_out", "w_gla_out", "w_in", "small"),
               "ffn1_bgu": ("w1_down", "w1_gu")}
    GATHER = {("ffn1_gu", "ffn1_down"): ("w_in", "w_conv_out", "w_gla_out", "w_out"),
              ("in_proj", "gla_proj"): ("w2_gu", "w2_down")}
    LATE = tuple(n for names in GATHER.values() for n in names)
    SWAP = {"conv_bproj": "in_bproj", "ffn1_wdown": "ffn1_wgu"}

    def __init__(self, shard_of, install, place):
        self.shard_of, self.install, self.place = shard_of, install, place
        self.bufs, self.pending, self.landed, self.swapped = {}, None, {}, {}

    def side(self, tag, grads):
        for (ici, d2d), names in self.GATHER.items():
            shards = [self.shard_of(n) for n in names]
            if tag == ici:
                return _gather_ici_side(shards)
            if tag == d2d:
                return _gather_d2d_side(shards, self.bufs[ici])
        if tag in self.SWAP:
            gs = [_grad_pieces(n, grads) for n in self.SCATTER[self.SWAP[tag]]]
            self.swapped[self.SWAP[tag]] = [gs, None]
            return _swap_side(gs)
        if tag in self.SCATTER:
            names = self.SCATTER[tag]
            self.pending = (names, _chip_sums(tag, names, grads, self.place, self.swapped.get(tag)))
            return _scatter_side(self.pending[1])
        return None

    def done(self, tag, outs):
        for (ici, d2d), names in self.GATHER.items():
            if tag == ici:
                self.bufs[ici] = outs
                return
            if tag == d2d:
                self.install(dict(zip(names, outs)))
                return
        if tag in self.SWAP:
            self.swapped[self.SWAP[tag]][1] = outs
            return
        for n, cs, r in zip(*self.pending, outs):
            self.landed[n] = (cs, r)


def _with_side(comm, tag, grads, call):
    side = comm.side(tag, grads) if comm is not None else None
    if side is None:
        return call(None)
    res, outs = call(side)
    comm.done(tag, outs)
    return res


def _allreduce_small(v, name="allreduce_small", reduce=True):
    def body(x_ref, out_ref, *scratch):
        gath = out_ref if not reduce else scratch[0]
        send_sems, recv_sems, local_sem = scratch[-3:]
        x, y, c, chips = _place()
        me, sibling = (x, y, c), (x, y, 1 - c)

        def slot(px, py, pc):
            return gath.at[4 * px + 2 * py + pc]

        def copy(k, block, to, src=None):
            return pltpu.make_async_remote_copy(
                src_ref=slot(*block) if src is None else src, dst_ref=slot(*block), send_sem=send_sems.at[k],
                recv_sem=recv_sems.at[k], device_id=to, device_id_type=MESH)

        mine = pltpu.make_async_copy(x_ref, slot(*me), local_sem)
        mine.start()
        first = [copy(0, me, sibling, src=x_ref)]
        first += [copy(1 + j, me, (*chip, c), src=x_ref) for j, chip in enumerate(chips)]
        for cp in first:
            cp.start()
        passed = [copy(4 + j, (*chip, c), sibling) for j, chip in enumerate(chips)]
        for j, chip in enumerate(chips):
            copy(1 + j, (*chip, c), me).wait_recv()
            passed[j].start()
        copy(0, sibling, me).wait_recv()
        for j, chip in enumerate(chips):
            copy(4 + j, (*chip, 1 - c), me).wait_recv()
        for cp in first + passed:
            cp.wait_send()
        mine.wait()
        if reduce:
            acc = gath[0]
            for k in range(1, 8):
                acc = acc + gath[k]
            out_ref[...] = acc

    vm = pl.BlockSpec(memory_space=pltpu.VMEM)
    sems = [pltpu.SemaphoreType.DMA((7,)), pltpu.SemaphoreType.DMA((7,)), pltpu.SemaphoreType.DMA(())]
    return pl.pallas_call(
        body, out_shape=jax.ShapeDtypeStruct(v.shape if reduce else (8,) + v.shape, F32), in_specs=[vm], out_specs=vm,
        scratch_shapes=([pltpu.VMEM((8,) + v.shape, F32)] if reduce else []) + sems, name=name)(v)


def _mod_rows_exchange(part):
    w = part.shape[1]

    def body(p_ref, out_ref, send_sems, recv_sems):
        x, y, c, chips = _place()
        me = 2 * x + y
        copy = _remote(send_sems, recv_sems)
        sends = []
        for j, (px, py) in enumerate(chips):
            rows = pl.ds(pl.multiple_of(8 * (4 * px + 2 * py + c), 8), 8)
            sends.append(copy(j, p_ref.at[rows, :], out_ref.at[me], (px, py, c)))
            sends[-1].start()
        out_ref[me] = p_ref[pl.ds(pl.multiple_of(8 * (4 * x + 2 * y + c), 8), 8), :]
        for j, (px, py) in enumerate(chips):
            landing = out_ref.at[2 * px + py]
            copy(j, landing, landing, (px, py, c)).wait_recv()
        for cp in sends:
            cp.wait_send()

    vm = pl.BlockSpec(memory_space=pltpu.VMEM)
    return pl.pallas_call(body, out_shape=jax.ShapeDtypeStruct((4, 8, w), F32), in_specs=[vm], out_specs=vm,
                          scratch_shapes=_sems(3), name="mod_rows_exchange")(part)


def _dmod_exchange(dm):
    w = dm.shape[2]

    def body(d_ref, out_ref, send_sems, recv_sems):
        x, y, c, _ = _place()
        copy = _remote(send_sems, recv_sems)
        mine = 4 * x + 2 * y + c
        sends = []
        for r in range(1, 8):
            tx, ty, tc = x ^ (r >> 2), y ^ ((r >> 1) & 1), c ^ (r & 1)
            sends.append(copy(r - 1, d_ref.at[2 * tx + ty], out_ref.at[mine], (tx, ty, tc)))
            sends[-1].start()
        out_ref[mine] = d_ref[2 * x + y]
        for r in range(1, 8):
            tx, ty, tc = x ^ (r >> 2), y ^ ((r >> 1) & 1), c ^ (r & 1)
            landing = out_ref.at[4 * tx + 2 * ty + tc]
            copy(r - 1, landing, landing, (tx, ty, tc)).wait_recv()
        for cp in sends:
            cp.wait_send()

    vm = pl.BlockSpec(memory_space=pltpu.VMEM)
    return pl.pallas_call(body, out_shape=jax.ShapeDtypeStruct((8, 8, w), F32), in_specs=[vm], out_specs=vm,
                          scratch_shapes=_sems(7), name="dmod_exchange")(dm)


def _adamw(name, w, g, m, v):
    r, cols = w.shape
    budget = 262144
    tr = r if r * cols <= budget else next(c for c in (256, 128, 64, 32, 16, 8) if r % c == 0 and c * cols <= budget)

    def kern(w_ref, g_ref, m_ref, v_ref, go_ref, d_ref, nm_ref, nv_ref):
        gv = g_ref[...]
        go_ref[...] = gv
        nm = ADAM_B1 * m_ref[...] + (1.0 - ADAM_B1) * gv
        nv = ADAM_B2 * v_ref[...] + (1.0 - ADAM_B2) * jnp.square(gv)
        m_hat = nm / (1.0 - ADAM_B1 ** ADAM_STEP)
        v_hat = nv / (1.0 - ADAM_B2 ** ADAM_STEP)
        d_ref[...] = -ADAM_LR * (m_hat / (jnp.sqrt(v_hat) + ADAM_EPS) + ADAM_WD * w_ref[...])
        nm_ref[...] = nm
        nv_ref[...] = nv

    spec = pl.BlockSpec((tr, cols), lambda i: (i, 0))
    shp = jax.ShapeDtypeStruct((r, cols), F32)
    return pl.pallas_call(kern, grid=(r // tr,), in_specs=[spec] * 4, out_specs=[spec] * 4, out_shape=[shp] * 4,
                          compiler_params=_cparams(("parallel",)), name=name)(w, g, m, v)


SHARDED = (("w_mod", 1), ("w1_gu", 1), ("w1_down", 0), ("w_in", 1), ("dw_weight", 1), ("w_conv_out", 0),
           ("w_alpha_f", 1), ("w_alpha_b", 1), ("w_gla_out", 0), ("w_out", 0), ("w2_gu", 1), ("w2_down", 0))
REPLICATED = ("c_ctx", "b_mod", "g_ffn1", "g_mix", "dw_bias", "conv_ln_g", "conv_ln_b", "b_alpha_f", "b_alpha_b",
              "gla_norm_g", "g_ffn2", "g_final")
WEIGHTS = ("c_ctx", "w_mod", "b_mod", "g_ffn1", "w1_gu", "w1_down", "g_mix", "w_in", "dw_weight", "dw_bias",
           "conv_ln_g", "conv_ln_b", "w_conv_out", "w_alpha_f", "b_alpha_f", "w_alpha_b", "b_alpha_b", "gla_norm_g",
           "w_gla_out", "w_out", "g_ffn2", "w2_gu", "w2_down", "g_final")
MATRICES = ("w1_gu", "w1_down", "w_in", "w_conv_out", "w_gla_out", "w_out", "w2_gu", "w2_down")


def _pack_flat(parts, align):
    flat = jnp.concatenate([p.reshape(-1) for p in parts])
    pad = (-flat.shape[0]) % align
    return jnp.concatenate([flat, jnp.zeros((pad,), flat.dtype)]).reshape(-1, 1024)


def _unpack_flat(flat2d, shapes):
    flat = flat2d.reshape(-1)
    out, off = [], 0
    for s in shapes:
        n = math.prod(s)
        out.append(flat[off:off + n].reshape(s))
        off += n
    return out


def kernel(x, c, ctx, c_ctx, w_mod, b_mod, g_ffn1, w1_gu, w1_down, g_mix, w_in, dw_weight, dw_bias, conv_ln_g, conv_ln_b, w_conv_out, w_alpha_f, b_alpha_f, w_alpha_b, b_alpha_b, gla_norm_g, w_gla_out, w_out, g_ffn2, w2_gu, w2_down, g_final, loss_target, m_c_ctx, m_w_mod, m_b_mod, m_g_ffn1, m_w1_gu, m_w1_down, m_g_mix, m_w_in, m_dw_weight, m_dw_bias, m_conv_ln_g, m_conv_ln_b, m_w_conv_out, m_w_alpha_f, m_b_alpha_f, m_w_alpha_b, m_b_alpha_b, m_gla_norm_g, m_w_gla_out, m_w_out, m_g_ffn2, m_w2_gu, m_w2_down, m_g_final, v_c_ctx, v_w_mod, v_b_mod, v_g_ffn1, v_w1_gu, v_w1_down, v_g_mix, v_w_in, v_dw_weight, v_dw_bias, v_conv_ln_g, v_conv_ln_b, v_w_conv_out, v_w_alpha_f, v_b_alpha_f, v_w_alpha_b, v_b_alpha_b, v_gla_norm_g, v_w_gla_out, v_w_out, v_g_ffn2, v_w2_gu, v_w2_down, v_g_final):
    given = dict(locals())
    w = {n: given[n] for n in WEIGHTS}
    m = {n: given["m_" + n] for n in WEIGHTS}
    v = {n: given["v_" + n] for n in WEIGHTS}

    def bf16_shard(n):
        return w[n][0].astype(BF16)

    def install(wts, got):
        for n in ("w1_gu", "w2_gu"):
            if n in got:
                wts[n] = got[n]
        for n in ("w1_down", "w2_down", "w_conv_out", "w_gla_out", "w_out"):
            if n in got:
                wts[n] = got[n].reshape(-1, D)
        if "w_in" in got:
            wts["w_in_p"] = _perm_in_cols(jnp.concatenate([got["w_in"][j] for j in range(4)], axis=1))

    early = tuple(n for n in MATRICES if n not in _Overlap.LATE)
    shards =[bf16_shard(n) for n in early] + [_small_pack(w["dw_weight"][0], w["w_alpha_f"][0], w["w_alpha_b"][0])]
    got = dict(zip(early + ("small",), _gather_weights(shards)))
    wts = {n: w[n] for n in REPLICATED}
    install(wts, got)
    chip = 2 * lax.axis_index("x") + lax.axis_index("y")
    mod_cols = w["w_mod"].shape[2]
    wts["w_mod_shard"] = bf16_shard("w_mod")
    wts["b_mod_shard"] = lax.dynamic_slice(w["b_mod"], (0, chip * mod_cols), (1, mod_cols))
    sm = got["small"]
    wts["dw_weight"] = jnp.concatenate([sm[j, :CONV_W] for j in range(4)], axis=1)
    zpad = jnp.zeros((128, HEADS * DK), BF16)
    w_af = jnp.concatenate([sm[j, 32:32 + LOWRANK, :DK] for j in range(4)], axis=1)
    w_ab = jnp.concatenate([sm[j, 32:32 + LOWRANK, DK:] for j in range(4)], axis=1)
    wts["w_alpha_f_pad"] = zpad.at[0:LOWRANK].set(w_af.astype(BF16))
    wts["w_alpha_b_pad"] = zpad.at[LOWRANK:2 * LOWRANK].set(w_ab.astype(BF16))

    place = jnp.stack([lax.axis_index("c"), 2 * lax.axis_index("x") + lax.axis_index("y")]).astype(jnp.int32)
    comm = _Overlap(bf16_shard, lambda got_late: install(wts, got_late), place)
    loss, grad_x, grads = _local_step(x, c, ctx, loss_target, wts, comm)
    loss = lax.psum(loss, ("x", "y", "c"))

    tags = MATRICES + ("small",)
    rest = tuple(n for n in tags if n not in comm.landed)
    if rest:
        rest_sums = _chip_sums("rest", rest, grads, place)
        for n, cs, r in zip(rest, rest_sums, _scatter_chips(rest_sums)):
            comm.landed[n] = (cs, r)
    halves = [_sum_chips("grad_sum_" + t, *comm.landed[t], place) for t in tags]
    reduced = dict(zip(tags, _join_halves(halves)))
    g_shard = {n: reduced[n] for n in MATRICES}
    g_shard["w_mod"] = grads["w_mod"]
    g_shard["dw_weight"] = reduced["small"][:CONV_W]
    g_shard["w_alpha_f"] = reduced["small"][32:32 + LOWRANK, :DK]
    g_shard["w_alpha_b"] = reduced["small"][32:32 + LOWRANK, DK:]

    rep_shapes = [w[n].shape for n in REPLICATED]
    small = _allreduce_small(_pack_flat([grads[n].reshape(w[n].shape) for n in REPLICATED], 8 * 1024))

    g_out, d_out, m_out, v_out = {}, {}, {}, {}
    for n, _ in SHARDED:
        s2 = w[n].shape[1:]
        go, d, nm, nv = _adamw("adamw_" + n, w[n].reshape(s2), g_shard[n], m[n].reshape(s2), v[n].reshape(s2))
        g_out[n], d_out[n] = go.reshape(w[n].shape), d.reshape(w[n].shape)
        m_out[n], v_out[n] = nm.reshape(w[n].shape), nv.reshape(w[n].shape)
    pk = lambda t: _pack_flat([t[n] for n in REPLICATED], 8 * 1024)
    go, d, nm, nv = _adamw("adamw_vectors", pk(w), small, pk(m), pk(v))
    for n, gg, dd, mm, vv in zip(REPLICATED, _unpack_flat(go, rep_shapes), _unpack_flat(d, rep_shapes),
                                 _unpack_flat(nm, rep_shapes), _unpack_flat(nv, rep_shapes)):
        g_out[n], d_out[n], m_out[n], v_out[n] = gg, dd, mm, vv

    return (loss, grad_x, *[g_out[n] for n in WEIGHTS], *[d_out[n] for n in WEIGHTS],
            *[m_out[n] for n in WEIGHTS], *[v_out[n] for n in WEIGHTS])
```

```python
import functools
import math

import jax
import jax.numpy as jnp
from jax import lax
from jax.experimental import pallas as pl
from jax.experimental.pallas import tpu as pltpu

F32, BF16 = jnp.float32, jnp.bfloat16
MESH = pl.DeviceIdType.MESH
HIGHEST = lax.Precision.HIGHEST

D = 1024
FF = 2816
HEADS, DK, DV = 4, 128, 256
LOWRANK = 16
CONV_W = 31
CHUNK = 64
TAU = 16.0
EPS = 1e-6
Q_SCALE = DK ** -0.5
TM = 256
D_IN = 7200
D_INP = 7296
LR_COL = 7168
OG_CB, GA_CB, GB_CB = 6, 4, 5
VMEM_LIMIT = 52 * 1024 * 1024
WGRAD_VMEM = 40 * 1024 * 1024

ADAM_LR, ADAM_B1, ADAM_B2, ADAM_EPS, ADAM_WD, ADAM_STEP = 0.001, 0.9, 0.999, 1e-08, 0.01, 10


def _silu(x):
    return x * jax.nn.sigmoid(x)


def _rms(h, g):
    return h * lax.rsqrt(jnp.mean(h * h, axis=-1, keepdims=True) + EPS) * g


def _modnorm(x, g, shift, scale):
    return _rms(x, g) * (1 + scale) + shift


def _cparams(sem=None):
    return pltpu.CompilerParams(dimension_semantics=sem, vmem_limit_bytes=VMEM_LIMIT)


def _twice(v):
    return v, v


def tok(arr, width=None, cb=0, clamp=None):
    return ("tok", arr, arr.shape[1] if width is None else width, cb, clamp)


def mod(arr):
    return ("mod", arr)


def const(arr):
    return ("const", arr)


def _rowmap(name, body, n_tiles, ins, outs, *, tpb, nb, side=None):
    def modrow(i):
        return jnp.minimum(i // tpb, nb)

    in_specs, args = [], []
    for spec in ins:
        if spec[0] == "tok":
            _, arr, width, cb, clamp = spec
            if clamp is None:
                im = lambda i, cb=cb: (i, cb)
            else:
                im = lambda i, cb=cb, clamp=clamp: (jnp.minimum(i, clamp), cb)
            in_specs.append(pl.BlockSpec((TM, width), im))
        elif spec[0] == "mod":
            arr = spec[1]
            in_specs.append(pl.BlockSpec((1, 1, arr.shape[2]), lambda i: (modrow(i), 0, 0)))
        else:
            arr = spec[1]
            in_specs.append(pl.BlockSpec(arr.shape, lambda i, nd=arr.ndim: (0,) * nd))
        args.append(arr)
    out_specs, out_shapes, aliases = [], [], {}
    for o in outs:
        if o[0] == "tok":
            _, rows, width, dtype = o
            out_shapes.append(jax.ShapeDtypeStruct((rows, width), dtype))
            out_specs.append(pl.BlockSpec((TM, width), lambda i: (i, 0)))
        elif o[0] == "cols":
            buf, width, cb = o[1:4]
            first_tile = o[4] if len(o) > 4 else 0
            if not isinstance(buf, jax.ShapeDtypeStruct):
                aliases[len(args)] = len(out_shapes)
                in_specs.append(ANY)
                args.append(buf)
            out_shapes.append(jax.ShapeDtypeStruct(buf.shape, buf.dtype))
            out_specs.append(pl.BlockSpec((TM, width), lambda i, cb=cb, t0=first_tile: (i + t0, cb)))
        elif o[0] == "tok_head":
            _, rows, width, dtype, last = o
            out_shapes.append(jax.ShapeDtypeStruct((rows, width), dtype))
            out_specs.append(pl.BlockSpec((TM, width), lambda i, last=last: (jnp.minimum(i, last), 0)))
        elif o[0] == "tokT":
            _, rows, width, dtype = o
            out_shapes.append(jax.ShapeDtypeStruct((width, rows), dtype))
            out_specs.append(pl.BlockSpec((width, TM), lambda i: (0, i)))
        elif o[0] == "acc":
            _, rows, width = o
            out_shapes.append(jax.ShapeDtypeStruct((rows, width), F32))
            out_specs.append(pl.BlockSpec((rows, width), lambda i: (0, 0)))
        else:
            width = o[1]
            rows_visited = min((n_tiles - 1) // tpb, nb) + 1
            out_shapes.append(jax.ShapeDtypeStruct((rows_visited, 1, width), F32))
            out_specs.append(pl.BlockSpec((1, 1, width), lambda i: (modrow(i), 0, 0)))
    n_in = len(ins)

    def kern(*refs):
        i = pl.program_id(0)
        vals = []
        for r, spec in zip(refs[:n_in], ins):
            val = r[0] if spec[0] == "mod" else r[...]
            vals.append(val.astype(F32) if spec[0] == "tok" and val.dtype == BF16 else val)
        res = body(i, *vals)
        for r, o, val in zip(refs[len(args):], outs, res):
            if o[0] in ("tok", "cols"):
                r[...] = val.astype(r.dtype)
            elif o[0] == "tok_head":
                @pl.when(i <= o[4])
                def _():
                    r[...] = val.astype(r.dtype)
            elif o[0] == "tokT":
                r[...] = val.T.astype(r.dtype)
            elif o[0] == "acc":
                @pl.when(i == 0)
                def _():
                    r[...] = jnp.zeros(r.shape, F32)
                r[...] += jnp.broadcast_to(val, r.shape)
            else:
                first = jnp.logical_or(i == 0, modrow(i) != modrow(jnp.maximum(i - 1, 0)))

                @pl.when(first)
                def _():
                    r[...] = jnp.zeros(r.shape, F32)
                r[0] += val

    if side is not None:
        assert not aliases
        return _pallas(kern, grid=(n_tiles,), in_specs=in_specs, out_specs=out_specs, out_shape=out_shapes,
                       scratch_shapes=[], sem=("arbitrary",), name=name, args=args, side=side)
    return pl.pallas_call(
        kern, grid=(n_tiles,), in_specs=in_specs, out_specs=out_specs, out_shape=out_shapes,
        input_output_aliases=aliases, compiler_params=_cparams(("arbitrary",)), name=name)(*args)


def _pick(n, cands):
    for c in cands:
        if n % c == 0:
            return c
    return n


def _pallas(kern, *, grid, in_specs, out_specs, out_shape, scratch_shapes, sem, name, args, side=None):
    if side is None:
        return pl.pallas_call(kern, grid=grid, in_specs=in_specs, out_specs=out_specs, out_shape=out_shape,
                              scratch_shapes=scratch_shapes, compiler_params=_cparams(sem), name=name)(*args)
    single = not isinstance(out_shape, (list, tuple))
    shapes = [out_shape] if single else list(out_shape)
    ospecs = [out_specs] if single else list(out_specs)
    n_in, n_out, n_scr = len(in_specs), len(shapes), len(scratch_shapes)
    s_in, s_out = list(side["ins"]), list(side["outs"])

    def wrapped(*refs):
        pos = [0]

        def take(n):
            pos[0] += n
            return refs[pos[0] - n:pos[0]]
        ins, sins, outs, souts, scr, sems = take(n_in), take(len(s_in)), take(n_out), take(len(s_out)), take(n_scr), take(2)
        ids = [pl.program_id(k) for k in range(len(grid))]
        first = functools.reduce(jnp.logical_and, [i == 0 for i in ids])
        last = functools.reduce(jnp.logical_and, [i == g - 1 for i, g in zip(ids, grid)])
        copy = _remote(*sems)

        @pl.when(first)
        def _():
            side["start"](sins, souts, copy)
        kern(*ins, *outs, *scr)

        @pl.when(last)
        def _():
            side["finish"](sins, souts, copy)

    res = pl.pallas_call(
        wrapped, grid=grid, in_specs=list(in_specs) + [ANY] * len(s_in), out_specs=ospecs + [ANY] * len(s_out),
        out_shape=shapes + s_out, scratch_shapes=list(scratch_shapes) + _sems(side["nsem"]),
        input_output_aliases={n_in + a: n_out + b for a, b in side.get("alias", {}).items()},
        compiler_params=_cparams(("arbitrary",) * len(grid)), name=name)(*args, *s_in)
    main = res[:n_out]
    return (main[0] if single else main), list(res[n_out:])


def _mm(name, a, b, *, trans_b=False, out_dtype=F32, a_fn=None, bias=None, rows=None, side=None, residual=None):
    m, k = a.shape if a.ndim == 2 else (a.shape[1], 2 * a.shape[2])
    m = m if rows is None else rows
    shard = b.shape[2] if b.ndim == 3 else None
    if trans_b:
        n = b.shape[-2]
        tk = _pick(shard, (2816, 2304, 1408, 1024)) if shard else (
            k if k <= 2816 else _pick(k, (2816, 2432, 2304, 2048, 1536, 1408, 1024, 512, 256, 128)))
        tn = _pick(n, (1024, 512, 384, 256, 128))
    else:
        n = 4 * shard if shard else b.shape[1]
        tk = k if k <= 2816 else _pick(k, (2816, 2432, 2304, 2048, 1536, 1408, 1024, 512, 256, 128))
        tn = _pick(shard, (512, 384, 1408, 256, 128)) if shard else _pick(n, (1024, 2432, 512, 384, 256, 128))
    if residual is not None:
        tm = next(c for c in (512, 256) if m % c == 0 and (residual["tpb"] * TM) % c == 0)
    else:
        tm = _pick(m, (1024, 512, 256))
    nk = k // tk
    per = shard // (tk if trans_b else tn) if shard else None
    dims = (((1,), (1,)), ((), ())) if trans_b else (((1,), (0,)), ((), ()))

    def kern(*refs):
        a_ref, b_ref = refs[0], refs[1]
        bias_ref = refs[2] if bias is not None else None
        acc_ref = refs[-1]
        if residual is not None:
            x_ref, gate_ref, o_ref, xo_ref = refs[-5:-1]
        else:
            o_ref = refs[-2]
        kk = pl.program_id(2)
        av = a_ref[...]
        if a_fn is not None:
            av = a_fn(av)
        p = lax.dot_general(av.astype(BF16), b_ref[...].astype(BF16), dims, preferred_element_type=F32)

        def finish(total):
            if bias_ref is not None:
                total = total + bias_ref[...]
            o_ref[...] = total.astype(o_ref.dtype)
            if residual is not None:
                xo_ref[...] = x_ref[...] + residual["scale"] * gate_ref[0] * total

        if nk == 1:
            finish(p)
        else:
            @pl.when(kk == 0)
            def _():
                acc_ref[...] = p

            @pl.when(kk > 0)
            def _():
                acc_ref[...] += p

            @pl.when(kk == nk - 1)
            def _():
                finish(acc_ref[...])

    if shard and trans_b:
        b_spec = pl.BlockSpec((None, tn, tk), lambda i, j, kk: (kk // per, j, kk % per))
    elif shard:
        b_spec = pl.BlockSpec((None, tk, tn), lambda i, j, kk: (j // per, kk, j % per))
    elif trans_b:
        b_spec = pl.BlockSpec((tn, tk), lambda i, j, kk: (j, kk))
    else:
        b_spec = pl.BlockSpec((tk, tn), lambda i, j, kk: (kk, j))
    if a.ndim == 3:
        pa = a.shape[2] // tk
        a_spec = pl.BlockSpec((None, tm, tk), lambda i, j, kk: (kk // pa, i, kk % pa))
    else:
        a_spec = pl.BlockSpec((tm, tk), lambda i, j, kk: (i, kk))
    in_specs = [a_spec, b_spec]
    args = [a, b]
    if bias is not None:
        in_specs.append(pl.BlockSpec((1, tn), lambda i, j, kk: (0, j)))
        args.append(bias)
    out_specs = pl.BlockSpec((tm, tn), lambda i, j, kk: (i, j))
    out_shape = jax.ShapeDtypeStruct((m, n), out_dtype)
    if residual is not None:
        tiles, nb = residual["tpb"] * TM // tm, residual["gate"].shape[0] - 1
        in_specs += [out_specs, pl.BlockSpec((1, 1, tn), lambda i, j, kk: (jnp.minimum(i // tiles, nb), 0, j))]
        args += [residual["x"], residual["gate"]]
        out_specs, out_shape = [out_specs, out_specs], [out_shape, jax.ShapeDtypeStruct((m, n), F32)]
    return _pallas(
        kern, grid=(m // tm, n // tn, nk), in_specs=in_specs, out_specs=out_specs, out_shape=out_shape,
        scratch_shapes=[pltpu.VMEM((tm, tn) if nk > 1 else (8, 128), F32)],
        sem=("parallel", "parallel", "arbitrary"), name=name, args=args, side=side)


def _mm_tn(name, x, dy):
    t, k1, n1 = x.shape[0], x.shape[1], dy.shape[1]
    tt = _pick(t, (512, 256, 128, 64, 8))
    tk1 = _pick(k1, (1024, 512, 256, 128))
    tn = _pick(n1, (512, 384, 256, 128))
    ns = t // tt

    def kern(x_ref, dy_ref, o_ref, acc_ref):
        s = pl.program_id(2)
        p = lax.dot_general(x_ref[...].astype(BF16), dy_ref[...].astype(BF16), (((0,), (0,)), ((), ())),
                            preferred_element_type=F32)

        @pl.when(s == 0)
        def _():
            acc_ref[...] = p

        @pl.when(s > 0)
        def _():
            acc_ref[...] += p

        @pl.when(s == ns - 1)
        def _():
            o_ref[...] = acc_ref[...].astype(o_ref.dtype)

    return pl.pallas_call(
        kern, grid=(k1 // tk1, n1 // tn, ns),
        in_specs=[pl.BlockSpec((tt, tk1), lambda i, j, s: (s, i)), pl.BlockSpec((tt, tn), lambda i, j, s: (s, j))],
        out_specs=pl.BlockSpec((tk1, tn), lambda i, j, s: (i, j)), out_shape=jax.ShapeDtypeStruct((k1, n1), F32),
        scratch_shapes=[pltpu.VMEM((tk1, tn), F32)],
        compiler_params=_cparams(("parallel", "parallel", "arbitrary")), name=name)(x, dy)


def _wgrad(name, xt, dy, rows=None, col_shards=False, side=None):
    k1 = xt.shape[0]
    t = xt.shape[1] if rows is None else rows
    n1 = dy.shape[1] if dy.ndim == 2 else 2 * dy.shape[2]
    tn = _pick(n1 // 4, (1408, 512, 384, 256, 128)) if col_shards else _pick(n1, (1024, 2432, 512, 384, 256, 128))

    def token_tile(tm):
        fixed = tm * tn * (4 + 4 + 2 * 2)
        return next((c for c in (2048, 1536, 1024, 512, 256, 128)
                     if t % c == 0 and fixed + 4 * c * (tm + tn) <= WGRAD_VMEM), 128)
    tm = next((c for c in (1024, 1408, 512, 256) if k1 % c == 0 and token_tile(c) >= 1024),
              _pick(k1, (1024, 1408, 512, 256)))
    tk = token_tile(tm)
    ns = t // tk
    per = n1 // 4 // tn

    def kern(x_ref, dy_ref, o_ref, acc_ref):
        s = pl.program_id(2)
        p = jnp.dot(x_ref[...], dy_ref[...], preferred_element_type=F32)

        @pl.when(s == 0)
        def _():
            acc_ref[...] = p

        @pl.when(s > 0)
        def _():
            acc_ref[...] += p

        @pl.when(s == ns - 1)
        def _():
            o_ref[...] = acc_ref[...].astype(o_ref.dtype)

    if dy.ndim == 3:
        pd = dy.shape[2] // tn
        dy_spec = pl.BlockSpec((None, tk, tn), lambda i, j, s: (j // pd, s, j % pd))
    else:
        dy_spec = pl.BlockSpec((tk, tn), lambda i, j, s: (s, j))
    if col_shards:
        out_spec = pl.BlockSpec((None, tm, tn), lambda i, j, s: (j // per, i, j % per))
        out_shape = jax.ShapeDtypeStruct((4, k1, n1 // 4), BF16)
    else:
        out_spec = pl.BlockSpec((tm, tn), lambda i, j, s: (i, j))
        out_shape = jax.ShapeDtypeStruct((k1, n1), BF16)
    return _pallas(
        kern, grid=(k1 // tm, n1 // tn, ns),
        in_specs=[pl.BlockSpec((tm, tk), lambda i, j, s: (i, s)), dy_spec],
        out_specs=out_spec, out_shape=out_shape, scratch_shapes=[pltpu.VMEM((tm, tn), F32)],
        sem=("parallel", "parallel", "arbitrary"), name=name, args=[xt, dy], side=side)


def _swiglu_fwd(name, u, w_gu, side=None):
    m = u.shape[0]
    half = w_gu.shape[2]
    tm = _pick(m, (512, 256))

    def kern(u_ref, wa_ref, wb_ref, ab_ref, hm_ref, hmt_ref):
        uv = u_ref[...]
        a = jnp.dot(uv, wa_ref[...], preferred_element_type=F32)
        b = jnp.dot(uv, wb_ref[...], preferred_element_type=F32)
        s = jax.nn.sigmoid(a)
        silu_a = a * s
        ab_ref[0] = (b * (s * (1 + a * (1 - s)))).astype(BF16)
        ab_ref[1] = silu_a.astype(BF16)
        hm = (silu_a * b).astype(BF16)
        hm_ref[...] = hm
        hmt_ref[...] = hm.T

    return _pallas(
        kern, grid=(2, m // tm),
        in_specs=[pl.BlockSpec((tm, D), lambda j, i: (i, 0)),
                  pl.BlockSpec((None, D, half), lambda j, i: (j, 0, 0)),
                  pl.BlockSpec((None, D, half), lambda j, i: (2 + j, 0, 0))],
        out_specs=[pl.BlockSpec((2, tm, half), lambda j, i: (0, i, j)),
                   pl.BlockSpec((tm, half), lambda j, i: (i, j)),
                   pl.BlockSpec((half, tm), lambda j, i: (j, i))],
        out_shape=[jax.ShapeDtypeStruct((2, m, FF), BF16), jax.ShapeDtypeStruct((m, FF), BF16),
                   jax.ShapeDtypeStruct((FF, m), BF16)],
        scratch_shapes=[], sem=("parallel", "parallel"), name=name, args=[u, w_gu, w_gu], side=side)


def _swiglu_bwd(name, df, w_down, ab):
    m = df.shape[0]
    half = FF // 2
    tm = _pick(m, (512, 256))

    def kern(df_ref, w_ref, ab_ref, o_ref):
        dh = lax.dot_general(df_ref[...], w_ref[...], (((1,), (1,)), ((), ())), preferred_element_type=F32)
        o_ref[0] = (dh * ab_ref[0].astype(F32)).astype(BF16)
        o_ref[1] = (dh * ab_ref[1].astype(F32)).astype(BF16)

    return pl.pallas_call(
        kern, grid=(2, m // tm),
        in_specs=[pl.BlockSpec((tm, D), lambda j, i: (i, 0)),
                  pl.BlockSpec((half, D), lambda j, i: (j, 0)),
                  pl.BlockSpec((2, tm, half), lambda j, i: (0, i, j))],
        out_specs=pl.BlockSpec((2, tm, half), lambda j, i: (0, i, j)),
        out_shape=jax.ShapeDtypeStruct((2, m, FF), BF16),
        compiler_params=_cparams(("parallel", "parallel")), name=name)(df, w_down, ab)


def _gla_maps(bl, t, tc):
    nx, nc = t // CHUNK, tc // CHUNK
    nxb = bl * nx

    def rowblk(d, b, n):
        c_ctx = jnp.where(d == 0, n, nc - 1 - n)
        c_x = jnp.where(d == 0, n - nc, nx - 1 - (n - nc))
        return jnp.where(n < nc, nxb + b * nc + c_ctx, b * nx + c_x)

    def xblk(d, b, n):
        n2 = jnp.maximum(n, nc)
        return b * nx + jnp.where(d == 0, n2 - nc, nx - 1 - (n2 - nc))

    return nx, nc, rowblk, xblk


def _dot01(m, x, cm):
    x1 = x.astype(BF16)
    r1 = x - x1.astype(F32)
    x2 = r1.astype(BF16)
    x3 = (r1 - x2.astype(F32)).astype(BF16)
    w = x.shape[1]
    p = lax.dot_general(m.astype(BF16), jnp.concatenate([x1, x2, x3], axis=1), (((cm,), (0,)), ((), ())),
                        preferred_element_type=F32)
    return p[:, :w] + p[:, w:2 * w] + p[:, 2 * w:]


def _gla_chunk(m, q, k, v, g, h):
    gh = g[:, h * DK:(h + 1) * DK]
    b = _dot01(m, gh, 1)
    tot = jnp.sum(gh, axis=0, keepdims=True)
    mid = b[CHUNK // 2:CHUNK // 2 + 1, :]
    qh = q[:, h * DK:(h + 1) * DK] * Q_SCALE
    kh = k[:, h * DK:(h + 1) * DK]
    vh = v[:, h * DV:(h + 1) * DV]
    return b, tot, mid, qh, kh, vh


def _dot(a, b, ca, cb):
    return lax.dot_general(a.astype(BF16), b.astype(BF16), (((ca,), (cb,)), ((), ())),
                           preferred_element_type=F32)


def _gla_fwd(p, g2, mmats, bl, t, tc):
    nx, nc, rowblk, xblk = _gla_maps(bl, t, tc)
    ns = nx + nc

    def kern(q0, k0, v0, g0, q1, k1, v1, g1, m_ref, o0, o1, ss_ref, s_ref):
        n = pl.program_id(1)

        @pl.when(n == 0)
        def _():
            s_ref[...] = jnp.zeros(s_ref.shape, F32)
        sides = ((q0, k0, v0, g0, o0), (q1, k1, v1, g1, o1))
        loaded = [(m_ref[d], q_ref[...].astype(F32), k_ref[...].astype(F32), v_ref[...].astype(F32), g_ref[...])
                  for d, (q_ref, k_ref, v_ref, g_ref, _) in enumerate(sides)]
        chains = [(d, h) for d in range(2) for h in range(HEADS)]
        base = [_gla_chunk(*loaded[d], h) for d, h in chains]
        pre = []
        for (d, h), (b, tot, mid, qh, kh, vh) in zip(chains, base):
            s0 = s_ref[d, h * DV:(h + 1) * DV, :]
            ss_ref[d, 0, 0, h * DV:(h + 1) * DV, :] = s0.astype(BF16)
            pre.append((s0, kh * jnp.exp(tot - b), qh * jnp.exp(b), qh * jnp.exp(b - mid), kh * jnp.exp(mid - b)))
        raw = [(_dot(qm, km, 1, 1), _dot(qe, s0, 1, 1), _dot(bs[5], kl, 0, 0))
               for bs, (s0, kl, qe, qm, km) in zip(base, pre)]
        for (d, h), bs, (s0, kl, qe, qm, km), (att_raw, inter, s_add) in zip(chains, base, pre, raw):
            s_ref[d, h * DV:(h + 1) * DV, :] = s0 * jnp.exp(bs[1]) + s_add
            sides[d][4][:, h * DV:(h + 1) * DV] = (inter + _dot(loaded[d][0] * att_raw, bs[5], 1, 0)).astype(BF16)

    def operands(d):
        return [pl.BlockSpec((CHUNK, 512), lambda b, n: (rowblk(d, b, n), 4)),
                pl.BlockSpec((CHUNK, 512), lambda b, n: (rowblk(d, b, n), 5)),
                pl.BlockSpec((CHUNK, 1024), lambda b, n: (rowblk(d, b, n), 3)),
                pl.BlockSpec((CHUNK, 512), lambda b, n: (rowblk(d, b, n), d))]

    o_shape = jax.ShapeDtypeStruct((bl * t, HEADS * DV), BF16)
    return pl.pallas_call(
        kern, grid=(bl, ns),
        in_specs=operands(0) + operands(1) + [pl.BlockSpec((2, CHUNK, CHUNK), lambda b, n: (0, 0, 0))],
        out_specs=[pl.BlockSpec((CHUNK, 1024), lambda b, n: (xblk(0, b, n), 0)),
                   pl.BlockSpec((CHUNK, 1024), lambda b, n: (xblk(1, b, n), 0)),
                   pl.BlockSpec((2, 1, 1, HEADS * DV, DK), lambda b, n: (0, b, n, 0, 0))],
        out_shape=[o_shape, o_shape, jax.ShapeDtypeStruct((2, bl, ns, HEADS * DV, DK), BF16)],
        scratch_shapes=[pltpu.VMEM((2, HEADS * DV, DK), F32)],
        compiler_params=_cparams(("parallel", "arbitrary")), name="gla_fwd")(p, p, p, g2, p, p, p, g2, mmats)


def _gla_bwd(p, g2, mmats, ssave, do, bl, t, tc):
    nx, nc, rowblk, xblk = _gla_maps(bl, t, tc)
    ns = nx + nc
    rev = lambda s: ns - 1 - s

    def kern(q0, k0, v0, g0, do0, q1, k1, v1, g1, do1, m_ref, ss_ref,
             dq0, dk0, dv0, dg0, dq1, dk1, dv1, dg1, ds_ref):
        step = pl.program_id(1)
        n = ns - 1 - step

        @pl.when(step == 0)
        def _():
            ds_ref[...] = jnp.zeros(ds_ref.shape, F32)
        live = (n >= nc).astype(F32)
        sides = ((q0, k0, v0, g0, do0, dq0, dk0, dv0, dg0), (q1, k1, v1, g1, do1, dq1, dk1, dv1, dg1))
        loaded = [(m_ref[d], s[0][...].astype(F32), s[1][...].astype(F32), s[2][...].astype(F32), s[3][...])
                  for d, s in enumerate(sides)]
        dovs = [s[4][...] * live for s in sides]
        chains = [(d, h) for d in range(2) for h in range(HEADS)]
        base = [_gla_chunk(*loaded[d], h) for d, h in chains]
        pre = []
        for (d, h), (b, tot, mid, qh, kh, vh) in zip(chains, base):
            eb, ebm, emb, etb = jnp.exp(b), jnp.exp(b - mid), jnp.exp(mid - b), jnp.exp(tot - b)
            pre.append(dict(
                eb=eb, ebm=ebm, emb=emb, etb=etb, etot=jnp.exp(tot), qe=qh * eb, qm=qh * ebm, km=kh * emb, kl=kh * etb,
                vh=vh, doh=dovs[d][:, h * DV:(h + 1) * DV], s0=ss_ref[d, 0, 0, h * DV:(h + 1) * DV, :].astype(F32),
                ds1=ds_ref[d, h * DV:(h + 1) * DV, :]))
        first = [dict(att=_dot(c["qm"], c["km"], 1, 1), datt=_dot(c["doh"], c["vh"], 1, 1),
                      dqe=_dot(c["doh"], c["s0"], 1, 0), ds_add=_dot(c["doh"], c["qe"], 0, 0),
                      dkl=_dot(c["vh"], c["ds1"], 1, 0), dv_s=_dot(c["kl"], c["ds1"], 1, 1)) for c in pre]
        second = []
        for (d, h), c, f in zip(chains, pre, first):
            m = loaded[d][0]
            ds_ref[d, h * DV:(h + 1) * DV, :] = c["ds1"] * c["etot"] + f["ds_add"]
            att, datt = m * f["att"], m * f["datt"]
            second.append(dict(dqm=_dot(datt, c["km"], 1, 0), dkm=_dot(datt, c["qm"], 0, 0),
                               dv_a=_dot(att, c["doh"], 0, 0)))
        for (d, h), c, f, s in zip(chains, pre, first, second):
            dq_ref, dk_ref, dv_ref, dg_ref = sides[d][5:]
            dtot = c["etot"] * jnp.sum(c["ds1"] * c["s0"], axis=0, keepdims=True) + jnp.sum(
                f["dkl"] * c["kl"], axis=0, keepdims=True)
            db = f["dqe"] * c["qe"] + s["dqm"] * c["qm"] - s["dkm"] * c["km"] - f["dkl"] * c["kl"]
            dq_ref[:, h * DK:(h + 1) * DK] = ((f["dqe"] * c["eb"] + s["dqm"] * c["ebm"]) * Q_SCALE).astype(BF16)
            dk_ref[:, h * DK:(h + 1) * DK] = (s["dkm"] * c["emb"] + f["dkl"] * c["etb"]).astype(BF16)
            dv_ref[:, h * DV:(h + 1) * DV] = (s["dv_a"] + f["dv_s"]).astype(BF16)
            dg_ref[:, h * DK:(h + 1) * DK] = _dot01(loaded[d][0], db, 0) + dtot

    nt = p.shape[0]

    def operands(d):
        return [pl.BlockSpec((CHUNK, 512), lambda b, s: (rowblk(d, b, rev(s)), 4)),
                pl.BlockSpec((CHUNK, 512), lambda b, s: (rowblk(d, b, rev(s)), 5)),
                pl.BlockSpec((CHUNK, 1024), lambda b, s: (rowblk(d, b, rev(s)), 3)),
                pl.BlockSpec((CHUNK, 512), lambda b, s: (rowblk(d, b, rev(s)), d)),
                pl.BlockSpec((CHUNK, 1024), lambda b, s: (xblk(d, b, rev(s)), 0))]

    def results(d):
        row = lambda b, s: (rowblk(d, b, rev(s)), 0)
        return [pl.BlockSpec((CHUNK, 512), row), pl.BlockSpec((CHUNK, 512), row), pl.BlockSpec((CHUNK, 1024), row),
                pl.BlockSpec((CHUNK, 512), row)]

    shapes = [jax.ShapeDtypeStruct((nt, 512), BF16), jax.ShapeDtypeStruct((nt, 512), BF16),
              jax.ShapeDtypeStruct((nt, 1024), BF16), jax.ShapeDtypeStruct((nt, 512), F32)]
    out = pl.pallas_call(
        kern, grid=(bl, ns),
        in_specs=operands(0) + operands(1) + [
            pl.BlockSpec((2, CHUNK, CHUNK), lambda b, s: (0, 0, 0)),
            pl.BlockSpec((2, 1, 1, HEADS * DV, DK), lambda b, s: (0, b, rev(s), 0, 0))],
        out_specs=results(0) + results(1), out_shape=shapes + shapes,
        scratch_shapes=[pltpu.VMEM((2, HEADS * DV, DK), F32)],
        compiler_params=_cparams(("parallel", "arbitrary")), name="gla_bwd")(
            p, p, p, g2, do, p, p, p, g2, do, mmats, ssave)
    return out[:4], out[4:]


CONV_CT = 256
CONV_PAD = 16
CONV_RC = 128
CONV_HALO = 24


def _conv_fill(zp, z_ref, t):
    zp[0:CONV_PAD, :] = jnp.zeros((CONV_PAD, CONV_CT), F32)
    zp[CONV_PAD + t:2 * CONV_PAD + t, :] = jnp.zeros((CONV_PAD, CONV_CT), F32)
    zp[CONV_PAD:CONV_PAD + t, :] = z_ref[...]


def _dwconv(name, z, w, bias, bl, t, flip):
    def kern(z_ref, w_ref, b_ref, o_ref, zp):
        _conv_fill(zp, z_ref, t)
        offs = [(CONV_W - j) if flip else (j + 1) for j in range(CONV_W)]
        for r in range(0, t, CONV_RC):
            acc = jnp.broadcast_to(b_ref[...], (CONV_RC, CONV_CT))
            for rot in range(8):
                win = zp[r + rot:r + rot + CONV_RC + CONV_HALO, :]
                for j in range(CONV_W):
                    if offs[j] % 8 == rot:
                        a = offs[j] - rot
                        acc = acc + w_ref[j:j + 1, :] * win[a:a + CONV_RC, :]
            o_ref[r:r + CONV_RC, :] = acc

    return pl.pallas_call(
        kern, grid=(bl, 1024 // CONV_CT),
        in_specs=[pl.BlockSpec((t, CONV_CT), lambda b, c: (b, c)),
                  pl.BlockSpec((32, CONV_CT), lambda b, c: (0, c)),
                  pl.BlockSpec((1, CONV_CT), lambda b, c: (0, c))],
        out_specs=pl.BlockSpec((t, CONV_CT), lambda b, c: (b, c)),
        out_shape=jax.ShapeDtypeStruct(z.shape, F32),
        scratch_shapes=[pltpu.VMEM((t + 2 * CONV_PAD, CONV_CT), F32)],
        compiler_params=_cparams(("parallel", "parallel")), name=name)(z, w, bias)


def _dwconv_wgrad(z, dzc, bl, t):
    def kern(z_ref, d_ref, dw_ref, db_ref, zp):
        b = pl.program_id(1)

        @pl.when(b == 0)
        def _():
            dw_ref[...] = jnp.zeros(dw_ref.shape, F32)
            db_ref[...] = jnp.zeros(db_ref.shape, F32)
        _conv_fill(zp, z_ref, t)
        for rot in range(8):
            taps = [j for j in range(CONV_W) if (j + 1) % 8 == rot]
            accs = [jnp.zeros((8, CONV_CT), F32) for _ in taps]
            for r in range(0, t, CONV_RC):
                d = d_ref[r:r + CONV_RC, :]
                win = zp[r + rot:r + rot + CONV_RC + CONV_HALO, :]
                for k, j in enumerate(taps):
                    a = j + 1 - rot
                    prod = d * win[a:a + CONV_RC, :]
                    accs[k] = accs[k] + jnp.sum(prod.reshape(CONV_RC // 8, 8, CONV_CT), axis=0)
            for k, j in enumerate(taps):
                dw_ref[j:j + 1, :] += jnp.sum(accs[k], axis=0, keepdims=True)
        db_ref[...] += jnp.sum(d_ref[...], axis=0, keepdims=True)

    return pl.pallas_call(
        kern, grid=(1024 // CONV_CT, bl),
        in_specs=[pl.BlockSpec((t, CONV_CT), lambda c, b: (b, c)),
                  pl.BlockSpec((t, CONV_CT), lambda c, b: (b, c))],
        out_specs=[pl.BlockSpec((32, CONV_CT), lambda c, b: (0, c)),
                   pl.BlockSpec((1, CONV_CT), lambda c, b: (0, c))],
        out_shape=[jax.ShapeDtypeStruct((32, 1024), F32), jax.ShapeDtypeStruct((1, 1024), F32)],
        scratch_shapes=[pltpu.VMEM((t + 2 * CONV_PAD, CONV_CT), F32)],
        compiler_params=_cparams(("parallel", "arbitrary")), name="dwconv_wgrad")(z, dzc)


def _ffn_fwd(tag, xin, n_tiles, g, sh, sc, gate, wts, gu_name, down_name, tpb, nb, comm=None):
    rows = n_tiles * TM
    rm = functools.partial(_rowmap, tpb=tpb, nb=nb)
    u, ut = _with_side(
        comm, tag + "_norm", None,
        lambda s: rm(tag + "_norm", lambda i, x, g_, sh_, sc_: _twice(_modnorm(x, g_, sh_, sc_)), n_tiles,
                     [tok(xin), const(g), mod(sh), mod(sc)], [("tok", rows, D, BF16), ("tokT", rows, D, BF16)], side=s))
    w_gu, w_down = wts[gu_name], wts[down_name]
    ab, hm, hmt = _with_side(comm, tag + "_gu", None, lambda s: _swiglu_fwd(tag + "_gu", u, w_gu, side=s))
    res = dict(x=xin, gate=gate, scale=0.5, tpb=tpb)
    f, xout = _with_side(comm, tag + "_down", None,
                         lambda s: _mm(tag + "_down", hm, w_down, out_dtype=BF16, side=s, residual=res))
    return xout, (ut, ab, hmt, f)


def _ffn_bwd(tag, xin, saved, dxout, dx_clamp, n_tiles, g, sh, sc, gate, w_gu, w_down, tpb, nb, comm=None, grads=None,
             names=None, keep_tiles=None):
    ut, ab, hmt, f = saved
    rows = n_tiles * TM
    rm = functools.partial(_rowmap, tpb=tpb, nb=nb)

    def mask(i):
        return 1.0 if dx_clamp is None else (i <= dx_clamp).astype(F32)

    def b1(i, dx, f_, gt):
        dx = dx * mask(i)
        return (0.5 * gt * dx, jnp.sum(0.5 * f_ * dx, axis=0, keepdims=True))
    df, dgate = rm(tag + "_bres", b1, n_tiles, [tok(dxout, clamp=dx_clamp), tok(f), mod(gate)],
                   [("tok", rows, D, BF16), ("modacc", D)])
    grads[names[1]] = _with_side(comm, tag + "_wdown", grads, lambda s: _wgrad(tag + "_wdown", hmt, df, side=s))
    dab = _swiglu_bwd(tag + "_bdown", df, w_down, ab)
    grads[names[0]] = _with_side(comm, tag + "_wgu", grads,
                                 lambda s: _wgrad(tag + "_wgu", ut, dab, col_shards=True, side=s))
    du = _with_side(comm, tag + "_bgu", grads, lambda s: _mm(tag + "_bgu", dab, w_gu, trans_b=True, side=s))

    def b3(i, x, g_, sh_, sc_, du_, dx):
        _, vjp = jax.vjp(_modnorm, x, g_, sh_, sc_)
        dxn, dg, dsh, dsc = vjp(du_)
        return (dx * mask(i) + dxn, dg, dsh, dsc)
    dx_out = ("tok", rows, D, F32) if keep_tiles is None else ("tok_head", keep_tiles * TM, D, F32, keep_tiles - 1)
    dxin, dg, dsh, dsc = rm(tag + "_bnorm", b3, n_tiles,
                            [tok(xin), const(g), mod(sh), mod(sc), tok(du), tok(dxout, clamp=dx_clamp)],
                            [dx_out, ("acc", 1, D), ("modacc", D), ("modacc", D)])
    return dxin, dict(g=dg, sh=dsh, sc=dsc, gate=dgate)


def _perm_in_cols(w):
    pad = jnp.zeros(w.shape[:-1] + (D_INP - D_IN,), w.dtype)
    return jnp.concatenate([w[..., :4096], w[..., 5152:7200], w[..., 4096:5120], w[..., 5120:5152], pad], axis=-1)


def _unperm_in_cols(w):
    return jnp.concatenate([w[..., :4096], w[..., 6144:LR_COL], w[..., LR_COL:LR_COL + 32], w[..., 4096:6144]], axis=-1)


def _local_step(x, c, ctx, target, wts, comm=None):
    bl, t, _ = x.shape
    tc = ctx.shape[1]
    nx_rows, nc_rows = bl * t, bl * tc
    nt_rows = nx_rows + nc_rows
    tpb = t // TM
    nxt, ntt = nx_rows // TM, nt_rows // TM
    nb = bl
    rm = functools.partial(_rowmap, tpb=tpb, nb=nb)
    last_x = nxt - 1

    x0 = jnp.concatenate([x.reshape(nx_rows, D), ctx.reshape(nc_rows, D)], axis=0)
    tgt = target.reshape(nx_rows, D)

    cc = jnp.concatenate([c, wts["c_ctx"].reshape(1, D), jnp.zeros((8 - bl - 1, D), F32)], axis=0)
    modv, cc_all = _mod_forward(cc, wts["w_mod_shard"], wts["b_mod_shard"])
    mods = [modv[:nb + 1, k * D:(k + 1) * D].reshape(nb + 1, 1, D) for k in range(9)]

    x1, sv1 = _ffn_fwd("ffn1", x0, ntt, wts["g_ffn1"], mods[0], mods[1], mods[2], wts, "w1_gu", "w1_down",
                       tpb, nb, comm)
    u2, u2t = rm("in_norm", lambda i, x_, g_, sh_, sc_: _twice(_modnorm(x_, g_, sh_, sc_)), ntt,
                 [tok(x1), const(wts["g_mix"]), mod(mods[3]), mod(mods[4])],
                 [("tok", nt_rows, D, BF16), ("tokT", nt_rows, D, BF16)])
    w_inp = wts["w_in_p"]
    p = _with_side(comm, "in_proj", None, lambda s: _mm("in_proj", u2, w_inp, out_dtype=BF16, side=s))

    waf, wab, baf, bab = wts["w_alpha_f_pad"], wts["w_alpha_b_pad"], wts["b_alpha_f"], wts["b_alpha_b"]

    def dec_fwd(i, lr, wf, wb, bf_, bb_):
        zf = _dot(lr, wf, 1, 0) + bf_
        zb = _dot(lr, wb, 1, 0) + bb_
        return (jnp.concatenate([jax.nn.log_sigmoid(zf) / TAU, jax.nn.log_sigmoid(zb) / TAU], axis=1),)
    (gfb,) = rm("decay_fwd", dec_fwd, ntt, [tok(p, 128, LR_COL // 128), const(waf), const(wab), const(baf), const(bab)],
                [("tok", nt_rows, 1024, F32)])
    g2 = gfb
    tri = jnp.tril(jnp.ones((CHUNK, CHUNK), F32))
    mmats = jnp.stack([tri, tri.T])
    *o2, ssave = _gla_fwd(p, g2, mmats, bl, t, tc)

    gn_g = wts["gla_norm_g"]

    def gla_out(of, ob, og, gn):
        o = of + ob
        parts = []
        for h in range(HEADS):
            oh = o[:, h * DV:(h + 1) * DV]
            parts.append(oh * lax.rsqrt(jnp.mean(oh * oh, axis=-1, keepdims=True) + EPS))
        return jnp.concatenate(parts, axis=1) * gn * _silu(og)
    yg_in, yg_int = rm("gla_out", lambda i, of, ob, og, gn: _twice(gla_out(of, ob, og, gn)), nxt,
                       [tok(o2[0]), tok(o2[1]), tok(p, 1024, OG_CB), const(gn_g)],
                       [("tok", nx_rows, D, BF16), ("tokT", nx_rows, D, BF16)])
    y_gla = _with_side(comm, "gla_proj", None,
                       lambda s: _mm("gla_proj", yg_in, wts["w_gla_out"], out_dtype=BF16, side=s))

    (z,) = rm("glu", lambda i, a, b: (a * jax.nn.sigmoid(b),), nxt, [tok(p, 1024, 0), tok(p, 1024, 1)],
              [("tok", nx_rows, D, F32)])
    dw_w = jnp.concatenate([wts["dw_weight"], jnp.zeros((1, D), F32)], axis=0)
    zc = _dwconv("dwconv_fwd", z, dw_w, wts["dw_bias"], bl, t, False)

    def ln_silu(zc_, g_, b_):
        mu = jnp.mean(zc_, axis=-1, keepdims=True)
        var = jnp.mean(jnp.square(zc_ - mu), axis=-1, keepdims=True)
        return _silu((zc_ - mu) * lax.rsqrt(var + EPS) * g_ + b_)
    ln_g, ln_b = wts["conv_ln_g"], wts["conv_ln_b"]
    zl, zlt = rm("conv_ln", lambda i, zc_, g_, b_: _twice(ln_silu(zc_, g_, b_)), nxt,
                 [tok(zc), const(ln_g), const(ln_b)], [("tok", nx_rows, D, BF16), ("tokT", nx_rows, D, BF16)])
    y_conv = _mm("conv_proj", zl, wts["w_conv_out"], out_dtype=BF16)

    mg, mgt = rm("merge", lambda i, ga, gb, yc, yg: _twice(jax.nn.sigmoid(ga) * yc + jax.nn.sigmoid(gb) * yg), nxt,
                 [tok(p, 1024, GA_CB), tok(p, 1024, GB_CB), tok(y_conv), tok(y_gla)],
                 [("tok", nx_rows, D, BF16), ("tokT", nx_rows, D, BF16)])
    mix, x2 = _mm("out_proj", mg, wts["w_out"], out_dtype=BF16,
                  residual=dict(x=x1, gate=mods[5], scale=1.0, tpb=tpb))

    x3, sv2 = _ffn_fwd("ffn2", x2, nxt, wts["g_ffn2"], mods[6], mods[7], mods[8], wts, "w2_gu", "w2_down",
                       tpb, nb)
    g_fin = wts["g_final"].reshape(1, D)

    def head(i, x_, g_, tg):
        y, vjp = jax.vjp(_rms, x_, g_)
        diff = y - tg
        dx, dg = vjp(diff * (1.0 / D))
        loss = 0.5 * jnp.sum(jnp.mean(diff * diff, axis=-1, keepdims=True))
        return dx, dg, loss
    dx3, dg_final, loss_acc = rm("loss_head", head, nxt, [tok(x3), const(g_fin), tok(tgt)],
                                 [("tok", nx_rows, D, F32), ("acc", 1, D), ("acc", 8, 128)])
    loss = loss_acc[0, 0]

    grads = {}
    dx2, gf2 = _ffn_bwd("ffn2", x2, sv2, dx3, None, nxt, wts["g_ffn2"], mods[6], mods[7], mods[8],
                        wts["w2_gu"], wts["w2_down"], tpb, nb, comm, grads, ("w2_gu", "w2_down"))
    grads["g_ffn2"] = gf2["g"]

    dmix, dgate5 = rm("mix_bres", lambda i, dx, mx_, gt: (gt * dx, jnp.sum(mx_ * dx, axis=0, keepdims=True)), nxt,
                      [tok(dx2), tok(mix), mod(mods[5])], [("tok", nx_rows, D, BF16), ("modacc", D)])
    dmg = _mm("out_bproj", dmix, wts["w_out"], trans_b=True, out_dtype=BF16)
    grads["w_out"] = _wgrad("out_wgrad", mgt, dmix)

    def merge_bwd(i, dm, ga, gb, yc, yg):
        keep = (i <= last_x).astype(F32)
        dm = dm * keep
        sa, sb = jax.nn.sigmoid(ga), jax.nn.sigmoid(gb)
        return dm * sa, dm * sb, jnp.concatenate([dm * yc * sa * (1 - sa), dm * yg * sb * (1 - sb)], axis=1)
    cl = dict(clamp=last_x)
    dyc, dyg, dp = rm("merge_bwd", merge_bwd, ntt,
                      [tok(dmg, **cl), tok(p, 1024, GA_CB, last_x), tok(p, 1024, GB_CB, last_x), tok(y_conv, **cl),
                       tok(y_gla, **cl)],
                      [("tok", nt_rows, D, BF16), ("tok", nt_rows, D, BF16),
                       ("cols", jax.ShapeDtypeStruct((nt_rows, D_INP), BF16), 2048, 2)])

    dzl = _with_side(comm, "conv_bproj", grads, lambda s: _mm("conv_bproj", dyc, wts["w_conv_out"], trans_b=True,
                                                              rows=nx_rows, out_dtype=BF16, side=s))
    grads["w_conv_out"] = _wgrad("conv_wgrad", zlt, dyc, rows=nx_rows)

    def ln_bwd(i, zc_, g_, b_, dz_):
        _, vjp = jax.vjp(ln_silu, zc_, g_, b_)
        return vjp(dz_)
    dzc, dln_g, dln_b = rm("conv_ln_bwd", ln_bwd, nxt, [tok(zc), const(ln_g), const(ln_b), tok(dzl)],
                           [("tok", nx_rows, D, F32), ("acc", 1, D), ("acc", 1, D)])
    dz = _dwconv("dwconv_bwd", dzc, dw_w, jnp.zeros((1, D), F32), bl, t, True)
    ddw, ddb = _dwconv_wgrad(z, dzc, bl, t)
    grads.update(conv_ln_g=dln_g, conv_ln_b=dln_b, dw_weight=ddw[:CONV_W], dw_bias=ddb)

    def glu_bwd(i, dz_, a, b):
        keep = (i <= last_x).astype(F32)
        dz_ = dz_ * keep
        s = jax.nn.sigmoid(b)
        return (jnp.concatenate([dz_ * s, dz_ * a * s * (1 - s)], axis=1),)
    (dp,) = rm("glu_bwd", glu_bwd, ntt, [tok(dz, **cl), tok(p, 1024, 0, last_x), tok(p, 1024, 1, last_x)],
               [("cols", dp, 2048, 0)])

    dyg_in = _mm("gla_bproj", dyg, wts["w_gla_out"], trans_b=True, rows=nx_rows, out_dtype=BF16)
    grads["w_gla_out"] = _wgrad("gla_wgrad", yg_int, dyg, rows=nx_rows)

    def gla_out_bwd(i, of, ob, og, gn, dy):
        _, vjp = jax.vjp(gla_out, of, ob, og, gn)
        do_, _, dog_, dgn_ = vjp(dy)
        return do_, dog_, dgn_
    do, dp, dgn = rm("gla_out_bwd", gla_out_bwd, nxt,
                     [tok(o2[0]), tok(o2[1]), tok(p, 1024, OG_CB), const(gn_g), tok(dyg_in)],
                     [("tok", nx_rows, D, BF16), ("cols", dp, 1024, OG_CB), ("acc", 1, D)])
    grads["gla_norm_g"] = dgn
    (dp,) = rm("og_ctx_zero", lambda i: (jnp.zeros((TM, D), F32),), ntt - nxt, [], [("cols", dp, 1024, OG_CB, nxt)])

    dq2, dk2, dv2, dg2 = zip(*_gla_bwd(p, g2, mmats, ssave, do, bl, t, tc))

    def dec_bwd(i, lr, wf, wb, bf_, bb_, dgf, dgb_):
        zf = _dot(lr, wf, 1, 0) + bf_
        zb = _dot(lr, wb, 1, 0) + bb_
        dzf = dgf * (1 - jax.nn.sigmoid(zf)) * (1.0 / TAU)
        dzb = dgb_ * (1 - jax.nn.sigmoid(zb)) * (1.0 / TAU)
        dlr = _dot(dzf, wf, 1, 1) + _dot(dzb, wb, 1, 1)
        return (dlr, _dot(lr, dzf, 0, 0), _dot(lr, dzb, 0, 0), jnp.sum(dzf, axis=0, keepdims=True),
                jnp.sum(dzb, axis=0, keepdims=True))
    dp, dwaf, dwab, dbaf, dbab = rm(
        "decay_bwd", dec_bwd, ntt,
        [tok(p, 128, LR_COL // 128), const(waf), const(wab), const(baf), const(bab), tok(dg2[0]), tok(dg2[1])],
        [("cols", dp, 128, LR_COL // 128), ("acc", 128, 512), ("acc", 128, 512), ("acc", 1, 512), ("acc", 1, 512)])
    grads.update(w_alpha_f=dwaf[:LOWRANK], w_alpha_b=dwab[LOWRANK:2 * LOWRANK], b_alpha_f=dbaf, b_alpha_b=dbab)

    (dp,) = rm("gla_sum",
               lambda i, q0, q1, k0, k1, v0, v1: (jnp.concatenate([q0 + q1, k0 + k1, v0 + v1], axis=1),), ntt,
               [tok(dq2[0]), tok(dq2[1]), tok(dk2[0]), tok(dk2[1]), tok(dv2[0]), tok(dv2[1])],
               [("cols", dp, 2048, 1)])
    du2 = _with_side(comm, "in_bproj", grads, lambda s: _mm("in_bproj", dp, w_inp, trans_b=True, side=s))
    grads["w_in_p"] = _wgrad("in_wgrad", u2t, dp)

    def in_norm_bwd(i, x_, g_, sh_, sc_, du_, dx):
        keep = (i <= last_x).astype(F32)
        _, vjp = jax.vjp(_modnorm, x_, g_, sh_, sc_)
        dxn, dg, dsh, dsc = vjp(du_)
        return (dx * keep + dxn, dg, dsh, dsc)
    dx1, dg_mix, dsh3, dsc4 = rm("in_norm_bwd", in_norm_bwd, ntt,
                                 [tok(x1), const(wts["g_mix"]), mod(mods[3]), mod(mods[4]), tok(du2), tok(dx2, **cl)],
                                 [("tok", nt_rows, D, F32), ("acc", 1, D), ("modacc", D), ("modacc", D)])
    grads["g_mix"] = dg_mix

    dx0, gf1 = _ffn_bwd("ffn1", x0, sv1, dx1, None, ntt, wts["g_ffn1"], mods[0], mods[1], mods[2],
                        wts["w1_gu"], wts["w1_down"], tpb, nb, comm, grads, ("w1_gu", "w1_down"), keep_tiles=nxt)
    grads["g_ffn1"] = gf1["g"]
    grad_x = dx0.reshape(bl, t, D)

    dmods = [gf1["sh"], gf1["sc"], gf1["gate"], dsh3, dsc4, dgate5, gf2["sh"], gf2["sc"], gf2["gate"]]
    dmod = jnp.concatenate(
        [jnp.concatenate([a.reshape(a.shape[0], D), jnp.zeros((8 - a.shape[0], D), F32)], axis=0) for a in dmods],
        axis=1)
    grads["w_mod"], grads["c_ctx"], grads["b_mod"] = _mod_backward(dmod, cc_all, wts["w_mod_shard"], nb)
    grads["g_final"] = dg_final.reshape(D)
    return loss, grad_x, grads


ANY = pl.BlockSpec(memory_space=pl.ANY)


def _place():
    x, y, c = lax.axis_index("x"), lax.axis_index("y"), lax.axis_index("c")
    chips = [(1 - x, y), (x, 1 - y), (1 - x, 1 - y)]
    return x, y, c, chips


def _remote(send_sems, recv_sems):
    def copy(k, src, dst, to):
        return pltpu.make_async_remote_copy(src_ref=src, dst_ref=dst, send_sem=send_sems.at[k],
                                            recv_sem=recv_sems.at[k], device_id=to, device_id_type=MESH)
    return copy


def _sems(n):
    return [pltpu.SemaphoreType.DMA((n,)), pltpu.SemaphoreType.DMA((n,))]


def _swap_halves(name, gs):
    n = len(gs)

    def body(*refs):
        ins, outs = refs[:n], refs[n:2 * n]
        copy = _remote(refs[2 * n], refs[2 * n + 1])
        x, y, c, _ = _place()
        cps = []
        for i in range(n):
            hr = ins[i].shape[1] // 2
            cps.append(copy(i, ins[i].at[:, pl.ds((1 - c) * hr, hr), :], outs[i], (x, y, 1 - c)))
            cps[-1].start()
        for cp in cps:
            cp.wait()

    return pl.pallas_call(
        body, out_shape=[jax.ShapeDtypeStruct((4, g.shape[1] // 2, g.shape[2]), g.dtype) for g in gs],
        in_specs=[ANY] * n, out_specs=[ANY] * n, scratch_shapes=_sems(n), name=name)(*gs)


def _row_tile(hr):
    return hr if hr <= 256 else _pick(hr, (256, 176, 128, 64, 32, 16))


def _add_halves(name, g, r, place):
    hr = r.shape[1]
    tr = _row_tile(hr)
    nblk = hr // tr

    def kern(p_ref, g_ref, r_ref, o_ref):
        o_ref[...] = (g_ref[...].astype(F32) + r_ref[...].astype(F32)).astype(o_ref.dtype)

    blk = (1, tr, g.shape[2])
    return pl.pallas_call(
        kern,
        grid_spec=pltpu.PrefetchScalarGridSpec(
            num_scalar_prefetch=1, grid=(4, nblk),
            in_specs=[pl.BlockSpec(blk, lambda j, i, p: (j, p[0] * nblk + i, 0)),
                      pl.BlockSpec(blk, lambda j, i, p: (j, i, 0))],
            out_specs=pl.BlockSpec(blk, lambda j, i, p: (j, i, 0))),
        out_shape=jax.ShapeDtypeStruct(r.shape, r.dtype),
        compiler_params=_cparams(("parallel", "parallel")), name=name)(place, g, r)


def _scatter_chips(cs):
    n = len(cs)

    def body(*refs):
        ins, outs = refs[:n], refs[n:2 * n]
        copy = _remote(refs[2 * n], refs[2 * n + 1])
        x, y, c, chips = _place()
        me = 2 * x + y
        sends = []
        for i in range(n):
            for j, (px, py) in enumerate(chips):
                sends.append(copy(3 * i + j, ins[i].at[2 * px + py], outs[i].at[me], (px, py, c)))
                sends[-1].start()
        for i in range(n):
            for j, (px, py) in enumerate(chips):
                src = 2 * px + py
                copy(3 * i + j, ins[i].at[src], outs[i].at[src], (px, py, c)).wait_recv()
        for cp in sends:
            cp.wait_send()

    return pl.pallas_call(
        body, out_shape=[jax.ShapeDtypeStruct(a.shape, a.dtype) for a in cs], in_specs=[ANY] * n,
        out_specs=[ANY] * n, scratch_shapes=_sems(3 * n), name="grad_scatter_chips")(*cs)


def _sum_chips(name, cs, r, place):
    hr = r.shape[1]
    tr = _row_tile(hr)
    nblk = hr // tr

    def kern(p_ref, c_ref, r0, r1, r2, r3, o_ref):
        me = p_ref[1]
        acc = None
        for k, rk in enumerate((r0, r1, r2, r3)):
            val = jnp.where(me == k, c_ref[0].astype(F32), rk[0].astype(F32))
            acc = val if acc is None else acc + val
        o_ref[...] = acc

    blk = (1, tr, r.shape[2])

    def slot(k):
        return lambda i, p: (jnp.where(p[1] == k, (k + 1) % 4, k), i, 0)

    return pl.pallas_call(
        kern,
        grid_spec=pltpu.PrefetchScalarGridSpec(
            num_scalar_prefetch=1, grid=(nblk,),
            in_specs=[pl.BlockSpec(blk, lambda i, p: (p[1], i, 0))] + [pl.BlockSpec(blk, slot(k)) for k in range(4)],
            out_specs=pl.BlockSpec((tr, r.shape[2]), lambda i, p: (p[0] * nblk + i, 0))),
        out_shape=jax.ShapeDtypeStruct((2 * hr, r.shape[2]), F32),
        compiler_params=_cparams(("parallel",)), name=name)(place, cs, r, r, r, r)


def _join_halves(fs):
    n = len(fs)

    def body(*refs):
        ins, outs = refs[:n], refs[n:2 * n]
        copy = _remote(refs[2 * n], refs[2 * n + 1])
        x, y, c, _ = _place()
        cps = []
        for i in range(n):
            hr = ins[i].shape[0] // 2
            cps.append(copy(i, ins[i].at[pl.ds(c * hr, hr), :], outs[i].at[pl.ds(c * hr, hr), :], (x, y, 1 - c)))
            cps[-1].start()
        for i in range(n):
            hr = ins[i].shape[0] // 2
            other = outs[i].at[pl.ds((1 - c) * hr, hr), :]
            copy(i, other, other, (x, y, 1 - c)).wait_recv()
        for cp in cps:
            cp.wait_send()

    return pl.pallas_call(
        body, out_shape=[jax.ShapeDtypeStruct(f.shape, f.dtype) for f in fs], in_specs=[ANY] * n,
        out_specs=[ANY] * n, input_output_aliases={i: i for i in range(n)}, scratch_shapes=_sems(n),
        name="grad_join_halves")(*fs)


def _half_rows(n, hc):
    return pl.ds(hc * (n // 2), n // 2)


def _gather_ici_side(shards):
    n = len(shards)

    def copies(ins, outs, copy):
        x, y, c, chips = _place()
        me = 2 * x + y
        for i in range(n):
            rows = _half_rows(ins[i].shape[0], c)
            for j, (px, py) in enumerate(chips):
                yield (copy(3 * i + j, ins[i].at[rows, :], outs[i].at[me, rows, :], (px, py, c)),
                       outs[i].at[2 * px + py, rows, :])

    def start(ins, outs, copy):
        for cp, _ in copies(ins, outs, copy):
            cp.start()

    def finish(ins, outs, copy):
        x, y, c, chips = _place()
        k = 0
        for cp, landing in copies(ins, outs, copy):
            copy(k, landing, landing, (x, y, c)).wait_recv()
            k += 1
        for cp, _ in copies(ins, outs, copy):
            cp.wait_send()

    return dict(ins=list(shards), outs=[jax.ShapeDtypeStruct((4,) + s.shape, s.dtype) for s in shards], nsem=3 * n,
                start=start, finish=finish)


def _gather_d2d_side(shards, bufs):
    n = len(shards)

    def copies(ins, outs, copy):
        x, y, c, chips = _place()
        me = 2 * x + y
        sibling = (x, y, 1 - c)
        for i in range(n):
            a = ins[i].shape[0]
            yield copy(4 * i + 3, ins[i], outs[i].at[me], sibling), outs[i].at[me]
            for j, (px, py) in enumerate(chips):
                src = 2 * px + py
                mine = outs[i].at[src, _half_rows(a, c), :]
                yield copy(4 * i + j, mine, mine, sibling), outs[i].at[src, _half_rows(a, 1 - c), :]

    def start(ins, outs, copy):
        for cp, _ in copies(ins, outs, copy):
            cp.start()

    def finish(ins, outs, copy):
        x, y, c, _ = _place()
        for i in range(n):
            for k, (cp, landing) in enumerate(list(copies(ins, outs, copy))[4 * i:4 * i + 4]):
                sem = 4 * i + 3 if k == 0 else 4 * i + k - 1
                copy(sem, landing, landing, (x, y, 1 - c)).wait_recv()
        for cp, _ in copies(ins, outs, copy):
            cp.wait_send()

    return dict(ins=list(shards) + list(bufs), outs=[jax.ShapeDtypeStruct(b.shape, b.dtype) for b in bufs],
                nsem=4 * n, alias={n + i: i for i in range(n)}, start=start, finish=finish)


def _gather_d2d(shards, bufs):
    side = _gather_d2d_side(shards, bufs)
    n_in = len(side["ins"])

    def body(*refs):
        ins, outs = refs[:n_in], refs[n_in:n_in + len(bufs)]
        copy = _remote(*refs[n_in + len(bufs):])
        side["start"](ins, outs, copy)
        side["finish"](ins, outs, copy)

    return pl.pallas_call(
        body, out_shape=side["outs"], in_specs=[ANY] * n_in, out_specs=[ANY] * len(bufs),
        input_output_aliases=side["alias"], scratch_shapes=_sems(side["nsem"]), name="gather_weights_d2d")(*side["ins"])


def _scatter_side(cs):
    n = len(cs)

    def copies(ins, outs, copy):
        x, y, c, chips = _place()
        me = 2 * x + y
        for i in range(n):
            for j, (px, py) in enumerate(chips):
                yield copy(3 * i + j, ins[i].at[2 * px + py], outs[i].at[me], (px, py, c)), outs[i].at[2 * px + py]

    def start(ins, outs, copy):
        for cp, _ in copies(ins, outs, copy):
            cp.start()

    def finish(ins, outs, copy):
        x, y, c, _ = _place()
        for k, (cp, landing) in enumerate(copies(ins, outs, copy)):
            copy(k, landing, landing, (x, y, c)).wait_recv()
        for cp, _ in copies(ins, outs, copy):
            cp.wait_send()

    return dict(ins=list(cs), outs=[jax.ShapeDtypeStruct(a.shape, a.dtype) for a in cs], nsem=3 * n,
                start=start, finish=finish)


def _mod_forward(cc, w_shard, b_shard):
    cc_all = _allreduce_small(cc, "cond_gather", reduce=False).reshape(64, D)
    part = _mm("mod_fwd", cc_all, w_shard, a_fn=_silu, bias=b_shard)
    got = _mod_rows_exchange(part)
    return jnp.concatenate([got[j] for j in range(4)], axis=1), cc_all


def _mod_backward(dmod, cc_all, w_shard, ctx_row):
    w = w_shard.shape[1]
    blocks = _dmod_exchange(jnp.transpose(dmod.reshape(8, 4, w), (1, 0, 2))).reshape(64, w)
    dsc = _mm("mod_bproj", blocks, w_shard, trans_b=True)

    def tail(cc_ref, dsc_ref, dm_ref, s_ref, dctx_ref, db_ref):
        cc_ = cc_ref[...]
        s = jax.nn.sigmoid(cc_)
        s_ref[...] = (cc_ * s).astype(BF16)
        is_ctx = ((lax.broadcasted_iota(jnp.int32, (64, 1), 0) & 7) == ctx_row).astype(F32)
        dctx_ref[...] = 0.5 * jnp.sum(dsc_ref[...] * (s * (1 + cc_ * (1 - s))) * is_ctx, axis=0, keepdims=True)
        db_ref[...] = jnp.sum(dm_ref[...], axis=0, keepdims=True)

    s_all, dctx, db_mod = pl.pallas_call(
        tail, out_shape=[jax.ShapeDtypeStruct((64, D), BF16), jax.ShapeDtypeStruct((1, D), F32),
                         jax.ShapeDtypeStruct((1, dmod.shape[1]), F32)], name="mod_tail")(cc_all, dsc, dmod)
    return _mm_tn("mod_wgrad", s_all, blocks), dctx[0], db_mod


def _small_pack(dw, af, ab):
    return jnp.concatenate([dw, jnp.zeros((1, dw.shape[1]), F32), jnp.concatenate([af, ab], axis=1)], axis=0)


def _grad_pieces(n, grads):
    if n == "small":
        return jnp.stack([_small_pack(grads["dw_weight"][:, 256 * j:256 * (j + 1)],
                                      grads["w_alpha_f"][:, DK * j:DK * (j + 1)],
                                      grads["w_alpha_b"][:, DK * j:DK * (j + 1)]) for j in range(4)])
    if n == "w_in":
        g = _unperm_in_cols(grads["w_in_p"])
        return jnp.transpose(g.reshape(D, 4, D_IN // 4), (1, 0, 2))
    g = grads[n]
    return g if g.ndim == 3 else g.reshape(4, g.shape[0] // 4, g.shape[1])


def _swap_side(gs):
    n = len(gs)

    def copies(ins, outs, copy):
        x, y, c, _ = _place()
        for i in range(n):
            yield copy(i, ins[i].at[:, _half_rows(ins[i].shape[1], 1 - c), :], outs[i], (x, y, 1 - c))

    def start(ins, outs, copy):
        for cp in copies(ins, outs, copy):
            cp.start()

    def finish(ins, outs, copy):
        for cp in copies(ins, outs, copy):
            cp.wait()

    return dict(ins=list(gs), outs=[jax.ShapeDtypeStruct((4, g.shape[1] // 2, g.shape[2]), g.dtype) for g in gs],
                nsem=n, start=start, finish=finish)


def _chip_sums(tag, names, grads, place, swapped=None):
    gs = [_grad_pieces(n, grads) for n in names] if swapped is None else swapped[0]
    got = _swap_halves("grad_swap_" + tag, gs) if swapped is None else swapped[1]
    return [_add_halves("grad_add_" + n, g, r, place) for n, g, r in zip(names, gs, got)]


class _Overlap:
    SCATTER = {"in_bproj": ("w2_down", "w2_gu", "w_out"),
               "ffn1_wgu": ("w_conv_out", "w_gla_out", "w_in", "small"),
               "ffn1_bgu": ("w1_down", "w1_gu")}
    GATHER = {("ffn1_norm", None): ("w1_gu", "w1_down"),
              ("ffn1_gu", "ffn1_down"): ("w_in", "w_conv_out", "w_gla_out", "w_out", "small"),
              ("in_proj", "gla_proj"): ("w2_gu", "w2_down")}
    LATE = tuple(n for names in GATHER.values() for n in names)
    SWAP = {"conv_bproj": "in_bproj", "ffn1_wdown": "ffn1_wgu"}

    def __init__(self, shard_of, install, place):
        self.shard_of, self.install, self.place = shard_of, install, place
        self.bufs, self.pending, self.landed, self.swapped = {}, None, {}, {}

    def side(self, tag, grads):
        for (ici, d2d), names in self.GATHER.items():
            shards = [self.shard_of(n) for n in names]
            if tag == ici:
                return _gather_ici_side(shards)
            if tag == d2d:
                return _gather_d2d_side(shards, self.bufs[ici])
        if tag in self.SWAP:
            gs = [_grad_pieces(n, grads) for n in self.SCATTER[self.SWAP[tag]]]
            self.swapped[self.SWAP[tag]] = [gs, None]
            return _swap_side(gs)
        if tag in self.SCATTER:
            names = self.SCATTER[tag]
            self.pending = (names, _chip_sums(tag, names, grads, self.place, self.swapped.get(tag)))
            return _scatter_side(self.pending[1])
        return None

    def done(self, tag, outs):
        for (ici, d2d), names in self.GATHER.items():
            if tag == ici and d2d is None:
                self.install(dict(zip(names, _gather_d2d([self.shard_of(n) for n in names], outs))))
                return
            if tag == ici:
                self.bufs[ici] = outs
                return
            if tag == d2d:
                self.install(dict(zip(names, outs)))
                return
        if tag in self.SWAP:
            self.swapped[self.SWAP[tag]][1] = outs
            return
        for n, cs, r in zip(*self.pending, outs):
            self.landed[n] = (cs, r)


def _with_side(comm, tag, grads, call):
    side = comm.side(tag, grads) if comm is not None else None
    if side is None:
        return call(None)
    res, outs = call(side)
    comm.done(tag, outs)
    return res


def _allreduce_small(v, name="allreduce_small", reduce=True):
    def body(x_ref, out_ref, *scratch):
        gath = out_ref if not reduce else scratch[0]
        send_sems, recv_sems, local_sem = scratch[-3:]
        x, y, c, chips = _place()
        me, sibling = (x, y, c), (x, y, 1 - c)

        def slot(px, py, pc):
            return gath.at[4 * px + 2 * py + pc]

        def copy(k, block, to, src=None):
            return pltpu.make_async_remote_copy(
                src_ref=slot(*block) if src is None else src, dst_ref=slot(*block), send_sem=send_sems.at[k],
                recv_sem=recv_sems.at[k], device_id=to, device_id_type=MESH)

        mine = pltpu.make_async_copy(x_ref, slot(*me), local_sem)
        mine.start()
        first = [copy(0, me, sibling, src=x_ref)]
        first += [copy(1 + j, me, (*chip, c), src=x_ref) for j, chip in enumerate(chips)]
        for cp in first:
            cp.start()
        passed = [copy(4 + j, (*chip, c), sibling) for j, chip in enumerate(chips)]
        for j, chip in enumerate(chips):
            copy(1 + j, (*chip, c), me).wait_recv()
            passed[j].start()
        copy(0, sibling, me).wait_recv()
        for j, chip in enumerate(chips):
            copy(4 + j, (*chip, 1 - c), me).wait_recv()
        for cp in first + passed:
            cp.wait_send()
        mine.wait()
        if reduce:
            acc = gath[0]
            for k in range(1, 8):
                acc = acc + gath[k]
            out_ref[...] = acc

    vm = pl.BlockSpec(memory_space=pltpu.VMEM)
    sems = [pltpu.SemaphoreType.DMA((7,)), pltpu.SemaphoreType.DMA((7,)), pltpu.SemaphoreType.DMA(())]
    return pl.pallas_call(
        body, out_shape=jax.ShapeDtypeStruct(v.shape if reduce else (8,) + v.shape, F32), in_specs=[vm], out_specs=vm,
        scratch_shapes=([pltpu.VMEM((8,) + v.shape, F32)] if reduce else []) + sems, name=name)(v)


def _mod_rows_exchange(part):
    w = part.shape[1]

    def body(p_ref, out_ref, send_sems, recv_sems):
        x, y, c, chips = _place()
        me = 2 * x + y
        copy = _remote(send_sems, recv_sems)
        sends = []
        for j, (px, py) in enumerate(chips):
            rows = pl.ds(pl.multiple_of(8 * (4 * px + 2 * py + c), 8), 8)
            sends.append(copy(j, p_ref.at[rows, :], out_ref.at[me], (px, py, c)))
            sends[-1].start()
        out_ref[me] = p_ref[pl.ds(pl.multiple_of(8 * (4 * x + 2 * y + c), 8), 8), :]
        for j, (px, py) in enumerate(chips):
            landing = out_ref.at[2 * px + py]
            copy(j, landing, landing, (px, py, c)).wait_recv()
        for cp in sends:
            cp.wait_send()

    vm = pl.BlockSpec(memory_space=pltpu.VMEM)
    return pl.pallas_call(body, out_shape=jax.ShapeDtypeStruct((4, 8, w), F32), in_specs=[vm], out_specs=vm,
                          scratch_shapes=_sems(3), name="mod_rows_exchange")(part)


def _dmod_exchange(dm):
    w = dm.shape[2]

    def body(d_ref, out_ref, send_sems, recv_sems):
        x, y, c, _ = _place()
        copy = _remote(send_sems, recv_sems)
        mine = 4 * x + 2 * y + c
        sends = []
        for r in range(1, 8):
            tx, ty, tc = x ^ (r >> 2), y ^ ((r >> 1) & 1), c ^ (r & 1)
            sends.append(copy(r - 1, d_ref.at[2 * tx + ty], out_ref.at[mine], (tx, ty, tc)))
            sends[-1].start()
        out_ref[mine] = d_ref[2 * x + y]
        for r in range(1, 8):
            tx, ty, tc = x ^ (r >> 2), y ^ ((r >> 1) & 1), c ^ (r & 1)
            landing = out_ref.at[4 * tx + 2 * ty + tc]
            copy(r - 1, landing, landing, (tx, ty, tc)).wait_recv()
        for cp in sends:
            cp.wait_send()

    vm = pl.BlockSpec(memory_space=pltpu.VMEM)
    return pl.pallas_call(body, out_shape=jax.ShapeDtypeStruct((8, 8, w), F32), in_specs=[vm], out_specs=vm,
                          scratch_shapes=_sems(7), name="dmod_exchange")(dm)


def _adamw(name, w, g, m, v):
    r, cols = w.shape
    budget = 262144
    tr = r if r * cols <= budget else next(c for c in (256, 128, 64, 32, 16, 8) if r % c == 0 and c * cols <= budget)

    def kern(w_ref, g_ref, m_ref, v_ref, go_ref, d_ref, nm_ref, nv_ref):
        gv = g_ref[...]
        go_ref[...] = gv
        nm = ADAM_B1 * m_ref[...] + (1.0 - ADAM_B1) * gv
        nv = ADAM_B2 * v_ref[...] + (1.0 - ADAM_B2) * jnp.square(gv)
        m_hat = nm / (1.0 - ADAM_B1 ** ADAM_STEP)
        v_hat = nv / (1.0 - ADAM_B2 ** ADAM_STEP)
        d_ref[...] = -ADAM_LR * (m_hat / (jnp.sqrt(v_hat) + ADAM_EPS) + ADAM_WD * w_ref[...])
        nm_ref[...] = nm
        nv_ref[...] = nv

    spec = pl.BlockSpec((tr, cols), lambda i: (i, 0))
    shp = jax.ShapeDtypeStruct((r, cols), F32)
    return pl.pallas_call(kern, grid=(r // tr,), in_specs=[spec] * 4, out_specs=[spec] * 4, out_shape=[shp] * 4,
                          compiler_params=_cparams(("parallel",)), name=name)(w, g, m, v)


SHARDED = (("w_mod", 1), ("w1_gu", 1), ("w1_down", 0), ("w_in", 1), ("dw_weight", 1), ("w_conv_out", 0),
           ("w_alpha_f", 1), ("w_alpha_b", 1), ("w_gla_out", 0), ("w_out", 0), ("w2_gu", 1), ("w2_down", 0))
REPLICATED = ("c_ctx", "b_mod", "g_ffn1", "g_mix", "dw_bias", "conv_ln_g", "conv_ln_b", "b_alpha_f", "b_alpha_b",
              "gla_norm_g", "g_ffn2", "g_final")
WEIGHTS = ("c_ctx", "w_mod", "b_mod", "g_ffn1", "w1_gu", "w1_down", "g_mix", "w_in", "dw_weight", "dw_bias",
           "conv_ln_g", "conv_ln_b", "w_conv_out", "w_alpha_f", "b_alpha_f", "w_alpha_b", "b_alpha_b", "gla_norm_g",
           "w_gla_out", "w_out", "g_ffn2", "w2_gu", "w2_down", "g_final")
MATRICES = ("w1_gu", "w1_down", "w_in", "w_conv_out", "w_gla_out", "w_out", "w2_gu", "w2_down")


def _pack_flat(parts, align):
    flat = jnp.concatenate([p.reshape(-1) for p in parts])
    pad = (-flat.shape[0]) % align
    return jnp.concatenate([flat, jnp.zeros((pad,), flat.dtype)]).reshape(-1, 1024)


def _unpack_flat(flat2d, shapes):
    flat = flat2d.reshape(-1)
    out, off = [], 0
    for s in shapes:
        n = math.prod(s)
        out.append(flat[off:off + n].reshape(s))
        off += n
    return out


def kernel(x, c, ctx, c_ctx, w_mod, b_mod, g_ffn1, w1_gu, w1_down, g_mix, w_in, dw_weight, dw_bias, conv_ln_g, conv_ln_b, w_conv_out, w_alpha_f, b_alpha_f, w_alpha_b, b_alpha_b, gla_norm_g, w_gla_out, w_out, g_ffn2, w2_gu, w2_down, g_final, loss_target, m_c_ctx, m_w_mod, m_b_mod, m_g_ffn1, m_w1_gu, m_w1_down, m_g_mix, m_w_in, m_dw_weight, m_dw_bias, m_conv_ln_g, m_conv_ln_b, m_w_conv_out, m_w_alpha_f, m_b_alpha_f, m_w_alpha_b, m_b_alpha_b, m_gla_norm_g, m_w_gla_out, m_w_out, m_g_ffn2, m_w2_gu, m_w2_down, m_g_final, v_c_ctx, v_w_mod, v_b_mod, v_g_ffn1, v_w1_gu, v_w1_down, v_g_mix, v_w_in, v_dw_weight, v_dw_bias, v_conv_ln_g, v_conv_ln_b, v_w_conv_out, v_w_alpha_f, v_b_alpha_f, v_w_alpha_b, v_b_alpha_b, v_gla_norm_g, v_w_gla_out, v_w_out, v_g_ffn2, v_w2_gu, v_w2_down, v_g_final):
    given = dict(locals())
    w = {n: given[n] for n in WEIGHTS}
    m = {n: given["m_" + n] for n in WEIGHTS}
    v = {n: given["v_" + n] for n in WEIGHTS}

    def shard_of(n):
        if n == "small":
            return _small_pack(w["dw_weight"][0], w["w_alpha_f"][0], w["w_alpha_b"][0])
        return w[n][0].astype(BF16)

    def install(wts, got):
        for n in ("w1_gu", "w2_gu"):
            if n in got:
                wts[n] = got[n]
        for n in ("w1_down", "w2_down", "w_conv_out", "w_gla_out", "w_out"):
            if n in got:
                wts[n] = got[n].reshape(-1, D)
        if "w_in" in got:
            wts["w_in_p"] = _perm_in_cols(jnp.concatenate([got["w_in"][j] for j in range(4)], axis=1))
        if "small" in got:
            sm = got["small"]
            wts["dw_weight"] = jnp.concatenate([sm[j, :CONV_W] for j in range(4)], axis=1)
            zpad = jnp.zeros((128, HEADS * DK), BF16)
            w_af = jnp.concatenate([sm[j, 32:32 + LOWRANK, :DK] for j in range(4)], axis=1)
            w_ab = jnp.concatenate([sm[j, 32:32 + LOWRANK, DK:] for j in range(4)], axis=1)
            wts["w_alpha_f_pad"] = zpad.at[0:LOWRANK].set(w_af.astype(BF16))
            wts["w_alpha_b_pad"] = zpad.at[LOWRANK:2 * LOWRANK].set(w_ab.astype(BF16))

    wts = {n: w[n] for n in REPLICATED}
    chip = 2 * lax.axis_index("x") + lax.axis_index("y")
    mod_cols = w["w_mod"].shape[2]
    wts["w_mod_shard"] = shard_of("w_mod")
    wts["b_mod_shard"] = lax.dynamic_slice(w["b_mod"], (0, chip * mod_cols), (1, mod_cols))

    place = jnp.stack([lax.axis_index("c"), 2 * lax.axis_index("x") + lax.axis_index("y")]).astype(jnp.int32)
    comm = _Overlap(shard_of, lambda got: install(wts, got), place)
    loss, grad_x, grads = _local_step(x, c, ctx, loss_target, wts, comm)
    loss = lax.psum(loss, ("x", "y", "c"))

    tags = MATRICES + ("small",)
    rest = tuple(n for n in tags if n not in comm.landed)
    if rest:
        rest_sums = _chip_sums("rest", rest, grads, place)
        for n, cs, r in zip(rest, rest_sums, _scatter_chips(rest_sums)):
            comm.landed[n] = (cs, r)
    halves = [_sum_chips("grad_sum_" + t, *comm.landed[t], place) for t in tags]
    reduced = dict(zip(tags, _join_halves(halves)))
    g_shard = {n: reduced[n] for n in MATRICES}
    g_shard["w_mod"] = grads["w_mod"]
    g_shard["dw_weight"] = reduced["small"][:CONV_W]
    g_shard["w_alpha_f"] = reduced["small"][32:32 + LOWRANK, :DK]
    g_shard["w_alpha_b"] = reduced["small"][32:32 + LOWRANK, DK:]

    rep_shapes = [w[n].shape for n in REPLICATED]
    small = _allreduce_small(_pack_flat([grads[n].reshape(w[n].shape) for n in REPLICATED], 8 * 1024))

    g_out, d_out, m_out, v_out = {}, {}, {}, {}
    for n, _ in SHARDED:
        s2 = w[n].shape[1:]
        go, d, nm, nv = _adamw("adamw_" + n, w[n].reshape(s2), g_shard[n], m[n].reshape(s2), v[n].reshape(s2))
        g_out[n], d_out[n] = go.reshape(w[n].shape), d.reshape(w[n].shape)
        m_out[n], v_out[n] = nm.reshape(w[n].shape), nv.reshape(w[n].shape)
    pk = lambda t: _pack_flat([t[n] for n in REPLICATED], 8 * 1024)
    go, d, nm, nv = _adamw("adamw_vectors", pk(w), small, pk(m), pk(v))
    for n, gg, dd, mm, vv in zip(REPLICATED, _unpack_flat(go, rep_shapes), _unpack_flat(d, rep_shapes),
                                 _unpack_flat(nm, rep_shapes), _unpack_flat(nv, rep_shapes)):
        g_out[n], d_out[n], m_out[n], v_out[n] = gg, dd, mm, vv

    return (loss, grad_x, *[g_out[n] for n in WEIGHTS], *[d_out[n] for n in WEIGHTS],
            *[m_out[n] for n in WEIGHTS], *[v_out[n] for n in WEIGHTS])
```

```python
import functools
import math

import jax
import jax.numpy as jnp
from jax import lax
from jax.experimental import pallas as pl
from jax.experimental.pallas import tpu as pltpu

F32, BF16 = jnp.float32, jnp.bfloat16
MESH = pl.DeviceIdType.MESH

D = 1024
FF = 2816
HEADS, DK, DV = 4, 128, 256
LOWRANK = 16
CONV_W = 31
CHUNK = 64
TAU = 16.0
EPS = 1e-6
Q_SCALE = DK ** -0.5
TM = 256
D_IN = 7200
D_INP = 7296
LR_COL = 7168
OG_CB, GA_CB, GB_CB = 6, 4, 5
VMEM_LIMIT = 52 * 1024 * 1024
WGRAD_VMEM = 40 * 1024 * 1024

ADAM_LR, ADAM_B1, ADAM_B2, ADAM_EPS, ADAM_WD, ADAM_STEP = 0.001, 0.9, 0.999, 1e-08, 0.01, 10


def _silu(x):
    return x * jax.nn.sigmoid(x)


def _rms(h, g):
    return h * lax.rsqrt(jnp.mean(h * h, axis=-1, keepdims=True) + EPS) * g


def _modnorm(x, g, shift, scale):
    return _rms(x, g) * (1 + scale) + shift


def _cparams(sem=None):
    return pltpu.CompilerParams(dimension_semantics=sem, vmem_limit_bytes=VMEM_LIMIT)


def _twice(v):
    return v, v


def tok(arr, width=None, cb=0, clamp=None):
    return ("tok", arr, arr.shape[1] if width is None else width, cb, clamp)


def mod(arr):
    return ("mod", arr)


def const(arr):
    return ("const", arr)


def _rowmap(name, body, n_tiles, ins, outs, *, tpb, nb, side=None):
    def modrow(i):
        return jnp.minimum(i // tpb, nb)

    in_specs, args = [], []
    for spec in ins:
        if spec[0] == "tok":
            _, arr, width, cb, clamp = spec
            if clamp is None:
                im = lambda i, cb=cb: (i, cb)
            else:
                im = lambda i, cb=cb, clamp=clamp: (jnp.minimum(i, clamp), cb)
            in_specs.append(pl.BlockSpec((TM, width), im))
        elif spec[0] == "mod":
            arr = spec[1]
            in_specs.append(pl.BlockSpec((1, 1, arr.shape[2]), lambda i: (modrow(i), 0, 0)))
        else:
            arr = spec[1]
            in_specs.append(pl.BlockSpec(arr.shape, lambda i, nd=arr.ndim: (0,) * nd))
        args.append(arr)
    out_specs, out_shapes, aliases = [], [], {}
    for o in outs:
        if o[0] == "tok":
            _, rows, width, dtype = o
            out_shapes.append(jax.ShapeDtypeStruct((rows, width), dtype))
            out_specs.append(pl.BlockSpec((TM, width), lambda i: (i, 0)))
        elif o[0] == "cols":
            buf, width, cb = o[1:4]
            first_tile = o[4] if len(o) > 4 else 0
            if not isinstance(buf, jax.ShapeDtypeStruct):
                aliases[len(args)] = len(out_shapes)
                in_specs.append(ANY)
                args.append(buf)
            out_shapes.append(jax.ShapeDtypeStruct(buf.shape, buf.dtype))
            out_specs.append(pl.BlockSpec((TM, width), lambda i, cb=cb, t0=first_tile: (i + t0, cb)))
        elif o[0] == "tok_head":
            _, rows, width, dtype, last = o
            out_shapes.append(jax.ShapeDtypeStruct((rows, width), dtype))
            out_specs.append(pl.BlockSpec((TM, width), lambda i, last=last: (jnp.minimum(i, last), 0)))
        elif o[0] == "tokT":
            _, rows, width, dtype = o
            out_shapes.append(jax.ShapeDtypeStruct((width, rows), dtype))
            out_specs.append(pl.BlockSpec((width, TM), lambda i: (0, i)))
        elif o[0] == "acc":
            _, rows, width = o
            out_shapes.append(jax.ShapeDtypeStruct((rows, width), F32))
            out_specs.append(pl.BlockSpec((rows, width), lambda i: (0, 0)))
        else:
            width = o[1]
            rows_visited = min((n_tiles - 1) // tpb, nb) + 1
            out_shapes.append(jax.ShapeDtypeStruct((rows_visited, 1, width), F32))
            out_specs.append(pl.BlockSpec((1, 1, width), lambda i: (modrow(i), 0, 0)))
    n_in = len(ins)

    def kern(*refs):
        i = pl.program_id(0)
        vals = []
        for r, spec in zip(refs[:n_in], ins):
            val = r[0] if spec[0] == "mod" else r[...]
            vals.append(val.astype(F32) if spec[0] == "tok" and val.dtype == BF16 else val)
        res = body(i, *vals)
        for r, o, val in zip(refs[len(args):], outs, res):
            if o[0] in ("tok", "cols"):
                r[...] = val.astype(r.dtype)
            elif o[0] == "tok_head":
                @pl.when(i <= o[4])
                def _():
                    r[...] = val.astype(r.dtype)
            elif o[0] == "tokT":
                r[...] = val.T.astype(r.dtype)
            elif o[0] == "acc":
                @pl.when(i == 0)
                def _():
                    r[...] = jnp.zeros(r.shape, F32)
                r[...] += jnp.broadcast_to(val, r.shape)
            else:
                first = jnp.logical_or(i == 0, modrow(i) != modrow(jnp.maximum(i - 1, 0)))

                @pl.when(first)
                def _():
                    r[...] = jnp.zeros(r.shape, F32)
                r[0] += val

    if side is not None:
        assert not aliases
        return _pallas(kern, grid=(n_tiles,), in_specs=in_specs, out_specs=out_specs, out_shape=out_shapes,
                       scratch_shapes=[], sem=("arbitrary",), name=name, args=args, side=side)
    return pl.pallas_call(
        kern, grid=(n_tiles,), in_specs=in_specs, out_specs=out_specs, out_shape=out_shapes,
        input_output_aliases=aliases, compiler_params=_cparams(("arbitrary",)), name=name)(*args)


def _pick(n, cands):
    for c in cands:
        if n % c == 0:
            return c
    return n


def _pallas(kern, *, grid, in_specs, out_specs, out_shape, scratch_shapes, sem, name, args, side=None):
    if side is None:
        return pl.pallas_call(kern, grid=grid, in_specs=in_specs, out_specs=out_specs, out_shape=out_shape,
                              scratch_shapes=scratch_shapes, compiler_params=_cparams(sem), name=name)(*args)
    single = not isinstance(out_shape, (list, tuple))
    shapes = [out_shape] if single else list(out_shape)
    ospecs = [out_specs] if single else list(out_specs)
    n_in, n_out, n_scr = len(in_specs), len(shapes), len(scratch_shapes)
    s_in, s_out = list(side["ins"]), list(side["outs"])

    def wrapped(*refs):
        pos = [0]

        def take(n):
            pos[0] += n
            return refs[pos[0] - n:pos[0]]
        ins, sins, outs, souts, scr, sems = take(n_in), take(len(s_in)), take(n_out), take(len(s_out)), take(n_scr), take(2)
        ids = [pl.program_id(k) for k in range(len(grid))]
        first = functools.reduce(jnp.logical_and, [i == 0 for i in ids])
        last = functools.reduce(jnp.logical_and, [i == g - 1 for i, g in zip(ids, grid)])
        copy = _remote(*sems)

        @pl.when(first)
        def _():
            side["start"](sins, souts, copy)
        kern(*ins, *outs, *scr)

        @pl.when(last)
        def _():
            side["finish"](sins, souts, copy)

    res = pl.pallas_call(
        wrapped, grid=grid, in_specs=list(in_specs) + [ANY] * len(s_in), out_specs=ospecs + [ANY] * len(s_out),
        out_shape=shapes + s_out, scratch_shapes=list(scratch_shapes) + _sems(side["nsem"]),
        input_output_aliases={n_in + a: n_out + b for a, b in side.get("alias", {}).items()},
        compiler_params=_cparams(("arbitrary",) * len(grid)), name=name)(*args, *s_in)
    main = res[:n_out]
    return (main[0] if single else main), list(res[n_out:])


def _mm(name, a, b, *, trans_b=False, out_dtype=F32, a_fn=None, bias=None, rows=None, side=None, residual=None):
    m, k = a.shape if a.ndim == 2 else (a.shape[1], 2 * a.shape[2])
    m = m if rows is None else rows
    shard = b.shape[2] if b.ndim == 3 else None
    if trans_b:
        n = b.shape[-2]
        tk = _pick(shard, (2816, 2304, 1408, 1024)) if shard else (
            k if k <= 2816 else _pick(k, (2816, 2432, 2304, 2048, 1536, 1408, 1024, 512, 256, 128)))
        tn = _pick(n, (1024, 512, 384, 256, 128))
    else:
        n = 4 * shard if shard else b.shape[1]
        tk = k if k <= 2816 else _pick(k, (2816, 2432, 2304, 2048, 1536, 1408, 1024, 512, 256, 128))
        tn = _pick(shard, (512, 384, 1408, 256, 128)) if shard else _pick(n, (1024, 2432, 512, 384, 256, 128))
    if residual is not None:
        tm = next(c for c in (512, 256) if m % c == 0 and (residual["tpb"] * TM) % c == 0)
    else:
        tm = _pick(m, (1024, 512, 256))
    nk = k // tk
    per = shard // (tk if trans_b else tn) if shard else None
    dims = (((1,), (1,)), ((), ())) if trans_b else (((1,), (0,)), ((), ()))

    def kern(*refs):
        a_ref, b_ref = refs[0], refs[1]
        bias_ref = refs[2] if bias is not None else None
        acc_ref = refs[-1]
        if residual is not None:
            x_ref, gate_ref, o_ref, xo_ref = refs[-5:-1]
        else:
            o_ref = refs[-2]
        kk = pl.program_id(2)
        av = a_ref[...]
        if a_fn is not None:
            av = a_fn(av)
        p = lax.dot_general(av.astype(BF16), b_ref[...].astype(BF16), dims, preferred_element_type=F32)

        def finish(total):
            if bias_ref is not None:
                total = total + bias_ref[...]
            o_ref[...] = total.astype(o_ref.dtype)
            if residual is not None:
                xo_ref[...] = x_ref[...] + residual["scale"] * gate_ref[0] * total

        if nk == 1:
            finish(p)
        else:
            @pl.when(kk == 0)
            def _():
                acc_ref[...] = p

            @pl.when(kk > 0)
            def _():
                acc_ref[...] += p

            @pl.when(kk == nk - 1)
            def _():
                finish(acc_ref[...])

    if shard and trans_b:
        b_spec = pl.BlockSpec((None, tn, tk), lambda i, j, kk: (kk // per, j, kk % per))
    elif shard:
        b_spec = pl.BlockSpec((None, tk, tn), lambda i, j, kk: (j // per, kk, j % per))
    elif trans_b:
        b_spec = pl.BlockSpec((tn, tk), lambda i, j, kk: (j, kk))
    else:
        b_spec = pl.BlockSpec((tk, tn), lambda i, j, kk: (kk, j))
    if a.ndim == 3:
        pa = a.shape[2] // tk
        a_spec = pl.BlockSpec((None, tm, tk), lambda i, j, kk: (kk // pa, i, kk % pa))
    else:
        a_spec = pl.BlockSpec((tm, tk), lambda i, j, kk: (i, kk))
    in_specs = [a_spec, b_spec]
    args = [a, b]
    if bias is not None:
        in_specs.append(pl.BlockSpec((1, tn), lambda i, j, kk: (0, j)))
        args.append(bias)
    out_specs = pl.BlockSpec((tm, tn), lambda i, j, kk: (i, j))
    out_shape = jax.ShapeDtypeStruct((m, n), out_dtype)
    if residual is not None:
        tiles, nb = residual["tpb"] * TM // tm, residual["gate"].shape[0] - 1
        in_specs += [out_specs, pl.BlockSpec((1, 1, tn), lambda i, j, kk: (jnp.minimum(i // tiles, nb), 0, j))]
        args += [residual["x"], residual["gate"]]
        out_specs, out_shape = [out_specs, out_specs], [out_shape, jax.ShapeDtypeStruct((m, n), F32)]
    return _pallas(
        kern, grid=(m // tm, n // tn, nk), in_specs=in_specs, out_specs=out_specs, out_shape=out_shape,
        scratch_shapes=[pltpu.VMEM((tm, tn) if nk > 1 else (8, 128), F32)],
        sem=("parallel", "parallel", "arbitrary"), name=name, args=args, side=side)


def _mm_tn(name, x, dy):
    t, k1, n1 = x.shape[0], x.shape[1], dy.shape[1]
    tt = _pick(t, (512, 256, 128, 64, 8))
    tk1 = _pick(k1, (1024, 512, 256, 128))
    tn = _pick(n1, (512, 384, 256, 128))
    ns = t // tt

    def kern(x_ref, dy_ref, o_ref, acc_ref):
        s = pl.program_id(2)
        p = lax.dot_general(x_ref[...].astype(BF16), dy_ref[...].astype(BF16), (((0,), (0,)), ((), ())),
                            preferred_element_type=F32)

        @pl.when(s == 0)
        def _():
            acc_ref[...] = p

        @pl.when(s > 0)
        def _():
            acc_ref[...] += p

        @pl.when(s == ns - 1)
        def _():
            o_ref[...] = acc_ref[...].astype(o_ref.dtype)

    return pl.pallas_call(
        kern, grid=(k1 // tk1, n1 // tn, ns),
        in_specs=[pl.BlockSpec((tt, tk1), lambda i, j, s: (s, i)), pl.BlockSpec((tt, tn), lambda i, j, s: (s, j))],
        out_specs=pl.BlockSpec((tk1, tn), lambda i, j, s: (i, j)), out_shape=jax.ShapeDtypeStruct((k1, n1), F32),
        scratch_shapes=[pltpu.VMEM((tk1, tn), F32)],
        compiler_params=_cparams(("parallel", "parallel", "arbitrary")), name=name)(x, dy)


def _wgrad(name, xt, dy, rows=None, col_shards=False, side=None):
    k1 = xt.shape[0]
    t = xt.shape[1] if rows is None else rows
    n1 = dy.shape[1] if dy.ndim == 2 else 2 * dy.shape[2]
    tn = _pick(n1 // 4, (1408, 512, 384, 256, 128)) if col_shards else _pick(n1, (1024, 2432, 512, 384, 256, 128))

    def token_tile(tm):
        fixed = tm * tn * (4 + 4 + 2 * 2)
        return next((c for c in (2048, 1536, 1024, 512, 256, 128)
                     if t % c == 0 and fixed + 4 * c * (tm + tn) <= WGRAD_VMEM), 128)
    tm = next((c for c in (1024, 1408, 512, 256) if k1 % c == 0 and token_tile(c) >= 1024),
              _pick(k1, (1024, 1408, 512, 256)))
    tk = token_tile(tm)
    ns = t // tk
    per = n1 // 4 // tn

    def kern(x_ref, dy_ref, o_ref, acc_ref):
        s = pl.program_id(2)
        p = jnp.dot(x_ref[...], dy_ref[...], preferred_element_type=F32)

        @pl.when(s == 0)
        def _():
            acc_ref[...] = p

        @pl.when(s > 0)
        def _():
            acc_ref[...] += p

        @pl.when(s == ns - 1)
        def _():
            o_ref[...] = acc_ref[...].astype(o_ref.dtype)

    if dy.ndim == 3:
        pd = dy.shape[2] // tn
        dy_spec = pl.BlockSpec((None, tk, tn), lambda i, j, s: (j // pd, s, j % pd))
    else:
        dy_spec = pl.BlockSpec((tk, tn), lambda i, j, s: (s, j))
    if col_shards:
        out_spec = pl.BlockSpec((None, tm, tn), lambda i, j, s: (j // per, i, j % per))
        out_shape = jax.ShapeDtypeStruct((4, k1, n1 // 4), BF16)
    else:
        out_spec = pl.BlockSpec((tm, tn), lambda i, j, s: (i, j))
        out_shape = jax.ShapeDtypeStruct((k1, n1), BF16)
    return _pallas(
        kern, grid=(k1 // tm, n1 // tn, ns),
        in_specs=[pl.BlockSpec((tm, tk), lambda i, j, s: (i, s)), dy_spec],
        out_specs=out_spec, out_shape=out_shape, scratch_shapes=[pltpu.VMEM((tm, tn), F32)],
        sem=("parallel", "parallel", "arbitrary"), name=name, args=[xt, dy], side=side)


def _swiglu_fwd(name, u, w_gu, side=None):
    m = u.shape[0]
    half = w_gu.shape[2]
    tm = _pick(m, (512, 256))

    def kern(u_ref, wa_ref, wb_ref, ab_ref, hm_ref, hmt_ref):
        uv = u_ref[...]
        a = jnp.dot(uv, wa_ref[...], preferred_element_type=F32)
        b = jnp.dot(uv, wb_ref[...], preferred_element_type=F32)
        s = jax.nn.sigmoid(a)
        silu_a = a * s
        ab_ref[0] = (b * (s * (1 + a * (1 - s)))).astype(BF16)
        ab_ref[1] = silu_a.astype(BF16)
        hm = (silu_a * b).astype(BF16)
        hm_ref[...] = hm
        hmt_ref[...] = hm.T

    return _pallas(
        kern, grid=(2, m // tm),
        in_specs=[pl.BlockSpec((tm, D), lambda j, i: (i, 0)),
                  pl.BlockSpec((None, D, half), lambda j, i: (j, 0, 0)),
                  pl.BlockSpec((None, D, half), lambda j, i: (2 + j, 0, 0))],
        out_specs=[pl.BlockSpec((2, tm, half), lambda j, i: (0, i, j)),
                   pl.BlockSpec((tm, half), lambda j, i: (i, j)),
                   pl.BlockSpec((half, tm), lambda j, i: (j, i))],
        out_shape=[jax.ShapeDtypeStruct((2, m, FF), BF16), jax.ShapeDtypeStruct((m, FF), BF16),
                   jax.ShapeDtypeStruct((FF, m), BF16)],
        scratch_shapes=[], sem=("parallel", "parallel"), name=name, args=[u, w_gu, w_gu], side=side)


def _swiglu_bwd(name, df, w_down, ab):
    m = df.shape[0]
    half = FF // 2
    tm = _pick(m, (512, 256))

    def kern(df_ref, w_ref, ab_ref, o_ref):
        dh = lax.dot_general(df_ref[...], w_ref[...], (((1,), (1,)), ((), ())), preferred_element_type=F32)
        o_ref[0] = (dh * ab_ref[0].astype(F32)).astype(BF16)
        o_ref[1] = (dh * ab_ref[1].astype(F32)).astype(BF16)

    return pl.pallas_call(
        kern, grid=(2, m // tm),
        in_specs=[pl.BlockSpec((tm, D), lambda j, i: (i, 0)),
                  pl.BlockSpec((half, D), lambda j, i: (j, 0)),
                  pl.BlockSpec((2, tm, half), lambda j, i: (0, i, j))],
        out_specs=pl.BlockSpec((2, tm, half), lambda j, i: (0, i, j)),
        out_shape=jax.ShapeDtypeStruct((2, m, FF), BF16),
        compiler_params=_cparams(("parallel", "parallel")), name=name)(df, w_down, ab)


def _gla_maps(bl, t, tc):
    nx, nc = t // CHUNK, tc // CHUNK
    nxb = bl * nx

    def rowblk(d, b, n):
        c_ctx = jnp.where(d == 0, n, nc - 1 - n)
        c_x = jnp.where(d == 0, n - nc, nx - 1 - (n - nc))
        return jnp.where(n < nc, nxb + b * nc + c_ctx, b * nx + c_x)

    def xblk(d, b, n):
        n2 = jnp.maximum(n, nc)
        return b * nx + jnp.where(d == 0, n2 - nc, nx - 1 - (n2 - nc))

    return nx, nc, rowblk, xblk


def _dot01(m, x, cm):
    x1 = x.astype(BF16)
    r1 = x - x1.astype(F32)
    x2 = r1.astype(BF16)
    x3 = (r1 - x2.astype(F32)).astype(BF16)
    w = x.shape[1]
    p = lax.dot_general(m.astype(BF16), jnp.concatenate([x1, x2, x3], axis=1), (((cm,), (0,)), ((), ())),
                        preferred_element_type=F32)
    return p[:, :w] + p[:, w:2 * w] + p[:, 2 * w:]


def _gla_chunk(m, q, k, v, g, h):
    gh = g[:, h * DK:(h + 1) * DK]
    b = _dot01(m, gh, 1)
    tot = jnp.sum(gh, axis=0, keepdims=True)
    mid = b[CHUNK // 2:CHUNK // 2 + 1, :]
    qh = q[:, h * DK:(h + 1) * DK] * Q_SCALE
    kh = k[:, h * DK:(h + 1) * DK]
    vh = v[:, h * DV:(h + 1) * DV]
    return b, tot, mid, qh, kh, vh


def _dot(a, b, ca, cb):
    return lax.dot_general(a.astype(BF16), b.astype(BF16), (((ca,), (cb,)), ((), ())),
                           preferred_element_type=F32)


def _gla_fwd(p, g2, mmats, bl, t, tc):
    nx, nc, rowblk, xblk = _gla_maps(bl, t, tc)
    ns = nx + nc

    def kern(q0, k0, v0, g0, q1, k1, v1, g1, m_ref, o0, o1, ss_ref, s_ref):
        n = pl.program_id(1)

        @pl.when(n == 0)
        def _():
            s_ref[...] = jnp.zeros(s_ref.shape, F32)
        sides = ((q0, k0, v0, g0, o0), (q1, k1, v1, g1, o1))
        loaded = [(m_ref[d], q_ref[...].astype(F32), k_ref[...].astype(F32), v_ref[...].astype(F32), g_ref[...])
                  for d, (q_ref, k_ref, v_ref, g_ref, _) in enumerate(sides)]
        chains = [(d, h) for d in range(2) for h in range(HEADS)]
        base = [_gla_chunk(*loaded[d], h) for d, h in chains]
        pre = []
        for (d, h), (b, tot, mid, qh, kh, vh) in zip(chains, base):
            s0 = s_ref[d, h * DV:(h + 1) * DV, :]
            ss_ref[d, 0, 0, h * DV:(h + 1) * DV, :] = s0.astype(BF16)
            pre.append((s0, kh * jnp.exp(tot - b), qh * jnp.exp(b), qh * jnp.exp(b - mid), kh * jnp.exp(mid - b)))
        raw = [(_dot(qm, km, 1, 1), _dot(qe, s0, 1, 1), _dot(bs[5], kl, 0, 0))
               for bs, (s0, kl, qe, qm, km) in zip(base, pre)]
        for (d, h), bs, (s0, kl, qe, qm, km), (att_raw, inter, s_add) in zip(chains, base, pre, raw):
            s_ref[d, h * DV:(h + 1) * DV, :] = s0 * jnp.exp(bs[1]) + s_add
            sides[d][4][:, h * DV:(h + 1) * DV] = (inter + _dot(loaded[d][0] * att_raw, bs[5], 1, 0)).astype(BF16)

    def operands(d):
        return [pl.BlockSpec((CHUNK, 512), lambda b, n: (rowblk(d, b, n), 4)),
                pl.BlockSpec((CHUNK, 512), lambda b, n: (rowblk(d, b, n), 5)),
                pl.BlockSpec((CHUNK, 1024), lambda b, n: (rowblk(d, b, n), 3)),
                pl.BlockSpec((CHUNK, 512), lambda b, n: (rowblk(d, b, n), d))]

    o_shape = jax.ShapeDtypeStruct((bl * t, HEADS * DV), BF16)
    return pl.pallas_call(
        kern, grid=(bl, ns),
        in_specs=operands(0) + operands(1) + [pl.BlockSpec((2, CHUNK, CHUNK), lambda b, n: (0, 0, 0))],
        out_specs=[pl.BlockSpec((CHUNK, 1024), lambda b, n: (xblk(0, b, n), 0)),
                   pl.BlockSpec((CHUNK, 1024), lambda b, n: (xblk(1, b, n), 0)),
                   pl.BlockSpec((2, 1, 1, HEADS * DV, DK), lambda b, n: (0, b, n, 0, 0))],
        out_shape=[o_shape, o_shape, jax.ShapeDtypeStruct((2, bl, ns, HEADS * DV, DK), BF16)],
        scratch_shapes=[pltpu.VMEM((2, HEADS * DV, DK), F32)],
        compiler_params=_cparams(("parallel", "arbitrary")), name="gla_fwd")(p, p, p, g2, p, p, p, g2, mmats)


def _gla_bwd(p, g2, mmats, ssave, do, bl, t, tc):
    nx, nc, rowblk, xblk = _gla_maps(bl, t, tc)
    ns = nx + nc
    rev = lambda s: ns - 1 - s

    def kern(q0, k0, v0, g0, do0, q1, k1, v1, g1, do1, m_ref, ss_ref,
             dq0, dk0, dv0, dg0, dq1, dk1, dv1, dg1, ds_ref):
        step = pl.program_id(1)
        n = ns - 1 - step

        @pl.when(step == 0)
        def _():
            ds_ref[...] = jnp.zeros(ds_ref.shape, F32)
        live = (n >= nc).astype(F32)
        sides = ((q0, k0, v0, g0, do0, dq0, dk0, dv0, dg0), (q1, k1, v1, g1, do1, dq1, dk1, dv1, dg1))
        loaded = [(m_ref[d], s[0][...].astype(F32), s[1][...].astype(F32), s[2][...].astype(F32), s[3][...])
                  for d, s in enumerate(sides)]
        dovs = [s[4][...] * live for s in sides]
        chains = [(d, h) for d in range(2) for h in range(HEADS)]
        base = [_gla_chunk(*loaded[d], h) for d, h in chains]
        pre = []
        for (d, h), (b, tot, mid, qh, kh, vh) in zip(chains, base):
            eb, ebm, emb, etb = jnp.exp(b), jnp.exp(b - mid), jnp.exp(mid - b), jnp.exp(tot - b)
            pre.append(dict(
                eb=eb, ebm=ebm, emb=emb, etb=etb, etot=jnp.exp(tot), qe=qh * eb, qm=qh * ebm, km=kh * emb, kl=kh * etb,
                vh=vh, doh=dovs[d][:, h * DV:(h + 1) * DV], s0=ss_ref[d, 0, 0, h * DV:(h + 1) * DV, :].astype(F32),
                ds1=ds_ref[d, h * DV:(h + 1) * DV, :]))
        first = [dict(att=_dot(c["qm"], c["km"], 1, 1), datt=_dot(c["doh"], c["vh"], 1, 1),
                      dqe=_dot(c["doh"], c["s0"], 1, 0), ds_add=_dot(c["doh"], c["qe"], 0, 0),
                      dkl=_dot(c["vh"], c["ds1"], 1, 0), dv_s=_dot(c["kl"], c["ds1"], 1, 1)) for c in pre]
        second = []
        for (d, h), c, f in zip(chains, pre, first):
            m = loaded[d][0]
            ds_ref[d, h * DV:(h + 1) * DV, :] = c["ds1"] * c["etot"] + f["ds_add"]
            att, datt = m * f["att"], m * f["datt"]
            second.append(dict(dqm=_dot(datt, c["km"], 1, 0), dkm=_dot(datt, c["qm"], 0, 0),
                               dv_a=_dot(att, c["doh"], 0, 0)))
        for (d, h), c, f, s in zip(chains, pre, first, second):
            dq_ref, dk_ref, dv_ref, dg_ref = sides[d][5:]
            dtot = c["etot"] * jnp.sum(c["ds1"] * c["s0"], axis=0, keepdims=True) + jnp.sum(
                f["dkl"] * c["kl"], axis=0, keepdims=True)
            db = f["dqe"] * c["qe"] + s["dqm"] * c["qm"] - s["dkm"] * c["km"] - f["dkl"] * c["kl"]
            dq_ref[:, h * DK:(h + 1) * DK] = ((f["dqe"] * c["eb"] + s["dqm"] * c["ebm"]) * Q_SCALE).astype(BF16)
            dk_ref[:, h * DK:(h + 1) * DK] = (s["dkm"] * c["emb"] + f["dkl"] * c["etb"]).astype(BF16)
            dv_ref[:, h * DV:(h + 1) * DV] = (s["dv_a"] + f["dv_s"]).astype(BF16)
            dg_ref[:, h * DK:(h + 1) * DK] = _dot01(loaded[d][0], db, 0) + dtot

    nt = p.shape[0]

    def operands(d):
        return [pl.BlockSpec((CHUNK, 512), lambda b, s: (rowblk(d, b, rev(s)), 4)),
                pl.BlockSpec((CHUNK, 512), lambda b, s: (rowblk(d, b, rev(s)), 5)),
                pl.BlockSpec((CHUNK, 1024), lambda b, s: (rowblk(d, b, rev(s)), 3)),
                pl.BlockSpec((CHUNK, 512), lambda b, s: (rowblk(d, b, rev(s)), d)),
                pl.BlockSpec((CHUNK, 1024), lambda b, s: (xblk(d, b, rev(s)), 0))]

    def results(d):
        row = lambda b, s: (rowblk(d, b, rev(s)), 0)
        return [pl.BlockSpec((CHUNK, 512), row), pl.BlockSpec((CHUNK, 512), row), pl.BlockSpec((CHUNK, 1024), row),
                pl.BlockSpec((CHUNK, 512), row)]

    shapes = [jax.ShapeDtypeStruct((nt, 512), BF16), jax.ShapeDtypeStruct((nt, 512), BF16),
              jax.ShapeDtypeStruct((nt, 1024), BF16), jax.ShapeDtypeStruct((nt, 512), F32)]
    out = pl.pallas_call(
        kern, grid=(bl, ns),
        in_specs=operands(0) + operands(1) + [
            pl.BlockSpec((2, CHUNK, CHUNK), lambda b, s: (0, 0, 0)),
            pl.BlockSpec((2, 1, 1, HEADS * DV, DK), lambda b, s: (0, b, rev(s), 0, 0))],
        out_specs=results(0) + results(1), out_shape=shapes + shapes,
        scratch_shapes=[pltpu.VMEM((2, HEADS * DV, DK), F32)],
        compiler_params=_cparams(("parallel", "arbitrary")), name="gla_bwd")(
            p, p, p, g2, do, p, p, p, g2, do, mmats, ssave)
    return out[:4], out[4:]


CONV_CT = 256
CONV_PAD = 16
CONV_RC = 128
CONV_HALO = 24


def _conv_fill(zp, z_ref, t):
    zp[0:CONV_PAD, :] = jnp.zeros((CONV_PAD, CONV_CT), F32)
    zp[CONV_PAD + t:2 * CONV_PAD + t, :] = jnp.zeros((CONV_PAD, CONV_CT), F32)
    zp[CONV_PAD:CONV_PAD + t, :] = z_ref[...]


def _dwconv(name, z, w, bias, bl, t, flip):
    def kern(z_ref, w_ref, b_ref, o_ref, zp):
        _conv_fill(zp, z_ref, t)
        offs = [(CONV_W - j) if flip else (j + 1) for j in range(CONV_W)]
        for r in range(0, t, CONV_RC):
            acc = jnp.broadcast_to(b_ref[...], (CONV_RC, CONV_CT))
            for rot in range(8):
                win = zp[r + rot:r + rot + CONV_RC + CONV_HALO, :]
                for j in range(CONV_W):
                    if offs[j] % 8 == rot:
                        a = offs[j] - rot
                        acc = acc + w_ref[j:j + 1, :] * win[a:a + CONV_RC, :]
            o_ref[r:r + CONV_RC, :] = acc

    return pl.pallas_call(
        kern, grid=(bl, 1024 // CONV_CT),
        in_specs=[pl.BlockSpec((t, CONV_CT), lambda b, c: (b, c)),
                  pl.BlockSpec((32, CONV_CT), lambda b, c: (0, c)),
                  pl.BlockSpec((1, CONV_CT), lambda b, c: (0, c))],
        out_specs=pl.BlockSpec((t, CONV_CT), lambda b, c: (b, c)),
        out_shape=jax.ShapeDtypeStruct(z.shape, F32),
        scratch_shapes=[pltpu.VMEM((t + 2 * CONV_PAD, CONV_CT), F32)],
        compiler_params=_cparams(("parallel", "parallel")), name=name)(z, w, bias)


def _dwconv_wgrad(z, dzc, bl, t):
    def kern(z_ref, d_ref, dw_ref, db_ref, zp):
        b = pl.program_id(1)

        @pl.when(b == 0)
        def _():
            dw_ref[...] = jnp.zeros(dw_ref.shape, F32)
            db_ref[...] = jnp.zeros(db_ref.shape, F32)
        _conv_fill(zp, z_ref, t)
        for rot in range(8):
            taps = [j for j in range(CONV_W) if (j + 1) % 8 == rot]
            accs = [jnp.zeros((8, CONV_CT), F32) for _ in taps]
            for r in range(0, t, CONV_RC):
                d = d_ref[r:r + CONV_RC, :]
                win = zp[r + rot:r + rot + CONV_RC + CONV_HALO, :]
                for k, j in enumerate(taps):
                    a = j + 1 - rot
                    prod = d * win[a:a + CONV_RC, :]
                    accs[k] = accs[k] + jnp.sum(prod.reshape(CONV_RC // 8, 8, CONV_CT), axis=0)
            for k, j in enumerate(taps):
                dw_ref[j:j + 1, :] += jnp.sum(accs[k], axis=0, keepdims=True)
        db_ref[...] += jnp.sum(d_ref[...], axis=0, keepdims=True)

    return pl.pallas_call(
        kern, grid=(1024 // CONV_CT, bl),
        in_specs=[pl.BlockSpec((t, CONV_CT), lambda c, b: (b, c)),
                  pl.BlockSpec((t, CONV_CT), lambda c, b: (b, c))],
        out_specs=[pl.BlockSpec((32, CONV_CT), lambda c, b: (0, c)),
                   pl.BlockSpec((1, CONV_CT), lambda c, b: (0, c))],
        out_shape=[jax.ShapeDtypeStruct((32, 1024), F32), jax.ShapeDtypeStruct((1, 1024), F32)],
        scratch_shapes=[pltpu.VMEM((t + 2 * CONV_PAD, CONV_CT), F32)],
        compiler_params=_cparams(("parallel", "arbitrary")), name="dwconv_wgrad")(z, dzc)


def _ffn_fwd(tag, xin, n_tiles, g, sh, sc, gate, wts, gu_name, down_name, tpb, nb, comm=None):
    rows = n_tiles * TM
    rm = functools.partial(_rowmap, tpb=tpb, nb=nb)
    u, ut = _with_side(
        comm, tag + "_norm", None,
        lambda s: rm(tag + "_norm", lambda i, x, g_, sh_, sc_: _twice(_modnorm(x, g_, sh_, sc_)), n_tiles,
                     [tok(xin), const(g), mod(sh), mod(sc)], [("tok", rows, D, BF16), ("tokT", rows, D, BF16)], side=s))
    w_gu, w_down = wts[gu_name], wts[down_name]
    ab, hm, hmt = _with_side(comm, tag + "_gu", None, lambda s: _swiglu_fwd(tag + "_gu", u, w_gu, side=s))
    res = dict(x=xin, gate=gate, scale=0.5, tpb=tpb)
    f, xout = _with_side(comm, tag + "_down", None,
                         lambda s: _mm(tag + "_down", hm, w_down, out_dtype=BF16, side=s, residual=res))
    return xout, (ut, ab, hmt, f)


def _ffn_bwd(tag, xin, saved, dxout, dx_clamp, n_tiles, g, sh, sc, gate, w_gu, w_down, tpb, nb, comm=None, grads=None,
             names=None, keep_tiles=None):
    ut, ab, hmt, f = saved
    rows = n_tiles * TM
    rm = functools.partial(_rowmap, tpb=tpb, nb=nb)

    def mask(i):
        return 1.0 if dx_clamp is None else (i <= dx_clamp).astype(F32)

    def b1(i, dx, f_, gt):
        dx = dx * mask(i)
        return (0.5 * gt * dx, jnp.sum(0.5 * f_ * dx, axis=0, keepdims=True))
    df, dgate = rm(tag + "_bres", b1, n_tiles, [tok(dxout, clamp=dx_clamp), tok(f), mod(gate)],
                   [("tok", rows, D, BF16), ("modacc", D)])
    grads[names[1]] = _with_side(comm, tag + "_wdown", grads, lambda s: _wgrad(tag + "_wdown", hmt, df, side=s))
    dab = _swiglu_bwd(tag + "_bdown", df, w_down, ab)
    grads[names[0]] = _with_side(comm, tag + "_wgu", grads,
                                 lambda s: _wgrad(tag + "_wgu", ut, dab, col_shards=True, side=s))
    du = _with_side(comm, tag + "_bgu", grads, lambda s: _mm(tag + "_bgu", dab, w_gu, trans_b=True, side=s))

    def b3(i, x, g_, sh_, sc_, du_, dx):
        _, vjp = jax.vjp(_modnorm, x, g_, sh_, sc_)
        dxn, dg, dsh, dsc = vjp(du_)
        return (dx * mask(i) + dxn, dg, dsh, dsc)
    dx_out = ("tok", rows, D, F32) if keep_tiles is None else ("tok_head", keep_tiles * TM, D, F32, keep_tiles - 1)
    dxin, dg, dsh, dsc = rm(tag + "_bnorm", b3, n_tiles,
                            [tok(xin), const(g), mod(sh), mod(sc), tok(du), tok(dxout, clamp=dx_clamp)],
                            [dx_out, ("acc", 1, D), ("modacc", D), ("modacc", D)])
    return dxin, dict(g=dg, sh=dsh, sc=dsc, gate=dgate)


IN_SEGMENTS = ((0, 4096), (5152, 7200), (4096, 5120), (5120, 5152))


def _perm_in_cols(shards):
    width = shards.shape[2]
    parts = []
    for a, b in IN_SEGMENTS:
        for j in range(4):
            lo, hi = max(a, width * j), min(b, width * (j + 1))
            if lo < hi:
                parts.append(shards[j][:, lo - width * j:hi - width * j])
    parts.append(jnp.zeros((shards.shape[1], D_INP - D_IN), shards.dtype))
    return jnp.concatenate(parts, axis=1)


def _unperm_in_cols(w):
    width = D_IN // 4
    shards = []
    for j in range(4):
        parts, start = [], 0
        for a, b in IN_SEGMENTS:
            lo, hi = max(a, width * j), min(b, width * (j + 1))
            if lo < hi:
                parts.append((lo, w[:, start + lo - a:start + hi - a]))
            start += b - a
        shards.append(jnp.concatenate([p for _, p in sorted(parts, key=lambda q: q[0])], axis=1))
    return jnp.stack(shards)


def _local_step(x, c, ctx, target, wts, comm):
    bl, t, _ = x.shape
    tc = ctx.shape[1]
    nx_rows, nc_rows = bl * t, bl * tc
    nt_rows = nx_rows + nc_rows
    tpb = t // TM
    nxt, ntt = nx_rows // TM, nt_rows // TM
    nb = bl
    rm = functools.partial(_rowmap, tpb=tpb, nb=nb)
    last_x = nxt - 1

    x0 = jnp.concatenate([x.reshape(nx_rows, D), ctx.reshape(nc_rows, D)], axis=0)
    tgt = target.reshape(nx_rows, D)

    cc = jnp.concatenate([c, wts["c_ctx"].reshape(1, D), jnp.zeros((8 - bl - 1, D), F32)], axis=0)
    modv, cc_all = _mod_forward(cc, wts["w_mod_shard"], wts["b_mod_shard"])
    mods = [modv[:nb + 1, k * D:(k + 1) * D].reshape(nb + 1, 1, D) for k in range(9)]

    x1, sv1 = _ffn_fwd("ffn1", x0, ntt, wts["g_ffn1"], mods[0], mods[1], mods[2], wts, "w1_gu", "w1_down",
                       tpb, nb, comm)
    u2, u2t = rm("in_norm", lambda i, x_, g_, sh_, sc_: _twice(_modnorm(x_, g_, sh_, sc_)), ntt,
                 [tok(x1), const(wts["g_mix"]), mod(mods[3]), mod(mods[4])],
                 [("tok", nt_rows, D, BF16), ("tokT", nt_rows, D, BF16)])
    w_inp = wts["w_in_p"]
    p = _with_side(comm, "in_proj", None, lambda s: _mm("in_proj", u2, w_inp, out_dtype=BF16, side=s))

    waf, wab, baf, bab = wts["w_alpha_f_pad"], wts["w_alpha_b_pad"], wts["b_alpha_f"], wts["b_alpha_b"]

    def dec_fwd(i, lr, wf, wb, bf_, bb_):
        zf = _dot(lr, wf, 1, 0) + bf_
        zb = _dot(lr, wb, 1, 0) + bb_
        return (jnp.concatenate([jax.nn.log_sigmoid(zf) / TAU, jax.nn.log_sigmoid(zb) / TAU], axis=1),)
    (gfb,) = rm("decay_fwd", dec_fwd, ntt, [tok(p, 128, LR_COL // 128), const(waf), const(wab), const(baf), const(bab)],
                [("tok", nt_rows, 1024, F32)])
    g2 = gfb
    tri = jnp.tril(jnp.ones((CHUNK, CHUNK), F32))
    mmats = jnp.stack([tri, tri.T])
    *o2, ssave = _gla_fwd(p, g2, mmats, bl, t, tc)

    gn_g = wts["gla_norm_g"]

    def gla_out(of, ob, og, gn):
        o = of + ob
        parts = []
        for h in range(HEADS):
            oh = o[:, h * DV:(h + 1) * DV]
            parts.append(oh * lax.rsqrt(jnp.mean(oh * oh, axis=-1, keepdims=True) + EPS))
        return jnp.concatenate(parts, axis=1) * gn * _silu(og)
    yg_in, yg_int = rm("gla_out", lambda i, of, ob, og, gn: _twice(gla_out(of, ob, og, gn)), nxt,
                       [tok(o2[0]), tok(o2[1]), tok(p, 1024, OG_CB), const(gn_g)],
                       [("tok", nx_rows, D, BF16), ("tokT", nx_rows, D, BF16)])
    y_gla = _with_side(comm, "gla_proj", None,
                       lambda s: _mm("gla_proj", yg_in, wts["w_gla_out"], out_dtype=BF16, side=s))

    (z,) = rm("glu", lambda i, a, b: (a * jax.nn.sigmoid(b),), nxt, [tok(p, 1024, 0), tok(p, 1024, 1)],
              [("tok", nx_rows, D, F32)])
    dw_w = jnp.concatenate([wts["dw_weight"], jnp.zeros((1, D), F32)], axis=0)
    zc = _dwconv("dwconv_fwd", z, dw_w, wts["dw_bias"], bl, t, False)

    def ln_silu(zc_, g_, b_):
        mu = jnp.mean(zc_, axis=-1, keepdims=True)
        var = jnp.mean(jnp.square(zc_ - mu), axis=-1, keepdims=True)
        return _silu((zc_ - mu) * lax.rsqrt(var + EPS) * g_ + b_)
    ln_g, ln_b = wts["conv_ln_g"], wts["conv_ln_b"]
    zl, zlt = rm("conv_ln", lambda i, zc_, g_, b_: _twice(ln_silu(zc_, g_, b_)), nxt,
                 [tok(zc), const(ln_g), const(ln_b)], [("tok", nx_rows, D, BF16), ("tokT", nx_rows, D, BF16)])
    y_conv = _mm("conv_proj", zl, wts["w_conv_out"], out_dtype=BF16)

    mg, mgt = rm("merge", lambda i, ga, gb, yc, yg: _twice(jax.nn.sigmoid(ga) * yc + jax.nn.sigmoid(gb) * yg), nxt,
                 [tok(p, 1024, GA_CB), tok(p, 1024, GB_CB), tok(y_conv), tok(y_gla)],
                 [("tok", nx_rows, D, BF16), ("tokT", nx_rows, D, BF16)])
    mix, x2 = _mm("out_proj", mg, wts["w_out"], out_dtype=BF16,
                  residual=dict(x=x1, gate=mods[5], scale=1.0, tpb=tpb))

    x3, sv2 = _ffn_fwd("ffn2", x2, nxt, wts["g_ffn2"], mods[6], mods[7], mods[8], wts, "w2_gu", "w2_down",
                       tpb, nb)
    g_fin = wts["g_final"].reshape(1, D)

    def head(i, x_, g_, tg):
        y, vjp = jax.vjp(_rms, x_, g_)
        diff = y - tg
        dx, dg = vjp(diff * (1.0 / D))
        loss = 0.5 * jnp.sum(jnp.mean(diff * diff, axis=-1, keepdims=True))
        return dx, dg, loss
    dx3, dg_final, loss_acc = rm("loss_head", head, nxt, [tok(x3), const(g_fin), tok(tgt)],
                                 [("tok", nx_rows, D, F32), ("acc", 1, D), ("acc", 8, 128)])
    loss = loss_acc[0, 0]

    grads = {}
    dx2, gf2 = _ffn_bwd("ffn2", x2, sv2, dx3, None, nxt, wts["g_ffn2"], mods[6], mods[7], mods[8],
                        wts["w2_gu"], wts["w2_down"], tpb, nb, comm, grads, ("w2_gu", "w2_down"))
    grads["g_ffn2"] = gf2["g"]

    dmix, dgate5 = rm("mix_bres", lambda i, dx, mx_, gt: (gt * dx, jnp.sum(mx_ * dx, axis=0, keepdims=True)), nxt,
                      [tok(dx2), tok(mix), mod(mods[5])], [("tok", nx_rows, D, BF16), ("modacc", D)])
    dmg = _mm("out_bproj", dmix, wts["w_out"], trans_b=True, out_dtype=BF16)
    grads["w_out"] = _wgrad("out_wgrad", mgt, dmix)

    def merge_bwd(i, dm, ga, gb, yc, yg):
        keep = (i <= last_x).astype(F32)
        dm = dm * keep
        sa, sb = jax.nn.sigmoid(ga), jax.nn.sigmoid(gb)
        return dm * sa, dm * sb, jnp.concatenate([dm * yc * sa * (1 - sa), dm * yg * sb * (1 - sb)], axis=1)
    cl = dict(clamp=last_x)
    dyc, dyg, dp = rm("merge_bwd", merge_bwd, ntt,
                      [tok(dmg, **cl), tok(p, 1024, GA_CB, last_x), tok(p, 1024, GB_CB, last_x), tok(y_conv, **cl),
                       tok(y_gla, **cl)],
                      [("tok", nt_rows, D, BF16), ("tok", nt_rows, D, BF16),
                       ("cols", jax.ShapeDtypeStruct((nt_rows, D_INP), BF16), 2048, 2)])

    dzl = _with_side(comm, "conv_bproj", grads, lambda s: _mm("conv_bproj", dyc, wts["w_conv_out"], trans_b=True,
                                                              rows=nx_rows, out_dtype=BF16, side=s))
    grads["w_conv_out"] = _wgrad("conv_wgrad", zlt, dyc, rows=nx_rows)

    def ln_bwd(i, zc_, g_, b_, dz_):
        _, vjp = jax.vjp(ln_silu, zc_, g_, b_)
        return vjp(dz_)
    dzc, dln_g, dln_b = rm("conv_ln_bwd", ln_bwd, nxt, [tok(zc), const(ln_g), const(ln_b), tok(dzl)],
                           [("tok", nx_rows, D, F32), ("acc", 1, D), ("acc", 1, D)])
    dz = _dwconv("dwconv_bwd", dzc, dw_w, jnp.zeros((1, D), F32), bl, t, True)
    ddw, ddb = _dwconv_wgrad(z, dzc, bl, t)
    grads.update(conv_ln_g=dln_g, conv_ln_b=dln_b, dw_weight=ddw[:CONV_W], dw_bias=ddb)

    def glu_bwd(i, dz_, a, b):
        keep = (i <= last_x).astype(F32)
        dz_ = dz_ * keep
        s = jax.nn.sigmoid(b)
        return (jnp.concatenate([dz_ * s, dz_ * a * s * (1 - s)], axis=1),)
    (dp,) = rm("glu_bwd", glu_bwd, ntt, [tok(dz, **cl), tok(p, 1024, 0, last_x), tok(p, 1024, 1, last_x)],
               [("cols", dp, 2048, 0)])

    dyg_in = _mm("gla_bproj", dyg, wts["w_gla_out"], trans_b=True, rows=nx_rows, out_dtype=BF16)
    grads["w_gla_out"] = _wgrad("gla_wgrad", yg_int, dyg, rows=nx_rows)

    def gla_out_bwd(i, of, ob, og, gn, dy):
        _, vjp = jax.vjp(gla_out, of, ob, og, gn)
        do_, _, dog_, dgn_ = vjp(dy)
        return do_, dog_, dgn_
    do, dp, dgn = rm("gla_out_bwd", gla_out_bwd, nxt,
                     [tok(o2[0]), tok(o2[1]), tok(p, 1024, OG_CB), const(gn_g), tok(dyg_in)],
                     [("tok", nx_rows, D, BF16), ("cols", dp, 1024, OG_CB), ("acc", 1, D)])
    grads["gla_norm_g"] = dgn
    (dp,) = rm("og_ctx_zero", lambda i: (jnp.zeros((TM, D), F32),), ntt - nxt, [], [("cols", dp, 1024, OG_CB, nxt)])

    dq2, dk2, dv2, dg2 = zip(*_gla_bwd(p, g2, mmats, ssave, do, bl, t, tc))

    def dec_bwd(i, lr, wf, wb, bf_, bb_, dgf, dgb_):
        zf = _dot(lr, wf, 1, 0) + bf_
        zb = _dot(lr, wb, 1, 0) + bb_
        dzf = dgf * (1 - jax.nn.sigmoid(zf)) * (1.0 / TAU)
        dzb = dgb_ * (1 - jax.nn.sigmoid(zb)) * (1.0 / TAU)
        dlr = _dot(dzf, wf, 1, 1) + _dot(dzb, wb, 1, 1)
        return (dlr, _dot(lr, dzf, 0, 0), _dot(lr, dzb, 0, 0), jnp.sum(dzf, axis=0, keepdims=True),
                jnp.sum(dzb, axis=0, keepdims=True))
    dp, dwaf, dwab, dbaf, dbab = rm(
        "decay_bwd", dec_bwd, ntt,
        [tok(p, 128, LR_COL // 128), const(waf), const(wab), const(baf), const(bab), tok(dg2[0]), tok(dg2[1])],
        [("cols", dp, 128, LR_COL // 128), ("acc", 128, 512), ("acc", 128, 512), ("acc", 1, 512), ("acc", 1, 512)])
    grads.update(w_alpha_f=dwaf[:LOWRANK], w_alpha_b=dwab[LOWRANK:2 * LOWRANK], b_alpha_f=dbaf, b_alpha_b=dbab)

    (dp,) = rm("gla_sum",
               lambda i, q0, q1, k0, k1, v0, v1: (jnp.concatenate([q0 + q1, k0 + k1, v0 + v1], axis=1),), ntt,
               [tok(dq2[0]), tok(dq2[1]), tok(dk2[0]), tok(dk2[1]), tok(dv2[0]), tok(dv2[1])],
               [("cols", dp, 2048, 1)])
    du2 = _with_side(comm, "in_bproj", grads, lambda s: _mm("in_bproj", dp, w_inp, trans_b=True, side=s))
    grads["w_in_p"] = _wgrad("in_wgrad", u2t, dp)

    def in_norm_bwd(i, x_, g_, sh_, sc_, du_, dx):
        keep = (i <= last_x).astype(F32)
        _, vjp = jax.vjp(_modnorm, x_, g_, sh_, sc_)
        dxn, dg, dsh, dsc = vjp(du_)
        return (dx * keep + dxn, dg, dsh, dsc)
    dx1, dg_mix, dsh3, dsc4 = rm("in_norm_bwd", in_norm_bwd, ntt,
                                 [tok(x1), const(wts["g_mix"]), mod(mods[3]), mod(mods[4]), tok(du2), tok(dx2, **cl)],
                                 [("tok", nt_rows, D, F32), ("acc", 1, D), ("modacc", D), ("modacc", D)])
    grads["g_mix"] = dg_mix

    dx0, gf1 = _ffn_bwd("ffn1", x0, sv1, dx1, None, ntt, wts["g_ffn1"], mods[0], mods[1], mods[2],
                        wts["w1_gu"], wts["w1_down"], tpb, nb, comm, grads, ("w1_gu", "w1_down"), keep_tiles=nxt)
    grads["g_ffn1"] = gf1["g"]
    grad_x = dx0.reshape(bl, t, D)

    dmods = [gf1["sh"], gf1["sc"], gf1["gate"], dsh3, dsc4, dgate5, gf2["sh"], gf2["sc"], gf2["gate"]]
    dmod = jnp.concatenate(
        [jnp.concatenate([a.reshape(a.shape[0], D), jnp.zeros((8 - a.shape[0], D), F32)], axis=0) for a in dmods],
        axis=1)
    grads["w_mod"], grads["c_ctx"], grads["b_mod"] = _mod_backward(dmod, cc_all, wts["w_mod_shard"], nb)
    grads["g_final"] = dg_final.reshape(D)
    return loss, grad_x, grads


ANY = pl.BlockSpec(memory_space=pl.ANY)


def _place():
    x, y, c = lax.axis_index("x"), lax.axis_index("y"), lax.axis_index("c")
    chips = [(1 - x, y), (x, 1 - y), (1 - x, 1 - y)]
    return x, y, c, chips


def _remote(send_sems, recv_sems):
    def copy(k, src, dst, to):
        return pltpu.make_async_remote_copy(src_ref=src, dst_ref=dst, send_sem=send_sems.at[k],
                                            recv_sem=recv_sems.at[k], device_id=to, device_id_type=MESH)
    return copy


def _sems(n):
    return [pltpu.SemaphoreType.DMA((n,)), pltpu.SemaphoreType.DMA((n,))]


def _swap_halves(name, gs):
    n = len(gs)

    def body(*refs):
        ins, outs = refs[:n], refs[n:2 * n]
        copy = _remote(refs[2 * n], refs[2 * n + 1])
        x, y, c, _ = _place()
        cps = []
        for i in range(n):
            hr = ins[i].shape[1] // 2
            cps.append(copy(i, ins[i].at[:, pl.ds((1 - c) * hr, hr), :], outs[i], (x, y, 1 - c)))
            cps[-1].start()
        for cp in cps:
            cp.wait()

    return pl.pallas_call(
        body, out_shape=[jax.ShapeDtypeStruct((4, g.shape[1] // 2, g.shape[2]), g.dtype) for g in gs],
        in_specs=[ANY] * n, out_specs=[ANY] * n, scratch_shapes=_sems(n), name=name)(*gs)


def _row_tile(hr):
    return hr if hr <= 256 else _pick(hr, (256, 176, 128, 64, 32, 16))


def _add_halves(name, g, r, place):
    hr = r.shape[1]
    tr = _row_tile(hr)
    nblk = hr // tr

    def kern(p_ref, g_ref, r_ref, o_ref):
        o_ref[...] = (g_ref[...].astype(F32) + r_ref[...].astype(F32)).astype(o_ref.dtype)

    blk = (1, tr, g.shape[2])
    return pl.pallas_call(
        kern,
        grid_spec=pltpu.PrefetchScalarGridSpec(
            num_scalar_prefetch=1, grid=(4, nblk),
            in_specs=[pl.BlockSpec(blk, lambda j, i, p: (j, p[0] * nblk + i, 0)),
                      pl.BlockSpec(blk, lambda j, i, p: (j, i, 0))],
            out_specs=pl.BlockSpec(blk, lambda j, i, p: (j, i, 0))),
        out_shape=jax.ShapeDtypeStruct(r.shape, r.dtype),
        compiler_params=_cparams(("parallel", "parallel")), name=name)(place, g, r)


def _scatter_chips(cs):
    n = len(cs)

    def body(*refs):
        ins, outs = refs[:n], refs[n:2 * n]
        copy = _remote(refs[2 * n], refs[2 * n + 1])
        x, y, c, chips = _place()
        me = 2 * x + y
        sends = []
        for i in range(n):
            for j, (px, py) in enumerate(chips):
                sends.append(copy(3 * i + j, ins[i].at[2 * px + py], outs[i].at[me], (px, py, c)))
                sends[-1].start()
        for i in range(n):
            for j, (px, py) in enumerate(chips):
                src = 2 * px + py
                copy(3 * i + j, ins[i].at[src], outs[i].at[src], (px, py, c)).wait_recv()
        for cp in sends:
            cp.wait_send()

    return pl.pallas_call(
        body, out_shape=[jax.ShapeDtypeStruct(a.shape, a.dtype) for a in cs], in_specs=[ANY] * n,
        out_specs=[ANY] * n, scratch_shapes=_sems(3 * n), name="grad_scatter_chips")(*cs)


def _sum_chips(name, cs, r, place):
    hr = r.shape[1]
    tr = _row_tile(hr)
    nblk = hr // tr

    def kern(p_ref, c_ref, r0, r1, r2, r3, o_ref):
        me = p_ref[1]
        acc = None
        for k, rk in enumerate((r0, r1, r2, r3)):
            val = jnp.where(me == k, c_ref[0].astype(F32), rk[0].astype(F32))
            acc = val if acc is None else acc + val
        o_ref[...] = acc

    blk = (1, tr, r.shape[2])

    def slot(k):
        return lambda i, p: (jnp.where(p[1] == k, (k + 1) % 4, k), i, 0)

    return pl.pallas_call(
        kern,
        grid_spec=pltpu.PrefetchScalarGridSpec(
            num_scalar_prefetch=1, grid=(nblk,),
            in_specs=[pl.BlockSpec(blk, lambda i, p: (p[1], i, 0))] + [pl.BlockSpec(blk, slot(k)) for k in range(4)],
            out_specs=pl.BlockSpec((tr, r.shape[2]), lambda i, p: (p[0] * nblk + i, 0))),
        out_shape=jax.ShapeDtypeStruct((2 * hr, r.shape[2]), F32),
        compiler_params=_cparams(("parallel",)), name=name)(place, cs, r, r, r, r)


def _join_halves(fs):
    n = len(fs)

    def body(*refs):
        ins, outs = refs[:n], refs[n:2 * n]
        copy = _remote(refs[2 * n], refs[2 * n + 1])
        x, y, c, _ = _place()
        cps = []
        for i in range(n):
            hr = ins[i].shape[0] // 2
            cps.append(copy(i, ins[i].at[pl.ds(c * hr, hr), :], outs[i].at[pl.ds(c * hr, hr), :], (x, y, 1 - c)))
            cps[-1].start()
        for i in range(n):
            hr = ins[i].shape[0] // 2
            other = outs[i].at[pl.ds((1 - c) * hr, hr), :]
            copy(i, other, other, (x, y, 1 - c)).wait_recv()
        for cp in cps:
            cp.wait_send()

    return pl.pallas_call(
        body, out_shape=[jax.ShapeDtypeStruct(f.shape, f.dtype) for f in fs], in_specs=[ANY] * n,
        out_specs=[ANY] * n, input_output_aliases={i: i for i in range(n)}, scratch_shapes=_sems(n),
        name="grad_join_halves")(*fs)


def _half_rows(n, hc):
    return pl.ds(hc * (n // 2), n // 2)


def _gather_ici_side(shards):
    n = len(shards)

    def copies(ins, outs, copy):
        x, y, c, chips = _place()
        me = 2 * x + y
        for i in range(n):
            rows = _half_rows(ins[i].shape[0], c)
            for j, (px, py) in enumerate(chips):
                yield (copy(3 * i + j, ins[i].at[rows, :], outs[i].at[me, rows, :], (px, py, c)),
                       outs[i].at[2 * px + py, rows, :])

    def start(ins, outs, copy):
        for cp, _ in copies(ins, outs, copy):
            cp.start()

    def finish(ins, outs, copy):
        x, y, c, chips = _place()
        k = 0
        for cp, landing in copies(ins, outs, copy):
            copy(k, landing, landing, (x, y, c)).wait_recv()
            k += 1
        for cp, _ in copies(ins, outs, copy):
            cp.wait_send()

    return dict(ins=list(shards), outs=[jax.ShapeDtypeStruct((4,) + s.shape, s.dtype) for s in shards], nsem=3 * n,
                start=start, finish=finish)


def _gather_d2d_side(shards, bufs):
    n = len(shards)

    def copies(ins, outs, copy):
        x, y, c, chips = _place()
        me = 2 * x + y
        sibling = (x, y, 1 - c)
        for i in range(n):
            a = ins[i].shape[0]
            yield copy(4 * i + 3, ins[i], outs[i].at[me], sibling), outs[i].at[me]
            for j, (px, py) in enumerate(chips):
                src = 2 * px + py
                mine = outs[i].at[src, _half_rows(a, c), :]
                yield copy(4 * i + j, mine, mine, sibling), outs[i].at[src, _half_rows(a, 1 - c), :]

    def start(ins, outs, copy):
        for cp, _ in copies(ins, outs, copy):
            cp.start()

    def finish(ins, outs, copy):
        x, y, c, _ = _place()
        for i in range(n):
            for k, (cp, landing) in enumerate(list(copies(ins, outs, copy))[4 * i:4 * i + 4]):
                sem = 4 * i + 3 if k == 0 else 4 * i + k - 1
                copy(sem, landing, landing, (x, y, 1 - c)).wait_recv()
        for cp, _ in copies(ins, outs, copy):
            cp.wait_send()

    return dict(ins=list(shards) + list(bufs), outs=[jax.ShapeDtypeStruct(b.shape, b.dtype) for b in bufs],
                nsem=4 * n, alias={n + i: i for i in range(n)}, start=start, finish=finish)


def _gather_d2d(shards, bufs):
    side = _gather_d2d_side(shards, bufs)
    n_in = len(side["ins"])

    def body(*refs):
        ins, outs = refs[:n_in], refs[n_in:n_in + len(bufs)]
        copy = _remote(*refs[n_in + len(bufs):])
        side["start"](ins, outs, copy)
        side["finish"](ins, outs, copy)

    return pl.pallas_call(
        body, out_shape=side["outs"], in_specs=[ANY] * n_in, out_specs=[ANY] * len(bufs),
        input_output_aliases=side["alias"], scratch_shapes=_sems(side["nsem"]), name="gather_weights_d2d")(*side["ins"])


def _scatter_side(cs):
    n = len(cs)

    def copies(ins, outs, copy):
        x, y, c, chips = _place()
        me = 2 * x + y
        for i in range(n):
            for j, (px, py) in enumerate(chips):
                yield copy(3 * i + j, ins[i].at[2 * px + py], outs[i].at[me], (px, py, c)), outs[i].at[2 * px + py]

    def start(ins, outs, copy):
        for cp, _ in copies(ins, outs, copy):
            cp.start()

    def finish(ins, outs, copy):
        x, y, c, _ = _place()
        for k, (cp, landing) in enumerate(copies(ins, outs, copy)):
            copy(k, landing, landing, (x, y, c)).wait_recv()
        for cp, _ in copies(ins, outs, copy):
            cp.wait_send()

    return dict(ins=list(cs), outs=[jax.ShapeDtypeStruct(a.shape, a.dtype) for a in cs], nsem=3 * n,
                start=start, finish=finish)


def _mod_forward(cc, w_shard, b_shard):
    cc_all = _allreduce_small(cc, "cond_gather", reduce=False).reshape(64, D)
    part = _mm("mod_fwd", cc_all, w_shard, a_fn=_silu, bias=b_shard)
    got = _mod_rows_exchange(part)
    return jnp.concatenate([got[j] for j in range(4)], axis=1), cc_all


def _mod_backward(dmod, cc_all, w_shard, ctx_row):
    w = w_shard.shape[1]
    blocks = _dmod_exchange(jnp.transpose(dmod.reshape(8, 4, w), (1, 0, 2))).reshape(64, w)
    dsc = _mm("mod_bproj", blocks, w_shard, trans_b=True)

    def tail(cc_ref, dsc_ref, dm_ref, s_ref, dctx_ref, db_ref):
        cc_ = cc_ref[...]
        s = jax.nn.sigmoid(cc_)
        s_ref[...] = (cc_ * s).astype(BF16)
        is_ctx = ((lax.broadcasted_iota(jnp.int32, (64, 1), 0) & 7) == ctx_row).astype(F32)
        dctx_ref[...] = 0.5 * jnp.sum(dsc_ref[...] * (s * (1 + cc_ * (1 - s))) * is_ctx, axis=0, keepdims=True)
        db_ref[...] = jnp.sum(dm_ref[...], axis=0, keepdims=True)

    s_all, dctx, db_mod = pl.pallas_call(
        tail, out_shape=[jax.ShapeDtypeStruct((64, D), BF16), jax.ShapeDtypeStruct((1, D), F32),
                         jax.ShapeDtypeStruct((1, dmod.shape[1]), F32)], name="mod_tail")(cc_all, dsc, dmod)
    return _mm_tn("mod_wgrad", s_all, blocks), dctx[0], db_mod


def _small_pack(dw, af, ab):
    return jnp.concatenate([dw, jnp.zeros((1, dw.shape[1]), F32), jnp.concatenate([af, ab], axis=1)], axis=0)


def _grad_pieces(n, grads):
    if n == "small":
        return jnp.stack([_small_pack(grads["dw_weight"][:, 256 * j:256 * (j + 1)],
                                      grads["w_alpha_f"][:, DK * j:DK * (j + 1)],
                                      grads["w_alpha_b"][:, DK * j:DK * (j + 1)]) for j in range(4)])
    if n == "w_in":
        return _unperm_in_cols(grads["w_in_p"])
    g = grads[n]
    return g if g.ndim == 3 else g.reshape(4, g.shape[0] // 4, g.shape[1])


def _swap_side(gs):
    n = len(gs)

    def copies(ins, outs, copy):
        x, y, c, _ = _place()
        for i in range(n):
            yield copy(i, ins[i].at[:, _half_rows(ins[i].shape[1], 1 - c), :], outs[i], (x, y, 1 - c))

    def start(ins, outs, copy):
        for cp in copies(ins, outs, copy):
            cp.start()

    def finish(ins, outs, copy):
        for cp in copies(ins, outs, copy):
            cp.wait()

    return dict(ins=list(gs), outs=[jax.ShapeDtypeStruct((4, g.shape[1] // 2, g.shape[2]), g.dtype) for g in gs],
                nsem=n, start=start, finish=finish)


def _chip_sums(tag, names, grads, place, swapped=None):
    gs = [_grad_pieces(n, grads) for n in names] if swapped is None else swapped[0]
    got = _swap_halves("grad_swap_" + tag, gs) if swapped is None else swapped[1]
    return [_add_halves("grad_add_" + n, g, r, place) for n, g, r in zip(names, gs, got)]


class _Overlap:
    SCATTER = {"in_bproj": ("w2_down", "w2_gu", "w_out"),
               "ffn1_wgu": ("w_conv_out", "w_gla_out", "w_in", "small"),
               "ffn1_bgu": ("w1_down", "w1_gu")}
    GATHER = {("ffn1_norm", None): ("w1_gu", "w1_down"),
              ("ffn1_gu", "ffn1_down"): ("w_in", "w_conv_out", "w_gla_out", "w_out", "small"),
              ("in_proj", "gla_proj"): ("w2_gu", "w2_down")}
    LATE = tuple(n for names in GATHER.values() for n in names)
    SWAP = {"conv_bproj": "in_bproj", "ffn1_wdown": "ffn1_wgu"}

    def __init__(self, shard_of, install, place):
        self.shard_of, self.install, self.place = shard_of, install, place
        self.bufs, self.pending, self.landed, self.swapped = {}, None, {}, {}

    def side(self, tag, grads):
        for (ici, d2d), names in self.GATHER.items():
            shards = [self.shard_of(n) for n in names]
            if tag == ici:
                return _gather_ici_side(shards)
            if tag == d2d:
                return _gather_d2d_side(shards, self.bufs[ici])
        if tag in self.SWAP:
            gs = [_grad_pieces(n, grads) for n in self.SCATTER[self.SWAP[tag]]]
            self.swapped[self.SWAP[tag]] = [gs, None]
            return _swap_side(gs)
        if tag in self.SCATTER:
            names = self.SCATTER[tag]
            self.pending = (names, _chip_sums(tag, names, grads, self.place, self.swapped.get(tag)))
            return _scatter_side(self.pending[1])
        return None

    def done(self, tag, outs):
        for (ici, d2d), names in self.GATHER.items():
            if tag == ici and d2d is None:
                self.install(dict(zip(names, _gather_d2d([self.shard_of(n) for n in names], outs))))
                return
            if tag == ici:
                self.bufs[ici] = outs
                return
            if tag == d2d:
                self.install(dict(zip(names, outs)))
                return
        if tag in self.SWAP:
            self.swapped[self.SWAP[tag]][1] = outs
            return
        for n, cs, r in zip(*self.pending, outs):
            self.landed[n] = (cs, r)


def _with_side(comm, tag, grads, call):
    side = comm.side(tag, grads) if comm is not None else None
    if side is None:
        return call(None)
    res, outs = call(side)
    comm.done(tag, outs)
    return res


def _allreduce_small(v, name="allreduce_small", reduce=True):
    def body(x_ref, out_ref, *scratch):
        gath = out_ref if not reduce else scratch[0]
        send_sems, recv_sems, local_sem = scratch[-3:]
        x, y, c, chips = _place()
        me, sibling = (x, y, c), (x, y, 1 - c)

        def slot(px, py, pc):
            return gath.at[4 * px + 2 * py + pc]

        def copy(k, block, to, src=None):
            return pltpu.make_async_remote_copy(
                src_ref=slot(*block) if src is None else src, dst_ref=slot(*block), send_sem=send_sems.at[k],
                recv_sem=recv_sems.at[k], device_id=to, device_id_type=MESH)

        mine = pltpu.make_async_copy(x_ref, slot(*me), local_sem)
        mine.start()
        first = [copy(0, me, sibling, src=x_ref)]
        first += [copy(1 + j, me, (*chip, c), src=x_ref) for j, chip in enumerate(chips)]
        for cp in first:
            cp.start()
        passed = [copy(4 + j, (*chip, c), sibling) for j, chip in enumerate(chips)]
        for j, chip in enumerate(chips):
            copy(1 + j, (*chip, c), me).wait_recv()
            passed[j].start()
        copy(0, sibling, me).wait_recv()
        for j, chip in enumerate(chips):
            copy(4 + j, (*chip, 1 - c), me).wait_recv()
        for cp in first + passed:
            cp.wait_send()
        mine.wait()
        if reduce:
            acc = gath[0]
            for k in range(1, 8):
                acc = acc + gath[k]
            out_ref[...] = acc

    vm = pl.BlockSpec(memory_space=pltpu.VMEM)
    sems = [pltpu.SemaphoreType.DMA((7,)), pltpu.SemaphoreType.DMA((7,)), pltpu.SemaphoreType.DMA(())]
    return pl.pallas_call(
        body, out_shape=jax.ShapeDtypeStruct(v.shape if reduce else (8,) + v.shape, F32), in_specs=[vm], out_specs=vm,
        scratch_shapes=([pltpu.VMEM((8,) + v.shape, F32)] if reduce else []) + sems, name=name)(v)


def _mod_rows_exchange(part):
    w = part.shape[1]

    def body(p_ref, out_ref, send_sems, recv_sems):
        x, y, c, chips = _place()
        me = 2 * x + y
        copy = _remote(send_sems, recv_sems)
        sends = []
        for j, (px, py) in enumerate(chips):
            rows = pl.ds(pl.multiple_of(8 * (4 * px + 2 * py + c), 8), 8)
            sends.append(copy(j, p_ref.at[rows, :], out_ref.at[me], (px, py, c)))
            sends[-1].start()
        out_ref[me] = p_ref[pl.ds(pl.multiple_of(8 * (4 * x + 2 * y + c), 8), 8), :]
        for j, (px, py) in enumerate(chips):
            landing = out_ref.at[2 * px + py]
            copy(j, landing, landing, (px, py, c)).wait_recv()
        for cp in sends:
            cp.wait_send()

    vm = pl.BlockSpec(memory_space=pltpu.VMEM)
    return pl.pallas_call(body, out_shape=jax.ShapeDtypeStruct((4, 8, w), F32), in_specs=[vm], out_specs=vm,
                          scratch_shapes=_sems(3), name="mod_rows_exchange")(part)


def _dmod_exchange(dm):
    w = dm.shape[2]

    def body(d_ref, out_ref, send_sems, recv_sems):
        x, y, c, _ = _place()
        copy = _remote(send_sems, recv_sems)
        mine = 4 * x + 2 * y + c
        sends = []
        for r in range(1, 8):
            tx, ty, tc = x ^ (r >> 2), y ^ ((r >> 1) & 1), c ^ (r & 1)
            sends.append(copy(r - 1, d_ref.at[2 * tx + ty], out_ref.at[mine], (tx, ty, tc)))
            sends[-1].start()
        out_ref[mine] = d_ref[2 * x + y]
        for r in range(1, 8):
            tx, ty, tc = x ^ (r >> 2), y ^ ((r >> 1) & 1), c ^ (r & 1)
            landing = out_ref.at[4 * tx + 2 * ty + tc]
            copy(r - 1, landing, landing, (tx, ty, tc)).wait_recv()
        for cp in sends:
            cp.wait_send()

    vm = pl.BlockSpec(memory_space=pltpu.VMEM)
    return pl.pallas_call(body, out_shape=jax.ShapeDtypeStruct((8, 8, w), F32), in_specs=[vm], out_specs=vm,
                          scratch_shapes=_sems(7), name="dmod_exchange")(dm)


def _adamw(name, w, g, m, v):
    r, cols = w.shape
    budget = 262144
    tr = r if r * cols <= budget else next(c for c in (256, 128, 64, 32, 16, 8) if r % c == 0 and c * cols <= budget)

    def kern(w_ref, g_ref, m_ref, v_ref, go_ref, d_ref, nm_ref, nv_ref):
        gv = g_ref[...]
        go_ref[...] = gv
        nm = ADAM_B1 * m_ref[...] + (1.0 - ADAM_B1) * gv
        nv = ADAM_B2 * v_ref[...] + (1.0 - ADAM_B2) * jnp.square(gv)
        m_hat = nm / (1.0 - ADAM_B1 ** ADAM_STEP)
        v_hat = nv / (1.0 - ADAM_B2 ** ADAM_STEP)
        d_ref[...] = -ADAM_LR * (m_hat / (jnp.sqrt(v_hat) + ADAM_EPS) + ADAM_WD * w_ref[...])
        nm_ref[...] = nm
        nv_ref[...] = nv

    spec = pl.BlockSpec((tr, cols), lambda i: (i, 0))
    shp = jax.ShapeDtypeStruct((r, cols), F32)
    return pl.pallas_call(kern, grid=(r // tr,), in_specs=[spec] * 4, out_specs=[spec] * 4, out_shape=[shp] * 4,
                          compiler_params=_cparams(("parallel",)), name=name)(w, g, m, v)


SHARDED = (("w_mod", 1), ("w1_gu", 1), ("w1_down", 0), ("w_in", 1), ("dw_weight", 1), ("w_conv_out", 0),
           ("w_alpha_f", 1), ("w_alpha_b", 1), ("w_gla_out", 0), ("w_out", 0), ("w2_gu", 1), ("w2_down", 0))
REPLICATED = ("c_ctx", "b_mod", "g_ffn1", "g_mix", "dw_bias", "conv_ln_g", "conv_ln_b", "b_alpha_f", "b_alpha_b",
              "gla_norm_g", "g_ffn2", "g_final")
WEIGHTS = ("c_ctx", "w_mod", "b_mod", "g_ffn1", "w1_gu", "w1_down", "g_mix", "w_in", "dw_weight", "dw_bias",
           "conv_ln_g", "conv_ln_b", "w_conv_out", "w_alpha_f", "b_alpha_f", "w_alpha_b", "b_alpha_b", "gla_norm_g",
           "w_gla_out", "w_out", "g_ffn2", "w2_gu", "w2_down", "g_final")
MATRICES = ("w1_gu", "w1_down", "w_in", "w_conv_out", "w_gla_out", "w_out", "w2_gu", "w2_down")


def _pack_flat(parts, align):
    flat = jnp.concatenate([p.reshape(-1) for p in parts])
    pad = (-flat.shape[0]) % align
    return jnp.concatenate([flat, jnp.zeros((pad,), flat.dtype)]).reshape(-1, 1024)


def _unpack_flat(flat2d, shapes):
    flat = flat2d.reshape(-1)
    out, off = [], 0
    for s in shapes:
        n = math.prod(s)
        out.append(flat[off:off + n].reshape(s))
        off += n
    return out


def kernel(x, c, ctx, c_ctx, w_mod, b_mod, g_ffn1, w1_gu, w1_down, g_mix, w_in, dw_weight, dw_bias, conv_ln_g, conv_ln_b, w_conv_out, w_alpha_f, b_alpha_f, w_alpha_b, b_alpha_b, gla_norm_g, w_gla_out, w_out, g_ffn2, w2_gu, w2_down, g_final, loss_target, m_c_ctx, m_w_mod, m_b_mod, m_g_ffn1, m_w1_gu, m_w1_down, m_g_mix, m_w_in, m_dw_weight, m_dw_bias, m_conv_ln_g, m_conv_ln_b, m_w_conv_out, m_w_alpha_f, m_b_alpha_f, m_w_alpha_b, m_b_alpha_b, m_gla_norm_g, m_w_gla_out, m_w_out, m_g_ffn2, m_w2_gu, m_w2_down, m_g_final, v_c_ctx, v_w_mod, v_b_mod, v_g_ffn1, v_w1_gu, v_w1_down, v_g_mix, v_w_in, v_dw_weight, v_dw_bias, v_conv_ln_g, v_conv_ln_b, v_w_conv_out, v_w_alpha_f, v_b_alpha_f, v_w_alpha_b, v_b_alpha_b, v_gla_norm_g, v_w_gla_out, v_w_out, v_g_ffn2, v_w2_gu, v_w2_down, v_g_final):
    given = dict(locals())
    w = {n: given[n] for n in WEIGHTS}
    m = {n: given["m_" + n] for n in WEIGHTS}
    v = {n: given["v_" + n] for n in WEIGHTS}

    def shard_of(n):
        if n == "small":
            return _small_pack(w["dw_weight"][0], w["w_alpha_f"][0], w["w_alpha_b"][0])
        return w[n][0].astype(BF16)

    def install(wts, got):
        for n in ("w1_gu", "w2_gu"):
            if n in got:
                wts[n] = got[n]
        for n in ("w1_down", "w2_down", "w_conv_out", "w_gla_out", "w_out"):
            if n in got:
                wts[n] = got[n].reshape(-1, D)
        if "w_in" in got:
            wts["w_in_p"] = _perm_in_cols(got["w_in"])
        if "small" in got:
            sm = got["small"]
            wts["dw_weight"] = jnp.concatenate([sm[j, :CONV_W] for j in range(4)], axis=1)
            zpad = jnp.zeros((128, HEADS * DK), BF16)
            w_af = jnp.concatenate([sm[j, 32:32 + LOWRANK, :DK] for j in range(4)], axis=1)
            w_ab = jnp.concatenate([sm[j, 32:32 + LOWRANK, DK:] for j in range(4)], axis=1)
            wts["w_alpha_f_pad"] = zpad.at[0:LOWRANK].set(w_af.astype(BF16))
            wts["w_alpha_b_pad"] = zpad.at[LOWRANK:2 * LOWRANK].set(w_ab.astype(BF16))

    wts = {n: w[n] for n in REPLICATED}
    chip = 2 * lax.axis_index("x") + lax.axis_index("y")
    mod_cols = w["w_mod"].shape[2]
    wts["w_mod_shard"] = shard_of("w_mod")
    wts["b_mod_shard"] = lax.dynamic_slice(w["b_mod"], (0, chip * mod_cols), (1, mod_cols))

    place = jnp.stack([lax.axis_index("c"), 2 * lax.axis_index("x") + lax.axis_index("y")]).astype(jnp.int32)
    comm = _Overlap(shard_of, lambda got: install(wts, got), place)
    loss, grad_x, grads = _local_step(x, c, ctx, loss_target, wts, comm)
    loss = lax.psum(loss, ("x", "y", "c"))

    tags = MATRICES + ("small",)
    rest = tuple(n for n in tags if n not in comm.landed)
    if rest:
        rest_sums = _chip_sums("rest", rest, grads, place)
        for n, cs, r in zip(rest, rest_sums, _scatter_chips(rest_sums)):
            comm.landed[n] = (cs, r)
    halves = [_sum_chips("grad_sum_" + t, *comm.landed[t], place) for t in tags]
    reduced = dict(zip(tags, _join_halves(halves)))
    g_shard = {n: reduced[n] for n in MATRICES}
    g_shard["w_mod"] = grads["w_mod"]
    g_shard["dw_weight"] = reduced["small"][:CONV_W]
    g_shard["w_alpha_f"] = reduced["small"][32:32 + LOWRANK, :DK]
    g_shard["w_alpha_b"] = reduced["small"][32:32 + LOWRANK, DK:]

    rep_shapes = [w[n].shape for n in REPLICATED]
    small = _allreduce_small(_pack_flat([grads[n].reshape(w[n].shape) for n in REPLICATED], 8 * 1024))

    g_out, d_out, m_out, v_out = {}, {}, {}, {}
    for n, _ in SHARDED:
        s2 = w[n].shape[1:]
        go, d, nm, nv = _adamw("adamw_" + n, w[n].reshape(s2), g_shard[n], m[n].reshape(s2), v[n].reshape(s2))
        g_out[n], d_out[n] = go.reshape(w[n].shape), d.reshape(w[n].shape)
        m_out[n], v_out[n] = nm.reshape(w[n].shape), nv.reshape(w[n].shape)
    pk = lambda t: _pack_flat([t[n] for n in REPLICATED], 8 * 1024)
    go, d, nm, nv = _adamw("adamw_vectors", pk(w), small, pk(m), pk(v))
    for n, gg, dd, mm, vv in zip(REPLICATED, _unpack_flat(go, rep_shapes), _unpack_flat(d, rep_shapes),
                                 _unpack_flat(nm, rep_shapes), _unpack_flat(nv, rep_shapes)):
        g_out[n], d_out[n], m_out[n], v_out[n] = gg, dd, mm, vv

    return (loss, grad_x, *[g_out[n] for n in WEIGHTS], *[d_out[n] for n in WEIGHTS],
            *[m_out[n] for n in WEIGHTS], *[v_out[n] for n in WEIGHTS])
```

```python
import functools
import math

import jax
import jax.numpy as jnp
from jax import lax
from jax.experimental import pallas as pl
from jax.experimental.pallas import tpu as pltpu

F32, BF16 = jnp.float32, jnp.bfloat16
MESH = pl.DeviceIdType.MESH

D = 1024
FF = 2816
HEADS, DK, DV = 4, 128, 256
LOWRANK = 16
CONV_W = 31
CHUNK = 64
TAU = 16.0
EPS = 1e-6
Q_SCALE = DK ** -0.5
TM = 256
D_IN = 7200
D_INP = 7296
LR_COL = 7168
OG_CB, GA_CB, GB_CB = 6, 4, 5
VMEM_LIMIT = 52 * 1024 * 1024
WGRAD_VMEM = 40 * 1024 * 1024

ADAM_LR, ADAM_B1, ADAM_B2, ADAM_EPS, ADAM_WD, ADAM_STEP = 0.001, 0.9, 0.999, 1e-08, 0.01, 10


def _silu(x):
    return x * jax.nn.sigmoid(x)


def _rms(h, g):
    return h * lax.rsqrt(jnp.mean(h * h, axis=-1, keepdims=True) + EPS) * g


def _modnorm(x, g, shift, scale):
    return _rms(x, g) * (1 + scale) + shift


def _cparams(sem=None):
    return pltpu.CompilerParams(dimension_semantics=sem, vmem_limit_bytes=VMEM_LIMIT)


def _twice(v):
    return v, v


def tok(arr, width=None, cb=0, clamp=None):
    return ("tok", arr, arr.shape[1] if width is None else width, cb, clamp)


def mod(arr):
    return ("mod", arr)


def const(arr):
    return ("const", arr)


def _rowmap(name, body, n_tiles, ins, outs, *, tpb, nb, side=None):
    def modrow(i):
        return jnp.minimum(i // tpb, nb)

    in_specs, args = [], []
    for spec in ins:
        if spec[0] == "tok":
            _, arr, width, cb, clamp = spec
            if clamp is None:
                im = lambda i, cb=cb: (i, cb)
            else:
                im = lambda i, cb=cb, clamp=clamp: (jnp.minimum(i, clamp), cb)
            in_specs.append(pl.BlockSpec((TM, width), im))
        elif spec[0] == "mod":
            arr = spec[1]
            in_specs.append(pl.BlockSpec((1, 1, arr.shape[2]), lambda i: (modrow(i), 0, 0)))
        else:
            arr = spec[1]
            in_specs.append(pl.BlockSpec(arr.shape, lambda i, nd=arr.ndim: (0,) * nd))
        args.append(arr)
    out_specs, out_shapes, aliases = [], [], {}
    for o in outs:
        if o[0] == "tok":
            _, rows, width, dtype = o
            out_shapes.append(jax.ShapeDtypeStruct((rows, width), dtype))
            out_specs.append(pl.BlockSpec((TM, width), lambda i: (i, 0)))
        elif o[0] == "cols":
            buf, width, cb = o[1:4]
            first_tile = o[4] if len(o) > 4 else 0
            if not isinstance(buf, jax.ShapeDtypeStruct):
                aliases[len(args)] = len(out_shapes)
                in_specs.append(ANY)
                args.append(buf)
            out_shapes.append(jax.ShapeDtypeStruct(buf.shape, buf.dtype))
            out_specs.append(pl.BlockSpec((TM, width), lambda i, cb=cb, t0=first_tile: (i + t0, cb)))
        elif o[0] == "tok_head":
            _, rows, width, dtype, last = o
            out_shapes.append(jax.ShapeDtypeStruct((rows, width), dtype))
            out_specs.append(pl.BlockSpec((TM, width), lambda i, last=last: (jnp.minimum(i, last), 0)))
        elif o[0] == "tokT":
            _, rows, width, dtype = o
            out_shapes.append(jax.ShapeDtypeStruct((width, rows), dtype))
            out_specs.append(pl.BlockSpec((width, TM), lambda i: (0, i)))
        elif o[0] == "acc":
            _, rows, width = o
            out_shapes.append(jax.ShapeDtypeStruct((rows, width), F32))
            out_specs.append(pl.BlockSpec((rows, width), lambda i: (0, 0)))
        else:
            width = o[1]
            rows_visited = min((n_tiles - 1) // tpb, nb) + 1
            out_shapes.append(jax.ShapeDtypeStruct((rows_visited, 1, width), F32))
            out_specs.append(pl.BlockSpec((1, 1, width), lambda i: (modrow(i), 0, 0)))
    n_in = len(ins)

    def kern(*refs):
        i = pl.program_id(0)
        vals = []
        for r, spec in zip(refs[:n_in], ins):
            val = r[0] if spec[0] == "mod" else r[...]
            vals.append(val.astype(F32) if spec[0] == "tok" and val.dtype == BF16 else val)
        res = body(i, *vals)
        for r, o, val in zip(refs[len(args):], outs, res):
            if o[0] in ("tok", "cols"):
                r[...] = val.astype(r.dtype)
            elif o[0] == "tok_head":
                @pl.when(i <= o[4])
                def _():
                    r[...] = val.astype(r.dtype)
            elif o[0] == "tokT":
                r[...] = val.T.astype(r.dtype)
            elif o[0] == "acc":
                @pl.when(i == 0)
                def _():
                    r[...] = jnp.zeros(r.shape, F32)
                r[...] += jnp.broadcast_to(val, r.shape)
            else:
                first = jnp.logical_or(i == 0, modrow(i) != modrow(jnp.maximum(i - 1, 0)))

                @pl.when(first)
                def _():
                    r[...] = jnp.zeros(r.shape, F32)
                r[0] += val

    if side is not None:
        assert not aliases
        return _pallas(kern, grid=(n_tiles,), in_specs=in_specs, out_specs=out_specs, out_shape=out_shapes,
                       scratch_shapes=[], sem=("arbitrary",), name=name, args=args, side=side)
    return pl.pallas_call(
        kern, grid=(n_tiles,), in_specs=in_specs, out_specs=out_specs, out_shape=out_shapes,
        input_output_aliases=aliases, compiler_params=_cparams(("arbitrary",)), name=name)(*args)


def _pick(n, cands):
    for c in cands:
        if n % c == 0:
            return c
    return n


def _pallas(kern, *, grid, in_specs, out_specs, out_shape, scratch_shapes, sem, name, args, side=None):
    if side is None:
        return pl.pallas_call(kern, grid=grid, in_specs=in_specs, out_specs=out_specs, out_shape=out_shape,
                              scratch_shapes=scratch_shapes, compiler_params=_cparams(sem), name=name)(*args)
    single = not isinstance(out_shape, (list, tuple))
    shapes = [out_shape] if single else list(out_shape)
    ospecs = [out_specs] if single else list(out_specs)
    n_in, n_out, n_scr = len(in_specs), len(shapes), len(scratch_shapes)
    s_in, s_out = list(side["ins"]), list(side["outs"])

    def wrapped(*refs):
        pos = [0]

        def take(n):
            pos[0] += n
            return refs[pos[0] - n:pos[0]]
        ins, sins, outs, souts, scr, sems = take(n_in), take(len(s_in)), take(n_out), take(len(s_out)), take(n_scr), take(2)
        ids = [pl.program_id(k) for k in range(len(grid))]
        first = functools.reduce(jnp.logical_and, [i == 0 for i in ids])
        last = functools.reduce(jnp.logical_and, [i == g - 1 for i, g in zip(ids, grid)])
        copy = _remote(*sems)

        @pl.when(first)
        def _():
            side["start"](sins, souts, copy)
        kern(*ins, *outs, *scr)

        @pl.when(last)
        def _():
            side["finish"](sins, souts, copy)

    res = pl.pallas_call(
        wrapped, grid=grid, in_specs=list(in_specs) + [ANY] * len(s_in), out_specs=ospecs + [ANY] * len(s_out),
        out_shape=shapes + s_out, scratch_shapes=list(scratch_shapes) + _sems(side["nsem"]),
        input_output_aliases={n_in + a: n_out + b for a, b in side.get("alias", {}).items()},
        compiler_params=_cparams(("arbitrary",) * len(grid)), name=name)(*args, *s_in)
    main = res[:n_out]
    return (main[0] if single else main), list(res[n_out:])


def _mm(name, a, b, *, trans_b=False, out_dtype=F32, a_fn=None, bias=None, rows=None, side=None, residual=None):
    m, k = a.shape if a.ndim == 2 else (a.shape[1], 2 * a.shape[2])
    m = m if rows is None else rows
    shard = b.shape[2] if b.ndim == 3 else None
    if trans_b:
        n = b.shape[-2]
        tk = _pick(shard, (2816, 2304, 1408, 1024)) if shard else (
            k if k <= 2816 else _pick(k, (2816, 2432, 2304, 2048, 1536, 1408, 1024, 512, 256, 128)))
        tn = _pick(n, (1024, 512, 384, 256, 128))
    else:
        n = 4 * shard if shard else b.shape[1]
        tk = k if k <= 2816 else _pick(k, (2816, 2432, 2304, 2048, 1536, 1408, 1024, 512, 256, 128))
        tn = _pick(shard, (512, 384, 1408, 256, 128)) if shard else _pick(n, (1024, 2432, 512, 384, 256, 128))
    if residual is not None:
        tm = next(c for c in (512, 256) if m % c == 0 and (residual["tpb"] * TM) % c == 0)
    else:
        tm = _pick(m, (1024, 512, 256))
    nk = k // tk
    per = shard // (tk if trans_b else tn) if shard else None
    dims = (((1,), (1,)), ((), ())) if trans_b else (((1,), (0,)), ((), ()))

    def kern(*refs):
        a_ref, b_ref = refs[0], refs[1]
        bias_ref = refs[2] if bias is not None else None
        acc_ref = refs[-1]
        if residual is not None:
            x_ref, gate_ref, o_ref, xo_ref = refs[-5:-1]
        else:
            o_ref = refs[-2]
        kk = pl.program_id(2)
        av = a_ref[...]
        if a_fn is not None:
            av = a_fn(av)
        p = lax.dot_general(av.astype(BF16), b_ref[...].astype(BF16), dims, preferred_element_type=F32)

        def finish(total):
            if bias_ref is not None:
                total = total + bias_ref[...]
            o_ref[...] = total.astype(o_ref.dtype)
            if residual is not None:
                xo_ref[...] = x_ref[...] + residual["scale"] * gate_ref[0] * total

        if nk == 1:
            finish(p)
        else:
            @pl.when(kk == 0)
            def _():
                acc_ref[...] = p

            @pl.when(kk > 0)
            def _():
                acc_ref[...] += p

            @pl.when(kk == nk - 1)
            def _():
                finish(acc_ref[...])

    if shard and trans_b:
        b_spec = pl.BlockSpec((None, tn, tk), lambda i, j, kk: (kk // per, j, kk % per))
    elif shard:
        b_spec = pl.BlockSpec((None, tk, tn), lambda i, j, kk: (j // per, kk, j % per))
    elif trans_b:
        b_spec = pl.BlockSpec((tn, tk), lambda i, j, kk: (j, kk))
    else:
        b_spec = pl.BlockSpec((tk, tn), lambda i, j, kk: (kk, j))
    if a.ndim == 3:
        pa = a.shape[2] // tk
        a_spec = pl.BlockSpec((None, tm, tk), lambda i, j, kk: (kk // pa, i, kk % pa))
    else:
        a_spec = pl.BlockSpec((tm, tk), lambda i, j, kk: (i, kk))
    in_specs = [a_spec, b_spec]
    args = [a, b]
    if bias is not None:
        in_specs.append(pl.BlockSpec((1, tn), lambda i, j, kk: (0, j)))
        args.append(bias)
    out_specs = pl.BlockSpec((tm, tn), lambda i, j, kk: (i, j))
    out_shape = jax.ShapeDtypeStruct((m, n), out_dtype)
    if residual is not None:
        tiles, nb = residual["tpb"] * TM // tm, residual["gate"].shape[0] - 1
        in_specs += [out_specs, pl.BlockSpec((1, 1, tn), lambda i, j, kk: (jnp.minimum(i // tiles, nb), 0, j))]
        args += [residual["x"], residual["gate"]]
        out_specs, out_shape = [out_specs, out_specs], [out_shape, jax.ShapeDtypeStruct((m, n), F32)]
    return _pallas(
        kern, grid=(m // tm, n // tn, nk), in_specs=in_specs, out_specs=out_specs, out_shape=out_shape,
        scratch_shapes=[pltpu.VMEM((tm, tn) if nk > 1 else (8, 128), F32)],
        sem=("parallel", "parallel", "arbitrary"), name=name, args=args, side=side)


def _mm_tn(name, x, dy):
    t, k1, n1 = x.shape[0], x.shape[1], dy.shape[1]
    tt = _pick(t, (512, 256, 128, 64, 8))
    tk1 = _pick(k1, (1024, 512, 256, 128))
    tn = _pick(n1, (512, 384, 256, 128))
    ns = t // tt

    def kern(x_ref, dy_ref, o_ref, acc_ref):
        s = pl.program_id(2)
        p = lax.dot_general(x_ref[...].astype(BF16), dy_ref[...].astype(BF16), (((0,), (0,)), ((), ())),
                            preferred_element_type=F32)

        @pl.when(s == 0)
        def _():
            acc_ref[...] = p

        @pl.when(s > 0)
        def _():
            acc_ref[...] += p

        @pl.when(s == ns - 1)
        def _():
            o_ref[...] = acc_ref[...].astype(o_ref.dtype)

    return pl.pallas_call(
        kern, grid=(k1 // tk1, n1 // tn, ns),
        in_specs=[pl.BlockSpec((tt, tk1), lambda i, j, s: (s, i)), pl.BlockSpec((tt, tn), lambda i, j, s: (s, j))],
        out_specs=pl.BlockSpec((tk1, tn), lambda i, j, s: (i, j)), out_shape=jax.ShapeDtypeStruct((k1, n1), F32),
        scratch_shapes=[pltpu.VMEM((tk1, tn), F32)],
        compiler_params=_cparams(("parallel", "parallel", "arbitrary")), name=name)(x, dy)


def _wgrad(name, xt, dy, rows=None, col_shards=False, side=None):
    k1 = xt.shape[0]
    t = xt.shape[1] if rows is None else rows
    n1 = dy.shape[1] if dy.ndim == 2 else 2 * dy.shape[2]
    tn = _pick(n1 // 4, (1408, 512, 384, 256, 128)) if col_shards else _pick(n1, (1024, 2432, 512, 384, 256, 128))

    def token_tile(tm):
        fixed = tm * tn * (4 + 4 + 2 * 2)
        return next((c for c in (2048, 1536, 1024, 512, 256, 128)
                     if t % c == 0 and fixed + 4 * c * (tm + tn) <= WGRAD_VMEM), 128)
    tm = next((c for c in (1024, 1408, 512, 256) if k1 % c == 0 and token_tile(c) >= 1024),
              _pick(k1, (1024, 1408, 512, 256)))
    tk = token_tile(tm)
    ns = t // tk
    per = n1 // 4 // tn

    def kern(x_ref, dy_ref, o_ref, acc_ref):
        s = pl.program_id(2)
        p = jnp.dot(x_ref[...], dy_ref[...], preferred_element_type=F32)

        @pl.when(s == 0)
        def _():
            acc_ref[...] = p

        @pl.when(s > 0)
        def _():
            acc_ref[...] += p

        @pl.when(s == ns - 1)
        def _():
            o_ref[...] = acc_ref[...].astype(o_ref.dtype)

    if dy.ndim == 3:
        pd = dy.shape[2] // tn
        dy_spec = pl.BlockSpec((None, tk, tn), lambda i, j, s: (j // pd, s, j % pd))
    else:
        dy_spec = pl.BlockSpec((tk, tn), lambda i, j, s: (s, j))
    if col_shards:
        out_spec = pl.BlockSpec((None, tm, tn), lambda i, j, s: (j // per, i, j % per))
        out_shape = jax.ShapeDtypeStruct((4, k1, n1 // 4), BF16)
    else:
        out_spec = pl.BlockSpec((tm, tn), lambda i, j, s: (i, j))
        out_shape = jax.ShapeDtypeStruct((k1, n1), BF16)
    return _pallas(
        kern, grid=(k1 // tm, n1 // tn, ns),
        in_specs=[pl.BlockSpec((tm, tk), lambda i, j, s: (i, s)), dy_spec],
        out_specs=out_spec, out_shape=out_shape, scratch_shapes=[pltpu.VMEM((tm, tn), F32)],
        sem=("parallel", "parallel", "arbitrary"), name=name, args=[xt, dy], side=side)


def _swiglu_fwd(name, u, w_gu, side=None):
    m = u.shape[0]
    half = w_gu.shape[2]
    tm = _pick(m, (512, 256))

    def kern(u_ref, wa_ref, wb_ref, ab_ref, hm_ref, hmt_ref):
        uv = u_ref[...]
        a = jnp.dot(uv, wa_ref[...], preferred_element_type=F32)
        b = jnp.dot(uv, wb_ref[...], preferred_element_type=F32)
        s = jax.nn.sigmoid(a)
        silu_a = a * s
        ab_ref[0] = (b * (s * (1 + a * (1 - s)))).astype(BF16)
        ab_ref[1] = silu_a.astype(BF16)
        hm = (silu_a * b).astype(BF16)
        hm_ref[...] = hm
        hmt_ref[...] = hm.T

    return _pallas(
        kern, grid=(2, m // tm),
        in_specs=[pl.BlockSpec((tm, D), lambda j, i: (i, 0)),
                  pl.BlockSpec((None, D, half), lambda j, i: (j, 0, 0)),
                  pl.BlockSpec((None, D, half), lambda j, i: (2 + j, 0, 0))],
        out_specs=[pl.BlockSpec((2, tm, half), lambda j, i: (0, i, j)),
                   pl.BlockSpec((tm, half), lambda j, i: (i, j)),
                   pl.BlockSpec((half, tm), lambda j, i: (j, i))],
        out_shape=[jax.ShapeDtypeStruct((2, m, FF), BF16), jax.ShapeDtypeStruct((m, FF), BF16),
                   jax.ShapeDtypeStruct((FF, m), BF16)],
        scratch_shapes=[], sem=("parallel", "parallel"), name=name, args=[u, w_gu, w_gu], side=side)


def _swiglu_bwd(name, df, w_down, ab):
    m = df.shape[0]
    half = FF // 2
    tm = _pick(m, (512, 256))

    def kern(df_ref, w_ref, ab_ref, o_ref):
        dh = lax.dot_general(df_ref[...], w_ref[...], (((1,), (1,)), ((), ())), preferred_element_type=F32)
        o_ref[0] = (dh * ab_ref[0].astype(F32)).astype(BF16)
        o_ref[1] = (dh * ab_ref[1].astype(F32)).astype(BF16)

    return pl.pallas_call(
        kern, grid=(2, m // tm),
        in_specs=[pl.BlockSpec((tm, D), lambda j, i: (i, 0)),
                  pl.BlockSpec((half, D), lambda j, i: (j, 0)),
                  pl.BlockSpec((2, tm, half), lambda j, i: (0, i, j))],
        out_specs=pl.BlockSpec((2, tm, half), lambda j, i: (0, i, j)),
        out_shape=jax.ShapeDtypeStruct((2, m, FF), BF16),
        compiler_params=_cparams(("parallel", "parallel")), name=name)(df, w_down, ab)


def _gla_maps(bl, t, tc):
    nx, nc = t // CHUNK, tc // CHUNK
    nxb = bl * nx

    def rowblk(d, b, n):
        c_ctx = jnp.where(d == 0, n, nc - 1 - n)
        c_x = jnp.where(d == 0, n - nc, nx - 1 - (n - nc))
        return jnp.where(n < nc, nxb + b * nc + c_ctx, b * nx + c_x)

    def xblk(d, b, n):
        n2 = jnp.maximum(n, nc)
        return b * nx + jnp.where(d == 0, n2 - nc, nx - 1 - (n2 - nc))

    return nx, nc, rowblk, xblk


def _dot01(m, x, cm):
    x1 = x.astype(BF16)
    r1 = x - x1.astype(F32)
    x2 = r1.astype(BF16)
    x3 = (r1 - x2.astype(F32)).astype(BF16)
    w = x.shape[1]
    p = lax.dot_general(m.astype(BF16), jnp.concatenate([x1, x2, x3], axis=1), (((cm,), (0,)), ((), ())),
                        preferred_element_type=F32)
    return p[:, :w] + p[:, w:2 * w] + p[:, 2 * w:]


def _gla_chunk(m, q, k, v, g, h):
    gh = g[:, h * DK:(h + 1) * DK]
    b = _dot01(m, gh, 1)
    tot = jnp.sum(gh, axis=0, keepdims=True)
    mid = b[CHUNK // 2:CHUNK // 2 + 1, :]
    qh = q[:, h * DK:(h + 1) * DK] * Q_SCALE
    kh = k[:, h * DK:(h + 1) * DK]
    vh = v[:, h * DV:(h + 1) * DV]
    return b, tot, mid, qh, kh, vh


def _dot(a, b, ca, cb):
    return lax.dot_general(a.astype(BF16), b.astype(BF16), (((ca,), (cb,)), ((), ())),
                           preferred_element_type=F32)


def _gla_fwd(p, g2, mmats, bl, t, tc):
    nx, nc, rowblk, xblk = _gla_maps(bl, t, tc)
    ns = nx + nc

    def kern(q0, k0, v0, g0, q1, k1, v1, g1, m_ref, o0, o1, ss_ref, s_ref):
        n = pl.program_id(1)

        @pl.when(n == 0)
        def _():
            s_ref[...] = jnp.zeros(s_ref.shape, F32)
        sides = ((q0, k0, v0, g0, o0), (q1, k1, v1, g1, o1))
        loaded = [(m_ref[d], q_ref[...].astype(F32), k_ref[...].astype(F32), v_ref[...].astype(F32), g_ref[...])
                  for d, (q_ref, k_ref, v_ref, g_ref, _) in enumerate(sides)]
        chains = [(d, h) for d in range(2) for h in range(HEADS)]
        base = [_gla_chunk(*loaded[d], h) for d, h in chains]
        pre = []
        for (d, h), (b, tot, mid, qh, kh, vh) in zip(chains, base):
            s0 = s_ref[d, h * DV:(h + 1) * DV, :]
            ss_ref[d, 0, 0, h * DV:(h + 1) * DV, :] = s0.astype(BF16)
            pre.append((s0, kh * jnp.exp(tot - b), qh * jnp.exp(b), qh * jnp.exp(b - mid), kh * jnp.exp(mid - b)))
        raw = [(_dot(qm, km, 1, 1), _dot(qe, s0, 1, 1), _dot(bs[5], kl, 0, 0))
               for bs, (s0, kl, qe, qm, km) in zip(base, pre)]
        for (d, h), bs, (s0, kl, qe, qm, km), (att_raw, inter, s_add) in zip(chains, base, pre, raw):
            s_ref[d, h * DV:(h + 1) * DV, :] = s0 * jnp.exp(bs[1]) + s_add
            sides[d][4][:, h * DV:(h + 1) * DV] = (inter + _dot(loaded[d][0] * att_raw, bs[5], 1, 0)).astype(BF16)

    def operands(d):
        return [pl.BlockSpec((CHUNK, 512), lambda b, n: (rowblk(d, b, n), 4)),
                pl.BlockSpec((CHUNK, 512), lambda b, n: (rowblk(d, b, n), 5)),
                pl.BlockSpec((CHUNK, 1024), lambda b, n: (rowblk(d, b, n), 3)),
                pl.BlockSpec((CHUNK, 512), lambda b, n: (rowblk(d, b, n), d))]

    o_shape = jax.ShapeDtypeStruct((bl * t, HEADS * DV), BF16)
    return pl.pallas_call(
        kern, grid=(bl, ns),
        in_specs=operands(0) + operands(1) + [pl.BlockSpec((2, CHUNK, CHUNK), lambda b, n: (0, 0, 0))],
        out_specs=[pl.BlockSpec((CHUNK, 1024), lambda b, n: (xblk(0, b, n), 0)),
                   pl.BlockSpec((CHUNK, 1024), lambda b, n: (xblk(1, b, n), 0)),
                   pl.BlockSpec((2, 1, 1, HEADS * DV, DK), lambda b, n: (0, b, n, 0, 0))],
        out_shape=[o_shape, o_shape, jax.ShapeDtypeStruct((2, bl, ns, HEADS * DV, DK), BF16)],
        scratch_shapes=[pltpu.VMEM((2, HEADS * DV, DK), F32)],
        compiler_params=_cparams(("parallel", "arbitrary")), name="gla_fwd")(p, p, p, g2, p, p, p, g2, mmats)


def _gla_bwd(p, g2, mmats, ssave, do, bl, t, tc):
    nx, nc, rowblk, xblk = _gla_maps(bl, t, tc)
    ns = nx + nc
    rev = lambda s: ns - 1 - s

    def kern(q0, k0, v0, g0, do0, q1, k1, v1, g1, do1, m_ref, ss_ref,
             dq0, dk0, dv0, dg0, dq1, dk1, dv1, dg1, ds_ref):
        step = pl.program_id(1)
        n = ns - 1 - step

        @pl.when(step == 0)
        def _():
            ds_ref[...] = jnp.zeros(ds_ref.shape, F32)
        live = (n >= nc).astype(F32)
        sides = ((q0, k0, v0, g0, do0, dq0, dk0, dv0, dg0), (q1, k1, v1, g1, do1, dq1, dk1, dv1, dg1))
        loaded = [(m_ref[d], s[0][...].astype(F32), s[1][...].astype(F32), s[2][...].astype(F32), s[3][...])
                  for d, s in enumerate(sides)]
        dovs = [s[4][...] * live for s in sides]
        chains = [(d, h) for d in range(2) for h in range(HEADS)]
        base = [_gla_chunk(*loaded[d], h) for d, h in chains]
        pre = []
        for (d, h), (b, tot, mid, qh, kh, vh) in zip(chains, base):
            eb, ebm, emb, etb = jnp.exp(b), jnp.exp(b - mid), jnp.exp(mid - b), jnp.exp(tot - b)
            pre.append(dict(
                eb=eb, ebm=ebm, emb=emb, etb=etb, etot=jnp.exp(tot), qe=qh * eb, qm=qh * ebm, km=kh * emb, kl=kh * etb,
                vh=vh, doh=dovs[d][:, h * DV:(h + 1) * DV], s0=ss_ref[d, 0, 0, h * DV:(h + 1) * DV, :].astype(F32),
                ds1=ds_ref[d, h * DV:(h + 1) * DV, :]))
        first = [dict(att=_dot(c["qm"], c["km"], 1, 1), datt=_dot(c["doh"], c["vh"], 1, 1),
                      dqe=_dot(c["doh"], c["s0"], 1, 0), ds_add=_dot(c["doh"], c["qe"], 0, 0),
                      dkl=_dot(c["vh"], c["ds1"], 1, 0), dv_s=_dot(c["kl"], c["ds1"], 1, 1)) for c in pre]
        second = []
        for (d, h), c, f in zip(chains, pre, first):
            m = loaded[d][0]
            ds_ref[d, h * DV:(h + 1) * DV, :] = c["ds1"] * c["etot"] + f["ds_add"]
            att, datt = m * f["att"], m * f["datt"]
            second.append(dict(dqm=_dot(datt, c["km"], 1, 0), dkm=_dot(datt, c["qm"], 0, 0),
                               dv_a=_dot(att, c["doh"], 0, 0)))
        for (d, h), c, f, s in zip(chains, pre, first, second):
            dq_ref, dk_ref, dv_ref, dg_ref = sides[d][5:]
            dtot = c["etot"] * jnp.sum(c["ds1"] * c["s0"], axis=0, keepdims=True) + jnp.sum(
                f["dkl"] * c["kl"], axis=0, keepdims=True)
            db = f["dqe"] * c["qe"] + s["dqm"] * c["qm"] - s["dkm"] * c["km"] - f["dkl"] * c["kl"]
            dq_ref[:, h * DK:(h + 1) * DK] = ((f["dqe"] * c["eb"] + s["dqm"] * c["ebm"]) * Q_SCALE).astype(BF16)
            dk_ref[:, h * DK:(h + 1) * DK] = (s["dkm"] * c["emb"] + f["dkl"] * c["etb"]).astype(BF16)
            dv_ref[:, h * DV:(h + 1) * DV] = (s["dv_a"] + f["dv_s"]).astype(BF16)
            dg_ref[:, h * DK:(h + 1) * DK] = _dot01(loaded[d][0], db, 0) + dtot

    nt = p.shape[0]

    def operands(d):
        return [pl.BlockSpec((CHUNK, 512), lambda b, s: (rowblk(d, b, rev(s)), 4)),
                pl.BlockSpec((CHUNK, 512), lambda b, s: (rowblk(d, b, rev(s)), 5)),
                pl.BlockSpec((CHUNK, 1024), lambda b, s: (rowblk(d, b, rev(s)), 3)),
                pl.BlockSpec((CHUNK, 512), lambda b, s: (rowblk(d, b, rev(s)), d)),
                pl.BlockSpec((CHUNK, 1024), lambda b, s: (xblk(d, b, rev(s)), 0))]

    def results(d):
        row = lambda b, s: (rowblk(d, b, rev(s)), 0)
        return [pl.BlockSpec((CHUNK, 512), row), pl.BlockSpec((CHUNK, 512), row), pl.BlockSpec((CHUNK, 1024), row),
                pl.BlockSpec((CHUNK, 512), row)]

    shapes = [jax.ShapeDtypeStruct((nt, 512), BF16), jax.ShapeDtypeStruct((nt, 512), BF16),
              jax.ShapeDtypeStruct((nt, 1024), BF16), jax.ShapeDtypeStruct((nt, 512), F32)]
    out = pl.pallas_call(
        kern, grid=(bl, ns),
        in_specs=operands(0) + operands(1) + [
            pl.BlockSpec((2, CHUNK, CHUNK), lambda b, s: (0, 0, 0)),
            pl.BlockSpec((2, 1, 1, HEADS * DV, DK), lambda b, s: (0, b, rev(s), 0, 0))],
        out_specs=results(0) + results(1), out_shape=shapes + shapes,
        scratch_shapes=[pltpu.VMEM((2, HEADS * DV, DK), F32)],
        compiler_params=_cparams(("parallel", "arbitrary")), name="gla_bwd")(
            p, p, p, g2, do, p, p, p, g2, do, mmats, ssave)
    return out[:4], out[4:]


CONV_CT = 256
CONV_PAD = 16
CONV_RC = 128
CONV_HALO = 24


def _conv_fill(zp, z_ref, t):
    zp[0:CONV_PAD, :] = jnp.zeros((CONV_PAD, CONV_CT), F32)
    zp[CONV_PAD + t:2 * CONV_PAD + t, :] = jnp.zeros((CONV_PAD, CONV_CT), F32)
    zp[CONV_PAD:CONV_PAD + t, :] = z_ref[...]


def _dwconv(name, z, w, bias, bl, t, flip):
    def kern(z_ref, w_ref, b_ref, o_ref, zp):
        _conv_fill(zp, z_ref, t)
        offs = [(CONV_W - j) if flip else (j + 1) for j in range(CONV_W)]
        for r in range(0, t, CONV_RC):
            acc = jnp.broadcast_to(b_ref[...], (CONV_RC, CONV_CT))
            for rot in range(8):
                win = zp[r + rot:r + rot + CONV_RC + CONV_HALO, :]
                for j in range(CONV_W):
                    if offs[j] % 8 == rot:
                        a = offs[j] - rot
                        acc = acc + w_ref[j:j + 1, :] * win[a:a + CONV_RC, :]
            o_ref[r:r + CONV_RC, :] = acc

    return pl.pallas_call(
        kern, grid=(bl, 1024 // CONV_CT),
        in_specs=[pl.BlockSpec((t, CONV_CT), lambda b, c: (b, c)),
                  pl.BlockSpec((32, CONV_CT), lambda b, c: (0, c)),
                  pl.BlockSpec((1, CONV_CT), lambda b, c: (0, c))],
        out_specs=pl.BlockSpec((t, CONV_CT), lambda b, c: (b, c)),
        out_shape=jax.ShapeDtypeStruct(z.shape, F32),
        scratch_shapes=[pltpu.VMEM((t + 2 * CONV_PAD, CONV_CT), F32)],
        compiler_params=_cparams(("parallel", "parallel")), name=name)(z, w, bias)


def _dwconv_wgrad(z, dzc, bl, t):
    def kern(z_ref, d_ref, dw_ref, db_ref, zp):
        b = pl.program_id(1)

        @pl.when(b == 0)
        def _():
            dw_ref[...] = jnp.zeros(dw_ref.shape, F32)
            db_ref[...] = jnp.zeros(db_ref.shape, F32)
        _conv_fill(zp, z_ref, t)
        for rot in range(8):
            taps = [j for j in range(CONV_W) if (j + 1) % 8 == rot]
            accs = [jnp.zeros((8, CONV_CT), F32) for _ in taps]
            for r in range(0, t, CONV_RC):
                d = d_ref[r:r + CONV_RC, :]
                win = zp[r + rot:r + rot + CONV_RC + CONV_HALO, :]
                for k, j in enumerate(taps):
                    a = j + 1 - rot
                    prod = d * win[a:a + CONV_RC, :]
                    accs[k] = accs[k] + jnp.sum(prod.reshape(CONV_RC // 8, 8, CONV_CT), axis=0)
            for k, j in enumerate(taps):
                dw_ref[j:j + 1, :] += jnp.sum(accs[k], axis=0, keepdims=True)
        db_ref[...] += jnp.sum(d_ref[...], axis=0, keepdims=True)

    return pl.pallas_call(
        kern, grid=(1024 // CONV_CT, bl),
        in_specs=[pl.BlockSpec((t, CONV_CT), lambda c, b: (b, c)),
                  pl.BlockSpec((t, CONV_CT), lambda c, b: (b, c))],
        out_specs=[pl.BlockSpec((32, CONV_CT), lambda c, b: (0, c)),
                   pl.BlockSpec((1, CONV_CT), lambda c, b: (0, c))],
        out_shape=[jax.ShapeDtypeStruct((32, 1024), F32), jax.ShapeDtypeStruct((1, 1024), F32)],
        scratch_shapes=[pltpu.VMEM((t + 2 * CONV_PAD, CONV_CT), F32)],
        compiler_params=_cparams(("parallel", "arbitrary")), name="dwconv_wgrad")(z, dzc)


def _ffn_fwd(tag, xin, n_tiles, g, sh, sc, gate, wts, gu_name, down_name, tpb, nb, comm=None):
    rows = n_tiles * TM
    rm = functools.partial(_rowmap, tpb=tpb, nb=nb)
    u, ut = _with_side(
        comm, tag + "_norm", None,
        lambda s: rm(tag + "_norm", lambda i, x, g_, sh_, sc_: _twice(_modnorm(x, g_, sh_, sc_)), n_tiles,
                     [tok(xin), const(g), mod(sh), mod(sc)], [("tok", rows, D, BF16), ("tokT", rows, D, BF16)], side=s))
    w_gu, w_down = wts[gu_name], wts[down_name]
    ab, hm, hmt = _with_side(comm, tag + "_gu", None, lambda s: _swiglu_fwd(tag + "_gu", u, w_gu, side=s))
    res = dict(x=xin, gate=gate, scale=0.5, tpb=tpb)
    f, xout = _with_side(comm, tag + "_down", None,
                         lambda s: _mm(tag + "_down", hm, w_down, out_dtype=BF16, side=s, residual=res))
    return xout, (ut, ab, hmt, f)


def _ffn_bwd(tag, xin, saved, dxout, dx_clamp, n_tiles, g, sh, sc, gate, w_gu, w_down, tpb, nb, comm=None, grads=None,
             names=None, keep_tiles=None):
    ut, ab, hmt, f = saved
    rows = n_tiles * TM
    rm = functools.partial(_rowmap, tpb=tpb, nb=nb)

    def mask(i):
        return 1.0 if dx_clamp is None else (i <= dx_clamp).astype(F32)

    def b1(i, dx, f_, gt):
        dx = dx * mask(i)
        return (0.5 * gt * dx, jnp.sum(0.5 * f_ * dx, axis=0, keepdims=True))
    df, dgate = rm(tag + "_bres", b1, n_tiles, [tok(dxout, clamp=dx_clamp), tok(f), mod(gate)],
                   [("tok", rows, D, BF16), ("modacc", D)])
    grads[names[1]] = _with_side(comm, tag + "_wdown", grads, lambda s: _wgrad(tag + "_wdown", hmt, df, side=s))
    dab = _swiglu_bwd(tag + "_bdown", df, w_down, ab)
    grads[names[0]] = _with_side(comm, tag + "_wgu", grads,
                                 lambda s: _wgrad(tag + "_wgu", ut, dab, col_shards=True, side=s))
    du = _with_side(comm, tag + "_bgu", grads,
                    lambda s: _mm(tag + "_bgu", dab, w_gu, trans_b=True, out_dtype=BF16, side=s))

    def b3(i, x, g_, sh_, sc_, du_, dx):
        _, vjp = jax.vjp(_modnorm, x, g_, sh_, sc_)
        dxn, dg, dsh, dsc = vjp(du_)
        return (dx * mask(i) + dxn, dg, dsh, dsc)
    dx_out = ("tok", rows, D, F32) if keep_tiles is None else ("tok_head", keep_tiles * TM, D, F32, keep_tiles - 1)
    dxin, dg, dsh, dsc = rm(tag + "_bnorm", b3, n_tiles,
                            [tok(xin), const(g), mod(sh), mod(sc), tok(du), tok(dxout, clamp=dx_clamp)],
                            [dx_out, ("acc", 1, D), ("modacc", D), ("modacc", D)])
    return dxin, dict(g=dg, sh=dsh, sc=dsc, gate=dgate)


IN_SEGMENTS = ((0, 4096), (5152, 7200), (4096, 5120), (5120, 5152))


def _perm_in_cols(shards):
    width = shards.shape[2]
    parts = []
    for a, b in IN_SEGMENTS:
        for j in range(4):
            lo, hi = max(a, width * j), min(b, width * (j + 1))
            if lo < hi:
                parts.append(shards[j][:, lo - width * j:hi - width * j])
    parts.append(jnp.zeros((shards.shape[1], D_INP - D_IN), shards.dtype))
    return jnp.concatenate(parts, axis=1)


def _unperm_in_cols(w):
    width = D_IN // 4
    shards = []
    for j in range(4):
        parts, start = [], 0
        for a, b in IN_SEGMENTS:
            lo, hi = max(a, width * j), min(b, width * (j + 1))
            if lo < hi:
                parts.append((lo, w[:, start + lo - a:start + hi - a]))
            start += b - a
        shards.append(jnp.concatenate([p for _, p in sorted(parts, key=lambda q: q[0])], axis=1))
    return jnp.stack(shards)


def _local_step(x, c, ctx, target, wts, comm):
    bl, t, _ = x.shape
    tc = ctx.shape[1]
    nx_rows, nc_rows = bl * t, bl * tc
    nt_rows = nx_rows + nc_rows
    tpb = t // TM
    nxt, ntt = nx_rows // TM, nt_rows // TM
    nb = bl
    rm = functools.partial(_rowmap, tpb=tpb, nb=nb)
    last_x = nxt - 1

    x0 = jnp.concatenate([x.reshape(nx_rows, D), ctx.reshape(nc_rows, D)], axis=0)
    tgt = target.reshape(nx_rows, D)

    cc = jnp.concatenate([c, wts["c_ctx"].reshape(1, D), jnp.zeros((8 - bl - 1, D), F32)], axis=0)
    modv, cc_all = _mod_forward(cc, wts["w_mod_shard"], wts["b_mod_shard"])
    mods = [modv[:nb + 1, k * D:(k + 1) * D].reshape(nb + 1, 1, D) for k in range(9)]

    x1, sv1 = _ffn_fwd("ffn1", x0, ntt, wts["g_ffn1"], mods[0], mods[1], mods[2], wts, "w1_gu", "w1_down",
                       tpb, nb, comm)
    u2, u2t = rm("in_norm", lambda i, x_, g_, sh_, sc_: _twice(_modnorm(x_, g_, sh_, sc_)), ntt,
                 [tok(x1), const(wts["g_mix"]), mod(mods[3]), mod(mods[4])],
                 [("tok", nt_rows, D, BF16), ("tokT", nt_rows, D, BF16)])
    w_inp = wts["w_in_p"]
    p = _with_side(comm, "in_proj", None, lambda s: _mm("in_proj", u2, w_inp, out_dtype=BF16, side=s))

    waf, wab, baf, bab = wts["w_alpha_f_pad"], wts["w_alpha_b_pad"], wts["b_alpha_f"], wts["b_alpha_b"]

    def dec_fwd(i, lr, wf, wb, bf_, bb_):
        zf = _dot(lr, wf, 1, 0) + bf_
        zb = _dot(lr, wb, 1, 0) + bb_
        return (jnp.concatenate([jax.nn.log_sigmoid(zf) / TAU, jax.nn.log_sigmoid(zb) / TAU], axis=1),)
    (gfb,) = rm("decay_fwd", dec_fwd, ntt, [tok(p, 128, LR_COL // 128), const(waf), const(wab), const(baf), const(bab)],
                [("tok", nt_rows, 1024, F32)])
    g2 = gfb
    tri = jnp.tril(jnp.ones((CHUNK, CHUNK), F32))
    mmats = jnp.stack([tri, tri.T])
    *o2, ssave = _gla_fwd(p, g2, mmats, bl, t, tc)

    gn_g = wts["gla_norm_g"]

    def gla_out(of, ob, og, gn):
        o = of + ob
        parts = []
        for h in range(HEADS):
            oh = o[:, h * DV:(h + 1) * DV]
            parts.append(oh * lax.rsqrt(jnp.mean(oh * oh, axis=-1, keepdims=True) + EPS))
        return jnp.concatenate(parts, axis=1) * gn * _silu(og)
    yg_in, yg_int = rm("gla_out", lambda i, of, ob, og, gn: _twice(gla_out(of, ob, og, gn)), nxt,
                       [tok(o2[0]), tok(o2[1]), tok(p, 1024, OG_CB), const(gn_g)],
                       [("tok", nx_rows, D, BF16), ("tokT", nx_rows, D, BF16)])
    y_gla = _with_side(comm, "gla_proj", None,
                       lambda s: _mm("gla_proj", yg_in, wts["w_gla_out"], out_dtype=BF16, side=s))

    (z,) = rm("glu", lambda i, a, b: (a * jax.nn.sigmoid(b),), nxt, [tok(p, 1024, 0), tok(p, 1024, 1)],
              [("tok", nx_rows, D, F32)])
    dw_w = jnp.concatenate([wts["dw_weight"], jnp.zeros((1, D), F32)], axis=0)
    zc = _dwconv("dwconv_fwd", z, dw_w, wts["dw_bias"], bl, t, False)

    def ln_silu(zc_, g_, b_):
        mu = jnp.mean(zc_, axis=-1, keepdims=True)
        var = jnp.mean(jnp.square(zc_ - mu), axis=-1, keepdims=True)
        return _silu((zc_ - mu) * lax.rsqrt(var + EPS) * g_ + b_)
    ln_g, ln_b = wts["conv_ln_g"], wts["conv_ln_b"]
    zl, zlt = rm("conv_ln", lambda i, zc_, g_, b_: _twice(ln_silu(zc_, g_, b_)), nxt,
                 [tok(zc), const(ln_g), const(ln_b)], [("tok", nx_rows, D, BF16), ("tokT", nx_rows, D, BF16)])
    y_conv = _mm("conv_proj", zl, wts["w_conv_out"], out_dtype=BF16)

    mg, mgt = rm("merge", lambda i, ga, gb, yc, yg: _twice(jax.nn.sigmoid(ga) * yc + jax.nn.sigmoid(gb) * yg), nxt,
                 [tok(p, 1024, GA_CB), tok(p, 1024, GB_CB), tok(y_conv), tok(y_gla)],
                 [("tok", nx_rows, D, BF16), ("tokT", nx_rows, D, BF16)])
    mix, x2 = _mm("out_proj", mg, wts["w_out"], out_dtype=BF16,
                  residual=dict(x=x1, gate=mods[5], scale=1.0, tpb=tpb))

    x3, sv2 = _ffn_fwd("ffn2", x2, nxt, wts["g_ffn2"], mods[6], mods[7], mods[8], wts, "w2_gu", "w2_down",
                       tpb, nb)
    g_fin = wts["g_final"].reshape(1, D)

    def head(i, x_, g_, tg):
        y, vjp = jax.vjp(_rms, x_, g_)
        diff = y - tg
        dx, dg = vjp(diff * (1.0 / D))
        loss = 0.5 * jnp.sum(jnp.mean(diff * diff, axis=-1, keepdims=True))
        return dx, dg, loss
    dx3, dg_final, loss_acc = rm("loss_head", head, nxt, [tok(x3), const(g_fin), tok(tgt)],
                                 [("tok", nx_rows, D, F32), ("acc", 1, D), ("acc", 8, 128)])
    loss = loss_acc[0, 0]

    grads = {}
    dx2, gf2 = _ffn_bwd("ffn2", x2, sv2, dx3, None, nxt, wts["g_ffn2"], mods[6], mods[7], mods[8],
                        wts["w2_gu"], wts["w2_down"], tpb, nb, comm, grads, ("w2_gu", "w2_down"))
    grads["g_ffn2"] = gf2["g"]

    dmix, dgate5 = rm("mix_bres", lambda i, dx, mx_, gt: (gt * dx, jnp.sum(mx_ * dx, axis=0, keepdims=True)), nxt,
                      [tok(dx2), tok(mix), mod(mods[5])], [("tok", nx_rows, D, BF16), ("modacc", D)])
    dmg = _mm("out_bproj", dmix, wts["w_out"], trans_b=True, out_dtype=BF16)
    grads["w_out"] = _wgrad("out_wgrad", mgt, dmix)

    def merge_bwd(i, dm, ga, gb, yc, yg):
        keep = (i <= last_x).astype(F32)
        dm = dm * keep
        sa, sb = jax.nn.sigmoid(ga), jax.nn.sigmoid(gb)
        return dm * sa, dm * sb, jnp.concatenate([dm * yc * sa * (1 - sa), dm * yg * sb * (1 - sb)], axis=1)
    cl = dict(clamp=last_x)
    dyc, dyg, dp = rm("merge_bwd", merge_bwd, ntt,
                      [tok(dmg, **cl), tok(p, 1024, GA_CB, last_x), tok(p, 1024, GB_CB, last_x), tok(y_conv, **cl),
                       tok(y_gla, **cl)],
                      [("tok", nt_rows, D, BF16), ("tok", nt_rows, D, BF16),
                       ("cols", jax.ShapeDtypeStruct((nt_rows, D_INP), BF16), 2048, 2)])

    dzl = _with_side(comm, "conv_bproj", grads, lambda s: _mm("conv_bproj", dyc, wts["w_conv_out"], trans_b=True,
                                                              rows=nx_rows, out_dtype=BF16, side=s))
    grads["w_conv_out"] = _wgrad("conv_wgrad", zlt, dyc, rows=nx_rows)

    def ln_bwd(i, zc_, g_, b_, dz_):
        _, vjp = jax.vjp(ln_silu, zc_, g_, b_)
        return vjp(dz_)
    dzc, dln_g, dln_b = rm("conv_ln_bwd", ln_bwd, nxt, [tok(zc), const(ln_g), const(ln_b), tok(dzl)],
                           [("tok", nx_rows, D, F32), ("acc", 1, D), ("acc", 1, D)])
    dz = _dwconv("dwconv_bwd", dzc, dw_w, jnp.zeros((1, D), F32), bl, t, True)
    ddw, ddb = _dwconv_wgrad(z, dzc, bl, t)
    grads.update(conv_ln_g=dln_g, conv_ln_b=dln_b, dw_weight=ddw[:CONV_W], dw_bias=ddb)

    def glu_bwd(i, dz_, a, b):
        keep = (i <= last_x).astype(F32)
        dz_ = dz_ * keep
        s = jax.nn.sigmoid(b)
        return (jnp.concatenate([dz_ * s, dz_ * a * s * (1 - s)], axis=1),)
    (dp,) = rm("glu_bwd", glu_bwd, ntt, [tok(dz, **cl), tok(p, 1024, 0, last_x), tok(p, 1024, 1, last_x)],
               [("cols", dp, 2048, 0)])

    dyg_in = _mm("gla_bproj", dyg, wts["w_gla_out"], trans_b=True, rows=nx_rows, out_dtype=BF16)
    grads["w_gla_out"] = _wgrad("gla_wgrad", yg_int, dyg, rows=nx_rows)

    def gla_out_bwd(i, of, ob, og, gn, dy):
        _, vjp = jax.vjp(gla_out, of, ob, og, gn)
        do_, _, dog_, dgn_ = vjp(dy)
        return do_, dog_, dgn_
    do, dp, dgn = rm("gla_out_bwd", gla_out_bwd, nxt,
                     [tok(o2[0]), tok(o2[1]), tok(p, 1024, OG_CB), const(gn_g), tok(dyg_in)],
                     [("tok", nx_rows, D, BF16), ("cols", dp, 1024, OG_CB), ("acc", 1, D)])
    grads["gla_norm_g"] = dgn
    (dp,) = rm("og_ctx_zero", lambda i: (jnp.zeros((TM, D), F32),), ntt - nxt, [], [("cols", dp, 1024, OG_CB, nxt)])

    dq2, dk2, dv2, dg2 = zip(*_gla_bwd(p, g2, mmats, ssave, do, bl, t, tc))

    def dec_bwd(i, lr, wf, wb, bf_, bb_, dgf, dgb_):
        zf = _dot(lr, wf, 1, 0) + bf_
        zb = _dot(lr, wb, 1, 0) + bb_
        dzf = dgf * (1 - jax.nn.sigmoid(zf)) * (1.0 / TAU)
        dzb = dgb_ * (1 - jax.nn.sigmoid(zb)) * (1.0 / TAU)
        dlr = _dot(dzf, wf, 1, 1) + _dot(dzb, wb, 1, 1)
        return (dlr, _dot(lr, dzf, 0, 0), _dot(lr, dzb, 0, 0), jnp.sum(dzf, axis=0, keepdims=True),
                jnp.sum(dzb, axis=0, keepdims=True))
    dp, dwaf, dwab, dbaf, dbab = rm(
        "decay_bwd", dec_bwd, ntt,
        [tok(p, 128, LR_COL // 128), const(waf), const(wab), const(baf), const(bab), tok(dg2[0]), tok(dg2[1])],
        [("cols", dp, 128, LR_COL // 128), ("acc", 128, 512), ("acc", 128, 512), ("acc", 1, 512), ("acc", 1, 512)])
    grads.update(w_alpha_f=dwaf[:LOWRANK], w_alpha_b=dwab[LOWRANK:2 * LOWRANK], b_alpha_f=dbaf, b_alpha_b=dbab)

    (dp,) = rm("gla_sum",
               lambda i, q0, q1, k0, k1, v0, v1: (jnp.concatenate([q0 + q1, k0 + k1, v0 + v1], axis=1),), ntt,
               [tok(dq2[0]), tok(dq2[1]), tok(dk2[0]), tok(dk2[1]), tok(dv2[0]), tok(dv2[1])],
               [("cols", dp, 2048, 1)])
    du2 = _with_side(comm, "in_bproj", grads,
                     lambda s: _mm("in_bproj", dp, w_inp, trans_b=True, out_dtype=BF16, side=s))
    grads["w_in_p"] = _wgrad("in_wgrad", u2t, dp)

    def in_norm_bwd(i, x_, g_, sh_, sc_, du_, dx):
        keep = (i <= last_x).astype(F32)
        _, vjp = jax.vjp(_modnorm, x_, g_, sh_, sc_)
        dxn, dg, dsh, dsc = vjp(du_)
        return (dx * keep + dxn, dg, dsh, dsc)
    dx1, dg_mix, dsh3, dsc4 = rm("in_norm_bwd", in_norm_bwd, ntt,
                                 [tok(x1), const(wts["g_mix"]), mod(mods[3]), mod(mods[4]), tok(du2), tok(dx2, **cl)],
                                 [("tok", nt_rows, D, F32), ("acc", 1, D), ("modacc", D), ("modacc", D)])
    grads["g_mix"] = dg_mix

    dx0, gf1 = _ffn_bwd("ffn1", x0, sv1, dx1, None, ntt, wts["g_ffn1"], mods[0], mods[1], mods[2],
                        wts["w1_gu"], wts["w1_down"], tpb, nb, comm, grads, ("w1_gu", "w1_down"), keep_tiles=nxt)
    grads["g_ffn1"] = gf1["g"]
    grad_x = dx0.reshape(bl, t, D)

    dmods = [gf1["sh"], gf1["sc"], gf1["gate"], dsh3, dsc4, dgate5, gf2["sh"], gf2["sc"], gf2["gate"]]
    dmod = jnp.concatenate(
        [jnp.concatenate([a.reshape(a.shape[0], D), jnp.zeros((8 - a.shape[0], D), F32)], axis=0) for a in dmods],
        axis=1)
    grads["w_mod"], grads["c_ctx"], grads["b_mod"] = _mod_backward(dmod, cc_all, wts["w_mod_shard"], nb)
    grads["g_final"] = dg_final.reshape(D)
    return loss, grad_x, grads


ANY = pl.BlockSpec(memory_space=pl.ANY)


def _place():
    x, y, c = lax.axis_index("x"), lax.axis_index("y"), lax.axis_index("c")
    chips = [(1 - x, y), (x, 1 - y), (1 - x, 1 - y)]
    return x, y, c, chips


def _remote(send_sems, recv_sems):
    def copy(k, src, dst, to):
        return pltpu.make_async_remote_copy(src_ref=src, dst_ref=dst, send_sem=send_sems.at[k],
                                            recv_sem=recv_sems.at[k], device_id=to, device_id_type=MESH)
    return copy


def _sems(n):
    return [pltpu.SemaphoreType.DMA((n,)), pltpu.SemaphoreType.DMA((n,))]


def _swap_halves(name, gs):
    n = len(gs)

    def body(*refs):
        ins, outs = refs[:n], refs[n:2 * n]
        copy = _remote(refs[2 * n], refs[2 * n + 1])
        x, y, c, _ = _place()
        cps = []
        for i in range(n):
            hr = ins[i].shape[1] // 2
            cps.append(copy(i, ins[i].at[:, pl.ds((1 - c) * hr, hr), :], outs[i], (x, y, 1 - c)))
            cps[-1].start()
        for cp in cps:
            cp.wait()

    return pl.pallas_call(
        body, out_shape=[jax.ShapeDtypeStruct((4, g.shape[1] // 2, g.shape[2]), g.dtype) for g in gs],
        in_specs=[ANY] * n, out_specs=[ANY] * n, scratch_shapes=_sems(n), name=name)(*gs)


def _row_tile(hr):
    return hr if hr <= 256 else _pick(hr, (256, 176, 128, 64, 32, 16))


def _add_halves(name, g, r, place):
    hr = r.shape[1]
    tr = _row_tile(hr)
    nblk = hr // tr

    def kern(p_ref, g_ref, r_ref, o_ref):
        o_ref[...] = (g_ref[...].astype(F32) + r_ref[...].astype(F32)).astype(o_ref.dtype)

    blk = (1, tr, g.shape[2])
    return pl.pallas_call(
        kern,
        grid_spec=pltpu.PrefetchScalarGridSpec(
            num_scalar_prefetch=1, grid=(4, nblk),
            in_specs=[pl.BlockSpec(blk, lambda j, i, p: (j, p[0] * nblk + i, 0)),
                      pl.BlockSpec(blk, lambda j, i, p: (j, i, 0))],
            out_specs=pl.BlockSpec(blk, lambda j, i, p: (j, i, 0))),
        out_shape=jax.ShapeDtypeStruct(r.shape, r.dtype),
        compiler_params=_cparams(("parallel", "parallel")), name=name)(place, g, r)


def _scatter_chips(cs):
    n = len(cs)

    def body(*refs):
        ins, outs = refs[:n], refs[n:2 * n]
        copy = _remote(refs[2 * n], refs[2 * n + 1])
        x, y, c, chips = _place()
        me = 2 * x + y
        sends = []
        for i in range(n):
            for j, (px, py) in enumerate(chips):
                sends.append(copy(3 * i + j, ins[i].at[2 * px + py], outs[i].at[me], (px, py, c)))
                sends[-1].start()
        for i in range(n):
            for j, (px, py) in enumerate(chips):
                src = 2 * px + py
                copy(3 * i + j, ins[i].at[src], outs[i].at[src], (px, py, c)).wait_recv()
        for cp in sends:
            cp.wait_send()

    return pl.pallas_call(
        body, out_shape=[jax.ShapeDtypeStruct(a.shape, a.dtype) for a in cs], in_specs=[ANY] * n,
        out_specs=[ANY] * n, scratch_shapes=_sems(3 * n), name="grad_scatter_chips")(*cs)


def _sum_chips(name, cs, r, place):
    hr = r.shape[1]
    tr = _row_tile(hr)
    nblk = hr // tr

    def kern(p_ref, c_ref, r0, r1, r2, r3, o_ref):
        me = p_ref[1]
        acc = None
        for k, rk in enumerate((r0, r1, r2, r3)):
            val = jnp.where(me == k, c_ref[0].astype(F32), rk[0].astype(F32))
            acc = val if acc is None else acc + val
        o_ref[...] = acc

    blk = (1, tr, r.shape[2])

    def slot(k):
        return lambda i, p: (jnp.where(p[1] == k, (k + 1) % 4, k), i, 0)

    return pl.pallas_call(
        kern,
        grid_spec=pltpu.PrefetchScalarGridSpec(
            num_scalar_prefetch=1, grid=(nblk,),
            in_specs=[pl.BlockSpec(blk, lambda i, p: (p[1], i, 0))] + [pl.BlockSpec(blk, slot(k)) for k in range(4)],
            out_specs=pl.BlockSpec((tr, r.shape[2]), lambda i, p: (p[0] * nblk + i, 0))),
        out_shape=jax.ShapeDtypeStruct((2 * hr, r.shape[2]), F32),
        compiler_params=_cparams(("parallel",)), name=name)(place, cs, r, r, r, r)


def _join_halves(fs):
    n = len(fs)

    def body(*refs):
        ins, outs = refs[:n], refs[n:2 * n]
        copy = _remote(refs[2 * n], refs[2 * n + 1])
        x, y, c, _ = _place()
        cps = []
        for i in range(n):
            hr = ins[i].shape[0] // 2
            cps.append(copy(i, ins[i].at[pl.ds(c * hr, hr), :], outs[i].at[pl.ds(c * hr, hr), :], (x, y, 1 - c)))
            cps[-1].start()
        for i in range(n):
            hr = ins[i].shape[0] // 2
            other = outs[i].at[pl.ds((1 - c) * hr, hr), :]
            copy(i, other, other, (x, y, 1 - c)).wait_recv()
        for cp in cps:
            cp.wait_send()

    return pl.pallas_call(
        body, out_shape=[jax.ShapeDtypeStruct(f.shape, f.dtype) for f in fs], in_specs=[ANY] * n,
        out_specs=[ANY] * n, input_output_aliases={i: i for i in range(n)}, scratch_shapes=_sems(n),
        name="grad_join_halves")(*fs)


def _half_rows(n, hc):
    return pl.ds(hc * (n // 2), n // 2)


def _gather_ici_side(shards):
    n = len(shards)

    def copies(ins, outs, copy):
        x, y, c, chips = _place()
        me = 2 * x + y
        for i in range(n):
            rows = _half_rows(ins[i].shape[0], c)
            for j, (px, py) in enumerate(chips):
                yield (copy(3 * i + j, ins[i].at[rows, :], outs[i].at[me, rows, :], (px, py, c)),
                       outs[i].at[2 * px + py, rows, :])

    def start(ins, outs, copy):
        for cp, _ in copies(ins, outs, copy):
            cp.start()

    def finish(ins, outs, copy):
        x, y, c, chips = _place()
        k = 0
        for cp, landing in copies(ins, outs, copy):
            copy(k, landing, landing, (x, y, c)).wait_recv()
            k += 1
        for cp, _ in copies(ins, outs, copy):
            cp.wait_send()

    return dict(ins=list(shards), outs=[jax.ShapeDtypeStruct((4,) + s.shape, s.dtype) for s in shards], nsem=3 * n,
                start=start, finish=finish)


def _gather_d2d_side(shards, bufs):
    n = len(shards)

    def copies(ins, outs, copy):
        x, y, c, chips = _place()
        me = 2 * x + y
        sibling = (x, y, 1 - c)
        for i in range(n):
            a = ins[i].shape[0]
            yield copy(4 * i + 3, ins[i], outs[i].at[me], sibling), outs[i].at[me]
            for j, (px, py) in enumerate(chips):
                src = 2 * px + py
                mine = outs[i].at[src, _half_rows(a, c), :]
                yield copy(4 * i + j, mine, mine, sibling), outs[i].at[src, _half_rows(a, 1 - c), :]

    def start(ins, outs, copy):
        for cp, _ in copies(ins, outs, copy):
            cp.start()

    def finish(ins, outs, copy):
        x, y, c, _ = _place()
        for i in range(n):
            for k, (cp, landing) in enumerate(list(copies(ins, outs, copy))[4 * i:4 * i + 4]):
                sem = 4 * i + 3 if k == 0 else 4 * i + k - 1
                copy(sem, landing, landing, (x, y, 1 - c)).wait_recv()
        for cp, _ in copies(ins, outs, copy):
            cp.wait_send()

    return dict(ins=list(shards) + list(bufs), outs=[jax.ShapeDtypeStruct(b.shape, b.dtype) for b in bufs],
                nsem=4 * n, alias={n + i: i for i in range(n)}, start=start, finish=finish)


def _gather_d2d(shards, bufs):
    side = _gather_d2d_side(shards, bufs)
    n_in = len(side["ins"])

    def body(*refs):
        ins, outs = refs[:n_in], refs[n_in:n_in + len(bufs)]
        copy = _remote(*refs[n_in + len(bufs):])
        side["start"](ins, outs, copy)
        side["finish"](ins, outs, copy)

    return pl.pallas_call(
        body, out_shape=side["outs"], in_specs=[ANY] * n_in, out_specs=[ANY] * len(bufs),
        input_output_aliases=side["alias"], scratch_shapes=_sems(side["nsem"]), name="gather_weights_d2d")(*side["ins"])


def _scatter_side(cs):
    n = len(cs)

    def copies(ins, outs, copy):
        x, y, c, chips = _place()
        me = 2 * x + y
        for i in range(n):
            for j, (px, py) in enumerate(chips):
                yield copy(3 * i + j, ins[i].at[2 * px + py], outs[i].at[me], (px, py, c)), outs[i].at[2 * px + py]

    def start(ins, outs, copy):
        for cp, _ in copies(ins, outs, copy):
            cp.start()

    def finish(ins, outs, copy):
        x, y, c, _ = _place()
        for k, (cp, landing) in enumerate(copies(ins, outs, copy)):
            copy(k, landing, landing, (x, y, c)).wait_recv()
        for cp, _ in copies(ins, outs, copy):
            cp.wait_send()

    return dict(ins=list(cs), outs=[jax.ShapeDtypeStruct(a.shape, a.dtype) for a in cs], nsem=3 * n,
                start=start, finish=finish)


def _mod_forward(cc, w_shard, b_shard):
    cc_all = _allreduce_small(cc, "cond_gather", reduce=False).reshape(64, D)
    part = _mm("mod_fwd", cc_all, w_shard, a_fn=_silu, bias=b_shard)
    got = _mod_rows_exchange(part)
    return jnp.concatenate([got[j] for j in range(4)], axis=1), cc_all


def _mod_backward(dmod, cc_all, w_shard, ctx_row):
    w = w_shard.shape[1]
    blocks = _dmod_exchange(jnp.transpose(dmod.reshape(8, 4, w), (1, 0, 2))).reshape(64, w)
    dsc = _mm("mod_bproj", blocks, w_shard, trans_b=True)

    def tail(cc_ref, dsc_ref, dm_ref, s_ref, dctx_ref, db_ref):
        cc_ = cc_ref[...]
        s = jax.nn.sigmoid(cc_)
        s_ref[...] = (cc_ * s).astype(BF16)
        is_ctx = ((lax.broadcasted_iota(jnp.int32, (64, 1), 0) & 7) == ctx_row).astype(F32)
        dctx_ref[...] = 0.5 * jnp.sum(dsc_ref[...] * (s * (1 + cc_ * (1 - s))) * is_ctx, axis=0, keepdims=True)
        db_ref[...] = jnp.sum(dm_ref[...], axis=0, keepdims=True)

    s_all, dctx, db_mod = pl.pallas_call(
        tail, out_shape=[jax.ShapeDtypeStruct((64, D), BF16), jax.ShapeDtypeStruct((1, D), F32),
                         jax.ShapeDtypeStruct((1, dmod.shape[1]), F32)], name="mod_tail")(cc_all, dsc, dmod)
    return _mm_tn("mod_wgrad", s_all, blocks), dctx[0], db_mod


def _small_pack(dw, af, ab):
    return jnp.concatenate([dw, jnp.zeros((1, dw.shape[1]), F32), jnp.concatenate([af, ab], axis=1)], axis=0)


def _grad_pieces(n, grads):
    if n == "small":
        return jnp.stack([_small_pack(grads["dw_weight"][:, 256 * j:256 * (j + 1)],
                                      grads["w_alpha_f"][:, DK * j:DK * (j + 1)],
                                      grads["w_alpha_b"][:, DK * j:DK * (j + 1)]) for j in range(4)])
    if n == "w_in":
        return _unperm_in_cols(grads["w_in_p"])
    g = grads[n]
    return g if g.ndim == 3 else g.reshape(4, g.shape[0] // 4, g.shape[1])


def _swap_side(gs):
    n = len(gs)

    def copies(ins, outs, copy):
        x, y, c, _ = _place()
        for i in range(n):
            yield copy(i, ins[i].at[:, _half_rows(ins[i].shape[1], 1 - c), :], outs[i], (x, y, 1 - c))

    def start(ins, outs, copy):
        for cp in copies(ins, outs, copy):
            cp.start()

    def finish(ins, outs, copy):
        for cp in copies(ins, outs, copy):
            cp.wait()

    return dict(ins=list(gs), outs=[jax.ShapeDtypeStruct((4, g.shape[1] // 2, g.shape[2]), g.dtype) for g in gs],
                nsem=n, start=start, finish=finish)


def _chip_sums(tag, names, grads, place, swapped=None):
    gs = [_grad_pieces(n, grads) for n in names] if swapped is None else swapped[0]
    got = _swap_halves("grad_swap_" + tag, gs) if swapped is None else swapped[1]
    return [_add_halves("grad_add_" + n, g, r, place) for n, g, r in zip(names, gs, got)]


class _Overlap:
    SCATTER = {"in_bproj": ("w2_down", "w2_gu", "w_out"),
               "ffn1_wgu": ("w_conv_out", "w_gla_out", "w_in", "small"),
               "ffn1_bgu": ("w1_down", "w1_gu")}
    GATHER = {("ffn1_norm", None): ("w1_gu", "w1_down"),
              ("ffn1_gu", "ffn1_down"): ("w_in", "w_conv_out", "w_gla_out", "w_out", "small"),
              ("in_proj", "gla_proj"): ("w2_gu", "w2_down")}
    LATE = tuple(n for names in GATHER.values() for n in names)
    SWAP = {"conv_bproj": "in_bproj", "ffn1_wdown": "ffn1_wgu"}

    def __init__(self, shard_of, install, place):
        self.shard_of, self.install, self.place = shard_of, install, place
        self.bufs, self.pending, self.landed, self.swapped = {}, None, {}, {}

    def side(self, tag, grads):
        for (ici, d2d), names in self.GATHER.items():
            shards = [self.shard_of(n) for n in names]
            if tag == ici:
                return _gather_ici_side(shards)
            if tag == d2d:
                return _gather_d2d_side(shards, self.bufs[ici])
        if tag in self.SWAP:
            gs = [_grad_pieces(n, grads) for n in self.SCATTER[self.SWAP[tag]]]
            self.swapped[self.SWAP[tag]] = [gs, None]
            return _swap_side(gs)
        if tag in self.SCATTER:
            names = self.SCATTER[tag]
            self.pending = (names, _chip_sums(tag, names, grads, self.place, self.swapped.get(tag)))
            return _scatter_side(self.pending[1])
        return None

    def done(self, tag, outs):
        for (ici, d2d), names in self.GATHER.items():
            if tag == ici and d2d is None:
                self.install(dict(zip(names, _gather_d2d([self.shard_of(n) for n in names], outs))))
                return
            if tag == ici:
                self.bufs[ici] = outs
                return
            if tag == d2d:
                self.install(dict(zip(names, outs)))
                return
        if tag in self.SWAP:
            self.swapped[self.SWAP[tag]][1] = outs
            return
        for n, cs, r in zip(*self.pending, outs):
            self.landed[n] = (cs, r)


def _with_side(comm, tag, grads, call):
    side = comm.side(tag, grads) if comm is not None else None
    if side is None:
        return call(None)
    res, outs = call(side)
    comm.done(tag, outs)
    return res


def _allreduce_small(v, name="allreduce_small", reduce=True):
    def body(x_ref, out_ref, *scratch):
        gath = out_ref if not reduce else scratch[0]
        send_sems, recv_sems, local_sem = scratch[-3:]
        x, y, c, chips = _place()
        me, sibling = (x, y, c), (x, y, 1 - c)

        def slot(px, py, pc):
            return gath.at[4 * px + 2 * py + pc]

        def copy(k, block, to, src=None):
            return pltpu.make_async_remote_copy(
                src_ref=slot(*block) if src is None else src, dst_ref=slot(*block), send_sem=send_sems.at[k],
                recv_sem=recv_sems.at[k], device_id=to, device_id_type=MESH)

        mine = pltpu.make_async_copy(x_ref, slot(*me), local_sem)
        mine.start()
        first = [copy(0, me, sibling, src=x_ref)]
        first += [copy(1 + j, me, (*chip, c), src=x_ref) for j, chip in enumerate(chips)]
        for cp in first:
            cp.start()
        passed = [copy(4 + j, (*chip, c), sibling) for j, chip in enumerate(chips)]
        for j, chip in enumerate(chips):
            copy(1 + j, (*chip, c), me).wait_recv()
            passed[j].start()
        copy(0, sibling, me).wait_recv()
        for j, chip in enumerate(chips):
            copy(4 + j, (*chip, 1 - c), me).wait_recv()
        for cp in first + passed:
            cp.wait_send()
        mine.wait()
        if reduce:
            acc = gath[0]
            for k in range(1, 8):
                acc = acc + gath[k]
            out_ref[...] = acc

    vm = pl.BlockSpec(memory_space=pltpu.VMEM)
    sems = [pltpu.SemaphoreType.DMA((7,)), pltpu.SemaphoreType.DMA((7,)), pltpu.SemaphoreType.DMA(())]
    return pl.pallas_call(
        body, out_shape=jax.ShapeDtypeStruct(v.shape if reduce else (8,) + v.shape, F32), in_specs=[vm], out_specs=vm,
        scratch_shapes=([pltpu.VMEM((8,) + v.shape, F32)] if reduce else []) + sems, name=name)(v)


def _mod_rows_exchange(part):
    w = part.shape[1]

    def body(p_ref, out_ref, send_sems, recv_sems):
        x, y, c, chips = _place()
        me = 2 * x + y
        copy = _remote(send_sems, recv_sems)
        sends = []
        for j, (px, py) in enumerate(chips):
            rows = pl.ds(pl.multiple_of(8 * (4 * px + 2 * py + c), 8), 8)
            sends.append(copy(j, p_ref.at[rows, :], out_ref.at[me], (px, py, c)))
            sends[-1].start()
        out_ref[me] = p_ref[pl.ds(pl.multiple_of(8 * (4 * x + 2 * y + c), 8), 8), :]
        for j, (px, py) in enumerate(chips):
            landing = out_ref.at[2 * px + py]
            copy(j, landing, landing, (px, py, c)).wait_recv()
        for cp in sends:
            cp.wait_send()

    vm = pl.BlockSpec(memory_space=pltpu.VMEM)
    return pl.pallas_call(body, out_shape=jax.ShapeDtypeStruct((4, 8, w), F32), in_specs=[vm], out_specs=vm,
                          scratch_shapes=_sems(3), name="mod_rows_exchange")(part)


def _dmod_exchange(dm):
    w = dm.shape[2]

    def body(d_ref, out_ref, send_sems, recv_sems):
        x, y, c, _ = _place()
        copy = _remote(send_sems, recv_sems)
        mine = 4 * x + 2 * y + c
        sends = []
        for r in range(1, 8):
            tx, ty, tc = x ^ (r >> 2), y ^ ((r >> 1) & 1), c ^ (r & 1)
            sends.append(copy(r - 1, d_ref.at[2 * tx + ty], out_ref.at[mine], (tx, ty, tc)))
            sends[-1].start()
        out_ref[mine] = d_ref[2 * x + y]
        for r in range(1, 8):
            tx, ty, tc = x ^ (r >> 2), y ^ ((r >> 1) & 1), c ^ (r & 1)
            landing = out_ref.at[4 * tx + 2 * ty + tc]
            copy(r - 1, landing, landing, (tx, ty, tc)).wait_recv()
        for cp in sends:
            cp.wait_send()

    vm = pl.BlockSpec(memory_space=pltpu.VMEM)
    return pl.pallas_call(body, out_shape=jax.ShapeDtypeStruct((8, 8, w), F32), in_specs=[vm], out_specs=vm,
                          scratch_shapes=_sems(7), name="dmod_exchange")(dm)


def _adamw(name, w, g, m, v):
    r, cols = w.shape
    budget = 262144
    tr = r if r * cols <= budget else next(c for c in (256, 128, 64, 32, 16, 8) if r % c == 0 and c * cols <= budget)

    def kern(w_ref, g_ref, m_ref, v_ref, go_ref, d_ref, nm_ref, nv_ref):
        gv = g_ref[...]
        go_ref[...] = gv
        nm = ADAM_B1 * m_ref[...] + (1.0 - ADAM_B1) * gv
        nv = ADAM_B2 * v_ref[...] + (1.0 - ADAM_B2) * jnp.square(gv)
        m_hat = nm / (1.0 - ADAM_B1 ** ADAM_STEP)
        v_hat = nv / (1.0 - ADAM_B2 ** ADAM_STEP)
        d_ref[...] = -ADAM_LR * (m_hat / (jnp.sqrt(v_hat) + ADAM_EPS) + ADAM_WD * w_ref[...])
        nm_ref[...] = nm
        nv_ref[...] = nv

    spec = pl.BlockSpec((tr, cols), lambda i: (i, 0))
    shp = jax.ShapeDtypeStruct((r, cols), F32)
    return pl.pallas_call(kern, grid=(r // tr,), in_specs=[spec] * 4, out_specs=[spec] * 4, out_shape=[shp] * 4,
                          compiler_params=_cparams(("parallel",)), name=name)(w, g, m, v)


SHARDED = (("w_mod", 1), ("w1_gu", 1), ("w1_down", 0), ("w_in", 1), ("dw_weight", 1), ("w_conv_out", 0),
           ("w_alpha_f", 1), ("w_alpha_b", 1), ("w_gla_out", 0), ("w_out", 0), ("w2_gu", 1), ("w2_down", 0))
REPLICATED = ("c_ctx", "b_mod", "g_ffn1", "g_mix", "dw_bias", "conv_ln_g", "conv_ln_b", "b_alpha_f", "b_alpha_b",
              "gla_norm_g", "g_ffn2", "g_final")
WEIGHTS = ("c_ctx", "w_mod", "b_mod", "g_ffn1", "w1_gu", "w1_down", "g_mix", "w_in", "dw_weight", "dw_bias",
           "conv_ln_g", "conv_ln_b", "w_conv_out", "w_alpha_f", "b_alpha_f", "w_alpha_b", "b_alpha_b", "gla_norm_g",
           "w_gla_out", "w_out", "g_ffn2", "w2_gu", "w2_down", "g_final")
MATRICES = ("w1_gu", "w1_down", "w_in", "w_conv_out", "w_gla_out", "w_out", "w2_gu", "w2_down")


def _pack_flat(parts, align):
    flat = jnp.concatenate([p.reshape(-1) for p in parts])
    pad = (-flat.shape[0]) % align
    return jnp.concatenate([flat, jnp.zeros((pad,), flat.dtype)]).reshape(-1, 1024)


def _unpack_flat(flat2d, shapes):
    flat = flat2d.reshape(-1)
    out, off = [], 0
    for s in shapes:
        n = math.prod(s)
        out.append(flat[off:off + n].reshape(s))
        off += n
    return out


def kernel(x, c, ctx, c_ctx, w_mod, b_mod, g_ffn1, w1_gu, w1_down, g_mix, w_in, dw_weight, dw_bias, conv_ln_g, conv_ln_b, w_conv_out, w_alpha_f, b_alpha_f, w_alpha_b, b_alpha_b, gla_norm_g, w_gla_out, w_out, g_ffn2, w2_gu, w2_down, g_final, loss_target, m_c_ctx, m_w_mod, m_b_mod, m_g_ffn1, m_w1_gu, m_w1_down, m_g_mix, m_w_in, m_dw_weight, m_dw_bias, m_conv_ln_g, m_conv_ln_b, m_w_conv_out, m_w_alpha_f, m_b_alpha_f, m_w_alpha_b, m_b_alpha_b, m_gla_norm_g, m_w_gla_out, m_w_out, m_g_ffn2, m_w2_gu, m_w2_down, m_g_final, v_c_ctx, v_w_mod, v_b_mod, v_g_ffn1, v_w1_gu, v_w1_down, v_g_mix, v_w_in, v_dw_weight, v_dw_bias, v_conv_ln_g, v_conv_ln_b, v_w_conv_out, v_w_alpha_f, v_b_alpha_f, v_w_alpha_b, v_b_alpha_b, v_gla_norm_g, v_w_gla_out, v_w_out, v_g_ffn2, v_w2_gu, v_w2_down, v_g_final):
    given = dict(locals())
    w = {n: given[n] for n in WEIGHTS}
    m = {n: given["m_" + n] for n in WEIGHTS}
    v = {n: given["v_" + n] for n in WEIGHTS}

    def shard_of(n):
        if n == "small":
            return _small_pack(w["dw_weight"][0], w["w_alpha_f"][0], w["w_alpha_b"][0])
        return w[n][0].astype(BF16)

    def install(wts, got):
        for n in ("w1_gu", "w2_gu"):
            if n in got:
                wts[n] = got[n]
        for n in ("w1_down", "w2_down", "w_conv_out", "w_gla_out", "w_out"):
            if n in got:
                wts[n] = got[n].reshape(-1, D)
        if "w_in" in got:
            wts["w_in_p"] = _perm_in_cols(got["w_in"])
        if "small" in got:
            sm = got["small"]
            wts["dw_weight"] = jnp.concatenate([sm[j, :CONV_W] for j in range(4)], axis=1)
            zpad = jnp.zeros((128, HEADS * DK), BF16)
            w_af = jnp.concatenate([sm[j, 32:32 + LOWRANK, :DK] for j in range(4)], axis=1)
            w_ab = jnp.concatenate([sm[j, 32:32 + LOWRANK, DK:] for j in range(4)], axis=1)
            wts["w_alpha_f_pad"] = zpad.at[0:LOWRANK].set(w_af.astype(BF16))
            wts["w_alpha_b_pad"] = zpad.at[LOWRANK:2 * LOWRANK].set(w_ab.astype(BF16))

    wts = {n: w[n] for n in REPLICATED}
    chip = 2 * lax.axis_index("x") + lax.axis_index("y")
    mod_cols = w["w_mod"].shape[2]
    wts["w_mod_shard"] = shard_of("w_mod")
    wts["b_mod_shard"] = lax.dynamic_slice(w["b_mod"], (0, chip * mod_cols), (1, mod_cols))

    place = jnp.stack([lax.axis_index("c"), 2 * lax.axis_index("x") + lax.axis_index("y")]).astype(jnp.int32)
    comm = _Overlap(shard_of, lambda got: install(wts, got), place)
    loss, grad_x, grads = _local_step(x, c, ctx, loss_target, wts, comm)
    loss = lax.psum(loss, ("x", "y", "c"))

    tags = MATRICES + ("small",)
    rest = tuple(n for n in tags if n not in comm.landed)
    if rest:
        rest_sums = _chip_sums("rest", rest, grads, place)
        for n, cs, r in zip(rest, rest_sums, _scatter_chips(rest_sums)):
            comm.landed[n] = (cs, r)
    halves = [_sum_chips("grad_sum_" + t, *comm.landed[t], place) for t in tags]
    reduced = dict(zip(tags, _join_halves(halves)))
    g_shard = {n: reduced[n] for n in MATRICES}
    g_shard["w_mod"] = grads["w_mod"]
    g_shard["dw_weight"] = reduced["small"][:CONV_W]
    g_shard["w_alpha_f"] = reduced["small"][32:32 + LOWRANK, :DK]
    g_shard["w_alpha_b"] = reduced["small"][32:32 + LOWRANK, DK:]

    rep_shapes = [w[n].shape for n in REPLICATED]
    small = _allreduce_small(_pack_flat([grads[n].reshape(w[n].shape) for n in REPLICATED], 8 * 1024))

    g_out, d_out, m_out, v_out = {}, {}, {}, {}
    for n, _ in SHARDED:
        s2 = w[n].shape[1:]
        go, d, nm, nv = _adamw("adamw_" + n, w[n].reshape(s2), g_shard[n], m[n].reshape(s2), v[n].reshape(s2))
        g_out[n], d_out[n] = go.reshape(w[n].shape), d.reshape(w[n].shape)
        m_out[n], v_out[n] = nm.reshape(w[n].shape), nv.reshape(w[n].shape)
    pk = lambda t: _pack_flat([t[n] for n in REPLICATED], 8 * 1024)
    go, d, nm, nv = _adamw("adamw_vectors", pk(w), small, pk(m), pk(v))
    for n, gg, dd, mm, vv in zip(REPLICATED, _unpack_flat(go, rep_shapes), _unpack_flat(d, rep_shapes),
                                 _unpack_flat(nm, rep_shapes), _unpack_flat(nv, rep_shapes)):
        g_out[n], d_out[n], m_out[n], v_out[n] = gg, dd, mm, vv

    return (loss, grad_x, *[g_out[n] for n in WEIGHTS], *[d_out[n] for n in WEIGHTS],
            *[m_out[n] for n in WEIGHTS], *[v_out[n] for n in WEIGHTS])
```

```python
import functools
import math

import jax
import jax.numpy as jnp
from jax import lax
from jax.experimental import pallas as pl
from jax.experimental.pallas import tpu as pltpu

F32, BF16 = jnp.float32, jnp.bfloat16
MESH = pl.DeviceIdType.MESH

D = 1024
FF = 2816
HEADS, DK, DV = 4, 128, 256
LOWRANK = 16
CONV_W = 31
CHUNK = 64
TAU = 16.0
EPS = 1e-6
Q_SCALE = DK ** -0.5
TM = 512
D_IN = 7200
D_INP = 7296
LR_COL = 7168
OG_CB, GA_CB, GB_CB = 6, 4, 5
VMEM_LIMIT = 52 * 1024 * 1024
WGRAD_VMEM = 40 * 1024 * 1024

ADAM_LR, ADAM_B1, ADAM_B2, ADAM_EPS, ADAM_WD, ADAM_STEP = 0.001, 0.9, 0.999, 1e-08, 0.01, 10


def _silu(x):
    return x * jax.nn.sigmoid(x)


def _rms(h, g):
    return h * lax.rsqrt(jnp.mean(h * h, axis=-1, keepdims=True) + EPS) * g


def _modnorm(x, g, shift, scale):
    return _rms(x, g) * (1 + scale) + shift


def _cparams(sem=None):
    return pltpu.CompilerParams(dimension_semantics=sem, vmem_limit_bytes=VMEM_LIMIT)


def _twice(v):
    return v, v


def tok(arr, width=None, cb=0, clamp=None):
    return ("tok", arr, arr.shape[1] if width is None else width, cb, clamp)


def mod(arr):
    return ("mod", arr)


def const(arr):
    return ("const", arr)


def _rowmap(name, body, n_tiles, ins, outs, *, tpb, nb, side=None):
    def modrow(i):
        return jnp.minimum(i // tpb, nb)

    in_specs, args = [], []
    for spec in ins:
        if spec[0] == "tok":
            _, arr, width, cb, clamp = spec
            if clamp is None:
                im = lambda i, cb=cb: (i, cb)
            else:
                im = lambda i, cb=cb, clamp=clamp: (jnp.minimum(i, clamp), cb)
            in_specs.append(pl.BlockSpec((TM, width), im))
        elif spec[0] == "mod":
            arr = spec[1]
            in_specs.append(pl.BlockSpec((1, 1, arr.shape[2]), lambda i: (modrow(i), 0, 0)))
        else:
            arr = spec[1]
            in_specs.append(pl.BlockSpec(arr.shape, lambda i, nd=arr.ndim: (0,) * nd))
        args.append(arr)
    out_specs, out_shapes, aliases = [], [], {}
    for o in outs:
        if o[0] == "tok":
            _, rows, width, dtype = o
            out_shapes.append(jax.ShapeDtypeStruct((rows, width), dtype))
            out_specs.append(pl.BlockSpec((TM, width), lambda i: (i, 0)))
        elif o[0] == "cols":
            buf, width, cb = o[1:4]
            first_tile = o[4] if len(o) > 4 else 0
            if not isinstance(buf, jax.ShapeDtypeStruct):
                aliases[len(args)] = len(out_shapes)
                in_specs.append(ANY)
                args.append(buf)
            out_shapes.append(jax.ShapeDtypeStruct(buf.shape, buf.dtype))
            out_specs.append(pl.BlockSpec((TM, width), lambda i, cb=cb, t0=first_tile: (i + t0, cb)))
        elif o[0] == "tok_head":
            _, rows, width, dtype, last = o
            out_shapes.append(jax.ShapeDtypeStruct((rows, width), dtype))
            out_specs.append(pl.BlockSpec((TM, width), lambda i, last=last: (jnp.minimum(i, last), 0)))
        elif o[0] == "tokT":
            _, rows, width, dtype = o
            out_shapes.append(jax.ShapeDtypeStruct((width, rows), dtype))
            out_specs.append(pl.BlockSpec((width, TM), lambda i: (0, i)))
        elif o[0] == "acc":
            _, rows, width = o
            out_shapes.append(jax.ShapeDtypeStruct((rows, width), F32))
            out_specs.append(pl.BlockSpec((rows, width), lambda i: (0, 0)))
        else:
            width = o[1]
            rows_visited = min((n_tiles - 1) // tpb, nb) + 1
            out_shapes.append(jax.ShapeDtypeStruct((rows_visited, 1, width), F32))
            out_specs.append(pl.BlockSpec((1, 1, width), lambda i: (modrow(i), 0, 0)))
    n_in = len(ins)

    def kern(*refs):
        i = pl.program_id(0)
        vals = []
        for r, spec in zip(refs[:n_in], ins):
            val = r[0] if spec[0] == "mod" else r[...]
            vals.append(val.astype(F32) if spec[0] == "tok" and val.dtype == BF16 else val)
        res = body(i, *vals)
        for r, o, val in zip(refs[len(args):], outs, res):
            if o[0] in ("tok", "cols"):
                r[...] = val.astype(r.dtype)
            elif o[0] == "tok_head":
                @pl.when(i <= o[4])
                def _():
                    r[...] = val.astype(r.dtype)
            elif o[0] == "tokT":
                r[...] = val.T.astype(r.dtype)
            elif o[0] == "acc":
                @pl.when(i == 0)
                def _():
                    r[...] = jnp.zeros(r.shape, F32)
                r[...] += jnp.broadcast_to(val, r.shape)
            else:
                first = jnp.logical_or(i == 0, modrow(i) != modrow(jnp.maximum(i - 1, 0)))

                @pl.when(first)
                def _():
                    r[...] = jnp.zeros(r.shape, F32)
                r[0] += val

    if side is not None:
        assert not aliases
        return _pallas(kern, grid=(n_tiles,), in_specs=in_specs, out_specs=out_specs, out_shape=out_shapes,
                       scratch_shapes=[], sem=("arbitrary",), name=name, args=args, side=side)
    return pl.pallas_call(
        kern, grid=(n_tiles,), in_specs=in_specs, out_specs=out_specs, out_shape=out_shapes,
        input_output_aliases=aliases, compiler_params=_cparams(("arbitrary",)), name=name)(*args)


def _pick(n, cands):
    for c in cands:
        if n % c == 0:
            return c
    return n


def _pallas(kern, *, grid, in_specs, out_specs, out_shape, scratch_shapes, sem, name, args, side=None):
    if side is None:
        return pl.pallas_call(kern, grid=grid, in_specs=in_specs, out_specs=out_specs, out_shape=out_shape,
                              scratch_shapes=scratch_shapes, compiler_params=_cparams(sem), name=name)(*args)
    single = not isinstance(out_shape, (list, tuple))
    shapes = [out_shape] if single else list(out_shape)
    ospecs = [out_specs] if single else list(out_specs)
    n_in, n_out, n_scr = len(in_specs), len(shapes), len(scratch_shapes)
    s_in, s_out = list(side["ins"]), list(side["outs"])

    def wrapped(*refs):
        pos = [0]

        def take(n):
            pos[0] += n
            return refs[pos[0] - n:pos[0]]
        ins, sins, outs, souts, scr, sems = take(n_in), take(len(s_in)), take(n_out), take(len(s_out)), take(n_scr), take(2)
        ids = [pl.program_id(k) for k in range(len(grid))]
        first = functools.reduce(jnp.logical_and, [i == 0 for i in ids])
        last = functools.reduce(jnp.logical_and, [i == g - 1 for i, g in zip(ids, grid)])
        copy = _remote(*sems)

        @pl.when(first)
        def _():
            side["start"](sins, souts, copy)
        kern(*ins, *outs, *scr)

        @pl.when(last)
        def _():
            side["finish"](sins, souts, copy)

    res = pl.pallas_call(
        wrapped, grid=grid, in_specs=list(in_specs) + [ANY] * len(s_in), out_specs=ospecs + [ANY] * len(s_out),
        out_shape=shapes + s_out, scratch_shapes=list(scratch_shapes) + _sems(side["nsem"]),
        input_output_aliases={n_in + a: n_out + b for a, b in side.get("alias", {}).items()},
        compiler_params=_cparams(("arbitrary",) * len(grid)), name=name)(*args, *s_in)
    main = res[:n_out]
    return (main[0] if single else main), list(res[n_out:])


def _mm(name, a, b, *, trans_b=False, out_dtype=F32, a_fn=None, bias=None, rows=None, side=None, residual=None):
    m, k = a.shape if a.ndim == 2 else (a.shape[1], 2 * a.shape[2])
    m = m if rows is None else rows
    shard = b.shape[2] if b.ndim == 3 else None
    if trans_b:
        n = b.shape[-2]
        tk = _pick(shard, (2816, 2304, 1408, 1024)) if shard else (
            k if k <= 2816 else _pick(k, (2816, 2432, 2304, 2048, 1536, 1408, 1024, 512, 256, 128)))
        tn = _pick(n, (1024, 512, 384, 256, 128))
    else:
        n = 4 * shard if shard else b.shape[1]
        tk = k if k <= 2816 else _pick(k, (2816, 2432, 2304, 2048, 1536, 1408, 1024, 512, 256, 128))
        tn = _pick(shard, (512, 384, 1408, 256, 128)) if shard else _pick(n, (1024, 2432, 512, 384, 256, 128))
    if residual is not None:
        tm = next(c for c in (512, 256) if m % c == 0 and (residual["tpb"] * TM) % c == 0)
    else:
        tm = _pick(m, (1024, 512, 256))
    nk = k // tk
    per = shard // (tk if trans_b else tn) if shard else None
    dims = (((1,), (1,)), ((), ())) if trans_b else (((1,), (0,)), ((), ()))

    def kern(*refs):
        a_ref, b_ref = refs[0], refs[1]
        bias_ref = refs[2] if bias is not None else None
        acc_ref = refs[-1]
        if residual is not None:
            x_ref, gate_ref, o_ref, xo_ref = refs[-5:-1]
        else:
            o_ref = refs[-2]
        kk = pl.program_id(2)
        av = a_ref[...]
        if a_fn is not None:
            av = a_fn(av)
        p = lax.dot_general(av.astype(BF16), b_ref[...].astype(BF16), dims, preferred_element_type=F32)

        def finish(total):
            if bias_ref is not None:
                total = total + bias_ref[...]
            o_ref[...] = total.astype(o_ref.dtype)
            if residual is not None:
                xo_ref[...] = x_ref[...] + residual["scale"] * gate_ref[0] * total

        if nk == 1:
            finish(p)
        else:
            @pl.when(kk == 0)
            def _():
                acc_ref[...] = p

            @pl.when(kk > 0)
            def _():
                acc_ref[...] += p

            @pl.when(kk == nk - 1)
            def _():
                finish(acc_ref[...])

    if shard and trans_b:
        b_spec = pl.BlockSpec((None, tn, tk), lambda i, j, kk: (kk // per, j, kk % per))
    elif shard:
        b_spec = pl.BlockSpec((None, tk, tn), lambda i, j, kk: (j // per, kk, j % per))
    elif trans_b:
        b_spec = pl.BlockSpec((tn, tk), lambda i, j, kk: (j, kk))
    else:
        b_spec = pl.BlockSpec((tk, tn), lambda i, j, kk: (kk, j))
    if a.ndim == 3:
        pa = a.shape[2] // tk
        a_spec = pl.BlockSpec((None, tm, tk), lambda i, j, kk: (kk // pa, i, kk % pa))
    else:
        a_spec = pl.BlockSpec((tm, tk), lambda i, j, kk: (i, kk))
    in_specs = [a_spec, b_spec]
    args = [a, b]
    if bias is not None:
        in_specs.append(pl.BlockSpec((1, tn), lambda i, j, kk: (0, j)))
        args.append(bias)
    out_specs = pl.BlockSpec((tm, tn), lambda i, j, kk: (i, j))
    out_shape = jax.ShapeDtypeStruct((m, n), out_dtype)
    if residual is not None:
        tiles, nb = residual["tpb"] * TM // tm, residual["gate"].shape[0] - 1
        in_specs += [out_specs, pl.BlockSpec((1, 1, tn), lambda i, j, kk: (jnp.minimum(i // tiles, nb), 0, j))]
        args += [residual["x"], residual["gate"]]
        out_specs, out_shape = [out_specs, out_specs], [out_shape, jax.ShapeDtypeStruct((m, n), F32)]
    return _pallas(
        kern, grid=(m // tm, n // tn, nk), in_specs=in_specs, out_specs=out_specs, out_shape=out_shape,
        scratch_shapes=[pltpu.VMEM((tm, tn) if nk > 1 else (8, 128), F32)],
        sem=("parallel", "parallel", "arbitrary"), name=name, args=args, side=side)


def _mm_tn(name, x, dy):
    t, k1, n1 = x.shape[0], x.shape[1], dy.shape[1]
    tt = _pick(t, (512, 256, 128, 64, 8))
    tk1 = _pick(k1, (1024, 512, 256, 128))
    tn = _pick(n1, (512, 384, 256, 128))
    ns = t // tt

    def kern(x_ref, dy_ref, o_ref, acc_ref):
        s = pl.program_id(2)
        p = lax.dot_general(x_ref[...].astype(BF16), dy_ref[...].astype(BF16), (((0,), (0,)), ((), ())),
                            preferred_element_type=F32)

        @pl.when(s == 0)
        def _():
            acc_ref[...] = p

        @pl.when(s > 0)
        def _():
            acc_ref[...] += p

        @pl.when(s == ns - 1)
        def _():
            o_ref[...] = acc_ref[...].astype(o_ref.dtype)

    return pl.pallas_call(
        kern, grid=(k1 // tk1, n1 // tn, ns),
        in_specs=[pl.BlockSpec((tt, tk1), lambda i, j, s: (s, i)), pl.BlockSpec((tt, tn), lambda i, j, s: (s, j))],
        out_specs=pl.BlockSpec((tk1, tn), lambda i, j, s: (i, j)), out_shape=jax.ShapeDtypeStruct((k1, n1), F32),
        scratch_shapes=[pltpu.VMEM((tk1, tn), F32)],
        compiler_params=_cparams(("parallel", "parallel", "arbitrary")), name=name)(x, dy)


def _wgrad(name, xt, dy, rows=None, col_shards=False, side=None):
    k1 = xt.shape[0]
    t = xt.shape[1] if rows is None else rows
    n1 = dy.shape[1] if dy.ndim == 2 else 2 * dy.shape[2]
    tn = _pick(n1 // 4, (1408, 512, 384, 256, 128)) if col_shards else _pick(n1, (1024, 2432, 512, 384, 256, 128))

    def token_tile(tm):
        fixed = tm * tn * (4 + 4 + 2 * 2)
        return next((c for c in (2048, 1536, 1024, 512, 256, 128)
                     if t % c == 0 and fixed + 4 * c * (tm + tn) <= WGRAD_VMEM), 128)
    tm = next((c for c in (1024, 1408, 512, 256) if k1 % c == 0 and token_tile(c) >= 1024),
              _pick(k1, (1024, 1408, 512, 256)))
    tk = token_tile(tm)
    ns = t // tk
    per = n1 // 4 // tn

    def kern(x_ref, dy_ref, o_ref, acc_ref):
        s = pl.program_id(2)
        p = jnp.dot(x_ref[...], dy_ref[...], preferred_element_type=F32)

        @pl.when(s == 0)
        def _():
            acc_ref[...] = p

        @pl.when(s > 0)
        def _():
            acc_ref[...] += p

        @pl.when(s == ns - 1)
        def _():
            o_ref[...] = acc_ref[...].astype(o_ref.dtype)

    if dy.ndim == 3:
        pd = dy.shape[2] // tn
        dy_spec = pl.BlockSpec((None, tk, tn), lambda i, j, s: (j // pd, s, j % pd))
    else:
        dy_spec = pl.BlockSpec((tk, tn), lambda i, j, s: (s, j))
    if col_shards:
        out_spec = pl.BlockSpec((None, tm, tn), lambda i, j, s: (j // per, i, j % per))
        out_shape = jax.ShapeDtypeStruct((4, k1, n1 // 4), BF16)
    else:
        out_spec = pl.BlockSpec((tm, tn), lambda i, j, s: (i, j))
        out_shape = jax.ShapeDtypeStruct((k1, n1), BF16)
    return _pallas(
        kern, grid=(k1 // tm, n1 // tn, ns),
        in_specs=[pl.BlockSpec((tm, tk), lambda i, j, s: (i, s)), dy_spec],
        out_specs=out_spec, out_shape=out_shape, scratch_shapes=[pltpu.VMEM((tm, tn), F32)],
        sem=("parallel", "parallel", "arbitrary"), name=name, args=[xt, dy], side=side)


def _swiglu_fwd(name, u, w_gu, side=None):
    m = u.shape[0]
    half = w_gu.shape[2]
    tm = _pick(m, (512, 256))

    def kern(u_ref, wa_ref, wb_ref, ab_ref, hm_ref, hmt_ref):
        uv = u_ref[...]
        a = jnp.dot(uv, wa_ref[...], preferred_element_type=F32)
        b = jnp.dot(uv, wb_ref[...], preferred_element_type=F32)
        s = jax.nn.sigmoid(a)
        silu_a = a * s
        ab_ref[0] = (b * (s * (1 + a * (1 - s)))).astype(BF16)
        ab_ref[1] = silu_a.astype(BF16)
        hm = (silu_a * b).astype(BF16)
        hm_ref[...] = hm
        hmt_ref[...] = hm.T

    return _pallas(
        kern, grid=(2, m // tm),
        in_specs=[pl.BlockSpec((tm, D), lambda j, i: (i, 0)),
                  pl.BlockSpec((None, D, half), lambda j, i: (j, 0, 0)),
                  pl.BlockSpec((None, D, half), lambda j, i: (2 + j, 0, 0))],
        out_specs=[pl.BlockSpec((2, tm, half), lambda j, i: (0, i, j)),
                   pl.BlockSpec((tm, half), lambda j, i: (i, j)),
                   pl.BlockSpec((half, tm), lambda j, i: (j, i))],
        out_shape=[jax.ShapeDtypeStruct((2, m, FF), BF16), jax.ShapeDtypeStruct((m, FF), BF16),
                   jax.ShapeDtypeStruct((FF, m), BF16)],
        scratch_shapes=[], sem=("parallel", "parallel"), name=name, args=[u, w_gu, w_gu], side=side)


def _swiglu_bwd(name, df, w_down, ab):
    m = df.shape[0]
    half = FF // 2
    tm = _pick(m, (512, 256))

    def kern(df_ref, w_ref, ab_ref, o_ref):
        dh = lax.dot_general(df_ref[...], w_ref[...], (((1,), (1,)), ((), ())), preferred_element_type=F32)
        o_ref[0] = (dh * ab_ref[0].astype(F32)).astype(BF16)
        o_ref[1] = (dh * ab_ref[1].astype(F32)).astype(BF16)

    return pl.pallas_call(
        kern, grid=(2, m // tm),
        in_specs=[pl.BlockSpec((tm, D), lambda j, i: (i, 0)),
                  pl.BlockSpec((half, D), lambda j, i: (j, 0)),
                  pl.BlockSpec((2, tm, half), lambda j, i: (0, i, j))],
        out_specs=pl.BlockSpec((2, tm, half), lambda j, i: (0, i, j)),
        out_shape=jax.ShapeDtypeStruct((2, m, FF), BF16),
        compiler_params=_cparams(("parallel", "parallel")), name=name)(df, w_down, ab)


def _gla_maps(bl, t, tc):
    nx, nc = t // CHUNK, tc // CHUNK
    nxb = bl * nx

    def rowblk(d, b, n):
        c_ctx = jnp.where(d == 0, n, nc - 1 - n)
        c_x = jnp.where(d == 0, n - nc, nx - 1 - (n - nc))
        return jnp.where(n < nc, nxb + b * nc + c_ctx, b * nx + c_x)

    def xblk(d, b, n):
        n2 = jnp.maximum(n, nc)
        return b * nx + jnp.where(d == 0, n2 - nc, nx - 1 - (n2 - nc))

    return nx, nc, rowblk, xblk


def _dot01(m, x, cm):
    x1 = x.astype(BF16)
    r1 = x - x1.astype(F32)
    x2 = r1.astype(BF16)
    x3 = (r1 - x2.astype(F32)).astype(BF16)
    w = x.shape[1]
    p = lax.dot_general(m.astype(BF16), jnp.concatenate([x1, x2, x3], axis=1), (((cm,), (0,)), ((), ())),
                        preferred_element_type=F32)
    return p[:, :w] + p[:, w:2 * w] + p[:, 2 * w:]


def _gla_chunk(m, q, k, v, g, h):
    gh = g[:, h * DK:(h + 1) * DK]
    b = _dot01(m, gh, 1)
    tot = jnp.sum(gh, axis=0, keepdims=True)
    mid = b[CHUNK // 2:CHUNK // 2 + 1, :]
    qh = q[:, h * DK:(h + 1) * DK] * Q_SCALE
    kh = k[:, h * DK:(h + 1) * DK]
    vh = v[:, h * DV:(h + 1) * DV]
    return b, tot, mid, qh, kh, vh


def _dot(a, b, ca, cb):
    return lax.dot_general(a.astype(BF16), b.astype(BF16), (((ca,), (cb,)), ((), ())),
                           preferred_element_type=F32)


def _gla_fwd(p, g2, mmats, bl, t, tc):
    nx, nc, rowblk, xblk = _gla_maps(bl, t, tc)
    ns = nx + nc

    def kern(q0, k0, v0, g0, q1, k1, v1, g1, m_ref, o0, o1, ss_ref, s_ref):
        n = pl.program_id(1)

        @pl.when(n == 0)
        def _():
            s_ref[...] = jnp.zeros(s_ref.shape, F32)
        sides = ((q0, k0, v0, g0, o0), (q1, k1, v1, g1, o1))
        loaded = [(m_ref[d], q_ref[...].astype(F32), k_ref[...].astype(F32), v_ref[...].astype(F32), g_ref[...])
                  for d, (q_ref, k_ref, v_ref, g_ref, _) in enumerate(sides)]
        chains = [(d, h) for d in range(2) for h in range(HEADS)]
        base = [_gla_chunk(*loaded[d], h) for d, h in chains]
        pre = []
        for (d, h), (b, tot, mid, qh, kh, vh) in zip(chains, base):
            s0 = s_ref[d, h * DV:(h + 1) * DV, :]
            ss_ref[d, 0, 0, h * DV:(h + 1) * DV, :] = s0.astype(BF16)
            pre.append((s0, kh * jnp.exp(tot - b), qh * jnp.exp(b), qh * jnp.exp(b - mid), kh * jnp.exp(mid - b)))
        raw = [(_dot(qm, km, 1, 1), _dot(qe, s0, 1, 1), _dot(bs[5], kl, 0, 0))
               for bs, (s0, kl, qe, qm, km) in zip(base, pre)]
        for (d, h), bs, (s0, kl, qe, qm, km), (att_raw, inter, s_add) in zip(chains, base, pre, raw):
            s_ref[d, h * DV:(h + 1) * DV, :] = s0 * jnp.exp(bs[1]) + s_add
            sides[d][4][:, h * DV:(h + 1) * DV] = (inter + _dot(loaded[d][0] * att_raw, bs[5], 1, 0)).astype(BF16)

    def operands(d):
        return [pl.BlockSpec((CHUNK, 512), lambda b, n: (rowblk(d, b, n), 4)),
                pl.BlockSpec((CHUNK, 512), lambda b, n: (rowblk(d, b, n), 5)),
                pl.BlockSpec((CHUNK, 1024), lambda b, n: (rowblk(d, b, n), 3)),
                pl.BlockSpec((CHUNK, 512), lambda b, n: (rowblk(d, b, n), d))]

    o_shape = jax.ShapeDtypeStruct((bl * t, HEADS * DV), BF16)
    return pl.pallas_call(
        kern, grid=(bl, ns),
        in_specs=operands(0) + operands(1) + [pl.BlockSpec((2, CHUNK, CHUNK), lambda b, n: (0, 0, 0))],
        out_specs=[pl.BlockSpec((CHUNK, 1024), lambda b, n: (xblk(0, b, n), 0)),
                   pl.BlockSpec((CHUNK, 1024), lambda b, n: (xblk(1, b, n), 0)),
                   pl.BlockSpec((2, 1, 1, HEADS * DV, DK), lambda b, n: (0, b, n, 0, 0))],
        out_shape=[o_shape, o_shape, jax.ShapeDtypeStruct((2, bl, ns, HEADS * DV, DK), BF16)],
        scratch_shapes=[pltpu.VMEM((2, HEADS * DV, DK), F32)],
        compiler_params=_cparams(("parallel", "arbitrary")), name="gla_fwd")(p, p, p, g2, p, p, p, g2, mmats)


def _gla_bwd(p, g2, mmats, ssave, do, bl, t, tc):
    nx, nc, rowblk, xblk = _gla_maps(bl, t, tc)
    ns = nx + nc
    rev = lambda s: ns - 1 - s

    def kern(q0, k0, v0, g0, do0, q1, k1, v1, g1, do1, m_ref, ss_ref,
             dq0, dk0, dv0, dg0, dq1, dk1, dv1, dg1, ds_ref):
        step = pl.program_id(1)
        n = ns - 1 - step

        @pl.when(step == 0)
        def _():
            ds_ref[...] = jnp.zeros(ds_ref.shape, F32)
        live = (n >= nc).astype(F32)
        sides = ((q0, k0, v0, g0, do0, dq0, dk0, dv0, dg0), (q1, k1, v1, g1, do1, dq1, dk1, dv1, dg1))
        loaded = [(m_ref[d], s[0][...].astype(F32), s[1][...].astype(F32), s[2][...].astype(F32), s[3][...])
                  for d, s in enumerate(sides)]
        dovs = [s[4][...] * live for s in sides]
        chains = [(d, h) for d in range(2) for h in range(HEADS)]
        base = [_gla_chunk(*loaded[d], h) for d, h in chains]
        pre = []
        for (d, h), (b, tot, mid, qh, kh, vh) in zip(chains, base):
            eb, ebm, emb, etb = jnp.exp(b), jnp.exp(b - mid), jnp.exp(mid - b), jnp.exp(tot - b)
            pre.append(dict(
                eb=eb, ebm=ebm, emb=emb, etb=etb, etot=jnp.exp(tot), qe=qh * eb, qm=qh * ebm, km=kh * emb, kl=kh * etb,
                vh=vh, doh=dovs[d][:, h * DV:(h + 1) * DV], s0=ss_ref[d, 0, 0, h * DV:(h + 1) * DV, :].astype(F32),
                ds1=ds_ref[d, h * DV:(h + 1) * DV, :]))
        first = [dict(att=_dot(c["qm"], c["km"], 1, 1), datt=_dot(c["doh"], c["vh"], 1, 1),
                      dqe=_dot(c["doh"], c["s0"], 1, 0), ds_add=_dot(c["doh"], c["qe"], 0, 0),
                      dkl=_dot(c["vh"], c["ds1"], 1, 0), dv_s=_dot(c["kl"], c["ds1"], 1, 1)) for c in pre]
        second = []
        for (d, h), c, f in zip(chains, pre, first):
            m = loaded[d][0]
            ds_ref[d, h * DV:(h + 1) * DV, :] = c["ds1"] * c["etot"] + f["ds_add"]
            att, datt = m * f["att"], m * f["datt"]
            second.append(dict(dqm=_dot(datt, c["km"], 1, 0), dkm=_dot(datt, c["qm"], 0, 0),
                               dv_a=_dot(att, c["doh"], 0, 0)))
        for (d, h), c, f, s in zip(chains, pre, first, second):
            dq_ref, dk_ref, dv_ref, dg_ref = sides[d][5:]
            dtot = c["etot"] * jnp.sum(c["ds1"] * c["s0"], axis=0, keepdims=True) + jnp.sum(
                f["dkl"] * c["kl"], axis=0, keepdims=True)
            db = f["dqe"] * c["qe"] + s["dqm"] * c["qm"] - s["dkm"] * c["km"] - f["dkl"] * c["kl"]
            dq_ref[:, h * DK:(h + 1) * DK] = ((f["dqe"] * c["eb"] + s["dqm"] * c["ebm"]) * Q_SCALE).astype(BF16)
            dk_ref[:, h * DK:(h + 1) * DK] = (s["dkm"] * c["emb"] + f["dkl"] * c["etb"]).astype(BF16)
            dv_ref[:, h * DV:(h + 1) * DV] = (s["dv_a"] + f["dv_s"]).astype(BF16)
            dg_ref[:, h * DK:(h + 1) * DK] = _dot01(loaded[d][0], db, 0) + dtot

    nt = p.shape[0]

    def operands(d):
        return [pl.BlockSpec((CHUNK, 512), lambda b, s: (rowblk(d, b, rev(s)), 4)),
                pl.BlockSpec((CHUNK, 512), lambda b, s: (rowblk(d, b, rev(s)), 5)),
                pl.BlockSpec((CHUNK, 1024), lambda b, s: (rowblk(d, b, rev(s)), 3)),
                pl.BlockSpec((CHUNK, 512), lambda b, s: (rowblk(d, b, rev(s)), d)),
                pl.BlockSpec((CHUNK, 1024), lambda b, s: (xblk(d, b, rev(s)), 0))]

    def results(d):
        row = lambda b, s: (rowblk(d, b, rev(s)), 0)
        return [pl.BlockSpec((CHUNK, 512), row), pl.BlockSpec((CHUNK, 512), row), pl.BlockSpec((CHUNK, 1024), row),
                pl.BlockSpec((CHUNK, 512), row)]

    shapes = [jax.ShapeDtypeStruct((nt, 512), BF16), jax.ShapeDtypeStruct((nt, 512), BF16),
              jax.ShapeDtypeStruct((nt, 1024), BF16), jax.ShapeDtypeStruct((nt, 512), F32)]
    out = pl.pallas_call(
        kern, grid=(bl, ns),
        in_specs=operands(0) + operands(1) + [
            pl.BlockSpec((2, CHUNK, CHUNK), lambda b, s: (0, 0, 0)),
            pl.BlockSpec((2, 1, 1, HEADS * DV, DK), lambda b, s: (0, b, rev(s), 0, 0))],
        out_specs=results(0) + results(1), out_shape=shapes + shapes,
        scratch_shapes=[pltpu.VMEM((2, HEADS * DV, DK), F32)],
        compiler_params=_cparams(("parallel", "arbitrary")), name="gla_bwd")(
            p, p, p, g2, do, p, p, p, g2, do, mmats, ssave)
    return out[:4], out[4:]


CONV_CT = 256
CONV_PAD = 16
CONV_RC = 128
CONV_HALO = 24


def _conv_fill(zp, z_ref, t):
    zp[0:CONV_PAD, :] = jnp.zeros((CONV_PAD, CONV_CT), F32)
    zp[CONV_PAD + t:2 * CONV_PAD + t, :] = jnp.zeros((CONV_PAD, CONV_CT), F32)
    zp[CONV_PAD:CONV_PAD + t, :] = z_ref[...]


def _dwconv(name, z, w, bias, bl, t, flip):
    def kern(z_ref, w_ref, b_ref, o_ref, zp):
        _conv_fill(zp, z_ref, t)
        offs = [(CONV_W - j) if flip else (j + 1) for j in range(CONV_W)]
        for r in range(0, t, CONV_RC):
            acc = jnp.broadcast_to(b_ref[...], (CONV_RC, CONV_CT))
            for rot in range(8):
                win = zp[r + rot:r + rot + CONV_RC + CONV_HALO, :]
                for j in range(CONV_W):
                    if offs[j] % 8 == rot:
                        a = offs[j] - rot
                        acc = acc + w_ref[j:j + 1, :] * win[a:a + CONV_RC, :]
            o_ref[r:r + CONV_RC, :] = acc

    return pl.pallas_call(
        kern, grid=(bl, 1024 // CONV_CT),
        in_specs=[pl.BlockSpec((t, CONV_CT), lambda b, c: (b, c)),
                  pl.BlockSpec((32, CONV_CT), lambda b, c: (0, c)),
                  pl.BlockSpec((1, CONV_CT), lambda b, c: (0, c))],
        out_specs=pl.BlockSpec((t, CONV_CT), lambda b, c: (b, c)),
        out_shape=jax.ShapeDtypeStruct(z.shape, F32),
        scratch_shapes=[pltpu.VMEM((t + 2 * CONV_PAD, CONV_CT), F32)],
        compiler_params=_cparams(("parallel", "parallel")), name=name)(z, w, bias)


def _dwconv_wgrad(z, dzc, bl, t):
    def kern(z_ref, d_ref, dw_ref, db_ref, zp):
        b = pl.program_id(1)

        @pl.when(b == 0)
        def _():
            dw_ref[...] = jnp.zeros(dw_ref.shape, F32)
            db_ref[...] = jnp.zeros(db_ref.shape, F32)
        _conv_fill(zp, z_ref, t)
        for rot in range(8):
            taps = [j for j in range(CONV_W) if (j + 1) % 8 == rot]
            accs = [jnp.zeros((8, CONV_CT), F32) for _ in taps]
            for r in range(0, t, CONV_RC):
                d = d_ref[r:r + CONV_RC, :]
                win = zp[r + rot:r + rot + CONV_RC + CONV_HALO, :]
                for k, j in enumerate(taps):
                    a = j + 1 - rot
                    prod = d * win[a:a + CONV_RC, :]
                    accs[k] = accs[k] + jnp.sum(prod.reshape(CONV_RC // 8, 8, CONV_CT), axis=0)
            for k, j in enumerate(taps):
                dw_ref[j:j + 1, :] += jnp.sum(accs[k], axis=0, keepdims=True)
        db_ref[...] += jnp.sum(d_ref[...], axis=0, keepdims=True)

    return pl.pallas_call(
        kern, grid=(1024 // CONV_CT, bl),
        in_specs=[pl.BlockSpec((t, CONV_CT), lambda c, b: (b, c)),
                  pl.BlockSpec((t, CONV_CT), lambda c, b: (b, c))],
        out_specs=[pl.BlockSpec((32, CONV_CT), lambda c, b: (0, c)),
                   pl.BlockSpec((1, CONV_CT), lambda c, b: (0, c))],
        out_shape=[jax.ShapeDtypeStruct((32, 1024), F32), jax.ShapeDtypeStruct((1, 1024), F32)],
        scratch_shapes=[pltpu.VMEM((t + 2 * CONV_PAD, CONV_CT), F32)],
        compiler_params=_cparams(("parallel", "arbitrary")), name="dwconv_wgrad")(z, dzc)


def _ffn_fwd(tag, xin, n_tiles, g, sh, sc, gate, wts, gu_name, down_name, tpb, nb, comm=None):
    rows = n_tiles * TM
    rm = functools.partial(_rowmap, tpb=tpb, nb=nb)
    u, ut = _with_side(
        comm, tag + "_norm", None,
        lambda s: rm(tag + "_norm", lambda i, x, g_, sh_, sc_: _twice(_modnorm(x, g_, sh_, sc_)), n_tiles,
                     [tok(xin), const(g), mod(sh), mod(sc)], [("tok", rows, D, BF16), ("tokT", rows, D, BF16)], side=s))
    w_gu, w_down = wts[gu_name], wts[down_name]
    ab, hm, hmt = _with_side(comm, tag + "_gu", None, lambda s: _swiglu_fwd(tag + "_gu", u, w_gu, side=s))
    res = dict(x=xin, gate=gate, scale=0.5, tpb=tpb)
    f, xout = _with_side(comm, tag + "_down", None,
                         lambda s: _mm(tag + "_down", hm, w_down, out_dtype=BF16, side=s, residual=res))
    return xout, (ut, ab, hmt, f)


def _ffn_bwd(tag, xin, saved, dxout, dx_clamp, n_tiles, g, sh, sc, gate, w_gu, w_down, tpb, nb, comm=None, grads=None,
             names=None, keep_tiles=None):
    ut, ab, hmt, f = saved
    rows = n_tiles * TM
    rm = functools.partial(_rowmap, tpb=tpb, nb=nb)

    def mask(i):
        return 1.0 if dx_clamp is None else (i <= dx_clamp).astype(F32)

    def b1(i, dx, f_, gt):
        dx = dx * mask(i)
        return (0.5 * gt * dx, jnp.sum(0.5 * f_ * dx, axis=0, keepdims=True))
    df, dgate = rm(tag + "_bres", b1, n_tiles, [tok(dxout, clamp=dx_clamp), tok(f), mod(gate)],
                   [("tok", rows, D, BF16), ("modacc", D)])
    grads[names[1]] = _with_side(comm, tag + "_wdown", grads, lambda s: _wgrad(tag + "_wdown", hmt, df, side=s))
    dab = _swiglu_bwd(tag + "_bdown", df, w_down, ab)
    grads[names[0]] = _with_side(comm, tag + "_wgu", grads,
                                 lambda s: _wgrad(tag + "_wgu", ut, dab, col_shards=True, side=s))
    du = _with_side(comm, tag + "_bgu", grads,
                    lambda s: _mm(tag + "_bgu", dab, w_gu, trans_b=True, out_dtype=BF16, side=s))

    def b3(i, x, g_, sh_, sc_, du_, dx):
        _, vjp = jax.vjp(_modnorm, x, g_, sh_, sc_)
        dxn, dg, dsh, dsc = vjp(du_)
        return (dx * mask(i) + dxn, dg, dsh, dsc)
    dx_out = ("tok", rows, D, F32) if keep_tiles is None else ("tok_head", keep_tiles * TM, D, F32, keep_tiles - 1)
    dxin, dg, dsh, dsc = rm(tag + "_bnorm", b3, n_tiles,
                            [tok(xin), const(g), mod(sh), mod(sc), tok(du), tok(dxout, clamp=dx_clamp)],
                            [dx_out, ("acc", 1, D), ("modacc", D), ("modacc", D)])
    return dxin, dict(g=dg, sh=dsh, sc=dsc, gate=dgate)


IN_SEGMENTS = ((0, 4096), (5152, 7200), (4096, 5120), (5120, 5152))


def _perm_in_cols(shards):
    width = shards.shape[2]
    parts = []
    for a, b in IN_SEGMENTS:
        for j in range(4):
            lo, hi = max(a, width * j), min(b, width * (j + 1))
            if lo < hi:
                parts.append(shards[j][:, lo - width * j:hi - width * j])
    parts.append(jnp.zeros((shards.shape[1], D_INP - D_IN), shards.dtype))
    return jnp.concatenate(parts, axis=1)


def _unperm_in_cols(w):
    width = D_IN // 4
    shards = []
    for j in range(4):
        parts, start = [], 0
        for a, b in IN_SEGMENTS:
            lo, hi = max(a, width * j), min(b, width * (j + 1))
            if lo < hi:
                parts.append((lo, w[:, start + lo - a:start + hi - a]))
            start += b - a
        shards.append(jnp.concatenate([p for _, p in sorted(parts, key=lambda q: q[0])], axis=1))
    return jnp.stack(shards)


def _local_step(x, c, ctx, target, wts, comm):
    bl, t, _ = x.shape
    tc = ctx.shape[1]
    nx_rows, nc_rows = bl * t, bl * tc
    nt_rows = nx_rows + nc_rows
    tpb = t // TM
    nxt, ntt = nx_rows // TM, nt_rows // TM
    nb = bl
    rm = functools.partial(_rowmap, tpb=tpb, nb=nb)
    last_x = nxt - 1

    x0 = jnp.concatenate([x.reshape(nx_rows, D), ctx.reshape(nc_rows, D)], axis=0)
    tgt = target.reshape(nx_rows, D)

    cc = jnp.concatenate([c, wts["c_ctx"].reshape(1, D), jnp.zeros((8 - bl - 1, D), F32)], axis=0)
    modv, cc_all = _mod_forward(cc, wts["w_mod_shard"], wts["b_mod_shard"])
    mods = [modv[:nb + 1, k * D:(k + 1) * D].reshape(nb + 1, 1, D) for k in range(9)]

    x1, sv1 = _ffn_fwd("ffn1", x0, ntt, wts["g_ffn1"], mods[0], mods[1], mods[2], wts, "w1_gu", "w1_down",
                       tpb, nb, comm)
    u2, u2t = rm("in_norm", lambda i, x_, g_, sh_, sc_: _twice(_modnorm(x_, g_, sh_, sc_)), ntt,
                 [tok(x1), const(wts["g_mix"]), mod(mods[3]), mod(mods[4])],
                 [("tok", nt_rows, D, BF16), ("tokT", nt_rows, D, BF16)])
    w_inp = wts["w_in_p"]
    p = _with_side(comm, "in_proj", None, lambda s: _mm("in_proj", u2, w_inp, out_dtype=BF16, side=s))

    waf, wab, baf, bab = wts["w_alpha_f_pad"], wts["w_alpha_b_pad"], wts["b_alpha_f"], wts["b_alpha_b"]

    def dec_fwd(i, lr, wf, wb, bf_, bb_):
        zf = _dot(lr, wf, 1, 0) + bf_
        zb = _dot(lr, wb, 1, 0) + bb_
        return (jnp.concatenate([jax.nn.log_sigmoid(zf) / TAU, jax.nn.log_sigmoid(zb) / TAU], axis=1),)
    (gfb,) = rm("decay_fwd", dec_fwd, ntt, [tok(p, 128, LR_COL // 128), const(waf), const(wab), const(baf), const(bab)],
                [("tok", nt_rows, 1024, F32)])
    g2 = gfb
    tri = jnp.tril(jnp.ones((CHUNK, CHUNK), F32))
    mmats = jnp.stack([tri, tri.T])
    *o2, ssave = _gla_fwd(p, g2, mmats, bl, t, tc)

    gn_g = wts["gla_norm_g"]

    def gla_out(of, ob, og, gn):
        o = of + ob
        parts = []
        for h in range(HEADS):
            oh = o[:, h * DV:(h + 1) * DV]
            parts.append(oh * lax.rsqrt(jnp.mean(oh * oh, axis=-1, keepdims=True) + EPS))
        return jnp.concatenate(parts, axis=1) * gn * _silu(og)
    yg_in, yg_int = rm("gla_out", lambda i, of, ob, og, gn: _twice(gla_out(of, ob, og, gn)), nxt,
                       [tok(o2[0]), tok(o2[1]), tok(p, 1024, OG_CB), const(gn_g)],
                       [("tok", nx_rows, D, BF16), ("tokT", nx_rows, D, BF16)])
    y_gla = _with_side(comm, "gla_proj", None,
                       lambda s: _mm("gla_proj", yg_in, wts["w_gla_out"], out_dtype=BF16, side=s))

    (z,) = rm("glu", lambda i, a, b: (a * jax.nn.sigmoid(b),), nxt, [tok(p, 1024, 0), tok(p, 1024, 1)],
              [("tok", nx_rows, D, F32)])
    dw_w = jnp.concatenate([wts["dw_weight"], jnp.zeros((1, D), F32)], axis=0)
    zc = _dwconv("dwconv_fwd", z, dw_w, wts["dw_bias"], bl, t, False)

    def ln_silu(zc_, g_, b_):
        mu = jnp.mean(zc_, axis=-1, keepdims=True)
        var = jnp.mean(jnp.square(zc_ - mu), axis=-1, keepdims=True)
        return _silu((zc_ - mu) * lax.rsqrt(var + EPS) * g_ + b_)
    ln_g, ln_b = wts["conv_ln_g"], wts["conv_ln_b"]
    zl, zlt = rm("conv_ln", lambda i, zc_, g_, b_: _twice(ln_silu(zc_, g_, b_)), nxt,
                 [tok(zc), const(ln_g), const(ln_b)], [("tok", nx_rows, D, BF16), ("tokT", nx_rows, D, BF16)])
    y_conv = _mm("conv_proj", zl, wts["w_conv_out"], out_dtype=BF16)

    mg, mgt = rm("merge", lambda i, ga, gb, yc, yg: _twice(jax.nn.sigmoid(ga) * yc + jax.nn.sigmoid(gb) * yg), nxt,
                 [tok(p, 1024, GA_CB), tok(p, 1024, GB_CB), tok(y_conv), tok(y_gla)],
                 [("tok", nx_rows, D, BF16), ("tokT", nx_rows, D, BF16)])
    mix, x2 = _mm("out_proj", mg, wts["w_out"], out_dtype=BF16,
                  residual=dict(x=x1, gate=mods[5], scale=1.0, tpb=tpb))

    x3, sv2 = _ffn_fwd("ffn2", x2, nxt, wts["g_ffn2"], mods[6], mods[7], mods[8], wts, "w2_gu", "w2_down",
                       tpb, nb)
    g_fin = wts["g_final"].reshape(1, D)

    def head(i, x_, g_, tg):
        y, vjp = jax.vjp(_rms, x_, g_)
        diff = y - tg
        dx, dg = vjp(diff * (1.0 / D))
        loss = 0.5 * jnp.sum(jnp.mean(diff * diff, axis=-1, keepdims=True))
        return dx, dg, loss
    dx3, dg_final, loss_acc = rm("loss_head", head, nxt, [tok(x3), const(g_fin), tok(tgt)],
                                 [("tok", nx_rows, D, F32), ("acc", 1, D), ("acc", 8, 128)])
    loss = loss_acc[0, 0]

    grads = {}
    dx2, gf2 = _ffn_bwd("ffn2", x2, sv2, dx3, None, nxt, wts["g_ffn2"], mods[6], mods[7], mods[8],
                        wts["w2_gu"], wts["w2_down"], tpb, nb, comm, grads, ("w2_gu", "w2_down"))
    grads["g_ffn2"] = gf2["g"]

    dmix, dgate5 = rm("mix_bres", lambda i, dx, mx_, gt: (gt * dx, jnp.sum(mx_ * dx, axis=0, keepdims=True)), nxt,
                      [tok(dx2), tok(mix), mod(mods[5])], [("tok", nx_rows, D, BF16), ("modacc", D)])
    dmg = _mm("out_bproj", dmix, wts["w_out"], trans_b=True, out_dtype=BF16)
    grads["w_out"] = _wgrad("out_wgrad", mgt, dmix)

    def merge_bwd(i, dm, ga, gb, yc, yg):
        keep = (i <= last_x).astype(F32)
        dm = dm * keep
        sa, sb = jax.nn.sigmoid(ga), jax.nn.sigmoid(gb)
        return dm * sa, dm * sb, jnp.concatenate([dm * yc * sa * (1 - sa), dm * yg * sb * (1 - sb)], axis=1)
    cl = dict(clamp=last_x)
    dyc, dyg, dp = rm("merge_bwd", merge_bwd, ntt,
                      [tok(dmg, **cl), tok(p, 1024, GA_CB, last_x), tok(p, 1024, GB_CB, last_x), tok(y_conv, **cl),
                       tok(y_gla, **cl)],
                      [("tok", nt_rows, D, BF16), ("tok", nt_rows, D, BF16),
                       ("cols", jax.ShapeDtypeStruct((nt_rows, D_INP), BF16), 2048, 2)])

    dzl = _with_side(comm, "conv_bproj", grads, lambda s: _mm("conv_bproj", dyc, wts["w_conv_out"], trans_b=True,
                                                              rows=nx_rows, out_dtype=BF16, side=s))
    grads["w_conv_out"] = _wgrad("conv_wgrad", zlt, dyc, rows=nx_rows)

    def ln_bwd(i, zc_, g_, b_, dz_):
        _, vjp = jax.vjp(ln_silu, zc_, g_, b_)
        return vjp(dz_)
    dzc, dln_g, dln_b = rm("conv_ln_bwd", ln_bwd, nxt, [tok(zc), const(ln_g), const(ln_b), tok(dzl)],
                           [("tok", nx_rows, D, F32), ("acc", 1, D), ("acc", 1, D)])
    dz = _dwconv("dwconv_bwd", dzc, dw_w, jnp.zeros((1, D), F32), bl, t, True)
    ddw, ddb = _dwconv_wgrad(z, dzc, bl, t)
    grads.update(conv_ln_g=dln_g, conv_ln_b=dln_b, dw_weight=ddw[:CONV_W], dw_bias=ddb)

    def glu_bwd(i, dz_, a, b):
        keep = (i <= last_x).astype(F32)
        dz_ = dz_ * keep
        s = jax.nn.sigmoid(b)
        return (jnp.concatenate([dz_ * s, dz_ * a * s * (1 - s)], axis=1),)
    (dp,) = rm("glu_bwd", glu_bwd, ntt, [tok(dz, **cl), tok(p, 1024, 0, last_x), tok(p, 1024, 1, last_x)],
               [("cols", dp, 2048, 0)])

    dyg_in = _mm("gla_bproj", dyg, wts["w_gla_out"], trans_b=True, rows=nx_rows, out_dtype=BF16)
    grads["w_gla_out"] = _wgrad("gla_wgrad", yg_int, dyg, rows=nx_rows)

    def gla_out_bwd(i, of, ob, og, gn, dy):
        _, vjp = jax.vjp(gla_out, of, ob, og, gn)
        do_, _, dog_, dgn_ = vjp(dy)
        return do_, dog_, dgn_
    do, dp, dgn = rm("gla_out_bwd", gla_out_bwd, nxt,
                     [tok(o2[0]), tok(o2[1]), tok(p, 1024, OG_CB), const(gn_g), tok(dyg_in)],
                     [("tok", nx_rows, D, BF16), ("cols", dp, 1024, OG_CB), ("acc", 1, D)])
    grads["gla_norm_g"] = dgn
    (dp,) = rm("og_ctx_zero", lambda i: (jnp.zeros((TM, D), F32),), ntt - nxt, [], [("cols", dp, 1024, OG_CB, nxt)])

    dq2, dk2, dv2, dg2 = zip(*_gla_bwd(p, g2, mmats, ssave, do, bl, t, tc))

    def dec_bwd(i, lr, wf, wb, bf_, bb_, dgf, dgb_):
        zf = _dot(lr, wf, 1, 0) + bf_
        zb = _dot(lr, wb, 1, 0) + bb_
        dzf = dgf * (1 - jax.nn.sigmoid(zf)) * (1.0 / TAU)
        dzb = dgb_ * (1 - jax.nn.sigmoid(zb)) * (1.0 / TAU)
        dlr = _dot(dzf, wf, 1, 1) + _dot(dzb, wb, 1, 1)
        return (dlr, _dot(lr, dzf, 0, 0), _dot(lr, dzb, 0, 0), jnp.sum(dzf, axis=0, keepdims=True),
                jnp.sum(dzb, axis=0, keepdims=True))
    dp, dwaf, dwab, dbaf, dbab = rm(
        "decay_bwd", dec_bwd, ntt,
        [tok(p, 128, LR_COL // 128), const(waf), const(wab), const(baf), const(bab), tok(dg2[0]), tok(dg2[1])],
        [("cols", dp, 128, LR_COL // 128), ("acc", 128, 512), ("acc", 128, 512), ("acc", 1, 512), ("acc", 1, 512)])
    grads.update(w_alpha_f=dwaf[:LOWRANK], w_alpha_b=dwab[LOWRANK:2 * LOWRANK], b_alpha_f=dbaf, b_alpha_b=dbab)

    (dp,) = rm("gla_sum",
               lambda i, q0, q1, k0, k1, v0, v1: (jnp.concatenate([q0 + q1, k0 + k1, v0 + v1], axis=1),), ntt,
               [tok(dq2[0]), tok(dq2[1]), tok(dk2[0]), tok(dk2[1]), tok(dv2[0]), tok(dv2[1])],
               [("cols", dp, 2048, 1)])
    du2 = _with_side(comm, "in_bproj", grads,
                     lambda s: _mm("in_bproj", dp, w_inp, trans_b=True, out_dtype=BF16, side=s))
    grads["w_in_p"] = _wgrad("in_wgrad", u2t, dp)

    def in_norm_bwd(i, x_, g_, sh_, sc_, du_, dx):
        keep = (i <= last_x).astype(F32)
        _, vjp = jax.vjp(_modnorm, x_, g_, sh_, sc_)
        dxn, dg, dsh, dsc = vjp(du_)
        return (dx * keep + dxn, dg, dsh, dsc)
    dx1, dg_mix, dsh3, dsc4 = rm("in_norm_bwd", in_norm_bwd, ntt,
                                 [tok(x1), const(wts["g_mix"]), mod(mods[3]), mod(mods[4]), tok(du2), tok(dx2, **cl)],
                                 [("tok", nt_rows, D, F32), ("acc", 1, D), ("modacc", D), ("modacc", D)])
    grads["g_mix"] = dg_mix

    dx0, gf1 = _ffn_bwd("ffn1", x0, sv1, dx1, None, ntt, wts["g_ffn1"], mods[0], mods[1], mods[2],
                        wts["w1_gu"], wts["w1_down"], tpb, nb, comm, grads, ("w1_gu", "w1_down"), keep_tiles=nxt)
    grads["g_ffn1"] = gf1["g"]
    grad_x = dx0.reshape(bl, t, D)

    dmods = [gf1["sh"], gf1["sc"], gf1["gate"], dsh3, dsc4, dgate5, gf2["sh"], gf2["sc"], gf2["gate"]]
    dmod = jnp.concatenate(
        [jnp.concatenate([a.reshape(a.shape[0], D), jnp.zeros((8 - a.shape[0], D), F32)], axis=0) for a in dmods],
        axis=1)
    grads["w_mod"], grads["c_ctx"], grads["b_mod"] = _mod_backward(dmod, cc_all, wts["w_mod_shard"], nb)
    grads["g_final"] = dg_final.reshape(D)
    return loss, grad_x, grads


ANY = pl.BlockSpec(memory_space=pl.ANY)


def _place():
    x, y, c = lax.axis_index("x"), lax.axis_index("y"), lax.axis_index("c")
    chips = [(1 - x, y), (x, 1 - y), (1 - x, 1 - y)]
    return x, y, c, chips


def _remote(send_sems, recv_sems):
    def copy(k, src, dst, to):
        return pltpu.make_async_remote_copy(src_ref=src, dst_ref=dst, send_sem=send_sems.at[k],
                                            recv_sem=recv_sems.at[k], device_id=to, device_id_type=MESH)
    return copy


def _sems(n):
    return [pltpu.SemaphoreType.DMA((n,)), pltpu.SemaphoreType.DMA((n,))]


def _swap_halves(name, gs):
    n = len(gs)

    def body(*refs):
        ins, outs = refs[:n], refs[n:2 * n]
        copy = _remote(refs[2 * n], refs[2 * n + 1])
        x, y, c, _ = _place()
        cps = []
        for i in range(n):
            hr = ins[i].shape[1] // 2
            cps.append(copy(i, ins[i].at[:, pl.ds((1 - c) * hr, hr), :], outs[i], (x, y, 1 - c)))
            cps[-1].start()
        for cp in cps:
            cp.wait()

    return pl.pallas_call(
        body, out_shape=[jax.ShapeDtypeStruct((4, g.shape[1] // 2, g.shape[2]), g.dtype) for g in gs],
        in_specs=[ANY] * n, out_specs=[ANY] * n, scratch_shapes=_sems(n), name=name)(*gs)


def _row_tile(hr):
    return hr if hr <= 256 else _pick(hr, (256, 176, 128, 64, 32, 16))


def _add_halves(name, g, r, place):
    hr = r.shape[1]
    tr = _row_tile(hr)
    nblk = hr // tr

    def kern(p_ref, g_ref, r_ref, o_ref):
        o_ref[...] = (g_ref[...].astype(F32) + r_ref[...].astype(F32)).astype(o_ref.dtype)

    blk = (1, tr, g.shape[2])
    return pl.pallas_call(
        kern,
        grid_spec=pltpu.PrefetchScalarGridSpec(
            num_scalar_prefetch=1, grid=(4, nblk),
            in_specs=[pl.BlockSpec(blk, lambda j, i, p: (j, p[0] * nblk + i, 0)),
                      pl.BlockSpec(blk, lambda j, i, p: (j, i, 0))],
            out_specs=pl.BlockSpec(blk, lambda j, i, p: (j, i, 0))),
        out_shape=jax.ShapeDtypeStruct(r.shape, r.dtype),
        compiler_params=_cparams(("parallel", "parallel")), name=name)(place, g, r)


def _scatter_chips(cs):
    n = len(cs)

    def body(*refs):
        ins, outs = refs[:n], refs[n:2 * n]
        copy = _remote(refs[2 * n], refs[2 * n + 1])
        x, y, c, chips = _place()
        me = 2 * x + y
        sends = []
        for i in range(n):
            for j, (px, py) in enumerate(chips):
                sends.append(copy(3 * i + j, ins[i].at[2 * px + py], outs[i].at[me], (px, py, c)))
                sends[-1].start()
        for i in range(n):
            for j, (px, py) in enumerate(chips):
                src = 2 * px + py
                copy(3 * i + j, ins[i].at[src], outs[i].at[src], (px, py, c)).wait_recv()
        for cp in sends:
            cp.wait_send()

    return pl.pallas_call(
        body, out_shape=[jax.ShapeDtypeStruct(a.shape, a.dtype) for a in cs], in_specs=[ANY] * n,
        out_specs=[ANY] * n, scratch_shapes=_sems(3 * n), name="grad_scatter_chips")(*cs)


def _sum_chips(name, cs, r, place):
    hr = r.shape[1]
    tr = _row_tile(hr)
    nblk = hr // tr

    def kern(p_ref, c_ref, r0, r1, r2, r3, o_ref):
        me = p_ref[1]
        acc = None
        for k, rk in enumerate((r0, r1, r2, r3)):
            val = jnp.where(me == k, c_ref[0].astype(F32), rk[0].astype(F32))
            acc = val if acc is None else acc + val
        o_ref[...] = acc

    blk = (1, tr, r.shape[2])

    def slot(k):
        return lambda i, p: (jnp.where(p[1] == k, (k + 1) % 4, k), i, 0)

    return pl.pallas_call(
        kern,
        grid_spec=pltpu.PrefetchScalarGridSpec(
            num_scalar_prefetch=1, grid=(nblk,),
            in_specs=[pl.BlockSpec(blk, lambda i, p: (p[1], i, 0))] + [pl.BlockSpec(blk, slot(k)) for k in range(4)],
            out_specs=pl.BlockSpec((tr, r.shape[2]), lambda i, p: (p[0] * nblk + i, 0))),
        out_shape=jax.ShapeDtypeStruct((2 * hr, r.shape[2]), F32),
        compiler_params=_cparams(("parallel",)), name=name)(place, cs, r, r, r, r)


def _join_halves(fs):
    n = len(fs)

    def body(*refs):
        ins, outs = refs[:n], refs[n:2 * n]
        copy = _remote(refs[2 * n], refs[2 * n + 1])
        x, y, c, _ = _place()
        cps = []
        for i in range(n):
            hr = ins[i].shape[0] // 2
            cps.append(copy(i, ins[i].at[pl.ds(c * hr, hr), :], outs[i].at[pl.ds(c * hr, hr), :], (x, y, 1 - c)))
            cps[-1].start()
        for i in range(n):
            hr = ins[i].shape[0] // 2
            other = outs[i].at[pl.ds((1 - c) * hr, hr), :]
            copy(i, other, other, (x, y, 1 - c)).wait_recv()
        for cp in cps:
            cp.wait_send()

    return pl.pallas_call(
        body, out_shape=[jax.ShapeDtypeStruct(f.shape, f.dtype) for f in fs], in_specs=[ANY] * n,
        out_specs=[ANY] * n, input_output_aliases={i: i for i in range(n)}, scratch_shapes=_sems(n),
        name="grad_join_halves")(*fs)


def _half_rows(n, hc):
    return pl.ds(hc * (n // 2), n // 2)


def _gather_ici_side(shards):
    n = len(shards)

    def copies(ins, outs, copy):
        x, y, c, chips = _place()
        me = 2 * x + y
        for i in range(n):
            rows = _half_rows(ins[i].shape[0], c)
            for j, (px, py) in enumerate(chips):
                yield (copy(3 * i + j, ins[i].at[rows, :], outs[i].at[me, rows, :], (px, py, c)),
                       outs[i].at[2 * px + py, rows, :])

    def start(ins, outs, copy):
        for cp, _ in copies(ins, outs, copy):
            cp.start()

    def finish(ins, outs, copy):
        x, y, c, chips = _place()
        k = 0
        for cp, landing in copies(ins, outs, copy):
            copy(k, landing, landing, (x, y, c)).wait_recv()
            k += 1
        for cp, _ in copies(ins, outs, copy):
            cp.wait_send()

    return dict(ins=list(shards), outs=[jax.ShapeDtypeStruct((4,) + s.shape, s.dtype) for s in shards], nsem=3 * n,
                start=start, finish=finish)


def _gather_d2d_side(shards, bufs):
    n = len(shards)

    def copies(ins, outs, copy):
        x, y, c, chips = _place()
        me = 2 * x + y
        sibling = (x, y, 1 - c)
        for i in range(n):
            a = ins[i].shape[0]
            yield copy(4 * i + 3, ins[i], outs[i].at[me], sibling), outs[i].at[me]
            for j, (px, py) in enumerate(chips):
                src = 2 * px + py
                mine = outs[i].at[src, _half_rows(a, c), :]
                yield copy(4 * i + j, mine, mine, sibling), outs[i].at[src, _half_rows(a, 1 - c), :]

    def start(ins, outs, copy):
        for cp, _ in copies(ins, outs, copy):
            cp.start()

    def finish(ins, outs, copy):
        x, y, c, _ = _place()
        for i in range(n):
            for k, (cp, landing) in enumerate(list(copies(ins, outs, copy))[4 * i:4 * i + 4]):
                sem = 4 * i + 3 if k == 0 else 4 * i + k - 1
                copy(sem, landing, landing, (x, y, 1 - c)).wait_recv()
        for cp, _ in copies(ins, outs, copy):
            cp.wait_send()

    return dict(ins=list(shards) + list(bufs), outs=[jax.ShapeDtypeStruct(b.shape, b.dtype) for b in bufs],
                nsem=4 * n, alias={n + i: i for i in range(n)}, start=start, finish=finish)


def _gather_d2d(shards, bufs):
    side = _gather_d2d_side(shards, bufs)
    n_in = len(side["ins"])

    def body(*refs):
        ins, outs = refs[:n_in], refs[n_in:n_in + len(bufs)]
        copy = _remote(*refs[n_in + len(bufs):])
        side["start"](ins, outs, copy)
        side["finish"](ins, outs, copy)

    return pl.pallas_call(
        body, out_shape=side["outs"], in_specs=[ANY] * n_in, out_specs=[ANY] * len(bufs),
        input_output_aliases=side["alias"], scratch_shapes=_sems(side["nsem"]), name="gather_weights_d2d")(*side["ins"])


def _scatter_side(cs):
    n = len(cs)

    def copies(ins, outs, copy):
        x, y, c, chips = _place()
        me = 2 * x + y
        for i in range(n):
            for j, (px, py) in enumerate(chips):
                yield copy(3 * i + j, ins[i].at[2 * px + py], outs[i].at[me], (px, py, c)), outs[i].at[2 * px + py]

    def start(ins, outs, copy):
        for cp, _ in copies(ins, outs, copy):
            cp.start()

    def finish(ins, outs, copy):
        x, y, c, _ = _place()
        for k, (cp, landing) in enumerate(copies(ins, outs, copy)):
            copy(k, landing, landing, (x, y, c)).wait_recv()
        for cp, _ in copies(ins, outs, copy):
            cp.wait_send()

    return dict(ins=list(cs), outs=[jax.ShapeDtypeStruct(a.shape, a.dtype) for a in cs], nsem=3 * n,
                start=start, finish=finish)


def _mod_forward(cc, w_shard, b_shard):
    cc_all = _allreduce_small(cc, "cond_gather", reduce=False).reshape(64, D)
    part = _mm("mod_fwd", cc_all, w_shard, a_fn=_silu, bias=b_shard)
    got = _mod_rows_exchange(part)
    return jnp.concatenate([got[j] for j in range(4)], axis=1), cc_all


def _mod_backward(dmod, cc_all, w_shard, ctx_row):
    w = w_shard.shape[1]
    blocks = _dmod_exchange(jnp.transpose(dmod.reshape(8, 4, w), (1, 0, 2))).reshape(64, w)
    dsc = _mm("mod_bproj", blocks, w_shard, trans_b=True)

    def tail(cc_ref, dsc_ref, dm_ref, s_ref, dctx_ref, db_ref):
        cc_ = cc_ref[...]
        s = jax.nn.sigmoid(cc_)
        s_ref[...] = (cc_ * s).astype(BF16)
        is_ctx = ((lax.broadcasted_iota(jnp.int32, (64, 1), 0) & 7) == ctx_row).astype(F32)
        dctx_ref[...] = 0.5 * jnp.sum(dsc_ref[...] * (s * (1 + cc_ * (1 - s))) * is_ctx, axis=0, keepdims=True)
        db_ref[...] = jnp.sum(dm_ref[...], axis=0, keepdims=True)

    s_all, dctx, db_mod = pl.pallas_call(
        tail, out_shape=[jax.ShapeDtypeStruct((64, D), BF16), jax.ShapeDtypeStruct((1, D), F32),
                         jax.ShapeDtypeStruct((1, dmod.shape[1]), F32)], name="mod_tail")(cc_all, dsc, dmod)
    return _mm_tn("mod_wgrad", s_all, blocks), dctx[0], db_mod


def _small_pack(dw, af, ab):
    return jnp.concatenate([dw, jnp.zeros((1, dw.shape[1]), F32), jnp.concatenate([af, ab], axis=1)], axis=0)


def _grad_pieces(n, grads):
    if n == "small":
        return jnp.stack([_small_pack(grads["dw_weight"][:, 256 * j:256 * (j + 1)],
                                      grads["w_alpha_f"][:, DK * j:DK * (j + 1)],
                                      grads["w_alpha_b"][:, DK * j:DK * (j + 1)]) for j in range(4)])
    if n == "w_in":
        return _unperm_in_cols(grads["w_in_p"])
    g = grads[n]
    return g if g.ndim == 3 else g.reshape(4, g.shape[0] // 4, g.shape[1])


def _swap_side(gs):
    n = len(gs)

    def copies(ins, outs, copy):
        x, y, c, _ = _place()
        for i in range(n):
            yield copy(i, ins[i].at[:, _half_rows(ins[i].shape[1], 1 - c), :], outs[i], (x, y, 1 - c))

    def start(ins, outs, copy):
        for cp in copies(ins, outs, copy):
            cp.start()

    def finish(ins, outs, copy):
        for cp in copies(ins, outs, copy):
            cp.wait()

    return dict(ins=list(gs), outs=[jax.ShapeDtypeStruct((4, g.shape[1] // 2, g.shape[2]), g.dtype) for g in gs],
                nsem=n, start=start, finish=finish)


def _chip_sums(tag, names, grads, place, swapped=None):
    gs = [_grad_pieces(n, grads) for n in names] if swapped is None else swapped[0]
    got = _swap_halves("grad_swap_" + tag, gs) if swapped is None else swapped[1]
    return [_add_halves("grad_add_" + n, g, r, place) for n, g, r in zip(names, gs, got)]


class _Overlap:
    SCATTER = {"in_bproj": ("w2_down", "w2_gu", "w_out"),
               "ffn1_wgu": ("w_conv_out", "w_gla_out", "w_in", "small"),
               "ffn1_bgu": ("w1_down", "w1_gu")}
    GATHER = {("ffn1_norm", None): ("w1_gu", "w1_down"),
              ("ffn1_gu", "ffn1_down"): ("w_in", "w_conv_out", "w_gla_out", "w_out", "small"),
              ("in_proj", "gla_proj"): ("w2_gu", "w2_down")}
    LATE = tuple(n for names in GATHER.values() for n in names)
    SWAP = {"conv_bproj": "in_bproj", "ffn1_wdown": "ffn1_wgu"}

    def __init__(self, shard_of, install, place):
        self.shard_of, self.install, self.place = shard_of, install, place
        self.bufs, self.pending, self.landed, self.swapped = {}, None, {}, {}

    def side(self, tag, grads):
        for (ici, d2d), names in self.GATHER.items():
            shards = [self.shard_of(n) for n in names]
            if tag == ici:
                return _gather_ici_side(shards)
            if tag == d2d:
                return _gather_d2d_side(shards, self.bufs[ici])
        if tag in self.SWAP:
            gs = [_grad_pieces(n, grads) for n in self.SCATTER[self.SWAP[tag]]]
            self.swapped[self.SWAP[tag]] = [gs, None]
            return _swap_side(gs)
        if tag in self.SCATTER:
            names = self.SCATTER[tag]
            self.pending = (names, _chip_sums(tag, names, grads, self.place, self.swapped.get(tag)))
            return _scatter_side(self.pending[1])
        return None

    def done(self, tag, outs):
        for (ici, d2d), names in self.GATHER.items():
            if tag == ici and d2d is None:
                self.install(dict(zip(names, _gather_d2d([self.shard_of(n) for n in names], outs))))
                return
            if tag == ici:
                self.bufs[ici] = outs
                return
            if tag == d2d:
                self.install(dict(zip(names, outs)))
                return
        if tag in self.SWAP:
            self.swapped[self.SWAP[tag]][1] = outs
            return
        for n, cs, r in zip(*self.pending, outs):
            self.landed[n] = (cs, r)


def _with_side(comm, tag, grads, call):
    side = comm.side(tag, grads) if comm is not None else None
    if side is None:
        return call(None)
    res, outs = call(side)
    comm.done(tag, outs)
    return res


def _allreduce_small(v, name="allreduce_small", reduce=True):
    def body(x_ref, out_ref, *scratch):
        gath = out_ref if not reduce else scratch[0]
        send_sems, recv_sems, local_sem = scratch[-3:]
        x, y, c, chips = _place()
        me, sibling = (x, y, c), (x, y, 1 - c)

        def slot(px, py, pc):
            return gath.at[4 * px + 2 * py + pc]

        def copy(k, block, to, src=None):
            return pltpu.make_async_remote_copy(
                src_ref=slot(*block) if src is None else src, dst_ref=slot(*block), send_sem=send_sems.at[k],
                recv_sem=recv_sems.at[k], device_id=to, device_id_type=MESH)

        mine = pltpu.make_async_copy(x_ref, slot(*me), local_sem)
        mine.start()
        first = [copy(0, me, sibling, src=x_ref)]
        first += [copy(1 + j, me, (*chip, c), src=x_ref) for j, chip in enumerate(chips)]
        for cp in first:
            cp.start()
        passed = [copy(4 + j, (*chip, c), sibling) for j, chip in enumerate(chips)]
        for j, chip in enumerate(chips):
            copy(1 + j, (*chip, c), me).wait_recv()
            passed[j].start()
        copy(0, sibling, me).wait_recv()
        for j, chip in enumerate(chips):
            copy(4 + j, (*chip, 1 - c), me).wait_recv()
        for cp in first + passed:
            cp.wait_send()
        mine.wait()
        if reduce:
            acc = gath[0]
            for k in range(1, 8):
                acc = acc + gath[k]
            out_ref[...] = acc

    vm = pl.BlockSpec(memory_space=pltpu.VMEM)
    sems = [pltpu.SemaphoreType.DMA((7,)), pltpu.SemaphoreType.DMA((7,)), pltpu.SemaphoreType.DMA(())]
    return pl.pallas_call(
        body, out_shape=jax.ShapeDtypeStruct(v.shape if reduce else (8,) + v.shape, F32), in_specs=[vm], out_specs=vm,
        scratch_shapes=([pltpu.VMEM((8,) + v.shape, F32)] if reduce else []) + sems, name=name)(v)


def _mod_rows_exchange(part):
    w = part.shape[1]

    def body(p_ref, out_ref, send_sems, recv_sems):
        x, y, c, chips = _place()
        me = 2 * x + y
        copy = _remote(send_sems, recv_sems)
        sends = []
        for j, (px, py) in enumerate(chips):
            rows = pl.ds(pl.multiple_of(8 * (4 * px + 2 * py + c), 8), 8)
            sends.append(copy(j, p_ref.at[rows, :], out_ref.at[me], (px, py, c)))
            sends[-1].start()
        out_ref[me] = p_ref[pl.ds(pl.multiple_of(8 * (4 * x + 2 * y + c), 8), 8), :]
        for j, (px, py) in enumerate(chips):
            landing = out_ref.at[2 * px + py]
            copy(j, landing, landing, (px, py, c)).wait_recv()
        for cp in sends:
            cp.wait_send()

    vm = pl.BlockSpec(memory_space=pltpu.VMEM)
    return pl.pallas_call(body, out_shape=jax.ShapeDtypeStruct((4, 8, w), F32), in_specs=[vm], out_specs=vm,
                          scratch_shapes=_sems(3), name="mod_rows_exchange")(part)


def _dmod_exchange(dm):
    w = dm.shape[2]

    def body(d_ref, out_ref, send_sems, recv_sems):
        x, y, c, _ = _place()
        copy = _remote(send_sems, recv_sems)
        mine = 4 * x + 2 * y + c
        sends = []
        for r in range(1, 8):
            tx, ty, tc = x ^ (r >> 2), y ^ ((r >> 1) & 1), c ^ (r & 1)
            sends.append(copy(r - 1, d_ref.at[2 * tx + ty], out_ref.at[mine], (tx, ty, tc)))
            sends[-1].start()
        out_ref[mine] = d_ref[2 * x + y]
        for r in range(1, 8):
            tx, ty, tc = x ^ (r >> 2), y ^ ((r >> 1) & 1), c ^ (r & 1)
            landing = out_ref.at[4 * tx + 2 * ty + tc]
            copy(r - 1, landing, landing, (tx, ty, tc)).wait_recv()
        for cp in sends:
            cp.wait_send()

    vm = pl.BlockSpec(memory_space=pltpu.VMEM)
    return pl.pallas_call(body, out_shape=jax.ShapeDtypeStruct((8, 8, w), F32), in_specs=[vm], out_specs=vm,
                          scratch_shapes=_sems(7), name="dmod_exchange")(dm)


def _adamw(name, w, g, m, v):
    r, cols = w.shape
    budget = 262144
    tr = r if r * cols <= budget else next(c for c in (256, 128, 64, 32, 16, 8) if r % c == 0 and c * cols <= budget)

    def kern(w_ref, g_ref, m_ref, v_ref, go_ref, d_ref, nm_ref, nv_ref):
        gv = g_ref[...]
        go_ref[...] = gv
        nm = ADAM_B1 * m_ref[...] + (1.0 - ADAM_B1) * gv
        nv = ADAM_B2 * v_ref[...] + (1.0 - ADAM_B2) * jnp.square(gv)
        m_hat = nm / (1.0 - ADAM_B1 ** ADAM_STEP)
        v_hat = nv / (1.0 - ADAM_B2 ** ADAM_STEP)
        d_ref[...] = -ADAM_LR * (m_hat / (jnp.sqrt(v_hat) + ADAM_EPS) + ADAM_WD * w_ref[...])
        nm_ref[...] = nm
        nv_ref[...] = nv

    spec = pl.BlockSpec((tr, cols), lambda i: (i, 0))
    shp = jax.ShapeDtypeStruct((r, cols), F32)
    return pl.pallas_call(kern, grid=(r // tr,), in_specs=[spec] * 4, out_specs=[spec] * 4, out_shape=[shp] * 4,
                          compiler_params=_cparams(("parallel",)), name=name)(w, g, m, v)


SHARDED = (("w_mod", 1), ("w1_gu", 1), ("w1_down", 0), ("w_in", 1), ("dw_weight", 1), ("w_conv_out", 0),
           ("w_alpha_f", 1), ("w_alpha_b", 1), ("w_gla_out", 0), ("w_out", 0), ("w2_gu", 1), ("w2_down", 0))
REPLICATED = ("c_ctx", "b_mod", "g_ffn1", "g_mix", "dw_bias", "conv_ln_g", "conv_ln_b", "b_alpha_f", "b_alpha_b",
              "gla_norm_g", "g_ffn2", "g_final")
WEIGHTS = ("c_ctx", "w_mod", "b_mod", "g_ffn1", "w1_gu", "w1_down", "g_mix", "w_in", "dw_weight", "dw_bias",
           "conv_ln_g", "conv_ln_b", "w_conv_out", "w_alpha_f", "b_alpha_f", "w_alpha_b", "b_alpha_b", "gla_norm_g",
           "w_gla_out", "w_out", "g_ffn2", "w2_gu", "w2_down", "g_final")
MATRICES = ("w1_gu", "w1_down", "w_in", "w_conv_out", "w_gla_out", "w_out", "w2_gu", "w2_down")


def _pack_flat(parts, align):
    flat = jnp.concatenate([p.reshape(-1) for p in parts])
    pad = (-flat.shape[0]) % align
    return jnp.concatenate([flat, jnp.zeros((pad,), flat.dtype)]).reshape(-1, 1024)


def _unpack_flat(flat2d, shapes):
    flat = flat2d.reshape(-1)
    out, off = [], 0
    for s in shapes:
        n = math.prod(s)
        out.append(flat[off:off + n].reshape(s))
        off += n
    return out


def kernel(x, c, ctx, c_ctx, w_mod, b_mod, g_ffn1, w1_gu, w1_down, g_mix, w_in, dw_weight, dw_bias, conv_ln_g, conv_ln_b, w_conv_out, w_alpha_f, b_alpha_f, w_alpha_b, b_alpha_b, gla_norm_g, w_gla_out, w_out, g_ffn2, w2_gu, w2_down, g_final, loss_target, m_c_ctx, m_w_mod, m_b_mod, m_g_ffn1, m_w1_gu, m_w1_down, m_g_mix, m_w_in, m_dw_weight, m_dw_bias, m_conv_ln_g, m_conv_ln_b, m_w_conv_out, m_w_alpha_f, m_b_alpha_f, m_w_alpha_b, m_b_alpha_b, m_gla_norm_g, m_w_gla_out, m_w_out, m_g_ffn2, m_w2_gu, m_w2_down, m_g_final, v_c_ctx, v_w_mod, v_b_mod, v_g_ffn1, v_w1_gu, v_w1_down, v_g_mix, v_w_in, v_dw_weight, v_dw_bias, v_conv_ln_g, v_conv_ln_b, v_w_conv_out, v_w_alpha_f, v_b_alpha_f, v_w_alpha_b, v_b_alpha_b, v_gla_norm_g, v_w_gla_out, v_w_out, v_g_ffn2, v_w2_gu, v_w2_down, v_g_final):
    given = dict(locals())
    w = {n: given[n] for n in WEIGHTS}
    m = {n: given["m_" + n] for n in WEIGHTS}
    v = {n: given["v_" + n] for n in WEIGHTS}

    def shard_of(n):
        if n == "small":
            return _small_pack(w["dw_weight"][0], w["w_alpha_f"][0], w["w_alpha_b"][0])
        return w[n][0].astype(BF16)

    def install(wts, got):
        for n in ("w1_gu", "w2_gu"):
            if n in got:
                wts[n] = got[n]
        for n in ("w1_down", "w2_down", "w_conv_out", "w_gla_out", "w_out"):
            if n in got:
                wts[n] = got[n].reshape(-1, D)
        if "w_in" in got:
            wts["w_in_p"] = _perm_in_cols(got["w_in"])
        if "small" in got:
            sm = got["small"]
            wts["dw_weight"] = jnp.concatenate([sm[j, :CONV_W] for j in range(4)], axis=1)
            zpad = jnp.zeros((128, HEADS * DK), BF16)
            w_af = jnp.concatenate([sm[j, 32:32 + LOWRANK, :DK] for j in range(4)], axis=1)
            w_ab = jnp.concatenate([sm[j, 32:32 + LOWRANK, DK:] for j in range(4)], axis=1)
            wts["w_alpha_f_pad"] = zpad.at[0:LOWRANK].set(w_af.astype(BF16))
            wts["w_alpha_b_pad"] = zpad.at[LOWRANK:2 * LOWRANK].set(w_ab.astype(BF16))

    wts = {n: w[n] for n in REPLICATED}
    chip = 2 * lax.axis_index("x") + lax.axis_index("y")
    mod_cols = w["w_mod"].shape[2]
    wts["w_mod_shard"] = shard_of("w_mod")
    wts["b_mod_shard"] = lax.dynamic_slice(w["b_mod"], (0, chip * mod_cols), (1, mod_cols))

    place = jnp.stack([lax.axis_index("c"), 2 * lax.axis_index("x") + lax.axis_index("y")]).astype(jnp.int32)
    comm = _Overlap(shard_of, lambda got: install(wts, got), place)
    loss, grad_x, grads = _local_step(x, c, ctx, loss_target, wts, comm)
    loss = lax.psum(loss, ("x", "y", "c"))

    tags = MATRICES + ("small",)
    rest = tuple(n for n in tags if n not in comm.landed)
    if rest:
        rest_sums = _chip_sums("rest", rest, grads, place)
        for n, cs, r in zip(rest, rest_sums, _scatter_chips(rest_sums)):
            comm.landed[n] = (cs, r)
    halves = [_sum_chips("grad_sum_" + t, *comm.landed[t], place) for t in tags]
    reduced = dict(zip(tags, _join_halves(halves)))
    g_shard = {n: reduced[n] for n in MATRICES}
    g_shard["w_mod"] = grads["w_mod"]
    g_shard["dw_weight"] = reduced["small"][:CONV_W]
    g_shard["w_alpha_f"] = reduced["small"][32:32 + LOWRANK, :DK]
    g_shard["w_alpha_b"] = reduced["small"][32:32 + LOWRANK, DK:]

    rep_shapes = [w[n].shape for n in REPLICATED]
    small = _allreduce_small(_pack_flat([grads[n].reshape(w[n].shape) for n in REPLICATED], 8 * 1024))

    g_out, d_out, m_out, v_out = {}, {}, {}, {}
    for n, _ in SHARDED:
        s2 = w[n].shape[1:]
        go, d, nm, nv = _adamw("adamw_" + n, w[n].reshape(s2), g_shard[n], m[n].reshape(s2), v[n].reshape(s2))
        g_out[n], d_out[n] = go.reshape(w[n].shape), d.reshape(w[n].shape)
        m_out[n], v_out[n] = nm.reshape(w[n].shape), nv.reshape(w[n].shape)
    pk = lambda t: _pack_flat([t[n] for n in REPLICATED], 8 * 1024)
    go, d, nm, nv = _adamw("adamw_vectors", pk(w), small, pk(m), pk(v))
    for n, gg, dd, mm, vv in zip(REPLICATED, _unpack_flat(go, rep_shapes), _unpack_flat(d, rep_shapes),
                                 _unpack_flat(nm, rep_shapes), _unpack_flat(nv, rep_shapes)):
        g_out[n], d_out[n], m_out[n], v_out[n] = gg, dd, mm, vv

    return (loss, grad_x, *[g_out[n] for n in WEIGHTS], *[d_out[n] for n in WEIGHTS],
            *[m_out[n] for n in WEIGHTS], *[v_out[n] for n in WEIGHTS])
```

```python
import functools
import math

import jax
import jax.numpy as jnp
from jax import lax
from jax.experimental import pallas as pl
from jax.experimental.pallas import tpu as pltpu

F32, BF16 = jnp.float32, jnp.bfloat16
MESH = pl.DeviceIdType.MESH

D = 1024
FF = 2816
HEADS, DK, DV = 4, 128, 256
LOWRANK = 16
CONV_W = 31
CHUNK = 64
TAU = 16.0
EPS = 1e-6
Q_SCALE = DK ** -0.5
TM = 1024
D_IN = 7200
D_INP = 7296
LR_COL = 7168
OG_CB, GA_CB, GB_CB = 6, 4, 5
VMEM_LIMIT = 52 * 1024 * 1024
WGRAD_VMEM = 40 * 1024 * 1024

ADAM_LR, ADAM_B1, ADAM_B2, ADAM_EPS, ADAM_WD, ADAM_STEP = 0.001, 0.9, 0.999, 1e-08, 0.01, 10


def _silu(x):
    return x * jax.nn.sigmoid(x)


def _rms(h, g):
    return h * lax.rsqrt(jnp.mean(h * h, axis=-1, keepdims=True) + EPS) * g


def _modnorm(x, g, shift, scale):
    return _rms(x, g) * (1 + scale) + shift


def _cparams(sem=None):
    return pltpu.CompilerParams(dimension_semantics=sem, vmem_limit_bytes=VMEM_LIMIT)


def _twice(v):
    return v, v


def tok(arr, width=None, cb=0, clamp=None):
    return ("tok", arr, arr.shape[1] if width is None else width, cb, clamp)


def mod(arr):
    return ("mod", arr)


def const(arr):
    return ("const", arr)


def _rowmap(name, body, n_tiles, ins, outs, *, tpb, nb, side=None):
    def modrow(i):
        return jnp.minimum(i // tpb, nb)

    in_specs, args = [], []
    for spec in ins:
        if spec[0] == "tok":
            _, arr, width, cb, clamp = spec
            if clamp is None:
                im = lambda i, cb=cb: (i, cb)
            else:
                im = lambda i, cb=cb, clamp=clamp: (jnp.minimum(i, clamp), cb)
            in_specs.append(pl.BlockSpec((TM, width), im))
        elif spec[0] == "mod":
            arr = spec[1]
            in_specs.append(pl.BlockSpec((1, 1, arr.shape[2]), lambda i: (modrow(i), 0, 0)))
        else:
            arr = spec[1]
            in_specs.append(pl.BlockSpec(arr.shape, lambda i, nd=arr.ndim: (0,) * nd))
        args.append(arr)
    out_specs, out_shapes, aliases = [], [], {}
    for o in outs:
        if o[0] == "tok":
            _, rows, width, dtype = o
            out_shapes.append(jax.ShapeDtypeStruct((rows, width), dtype))
            out_specs.append(pl.BlockSpec((TM, width), lambda i: (i, 0)))
        elif o[0] == "cols":
            buf, width, cb = o[1:4]
            first_tile = o[4] if len(o) > 4 else 0
            if not isinstance(buf, jax.ShapeDtypeStruct):
                aliases[len(args)] = len(out_shapes)
                in_specs.append(ANY)
                args.append(buf)
            out_shapes.append(jax.ShapeDtypeStruct(buf.shape, buf.dtype))
            out_specs.append(pl.BlockSpec((TM, width), lambda i, cb=cb, t0=first_tile: (i + t0, cb)))
        elif o[0] == "tok_head":
            _, rows, width, dtype, last = o
            out_shapes.append(jax.ShapeDtypeStruct((rows, width), dtype))
            out_specs.append(pl.BlockSpec((TM, width), lambda i, last=last: (jnp.minimum(i, last), 0)))
        elif o[0] == "tokT":
            _, rows, width, dtype = o
            out_shapes.append(jax.ShapeDtypeStruct((width, rows), dtype))
            out_specs.append(pl.BlockSpec((width, TM), lambda i: (0, i)))
        elif o[0] == "acc":
            _, rows, width = o
            out_shapes.append(jax.ShapeDtypeStruct((rows, width), F32))
            out_specs.append(pl.BlockSpec((rows, width), lambda i: (0, 0)))
        else:
            width = o[1]
            rows_visited = min((n_tiles - 1) // tpb, nb) + 1
            out_shapes.append(jax.ShapeDtypeStruct((rows_visited, 1, width), F32))
            out_specs.append(pl.BlockSpec((1, 1, width), lambda i: (modrow(i), 0, 0)))
    n_in = len(ins)

    def kern(*refs):
        i = pl.program_id(0)
        vals = []
        for r, spec in zip(refs[:n_in], ins):
            val = r[0] if spec[0] == "mod" else r[...]
            vals.append(val.astype(F32) if spec[0] == "tok" and val.dtype == BF16 else val)
        res = body(i, *vals)
        for r, o, val in zip(refs[len(args):], outs, res):
            if o[0] in ("tok", "cols"):
                r[...] = val.astype(r.dtype)
            elif o[0] == "tok_head":
                @pl.when(i <= o[4])
                def _():
                    r[...] = val.astype(r.dtype)
            elif o[0] == "tokT":
                r[...] = val.T.astype(r.dtype)
            elif o[0] == "acc":
                @pl.when(i == 0)
                def _():
                    r[...] = jnp.zeros(r.shape, F32)
                r[...] += jnp.broadcast_to(val, r.shape)
            else:
                first = jnp.logical_or(i == 0, modrow(i) != modrow(jnp.maximum(i - 1, 0)))

                @pl.when(first)
                def _():
                    r[...] = jnp.zeros(r.shape, F32)
                r[0] += val

    if side is not None:
        assert not aliases
        return _pallas(kern, grid=(n_tiles,), in_specs=in_specs, out_specs=out_specs, out_shape=out_shapes,
                       scratch_shapes=[], sem=("arbitrary",), name=name, args=args, side=side)
    return pl.pallas_call(
        kern, grid=(n_tiles,), in_specs=in_specs, out_specs=out_specs, out_shape=out_shapes,
        input_output_aliases=aliases, compiler_params=_cparams(("arbitrary",)), name=name)(*args)


def _pick(n, cands):
    for c in cands:
        if n % c == 0:
            return c
    return n


def _pallas(kern, *, grid, in_specs, out_specs, out_shape, scratch_shapes, sem, name, args, side=None):
    if side is None:
        return pl.pallas_call(kern, grid=grid, in_specs=in_specs, out_specs=out_specs, out_shape=out_shape,
                              scratch_shapes=scratch_shapes, compiler_params=_cparams(sem), name=name)(*args)
    single = not isinstance(out_shape, (list, tuple))
    shapes = [out_shape] if single else list(out_shape)
    ospecs = [out_specs] if single else list(out_specs)
    n_in, n_out, n_scr = len(in_specs), len(shapes), len(scratch_shapes)
    s_in, s_out = list(side["ins"]), list(side["outs"])

    def wrapped(*refs):
        pos = [0]

        def take(n):
            pos[0] += n
            return refs[pos[0] - n:pos[0]]
        ins, sins, outs, souts, scr, sems = take(n_in), take(len(s_in)), take(n_out), take(len(s_out)), take(n_scr), take(2)
        ids = [pl.program_id(k) for k in range(len(grid))]
        first = functools.reduce(jnp.logical_and, [i == 0 for i in ids])
        last = functools.reduce(jnp.logical_and, [i == g - 1 for i, g in zip(ids, grid)])
        copy = _remote(*sems)

        @pl.when(first)
        def _():
            side["start"](sins, souts, copy)
        kern(*ins, *outs, *scr)

        @pl.when(last)
        def _():
            side["finish"](sins, souts, copy)

    res = pl.pallas_call(
        wrapped, grid=grid, in_specs=list(in_specs) + [ANY] * len(s_in), out_specs=ospecs + [ANY] * len(s_out),
        out_shape=shapes + s_out, scratch_shapes=list(scratch_shapes) + _sems(side["nsem"]),
        input_output_aliases={n_in + a: n_out + b for a, b in side.get("alias", {}).items()},
        compiler_params=_cparams(("arbitrary",) * len(grid)), name=name)(*args, *s_in)
    main = res[:n_out]
    return (main[0] if single else main), list(res[n_out:])


def _mm(name, a, b, *, trans_b=False, out_dtype=F32, a_fn=None, bias=None, rows=None, side=None, residual=None):
    m, k = a.shape if a.ndim == 2 else (a.shape[1], 2 * a.shape[2])
    m = m if rows is None else rows
    shard = b.shape[2] if b.ndim == 3 else None
    if trans_b:
        n = b.shape[-2]
        tk = _pick(shard, (2816, 2304, 1408, 1024)) if shard else (
            k if k <= 2816 else _pick(k, (2816, 2432, 2304, 2048, 1536, 1408, 1024, 512, 256, 128)))
        tn = _pick(n, (1024, 512, 384, 256, 128))
    else:
        n = 4 * shard if shard else b.shape[1]
        tk = k if k <= 2816 else _pick(k, (2816, 2432, 2304, 2048, 1536, 1408, 1024, 512, 256, 128))
        tn = _pick(shard, (512, 384, 1408, 256, 128)) if shard else _pick(n, (1024, 2432, 512, 384, 256, 128))
    if residual is not None:
        tm = next(c for c in (512, 256) if m % c == 0 and (residual["tpb"] * TM) % c == 0)
    else:
        tm = _pick(m, (1024, 512, 256))
    nk = k // tk
    per = shard // (tk if trans_b else tn) if shard else None
    dims = (((1,), (1,)), ((), ())) if trans_b else (((1,), (0,)), ((), ()))

    def kern(*refs):
        a_ref, b_ref = refs[0], refs[1]
        bias_ref = refs[2] if bias is not None else None
        acc_ref = refs[-1]
        if residual is not None:
            x_ref, gate_ref, o_ref, xo_ref = refs[-5:-1]
        else:
            o_ref = refs[-2]
        kk = pl.program_id(2)
        av = a_ref[...]
        if a_fn is not None:
            av = a_fn(av)
        p = lax.dot_general(av.astype(BF16), b_ref[...].astype(BF16), dims, preferred_element_type=F32)

        def finish(total):
            if bias_ref is not None:
                total = total + bias_ref[...]
            o_ref[...] = total.astype(o_ref.dtype)
            if residual is not None:
                xo_ref[...] = x_ref[...] + residual["scale"] * gate_ref[0] * total

        if nk == 1:
            finish(p)
        else:
            @pl.when(kk == 0)
            def _():
                acc_ref[...] = p

            @pl.when(kk > 0)
            def _():
                acc_ref[...] += p

            @pl.when(kk == nk - 1)
            def _():
                finish(acc_ref[...])

    if shard and trans_b:
        b_spec = pl.BlockSpec((None, tn, tk), lambda i, j, kk: (kk // per, j, kk % per))
    elif shard:
        b_spec = pl.BlockSpec((None, tk, tn), lambda i, j, kk: (j // per, kk, j % per))
    elif trans_b:
        b_spec = pl.BlockSpec((tn, tk), lambda i, j, kk: (j, kk))
    else:
        b_spec = pl.BlockSpec((tk, tn), lambda i, j, kk: (kk, j))
    if a.ndim == 3:
        pa = a.shape[2] // tk
        a_spec = pl.BlockSpec((None, tm, tk), lambda i, j, kk: (kk // pa, i, kk % pa))
    else:
        a_spec = pl.BlockSpec((tm, tk), lambda i, j, kk: (i, kk))
    in_specs = [a_spec, b_spec]
    args = [a, b]
    if bias is not None:
        in_specs.append(pl.BlockSpec((1, tn), lambda i, j, kk: (0, j)))
        args.append(bias)
    out_specs = pl.BlockSpec((tm, tn), lambda i, j, kk: (i, j))
    out_shape = jax.ShapeDtypeStruct((m, n), out_dtype)
    if residual is not None:
        tiles, nb = residual["tpb"] * TM // tm, residual["gate"].shape[0] - 1
        in_specs += [out_specs, pl.BlockSpec((1, 1, tn), lambda i, j, kk: (jnp.minimum(i // tiles, nb), 0, j))]
        args += [residual["x"], residual["gate"]]
        out_specs, out_shape = [out_specs, out_specs], [out_shape, jax.ShapeDtypeStruct((m, n), F32)]
    return _pallas(
        kern, grid=(m // tm, n // tn, nk), in_specs=in_specs, out_specs=out_specs, out_shape=out_shape,
        scratch_shapes=[pltpu.VMEM((tm, tn) if nk > 1 else (8, 128), F32)],
        sem=("parallel", "parallel", "arbitrary"), name=name, args=args, side=side)


def _mm_tn(name, x, dy):
    t, k1, n1 = x.shape[0], x.shape[1], dy.shape[1]
    tt = _pick(t, (512, 256, 128, 64, 8))
    tk1 = _pick(k1, (1024, 512, 256, 128))
    tn = _pick(n1, (512, 384, 256, 128))
    ns = t // tt

    def kern(x_ref, dy_ref, o_ref, acc_ref):
        s = pl.program_id(2)
        p = lax.dot_general(x_ref[...].astype(BF16), dy_ref[...].astype(BF16), (((0,), (0,)), ((), ())),
                            preferred_element_type=F32)

        @pl.when(s == 0)
        def _():
            acc_ref[...] = p

        @pl.when(s > 0)
        def _():
            acc_ref[...] += p

        @pl.when(s == ns - 1)
        def _():
            o_ref[...] = acc_ref[...].astype(o_ref.dtype)

    return pl.pallas_call(
        kern, grid=(k1 // tk1, n1 // tn, ns),
        in_specs=[pl.BlockSpec((tt, tk1), lambda i, j, s: (s, i)), pl.BlockSpec((tt, tn), lambda i, j, s: (s, j))],
        out_specs=pl.BlockSpec((tk1, tn), lambda i, j, s: (i, j)), out_shape=jax.ShapeDtypeStruct((k1, n1), F32),
        scratch_shapes=[pltpu.VMEM((tk1, tn), F32)],
        compiler_params=_cparams(("parallel", "parallel", "arbitrary")), name=name)(x, dy)


def _wgrad(name, xt, dy, rows=None, col_shards=False, side=None):
    k1 = xt.shape[0]
    t = xt.shape[1] if rows is None else rows
    n1 = dy.shape[1] if dy.ndim == 2 else 2 * dy.shape[2]
    tn = _pick(n1 // 4, (1408, 512, 384, 256, 128)) if col_shards else _pick(n1, (1024, 2432, 512, 384, 256, 128))

    def token_tile(tm):
        fixed = tm * tn * (4 + 4 + 2 * 2)
        return next((c for c in (2048, 1536, 1024, 512, 256, 128)
                     if t % c == 0 and fixed + 4 * c * (tm + tn) <= WGRAD_VMEM), 128)
    tm = next((c for c in (1024, 1408, 512, 256) if k1 % c == 0 and token_tile(c) >= 1024),
              _pick(k1, (1024, 1408, 512, 256)))
    tk = token_tile(tm)
    ns = t // tk
    per = n1 // 4 // tn

    def kern(x_ref, dy_ref, o_ref, acc_ref):
        s = pl.program_id(2)
        p = jnp.dot(x_ref[...], dy_ref[...], preferred_element_type=F32)

        @pl.when(s == 0)
        def _():
            acc_ref[...] = p

        @pl.when(s > 0)
        def _():
            acc_ref[...] += p

        @pl.when(s == ns - 1)
        def _():
            o_ref[...] = acc_ref[...].astype(o_ref.dtype)

    if dy.ndim == 3:
        pd = dy.shape[2] // tn
        dy_spec = pl.BlockSpec((None, tk, tn), lambda i, j, s: (j // pd, s, j % pd))
    else:
        dy_spec = pl.BlockSpec((tk, tn), lambda i, j, s: (s, j))
    if col_shards:
        out_spec = pl.BlockSpec((None, tm, tn), lambda i, j, s: (j // per, i, j % per))
        out_shape = jax.ShapeDtypeStruct((4, k1, n1 // 4), BF16)
    else:
        out_spec = pl.BlockSpec((tm, tn), lambda i, j, s: (i, j))
        out_shape = jax.ShapeDtypeStruct((k1, n1), BF16)
    return _pallas(
        kern, grid=(k1 // tm, n1 // tn, ns),
        in_specs=[pl.BlockSpec((tm, tk), lambda i, j, s: (i, s)), dy_spec],
        out_specs=out_spec, out_shape=out_shape, scratch_shapes=[pltpu.VMEM((tm, tn), F32)],
        sem=("parallel", "parallel", "arbitrary"), name=name, args=[xt, dy], side=side)


def _swiglu_fwd(name, u, w_gu, side=None):
    m = u.shape[0]
    half = w_gu.shape[2]
    tm = _pick(m, (512, 256))

    def kern(u_ref, wa_ref, wb_ref, ab_ref, hm_ref, hmt_ref):
        uv = u_ref[...]
        a = jnp.dot(uv, wa_ref[...], preferred_element_type=F32)
        b = jnp.dot(uv, wb_ref[...], preferred_element_type=F32)
        s = jax.nn.sigmoid(a)
        silu_a = a * s
        ab_ref[0] = (b * (s * (1 + a * (1 - s)))).astype(BF16)
        ab_ref[1] = silu_a.astype(BF16)
        hm = (silu_a * b).astype(BF16)
        hm_ref[...] = hm
        hmt_ref[...] = hm.T

    return _pallas(
        kern, grid=(2, m // tm),
        in_specs=[pl.BlockSpec((tm, D), lambda j, i: (i, 0)),
                  pl.BlockSpec((None, D, half), lambda j, i: (j, 0, 0)),
                  pl.BlockSpec((None, D, half), lambda j, i: (2 + j, 0, 0))],
        out_specs=[pl.BlockSpec((2, tm, half), lambda j, i: (0, i, j)),
                   pl.BlockSpec((tm, half), lambda j, i: (i, j)),
                   pl.BlockSpec((half, tm), lambda j, i: (j, i))],
        out_shape=[jax.ShapeDtypeStruct((2, m, FF), BF16), jax.ShapeDtypeStruct((m, FF), BF16),
                   jax.ShapeDtypeStruct((FF, m), BF16)],
        scratch_shapes=[], sem=("parallel", "parallel"), name=name, args=[u, w_gu, w_gu], side=side)


def _swiglu_bwd(name, df, w_down, ab):
    m = df.shape[0]
    half = FF // 2
    tm = _pick(m, (512, 256))

    def kern(df_ref, w_ref, ab_ref, o_ref):
        dh = lax.dot_general(df_ref[...], w_ref[...], (((1,), (1,)), ((), ())), preferred_element_type=F32)
        o_ref[0] = (dh * ab_ref[0].astype(F32)).astype(BF16)
        o_ref[1] = (dh * ab_ref[1].astype(F32)).astype(BF16)

    return pl.pallas_call(
        kern, grid=(2, m // tm),
        in_specs=[pl.BlockSpec((tm, D), lambda j, i: (i, 0)),
                  pl.BlockSpec((half, D), lambda j, i: (j, 0)),
                  pl.BlockSpec((2, tm, half), lambda j, i: (0, i, j))],
        out_specs=pl.BlockSpec((2, tm, half), lambda j, i: (0, i, j)),
        out_shape=jax.ShapeDtypeStruct((2, m, FF), BF16),
        compiler_params=_cparams(("parallel", "parallel")), name=name)(df, w_down, ab)


def _gla_maps(bl, t, tc):
    nx, nc = t // CHUNK, tc // CHUNK
    nxb = bl * nx

    def rowblk(d, b, n):
        c_ctx = jnp.where(d == 0, n, nc - 1 - n)
        c_x = jnp.where(d == 0, n - nc, nx - 1 - (n - nc))
        return jnp.where(n < nc, nxb + b * nc + c_ctx, b * nx + c_x)

    def xblk(d, b, n):
        n2 = jnp.maximum(n, nc)
        return b * nx + jnp.where(d == 0, n2 - nc, nx - 1 - (n2 - nc))

    return nx, nc, rowblk, xblk


def _dot01(m, x, cm):
    x1 = x.astype(BF16)
    r1 = x - x1.astype(F32)
    x2 = r1.astype(BF16)
    x3 = (r1 - x2.astype(F32)).astype(BF16)
    w = x.shape[1]
    p = lax.dot_general(m.astype(BF16), jnp.concatenate([x1, x2, x3], axis=1), (((cm,), (0,)), ((), ())),
                        preferred_element_type=F32)
    return p[:, :w] + p[:, w:2 * w] + p[:, 2 * w:]


def _gla_chunk(m, q, k, v, g, h):
    gh = g[:, h * DK:(h + 1) * DK]
    b = _dot01(m, gh, 1)
    tot = jnp.sum(gh, axis=0, keepdims=True)
    mid = b[CHUNK // 2:CHUNK // 2 + 1, :]
    qh = q[:, h * DK:(h + 1) * DK] * Q_SCALE
    kh = k[:, h * DK:(h + 1) * DK]
    vh = v[:, h * DV:(h + 1) * DV]
    return b, tot, mid, qh, kh, vh


def _dot(a, b, ca, cb):
    return lax.dot_general(a.astype(BF16), b.astype(BF16), (((ca,), (cb,)), ((), ())),
                           preferred_element_type=F32)


def _gla_fwd(p, g2, mmats, bl, t, tc):
    nx, nc, rowblk, xblk = _gla_maps(bl, t, tc)
    ns = nx + nc

    def kern(q0, k0, v0, g0, q1, k1, v1, g1, m_ref, o0, o1, ss_ref, s_ref):
        n = pl.program_id(1)

        @pl.when(n == 0)
        def _():
            s_ref[...] = jnp.zeros(s_ref.shape, F32)
        sides = ((q0, k0, v0, g0, o0), (q1, k1, v1, g1, o1))
        loaded = [(m_ref[d], q_ref[...].astype(F32), k_ref[...].astype(F32), v_ref[...].astype(F32), g_ref[...])
                  for d, (q_ref, k_ref, v_ref, g_ref, _) in enumerate(sides)]
        chains = [(d, h) for d in range(2) for h in range(HEADS)]
        base = [_gla_chunk(*loaded[d], h) for d, h in chains]
        pre = []
        for (d, h), (b, tot, mid, qh, kh, vh) in zip(chains, base):
            s0 = s_ref[d, h * DV:(h + 1) * DV, :]
            ss_ref[d, 0, 0, h * DV:(h + 1) * DV, :] = s0.astype(BF16)
            pre.append((s0, kh * jnp.exp(tot - b), qh * jnp.exp(b), qh * jnp.exp(b - mid), kh * jnp.exp(mid - b)))
        raw = [(_dot(qm, km, 1, 1), _dot(qe, s0, 1, 1), _dot(bs[5], kl, 0, 0))
               for bs, (s0, kl, qe, qm, km) in zip(base, pre)]
        for (d, h), bs, (s0, kl, qe, qm, km), (att_raw, inter, s_add) in zip(chains, base, pre, raw):
            s_ref[d, h * DV:(h + 1) * DV, :] = s0 * jnp.exp(bs[1]) + s_add
            sides[d][4][:, h * DV:(h + 1) * DV] = (inter + _dot(loaded[d][0] * att_raw, bs[5], 1, 0)).astype(BF16)

    def operands(d):
        return [pl.BlockSpec((CHUNK, 512), lambda b, n: (rowblk(d, b, n), 4)),
                pl.BlockSpec((CHUNK, 512), lambda b, n: (rowblk(d, b, n), 5)),
                pl.BlockSpec((CHUNK, 1024), lambda b, n: (rowblk(d, b, n), 3)),
                pl.BlockSpec((CHUNK, 512), lambda b, n: (rowblk(d, b, n), d))]

    o_shape = jax.ShapeDtypeStruct((bl * t, HEADS * DV), BF16)
    return pl.pallas_call(
        kern, grid=(bl, ns),
        in_specs=operands(0) + operands(1) + [pl.BlockSpec((2, CHUNK, CHUNK), lambda b, n: (0, 0, 0))],
        out_specs=[pl.BlockSpec((CHUNK, 1024), lambda b, n: (xblk(0, b, n), 0)),
                   pl.BlockSpec((CHUNK, 1024), lambda b, n: (xblk(1, b, n), 0)),
                   pl.BlockSpec((2, 1, 1, HEADS * DV, DK), lambda b, n: (0, b, n, 0, 0))],
        out_shape=[o_shape, o_shape, jax.ShapeDtypeStruct((2, bl, ns, HEADS * DV, DK), BF16)],
        scratch_shapes=[pltpu.VMEM((2, HEADS * DV, DK), F32)],
        compiler_params=_cparams(("parallel", "arbitrary")), name="gla_fwd")(p, p, p, g2, p, p, p, g2, mmats)


def _gla_bwd(p, g2, mmats, ssave, do, bl, t, tc):
    nx, nc, rowblk, xblk = _gla_maps(bl, t, tc)
    ns = nx + nc
    rev = lambda s: ns - 1 - s

    def kern(q0, k0, v0, g0, do0, q1, k1, v1, g1, do1, m_ref, ss_ref,
             dq0, dk0, dv0, dg0, dq1, dk1, dv1, dg1, ds_ref):
        step = pl.program_id(1)
        n = ns - 1 - step

        @pl.when(step == 0)
        def _():
            ds_ref[...] = jnp.zeros(ds_ref.shape, F32)
        live = (n >= nc).astype(F32)
        sides = ((q0, k0, v0, g0, do0, dq0, dk0, dv0, dg0), (q1, k1, v1, g1, do1, dq1, dk1, dv1, dg1))
        loaded = [(m_ref[d], s[0][...].astype(F32), s[1][...].astype(F32), s[2][...].astype(F32), s[3][...])
                  for d, s in enumerate(sides)]
        dovs = [s[4][...] * live for s in sides]
        chains = [(d, h) for d in range(2) for h in range(HEADS)]
        base = [_gla_chunk(*loaded[d], h) for d, h in chains]
        pre = []
        for (d, h), (b, tot, mid, qh, kh, vh) in zip(chains, base):
            eb, ebm, emb, etb = jnp.exp(b), jnp.exp(b - mid), jnp.exp(mid - b), jnp.exp(tot - b)
            pre.append(dict(
                eb=eb, ebm=ebm, emb=emb, etb=etb, etot=jnp.exp(tot), qe=qh * eb, qm=qh * ebm, km=kh * emb, kl=kh * etb,
                vh=vh, doh=dovs[d][:, h * DV:(h + 1) * DV], s0=ss_ref[d, 0, 0, h * DV:(h + 1) * DV, :].astype(F32),
                ds1=ds_ref[d, h * DV:(h + 1) * DV, :]))
        first = [dict(att=_dot(c["qm"], c["km"], 1, 1), datt=_dot(c["doh"], c["vh"], 1, 1),
                      dqe=_dot(c["doh"], c["s0"], 1, 0), ds_add=_dot(c["doh"], c["qe"], 0, 0),
                      dkl=_dot(c["vh"], c["ds1"], 1, 0), dv_s=_dot(c["kl"], c["ds1"], 1, 1)) for c in pre]
        second = []
        for (d, h), c, f in zip(chains, pre, first):
            m = loaded[d][0]
            ds_ref[d, h * DV:(h + 1) * DV, :] = c["ds1"] * c["etot"] + f["ds_add"]
            att, datt = m * f["att"], m * f["datt"]
            second.append(dict(dqm=_dot(datt, c["km"], 1, 0), dkm=_dot(datt, c["qm"], 0, 0),
                               dv_a=_dot(att, c["doh"], 0, 0)))
        for (d, h), c, f, s in zip(chains, pre, first, second):
            dq_ref, dk_ref, dv_ref, dg_ref = sides[d][5:]
            dtot = c["etot"] * jnp.sum(c["ds1"] * c["s0"], axis=0, keepdims=True) + jnp.sum(
                f["dkl"] * c["kl"], axis=0, keepdims=True)
            db = f["dqe"] * c["qe"] + s["dqm"] * c["qm"] - s["dkm"] * c["km"] - f["dkl"] * c["kl"]
            dq_ref[:, h * DK:(h + 1) * DK] = ((f["dqe"] * c["eb"] + s["dqm"] * c["ebm"]) * Q_SCALE).astype(BF16)
            dk_ref[:, h * DK:(h + 1) * DK] = (s["dkm"] * c["emb"] + f["dkl"] * c["etb"]).astype(BF16)
            dv_ref[:, h * DV:(h + 1) * DV] = (s["dv_a"] + f["dv_s"]).astype(BF16)
            dg_ref[:, h * DK:(h + 1) * DK] = _dot01(loaded[d][0], db, 0) + dtot

    nt = p.shape[0]

    def operands(d):
        return [pl.BlockSpec((CHUNK, 512), lambda b, s: (rowblk(d, b, rev(s)), 4)),
                pl.BlockSpec((CHUNK, 512), lambda b, s: (rowblk(d, b, rev(s)), 5)),
                pl.BlockSpec((CHUNK, 1024), lambda b, s: (rowblk(d, b, rev(s)), 3)),
                pl.BlockSpec((CHUNK, 512), lambda b, s: (rowblk(d, b, rev(s)), d)),
                pl.BlockSpec((CHUNK, 1024), lambda b, s: (xblk(d, b, rev(s)), 0))]

    def results(d):
        row = lambda b, s: (rowblk(d, b, rev(s)), 0)
        return [pl.BlockSpec((CHUNK, 512), row), pl.BlockSpec((CHUNK, 512), row), pl.BlockSpec((CHUNK, 1024), row),
                pl.BlockSpec((CHUNK, 512), row)]

    shapes = [jax.ShapeDtypeStruct((nt, 512), BF16), jax.ShapeDtypeStruct((nt, 512), BF16),
              jax.ShapeDtypeStruct((nt, 1024), BF16), jax.ShapeDtypeStruct((nt, 512), F32)]
    out = pl.pallas_call(
        kern, grid=(bl, ns),
        in_specs=operands(0) + operands(1) + [
            pl.BlockSpec((2, CHUNK, CHUNK), lambda b, s: (0, 0, 0)),
            pl.BlockSpec((2, 1, 1, HEADS * DV, DK), lambda b, s: (0, b, rev(s), 0, 0))],
        out_specs=results(0) + results(1), out_shape=shapes + shapes,
        scratch_shapes=[pltpu.VMEM((2, HEADS * DV, DK), F32)],
        compiler_params=_cparams(("parallel", "arbitrary")), name="gla_bwd")(
            p, p, p, g2, do, p, p, p, g2, do, mmats, ssave)
    return out[:4], out[4:]


CONV_CT = 256
CONV_PAD = 16
CONV_RC = 128
CONV_HALO = 24


def _conv_fill(zp, z_ref, t):
    zp[0:CONV_PAD, :] = jnp.zeros((CONV_PAD, CONV_CT), F32)
    zp[CONV_PAD + t:2 * CONV_PAD + t, :] = jnp.zeros((CONV_PAD, CONV_CT), F32)
    zp[CONV_PAD:CONV_PAD + t, :] = z_ref[...]


def _dwconv(name, z, w, bias, bl, t, flip):
    def kern(z_ref, w_ref, b_ref, o_ref, zp):
        _conv_fill(zp, z_ref, t)
        offs = [(CONV_W - j) if flip else (j + 1) for j in range(CONV_W)]
        for r in range(0, t, CONV_RC):
            acc = jnp.broadcast_to(b_ref[...], (CONV_RC, CONV_CT))
            for rot in range(8):
                win = zp[r + rot:r + rot + CONV_RC + CONV_HALO, :]
                for j in range(CONV_W):
                    if offs[j] % 8 == rot:
                        a = offs[j] - rot
                        acc = acc + w_ref[j:j + 1, :] * win[a:a + CONV_RC, :]
            o_ref[r:r + CONV_RC, :] = acc

    return pl.pallas_call(
        kern, grid=(bl, 1024 // CONV_CT),
        in_specs=[pl.BlockSpec((t, CONV_CT), lambda b, c: (b, c)),
                  pl.BlockSpec((32, CONV_CT), lambda b, c: (0, c)),
                  pl.BlockSpec((1, CONV_CT), lambda b, c: (0, c))],
        out_specs=pl.BlockSpec((t, CONV_CT), lambda b, c: (b, c)),
        out_shape=jax.ShapeDtypeStruct(z.shape, F32),
        scratch_shapes=[pltpu.VMEM((t + 2 * CONV_PAD, CONV_CT), F32)],
        compiler_params=_cparams(("parallel", "parallel")), name=name)(z, w, bias)


def _dwconv_wgrad(z, dzc, bl, t):
    def kern(z_ref, d_ref, dw_ref, db_ref, zp):
        b = pl.program_id(1)

        @pl.when(b == 0)
        def _():
            dw_ref[...] = jnp.zeros(dw_ref.shape, F32)
            db_ref[...] = jnp.zeros(db_ref.shape, F32)
        _conv_fill(zp, z_ref, t)
        for rot in range(8):
            taps = [j for j in range(CONV_W) if (j + 1) % 8 == rot]
            accs = [jnp.zeros((8, CONV_CT), F32) for _ in taps]
            for r in range(0, t, CONV_RC):
                d = d_ref[r:r + CONV_RC, :]
                win = zp[r + rot:r + rot + CONV_RC + CONV_HALO, :]
                for k, j in enumerate(taps):
                    a = j + 1 - rot
                    prod = d * win[a:a + CONV_RC, :]
                    accs[k] = accs[k] + jnp.sum(prod.reshape(CONV_RC // 8, 8, CONV_CT), axis=0)
            for k, j in enumerate(taps):
                dw_ref[j:j + 1, :] += jnp.sum(accs[k], axis=0, keepdims=True)
        db_ref[...] += jnp.sum(d_ref[...], axis=0, keepdims=True)

    return pl.pallas_call(
        kern, grid=(1024 // CONV_CT, bl),
        in_specs=[pl.BlockSpec((t, CONV_CT), lambda c, b: (b, c)),
                  pl.BlockSpec((t, CONV_CT), lambda c, b: (b, c))],
        out_specs=[pl.BlockSpec((32, CONV_CT), lambda c, b: (0, c)),
                   pl.BlockSpec((1, CONV_CT), lambda c, b: (0, c))],
        out_shape=[jax.ShapeDtypeStruct((32, 1024), F32), jax.ShapeDtypeStruct((1, 1024), F32)],
        scratch_shapes=[pltpu.VMEM((t + 2 * CONV_PAD, CONV_CT), F32)],
        compiler_params=_cparams(("parallel", "arbitrary")), name="dwconv_wgrad")(z, dzc)


def _ffn_fwd(tag, xin, n_tiles, g, sh, sc, gate, wts, gu_name, down_name, tpb, nb, comm=None):
    rows = n_tiles * TM
    rm = functools.partial(_rowmap, tpb=tpb, nb=nb)
    u, ut = _with_side(
        comm, tag + "_norm", None,
        lambda s: rm(tag + "_norm", lambda i, x, g_, sh_, sc_: _twice(_modnorm(x, g_, sh_, sc_)), n_tiles,
                     [tok(xin), const(g), mod(sh), mod(sc)], [("tok", rows, D, BF16), ("tokT", rows, D, BF16)], side=s))
    w_gu, w_down = wts[gu_name], wts[down_name]
    ab, hm, hmt = _with_side(comm, tag + "_gu", None, lambda s: _swiglu_fwd(tag + "_gu", u, w_gu, side=s))
    res = dict(x=xin, gate=gate, scale=0.5, tpb=tpb)
    f, xout = _with_side(comm, tag + "_down", None,
                         lambda s: _mm(tag + "_down", hm, w_down, out_dtype=BF16, side=s, residual=res))
    return xout, (ut, ab, hmt, f)


def _ffn_bwd(tag, xin, saved, dxout, dx_clamp, n_tiles, g, sh, sc, gate, w_gu, w_down, tpb, nb, comm=None, grads=None,
             names=None, keep_tiles=None):
    ut, ab, hmt, f = saved
    rows = n_tiles * TM
    rm = functools.partial(_rowmap, tpb=tpb, nb=nb)

    def mask(i):
        return 1.0 if dx_clamp is None else (i <= dx_clamp).astype(F32)

    def b1(i, dx, f_, gt):
        dx = dx * mask(i)
        return (0.5 * gt * dx, jnp.sum(0.5 * f_ * dx, axis=0, keepdims=True))
    df, dgate = rm(tag + "_bres", b1, n_tiles, [tok(dxout, clamp=dx_clamp), tok(f), mod(gate)],
                   [("tok", rows, D, BF16), ("modacc", D)])
    grads[names[1]] = _with_side(comm, tag + "_wdown", grads, lambda s: _wgrad(tag + "_wdown", hmt, df, side=s))
    dab = _swiglu_bwd(tag + "_bdown", df, w_down, ab)
    grads[names[0]] = _with_side(comm, tag + "_wgu", grads,
                                 lambda s: _wgrad(tag + "_wgu", ut, dab, col_shards=True, side=s))
    du = _with_side(comm, tag + "_bgu", grads,
                    lambda s: _mm(tag + "_bgu", dab, w_gu, trans_b=True, out_dtype=BF16, side=s))

    def b3(i, x, g_, sh_, sc_, du_, dx):
        _, vjp = jax.vjp(_modnorm, x, g_, sh_, sc_)
        dxn, dg, dsh, dsc = vjp(du_)
        return (dx * mask(i) + dxn, dg, dsh, dsc)
    dx_out = ("tok", rows, D, F32) if keep_tiles is None else ("tok_head", keep_tiles * TM, D, F32, keep_tiles - 1)
    dxin, dg, dsh, dsc = rm(tag + "_bnorm", b3, n_tiles,
                            [tok(xin), const(g), mod(sh), mod(sc), tok(du), tok(dxout, clamp=dx_clamp)],
                            [dx_out, ("acc", 1, D), ("modacc", D), ("modacc", D)])
    return dxin, dict(g=dg, sh=dsh, sc=dsc, gate=dgate)


IN_SEGMENTS = ((0, 4096), (5152, 7200), (4096, 5120), (5120, 5152))


def _perm_in_cols(shards):
    width = shards.shape[2]
    parts = []
    for a, b in IN_SEGMENTS:
        for j in range(4):
            lo, hi = max(a, width * j), min(b, width * (j + 1))
            if lo < hi:
                parts.append(shards[j][:, lo - width * j:hi - width * j])
    parts.append(jnp.zeros((shards.shape[1], D_INP - D_IN), shards.dtype))
    return jnp.concatenate(parts, axis=1)


def _unperm_in_cols(w):
    width = D_IN // 4
    shards = []
    for j in range(4):
        parts, start = [], 0
        for a, b in IN_SEGMENTS:
            lo, hi = max(a, width * j), min(b, width * (j + 1))
            if lo < hi:
                parts.append((lo, w[:, start + lo - a:start + hi - a]))
            start += b - a
        shards.append(jnp.concatenate([p for _, p in sorted(parts, key=lambda q: q[0])], axis=1))
    return jnp.stack(shards)


def _local_step(x, c, ctx, target, wts, comm):
    bl, t, _ = x.shape
    tc = ctx.shape[1]
    nx_rows, nc_rows = bl * t, bl * tc
    nt_rows = nx_rows + nc_rows
    tpb = t // TM
    nxt, ntt = nx_rows // TM, nt_rows // TM
    nb = bl
    rm = functools.partial(_rowmap, tpb=tpb, nb=nb)
    last_x = nxt - 1

    x0 = jnp.concatenate([x.reshape(nx_rows, D), ctx.reshape(nc_rows, D)], axis=0)
    tgt = target.reshape(nx_rows, D)

    cc = jnp.concatenate([c, wts["c_ctx"].reshape(1, D), jnp.zeros((8 - bl - 1, D), F32)], axis=0)
    modv, cc_all = _mod_forward(cc, wts["w_mod_shard"], wts["b_mod_shard"])
    mods = [modv[:nb + 1, k * D:(k + 1) * D].reshape(nb + 1, 1, D) for k in range(9)]

    x1, sv1 = _ffn_fwd("ffn1", x0, ntt, wts["g_ffn1"], mods[0], mods[1], mods[2], wts, "w1_gu", "w1_down",
                       tpb, nb, comm)
    u2, u2t = rm("in_norm", lambda i, x_, g_, sh_, sc_: _twice(_modnorm(x_, g_, sh_, sc_)), ntt,
                 [tok(x1), const(wts["g_mix"]), mod(mods[3]), mod(mods[4])],
                 [("tok", nt_rows, D, BF16), ("tokT", nt_rows, D, BF16)])
    w_inp = wts["w_in_p"]
    p = _with_side(comm, "in_proj", None, lambda s: _mm("in_proj", u2, w_inp, out_dtype=BF16, side=s))

    waf, wab, baf, bab = wts["w_alpha_f_pad"], wts["w_alpha_b_pad"], wts["b_alpha_f"], wts["b_alpha_b"]

    def dec_fwd(i, lr, wf, wb, bf_, bb_):
        zf = _dot(lr, wf, 1, 0) + bf_
        zb = _dot(lr, wb, 1, 0) + bb_
        return (jnp.concatenate([jax.nn.log_sigmoid(zf) / TAU, jax.nn.log_sigmoid(zb) / TAU], axis=1),)
    (gfb,) = rm("decay_fwd", dec_fwd, ntt, [tok(p, 128, LR_COL // 128), const(waf), const(wab), const(baf), const(bab)],
                [("tok", nt_rows, 1024, F32)])
    g2 = gfb
    tri = jnp.tril(jnp.ones((CHUNK, CHUNK), F32))
    mmats = jnp.stack([tri, tri.T])
    *o2, ssave = _gla_fwd(p, g2, mmats, bl, t, tc)

    gn_g = wts["gla_norm_g"]

    def gla_out(of, ob, og, gn):
        o = of + ob
        parts = []
        for h in range(HEADS):
            oh = o[:, h * DV:(h + 1) * DV]
            parts.append(oh * lax.rsqrt(jnp.mean(oh * oh, axis=-1, keepdims=True) + EPS))
        return jnp.concatenate(parts, axis=1) * gn * _silu(og)
    yg_in, yg_int = rm("gla_out", lambda i, of, ob, og, gn: _twice(gla_out(of, ob, og, gn)), nxt,
                       [tok(o2[0]), tok(o2[1]), tok(p, 1024, OG_CB), const(gn_g)],
                       [("tok", nx_rows, D, BF16), ("tokT", nx_rows, D, BF16)])
    y_gla = _with_side(comm, "gla_proj", None,
                       lambda s: _mm("gla_proj", yg_in, wts["w_gla_out"], out_dtype=BF16, side=s))

    (z,) = rm("glu", lambda i, a, b: (a * jax.nn.sigmoid(b),), nxt, [tok(p, 1024, 0), tok(p, 1024, 1)],
              [("tok", nx_rows, D, F32)])
    dw_w = jnp.concatenate([wts["dw_weight"], jnp.zeros((1, D), F32)], axis=0)
    zc = _dwconv("dwconv_fwd", z, dw_w, wts["dw_bias"], bl, t, False)

    def ln_silu(zc_, g_, b_):
        mu = jnp.mean(zc_, axis=-1, keepdims=True)
        var = jnp.mean(jnp.square(zc_ - mu), axis=-1, keepdims=True)
        return _silu((zc_ - mu) * lax.rsqrt(var + EPS) * g_ + b_)
    ln_g, ln_b = wts["conv_ln_g"], wts["conv_ln_b"]
    zl, zlt = rm("conv_ln", lambda i, zc_, g_, b_: _twice(ln_silu(zc_, g_, b_)), nxt,
                 [tok(zc), const(ln_g), const(ln_b)], [("tok", nx_rows, D, BF16), ("tokT", nx_rows, D, BF16)])
    y_conv = _mm("conv_proj", zl, wts["w_conv_out"], out_dtype=BF16)

    mg, mgt = rm("merge", lambda i, ga, gb, yc, yg: _twice(jax.nn.sigmoid(ga) * yc + jax.nn.sigmoid(gb) * yg), nxt,
                 [tok(p, 1024, GA_CB), tok(p, 1024, GB_CB), tok(y_conv), tok(y_gla)],
                 [("tok", nx_rows, D, BF16), ("tokT", nx_rows, D, BF16)])
    mix, x2 = _mm("out_proj", mg, wts["w_out"], out_dtype=BF16,
                  residual=dict(x=x1, gate=mods[5], scale=1.0, tpb=tpb))

    x3, sv2 = _ffn_fwd("ffn2", x2, nxt, wts["g_ffn2"], mods[6], mods[7], mods[8], wts, "w2_gu", "w2_down",
                       tpb, nb)
    g_fin = wts["g_final"].reshape(1, D)

    def head(i, x_, g_, tg):
        y, vjp = jax.vjp(_rms, x_, g_)
        diff = y - tg
        dx, dg = vjp(diff * (1.0 / D))
        loss = 0.5 * jnp.sum(jnp.mean(diff * diff, axis=-1, keepdims=True))
        return dx, dg, loss
    dx3, dg_final, loss_acc = rm("loss_head", head, nxt, [tok(x3), const(g_fin), tok(tgt)],
                                 [("tok", nx_rows, D, F32), ("acc", 1, D), ("acc", 8, 128)])
    loss = loss_acc[0, 0]

    grads = {}
    dx2, gf2 = _ffn_bwd("ffn2", x2, sv2, dx3, None, nxt, wts["g_ffn2"], mods[6], mods[7], mods[8],
                        wts["w2_gu"], wts["w2_down"], tpb, nb, comm, grads, ("w2_gu", "w2_down"))
    grads["g_ffn2"] = gf2["g"]

    dmix, dgate5 = rm("mix_bres", lambda i, dx, mx_, gt: (gt * dx, jnp.sum(mx_ * dx, axis=0, keepdims=True)), nxt,
                      [tok(dx2), tok(mix), mod(mods[5])], [("tok", nx_rows, D, BF16), ("modacc", D)])
    dmg = _mm("out_bproj", dmix, wts["w_out"], trans_b=True, out_dtype=BF16)
    grads["w_out"] = _wgrad("out_wgrad", mgt, dmix)

    def merge_bwd(i, dm, ga, gb, yc, yg):
        keep = (i <= last_x).astype(F32)
        dm = dm * keep
        sa, sb = jax.nn.sigmoid(ga), jax.nn.sigmoid(gb)
        return dm * sa, dm * sb, jnp.concatenate([dm * yc * sa * (1 - sa), dm * yg * sb * (1 - sb)], axis=1)
    cl = dict(clamp=last_x)
    dyc, dyg, dp = rm("merge_bwd", merge_bwd, ntt,
                      [tok(dmg, **cl), tok(p, 1024, GA_CB, last_x), tok(p, 1024, GB_CB, last_x), tok(y_conv, **cl),
                       tok(y_gla, **cl)],
                      [("tok", nt_rows, D, BF16), ("tok", nt_rows, D, BF16),
                       ("cols", jax.ShapeDtypeStruct((nt_rows, D_INP), BF16), 2048, 2)])

    dzl = _with_side(comm, "conv_bproj", grads, lambda s: _mm("conv_bproj", dyc, wts["w_conv_out"], trans_b=True,
                                                              rows=nx_rows, out_dtype=BF16, side=s))
    grads["w_conv_out"] = _wgrad("conv_wgrad", zlt, dyc, rows=nx_rows)

    def ln_bwd(i, zc_, g_, b_, dz_):
        _, vjp = jax.vjp(ln_silu, zc_, g_, b_)
        return vjp(dz_)
    dzc, dln_g, dln_b = rm("conv_ln_bwd", ln_bwd, nxt, [tok(zc), const(ln_g), const(ln_b), tok(dzl)],
                           [("tok", nx_rows, D, F32), ("acc", 1, D), ("acc", 1, D)])
    dz = _dwconv("dwconv_bwd", dzc, dw_w, jnp.zeros((1, D), F32), bl, t, True)
    ddw, ddb = _dwconv_wgrad(z, dzc, bl, t)
    grads.update(conv_ln_g=dln_g, conv_ln_b=dln_b, dw_weight=ddw[:CONV_W], dw_bias=ddb)

    def glu_bwd(i, dz_, a, b):
        keep = (i <= last_x).astype(F32)
        dz_ = dz_ * keep
        s = jax.nn.sigmoid(b)
        return (jnp.concatenate([dz_ * s, dz_ * a * s * (1 - s)], axis=1),)
    (dp,) = rm("glu_bwd", glu_bwd, ntt, [tok(dz, **cl), tok(p, 1024, 0, last_x), tok(p, 1024, 1, last_x)],
               [("cols", dp, 2048, 0)])

    dyg_in = _mm("gla_bproj", dyg, wts["w_gla_out"], trans_b=True, rows=nx_rows, out_dtype=BF16)
    grads["w_gla_out"] = _wgrad("gla_wgrad", yg_int, dyg, rows=nx_rows)

    def gla_out_bwd(i, of, ob, og, gn, dy):
        _, vjp = jax.vjp(gla_out, of, ob, og, gn)
        do_, _, dog_, dgn_ = vjp(dy)
        return do_, dog_, dgn_
    do, dp, dgn = rm("gla_out_bwd", gla_out_bwd, nxt,
                     [tok(o2[0]), tok(o2[1]), tok(p, 1024, OG_CB), const(gn_g), tok(dyg_in)],
                     [("tok", nx_rows, D, BF16), ("cols", dp, 1024, OG_CB), ("acc", 1, D)])
    grads["gla_norm_g"] = dgn
    (dp,) = rm("og_ctx_zero", lambda i: (jnp.zeros((TM, D), F32),), ntt - nxt, [], [("cols", dp, 1024, OG_CB, nxt)])

    dq2, dk2, dv2, dg2 = zip(*_gla_bwd(p, g2, mmats, ssave, do, bl, t, tc))

    def dec_bwd(i, lr, wf, wb, bf_, bb_, dgf, dgb_):
        zf = _dot(lr, wf, 1, 0) + bf_
        zb = _dot(lr, wb, 1, 0) + bb_
        dzf = dgf * (1 - jax.nn.sigmoid(zf)) * (1.0 / TAU)
        dzb = dgb_ * (1 - jax.nn.sigmoid(zb)) * (1.0 / TAU)
        dlr = _dot(dzf, wf, 1, 1) + _dot(dzb, wb, 1, 1)
        return (dlr, _dot(lr, dzf, 0, 0), _dot(lr, dzb, 0, 0), jnp.sum(dzf, axis=0, keepdims=True),
                jnp.sum(dzb, axis=0, keepdims=True))
    dp, dwaf, dwab, dbaf, dbab = rm(
        "decay_bwd", dec_bwd, ntt,
        [tok(p, 128, LR_COL // 128), const(waf), const(wab), const(baf), const(bab), tok(dg2[0]), tok(dg2[1])],
        [("cols", dp, 128, LR_COL // 128), ("acc", 128, 512), ("acc", 128, 512), ("acc", 1, 512), ("acc", 1, 512)])
    grads.update(w_alpha_f=dwaf[:LOWRANK], w_alpha_b=dwab[LOWRANK:2 * LOWRANK], b_alpha_f=dbaf, b_alpha_b=dbab)

    (dp,) = rm("gla_sum",
               lambda i, q0, q1, k0, k1, v0, v1: (jnp.concatenate([q0 + q1, k0 + k1, v0 + v1], axis=1),), ntt,
               [tok(dq2[0]), tok(dq2[1]), tok(dk2[0]), tok(dk2[1]), tok(dv2[0]), tok(dv2[1])],
               [("cols", dp, 2048, 1)])
    du2 = _with_side(comm, "in_bproj", grads,
                     lambda s: _mm("in_bproj", dp, w_inp, trans_b=True, out_dtype=BF16, side=s))
    grads["w_in_p"] = _wgrad("in_wgrad", u2t, dp)

    def in_norm_bwd(i, x_, g_, sh_, sc_, du_, dx):
        keep = (i <= last_x).astype(F32)
        _, vjp = jax.vjp(_modnorm, x_, g_, sh_, sc_)
        dxn, dg, dsh, dsc = vjp(du_)
        return (dx * keep + dxn, dg, dsh, dsc)
    dx1, dg_mix, dsh3, dsc4 = rm("in_norm_bwd", in_norm_bwd, ntt,
                                 [tok(x1), const(wts["g_mix"]), mod(mods[3]), mod(mods[4]), tok(du2), tok(dx2, **cl)],
                                 [("tok", nt_rows, D, F32), ("acc", 1, D), ("modacc", D), ("modacc", D)])
    grads["g_mix"] = dg_mix

    dx0, gf1 = _ffn_bwd("ffn1", x0, sv1, dx1, None, ntt, wts["g_ffn1"], mods[0], mods[1], mods[2],
                        wts["w1_gu"], wts["w1_down"], tpb, nb, comm, grads, ("w1_gu", "w1_down"), keep_tiles=nxt)
    grads["g_ffn1"] = gf1["g"]
    grad_x = dx0.reshape(bl, t, D)

    dmods = [gf1["sh"], gf1["sc"], gf1["gate"], dsh3, dsc4, dgate5, gf2["sh"], gf2["sc"], gf2["gate"]]
    dmod = jnp.concatenate(
        [jnp.concatenate([a.reshape(a.shape[0], D), jnp.zeros((8 - a.shape[0], D), F32)], axis=0) for a in dmods],
        axis=1)
    grads["w_mod"], grads["c_ctx"], grads["b_mod"] = _mod_backward(dmod, cc_all, wts["w_mod_shard"], nb)
    grads["g_final"] = dg_final.reshape(D)
    return loss, grad_x, grads


ANY = pl.BlockSpec(memory_space=pl.ANY)


def _place():
    x, y, c = lax.axis_index("x"), lax.axis_index("y"), lax.axis_index("c")
    chips = [(1 - x, y), (x, 1 - y), (1 - x, 1 - y)]
    return x, y, c, chips


def _remote(send_sems, recv_sems):
    def copy(k, src, dst, to):
        return pltpu.make_async_remote_copy(src_ref=src, dst_ref=dst, send_sem=send_sems.at[k],
                                            recv_sem=recv_sems.at[k], device_id=to, device_id_type=MESH)
    return copy


def _sems(n):
    return [pltpu.SemaphoreType.DMA((n,)), pltpu.SemaphoreType.DMA((n,))]


def _swap_halves(name, gs):
    n = len(gs)

    def body(*refs):
        ins, outs = refs[:n], refs[n:2 * n]
        copy = _remote(refs[2 * n], refs[2 * n + 1])
        x, y, c, _ = _place()
        cps = []
        for i in range(n):
            hr = ins[i].shape[1] // 2
            cps.append(copy(i, ins[i].at[:, pl.ds((1 - c) * hr, hr), :], outs[i], (x, y, 1 - c)))
            cps[-1].start()
        for cp in cps:
            cp.wait()

    return pl.pallas_call(
        body, out_shape=[jax.ShapeDtypeStruct((4, g.shape[1] // 2, g.shape[2]), g.dtype) for g in gs],
        in_specs=[ANY] * n, out_specs=[ANY] * n, scratch_shapes=_sems(n), name=name)(*gs)


def _row_tile(hr):
    return hr if hr <= 256 else _pick(hr, (256, 176, 128, 64, 32, 16))


def _add_halves(name, g, r, place):
    hr = r.shape[1]
    tr = _row_tile(hr)
    nblk = hr // tr

    def kern(p_ref, g_ref, r_ref, o_ref):
        o_ref[...] = (g_ref[...].astype(F32) + r_ref[...].astype(F32)).astype(o_ref.dtype)

    blk = (1, tr, g.shape[2])
    return pl.pallas_call(
        kern,
        grid_spec=pltpu.PrefetchScalarGridSpec(
            num_scalar_prefetch=1, grid=(4, nblk),
            in_specs=[pl.BlockSpec(blk, lambda j, i, p: (j, p[0] * nblk + i, 0)),
                      pl.BlockSpec(blk, lambda j, i, p: (j, i, 0))],
            out_specs=pl.BlockSpec(blk, lambda j, i, p: (j, i, 0))),
        out_shape=jax.ShapeDtypeStruct(r.shape, r.dtype),
        compiler_params=_cparams(("parallel", "parallel")), name=name)(place, g, r)


def _scatter_chips(cs):
    n = len(cs)

    def body(*refs):
        ins, outs = refs[:n], refs[n:2 * n]
        copy = _remote(refs[2 * n], refs[2 * n + 1])
        x, y, c, chips = _place()
        me = 2 * x + y
        sends = []
        for i in range(n):
            for j, (px, py) in enumerate(chips):
                sends.append(copy(3 * i + j, ins[i].at[2 * px + py], outs[i].at[me], (px, py, c)))
                sends[-1].start()
        for i in range(n):
            for j, (px, py) in enumerate(chips):
                src = 2 * px + py
                copy(3 * i + j, ins[i].at[src], outs[i].at[src], (px, py, c)).wait_recv()
        for cp in sends:
            cp.wait_send()

    return pl.pallas_call(
        body, out_shape=[jax.ShapeDtypeStruct(a.shape, a.dtype) for a in cs], in_specs=[ANY] * n,
        out_specs=[ANY] * n, scratch_shapes=_sems(3 * n), name="grad_scatter_chips")(*cs)


def _sum_chips(name, cs, r, place):
    hr = r.shape[1]
    tr = _row_tile(hr)
    nblk = hr // tr

    def kern(p_ref, c_ref, r0, r1, r2, r3, o_ref):
        me = p_ref[1]
        acc = None
        for k, rk in enumerate((r0, r1, r2, r3)):
            val = jnp.where(me == k, c_ref[0].astype(F32), rk[0].astype(F32))
            acc = val if acc is None else acc + val
        o_ref[...] = acc

    blk = (1, tr, r.shape[2])

    def slot(k):
        return lambda i, p: (jnp.where(p[1] == k, (k + 1) % 4, k), i, 0)

    return pl.pallas_call(
        kern,
        grid_spec=pltpu.PrefetchScalarGridSpec(
            num_scalar_prefetch=1, grid=(nblk,),
            in_specs=[pl.BlockSpec(blk, lambda i, p: (p[1], i, 0))] + [pl.BlockSpec(blk, slot(k)) for k in range(4)],
            out_specs=pl.BlockSpec((tr, r.shape[2]), lambda i, p: (p[0] * nblk + i, 0))),
        out_shape=jax.ShapeDtypeStruct((2 * hr, r.shape[2]), F32),
        compiler_params=_cparams(("parallel",)), name=name)(place, cs, r, r, r, r)


def _join_halves(fs):
    n = len(fs)

    def body(*refs):
        ins, outs = refs[:n], refs[n:2 * n]
        copy = _remote(refs[2 * n], refs[2 * n + 1])
        x, y, c, _ = _place()
        cps = []
        for i in range(n):
            hr = ins[i].shape[0] // 2
            cps.append(copy(i, ins[i].at[pl.ds(c * hr, hr), :], outs[i].at[pl.ds(c * hr, hr), :], (x, y, 1 - c)))
            cps[-1].start()
        for i in range(n):
            hr = ins[i].shape[0] // 2
            other = outs[i].at[pl.ds((1 - c) * hr, hr), :]
            copy(i, other, other, (x, y, 1 - c)).wait_recv()
        for cp in cps:
            cp.wait_send()

    return pl.pallas_call(
        body, out_shape=[jax.ShapeDtypeStruct(f.shape, f.dtype) for f in fs], in_specs=[ANY] * n,
        out_specs=[ANY] * n, input_output_aliases={i: i for i in range(n)}, scratch_shapes=_sems(n),
        name="grad_join_halves")(*fs)


def _half_rows(n, hc):
    return pl.ds(hc * (n // 2), n // 2)


def _gather_ici_side(shards):
    n = len(shards)

    def copies(ins, outs, copy):
        x, y, c, chips = _place()
        me = 2 * x + y
        for i in range(n):
            rows = _half_rows(ins[i].shape[0], c)
            for j, (px, py) in enumerate(chips):
                yield (copy(3 * i + j, ins[i].at[rows, :], outs[i].at[me, rows, :], (px, py, c)),
                       outs[i].at[2 * px + py, rows, :])

    def start(ins, outs, copy):
        for cp, _ in copies(ins, outs, copy):
            cp.start()

    def finish(ins, outs, copy):
        x, y, c, chips = _place()
        k = 0
        for cp, landing in copies(ins, outs, copy):
            copy(k, landing, landing, (x, y, c)).wait_recv()
            k += 1
        for cp, _ in copies(ins, outs, copy):
            cp.wait_send()

    return dict(ins=list(shards), outs=[jax.ShapeDtypeStruct((4,) + s.shape, s.dtype) for s in shards], nsem=3 * n,
                start=start, finish=finish)


def _gather_d2d_side(shards, bufs):
    n = len(shards)

    def copies(ins, outs, copy):
        x, y, c, chips = _place()
        me = 2 * x + y
        sibling = (x, y, 1 - c)
        for i in range(n):
            a = ins[i].shape[0]
            yield copy(4 * i + 3, ins[i], outs[i].at[me], sibling), outs[i].at[me]
            for j, (px, py) in enumerate(chips):
                src = 2 * px + py
                mine = outs[i].at[src, _half_rows(a, c), :]
                yield copy(4 * i + j, mine, mine, sibling), outs[i].at[src, _half_rows(a, 1 - c), :]

    def start(ins, outs, copy):
        for cp, _ in copies(ins, outs, copy):
            cp.start()

    def finish(ins, outs, copy):
        x, y, c, _ = _place()
        for i in range(n):
            for k, (cp, landing) in enumerate(list(copies(ins, outs, copy))[4 * i:4 * i + 4]):
                sem = 4 * i + 3 if k == 0 else 4 * i + k - 1
                copy(sem, landing, landing, (x, y, 1 - c)).wait_recv()
        for cp, _ in copies(ins, outs, copy):
            cp.wait_send()

    return dict(ins=list(shards) + list(bufs), outs=[jax.ShapeDtypeStruct(b.shape, b.dtype) for b in bufs],
                nsem=4 * n, alias={n + i: i for i in range(n)}, start=start, finish=finish)


def _gather_d2d(shards, bufs):
    side = _gather_d2d_side(shards, bufs)
    n_in = len(side["ins"])

    def body(*refs):
        ins, outs = refs[:n_in], refs[n_in:n_in + len(bufs)]
        copy = _remote(*refs[n_in + len(bufs):])
        side["start"](ins, outs, copy)
        side["finish"](ins, outs, copy)

    return pl.pallas_call(
        body, out_shape=side["outs"], in_specs=[ANY] * n_in, out_specs=[ANY] * len(bufs),
        input_output_aliases=side["alias"], scratch_shapes=_sems(side["nsem"]), name="gather_weights_d2d")(*side["ins"])


def _scatter_side(cs):
    n = len(cs)

    def copies(ins, outs, copy):
        x, y, c, chips = _place()
        me = 2 * x + y
        for i in range(n):
            for j, (px, py) in enumerate(chips):
                yield copy(3 * i + j, ins[i].at[2 * px + py], outs[i].at[me], (px, py, c)), outs[i].at[2 * px + py]

    def start(ins, outs, copy):
        for cp, _ in copies(ins, outs, copy):
            cp.start()

    def finish(ins, outs, copy):
        x, y, c, _ = _place()
        for k, (cp, landing) in enumerate(copies(ins, outs, copy)):
            copy(k, landing, landing, (x, y, c)).wait_recv()
        for cp, _ in copies(ins, outs, copy):
            cp.wait_send()

    return dict(ins=list(cs), outs=[jax.ShapeDtypeStruct(a.shape, a.dtype) for a in cs], nsem=3 * n,
                start=start, finish=finish)


def _mod_forward(cc, w_shard, b_shard):
    cc_all = _allreduce_small(cc, "cond_gather", reduce=False).reshape(64, D)
    part = _mm("mod_fwd", cc_all, w_shard, a_fn=_silu, bias=b_shard)
    got = _mod_rows_exchange(part)
    return jnp.concatenate([got[j] for j in range(4)], axis=1), cc_all


def _mod_backward(dmod, cc_all, w_shard, ctx_row):
    w = w_shard.shape[1]
    blocks = _dmod_exchange(jnp.transpose(dmod.reshape(8, 4, w), (1, 0, 2))).reshape(64, w)
    dsc = _mm("mod_bproj", blocks, w_shard, trans_b=True)

    def tail(cc_ref, dsc_ref, dm_ref, s_ref, dctx_ref, db_ref):
        cc_ = cc_ref[...]
        s = jax.nn.sigmoid(cc_)
        s_ref[...] = (cc_ * s).astype(BF16)
        is_ctx = ((lax.broadcasted_iota(jnp.int32, (64, 1), 0) & 7) == ctx_row).astype(F32)
        dctx_ref[...] = 0.5 * jnp.sum(dsc_ref[...] * (s * (1 + cc_ * (1 - s))) * is_ctx, axis=0, keepdims=True)
        db_ref[...] = jnp.sum(dm_ref[...], axis=0, keepdims=True)

    s_all, dctx, db_mod = pl.pallas_call(
        tail, out_shape=[jax.ShapeDtypeStruct((64, D), BF16), jax.ShapeDtypeStruct((1, D), F32),
                         jax.ShapeDtypeStruct((1, dmod.shape[1]), F32)], name="mod_tail")(cc_all, dsc, dmod)
    return _mm_tn("mod_wgrad", s_all, blocks), dctx[0], db_mod


def _small_pack(dw, af, ab):
    return jnp.concatenate([dw, jnp.zeros((1, dw.shape[1]), F32), jnp.concatenate([af, ab], axis=1)], axis=0)


def _grad_pieces(n, grads):
    if n == "small":
        return jnp.stack([_small_pack(grads["dw_weight"][:, 256 * j:256 * (j + 1)],
                                      grads["w_alpha_f"][:, DK * j:DK * (j + 1)],
                                      grads["w_alpha_b"][:, DK * j:DK * (j + 1)]) for j in range(4)])
    if n == "w_in":
        return _unperm_in_cols(grads["w_in_p"])
    g = grads[n]
    return g if g.ndim == 3 else g.reshape(4, g.shape[0] // 4, g.shape[1])


def _swap_side(gs):
    n = len(gs)

    def copies(ins, outs, copy):
        x, y, c, _ = _place()
        for i in range(n):
            yield copy(i, ins[i].at[:, _half_rows(ins[i].shape[1], 1 - c), :], outs[i], (x, y, 1 - c))

    def start(ins, outs, copy):
        for cp in copies(ins, outs, copy):
            cp.start()

    def finish(ins, outs, copy):
        for cp in copies(ins, outs, copy):
            cp.wait()

    return dict(ins=list(gs), outs=[jax.ShapeDtypeStruct((4, g.shape[1] // 2, g.shape[2]), g.dtype) for g in gs],
                nsem=n, start=start, finish=finish)


def _chip_sums(tag, names, grads, place, swapped=None):
    gs = [_grad_pieces(n, grads) for n in names] if swapped is None else swapped[0]
    got = _swap_halves("grad_swap_" + tag, gs) if swapped is None else swapped[1]
    return [_add_halves("grad_add_" + n, g, r, place) for n, g, r in zip(names, gs, got)]


class _Overlap:
    SCATTER = {"in_bproj": ("w2_down", "w2_gu", "w_out"),
               "ffn1_wgu": ("w_conv_out", "w_gla_out", "w_in", "small"),
               "ffn1_bgu": ("w1_down", "w1_gu")}
    GATHER = {("ffn1_norm", None): ("w1_gu", "w1_down"),
              ("ffn1_gu", "ffn1_down"): ("w_in", "w_conv_out", "w_gla_out", "w_out", "small"),
              ("in_proj", "gla_proj"): ("w2_gu", "w2_down")}
    LATE = tuple(n for names in GATHER.values() for n in names)
    SWAP = {"conv_bproj": "in_bproj", "ffn1_wdown": "ffn1_wgu"}

    def __init__(self, shard_of, install, place):
        self.shard_of, self.install, self.place = shard_of, install, place
        self.bufs, self.pending, self.landed, self.swapped = {}, None, {}, {}

    def side(self, tag, grads):
        for (ici, d2d), names in self.GATHER.items():
            shards = [self.shard_of(n) for n in names]
            if tag == ici:
                return _gather_ici_side(shards)
            if tag == d2d:
                return _gather_d2d_side(shards, self.bufs[ici])
        if tag in self.SWAP:
            gs = [_grad_pieces(n, grads) for n in self.SCATTER[self.SWAP[tag]]]
            self.swapped[self.SWAP[tag]] = [gs, None]
            return _swap_side(gs)
        if tag in self.SCATTER:
            names = self.SCATTER[tag]
            self.pending = (names, _chip_sums(tag, names, grads, self.place, self.swapped.get(tag)))
            return _scatter_side(self.pending[1])
        return None

    def done(self, tag, outs):
        for (ici, d2d), names in self.GATHER.items():
            if tag == ici and d2d is None:
                self.install(dict(zip(names, _gather_d2d([self.shard_of(n) for n in names], outs))))
                return
            if tag == ici:
                self.bufs[ici] = outs
                return
            if tag == d2d:
                self.install(dict(zip(names, outs)))
                return
        if tag in self.SWAP:
            self.swapped[self.SWAP[tag]][1] = outs
            return
        for n, cs, r in zip(*self.pending, outs):
            self.landed[n] = (cs, r)


def _with_side(comm, tag, grads, call):
    side = comm.side(tag, grads) if comm is not None else None
    if side is None:
        return call(None)
    res, outs = call(side)
    comm.done(tag, outs)
    return res


def _allreduce_small(v, name="allreduce_small", reduce=True):
    def body(x_ref, out_ref, *scratch):
        gath = out_ref if not reduce else scratch[0]
        send_sems, recv_sems, local_sem = scratch[-3:]
        x, y, c, chips = _place()
        me, sibling = (x, y, c), (x, y, 1 - c)

        def slot(px, py, pc):
            return gath.at[4 * px + 2 * py + pc]

        def copy(k, block, to, src=None):
            return pltpu.make_async_remote_copy(
                src_ref=slot(*block) if src is None else src, dst_ref=slot(*block), send_sem=send_sems.at[k],
                recv_sem=recv_sems.at[k], device_id=to, device_id_type=MESH)

        mine = pltpu.make_async_copy(x_ref, slot(*me), local_sem)
        mine.start()
        first = [copy(0, me, sibling, src=x_ref)]
        first += [copy(1 + j, me, (*chip, c), src=x_ref) for j, chip in enumerate(chips)]
        for cp in first:
            cp.start()
        passed = [copy(4 + j, (*chip, c), sibling) for j, chip in enumerate(chips)]
        for j, chip in enumerate(chips):
            copy(1 + j, (*chip, c), me).wait_recv()
            passed[j].start()
        copy(0, sibling, me).wait_recv()
        for j, chip in enumerate(chips):
            copy(4 + j, (*chip, 1 - c), me).wait_recv()
        for cp in first + passed:
            cp.wait_send()
        mine.wait()
        if reduce:
            acc = gath[0]
            for k in range(1, 8):
                acc = acc + gath[k]
            out_ref[...] = acc

    vm = pl.BlockSpec(memory_space=pltpu.VMEM)
    sems = [pltpu.SemaphoreType.DMA((7,)), pltpu.SemaphoreType.DMA((7,)), pltpu.SemaphoreType.DMA(())]
    return pl.pallas_call(
        body, out_shape=jax.ShapeDtypeStruct(v.shape if reduce else (8,) + v.shape, F32), in_specs=[vm], out_specs=vm,
        scratch_shapes=([pltpu.VMEM((8,) + v.shape, F32)] if reduce else []) + sems, name=name)(v)


def _mod_rows_exchange(part):
    w = part.shape[1]

    def body(p_ref, out_ref, send_sems, recv_sems):
        x, y, c, chips = _place()
        me = 2 * x + y
        copy = _remote(send_sems, recv_sems)
        sends = []
        for j, (px, py) in enumerate(chips):
            rows = pl.ds(pl.multiple_of(8 * (4 * px + 2 * py + c), 8), 8)
            sends.append(copy(j, p_ref.at[rows, :], out_ref.at[me], (px, py, c)))
            sends[-1].start()
        out_ref[me] = p_ref[pl.ds(pl.multiple_of(8 * (4 * x + 2 * y + c), 8), 8), :]
        for j, (px, py) in enumerate(chips):
            landing = out_ref.at[2 * px + py]
            copy(j, landing, landing, (px, py, c)).wait_recv()
        for cp in sends:
            cp.wait_send()

    vm = pl.BlockSpec(memory_space=pltpu.VMEM)
    return pl.pallas_call(body, out_shape=jax.ShapeDtypeStruct((4, 8, w), F32), in_specs=[vm], out_specs=vm,
                          scratch_shapes=_sems(3), name="mod_rows_exchange")(part)


def _dmod_exchange(dm):
    w = dm.shape[2]

    def body(d_ref, out_ref, send_sems, recv_sems):
        x, y, c, _ = _place()
        copy = _remote(send_sems, recv_sems)
        mine = 4 * x + 2 * y + c
        sends = []
        for r in range(1, 8):
            tx, ty, tc = x ^ (r >> 2), y ^ ((r >> 1) & 1), c ^ (r & 1)
            sends.append(copy(r - 1, d_ref.at[2 * tx + ty], out_ref.at[mine], (tx, ty, tc)))
            sends[-1].start()
        out_ref[mine] = d_ref[2 * x + y]
        for r in range(1, 8):
            tx, ty, tc = x ^ (r >> 2), y ^ ((r >> 1) & 1), c ^ (r & 1)
            landing = out_ref.at[4 * tx + 2 * ty + tc]
            copy(r - 1, landing, landing, (tx, ty, tc)).wait_recv()
        for cp in sends:
            cp.wait_send()

    vm = pl.BlockSpec(memory_space=pltpu.VMEM)
    return pl.pallas_call(body, out_shape=jax.ShapeDtypeStruct((8, 8, w), F32), in_specs=[vm], out_specs=vm,
                          scratch_shapes=_sems(7), name="dmod_exchange")(dm)


def _adamw(name, w, g, m, v):
    r, cols = w.shape
    budget = 262144
    tr = r if r * cols <= budget else next(c for c in (256, 128, 64, 32, 16, 8) if r % c == 0 and c * cols <= budget)

    def kern(w_ref, g_ref, m_ref, v_ref, go_ref, d_ref, nm_ref, nv_ref):
        gv = g_ref[...]
        go_ref[...] = gv
        nm = ADAM_B1 * m_ref[...] + (1.0 - ADAM_B1) * gv
        nv = ADAM_B2 * v_ref[...] + (1.0 - ADAM_B2) * jnp.square(gv)
        m_hat = nm / (1.0 - ADAM_B1 ** ADAM_STEP)
        v_hat = nv / (1.0 - ADAM_B2 ** ADAM_STEP)
        d_ref[...] = -ADAM_LR * (m_hat / (jnp.sqrt(v_hat) + ADAM_EPS) + ADAM_WD * w_ref[...])
        nm_ref[...] = nm
        nv_ref[...] = nv

    spec = pl.BlockSpec((tr, cols), lambda i: (i, 0))
    shp = jax.ShapeDtypeStruct((r, cols), F32)
    return pl.pallas_call(kern, grid=(r // tr,), in_specs=[spec] * 4, out_specs=[spec] * 4, out_shape=[shp] * 4,
                          compiler_params=_cparams(("parallel",)), name=name)(w, g, m, v)


SHARDED = (("w_mod", 1), ("w1_gu", 1), ("w1_down", 0), ("w_in", 1), ("dw_weight", 1), ("w_conv_out", 0),
           ("w_alpha_f", 1), ("w_alpha_b", 1), ("w_gla_out", 0), ("w_out", 0), ("w2_gu", 1), ("w2_down", 0))
REPLICATED = ("c_ctx", "b_mod", "g_ffn1", "g_mix", "dw_bias", "conv_ln_g", "conv_ln_b", "b_alpha_f", "b_alpha_b",
              "gla_norm_g", "g_ffn2", "g_final")
WEIGHTS = ("c_ctx", "w_mod", "b_mod", "g_ffn1", "w1_gu", "w1_down", "g_mix", "w_in", "dw_weight", "dw_bias",
           "conv_ln_g", "conv_ln_b", "w_conv_out", "w_alpha_f", "b_alpha_f", "w_alpha_b", "b_alpha_b", "gla_norm_g",
           "w_gla_out", "w_out", "g_ffn2", "w2_gu", "w2_down", "g_final")
MATRICES = ("w1_gu", "w1_down", "w_in", "w_conv_out", "w_gla_out", "w_out", "w2_gu", "w2_down")


def _pack_flat(parts, align):
    flat = jnp.concatenate([p.reshape(-1) for p in parts])
    pad = (-flat.shape[0]) % align
    return jnp.concatenate([flat, jnp.zeros((pad,), flat.dtype)]).reshape(-1, 1024)


def _unpack_flat(flat2d, shapes):
    flat = flat2d.reshape(-1)
    out, off = [], 0
    for s in shapes:
        n = math.prod(s)
        out.append(flat[off:off + n].reshape(s))
        off += n
    return out


def kernel(x, c, ctx, c_ctx, w_mod, b_mod, g_ffn1, w1_gu, w1_down, g_mix, w_in, dw_weight, dw_bias, conv_ln_g, conv_ln_b, w_conv_out, w_alpha_f, b_alpha_f, w_alpha_b, b_alpha_b, gla_norm_g, w_gla_out, w_out, g_ffn2, w2_gu, w2_down, g_final, loss_target, m_c_ctx, m_w_mod, m_b_mod, m_g_ffn1, m_w1_gu, m_w1_down, m_g_mix, m_w_in, m_dw_weight, m_dw_bias, m_conv_ln_g, m_conv_ln_b, m_w_conv_out, m_w_alpha_f, m_b_alpha_f, m_w_alpha_b, m_b_alpha_b, m_gla_norm_g, m_w_gla_out, m_w_out, m_g_ffn2, m_w2_gu, m_w2_down, m_g_final, v_c_ctx, v_w_mod, v_b_mod, v_g_ffn1, v_w1_gu, v_w1_down, v_g_mix, v_w_in, v_dw_weight, v_dw_bias, v_conv_ln_g, v_conv_ln_b, v_w_conv_out, v_w_alpha_f, v_b_alpha_f, v_w_alpha_b, v_b_alpha_b, v_gla_norm_g, v_w_gla_out, v_w_out, v_g_ffn2, v_w2_gu, v_w2_down, v_g_final):
    given = dict(locals())
    w = {n: given[n] for n in WEIGHTS}
    m = {n: given["m_" + n] for n in WEIGHTS}
    v = {n: given["v_" + n] for n in WEIGHTS}

    def shard_of(n):
        if n == "small":
            return _small_pack(w["dw_weight"][0], w["w_alpha_f"][0], w["w_alpha_b"][0])
        return w[n][0].astype(BF16)

    def install(wts, got):
        for n in ("w1_gu", "w2_gu"):
            if n in got:
                wts[n] = got[n]
        for n in ("w1_down", "w2_down", "w_conv_out", "w_gla_out", "w_out"):
            if n in got:
                wts[n] = got[n].reshape(-1, D)
        if "w_in" in got:
            wts["w_in_p"] = _perm_in_cols(got["w_in"])
        if "small" in got:
            sm = got["small"]
            wts["dw_weight"] = jnp.concatenate([sm[j, :CONV_W] for j in range(4)], axis=1)
            zpad = jnp.zeros((128, HEADS * DK), BF16)
            w_af = jnp.concatenate([sm[j, 32:32 + LOWRANK, :DK] for j in range(4)], axis=1)
            w_ab = jnp.concatenate([sm[j, 32:32 + LOWRANK, DK:] for j in range(4)], axis=1)
            wts["w_alpha_f_pad"] = zpad.at[0:LOWRANK].set(w_af.astype(BF16))
            wts["w_alpha_b_pad"] = zpad.at[LOWRANK:2 * LOWRANK].set(w_ab.astype(BF16))

    wts = {n: w[n] for n in REPLICATED}
    chip = 2 * lax.axis_index("x") + lax.axis_index("y")
    mod_cols = w["w_mod"].shape[2]
    wts["w_mod_shard"] = shard_of("w_mod")
    wts["b_mod_shard"] = lax.dynamic_slice(w["b_mod"], (0, chip * mod_cols), (1, mod_cols))

    place = jnp.stack([lax.axis_index("c"), 2 * lax.axis_index("x") + lax.axis_index("y")]).astype(jnp.int32)
    comm = _Overlap(shard_of, lambda got: install(wts, got), place)
    loss, grad_x, grads = _local_step(x, c, ctx, loss_target, wts, comm)
    loss = lax.psum(loss, ("x", "y", "c"))

    tags = MATRICES + ("small",)
    rest = tuple(n for n in tags if n not in comm.landed)
    if rest:
        rest_sums = _chip_sums("rest", rest, grads, place)
        for n, cs, r in zip(rest, rest_sums, _scatter_chips(rest_sums)):
            comm.landed[n] = (cs, r)
    halves = [_sum_chips("grad_sum_" + t, *comm.landed[t], place) for t in tags]
    reduced = dict(zip(tags, _join_halves(halves)))
    g_shard = {n: reduced[n] for n in MATRICES}
    g_shard["w_mod"] = grads["w_mod"]
    g_shard["dw_weight"] = reduced["small"][:CONV_W]
    g_shard["w_alpha_f"] = reduced["small"][32:32 + LOWRANK, :DK]
    g_shard["w_alpha_b"] = reduced["small"][32:32 + LOWRANK, DK:]

    rep_shapes = [w[n].shape for n in REPLICATED]
    small = _allreduce_small(_pack_flat([grads[n].reshape(w[n].shape) for n in REPLICATED], 8 * 1024))

    g_out, d_out, m_out, v_out = {}, {}, {}, {}
    for n, _ in SHARDED:
        s2 = w[n].shape[1:]
        go, d, nm, nv = _adamw("adamw_" + n, w[n].reshape(s2), g_shard[n], m[n].reshape(s2), v[n].reshape(s2))
        g_out[n], d_out[n] = go.reshape(w[n].shape), d.reshape(w[n].shape)
        m_out[n], v_out[n] = nm.reshape(w[n].shape), nv.reshape(w[n].shape)
    pk = lambda t: _pack_flat([t[n] for n in REPLICATED], 8 * 1024)
    go, d, nm, nv = _adamw("adamw_vectors", pk(w), small, pk(m), pk(v))
    for n, gg, dd, mm, vv in zip(REPLICATED, _unpack_flat(go, rep_shapes), _unpack_flat(d, rep_shapes),
                                 _unpack_flat(nm, rep_shapes), _unpack_flat(nv, rep_shapes)):
        g_out[n], d_out[n], m_out[n], v_out[n] = gg, dd, mm, vv

    return (loss, grad_x, *[g_out[n] for n in WEIGHTS], *[d_out[n] for n in WEIGHTS],
            *[m_out[n] for n in WEIGHTS], *[v_out[n] for n in WEIGHTS])
```
